```python
import jax
import jax.numpy as jnp
from jax import lax
import numpy as np

D_MODEL = 1024
BATCH = 16
SEQ = 2048
DEPTH = 4

GRID_W = 64
CTX_LEN = 256
N_MIXERS = 4
MIX_CONV = 0
MIX_POOL = 1
MIX_MLA = 2
MIX_CHUNK = 3
BRANCH = D_MODEL
EPS = 1e-6
CONV_WIDTH = 31
POOL_WINDOWS = (2, 4, 8, 16)
POOL_GROUP = BRANCH // len(POOL_WINDOWS)
MLA_HEADS = D_MODEL // 128
MLA_NOPE = 128
MLA_ROPE = 64
MLA_V = 128
MLA_Q_RANK = 3 * D_MODEL // 8
MLA_KV_RANK = D_MODEL // 4
MLA_KVC = MLA_KV_RANK + MLA_ROPE
MLA_SCALE = (MLA_NOPE + MLA_ROPE) ** -0.5
ROPE_THETA = 10000.0
Q_BLOCK = 128
CHUNK = 128
CHUNK_GROUPS = 8
CHUNK_GC = BRANCH // CHUNK_GROUPS

kernel_name = 'hybrid_interleaved_diffusion_block'


def _n_layers_of(kind):
    return len(range(kind, DEPTH, N_MIXERS))


def _rms(x, g):
    xf = x.astype(jnp.float32)
    y = xf * lax.rsqrt(jnp.mean(xf * xf, axis=-1, keepdims=True) + EPS)
    return (y * g.astype(jnp.float32)).astype(x.dtype)


def _layernorm(x, g, b):
    xf = x.astype(jnp.float32)
    mu = jnp.mean(xf, axis=-1, keepdims=True)
    var = jnp.mean(jnp.square(xf - mu), axis=-1, keepdims=True)
    y = (xf - mu) * lax.rsqrt(var + EPS)
    return (y * g.astype(jnp.float32) + b.astype(jnp.float32)).astype(x.dtype)


def _rope_tables(rows):
    row_id = jnp.repeat(jnp.arange(rows), GRID_W).astype(jnp.float32)
    col_id = jnp.tile(jnp.arange(GRID_W), rows).astype(jnp.float32)
    axis_dim = MLA_ROPE // 2
    freqs = ROPE_THETA ** (-jnp.arange(0, axis_dim, 2, dtype=jnp.float32) / axis_dim)
    ar = row_id[:, None] * freqs
    ac = col_id[:, None] * freqs
    return (jnp.cos(ar), jnp.sin(ar), jnp.cos(ac), jnp.sin(ac))


def _rope1d(x, cos, sin):
    x1, x2 = jnp.split(x, 2, axis=-1)
    return jnp.concatenate([x1 * cos - x2 * sin, x1 * sin + x2 * cos], axis=-1)


def _rope2d(x, tabs):
    cr, sr, cc, sc = [t.astype(x.dtype) for t in tabs]
    if x.ndim == 4:
        cr, sr, cc, sc = cr[:, None], sr[:, None], cc[:, None], sc[:, None]
    xr, xc = jnp.split(x, 2, axis=-1)
    return jnp.concatenate([_rope1d(xr, cr, sr), _rope1d(xc, cc, sc)], axis=-1)


def _conv_mixer(h, w_in, dw, db, ln_g, ln_b, w_out):
    a, b, g = jnp.split(h @ w_in, 3, axis=-1)
    y = a * jax.nn.sigmoid(b)
    y = lax.conv_general_dilated(
        y, dw[:, None, :].astype(y.dtype), window_strides=(1,),
        padding=[(CONV_WIDTH // 2, CONV_WIDTH // 2)],
        dimension_numbers=('NWC', 'WIO', 'NWC'),
        feature_group_count=BRANCH) + db
    y = jax.nn.silu(_layernorm(y, ln_g, ln_b)) * jax.nn.silu(g)
    return y @ w_out


def _window_mean(v, w):
    L = v.shape[1]
    cs = jnp.pad(jnp.cumsum(v.astype(jnp.float32), axis=1), ((0, 0), (1, 0), (0, 0)))
    t = jnp.arange(L)
    start = jnp.clip(t - w // 2, 0, L)
    end = jnp.clip(t + (w - w // 2), 0, L)
    total = jnp.take(cs, end, axis=1) - jnp.take(cs, start, axis=1)
    cnt = (end - start).astype(jnp.float32)
    return (total / cnt[None, :, None]).astype(v.dtype)


def _pool_mixer(h, w_in, w_grp, scale, w_out):
    v, g = jnp.split(h @ w_in, 2, axis=-1)
    B, L, _ = v.shape
    vg = v.reshape(B, L, len(POOL_WINDOWS), POOL_GROUP)
    pooled = jnp.stack([_window_mean(vg[:, :, k], w) for k, w in enumerate(POOL_WINDOWS)], axis=2) - vg
    y = jnp.einsum('blgc,gcd->blgd', pooled, w_grp).reshape(B, L, BRANCH) * scale
    return (y * jax.nn.silu(g)) @ w_out


def _mla_keys(pkv, kv_norm, w_ukv, k_nope_g, k_rope_g, tabs):
    ckv, kr = jnp.split(pkv, [MLA_KV_RANK], axis=-1)
    B, L, _ = ckv.shape
    kv = (_rms(ckv, kv_norm) @ w_ukv).reshape(B, L, MLA_HEADS, MLA_NOPE + MLA_V)
    kn, v = jnp.split(kv, [MLA_NOPE], axis=-1)
    kn = _rms(kn, k_nope_g)
    kr = _rms(kr, k_rope_g)
    if tabs is not None:
        kr = _rope2d(kr, tabs)
    return kn, kr, v


def _mla_queries(cq, q_norm, w_uq, q_nope_g, q_rope_g, tabs):
    B, L, _ = cq.shape
    q = (_rms(cq, q_norm) @ w_uq).reshape(B, L, MLA_HEADS, MLA_NOPE + MLA_ROPE)
    qn, qr = jnp.split(q, [MLA_NOPE], axis=-1)
    qn = _rms(qn, q_nope_g)
    qr = _rms(qr, q_rope_g)
    if tabs is not None:
        qr = _rope2d(qr, tabs)
    return qn, qr


def _attend(qn, qr, kn, kr, v):
    s = (jnp.einsum('bqhd,bkhd->bhqk', qn, kn)
         + jnp.einsum('bqhr,bkr->bhqk', qr, kr)).astype(jnp.float32) * MLA_SCALE
    p = jax.nn.softmax(s, axis=-1).astype(v.dtype)
    return jnp.einsum('bhqk,bkhd->bqhd', p, v)


def _attend_blocks(qn, qr, kn, kr, v):
    B, L, H, _ = qn.shape
    nb = L // Q_BLOCK

    def blk(t):
        return jnp.moveaxis(t.reshape((B, nb, Q_BLOCK) + t.shape[2:]), 1, 0)

    out = lax.map(lambda q: _attend(q[0], q[1], kn, kr, v), (blk(qn), blk(qr)))
    return jnp.moveaxis(out, 0, 1).reshape(B, L, H * MLA_V)


def _mla_mixer(h, hc, with_ctx_out, tabs, w_in, q_norm, kv_norm, w_uq, w_ukv, nope_g, rope_g, w_out):
    pkv, cq, g = jnp.split(h @ w_in, [MLA_KVC, MLA_KVC + MLA_Q_RANK], axis=-1)
    kn, kr, v = _mla_keys(pkv, kv_norm, w_ukv, nope_g[1], rope_g[1], tabs)
    qn, qr = _mla_queries(cq, q_norm, w_uq, nope_g[0], rope_g[0], tabs)
    pc = hc @ (w_in if with_ctx_out else w_in[:, :MLA_KVC])
    knc, krc, vc = _mla_keys(pc[..., :MLA_KVC], kv_norm, w_ukv, nope_g[1], rope_g[1], None)
    o = _attend_blocks(qn, qr,
                       jnp.concatenate([kn, knc], axis=1),
                       jnp.concatenate([kr, krc], axis=1),
                       jnp.concatenate([v, vc], axis=1))
    out = (o * jax.nn.silu(g)) @ w_out
    out_c = None
    if with_ctx_out:
        cqc, gc = jnp.split(pc[..., MLA_KVC:], [MLA_Q_RANK], axis=-1)
        qnc, qrc = _mla_queries(cqc, q_norm, w_uq, nope_g[0], rope_g[0], None)
        Bc, Lc = hc.shape[0], hc.shape[1]
        oc = _attend(qnc, qrc, knc, krc, vc).reshape(Bc, Lc, MLA_HEADS * MLA_V)
        out_c = (oc * jax.nn.silu(gc)) @ w_out
    return out, out_c


def _chunk_mixer(h, w_in, ln_g, ln_b, w_s, b_s, w_out):
    u, v, g = jnp.split(h @ w_in, 3, axis=-1)
    B, L, _ = v.shape
    v = _layernorm(v, ln_g, ln_b).reshape(B, L // CHUNK, CHUNK, CHUNK_GROUPS, CHUNK_GC)
    s = jnp.einsum('gpq,bnqgc->bnpgc', w_s, v) + b_s[:, :, None]
    y = u * s.reshape(B, L, BRANCH) * jax.nn.silu(g)
    return y @ w_out


def _fwd_setup_inputs(seed: int = 0) -> dict:
    key = jax.random.key(seed)
    ks = iter(jax.random.split(key, 40))

    def nrm(shape, s):
        return jax.random.normal(next(ks), shape, jnp.float32) * s

    nA, nB, nC, nD = (_n_layers_of(k) for k in range(N_MIXERS))
    D, E = D_MODEL, BRANCH
    HQK = MLA_HEADS * (MLA_NOPE + MLA_ROPE)
    HKV = MLA_HEADS * (MLA_NOPE + MLA_V)
    HV = MLA_HEADS * MLA_V
    return {
        'x': nrm((BATCH, SEQ, D), 1.0),
        'c': nrm((BATCH, D), 1.0),
        'ctx': nrm((BATCH, CTX_LEN, D), 1.0),
        'c_ctx': nrm((D,), 1.0),
        'norm_g': 1.0 + nrm((DEPTH, D), 0.05),
        'w_mod': nrm((DEPTH, D, 3 * D), 0.5 * D ** -0.5),
        'b_mod': nrm((DEPTH, 3 * D), 0.01),
        'cv_w_in': nrm((nA, D, 3 * E), D ** -0.5),
        'cv_dw': nrm((nA, CONV_WIDTH, E), CONV_WIDTH ** -0.5),
        'cv_db': nrm((nA, E), 0.01),
        'cv_ln_g': 1.0 + nrm((nA, E), 0.05),
        'cv_ln_b': nrm((nA, E), 0.01),
        'cv_w_out': nrm((nA, E, D), E ** -0.5),
        'pl_w_in': nrm((nB, D, 2 * E), D ** -0.5),
        'pl_w_grp': nrm((nB, len(POOL_WINDOWS), POOL_GROUP, POOL_GROUP), POOL_GROUP ** -0.5),
        'pl_scale': 1.0 + nrm((nB, E), 0.05),
        'pl_w_out': nrm((nB, E, D), E ** -0.5),
        'ml_w_in': nrm((nC, D, MLA_KVC + MLA_Q_RANK + HV), D ** -0.5),
        'ml_q_norm': 1.0 + nrm((nC, MLA_Q_RANK), 0.05),
        'ml_kv_norm': 1.0 + nrm((nC, MLA_KV_RANK), 0.05),
        'ml_w_uq': nrm((nC, MLA_Q_RANK, HQK), MLA_Q_RANK ** -0.5),
        'ml_w_ukv': nrm((nC, MLA_KV_RANK, HKV), MLA_KV_RANK ** -0.5),
        'ml_nope_norm': 1.0 + nrm((nC, 2, MLA_NOPE), 0.05),
        'ml_rope_norm': 1.0 + nrm((nC, 2, MLA_ROPE), 0.05),
        'ml_w_out': nrm((nC, HV, D), HV ** -0.5),
        'ch_w_in': nrm((nD, D, 3 * E), D ** -0.5),
        'ch_ln_g': 1.0 + nrm((nD, E), 0.05),
        'ch_ln_b': nrm((nD, E), 0.01),
        'ch_w_s': nrm((nD, CHUNK_GROUPS, CHUNK, CHUNK), CHUNK ** -0.5),
        'ch_b_s': 1.0 + nrm((nD, CHUNK, CHUNK_GROUPS), 0.05),
        'ch_w_out': nrm((nD, E, D), E ** -0.5),
    }


def _fwd_reference(x, c, ctx, c_ctx, norm_g, w_mod, b_mod,
              cv_w_in, cv_dw, cv_db, cv_ln_g, cv_ln_b, cv_w_out,
              pl_w_in, pl_w_grp, pl_scale, pl_w_out,
              ml_w_in, ml_q_norm, ml_kv_norm, ml_w_uq, ml_w_ukv, ml_nope_norm, ml_rope_norm, ml_w_out,
              ch_w_in, ch_ln_g, ch_ln_b, ch_w_s, ch_b_s, ch_w_out):
    L = x.shape[1]
    ROWS = L // GRID_W
    tabs = _rope_tables(ROWS)
    s_lat = jax.nn.silu(c)
    s_ctx = jax.nn.silu(c_ctx)
    cx = ctx
    for i in range(DEPTH):
        kind, j = i % N_MIXERS, i // N_MIXERS
        ctx_out = any(k % N_MIXERS == MIX_MLA for k in range(i + 1, DEPTH))
        ctx_in = ctx_out or kind == MIX_MLA
        sh, sc, gt = jnp.split((s_lat @ w_mod[i] + b_mod[i])[:, None, :], 3, axis=-1)
        h = _rms(x, norm_g[i]) * (1.0 + sc) + sh
        if ctx_in:
            shc, scc, gtc = jnp.split(s_ctx @ w_mod[i] + b_mod[i], 3, axis=-1)
            hc = _rms(cx, norm_g[i]) * (1.0 + scc) + shc
        if kind == MIX_CONV:
            args = (cv_w_in[j], cv_dw[j], cv_db[j], cv_ln_g[j], cv_ln_b[j], cv_w_out[j])
            o = _conv_mixer(h, *args)
            oc = _conv_mixer(hc, *args) if ctx_out else None
        elif kind == MIX_POOL:
            args = (pl_w_in[j], pl_w_grp[j], pl_scale[j], pl_w_out[j])
            o = _pool_mixer(h, *args)
            oc = _pool_mixer(hc, *args) if ctx_out else None
        elif kind == MIX_MLA:
            o, oc = _mla_mixer(h, hc, ctx_out, tabs, ml_w_in[j], ml_q_norm[j], ml_kv_norm[j],
                               ml_w_uq[j], ml_w_ukv[j], ml_nope_norm[j], ml_rope_norm[j], ml_w_out[j])
        else:
            args = (ch_w_in[j], ch_ln_g[j], ch_ln_b[j], ch_w_s[j], ch_b_s[j], ch_w_out[j])
            o = _chunk_mixer(h, *args)
            oc = _chunk_mixer(hc, *args) if ctx_out else None
        x = x + gt * o
        if ctx_out:
            cx = cx + gtc * oc
    return x


import jax as _jax
import jax.numpy as _jnp

TWIN_FORMAT = 'train_step'
FWD_PARAMS = ['x', 'c', 'ctx', 'c_ctx', 'norm_g', 'w_mod', 'b_mod', 'cv_w_in', 'cv_dw', 'cv_db', 'cv_ln_g', 'cv_ln_b', 'cv_w_out', 'pl_w_in', 'pl_w_grp', 'pl_scale', 'pl_w_out', 'ml_w_in', 'ml_q_norm', 'ml_kv_norm', 'ml_w_uq', 'ml_w_ukv', 'ml_nope_norm', 'ml_rope_norm', 'ml_w_out', 'ch_w_in', 'ch_ln_g', 'ch_ln_b', 'ch_w_s', 'ch_b_s', 'ch_w_out']
TWIN_WEIGHTS = ['c_ctx', 'norm_g', 'w_mod', 'b_mod', 'cv_w_in', 'cv_dw', 'cv_db', 'cv_ln_g', 'cv_ln_b', 'cv_w_out', 'pl_w_in', 'pl_w_grp', 'pl_scale', 'pl_w_out', 'ml_w_in', 'ml_q_norm', 'ml_kv_norm', 'ml_w_uq', 'ml_w_ukv', 'ml_nope_norm', 'ml_rope_norm', 'ml_w_out', 'ch_w_in', 'ch_ln_g', 'ch_ln_b', 'ch_w_s', 'ch_b_s', 'ch_w_out']
TWIN_DIFF_INPUT = 'x'
TWIN_INPUTS = ['x', 'c', 'ctx', 'c_ctx', 'norm_g', 'w_mod', 'b_mod', 'cv_w_in', 'cv_dw', 'cv_db', 'cv_ln_g', 'cv_ln_b', 'cv_w_out', 'pl_w_in', 'pl_w_grp', 'pl_scale', 'pl_w_out', 'ml_w_in', 'ml_q_norm', 'ml_kv_norm', 'ml_w_uq', 'ml_w_ukv', 'ml_nope_norm', 'ml_rope_norm', 'ml_w_out', 'ch_w_in', 'ch_ln_g', 'ch_ln_b', 'ch_w_s', 'ch_b_s', 'ch_w_out', 'loss_target', 'm_c_ctx', 'm_norm_g', 'm_w_mod', 'm_b_mod', 'm_cv_w_in', 'm_cv_dw', 'm_cv_db', 'm_cv_ln_g', 'm_cv_ln_b', 'm_cv_w_out', 'm_pl_w_in', 'm_pl_w_grp', 'm_pl_scale', 'm_pl_w_out', 'm_ml_w_in', 'm_ml_q_norm', 'm_ml_kv_norm', 'm_ml_w_uq', 'm_ml_w_ukv', 'm_ml_nope_norm', 'm_ml_rope_norm', 'm_ml_w_out', 'm_ch_w_in', 'm_ch_ln_g', 'm_ch_ln_b', 'm_ch_w_s', 'm_ch_b_s', 'm_ch_w_out', 'v_c_ctx', 'v_norm_g', 'v_w_mod', 'v_b_mod', 'v_cv_w_in', 'v_cv_dw', 'v_cv_db', 'v_cv_ln_g', 'v_cv_ln_b', 'v_cv_w_out', 'v_pl_w_in', 'v_pl_w_grp', 'v_pl_scale', 'v_pl_w_out', 'v_ml_w_in', 'v_ml_q_norm', 'v_ml_kv_norm', 'v_ml_w_uq', 'v_ml_w_ukv', 'v_ml_nope_norm', 'v_ml_rope_norm', 'v_ml_w_out', 'v_ch_w_in', 'v_ch_ln_g', 'v_ch_ln_b', 'v_ch_w_s', 'v_ch_b_s', 'v_ch_w_out']
TWIN_OUTPUTS = ['loss', 'grad_x', 'grad_c_ctx', 'grad_norm_g', 'grad_w_mod', 'grad_b_mod', 'grad_cv_w_in', 'grad_cv_dw', 'grad_cv_db', 'grad_cv_ln_g', 'grad_cv_ln_b', 'grad_cv_w_out', 'grad_pl_w_in', 'grad_pl_w_grp', 'grad_pl_scale', 'grad_pl_w_out', 'grad_ml_w_in', 'grad_ml_q_norm', 'grad_ml_kv_norm', 'grad_ml_w_uq', 'grad_ml_w_ukv', 'grad_ml_nope_norm', 'grad_ml_rope_norm', 'grad_ml_w_out', 'grad_ch_w_in', 'grad_ch_ln_g', 'grad_ch_ln_b', 'grad_ch_w_s', 'grad_ch_b_s', 'grad_ch_w_out', 'delta_c_ctx', 'delta_norm_g', 'delta_w_mod', 'delta_b_mod', 'delta_cv_w_in', 'delta_cv_dw', 'delta_cv_db', 'delta_cv_ln_g', 'delta_cv_ln_b', 'delta_cv_w_out', 'delta_pl_w_in', 'delta_pl_w_grp', 'delta_pl_scale', 'delta_pl_w_out', 'delta_ml_w_in', 'delta_ml_q_norm', 'delta_ml_kv_norm', 'delta_ml_w_uq', 'delta_ml_w_ukv', 'delta_ml_nope_norm', 'delta_ml_rope_norm', 'delta_ml_w_out', 'delta_ch_w_in', 'delta_ch_ln_g', 'delta_ch_ln_b', 'delta_ch_w_s', 'delta_ch_b_s', 'delta_ch_w_out', 'new_m_c_ctx', 'new_m_norm_g', 'new_m_w_mod', 'new_m_b_mod', 'new_m_cv_w_in', 'new_m_cv_dw', 'new_m_cv_db', 'new_m_cv_ln_g', 'new_m_cv_ln_b', 'new_m_cv_w_out', 'new_m_pl_w_in', 'new_m_pl_w_grp', 'new_m_pl_scale', 'new_m_pl_w_out', 'new_m_ml_w_in', 'new_m_ml_q_norm', 'new_m_ml_kv_norm', 'new_m_ml_w_uq', 'new_m_ml_w_ukv', 'new_m_ml_nope_norm', 'new_m_ml_rope_norm', 'new_m_ml_w_out', 'new_m_ch_w_in', 'new_m_ch_ln_g', 'new_m_ch_ln_b', 'new_m_ch_w_s', 'new_m_ch_b_s', 'new_m_ch_w_out', 'new_v_c_ctx', 'new_v_norm_g', 'new_v_w_mod', 'new_v_b_mod', 'new_v_cv_w_in', 'new_v_cv_dw', 'new_v_cv_db', 'new_v_cv_ln_g', 'new_v_cv_ln_b', 'new_v_cv_w_out', 'new_v_pl_w_in', 'new_v_pl_w_grp', 'new_v_pl_scale', 'new_v_pl_w_out', 'new_v_ml_w_in', 'new_v_ml_q_norm', 'new_v_ml_kv_norm', 'new_v_ml_w_uq', 'new_v_ml_w_ukv', 'new_v_ml_nope_norm', 'new_v_ml_rope_norm', 'new_v_ml_w_out', 'new_v_ch_w_in', 'new_v_ch_ln_g', 'new_v_ch_ln_b', 'new_v_ch_w_s', 'new_v_ch_b_s', 'new_v_ch_w_out']
TWIN_LEAF_KINDS = {'loss': 'loss', 'grad_x': 'grad_x', 'grad_c_ctx': 'grad_w', 'grad_norm_g': 'grad_w', 'grad_w_mod': 'grad_w', 'grad_b_mod': 'grad_w', 'grad_cv_w_in': 'grad_w', 'grad_cv_dw': 'grad_w', 'grad_cv_db': 'grad_w', 'grad_cv_ln_g': 'grad_w', 'grad_cv_ln_b': 'grad_w', 'grad_cv_w_out': 'grad_w', 'grad_pl_w_in': 'grad_w', 'grad_pl_w_grp': 'grad_w', 'grad_pl_scale': 'grad_w', 'grad_pl_w_out': 'grad_w', 'grad_ml_w_in': 'grad_w', 'grad_ml_q_norm': 'grad_w', 'grad_ml_kv_norm': 'grad_w', 'grad_ml_w_uq': 'grad_w', 'grad_ml_w_ukv': 'grad_w', 'grad_ml_nope_norm': 'grad_w', 'grad_ml_rope_norm': 'grad_w', 'grad_ml_w_out': 'grad_w', 'grad_ch_w_in': 'grad_w', 'grad_ch_ln_g': 'grad_w', 'grad_ch_ln_b': 'grad_w', 'grad_ch_w_s': 'grad_w', 'grad_ch_b_s': 'grad_w', 'grad_ch_w_out': 'grad_w', 'delta_c_ctx': 'delta_w', 'delta_norm_g': 'delta_w', 'delta_w_mod': 'delta_w', 'delta_b_mod': 'delta_w', 'delta_cv_w_in': 'delta_w', 'delta_cv_dw': 'delta_w', 'delta_cv_db': 'delta_w', 'delta_cv_ln_g': 'delta_w', 'delta_cv_ln_b': 'delta_w', 'delta_cv_w_out': 'delta_w', 'delta_pl_w_in': 'delta_w', 'delta_pl_w_grp': 'delta_w', 'delta_pl_scale': 'delta_w', 'delta_pl_w_out': 'delta_w', 'delta_ml_w_in': 'delta_w', 'delta_ml_q_norm': 'delta_w', 'delta_ml_kv_norm': 'delta_w', 'delta_ml_w_uq': 'delta_w', 'delta_ml_w_ukv': 'delta_w', 'delta_ml_nope_norm': 'delta_w', 'delta_ml_rope_norm': 'delta_w', 'delta_ml_w_out': 'delta_w', 'delta_ch_w_in': 'delta_w', 'delta_ch_ln_g': 'delta_w', 'delta_ch_ln_b': 'delta_w', 'delta_ch_w_s': 'delta_w', 'delta_ch_b_s': 'delta_w', 'delta_ch_w_out': 'delta_w', 'new_m_c_ctx': 'new_m', 'new_m_norm_g': 'new_m', 'new_m_w_mod': 'new_m', 'new_m_b_mod': 'new_m', 'new_m_cv_w_in': 'new_m', 'new_m_cv_dw': 'new_m', 'new_m_cv_db': 'new_m', 'new_m_cv_ln_g': 'new_m', 'new_m_cv_ln_b': 'new_m', 'new_m_cv_w_out': 'new_m', 'new_m_pl_w_in': 'new_m', 'new_m_pl_w_grp': 'new_m', 'new_m_pl_scale': 'new_m', 'new_m_pl_w_out': 'new_m', 'new_m_ml_w_in': 'new_m', 'new_m_ml_q_norm': 'new_m', 'new_m_ml_kv_norm': 'new_m', 'new_m_ml_w_uq': 'new_m', 'new_m_ml_w_ukv': 'new_m', 'new_m_ml_nope_norm': 'new_m', 'new_m_ml_rope_norm': 'new_m', 'new_m_ml_w_out': 'new_m', 'new_m_ch_w_in': 'new_m', 'new_m_ch_ln_g': 'new_m', 'new_m_ch_ln_b': 'new_m', 'new_m_ch_w_s': 'new_m', 'new_m_ch_b_s': 'new_m', 'new_m_ch_w_out': 'new_m', 'new_v_c_ctx': 'new_v', 'new_v_norm_g': 'new_v', 'new_v_w_mod': 'new_v', 'new_v_b_mod': 'new_v', 'new_v_cv_w_in': 'new_v', 'new_v_cv_dw': 'new_v', 'new_v_cv_db': 'new_v', 'new_v_cv_ln_g': 'new_v', 'new_v_cv_ln_b': 'new_v', 'new_v_cv_w_out': 'new_v', 'new_v_pl_w_in': 'new_v', 'new_v_pl_w_grp': 'new_v', 'new_v_pl_scale': 'new_v', 'new_v_pl_w_out': 'new_v', 'new_v_ml_w_in': 'new_v', 'new_v_ml_q_norm': 'new_v', 'new_v_ml_kv_norm': 'new_v', 'new_v_ml_w_uq': 'new_v', 'new_v_ml_w_ukv': 'new_v', 'new_v_ml_nope_norm': 'new_v', 'new_v_ml_rope_norm': 'new_v', 'new_v_ml_w_out': 'new_v', 'new_v_ch_w_in': 'new_v', 'new_v_ch_ln_g': 'new_v', 'new_v_ch_ln_b': 'new_v', 'new_v_ch_w_s': 'new_v', 'new_v_ch_b_s': 'new_v', 'new_v_ch_w_out': 'new_v'}


def _forward(args):
    return _fwd_reference(*[args[k] for k in FWD_PARAMS])


def _output_shape():
    out = _jax.eval_shape(lambda: _forward(_fwd_setup_inputs(0)))
    return out.shape, out.dtype

N_MICROBATCH = 1
ADAM_LR = 0.001
ADAM_B1 = 0.9
ADAM_B2 = 0.999
ADAM_EPS = 1e-08
ADAM_WD = 0.01
ADAM_STEP = 10
PER_EXAMPLE_BATCH_AXIS = {'x': 0, 'c': 0, 'ctx': 0, 'loss_target': 0}
SHARED_INPUTS = []
_WEIGHT_DTYPES = {'c_ctx': _jnp.float32, 'norm_g': _jnp.float32, 'w_mod': _jnp.float32, 'b_mod': _jnp.float32, 'cv_w_in': _jnp.float32, 'cv_dw': _jnp.float32, 'cv_db': _jnp.float32, 'cv_ln_g': _jnp.float32, 'cv_ln_b': _jnp.float32, 'cv_w_out': _jnp.float32, 'pl_w_in': _jnp.float32, 'pl_w_grp': _jnp.float32, 'pl_scale': _jnp.float32, 'pl_w_out': _jnp.float32, 'ml_w_in': _jnp.float32, 'ml_q_norm': _jnp.float32, 'ml_kv_norm': _jnp.float32, 'ml_w_uq': _jnp.float32, 'ml_w_ukv': _jnp.float32, 'ml_nope_norm': _jnp.float32, 'ml_rope_norm': _jnp.float32, 'ml_w_out': _jnp.float32, 'ch_w_in': _jnp.float32, 'ch_ln_g': _jnp.float32, 'ch_ln_b': _jnp.float32, 'ch_w_s': _jnp.float32, 'ch_b_s': _jnp.float32, 'ch_w_out': _jnp.float32}
MOMENT_SCALE = {'c_ctx': 2.005158e-02, 'norm_g': 3.424085e+00, 'w_mod': 1.157648e+00, 'b_mod': 2.538883e+00, 'cv_w_in': 5.037735e-02, 'cv_dw': 5.591226e-02, 'cv_db': 2.289034e-01, 'cv_ln_g': 6.026191e-01, 'cv_ln_b': 3.762426e-01, 'cv_w_out': 4.966669e-02, 'pl_w_in': 9.994706e-02, 'pl_w_grp': 8.264902e-02, 'pl_scale': 1.198260e+00, 'pl_w_out': 6.972715e-02, 'ml_w_in': 7.222479e-02, 'ml_q_norm': 1.013427e-02, 'ml_kv_norm': 3.677277e-01, 'ml_w_uq': 4.920366e-03, 'ml_w_ukv': 4.042466e-02, 'ml_nope_norm': 2.076125e-02, 'ml_rope_norm': 1.378238e-02, 'ml_w_out': 3.997767e-02, 'ch_w_in': 2.137464e-01, 'ch_ln_g': 1.511129e+00, 'ch_ln_b': 2.024471e-01, 'ch_w_s': 1.340699e-01, 'ch_b_s': 1.484911e+00, 'ch_w_out': 1.161411e-01}


def _to_microbatches(a, axis):
    t = _jnp.moveaxis(a, axis, 0)
    t = t.reshape((N_MICROBATCH, t.shape[0] // N_MICROBATCH) + t.shape[1:])
    return _jnp.moveaxis(t, 1, axis + 1)


def setup_inputs(seed: int = 0) -> dict:
    inp = _fwd_setup_inputs(seed)
    key = _jax.random.fold_in(_jax.random.key(seed), 7919)
    shape, _ = _output_shape()
    out = dict(inp)
    out["loss_target"] = _jax.random.normal(_jax.random.fold_in(key, 0), shape, _jnp.float32)
    for i, name in enumerate(TWIN_WEIGHTS):
        w = inp[name].astype(_jnp.float32)
        if MOMENT_SCALE is None:
            s = _jnp.sqrt(_jnp.mean(_jnp.square(w)) + 1e-30)
        else:
            s = MOMENT_SCALE[name]
        km, kv = _jax.random.split(_jax.random.fold_in(key, i + 1))
        out[name] = w
        out["m_" + name] = s * _jax.random.normal(km, w.shape, _jnp.float32)
        out["v_" + name] = (s * s) * _jax.random.uniform(kv, w.shape, _jnp.float32, 0.5, 1.5)
    if N_MICROBATCH > 1:
        for name, axis in PER_EXAMPLE_BATCH_AXIS.items():
            out[name] = _to_microbatches(out[name], axis)
    return {'x': out['x'], 'c': out['c'], 'ctx': out['ctx'], 'c_ctx': out['c_ctx'], 'norm_g': out['norm_g'], 'w_mod': out['w_mod'], 'b_mod': out['b_mod'], 'cv_w_in': out['cv_w_in'], 'cv_dw': out['cv_dw'], 'cv_db': out['cv_db'], 'cv_ln_g': out['cv_ln_g'], 'cv_ln_b': out['cv_ln_b'], 'cv_w_out': out['cv_w_out'], 'pl_w_in': out['pl_w_in'], 'pl_w_grp': out['pl_w_grp'], 'pl_scale': out['pl_scale'], 'pl_w_out': out['pl_w_out'], 'ml_w_in': out['ml_w_in'], 'ml_q_norm': out['ml_q_norm'], 'ml_kv_norm': out['ml_kv_norm'], 'ml_w_uq': out['ml_w_uq'], 'ml_w_ukv': out['ml_w_ukv'], 'ml_nope_norm': out['ml_nope_norm'], 'ml_rope_norm': out['ml_rope_norm'], 'ml_w_out': out['ml_w_out'], 'ch_w_in': out['ch_w_in'], 'ch_ln_g': out['ch_ln_g'], 'ch_ln_b': out['ch_ln_b'], 'ch_w_s': out['ch_w_s'], 'ch_b_s': out['ch_b_s'], 'ch_w_out': out['ch_w_out'], 'loss_target': out['loss_target'], 'm_c_ctx': out['m_c_ctx'], 'm_norm_g': out['m_norm_g'], 'm_w_mod': out['m_w_mod'], 'm_b_mod': out['m_b_mod'], 'm_cv_w_in': out['m_cv_w_in'], 'm_cv_dw': out['m_cv_dw'], 'm_cv_db': out['m_cv_db'], 'm_cv_ln_g': out['m_cv_ln_g'], 'm_cv_ln_b': out['m_cv_ln_b'], 'm_cv_w_out': out['m_cv_w_out'], 'm_pl_w_in': out['m_pl_w_in'], 'm_pl_w_grp': out['m_pl_w_grp'], 'm_pl_scale': out['m_pl_scale'], 'm_pl_w_out': out['m_pl_w_out'], 'm_ml_w_in': out['m_ml_w_in'], 'm_ml_q_norm': out['m_ml_q_norm'], 'm_ml_kv_norm': out['m_ml_kv_norm'], 'm_ml_w_uq': out['m_ml_w_uq'], 'm_ml_w_ukv': out['m_ml_w_ukv'], 'm_ml_nope_norm': out['m_ml_nope_norm'], 'm_ml_rope_norm': out['m_ml_rope_norm'], 'm_ml_w_out': out['m_ml_w_out'], 'm_ch_w_in': out['m_ch_w_in'], 'm_ch_ln_g': out['m_ch_ln_g'], 'm_ch_ln_b': out['m_ch_ln_b'], 'm_ch_w_s': out['m_ch_w_s'], 'm_ch_b_s': out['m_ch_b_s'], 'm_ch_w_out': out['m_ch_w_out'], 'v_c_ctx': out['v_c_ctx'], 'v_norm_g': out['v_norm_g'], 'v_w_mod': out['v_w_mod'], 'v_b_mod': out['v_b_mod'], 'v_cv_w_in': out['v_cv_w_in'], 'v_cv_dw': out['v_cv_dw'], 'v_cv_db': out['v_cv_db'], 'v_cv_ln_g': out['v_cv_ln_g'], 'v_cv_ln_b': out['v_cv_ln_b'], 'v_cv_w_out': out['v_cv_w_out'], 'v_pl_w_in': out['v_pl_w_in'], 'v_pl_w_grp': out['v_pl_w_grp'], 'v_pl_scale': out['v_pl_scale'], 'v_pl_w_out': out['v_pl_w_out'], 'v_ml_w_in': out['v_ml_w_in'], 'v_ml_q_norm': out['v_ml_q_norm'], 'v_ml_kv_norm': out['v_ml_kv_norm'], 'v_ml_w_uq': out['v_ml_w_uq'], 'v_ml_w_ukv': out['v_ml_w_ukv'], 'v_ml_nope_norm': out['v_ml_nope_norm'], 'v_ml_rope_norm': out['v_ml_rope_norm'], 'v_ml_w_out': out['v_ml_w_out'], 'v_ch_w_in': out['v_ch_w_in'], 'v_ch_ln_g': out['v_ch_ln_g'], 'v_ch_ln_b': out['v_ch_ln_b'], 'v_ch_w_s': out['v_ch_w_s'], 'v_ch_b_s': out['v_ch_b_s'], 'v_ch_w_out': out['v_ch_w_out']}


def _loss(weights, diff, rest, loss_target):
    with _jax.named_scope("forward"):
        args = {**rest, TWIN_DIFF_INPUT: diff, **{k: w.astype(_WEIGHT_DTYPES[k]) for k, w in weights.items()}}
        y = _forward(args)
    with _jax.named_scope("loss_head"):
        err = _jnp.square(y.astype(_jnp.float32) - loss_target)
        return 0.5 * _jnp.sum(_jnp.mean(err, axis=-1)) if err.ndim else 0.5 * err


def _adamw(w, g, m, v):
    m = ADAM_B1 * m + (1.0 - ADAM_B1) * g
    v = ADAM_B2 * v + (1.0 - ADAM_B2) * _jnp.square(g)
    m_hat = m / (1.0 - ADAM_B1 ** ADAM_STEP)
    v_hat = v / (1.0 - ADAM_B2 ** ADAM_STEP)
    delta = -ADAM_LR * (m_hat / (_jnp.sqrt(v_hat) + ADAM_EPS) + ADAM_WD * w)
    return delta, m, v


def reference(x, c, ctx, c_ctx, norm_g, w_mod, b_mod, cv_w_in, cv_dw, cv_db, cv_ln_g, cv_ln_b, cv_w_out, pl_w_in, pl_w_grp, pl_scale, pl_w_out, ml_w_in, ml_q_norm, ml_kv_norm, ml_w_uq, ml_w_ukv, ml_nope_norm, ml_rope_norm, ml_w_out, ch_w_in, ch_ln_g, ch_ln_b, ch_w_s, ch_b_s, ch_w_out, loss_target, m_c_ctx, m_norm_g, m_w_mod, m_b_mod, m_cv_w_in, m_cv_dw, m_cv_db, m_cv_ln_g, m_cv_ln_b, m_cv_w_out, m_pl_w_in, m_pl_w_grp, m_pl_scale, m_pl_w_out, m_ml_w_in, m_ml_q_norm, m_ml_kv_norm, m_ml_w_uq, m_ml_w_ukv, m_ml_nope_norm, m_ml_rope_norm, m_ml_w_out, m_ch_w_in, m_ch_ln_g, m_ch_ln_b, m_ch_w_s, m_ch_b_s, m_ch_w_out, v_c_ctx, v_norm_g, v_w_mod, v_b_mod, v_cv_w_in, v_cv_dw, v_cv_db, v_cv_ln_g, v_cv_ln_b, v_cv_w_out, v_pl_w_in, v_pl_w_grp, v_pl_scale, v_pl_w_out, v_ml_w_in, v_ml_q_norm, v_ml_kv_norm, v_ml_w_uq, v_ml_w_ukv, v_ml_nope_norm, v_ml_rope_norm, v_ml_w_out, v_ch_w_in, v_ch_ln_g, v_ch_ln_b, v_ch_w_s, v_ch_b_s, v_ch_w_out):
    given = dict(x=x, c=c, ctx=ctx, c_ctx=c_ctx, norm_g=norm_g, w_mod=w_mod, b_mod=b_mod, cv_w_in=cv_w_in, cv_dw=cv_dw, cv_db=cv_db, cv_ln_g=cv_ln_g, cv_ln_b=cv_ln_b, cv_w_out=cv_w_out, pl_w_in=pl_w_in, pl_w_grp=pl_w_grp, pl_scale=pl_scale, pl_w_out=pl_w_out, ml_w_in=ml_w_in, ml_q_norm=ml_q_norm, ml_kv_norm=ml_kv_norm, ml_w_uq=ml_w_uq, ml_w_ukv=ml_w_ukv, ml_nope_norm=ml_nope_norm, ml_rope_norm=ml_rope_norm, ml_w_out=ml_w_out, ch_w_in=ch_w_in, ch_ln_g=ch_ln_g, ch_ln_b=ch_ln_b, ch_w_s=ch_w_s, ch_b_s=ch_b_s, ch_w_out=ch_w_out, loss_target=loss_target, m_c_ctx=m_c_ctx, m_norm_g=m_norm_g, m_w_mod=m_w_mod, m_b_mod=m_b_mod, m_cv_w_in=m_cv_w_in, m_cv_dw=m_cv_dw, m_cv_db=m_cv_db, m_cv_ln_g=m_cv_ln_g, m_cv_ln_b=m_cv_ln_b, m_cv_w_out=m_cv_w_out, m_pl_w_in=m_pl_w_in, m_pl_w_grp=m_pl_w_grp, m_pl_scale=m_pl_scale, m_pl_w_out=m_pl_w_out, m_ml_w_in=m_ml_w_in, m_ml_q_norm=m_ml_q_norm, m_ml_kv_norm=m_ml_kv_norm, m_ml_w_uq=m_ml_w_uq, m_ml_w_ukv=m_ml_w_ukv, m_ml_nope_norm=m_ml_nope_norm, m_ml_rope_norm=m_ml_rope_norm, m_ml_w_out=m_ml_w_out, m_ch_w_in=m_ch_w_in, m_ch_ln_g=m_ch_ln_g, m_ch_ln_b=m_ch_ln_b, m_ch_w_s=m_ch_w_s, m_ch_b_s=m_ch_b_s, m_ch_w_out=m_ch_w_out, v_c_ctx=v_c_ctx, v_norm_g=v_norm_g, v_w_mod=v_w_mod, v_b_mod=v_b_mod, v_cv_w_in=v_cv_w_in, v_cv_dw=v_cv_dw, v_cv_db=v_cv_db, v_cv_ln_g=v_cv_ln_g, v_cv_ln_b=v_cv_ln_b, v_cv_w_out=v_cv_w_out, v_pl_w_in=v_pl_w_in, v_pl_w_grp=v_pl_w_grp, v_pl_scale=v_pl_scale, v_pl_w_out=v_pl_w_out, v_ml_w_in=v_ml_w_in, v_ml_q_norm=v_ml_q_norm, v_ml_kv_norm=v_ml_kv_norm, v_ml_w_uq=v_ml_w_uq, v_ml_w_ukv=v_ml_w_ukv, v_ml_nope_norm=v_ml_nope_norm, v_ml_rope_norm=v_ml_rope_norm, v_ml_w_out=v_ml_w_out, v_ch_w_in=v_ch_w_in, v_ch_ln_g=v_ch_ln_g, v_ch_ln_b=v_ch_ln_b, v_ch_w_s=v_ch_w_s, v_ch_b_s=v_ch_b_s, v_ch_w_out=v_ch_w_out)
    weights = {n: given[n] for n in TWIN_WEIGHTS}
    shared = {n: given[n] for n in SHARED_INPUTS}
    per_example = {n: given[n] for n in ['x', 'c', 'ctx']}
    grad_fn = _jax.value_and_grad(_loss, argnums=(0, 1))

    def one_microbatch(ex, loss_target):
        ex = dict(ex)
        diff = ex.pop(TWIN_DIFF_INPUT)
        return grad_fn(weights, diff, {**shared, **ex}, loss_target)

    if N_MICROBATCH == 1:
        loss, (grad_w, grad_x) = one_microbatch(per_example, given["loss_target"])
    else:
        def body(carry, xs):
            loss_sum, grad_sum = carry
            l_k, (gw_k, gx_k) = one_microbatch(xs[0], xs[1])
            with _jax.named_scope("update"):
                return (loss_sum + l_k, _jax.tree.map(_jnp.add, grad_sum, gw_k)), gx_k

        init = (_jnp.zeros((), _jnp.float32), _jax.tree.map(_jnp.zeros_like, weights))
        (loss, grad_w), grad_x = _jax.lax.scan(body, init, (per_example, given["loss_target"]))
    with _jax.named_scope("update"):
        delta_w, new_m, new_v = {}, {}, {}
        for n in TWIN_WEIGHTS:
            delta_w[n], new_m[n], new_v[n] = _adamw(weights[n], grad_w[n], given["m_" + n], given["v_" + n])
    return (loss, grad_x, *[grad_w[n] for n in TWIN_WEIGHTS], *[delta_w[n] for n in TWIN_WEIGHTS],
            *[new_m[n] for n in TWIN_WEIGHTS], *[new_v[n] for n in TWIN_WEIGHTS])
```

```python
import functools

import jax
import jax.numpy as jnp
from jax import lax
from jax.experimental import pallas as pl
from jax.experimental.pallas import tpu as pltpu

F32 = jnp.float32
BF16 = jnp.bfloat16

N_DEV = 8
EPS = 1e-6
CONV_WIDTH = 31
CONV_PAD = 16
POOL_WINDOWS = (2, 4, 8, 16)
POOL_TAPS = 16
MLA_HEADS = 8
MLA_NOPE = 128
MLA_ROPE = 64
MLA_Q_RANK = 384
MLA_KV_RANK = 256
MLA_SCALE = (MLA_NOPE + MLA_ROPE) ** -0.5
ROPE_THETA = 10000.0
GRID_W = 64
HEAD_W = 256
CHUNK = 128
CHUNK_GROUPS = 8
ADAM_LR = 0.001
ADAM_B1 = 0.9
ADAM_B2 = 0.999
ADAM_EPS = 1e-08
ADAM_WD = 0.01
ADAM_STEP = 10
LANES = 128
VMEM_LIMIT = 56 * 1024 * 1024
PACK_COLS = 1024

WEIGHTS = ['c_ctx', 'norm_g', 'w_mod', 'b_mod', 'cv_w_in', 'cv_dw', 'cv_db', 'cv_ln_g', 'cv_ln_b', 'cv_w_out',
           'pl_w_in', 'pl_w_grp', 'pl_scale', 'pl_w_out', 'ml_w_in', 'ml_q_norm', 'ml_kv_norm', 'ml_w_uq',
           'ml_w_ukv', 'ml_nope_norm', 'ml_rope_norm', 'ml_w_out', 'ch_w_in', 'ch_ln_g', 'ch_ln_b', 'ch_w_s',
           'ch_b_s', 'ch_w_out']
SHARD_AXIS = {'c_ctx': None, 'norm_g': None, 'w_mod': 2, 'b_mod': None, 'cv_w_in': 2, 'cv_dw': 2, 'cv_db': None,
              'cv_ln_g': None, 'cv_ln_b': None, 'cv_w_out': 1, 'pl_w_in': 2, 'pl_w_grp': 2, 'pl_scale': 1,
              'pl_w_out': 1, 'ml_w_in': 2, 'ml_q_norm': 1, 'ml_kv_norm': 1, 'ml_w_uq': 2, 'ml_w_ukv': 2,
              'ml_nope_norm': None, 'ml_rope_norm': None, 'ml_w_out': 1, 'ch_w_in': 2, 'ch_ln_g': 1, 'ch_ln_b': 1,
              'ch_w_s': None, 'ch_b_s': None, 'ch_w_out': 1}
MATMUL_WEIGHTS = ['cv_w_in', 'cv_w_out', 'pl_w_in', 'pl_w_grp', 'pl_w_out', 'ml_w_in', 'ml_w_uq', 'ml_w_ukv',
                  'ml_w_out', 'ch_w_in', 'ch_w_out']
VECTOR_WEIGHTS = ['cv_dw', 'pl_scale', 'ml_q_norm', 'ml_kv_norm', 'ch_ln_g', 'ch_ln_b']
EXCHANGED = MATMUL_WEIGHTS[:1] + ['cv_dw'] + MATMUL_WEIGHTS[1:4] + ['pl_scale'] + MATMUL_WEIGHTS[4:6] + [
    'ml_q_norm', 'ml_kv_norm'] + MATMUL_WEIGHTS[6:10] + ['ch_ln_g', 'ch_ln_b', 'ch_w_out']
REPLICATED = ['c_ctx', 'norm_g', 'cv_db', 'cv_ln_g', 'cv_ln_b', 'ml_nope_norm', 'ml_rope_norm', 'ch_w_s', 'ch_b_s']


def _pick(n, cands):
    for c in cands:
        if n % c == 0:
            return c
    raise ValueError(f"no tile for {n} among {cands}")


def _params(*sem):
    return pltpu.CompilerParams(dimension_semantics=sem, vmem_limit_bytes=VMEM_LIMIT)


def _sig(x):
    return 1.0 / (1.0 + jnp.exp(-x))


def _silu(x):
    return x * _sig(x)


def _dsilu(x):
    s = _sig(x)
    return s * (1.0 + x * (1.0 - s))


def _rowsum(v):
    return jnp.sum(v, axis=0, keepdims=True)


def _dot(a, b):
    return jnp.dot(a.astype(BF16), b.astype(BF16), preferred_element_type=F32)


def _dot_nt(a, b):
    return lax.dot_general(a.astype(BF16), b.astype(BF16), (((1,), (1,)), ((), ())), preferred_element_type=F32)


def _dot_tn(a, b):
    return lax.dot_general(a.astype(BF16), b.astype(BF16), (((0,), (0,)), ((), ())), preferred_element_type=F32)


class _Segs:
    def __init__(self, lens, tm):
        self.lens, self.tm, self.n = tuple(lens), tm, len(lens)
        self.starts, s = [], 0
        for l in lens:
            assert l % tm == 0
            self.starts.append(s // tm)
            s += l
        self.rows, self.tiles = s, s // tm

    def seg(self, i):
        r = 0
        for st in self.starts[1:]:
            r = r + jnp.where(i >= st, 1, 0)
        return r

    def is_first(self, i):
        f = i == 0
        for st in self.starts[1:]:
            f = jnp.logical_or(f, i == st)
        return f

    def spec(self, cols):
        return pl.BlockSpec((None, 1, cols), lambda i: (self.seg(i), 0, 0))


def _row(tm, cols, cb=0):
    return pl.BlockSpec((tm, cols), lambda i: (i, cb))


def _const(shape):
    return pl.BlockSpec(shape, lambda *_: (0,) * len(shape))


def _sds(shape, dtype=F32):
    return jax.ShapeDtypeStruct(shape, dtype)


def _mm(a, b, name, out_dtype=F32, rows=None):
    m, k, n = rows or a.shape[0], a.shape[1], b.shape[1]
    tm, tn = _pick(m, (512, 256, 128)), _pick(n, (512, 384, 256, 128))

    def body(a_ref, b_ref, o_ref):
        o_ref[...] = _dot(a_ref[...], b_ref[...]).astype(o_ref.dtype)

    return pl.pallas_call(
        body, grid=(m // tm, n // tn),
        in_specs=[pl.BlockSpec((tm, k), lambda i, j: (i, 0)), pl.BlockSpec((k, tn), lambda i, j: (0, j))],
        out_specs=pl.BlockSpec((tm, tn), lambda i, j: (i, j)), out_shape=_sds((m, n), out_dtype),
        compiler_params=_params("parallel", "parallel"), name=name)(a, b)


def _mm_nt(a, b, name, out_dtype=F32):
    m, k, n = a.shape[0], a.shape[1], b.shape[0]
    tm, tn = _pick(m, (512, 256, 128)), _pick(n, (512, 384, 256, 128))

    def body(a_ref, b_ref, o_ref):
        o_ref[...] = _dot_nt(a_ref[...], b_ref[...]).astype(o_ref.dtype)

    return pl.pallas_call(
        body, grid=(m // tm, n // tn),
        in_specs=[pl.BlockSpec((tm, k), lambda i, j: (i, 0)), pl.BlockSpec((tn, k), lambda i, j: (j, 0))],
        out_specs=pl.BlockSpec((tm, tn), lambda i, j: (i, j)), out_shape=_sds((m, n), out_dtype),
        compiler_params=_params("parallel", "parallel"), name=name)(a, b)


def _mm_tn(a, b, name, rows=None):
    t, k, n = rows or a.shape[0], a.shape[1], b.shape[1]
    tk, tn, tt = _pick(k, (512, 384, 256, 128)), _pick(n, (512, 384, 256, 128)), _pick(t, (512, 256, 128))

    def body(a_ref, b_ref, o_ref):
        @pl.when(pl.program_id(2) == 0)
        def _():
            o_ref[...] = jnp.zeros_like(o_ref)

        o_ref[...] += _dot_tn(a_ref[...], b_ref[...])

    return pl.pallas_call(
        body, grid=(k // tk, n // tn, t // tt),
        in_specs=[pl.BlockSpec((tt, tk), lambda i, j, s: (s, i)), pl.BlockSpec((tt, tn), lambda i, j, s: (s, j))],
        out_specs=pl.BlockSpec((tk, tn), lambda i, j, s: (i, j)), out_shape=_sds((k, n)),
        compiler_params=_params("parallel", "parallel", "arbitrary"), name=name)(a, b)


def _rms_mod_fwd(x, g, sc, sh, segs, name):
    d, tm = x.shape[1], segs.tm

    def body(x_ref, g_ref, sc_ref, sh_ref, h_ref):
        xf = x_ref[...]
        r = lax.rsqrt(jnp.mean(xf * xf, axis=-1, keepdims=True) + EPS)
        h_ref[...] = ((xf * r * g_ref[...]) * (1.0 + sc_ref[...]) + sh_ref[...]).astype(h_ref.dtype)

    return pl.pallas_call(
        body, grid=(segs.tiles,), in_specs=[_row(tm, d), _const((1, d)), segs.spec(d), segs.spec(d)],
        out_specs=_row(tm, d), out_shape=_sds((segs.rows, d), BF16), compiler_params=_params("parallel"),
        name=name)(x, g, sc, sh)


def _rms_mod_bwd(x, g, sc, sh, dh, dxr, segs, name):
    d, tm = x.shape[1], segs.tm

    def body(x_ref, g_ref, sc_ref, sh_ref, dh_ref, dxr_ref, dx_ref, dg_ref, dsc_ref, dsh_ref):
        i = pl.program_id(0)

        @pl.when(i == 0)
        def _():
            dg_ref[...] = jnp.zeros_like(dg_ref)

        @pl.when(segs.is_first(i))
        def _():
            dsc_ref[...] = jnp.zeros_like(dsc_ref)
            dsh_ref[...] = jnp.zeros_like(dsh_ref)

        xf, gg, dhf = x_ref[...], g_ref[...], dh_ref[...].astype(F32)
        r = lax.rsqrt(jnp.mean(xf * xf, axis=-1, keepdims=True) + EPS)
        xh = xf * r
        dsh_ref[...] += _rowsum(dhf)
        dsc_ref[...] += _rowsum(dhf * (xh * gg))
        du = dhf * (1.0 + sc_ref[...])
        dg_ref[...] += _rowsum(du * xh)
        dxh = du * gg
        dx_ref[...] = dxr_ref[...] + r * (dxh - xh * jnp.mean(dxh * xh, axis=-1, keepdims=True))

    return pl.pallas_call(
        body, grid=(segs.tiles,),
        in_specs=[_row(tm, d), _const((1, d)), segs.spec(d), segs.spec(d), _row(tm, d), _row(tm, d)],
        out_specs=[_row(tm, d), _const((1, d)), segs.spec(d), segs.spec(d)],
        out_shape=[_sds((segs.rows, d)), _sds((1, d)), _sds((segs.n, 1, d)), _sds((segs.n, 1, d))],
        compiler_params=_params("arbitrary"), name=name)(x, g, sc, sh, dh, dxr)


def _resid_fwd(x, o, gt, segs, name):
    d, tm = x.shape[1], segs.tm

    def body(x_ref, o_ref, gt_ref, y_ref):
        y_ref[...] = x_ref[...] + gt_ref[...] * o_ref[...]

    return pl.pallas_call(
        body, grid=(segs.tiles,), in_specs=[_row(tm, d), _row(tm, d), segs.spec(d)], out_specs=_row(tm, d),
        out_shape=_sds((segs.rows, d)), compiler_params=_params("parallel"), name=name)(x, o, gt)


def _resid_bwd(dxn, o, gt, segs, name):
    d, tm = o.shape[1], segs.tm

    def body(dxn_ref, o_ref, gt_ref, do_ref, dgt_ref):
        @pl.when(segs.is_first(pl.program_id(0)))
        def _():
            dgt_ref[...] = jnp.zeros_like(dgt_ref)

        dx = dxn_ref[...]
        do_ref[...] = (gt_ref[...] * dx).astype(do_ref.dtype)
        dgt_ref[...] += _rowsum(dx * o_ref[...])

    return pl.pallas_call(
        body, grid=(segs.tiles,), in_specs=[_row(tm, d), _row(tm, d), segs.spec(d)],
        out_specs=[_row(tm, d), segs.spec(d)], out_shape=[_sds((segs.rows, d), BF16), _sds((segs.n, 1, d))],
        compiler_params=_params("arbitrary"), name=name)(dxn, o, gt)


def _loss_head(y, tgt, tm, name):
    t, d = y.shape

    def body(y_ref, t_ref, l_ref, dy_ref):
        @pl.when(pl.program_id(0) == 0)
        def _():
            l_ref[...] = jnp.zeros_like(l_ref)

        e = y_ref[...] - t_ref[...]
        dy_ref[...] = e * (1.0 / d)
        l_ref[...] += _rowsum(e * e) * (0.5 / d)

    return pl.pallas_call(
        body, grid=(t // tm,), in_specs=[_row(tm, d), _row(tm, d)], out_specs=[_const((1, d)), _row(tm, d)],
        out_shape=[_sds((1, d)), _sds((t, d))], compiler_params=_params("arbitrary"), name=name)(y, tgt)


def _seq_spec(l, ce, row0, cb0=0):
    return pl.BlockSpec((l, ce), lambda j, s: (row0 // l + s, cb0 + j))


def _tap_sum(pad_ref, taps_ref, first_row, n_taps, l, ce, flip):
    out = []
    for r0 in range(0, l, CHUNK):
        rows = min(CHUNK, l - r0)
        acc = jnp.zeros((rows, ce), F32)
        for k in range(n_taps):
            kk = n_taps - 1 - k if flip else k
            acc = acc + pad_ref[pl.ds(first_row + r0 + k, rows), :] * taps_ref[kk:kk + 1, :]
        out.append(acc)
    return out


def _fill_pad(pad_ref, val, l, ce):
    pad_ref[pl.ds(0, CONV_PAD), :] = jnp.zeros((CONV_PAD, ce), F32)
    pad_ref[pl.ds(CONV_PAD + l, CONV_PAD), :] = jnp.zeros((CONV_PAD, ce), F32)
    pad_ref[pl.ds(CONV_PAD, l), :] = val


def _conv1_fwd(z, dw, db, nseq, l, row0, name):
    e = z.shape[1] // 3
    ce = LANES
    half = CONV_WIDTH // 2

    def body(a_ref, b_ref, dw_ref, db_ref, y_ref, pad_ref):
        _fill_pad(pad_ref, a_ref[...] * _sig(b_ref[...]), l, ce)
        pieces = _tap_sum(pad_ref, dw_ref, CONV_PAD - half, CONV_WIDTH, l, ce, False)
        for n, acc in enumerate(pieces):
            y_ref[pl.ds(n * CHUNK, acc.shape[0]), :] = acc + db_ref[...]

    return pl.pallas_call(
        body, grid=(e // ce, nseq),
        in_specs=[_seq_spec(l, ce, row0), _seq_spec(l, ce, row0, e // ce),
                  pl.BlockSpec((CONV_WIDTH, ce), lambda j, s: (0, j)), pl.BlockSpec((1, ce), lambda j, s: (0, j))],
        out_specs=pl.BlockSpec((l, ce), lambda j, s: (s, j)), out_shape=_sds((nseq * l, e)),
        scratch_shapes=[pltpu.VMEM((l + 2 * CONV_PAD, ce), F32)],
        compiler_params=_params("parallel", "arbitrary"), name=name)(z, z, dw, db)


def _conv1_bwd(dy2, z, dw, acc_dw, acc_db, nseq, l, row0, name):
    e = z.shape[1] // 3
    ce = LANES
    half = CONV_WIDTH // 2

    def body(dy_ref, a_ref, b_ref, dw_ref, adw_ref, adb_ref, da_ref, dbb_ref, ddw_ref, ddb_ref, ypad_ref, dpad_ref):
        @pl.when(pl.program_id(1) == 0)
        def _():
            ddw_ref[...] = adw_ref[...]
            ddb_ref[...] = adb_ref[...]

        a, sb = a_ref[...], _sig(b_ref[...])
        dy = dy_ref[...]
        _fill_pad(ypad_ref, a * sb, l, ce)
        _fill_pad(dpad_ref, dy, l, ce)
        ddb_ref[...] += _rowsum(dy)
        for k in range(CONV_WIDTH):
            ddw_ref[k:k + 1, :] += _rowsum(dy * ypad_ref[pl.ds(CONV_PAD - half + k, l), :])
        pieces = _tap_sum(dpad_ref, dw_ref, CONV_PAD - half, CONV_WIDTH, l, ce, True)
        for n, dy1 in enumerate(pieces):
            rows = pl.ds(n * CHUNK, dy1.shape[0])
            sbn = sb[n * CHUNK:n * CHUNK + dy1.shape[0], :]
            da_ref[rows, :] = dy1 * sbn
            dbb_ref[rows, :] = dy1 * a[n * CHUNK:n * CHUNK + dy1.shape[0], :] * sbn * (1.0 - sbn)

    cw = lambda j, s: (0, j)
    return pl.pallas_call(
        body, grid=(e // ce, nseq),
        in_specs=[_seq_spec(l, ce, row0), _seq_spec(l, ce, row0), _seq_spec(l, ce, row0, e // ce),
                  pl.BlockSpec((CONV_WIDTH, ce), cw), pl.BlockSpec((CONV_WIDTH, ce), cw), pl.BlockSpec((1, ce), cw)],
        out_specs=[pl.BlockSpec((l, ce), lambda j, s: (s, j)), pl.BlockSpec((l, ce), lambda j, s: (s, j)),
                   pl.BlockSpec((CONV_WIDTH, ce), cw), pl.BlockSpec((1, ce), cw)],
        out_shape=[_sds((nseq * l, e)), _sds((nseq * l, e)), _sds((CONV_WIDTH, e)), _sds((1, e))],
        scratch_shapes=[pltpu.VMEM((l + 2 * CONV_PAD, ce), F32), pltpu.VMEM((l + 2 * CONV_PAD, ce), F32)],
        compiler_params=_params("parallel", "arbitrary"), name=name)(dy2, z, z, dw, acc_dw, acc_db)


def _pool_tables(l, e):
    grp = e // len(POOL_WINDOWS)
    w = jnp.repeat(jnp.array(POOL_WINDOWS, jnp.int32), grp)[None, :]
    off = jnp.arange(POOL_TAPS, dtype=jnp.int32)[:, None] - POOL_TAPS // 2
    taps = jnp.logical_and(off >= -(w // 2), off < w - w // 2).astype(F32)
    t = jnp.arange(l, dtype=jnp.int32)[:, None]
    cnt = jnp.clip(t + (w - w // 2), 0, l) - jnp.clip(t - w // 2, 0, l)
    return taps, 1.0 / cnt.astype(F32)


def _pool1(v_src, taps, inv_cnt, nseq, l, row0, transpose, name, out_dtype):
    e = taps.shape[1]
    ce = LANES
    half = POOL_TAPS // 2

    def body(v_ref, taps_ref, ic_ref, o_ref, pad_ref):
        v = v_ref[...].astype(F32)
        if transpose:
            _fill_pad(pad_ref, v * ic_ref[...], l, ce)
            pieces = _tap_sum(pad_ref, taps_ref, CONV_PAD - half + 1, POOL_TAPS, l, ce, True)
        else:
            _fill_pad(pad_ref, v, l, ce)
            pieces = _tap_sum(pad_ref, taps_ref, CONV_PAD - half, POOL_TAPS, l, ce, False)
        for n, acc in enumerate(pieces):
            rows = pl.ds(n * CHUNK, acc.shape[0])
            vn = v[n * CHUNK:n * CHUNK + acc.shape[0], :]
            if transpose:
                o_ref[rows, :] = (acc - vn).astype(o_ref.dtype)
            else:
                o_ref[rows, :] = (acc * ic_ref[rows, :] - vn).astype(o_ref.dtype)

    return pl.pallas_call(
        body, grid=(e // ce, nseq),
        in_specs=[_seq_spec(l, ce, row0), pl.BlockSpec((POOL_TAPS, ce), lambda j, s: (0, j)),
                  pl.BlockSpec((l, ce), lambda j, s: (0, j))],
        out_specs=pl.BlockSpec((l, ce), lambda j, s: (s, j)), out_shape=_sds((nseq * l, e), out_dtype),
        scratch_shapes=[pltpu.VMEM((l + 2 * CONV_PAD, ce), F32)],
        compiler_params=_params("parallel", "arbitrary"), name=name)(v_src, taps, inv_cnt)


def _layernorm_parts(x, eps=EPS):
    mu = jnp.mean(x, axis=-1, keepdims=True)
    xc = x - mu
    r = lax.rsqrt(jnp.mean(xc * xc, axis=-1, keepdims=True) + eps)
    return xc * r, r


def _layernorm_bwd(dy, xh, r, g):
    dxh = dy * g
    return r * (dxh - jnp.mean(dxh, axis=-1, keepdims=True) - xh * jnp.mean(dxh * xh, axis=-1, keepdims=True))


def _conv2_fwd(y2, z, ln_g, ln_b, tm, name):
    t, e = y2.shape

    def body(y_ref, g_ref, lg_ref, lb_ref, o_ref):
        xh, _ = _layernorm_parts(y_ref[...])
        o_ref[...] = (_silu(xh * lg_ref[...] + lb_ref[...]) * _silu(g_ref[...])).astype(o_ref.dtype)

    return pl.pallas_call(
        body, grid=(t // tm,), in_specs=[_row(tm, e), _row(tm, e, 2), _const((1, e)), _const((1, e))],
        out_specs=_row(tm, e), out_shape=_sds((t, e), BF16), compiler_params=_params("parallel"),
        name=name)(y2, z, ln_g, ln_b)


def _conv2_bwd(dy4, y2, z, ln_g, ln_b, tm, name):
    t, e = y2.shape

    def body(dy_ref, y_ref, g_ref, lg_ref, lb_ref, dy2_ref, dg_ref, dlg_ref, dlb_ref):
        @pl.when(pl.program_id(0) == 0)
        def _():
            dlg_ref[...] = jnp.zeros_like(dlg_ref)
            dlb_ref[...] = jnp.zeros_like(dlb_ref)

        dy, gz = dy_ref[...], g_ref[...]
        xh, r = _layernorm_parts(y_ref[...])
        y3 = xh * lg_ref[...] + lb_ref[...]
        dg_ref[...] = dy * _silu(y3) * _dsilu(gz)
        dy3 = dy * _silu(gz) * _dsilu(y3)
        dlg_ref[...] += _rowsum(dy3 * xh)
        dlb_ref[...] += _rowsum(dy3)
        dy2_ref[...] = _layernorm_bwd(dy3, xh, r, lg_ref[...])

    return pl.pallas_call(
        body, grid=(t // tm,),
        in_specs=[_row(tm, e), _row(tm, e), _row(tm, e, 2), _const((1, e)), _const((1, e))],
        out_specs=[_row(tm, e), _row(tm, e), _const((1, e)), _const((1, e))],
        out_shape=[_sds((t, e)), _sds((t, e)), _sds((1, e)), _sds((1, e))],
        compiler_params=_params("arbitrary"), name=name)(dy4, y2, z, ln_g, ln_b)


def _pool2_fwd(pm, w_grp, scale, z, tm, name):
    t, e = pm.shape
    ng, gw = w_grp.shape[0], w_grp.shape[1]

    def body(pm_ref, w_ref, sc_ref, g_ref, o_ref):
        for k in range(ng):
            cols = slice(k * gw, (k + 1) * gw)
            y = _dot(pm_ref[:, cols], w_ref[k])
            o_ref[:, cols] = (y * sc_ref[:, cols] * _silu(g_ref[:, cols])).astype(o_ref.dtype)

    return pl.pallas_call(
        body, grid=(t // tm,), in_specs=[_row(tm, e), _const(w_grp.shape), _const((1, e)), _row(tm, e, 1)],
        out_specs=_row(tm, e), out_shape=_sds((t, e), BF16), compiler_params=_params("parallel"),
        name=name)(pm, w_grp, scale, z)


def _pool2_bwd(dy2, pm, w_grp, scale, z, tm, name):
    t, e = pm.shape
    ng, gw = w_grp.shape[0], w_grp.shape[1]

    def body(dy_ref, pm_ref, w_ref, sc_ref, g_ref, dpm_ref, dg_ref, dsc_ref, dw_ref):
        @pl.when(pl.program_id(0) == 0)
        def _():
            dsc_ref[...] = jnp.zeros_like(dsc_ref)
            dw_ref[...] = jnp.zeros_like(dw_ref)

        for k in range(ng):
            cols = slice(k * gw, (k + 1) * gw)
            dy, gz, sc, pmk = dy_ref[:, cols], g_ref[:, cols], sc_ref[:, cols], pm_ref[:, cols]
            y = _dot(pmk, w_ref[k])
            dg_ref[:, cols] = dy * (y * sc) * _dsilu(gz)
            dys = dy * _silu(gz)
            dsc_ref[:, cols] += _rowsum(dys * y)
            dyk = dys * sc
            dpm_ref[:, cols] = _dot_nt(dyk, w_ref[k])
            dw_ref[k] += _dot_tn(pmk, dyk)

    return pl.pallas_call(
        body, grid=(t // tm,),
        in_specs=[_row(tm, e), _row(tm, e), _const(w_grp.shape), _const((1, e)), _row(tm, e, 1)],
        out_specs=[_row(tm, e), _row(tm, e), _const((1, e)), _const(w_grp.shape)],
        out_shape=[_sds((t, e)), _sds((t, e)), _sds((1, e)), _sds(w_grp.shape)],
        compiler_params=_params("arbitrary"), name=name)(dy2, pm, w_grp, scale, z)


def _rms_f(x, g, n):
    r = lax.rsqrt(jnp.sum(x * x, axis=-1, keepdims=True) * (1.0 / n) + EPS)
    return x * r * g


def _rms_b(x, g, dy, n):
    r = lax.rsqrt(jnp.sum(x * x, axis=-1, keepdims=True) * (1.0 / n) + EPS)
    xh = x * r
    dxh = dy * g
    return r * (dxh - xh * (jnp.sum(dxh * xh, axis=-1, keepdims=True) * (1.0 / n))), dy * xh


def _swap16(x):
    lane = lax.broadcasted_iota(jnp.int32, x.shape, 1)
    return jnp.where(lane % 32 < 16, pltpu.roll(x, LANES - 16, 1), pltpu.roll(x, 16, 1))


def _rope(x, c, s):
    return x * c + _swap16(x) * s


def _rope_t(dy, c, s):
    return dy * c + _swap16(dy * s)


def _rope_tables(l, lc, nb):
    t = jnp.arange(l, dtype=jnp.int32)
    row_id, col_id = (t // GRID_W).astype(F32), (t % GRID_W).astype(F32)
    axis_dim = MLA_ROPE // 2
    freqs = ROPE_THETA ** (-jnp.arange(0, axis_dim, 2, dtype=F32) / axis_dim)
    ar, ac = row_id[:, None] * freqs, col_id[:, None] * freqs
    pad1, pad0 = jnp.ones((l, LANES - MLA_ROPE), F32), jnp.zeros((l, LANES - MLA_ROPE), F32)
    ctab = jnp.concatenate([jnp.cos(ar), jnp.cos(ar), jnp.cos(ac), jnp.cos(ac), pad1], axis=1)
    stab = jnp.concatenate([-jnp.sin(ar), jnp.sin(ar), -jnp.sin(ac), jnp.sin(ac), pad0], axis=1)
    ctab = jnp.concatenate([jnp.tile(ctab, (nb, 1)), jnp.ones((nb * lc, LANES), F32)], axis=0)
    stab = jnp.concatenate([jnp.tile(stab, (nb, 1)), jnp.zeros((nb * lc, LANES), F32)], axis=0)
    return ctab, stab


def _kv_pre_fwd(zkv, kv_norm, rope_g, ctab, stab, tm, name):
    t = zkv.shape[0]

    def body(z_ref, gk_ref, gr_ref, c_ref, s_ref, ck_ref, kr_ref):
        ck_ref[...] = _rms_f(z_ref[:, :MLA_KV_RANK], gk_ref[...], MLA_KV_RANK).astype(ck_ref.dtype)
        kr = _rms_f(z_ref[:, MLA_KV_RANK:], gr_ref[...], MLA_ROPE)
        kr_ref[...] = _rope(kr, c_ref[...], s_ref[...]).astype(kr_ref.dtype)

    w = MLA_KV_RANK + LANES
    return pl.pallas_call(
        body, grid=(t // tm,),
        in_specs=[_row(tm, w), _const((1, MLA_KV_RANK)), _const((1, LANES)), _row(tm, LANES), _row(tm, LANES)],
        out_specs=[_row(tm, MLA_KV_RANK), _row(tm, LANES)],
        out_shape=[_sds((t, MLA_KV_RANK), BF16), _sds((t, LANES), BF16)],
        compiler_params=_params("parallel"), name=name)(zkv, kv_norm, rope_g, ctab, stab)


def _kv_pre_bwd(dck, dkr, zkv, kv_norm, rope_g, ctab, stab, tm, name):
    t = zkv.shape[0]
    w = MLA_KV_RANK + LANES

    def body(dck_ref, dkr_ref, z_ref, gk_ref, gr_ref, c_ref, s_ref, dz_ref, dgk_ref, dgr_ref):
        @pl.when(pl.program_id(0) == 0)
        def _():
            dgk_ref[...] = jnp.zeros_like(dgk_ref)
            dgr_ref[...] = jnp.zeros_like(dgr_ref)

        dx, dg = _rms_b(z_ref[:, :MLA_KV_RANK], gk_ref[...], dck_ref[...], MLA_KV_RANK)
        dz_ref[:, :MLA_KV_RANK] = dx
        dgk_ref[...] += _rowsum(dg)
        dy = _rope_t(dkr_ref[...], c_ref[...], s_ref[...])
        dx, dg = _rms_b(z_ref[:, MLA_KV_RANK:], gr_ref[...], dy, MLA_ROPE)
        dz_ref[:, MLA_KV_RANK:] = dx
        dgr_ref[...] += _rowsum(dg)

    return pl.pallas_call(
        body, grid=(t // tm,),
        in_specs=[_row(tm, MLA_KV_RANK), _row(tm, LANES), _row(tm, w), _const((1, MLA_KV_RANK)), _const((1, LANES)),
                  _row(tm, LANES), _row(tm, LANES)],
        out_specs=[_row(tm, w), _const((1, MLA_KV_RANK)), _const((1, LANES))],
        out_shape=[_sds((t, w)), _sds((1, MLA_KV_RANK)), _sds((1, LANES))],
        compiler_params=_params("arbitrary"), name=name)(dck, dkr, zkv, kv_norm, rope_g, ctab, stab)


def _q_pre_fwd(zq, q_norm, tm, name):
    t, w = zq.shape

    def body(z_ref, g_ref, o_ref):
        o_ref[...] = _rms_f(z_ref[...], g_ref[...], w).astype(o_ref.dtype)

    return pl.pallas_call(
        body, grid=(t // tm,), in_specs=[_row(tm, w), _const((1, w))], out_specs=_row(tm, w),
        out_shape=_sds((t, w), BF16), compiler_params=_params("parallel"), name=name)(zq, q_norm)


def _q_pre_bwd(dcq, zq, q_norm, tm, name):
    t, w = zq.shape

    def body(d_ref, z_ref, g_ref, dz_ref, dg_ref):
        @pl.when(pl.program_id(0) == 0)
        def _():
            dg_ref[...] = jnp.zeros_like(dg_ref)

        dx, dg = _rms_b(z_ref[...], g_ref[...], d_ref[...], w)
        dz_ref[...] = dx
        dg_ref[...] += _rowsum(dg)

    return pl.pallas_call(
        body, grid=(t // tm,), in_specs=[_row(tm, w), _row(tm, w), _const((1, w))],
        out_specs=[_row(tm, w), _const((1, w))], out_shape=[_sds((t, w)), _sds((1, w))],
        compiler_params=_params("arbitrary"), name=name)(dcq, zq, q_norm)


def _q_post_fwd(q, nope_g, rope_g, ctab, stab, tm, name):
    t, w = q.shape

    def body(q_ref, gn_ref, gr_ref, c_ref, s_ref, o_ref):
        for h in range(MLA_HEADS):
            a = h * HEAD_W
            o_ref[:, a:a + LANES] = _rms_f(q_ref[:, a:a + LANES], gn_ref[...], MLA_NOPE).astype(o_ref.dtype)
            qr = _rms_f(q_ref[:, a + LANES:a + HEAD_W], gr_ref[...], MLA_ROPE)
            o_ref[:, a + LANES:a + HEAD_W] = _rope(qr, c_ref[...], s_ref[...]).astype(o_ref.dtype)

    return pl.pallas_call(
        body, grid=(t // tm,),
        in_specs=[_row(tm, w), _const((1, LANES)), _const((1, LANES)), _row(tm, LANES), _row(tm, LANES)],
        out_specs=_row(tm, w), out_shape=_sds((t, w), BF16), compiler_params=_params("parallel"),
        name=name)(q, nope_g, rope_g, ctab, stab)


def _q_post_bwd(dqf, q, nope_g, rope_g, ctab, stab, tm, name):
    t, w = q.shape

    def body(d_ref, q_ref, gn_ref, gr_ref, c_ref, s_ref, dq_ref, dgn_ref, dgr_ref):
        @pl.when(pl.program_id(0) == 0)
        def _():
            dgn_ref[...] = jnp.zeros_like(dgn_ref)
            dgr_ref[...] = jnp.zeros_like(dgr_ref)

        for h in range(MLA_HEADS):
            a = h * HEAD_W
            dx, dg = _rms_b(q_ref[:, a:a + LANES], gn_ref[...], d_ref[:, a:a + LANES], MLA_NOPE)
            dq_ref[:, a:a + LANES] = dx.astype(dq_ref.dtype)
            dgn_ref[...] += _rowsum(dg)
            dy = _rope_t(d_ref[:, a + LANES:a + HEAD_W], c_ref[...], s_ref[...])
            dx, dg = _rms_b(q_ref[:, a + LANES:a + HEAD_W], gr_ref[...], dy, MLA_ROPE)
            dq_ref[:, a + LANES:a + HEAD_W] = dx.astype(dq_ref.dtype)
            dgr_ref[...] += _rowsum(dg)

    return pl.pallas_call(
        body, grid=(t // tm,),
        in_specs=[_row(tm, w), _row(tm, w), _const((1, LANES)), _const((1, LANES)), _row(tm, LANES), _row(tm, LANES)],
        out_specs=[_row(tm, w), _const((1, LANES)), _const((1, LANES))],
        out_shape=[_sds((t, w), BF16), _sds((1, LANES)), _sds((1, LANES))],
        compiler_params=_params("arbitrary"), name=name)(dqf, q, nope_g, rope_g, ctab, stab)


def _k_post_fwd(kv, krr, nope_g, tm, name):
    t, w = kv.shape

    def body(kv_ref, kr_ref, gn_ref, k_ref, v_ref):
        for h in range(MLA_HEADS):
            a = h * HEAD_W
            k_ref[:, a:a + LANES] = _rms_f(kv_ref[:, a:a + LANES], gn_ref[...], MLA_NOPE).astype(k_ref.dtype)
            k_ref[:, a + LANES:a + HEAD_W] = kr_ref[...]
            v_ref[:, h * LANES:(h + 1) * LANES] = kv_ref[:, a + LANES:a + HEAD_W].astype(v_ref.dtype)

    return pl.pallas_call(
        body, grid=(t // tm,), in_specs=[_row(tm, w), _row(tm, LANES), _const((1, LANES))],
        out_specs=[_row(tm, w), _row(tm, w // 2)], out_shape=[_sds((t, w), BF16), _sds((t, w // 2), BF16)],
        compiler_params=_params("parallel"), name=name)(kv, krr, nope_g)


def _k_post_bwd(dkf, dvf, kv, nope_g, tm, name):
    t, w = kv.shape

    def body(dk_ref, dv_ref, kv_ref, gn_ref, dkv_ref, dkr_ref, dgn_ref):
        @pl.when(pl.program_id(0) == 0)
        def _():
            dgn_ref[...] = jnp.zeros_like(dgn_ref)

        dkr = jnp.zeros(dkr_ref.shape, F32)
        for h in range(MLA_HEADS):
            a = h * HEAD_W
            dx, dg = _rms_b(kv_ref[:, a:a + LANES], gn_ref[...], dk_ref[:, a:a + LANES], MLA_NOPE)
            dkv_ref[:, a:a + LANES] = dx.astype(dkv_ref.dtype)
            dgn_ref[...] += _rowsum(dg)
            dkv_ref[:, a + LANES:a + HEAD_W] = dv_ref[:, h * LANES:(h + 1) * LANES].astype(dkv_ref.dtype)
            dkr = dkr + dk_ref[:, a + LANES:a + HEAD_W]
        dkr_ref[...] = dkr

    return pl.pallas_call(
        body, grid=(t // tm,), in_specs=[_row(tm, w), _row(tm, w // 2), _row(tm, w), _const((1, LANES))],
        out_specs=[_row(tm, w), _row(tm, LANES), _const((1, LANES))],
        out_shape=[_sds((t, w), BF16), _sds((t, LANES)), _sds((1, LANES))],
        compiler_params=_params("arbitrary"), name=name)(dkf, dvf, kv, nope_g)


def _attn_specs(nb, l, lc, tq):
    nq = l // tq
    ctx0 = nb * l // lc
    q_spec = lambda w: pl.BlockSpec((tq, w), lambda b, h, i: (b * nq + i, h))
    lat = lambda w: pl.BlockSpec((l, w), lambda b, h, i: (b, h))
    ctx = lambda w: pl.BlockSpec((lc, w), lambda b, h, i: (ctx0 + b, h))
    return nq, q_spec, lat, ctx


def _attn_fwd(qf, kf, vf, nb, l, lc, name):
    tq = _pick(l, (256, 128))
    nq, q_spec, lat, ctx = _attn_specs(nb, l, lc, tq)

    def body(q_ref, kl_ref, kc_ref, vl_ref, vc_ref, o_ref, lse_ref):
        q = q_ref[...]
        s1 = _dot_nt(q, kl_ref[...]) * MLA_SCALE
        s2 = _dot_nt(q, kc_ref[...]) * MLA_SCALE
        m = jnp.maximum(jnp.max(s1, axis=-1, keepdims=True), jnp.max(s2, axis=-1, keepdims=True))
        p1, p2 = jnp.exp(s1 - m), jnp.exp(s2 - m)
        den = jnp.sum(p1, axis=-1, keepdims=True) + jnp.sum(p2, axis=-1, keepdims=True)
        o_ref[...] = (_dot(p1, vl_ref[...]) + _dot(p2, vc_ref[...])) / den
        lse_ref[...] = jnp.broadcast_to(m + jnp.log(den), lse_ref.shape)

    return pl.pallas_call(
        body, grid=(nb, MLA_HEADS, nq),
        in_specs=[q_spec(HEAD_W), lat(HEAD_W), ctx(HEAD_W), lat(LANES), ctx(LANES)],
        out_specs=[q_spec(LANES), q_spec(LANES)],
        out_shape=[_sds((nb * l, MLA_HEADS * LANES)), _sds((nb * l, MLA_HEADS * LANES))],
        compiler_params=_params("parallel", "parallel", "arbitrary"), name=name)(qf, kf, kf, vf, vf)


def _attn_bwd(do, o, lse, qf, kf, vf, nb, l, lc, name):
    tq = _pick(l, (256, 128))
    nq, q_spec, lat, ctx = _attn_specs(nb, l, lc, tq)
    out_lat = lambda w: pl.BlockSpec((l, w), lambda b, h, i: (b, h))
    out_ctx = lambda w: pl.BlockSpec((lc, w), lambda b, h, i: (b, h))

    def body(do_ref, o_ref, lse_ref, q_ref, kl_ref, kc_ref, vl_ref, vc_ref, dq_ref, dkl_ref, dkc_ref, dvl_ref, dvc_ref):
        @pl.when(pl.program_id(2) == 0)
        def _():
            dkl_ref[...] = jnp.zeros_like(dkl_ref)
            dkc_ref[...] = jnp.zeros_like(dkc_ref)
            dvl_ref[...] = jnp.zeros_like(dvl_ref)
            dvc_ref[...] = jnp.zeros_like(dvc_ref)

        q, dof = q_ref[...], do_ref[...]
        delta = jnp.sum(dof * o_ref[...], axis=-1, keepdims=True)
        lse = lse_ref[:, :1]
        dq = jnp.zeros(dq_ref.shape, F32)
        for k_ref, v_ref, dk_ref, dv_ref in ((kl_ref, vl_ref, dkl_ref, dvl_ref), (kc_ref, vc_ref, dkc_ref, dvc_ref)):
            p = jnp.exp(_dot_nt(q, k_ref[...]) * MLA_SCALE - lse)
            ds = p * (_dot_nt(dof, v_ref[...]) - delta) * MLA_SCALE
            dq = dq + _dot(ds, k_ref[...])
            dk_ref[...] += _dot_tn(ds, q)
            dv_ref[...] += _dot_tn(p, dof)
        dq_ref[...] = dq

    kw, vw = MLA_HEADS * HEAD_W, MLA_HEADS * LANES
    return pl.pallas_call(
        body, grid=(nb, MLA_HEADS, nq),
        in_specs=[q_spec(LANES), q_spec(LANES), q_spec(LANES), q_spec(HEAD_W), lat(HEAD_W), ctx(HEAD_W), lat(LANES),
                  ctx(LANES)],
        out_specs=[q_spec(HEAD_W), out_lat(HEAD_W), out_ctx(HEAD_W), out_lat(LANES), out_ctx(LANES)],
        out_shape=[_sds((nb * l, kw)), _sds((nb * l, kw)), _sds((nb * lc, kw)), _sds((nb * l, vw)),
                   _sds((nb * lc, vw))],
        compiler_params=_params("parallel", "parallel", "arbitrary"), name=name)(do, o, lse, qf, kf, kf, vf, vf)


def _gate_fwd(o, g, tm, name):
    t, e = o.shape

    def body(o_ref, g_ref, y_ref):
        y_ref[...] = (o_ref[...] * _silu(g_ref[...])).astype(y_ref.dtype)

    return pl.pallas_call(
        body, grid=(t // tm,), in_specs=[_row(tm, e), _row(tm, e)], out_specs=_row(tm, e),
        out_shape=_sds((t, e), BF16), compiler_params=_params("parallel"), name=name)(o, g)


def _gate_bwd(dy, o, g, tm, name):
    t, e = o.shape

    def body(dy_ref, o_ref, g_ref, do_ref, dg_ref):
        dy, gz = dy_ref[...], g_ref[...]
        do_ref[...] = dy * _silu(gz)
        dg_ref[...] = dy * o_ref[...] * _dsilu(gz)

    return pl.pallas_call(
        body, grid=(t // tm,), in_specs=[_row(tm, e), _row(tm, e), _row(tm, e)],
        out_specs=[_row(tm, e), _row(tm, e)], out_shape=[_sds((t, e)), _sds((t, e))],
        compiler_params=_params("parallel"), name=name)(dy, o, g)


def _chunk_fwd(z, ln_g, ln_b, w_s, bs_full, name):
    t, e = z.shape[0], z.shape[1] // 3

    def body(u_ref, v_ref, g_ref, lg_ref, lb_ref, w_ref, bs_ref, y_ref):
        xh, _ = _layernorm_parts(v_ref[...])
        vn = xh * lg_ref[...] + lb_ref[...]
        for k in range(CHUNK_GROUPS):
            cols = slice(k * LANES, (k + 1) * LANES)
            s = _dot(w_ref[k], vn[:, cols]) + bs_ref[:, cols]
            y_ref[:, cols] = (u_ref[:, cols] * s * _silu(g_ref[:, cols])).astype(y_ref.dtype)

    return pl.pallas_call(
        body, grid=(t // CHUNK,),
        in_specs=[_row(CHUNK, e, 0), _row(CHUNK, e, 1), _row(CHUNK, e, 2), _const((1, e)), _const((1, e)),
                  _const(w_s.shape), _const((CHUNK, e))],
        out_specs=_row(CHUNK, e), out_shape=_sds((t, e), BF16), compiler_params=_params("parallel"),
        name=name)(z, z, z, ln_g, ln_b, w_s, bs_full)


def _chunk_bwd(dy, z, ln_g, ln_b, w_s, bs_full, name):
    t, e = z.shape[0], z.shape[1] // 3

    def body(dy_ref, u_ref, v_ref, g_ref, lg_ref, lb_ref, w_ref, bs_ref, dz_ref, dw_ref, dbs_ref, dlg_ref, dlb_ref,
             acc_ref):
        i = pl.program_id(0)

        @pl.when(i == 0)
        def _():
            dw_ref[...] = jnp.zeros_like(dw_ref)
            dlg_ref[...] = jnp.zeros_like(dlg_ref)
            dlb_ref[...] = jnp.zeros_like(dlb_ref)
            acc_ref[...] = jnp.zeros_like(acc_ref)

        xh, r = _layernorm_parts(v_ref[...])
        vn = xh * lg_ref[...] + lb_ref[...]
        dvn = []
        for k in range(CHUNK_GROUPS):
            cols = slice(k * LANES, (k + 1) * LANES)
            dyk, u, gz = dy_ref[:, cols], u_ref[:, cols], g_ref[:, cols]
            s = _dot(w_ref[k], vn[:, cols]) + bs_ref[:, cols]
            sg = _silu(gz)
            dz_ref[:, cols] = (dyk * s * sg).astype(dz_ref.dtype)
            dz_ref[:, 2 * e + k * LANES:2 * e + (k + 1) * LANES] = (dyk * u * s * _dsilu(gz)).astype(dz_ref.dtype)
            ds = dyk * u * sg
            acc_ref[:, cols] += ds
            dw_ref[k] += _dot_nt(ds, vn[:, cols])
            dvn.append(_dot_tn(w_ref[k], ds))
        dvn = jnp.concatenate(dvn, axis=1)
        dlg_ref[...] += _rowsum(dvn * xh)
        dlb_ref[...] += _rowsum(dvn)
        dz_ref[:, e:2 * e] = _layernorm_bwd(dvn, xh, r, lg_ref[...]).astype(dz_ref.dtype)

        @pl.when(i == pl.num_programs(0) - 1)
        def _():
            lane = lax.broadcasted_iota(jnp.int32, dbs_ref.shape, 1)
            out = jnp.zeros(dbs_ref.shape, F32)
            for k in range(CHUNK_GROUPS):
                col = jnp.sum(acc_ref[:, k * LANES:(k + 1) * LANES], axis=1, keepdims=True)
                out = jnp.where(lane == k, col, out)
            dbs_ref[...] = out

    return pl.pallas_call(
        body, grid=(t // CHUNK,),
        in_specs=[_row(CHUNK, e), _row(CHUNK, e, 0), _row(CHUNK, e, 1), _row(CHUNK, e, 2), _const((1, e)),
                  _const((1, e)), _const(w_s.shape), _const((CHUNK, e))],
        out_specs=[_row(CHUNK, 3 * e), _const(w_s.shape), _const((CHUNK, CHUNK_GROUPS)), _const((1, e)),
                   _const((1, e))],
        out_shape=[_sds((t, 3 * e), BF16), _sds(w_s.shape), _sds((CHUNK, CHUNK_GROUPS)), _sds((1, e)), _sds((1, e))],
        scratch_shapes=[pltpu.VMEM((CHUNK, e), F32)],
        compiler_params=_params("arbitrary"), name=name)(dy, z, z, z, ln_g, ln_b, w_s, bs_full)


def _mod_rows(mods, layer, d, nseg):
    m = mods[layer, :nseg]
    return [m[:, None, k * d:(k + 1) * d] for k in range(3)]


def _local_step(x, ctx, tgt, w, mods):
    nb, l, d = x.shape
    lc = ctx.shape[1]
    e = d
    tl, ta = nb * l, nb * (l + lc)
    tm = _pick(lc, (256, 128))
    segs_a, segs_l = _Segs((l,) * nb + (lc,) * nb, tm), _Segs((l,) * nb, tm)
    norm_g = w['norm_g']
    g = {}

    xa0 = jnp.concatenate([x.reshape(tl, d), ctx.reshape(nb * lc, d)], axis=0)

    sh0, sc0, gt0 = _mod_rows(mods, 0, d, 2 * nb)
    h0 = _rms_mod_fwd(xa0, norm_g[0:1], sc0, sh0, segs_a, "l0_norm")
    z0 = _mm(h0, w['cv_w_in'], "l0_in")
    y2_0 = jnp.concatenate([_conv1_fwd(z0, w['cv_dw'], w['cv_db'], nb, l, 0, "l0_conv_lat"),
                            _conv1_fwd(z0, w['cv_dw'], w['cv_db'], nb, lc, tl, "l0_conv_ctx")], axis=0)
    y4_0 = _conv2_fwd(y2_0, z0, w['cv_ln_g'], w['cv_ln_b'], tm, "l0_gate")
    o0 = _mm(y4_0, w['cv_w_out'], "l0_out")
    xa1 = _resid_fwd(xa0, o0, gt0, segs_a, "l0_resid")

    sh1, sc1, gt1 = _mod_rows(mods, 1, d, 2 * nb)
    h1 = _rms_mod_fwd(xa1, norm_g[1:2], sc1, sh1, segs_a, "l1_norm")
    z1 = _mm(h1, w['pl_w_in'], "l1_in")
    taps_l, ic_l = _pool_tables(l, e)
    taps_c, ic_c = _pool_tables(lc, e)
    pm1 = jnp.concatenate([_pool1(z1, taps_l, ic_l, nb, l, 0, False, "l1_pool_lat", BF16),
                           _pool1(z1, taps_c, ic_c, nb, lc, tl, False, "l1_pool_ctx", BF16)], axis=0)
    y2_1 = _pool2_fwd(pm1, w['pl_w_grp'], w['pl_scale'], z1, tm, "l1_group")
    o1 = _mm(y2_1, w['pl_w_out'], "l1_out")
    xa2 = _resid_fwd(xa1, o1, gt1, segs_a, "l1_resid")

    sh2, sc2, gt2 = _mod_rows(mods, 2, d, 2 * nb)
    h2 = _rms_mod_fwd(xa2, norm_g[2:3], sc2, sh2, segs_a, "l2_norm")
    w_in = w['ml_w_in']
    kvc = MLA_KV_RANK + MLA_ROPE
    w_in_p = jnp.concatenate([w_in[:, :kvc], jnp.zeros((d, LANES - MLA_ROPE), w_in.dtype), w_in[:, kvc:]], axis=1)
    w_uq_p = jnp.pad(w['ml_w_uq'].reshape(MLA_Q_RANK, MLA_HEADS, MLA_NOPE + MLA_ROPE),
                     ((0, 0), (0, 0), (0, HEAD_W - MLA_NOPE - MLA_ROPE))).reshape(MLA_Q_RANK, MLA_HEADS * HEAD_W)
    rope_g = jnp.pad(w['ml_rope_norm'], ((0, 0), (0, LANES - MLA_ROPE)))
    nope_g = w['ml_nope_norm']
    ctab, stab = _rope_tables(l, lc, nb)
    kvw = MLA_KV_RANK + LANES
    z2 = _mm(h2, w_in_p, "l2_in")
    zkv, zq, zg = z2[:, :kvw], z2[:tl, kvw:kvw + MLA_Q_RANK], z2[:tl, kvw + MLA_Q_RANK:]
    ckvn, krr = _kv_pre_fwd(zkv, w['ml_kv_norm'], rope_g[1:2], ctab, stab, tm, "l2_kv_pre")
    cqn = _q_pre_fwd(zq, w['ml_q_norm'], tm, "l2_q_pre")
    q2 = _mm(cqn, w_uq_p, "l2_uq")
    kv2 = _mm(ckvn, w['ml_w_ukv'], "l2_ukv")
    qf = _q_post_fwd(q2, nope_g[0:1], rope_g[0:1], ctab, stab, tm, "l2_q_post")
    kf, vf = _k_post_fwd(kv2, krr, nope_g[1:2], tm, "l2_k_post")
    o_att, lse = _attn_fwd(qf, kf, vf, nb, l, lc, "l2_attn")
    og = _gate_fwd(o_att, zg, tm, "l2_gate")
    o2 = _mm(og, w['ml_w_out'], "l2_out")
    x3 = _resid_fwd(xa2, o2, gt2[:nb], segs_l, "l2_resid")

    sh3, sc3, gt3 = _mod_rows(mods, 3, d, nb)
    h3 = _rms_mod_fwd(x3, norm_g[3:4], sc3, sh3, segs_l, "l3_norm")
    z3 = _mm(h3, w['ch_w_in'], "l3_in")
    bs_full = jnp.repeat(w['ch_b_s'], e // CHUNK_GROUPS, axis=1)
    y3 = _chunk_fwd(z3, w['ch_ln_g'], w['ch_ln_b'], w['ch_w_s'], bs_full, "l3_chunk")
    o3 = _mm(y3, w['ch_w_out'], "l3_out")
    x4 = _resid_fwd(x3, o3, gt3, segs_l, "l3_resid")

    loss_vec, dx4 = _loss_head(x4, tgt.reshape(tl, d), tm, "loss")

    do3, dgt3 = _resid_bwd(dx4, o3, gt3, segs_l, "l3_resid_b")
    dy3 = _mm_nt(do3, w['ch_w_out'], "l3_out_bx")
    g['ch_w_out'] = _mm_tn(y3, do3, "l3_out_bw")
    dz3, g['ch_w_s'], g['ch_b_s'], g['ch_ln_g'], g['ch_ln_b'] = _chunk_bwd(
        dy3, z3, w['ch_ln_g'], w['ch_ln_b'], w['ch_w_s'], bs_full, "l3_chunk_b")
    dh3 = _mm_nt(dz3, w['ch_w_in'], "l3_in_bx")
    g['ch_w_in'] = _mm_tn(h3, dz3, "l3_in_bw")
    dx3, dng3, dsc3, dsh3 = _rms_mod_bwd(x3, norm_g[3:4], sc3, sh3, dh3, dx4, segs_l, "l3_norm_b")

    do2, dgt2 = _resid_bwd(dx3, o2, gt2[:nb], segs_l, "l2_resid_b")
    dog = _mm_nt(do2, w['ml_w_out'], "l2_out_bx")
    g['ml_w_out'] = _mm_tn(og, do2, "l2_out_bw")
    d_att, dzg = _gate_bwd(dog, o_att, zg, tm, "l2_gate_b")
    dqf, dkl, dkc, dvl, dvc = _attn_bwd(d_att, o_att, lse, qf, kf, vf, nb, l, lc, "l2_attn_b")
    dkf, dvf = jnp.concatenate([dkl, dkc], axis=0), jnp.concatenate([dvl, dvc], axis=0)
    dq2, dnope_q, drope_q = _q_post_bwd(dqf, q2, nope_g[0:1], rope_g[0:1], ctab, stab, tm, "l2_q_post_b")
    dkv2, dkrr, dnope_k = _k_post_bwd(dkf, dvf, kv2, nope_g[1:2], tm, "l2_k_post_b")
    dcqn = _mm_nt(dq2, w_uq_p, "l2_uq_bx")
    g_uq_p = _mm_tn(cqn, dq2, "l2_uq_bw")
    dckvn = _mm_nt(dkv2, w['ml_w_ukv'], "l2_ukv_bx")
    g['ml_w_ukv'] = _mm_tn(ckvn, dkv2, "l2_ukv_bw")
    dzq, g['ml_q_norm'] = _q_pre_bwd(dcqn, zq, w['ml_q_norm'], tm, "l2_q_pre_b")
    dzkv, g['ml_kv_norm'], drope_k = _kv_pre_bwd(dckvn, dkrr, zkv, w['ml_kv_norm'], rope_g[1:2], ctab, stab, tm,
                                                  "l2_kv_pre_b")
    dz2 = jnp.concatenate([dzkv, jnp.pad(jnp.concatenate([dzq, dzg], axis=1), ((0, ta - tl), (0, 0)))], axis=1)
    dh2 = _mm_nt(dz2, w_in_p, "l2_in_bx")
    g_in_p = _mm_tn(h2, dz2, "l2_in_bw")
    g['ml_w_in'] = jnp.concatenate([g_in_p[:, :kvc], g_in_p[:, kvw:]], axis=1)
    g['ml_w_uq'] = g_uq_p.reshape(MLA_Q_RANK, MLA_HEADS, HEAD_W)[:, :, :MLA_NOPE + MLA_ROPE].reshape(
        MLA_Q_RANK, MLA_HEADS * (MLA_NOPE + MLA_ROPE))
    g['ml_nope_norm'] = jnp.concatenate([dnope_q, dnope_k], axis=0)
    g['ml_rope_norm'] = jnp.concatenate([drope_q, drope_k], axis=0)[:, :MLA_ROPE]
    dx3a = jnp.pad(dx3, ((0, ta - tl), (0, 0)))
    dxa2, dng2, dsc2, dsh2 = _rms_mod_bwd(xa2, norm_g[2:3], sc2, sh2, dh2, dx3a, segs_a, "l2_norm_b")

    do1, dgt1 = _resid_bwd(dxa2, o1, gt1, segs_a, "l1_resid_b")
    dy2_1 = _mm_nt(do1, w['pl_w_out'], "l1_out_bx")
    g['pl_w_out'] = _mm_tn(y2_1, do1, "l1_out_bw")
    dpm, dgz1, g['pl_scale'], g['pl_w_grp'] = _pool2_bwd(dy2_1, pm1, w['pl_w_grp'], w['pl_scale'], z1, tm,
                                                          "l1_group_b")
    dv1 = jnp.concatenate([_pool1(dpm, taps_l, ic_l, nb, l, 0, True, "l1_pool_lat_b", F32),
                           _pool1(dpm, taps_c, ic_c, nb, lc, tl, True, "l1_pool_ctx_b", F32)], axis=0)
    dz1 = jnp.concatenate([dv1, dgz1], axis=1)
    dh1 = _mm_nt(dz1, w['pl_w_in'], "l1_in_bx")
    g['pl_w_in'] = _mm_tn(h1, dz1, "l1_in_bw")
    dxa1, dng1, dsc1, dsh1 = _rms_mod_bwd(xa1, norm_g[1:2], sc1, sh1, dh1, dxa2, segs_a, "l1_norm_b")

    do0, dgt0 = _resid_bwd(dxa1, o0, gt0, segs_a, "l0_resid_b")
    dy4 = _mm_nt(do0, w['cv_w_out'], "l0_out_bx")
    g['cv_w_out'] = _mm_tn(y4_0, do0, "l0_out_bw")
    dy2, dgz0, g['cv_ln_g'], g['cv_ln_b'] = _conv2_bwd(dy4, y2_0, z0, w['cv_ln_g'], w['cv_ln_b'], tm, "l0_gate_b")
    da_l, db_l, ddw, ddb = _conv1_bwd(dy2, z0, w['cv_dw'], jnp.zeros((CONV_WIDTH, e), F32), jnp.zeros((1, e), F32),
                                      nb, l, 0, "l0_conv_lat_b")
    da_c, db_c, g['cv_dw'], g['cv_db'] = _conv1_bwd(dy2, z0, w['cv_dw'], ddw, ddb, nb, lc, tl, "l0_conv_ctx_b")
    dz0 = jnp.concatenate([jnp.concatenate([da_l, da_c], axis=0), jnp.concatenate([db_l, db_c], axis=0), dgz0],
                          axis=1)
    dh0 = _mm_nt(dz0, w['cv_w_in'], "l0_in_bx")
    g['cv_w_in'] = _mm_tn(h0, dz0, "l0_in_bw")
    dxa0, dng0, dsc0, dsh0 = _rms_mod_bwd(xa0, norm_g[0:1], sc0, sh0, dh0, dxa1, segs_a, "l0_norm_b")

    def rows4(t):
        return jnp.pad(t[:, 0], ((0, 2 * nb - t.shape[0]), (0, 0)))

    dmods = jnp.stack([
        jnp.concatenate([rows4(dsh0), rows4(dsc0), rows4(dgt0)], axis=1),
        jnp.concatenate([rows4(dsh1), rows4(dsc1), rows4(dgt1)], axis=1),
        jnp.concatenate([rows4(dsh2), rows4(dsc2), rows4(dgt2)], axis=1),
        jnp.concatenate([rows4(dsh3), rows4(dsc3), rows4(dgt3)], axis=1)])
    dnorm_g = jnp.concatenate([dng0, dng1, dng2, dng3], axis=0)
    return loss_vec, dxa0[:tl].reshape(nb, l, d), g, dmods, dnorm_g


def _exchange(x, all_to_all, name):
    block = x.shape[1:] if all_to_all else x.shape

    def body(x_ref, o_ref, send_sems, recv_sems, local_sem):
        pos = (lax.axis_index("x"), lax.axis_index("y"), lax.axis_index("c"))
        me = 4 * pos[0] + 2 * pos[1] + pos[2]

        def peer(k):
            p = tuple(1 - v if k & bit else v for v, bit in zip(pos, (4, 2, 1)))
            return p, 4 * p[0] + 2 * p[1] + p[2]

        def copy(k, landing):
            dev, idx = peer(k)
            return pltpu.make_async_remote_copy(
                src_ref=x_ref.at[idx] if all_to_all else x_ref, dst_ref=o_ref.at[idx if landing else me],
                send_sem=send_sems.at[k - 1], recv_sem=recv_sems.at[k - 1], device_id=dev,
                device_id_type=pl.DeviceIdType.MESH)

        local = pltpu.make_async_copy(x_ref.at[me] if all_to_all else x_ref, o_ref.at[me], local_sem)
        local.start()
        sends = [copy(k, False) for k in range(1, N_DEV)]
        for s in sends:
            s.start()
        for k in range(1, N_DEV):
            copy(k, True).wait_recv()
        for s in sends:
            s.wait_send()
        local.wait()

    hbm = pl.BlockSpec(memory_space=pltpu.HBM)
    return pl.pallas_call(
        body, in_specs=[hbm], out_specs=hbm, out_shape=_sds((N_DEV,) + tuple(block), x.dtype),
        scratch_shapes=[pltpu.SemaphoreType.DMA((N_DEV - 1,)), pltpu.SemaphoreType.DMA((N_DEV - 1,)),
                        pltpu.SemaphoreType.DMA],
        compiler_params=pltpu.CompilerParams(has_side_effects=True), name=name)(x)


def _pack_rows(n):
    r = -(-n // PACK_COLS)
    return -(-r // 256) * 256 if r > 256 else -(-r // 16) * 16


def _pack(arrs, dtype):
    flat = jnp.concatenate([a.reshape(-1).astype(dtype) for a in arrs])
    rows = _pack_rows(flat.shape[0])
    return jnp.pad(flat, (0, rows * PACK_COLS - flat.shape[0])).reshape(rows, PACK_COLS)


def _pack_shards(arrs):
    flat = jnp.concatenate([a.astype(F32) for a in arrs], axis=1)
    rows = _pack_rows(flat.shape[1])
    return jnp.pad(flat, ((0, 0), (0, rows * PACK_COLS - flat.shape[1]))).reshape(N_DEV, rows, PACK_COLS)


def _unpack(packed, shapes, lead=()):
    flat = packed.reshape(tuple(lead) + (-1,))
    out, off = [], 0
    for s in shapes:
        n = 1
        for v in s:
            n *= v
        out.append(flat[..., off:off + n].reshape(tuple(lead) + tuple(s)))
        off += n
    return out


def _to_shards(full, ax):
    s = full.shape
    t = full.reshape(s[:ax] + (N_DEV, s[ax] // N_DEV) + s[ax + 1:])
    return jnp.moveaxis(t, ax, 0).reshape(N_DEV, -1)


def _from_shards(shards, local_shape, ax):
    t = jnp.moveaxis(shards.reshape((N_DEV,) + tuple(local_shape)), 0, ax)
    s = t.shape
    return t.reshape(s[:ax] + (s[ax] * s[ax + 1],) + s[ax + 2:])


def _mod_fwd(c_rows, w_mod, b_mod, name):
    nl, d, n = w_mod.shape
    r = c_rows.shape[0]

    def body(c_ref, w_ref, b_ref, o_ref):
        s = _silu(c_ref[...])
        for l in range(nl):
            o_ref[l] = _dot(s, w_ref[l]) + b_ref[l]

    return pl.pallas_call(body, out_shape=_sds((nl, r, n)),
                          compiler_params=pltpu.CompilerParams(vmem_limit_bytes=VMEM_LIMIT), name=name)(
        c_rows, w_mod, b_mod)


def _mod_bwd(c_rows, dcols, dall, w_mod, c_ctx, name):
    nl, d, n = w_mod.shape
    r = c_rows.shape[0]

    def body(c_ref, dc_ref, da_ref, w_ref, cc_ref, gw_ref, gb_ref, gc_ref):
        s = _silu(c_ref[...])
        ds = jnp.zeros((r, d), F32)
        for l in range(nl):
            gw_ref[l] = _dot_tn(s, dc_ref[l])
            gb_ref[l] = _rowsum(da_ref[l])
            ds = ds + _dot_nt(dc_ref[l], w_ref[l])
        row = lax.broadcasted_iota(jnp.int32, (r, d), 0)
        gc_ref[...] = _rowsum(jnp.where(row % 4 >= 2, ds, 0.0)) * _dsilu(cc_ref[...])

    return pl.pallas_call(body, out_shape=[_sds((nl, d, n)), _sds((nl, 1, 3 * d)), _sds((1, d))],
                          compiler_params=pltpu.CompilerParams(vmem_limit_bytes=VMEM_LIMIT), name=name)(
        c_rows, dcols, dall, w_mod, c_ctx)


def _adam(w, gparts, m, v, name):
    rows, cols = w.shape
    npart = gparts.shape[0]
    tr = _pick(rows, (256, 128, 64, 32, 16))
    c1, c2 = 1.0 - ADAM_B1 ** ADAM_STEP, 1.0 - ADAM_B2 ** ADAM_STEP

    def body(w_ref, g_ref, m_ref, v_ref, go_ref, d_ref, mo_ref, vo_ref):
        gsum = g_ref[0]
        for p in range(1, npart):
            gsum = gsum + g_ref[p]
        mn = ADAM_B1 * m_ref[...] + (1.0 - ADAM_B1) * gsum
        vn = ADAM_B2 * v_ref[...] + (1.0 - ADAM_B2) * (gsum * gsum)
        go_ref[...] = gsum
        mo_ref[...] = mn
        vo_ref[...] = vn
        d_ref[...] = -ADAM_LR * ((mn / c1) / (jnp.sqrt(vn / c2) + ADAM_EPS) + ADAM_WD * w_ref[...])

    spec = _row(tr, cols)
    return pl.pallas_call(
        body, grid=(rows // tr,),
        in_specs=[spec, pl.BlockSpec((npart, tr, cols), lambda i: (0, i, 0)), spec, spec],
        out_specs=[spec] * 4, out_shape=[_sds((rows, cols))] * 4, compiler_params=_params("parallel"),
        name=name)(w, gparts, m, v)


INPUTS = ['x', 'c', 'ctx'] + WEIGHTS + ['loss_target'] + ['m_' + n for n in WEIGHTS] + ['v_' + n for n in WEIGHTS]
AXES = ("x", "y", "c")


def _squeeze_layer(name, a):
    return a if name == 'norm_g' or a.ndim < 3 else a[0]


def _train_step(a):
    x, c, ctx, tgt = a['x'], a['c'], a['ctx'], a['loss_target']
    d = x.shape[-1]
    nb = x.shape[0]
    dev = 4 * lax.axis_index("x") + 2 * lax.axis_index("y") + lax.axis_index("c")
    local_shape = {n: a[n].shape for n in WEIGHTS}

    vec_names = ['c'] + VECTOR_WEIGHTS
    vec_all = _exchange(_pack([a[n] for n in vec_names], F32), False, "gather_vectors")
    vec_parts = dict(zip(vec_names, _unpack(vec_all, [a[n].shape for n in vec_names], lead=(N_DEV,))))
    mat_all = _exchange(_pack([a[n] for n in MATMUL_WEIGHTS], BF16), False, "gather_matrices")
    mat_parts = dict(zip(MATMUL_WEIGHTS, _unpack(mat_all, [a[n].shape for n in MATMUL_WEIGHTS], lead=(N_DEV,))))
    w = {}
    for n in WEIGHTS:
        if n in mat_parts or n in vec_parts:
            part = mat_parts[n] if n in mat_parts else vec_parts[n]
            w[n] = _squeeze_layer(n, _from_shards(part, local_shape[n], SHARD_AXIS[n]))
        elif SHARD_AXIS[n] is None:
            w[n] = _squeeze_layer(n, a[n])
    c_all = vec_parts['c'].reshape(N_DEV * nb, d)
    c_ctx = a['c_ctx'].reshape(1, d)

    w_mod = a['w_mod']
    nl, ncol = w_mod.shape[0], w_mod.shape[2]
    mod_rows = -(-(N_DEV * nb + 1) // 8) * 8
    c_rows = jnp.concatenate([c_all, c_ctx, jnp.zeros((mod_rows - N_DEV * nb - 1, d), F32)], axis=0)
    b_loc = lax.dynamic_slice(a['b_mod'], (0, dev * ncol), (nl, ncol))[:, None, :]
    mod_loc = _mod_fwd(c_rows, w_mod, b_loc, "mod_fwd")
    mod_all = _exchange(mod_loc.reshape(nl * mod_rows, ncol), False, "gather_mods")
    mod_all = mod_all.reshape(N_DEV, nl, mod_rows, ncol).transpose(1, 2, 0, 3).reshape(nl, mod_rows, N_DEV * ncol)
    ctx_row = mod_all[:, N_DEV * nb:N_DEV * nb + 1]
    mods = jnp.concatenate([lax.dynamic_slice(mod_all, (0, dev * nb, 0), (nl, nb, 3 * d))] + [ctx_row] * nb, axis=1)

    loss_vec, grad_x, g, dmods, dnorm_g = _local_step(x, ctx, tgt, w, mods)
    loss = lax.psum(jnp.sum(loss_vec), AXES)

    nseg = dmods.shape[1]
    dm_all = _exchange(dmods.reshape(nl * nseg, 3 * d), False, "gather_dmods")
    dm_all = dm_all.reshape(N_DEV, nl, nseg, 3 * d).transpose(1, 0, 2, 3).reshape(nl, N_DEV * nseg, 3 * d)
    dcols = lax.dynamic_slice(dm_all, (0, 0, dev * ncol), (nl, N_DEV * nseg, ncol))
    c_rows_b = jnp.concatenate([c_all.reshape(N_DEV, nb, d), jnp.broadcast_to(c_ctx, (N_DEV, nb, d))], axis=1)
    g_w_mod, g_b_mod, g_c_ctx = _mod_bwd(c_rows_b.reshape(N_DEV * nseg, d), dcols, dm_all, w_mod, c_ctx, "mod_bwd")

    g['c_ctx'], g['norm_g'] = g_c_ctx, dnorm_g
    rep_all = _exchange(_pack([g[n] for n in REPLICATED], F32), False, "gather_replicated_grads")
    whole = {n: tuple(N_DEV * s if i == SHARD_AXIS[n] else s for i, s in enumerate(local_shape[n])) for n in EXCHANGED}
    big = _pack_shards([_to_shards(g[n].reshape(whole[n]), SHARD_AXIS[n]) for n in EXCHANGED])
    big_all = _exchange(big, True, "scatter_sharded_grads")

    out = {}

    def update(names, gparts, tag):
        res = _adam(_pack([a[n] for n in names], F32), gparts, _pack([a['m_' + n] for n in names], F32),
                    _pack([a['v_' + n] for n in names], F32), "adam_" + tag)
        shapes = [local_shape[n] for n in names]
        for kind, packed in zip(('grad_', 'delta_', 'new_m_', 'new_v_'), res):
            for n, leaf in zip(names, _unpack(packed, shapes)):
                out[kind + n] = leaf

    update(EXCHANGED, big_all, "sharded")
    update(REPLICATED, rep_all, "replicated")
    update(['w_mod', 'b_mod'], _pack([g_w_mod, g_b_mod], F32)[None], "modulation")
    return (loss, grad_x) + tuple(out[kind + n] for kind in ('grad_', 'delta_', 'new_m_', 'new_v_') for n in WEIGHTS)


def kernel(x, c, ctx, c_ctx, norm_g, w_mod, b_mod, cv_w_in, cv_dw, cv_db, cv_ln_g, cv_ln_b, cv_w_out, pl_w_in, pl_w_grp, pl_scale, pl_w_out, ml_w_in, ml_q_norm, ml_kv_norm, ml_w_uq, ml_w_ukv, ml_nope_norm, ml_rope_norm, ml_w_out, ch_w_in, ch_ln_g, ch_ln_b, ch_w_s, ch_b_s, ch_w_out, loss_target, m_c_ctx, m_norm_g, m_w_mod, m_b_mod, m_cv_w_in, m_cv_dw, m_cv_db, m_cv_ln_g, m_cv_ln_b, m_cv_w_out, m_pl_w_in, m_pl_w_grp, m_pl_scale, m_pl_w_out, m_ml_w_in, m_ml_q_norm, m_ml_kv_norm, m_ml_w_uq, m_ml_w_ukv, m_ml_nope_norm, m_ml_rope_norm, m_ml_w_out, m_ch_w_in, m_ch_ln_g, m_ch_ln_b, m_ch_w_s, m_ch_b_s, m_ch_w_out, v_c_ctx, v_norm_g, v_w_mod, v_b_mod, v_cv_w_in, v_cv_dw, v_cv_db, v_cv_ln_g, v_cv_ln_b, v_cv_w_out, v_pl_w_in, v_pl_w_grp, v_pl_scale, v_pl_w_out, v_ml_w_in, v_ml_q_norm, v_ml_kv_norm, v_ml_w_uq, v_ml_w_ukv, v_ml_nope_norm, v_ml_rope_norm, v_ml_w_out, v_ch_w_in, v_ch_ln_g, v_ch_ln_b, v_ch_w_s, v_ch_b_s, v_ch_w_out):
    return _train_step(dict(zip(INPUTS, (x, c, ctx, c_ctx, norm_g, w_mod, b_mod, cv_w_in, cv_dw, cv_db, cv_ln_g, cv_ln_b, cv_w_out, pl_w_in, pl_w_grp, pl_scale, pl_w_out, ml_w_in, ml_q_norm, ml_kv_norm, ml_w_uq, ml_w_ukv, ml_nope_norm, ml_rope_norm, ml_w_out, ch_w_in, ch_ln_g, ch_ln_b, ch_w_s, ch_b_s, ch_w_out, loss_target, m_c_ctx, m_norm_g, m_w_mod, m_b_mod, m_cv_w_in, m_cv_dw, m_cv_db, m_cv_ln_g, m_cv_ln_b, m_cv_w_out, m_pl_w_in, m_pl_w_grp, m_pl_scale, m_pl_w_out, m_ml_w_in, m_ml_q_norm, m_ml_kv_norm, m_ml_w_uq, m_ml_w_ukv, m_ml_nope_norm, m_ml_rope_norm, m_ml_w_out, m_ch_w_in, m_ch_ln_g, m_ch_ln_b, m_ch_w_s, m_ch_b_s, m_ch_w_out, v_c_ctx, v_norm_g, v_w_mod, v_b_mod, v_cv_w_in, v_cv_dw, v_cv_db, v_cv_ln_g, v_cv_ln_b, v_cv_w_out, v_pl_w_in, v_pl_w_grp, v_pl_scale, v_pl_w_out, v_ml_w_in, v_ml_q_norm, v_ml_kv_norm, v_ml_w_uq, v_ml_w_ukv, v_ml_nope_norm, v_ml_rope_norm, v_ml_w_out, v_ch_w_in, v_ch_ln_g, v_ch_ln_b, v_ch_w_s, v_ch_b_s, v_ch_w_out))))
```

```python
import functools

import jax
import jax.numpy as jnp
from jax import lax
from jax.experimental import pallas as pl
from jax.experimental.pallas import tpu as pltpu

F32 = jnp.float32
BF16 = jnp.bfloat16

N_DEV = 8
EPS = 1e-6
CONV_WIDTH = 31
CONV_PAD = 16
POOL_WINDOWS = (2, 4, 8, 16)
POOL_TAPS = 16
MLA_HEADS = 8
MLA_NOPE = 128
MLA_ROPE = 64
MLA_Q_RANK = 384
MLA_KV_RANK = 256
MLA_SCALE = (MLA_NOPE + MLA_ROPE) ** -0.5
ROPE_THETA = 10000.0
GRID_W = 64
HEAD_W = 256
CHUNK = 128
CHUNK_GROUPS = 8
ADAM_LR = 0.001
ADAM_B1 = 0.9
ADAM_B2 = 0.999
ADAM_EPS = 1e-08
ADAM_WD = 0.01
ADAM_STEP = 10
LANES = 128
VMEM_LIMIT = 56 * 1024 * 1024
PACK_COLS = 1024

WEIGHTS = ['c_ctx', 'norm_g', 'w_mod', 'b_mod', 'cv_w_in', 'cv_dw', 'cv_db', 'cv_ln_g', 'cv_ln_b', 'cv_w_out',
           'pl_w_in', 'pl_w_grp', 'pl_scale', 'pl_w_out', 'ml_w_in', 'ml_q_norm', 'ml_kv_norm', 'ml_w_uq',
           'ml_w_ukv', 'ml_nope_norm', 'ml_rope_norm', 'ml_w_out', 'ch_w_in', 'ch_ln_g', 'ch_ln_b', 'ch_w_s',
           'ch_b_s', 'ch_w_out']
SHARD_AXIS = {'c_ctx': None, 'norm_g': None, 'w_mod': 2, 'b_mod': None, 'cv_w_in': 2, 'cv_dw': 2, 'cv_db': None,
              'cv_ln_g': None, 'cv_ln_b': None, 'cv_w_out': 1, 'pl_w_in': 2, 'pl_w_grp': 2, 'pl_scale': 1,
              'pl_w_out': 1, 'ml_w_in': 2, 'ml_q_norm': 1, 'ml_kv_norm': 1, 'ml_w_uq': 2, 'ml_w_ukv': 2,
              'ml_nope_norm': None, 'ml_rope_norm': None, 'ml_w_out': 1, 'ch_w_in': 2, 'ch_ln_g': 1, 'ch_ln_b': 1,
              'ch_w_s': None, 'ch_b_s': None, 'ch_w_out': 1}
MATMUL_WEIGHTS = ['cv_w_in', 'cv_w_out', 'pl_w_in', 'pl_w_grp', 'pl_w_out', 'ml_w_in', 'ml_w_uq', 'ml_w_ukv',
                  'ml_w_out', 'ch_w_in', 'ch_w_out']
VECTOR_WEIGHTS = ['cv_dw', 'pl_scale', 'ml_q_norm', 'ml_kv_norm', 'ch_ln_g', 'ch_ln_b']
EXCHANGED = MATMUL_WEIGHTS[:1] + ['cv_dw'] + MATMUL_WEIGHTS[1:4] + ['pl_scale'] + MATMUL_WEIGHTS[4:6] + [
    'ml_q_norm', 'ml_kv_norm'] + MATMUL_WEIGHTS[6:10] + ['ch_ln_g', 'ch_ln_b', 'ch_w_out']
REPLICATED = ['c_ctx', 'norm_g', 'cv_db', 'cv_ln_g', 'cv_ln_b', 'ml_nope_norm', 'ml_rope_norm', 'ch_w_s', 'ch_b_s']


def _pick(n, cands):
    for c in cands:
        if n % c == 0:
            return c
    raise ValueError(f"no tile for {n} among {cands}")


def _params(*sem):
    return pltpu.CompilerParams(dimension_semantics=sem, vmem_limit_bytes=VMEM_LIMIT)


def _sig(x):
    return 1.0 / (1.0 + jnp.exp(-x))


def _silu(x):
    return x * _sig(x)


def _dsilu(x):
    s = _sig(x)
    return s * (1.0 + x * (1.0 - s))


def _rowsum(v):
    return jnp.sum(v, axis=0, keepdims=True)


def _dot(a, b):
    return jnp.dot(a.astype(BF16), b.astype(BF16), preferred_element_type=F32)


def _dot_nt(a, b):
    return lax.dot_general(a.astype(BF16), b.astype(BF16), (((1,), (1,)), ((), ())), preferred_element_type=F32)


def _dot_tn(a, b):
    return lax.dot_general(a.astype(BF16), b.astype(BF16), (((0,), (0,)), ((), ())), preferred_element_type=F32)


class _Segs:
    def __init__(self, lens, tm):
        self.lens, self.tm, self.n = tuple(lens), tm, len(lens)
        self.starts, s = [], 0
        for l in lens:
            assert l % tm == 0
            self.starts.append(s // tm)
            s += l
        self.rows, self.tiles = s, s // tm

    def seg(self, i):
        r = 0
        for st in self.starts[1:]:
            r = r + jnp.where(i >= st, 1, 0)
        return r

    def is_first(self, i):
        f = i == 0
        for st in self.starts[1:]:
            f = jnp.logical_or(f, i == st)
        return f

    def spec(self, cols):
        return pl.BlockSpec((None, 1, cols), lambda i: (self.seg(i), 0, 0))


def _row(tm, cols, cb=0):
    return pl.BlockSpec((tm, cols), lambda i: (i, cb))


def _const(shape):
    return pl.BlockSpec(shape, lambda *_: (0,) * len(shape))


def _sds(shape, dtype=F32):
    return jax.ShapeDtypeStruct(shape, dtype)


N_TILES = (1024, 896, 768, 512, 384, 256, 128)


def _mm(a, b, name, out_dtype=F32, rows=None):
    m, k, n = rows or a.shape[0], a.shape[1], b.shape[1]
    tm, tn = _pick(m, (512, 256, 128)), _pick(n, N_TILES)

    def body(a_ref, b_ref, o_ref):
        o_ref[...] = _dot(a_ref[...], b_ref[...]).astype(o_ref.dtype)

    return pl.pallas_call(
        body, grid=(n // tn, m // tm),
        in_specs=[pl.BlockSpec((tm, k), lambda j, i: (i, 0)), pl.BlockSpec((k, tn), lambda j, i: (0, j))],
        out_specs=pl.BlockSpec((tm, tn), lambda j, i: (i, j)), out_shape=_sds((m, n), out_dtype),
        compiler_params=_params("parallel", "parallel"), name=name)(a, b)


def _mm_nt(a, b, name, out_dtype=F32):
    m, k, n = a.shape[0], a.shape[1], b.shape[0]
    tm, tn = _pick(m, (512, 256, 128)), _pick(n, N_TILES)

    def body(a_ref, b_ref, o_ref):
        o_ref[...] = _dot_nt(a_ref[...], b_ref[...]).astype(o_ref.dtype)

    return pl.pallas_call(
        body, grid=(n // tn, m // tm),
        in_specs=[pl.BlockSpec((tm, k), lambda j, i: (i, 0)), pl.BlockSpec((tn, k), lambda j, i: (j, 0))],
        out_specs=pl.BlockSpec((tm, tn), lambda j, i: (i, j)), out_shape=_sds((m, n), out_dtype),
        compiler_params=_params("parallel", "parallel"), name=name)(a, b)


def _mm_tn(a, b, name, rows=None, shards=None):
    t, k, n = rows or a.shape[0], a.shape[1], b.shape[1]
    tk, tt = _pick(k, N_TILES), _pick(t, (512, 256, 128))
    tn = n // shards if shards else _pick(n, N_TILES)
    assert tn % LANES == 0

    def body(a_ref, b_ref, o_ref, acc_ref):
        @pl.when(pl.program_id(2) == 0)
        def _():
            acc_ref[...] = jnp.zeros_like(acc_ref)

        acc_ref[...] += _dot_tn(a_ref[...], b_ref[...])

        @pl.when(pl.program_id(2) == pl.num_programs(2) - 1)
        def _():
            o_ref[...] = acc_ref[...].astype(o_ref.dtype)

    if shards:
        out_spec, out_shape = pl.BlockSpec((None, tk, tn), lambda i, j, s: (j, i, 0)), _sds((shards, k, tn), BF16)
    else:
        out_spec, out_shape = pl.BlockSpec((tk, tn), lambda i, j, s: (i, j)), _sds((k, n))
    return pl.pallas_call(
        body, grid=(k // tk, n // tn, t // tt),
        in_specs=[pl.BlockSpec((tt, tk), lambda i, j, s: (s, i)), pl.BlockSpec((tt, tn), lambda i, j, s: (s, j))],
        out_specs=out_spec, out_shape=out_shape, scratch_shapes=[pltpu.VMEM((tk, tn), F32)],
        compiler_params=_params("parallel", "parallel", "arbitrary"), name=name)(a, b)


def _rms_mod_fwd(x, g, sc, sh, segs, name):
    d, tm = x.shape[1], segs.tm

    def body(x_ref, g_ref, sc_ref, sh_ref, h_ref):
        xf = x_ref[...]
        r = lax.rsqrt(jnp.mean(xf * xf, axis=-1, keepdims=True) + EPS)
        h_ref[...] = ((xf * r * g_ref[...]) * (1.0 + sc_ref[...]) + sh_ref[...]).astype(h_ref.dtype)

    return pl.pallas_call(
        body, grid=(segs.tiles,), in_specs=[_row(tm, d), _const((1, d)), segs.spec(d), segs.spec(d)],
        out_specs=_row(tm, d), out_shape=_sds((segs.rows, d), BF16), compiler_params=_params("parallel"),
        name=name)(x, g, sc, sh)


def _rms_mod_bwd(x, g, sc, sh, dh, dxr, segs, name):
    d, tm = x.shape[1], segs.tm

    def body(x_ref, g_ref, sc_ref, sh_ref, dh_ref, dxr_ref, dx_ref, dg_ref, dsc_ref, dsh_ref):
        i = pl.program_id(0)

        @pl.when(i == 0)
        def _():
            dg_ref[...] = jnp.zeros_like(dg_ref)

        @pl.when(segs.is_first(i))
        def _():
            dsc_ref[...] = jnp.zeros_like(dsc_ref)
            dsh_ref[...] = jnp.zeros_like(dsh_ref)

        xf, gg, dhf = x_ref[...], g_ref[...], dh_ref[...].astype(F32)
        r = lax.rsqrt(jnp.mean(xf * xf, axis=-1, keepdims=True) + EPS)
        xh = xf * r
        dsh_ref[...] += _rowsum(dhf)
        dsc_ref[...] += _rowsum(dhf * (xh * gg))
        du = dhf * (1.0 + sc_ref[...])
        dg_ref[...] += _rowsum(du * xh)
        dxh = du * gg
        dx_ref[...] = dxr_ref[...] + r * (dxh - xh * jnp.mean(dxh * xh, axis=-1, keepdims=True))

    return pl.pallas_call(
        body, grid=(segs.tiles,),
        in_specs=[_row(tm, d), _const((1, d)), segs.spec(d), segs.spec(d), _row(tm, d), _row(tm, d)],
        out_specs=[_row(tm, d), _const((1, d)), segs.spec(d), segs.spec(d)],
        out_shape=[_sds((segs.rows, d)), _sds((1, d)), _sds((segs.n, 1, d)), _sds((segs.n, 1, d))],
        compiler_params=_params("arbitrary"), name=name)(x, g, sc, sh, dh, dxr)


def _resid_fwd(x, o, gt, segs, name):
    d, tm = x.shape[1], segs.tm

    def body(x_ref, o_ref, gt_ref, y_ref):
        y_ref[...] = x_ref[...] + gt_ref[...] * o_ref[...]

    return pl.pallas_call(
        body, grid=(segs.tiles,), in_specs=[_row(tm, d), _row(tm, d), segs.spec(d)], out_specs=_row(tm, d),
        out_shape=_sds((segs.rows, d)), compiler_params=_params("parallel"), name=name)(x, o, gt)


def _resid_bwd(dxn, o, gt, segs, name):
    d, tm = o.shape[1], segs.tm

    def body(dxn_ref, o_ref, gt_ref, do_ref, dgt_ref):
        @pl.when(segs.is_first(pl.program_id(0)))
        def _():
            dgt_ref[...] = jnp.zeros_like(dgt_ref)

        dx = dxn_ref[...]
        do_ref[...] = (gt_ref[...] * dx).astype(do_ref.dtype)
        dgt_ref[...] += _rowsum(dx * o_ref[...])

    return pl.pallas_call(
        body, grid=(segs.tiles,), in_specs=[_row(tm, d), _row(tm, d), segs.spec(d)],
        out_specs=[_row(tm, d), segs.spec(d)], out_shape=[_sds((segs.rows, d), BF16), _sds((segs.n, 1, d))],
        compiler_params=_params("arbitrary"), name=name)(dxn, o, gt)


def _loss_head(y, tgt, tm, name):
    t, d = y.shape

    def body(y_ref, t_ref, l_ref, dy_ref):
        @pl.when(pl.program_id(0) == 0)
        def _():
            l_ref[...] = jnp.zeros_like(l_ref)

        e = y_ref[...] - t_ref[...]
        dy_ref[...] = e * (1.0 / d)
        l_ref[...] += _rowsum(e * e) * (0.5 / d)

    return pl.pallas_call(
        body, grid=(t // tm,), in_specs=[_row(tm, d), _row(tm, d)], out_specs=[_const((1, d)), _row(tm, d)],
        out_shape=[_sds((1, d)), _sds((t, d))], compiler_params=_params("arbitrary"), name=name)(y, tgt)


def _seq_spec(l, ce, row0, cb0=0):
    return pl.BlockSpec((l, ce), lambda j, s: (row0 // l + s, cb0 + j))


def _tap_sum(pad_ref, taps_ref, first_row, n_taps, l, ce, flip):
    out = []
    for r0 in range(0, l, CHUNK):
        rows = min(CHUNK, l - r0)
        acc = jnp.zeros((rows, ce), F32)
        for k in range(n_taps):
            kk = n_taps - 1 - k if flip else k
            acc = acc + pad_ref[pl.ds(first_row + r0 + k, rows), :] * taps_ref[kk:kk + 1, :]
        out.append(acc)
    return out


def _fill_pad(pad_ref, val, l, ce):
    pad_ref[pl.ds(0, CONV_PAD), :] = jnp.zeros((CONV_PAD, ce), F32)
    pad_ref[pl.ds(CONV_PAD + l, CONV_PAD), :] = jnp.zeros((CONV_PAD, ce), F32)
    pad_ref[pl.ds(CONV_PAD, l), :] = val


def _conv1_fwd(z, dw, db, nseq, l, row0, name):
    e = z.shape[1] // 3
    ce = LANES
    half = CONV_WIDTH // 2

    def body(a_ref, b_ref, dw_ref, db_ref, y_ref, pad_ref):
        _fill_pad(pad_ref, a_ref[...] * _sig(b_ref[...]), l, ce)
        pieces = _tap_sum(pad_ref, dw_ref, CONV_PAD - half, CONV_WIDTH, l, ce, False)
        for n, acc in enumerate(pieces):
            y_ref[pl.ds(n * CHUNK, acc.shape[0]), :] = acc + db_ref[...]

    return pl.pallas_call(
        body, grid=(e // ce, nseq),
        in_specs=[_seq_spec(l, ce, row0), _seq_spec(l, ce, row0, e // ce),
                  pl.BlockSpec((CONV_WIDTH, ce), lambda j, s: (0, j)), pl.BlockSpec((1, ce), lambda j, s: (0, j))],
        out_specs=pl.BlockSpec((l, ce), lambda j, s: (s, j)), out_shape=_sds((nseq * l, e)),
        scratch_shapes=[pltpu.VMEM((l + 2 * CONV_PAD, ce), F32)],
        compiler_params=_params("parallel", "arbitrary"), name=name)(z, z, dw, db)


def _conv1_bwd(dy2, z, dw, acc_dw, acc_db, nseq, l, row0, name):
    e = z.shape[1] // 3
    ce = LANES
    half = CONV_WIDTH // 2

    def body(dy_ref, a_ref, b_ref, dw_ref, adw_ref, adb_ref, da_ref, dbb_ref, ddw_ref, ddb_ref, ypad_ref, dpad_ref):
        @pl.when(pl.program_id(1) == 0)
        def _():
            ddw_ref[...] = adw_ref[...]
            ddb_ref[...] = adb_ref[...]

        a, sb = a_ref[...], _sig(b_ref[...])
        dy = dy_ref[...]
        _fill_pad(ypad_ref, a * sb, l, ce)
        _fill_pad(dpad_ref, dy, l, ce)
        ddb_ref[...] += _rowsum(dy)
        for k in range(CONV_WIDTH):
            ddw_ref[k:k + 1, :] += _rowsum(dy * ypad_ref[pl.ds(CONV_PAD - half + k, l), :])
        pieces = _tap_sum(dpad_ref, dw_ref, CONV_PAD - half, CONV_WIDTH, l, ce, True)
        for n, dy1 in enumerate(pieces):
            rows = pl.ds(n * CHUNK, dy1.shape[0])
            sbn = sb[n * CHUNK:n * CHUNK + dy1.shape[0], :]
            da_ref[rows, :] = (dy1 * sbn).astype(da_ref.dtype)
            dbb_ref[rows, :] = (dy1 * a[n * CHUNK:n * CHUNK + dy1.shape[0], :] * sbn * (1.0 - sbn)).astype(dbb_ref.dtype)

    cw = lambda j, s: (0, j)
    return pl.pallas_call(
        body, grid=(e // ce, nseq),
        in_specs=[_seq_spec(l, ce, row0), _seq_spec(l, ce, row0), _seq_spec(l, ce, row0, e // ce),
                  pl.BlockSpec((CONV_WIDTH, ce), cw), pl.BlockSpec((CONV_WIDTH, ce), cw), pl.BlockSpec((1, ce), cw)],
        out_specs=[pl.BlockSpec((l, ce), lambda j, s: (s, j)), pl.BlockSpec((l, ce), lambda j, s: (s, j)),
                   pl.BlockSpec((CONV_WIDTH, ce), cw), pl.BlockSpec((1, ce), cw)],
        out_shape=[_sds((nseq * l, e), BF16), _sds((nseq * l, e), BF16), _sds((CONV_WIDTH, e)), _sds((1, e))],
        scratch_shapes=[pltpu.VMEM((l + 2 * CONV_PAD, ce), F32), pltpu.VMEM((l + 2 * CONV_PAD, ce), F32)],
        compiler_params=_params("parallel", "arbitrary"), name=name)(dy2, z, z, dw, acc_dw, acc_db)


def _pool_tables(l, e):
    grp = e // len(POOL_WINDOWS)
    w = jnp.repeat(jnp.array(POOL_WINDOWS, jnp.int32), grp)[None, :]
    off = jnp.arange(POOL_TAPS, dtype=jnp.int32)[:, None] - POOL_TAPS // 2
    taps = jnp.logical_and(off >= -(w // 2), off < w - w // 2).astype(F32)
    t = jnp.arange(l, dtype=jnp.int32)[:, None]
    cnt = jnp.clip(t + (w - w // 2), 0, l) - jnp.clip(t - w // 2, 0, l)
    return taps, 1.0 / cnt.astype(F32)


def _pool1(v_src, taps, inv_cnt, nseq, l, row0, transpose, name, out_dtype):
    e = taps.shape[1]
    ce = LANES
    half = POOL_TAPS // 2

    def body(v_ref, taps_ref, ic_ref, o_ref, pad_ref):
        v = v_ref[...].astype(F32)
        if transpose:
            _fill_pad(pad_ref, v * ic_ref[...], l, ce)
            pieces = _tap_sum(pad_ref, taps_ref, CONV_PAD - half + 1, POOL_TAPS, l, ce, True)
        else:
            _fill_pad(pad_ref, v, l, ce)
            pieces = _tap_sum(pad_ref, taps_ref, CONV_PAD - half, POOL_TAPS, l, ce, False)
        for n, acc in enumerate(pieces):
            rows = pl.ds(n * CHUNK, acc.shape[0])
            vn = v[n * CHUNK:n * CHUNK + acc.shape[0], :]
            if transpose:
                o_ref[rows, :] = (acc - vn).astype(o_ref.dtype)
            else:
                o_ref[rows, :] = (acc * ic_ref[rows, :] - vn).astype(o_ref.dtype)

    return pl.pallas_call(
        body, grid=(e // ce, nseq),
        in_specs=[_seq_spec(l, ce, row0), pl.BlockSpec((POOL_TAPS, ce), lambda j, s: (0, j)),
                  pl.BlockSpec((l, ce), lambda j, s: (0, j))],
        out_specs=pl.BlockSpec((l, ce), lambda j, s: (s, j)), out_shape=_sds((nseq * l, e), out_dtype),
        scratch_shapes=[pltpu.VMEM((l + 2 * CONV_PAD, ce), F32)],
        compiler_params=_params("parallel", "arbitrary"), name=name)(v_src, taps, inv_cnt)


def _layernorm_parts(x, eps=EPS):
    mu = jnp.mean(x, axis=-1, keepdims=True)
    xc = x - mu
    r = lax.rsqrt(jnp.mean(xc * xc, axis=-1, keepdims=True) + eps)
    return xc * r, r


def _layernorm_bwd(dy, xh, r, g):
    dxh = dy * g
    return r * (dxh - jnp.mean(dxh, axis=-1, keepdims=True) - xh * jnp.mean(dxh * xh, axis=-1, keepdims=True))


def _conv2_fwd(y2, z, ln_g, ln_b, tm, name):
    t, e = y2.shape

    def body(y_ref, g_ref, lg_ref, lb_ref, o_ref):
        xh, _ = _layernorm_parts(y_ref[...])
        o_ref[...] = (_silu(xh * lg_ref[...] + lb_ref[...]) * _silu(g_ref[...])).astype(o_ref.dtype)

    return pl.pallas_call(
        body, grid=(t // tm,), in_specs=[_row(tm, e), _row(tm, e, 2), _const((1, e)), _const((1, e))],
        out_specs=_row(tm, e), out_shape=_sds((t, e), BF16), compiler_params=_params("parallel"),
        name=name)(y2, z, ln_g, ln_b)


def _conv2_bwd(dy4, y2, z, ln_g, ln_b, tm, name):
    t, e = y2.shape

    def body(dy_ref, y_ref, g_ref, lg_ref, lb_ref, dy2_ref, dg_ref, dlg_ref, dlb_ref):
        @pl.when(pl.program_id(0) == 0)
        def _():
            dlg_ref[...] = jnp.zeros_like(dlg_ref)
            dlb_ref[...] = jnp.zeros_like(dlb_ref)

        dy, gz = dy_ref[...], g_ref[...]
        xh, r = _layernorm_parts(y_ref[...])
        y3 = xh * lg_ref[...] + lb_ref[...]
        dg_ref[...] = (dy * _silu(y3) * _dsilu(gz)).astype(dg_ref.dtype)
        dy3 = dy * _silu(gz) * _dsilu(y3)
        dlg_ref[...] += _rowsum(dy3 * xh)
        dlb_ref[...] += _rowsum(dy3)
        dy2_ref[...] = _layernorm_bwd(dy3, xh, r, lg_ref[...])

    return pl.pallas_call(
        body, grid=(t // tm,),
        in_specs=[_row(tm, e), _row(tm, e), _row(tm, e, 2), _const((1, e)), _const((1, e))],
        out_specs=[_row(tm, e), _row(tm, e), _const((1, e)), _const((1, e))],
        out_shape=[_sds((t, e)), _sds((t, e), BF16), _sds((1, e)), _sds((1, e))],
        compiler_params=_params("arbitrary"), name=name)(dy4, y2, z, ln_g, ln_b)


def _pool2_fwd(pm, w_grp, scale, z, tm, name):
    t, e = pm.shape
    ng, gw = w_grp.shape[0], w_grp.shape[1]

    def body(pm_ref, w_ref, sc_ref, g_ref, o_ref):
        for k in range(ng):
            cols = slice(k * gw, (k + 1) * gw)
            y = _dot(pm_ref[:, cols], w_ref[k])
            o_ref[:, cols] = (y * sc_ref[:, cols] * _silu(g_ref[:, cols])).astype(o_ref.dtype)

    return pl.pallas_call(
        body, grid=(t // tm,), in_specs=[_row(tm, e), _const(w_grp.shape), _const((1, e)), _row(tm, e, 1)],
        out_specs=_row(tm, e), out_shape=_sds((t, e), BF16), compiler_params=_params("parallel"),
        name=name)(pm, w_grp, scale, z)


def _pool2_bwd(dy2, pm, w_grp, scale, z, tm, name):
    t, e = pm.shape
    ng, gw = w_grp.shape[0], w_grp.shape[1]

    def body(dy_ref, pm_ref, w_ref, sc_ref, g_ref, dpm_ref, dg_ref, dsc_ref, dw_ref):
        @pl.when(pl.program_id(0) == 0)
        def _():
            dsc_ref[...] = jnp.zeros_like(dsc_ref)
            dw_ref[...] = jnp.zeros_like(dw_ref)

        for k in range(ng):
            cols = slice(k * gw, (k + 1) * gw)
            dy, gz, sc, pmk = dy_ref[:, cols], g_ref[:, cols], sc_ref[:, cols], pm_ref[:, cols]
            y = _dot(pmk, w_ref[k])
            dg_ref[:, cols] = (dy * (y * sc) * _dsilu(gz)).astype(dg_ref.dtype)
            dys = dy * _silu(gz)
            dsc_ref[:, cols] += _rowsum(dys * y)
            dyk = dys * sc
            dpm_ref[:, cols] = _dot_nt(dyk, w_ref[k])
            dw_ref[k] += _dot_tn(pmk, dyk)

    return pl.pallas_call(
        body, grid=(t // tm,),
        in_specs=[_row(tm, e), _row(tm, e), _const(w_grp.shape), _const((1, e)), _row(tm, e, 1)],
        out_specs=[_row(tm, e), _row(tm, e), _const((1, e)), _const(w_grp.shape)],
        out_shape=[_sds((t, e)), _sds((t, e), BF16), _sds((1, e)), _sds(w_grp.shape)],
        compiler_params=_params("arbitrary"), name=name)(dy2, pm, w_grp, scale, z)


def _rms_f(x, g, n):
    r = lax.rsqrt(jnp.sum(x * x, axis=-1, keepdims=True) * (1.0 / n) + EPS)
    return x * r * g


def _rms_b(x, g, dy, n):
    r = lax.rsqrt(jnp.sum(x * x, axis=-1, keepdims=True) * (1.0 / n) + EPS)
    xh = x * r
    dxh = dy * g
    return r * (dxh - xh * (jnp.sum(dxh * xh, axis=-1, keepdims=True) * (1.0 / n))), dy * xh


def _swap16(x):
    lane = lax.broadcasted_iota(jnp.int32, x.shape, 1)
    return jnp.where(lane % 32 < 16, pltpu.roll(x, LANES - 16, 1), pltpu.roll(x, 16, 1))


def _rope(x, c, s):
    return x * c + _swap16(x) * s


def _rope_t(dy, c, s):
    return dy * c + _swap16(dy * s)


def _rope_tables(l, lc, nb):
    t = jnp.arange(l, dtype=jnp.int32)
    row_id, col_id = (t // GRID_W).astype(F32), (t % GRID_W).astype(F32)
    axis_dim = MLA_ROPE // 2
    freqs = ROPE_THETA ** (-jnp.arange(0, axis_dim, 2, dtype=F32) / axis_dim)
    ar, ac = row_id[:, None] * freqs, col_id[:, None] * freqs
    pad1, pad0 = jnp.ones((l, LANES - MLA_ROPE), F32), jnp.zeros((l, LANES - MLA_ROPE), F32)
    ctab = jnp.concatenate([jnp.cos(ar), jnp.cos(ar), jnp.cos(ac), jnp.cos(ac), pad1], axis=1)
    stab = jnp.concatenate([-jnp.sin(ar), jnp.sin(ar), -jnp.sin(ac), jnp.sin(ac), pad0], axis=1)
    ctab = jnp.concatenate([jnp.tile(ctab, (nb, 1)), jnp.ones((nb * lc, LANES), F32)], axis=0)
    stab = jnp.concatenate([jnp.tile(stab, (nb, 1)), jnp.zeros((nb * lc, LANES), F32)], axis=0)
    return ctab, stab


def _kv_pre_fwd(zkv, kv_norm, rope_g, ctab, stab, tm, name):
    t = zkv.shape[0]

    def body(z_ref, gk_ref, gr_ref, c_ref, s_ref, ck_ref, kr_ref):
        ck_ref[...] = _rms_f(z_ref[:, :MLA_KV_RANK], gk_ref[...], MLA_KV_RANK).astype(ck_ref.dtype)
        kr = _rms_f(z_ref[:, MLA_KV_RANK:], gr_ref[...], MLA_ROPE)
        kr_ref[...] = _rope(kr, c_ref[...], s_ref[...]).astype(kr_ref.dtype)

    w = MLA_KV_RANK + LANES
    return pl.pallas_call(
        body, grid=(t // tm,),
        in_specs=[_row(tm, w), _const((1, MLA_KV_RANK)), _const((1, LANES)), _row(tm, LANES), _row(tm, LANES)],
        out_specs=[_row(tm, MLA_KV_RANK), _row(tm, LANES)],
        out_shape=[_sds((t, MLA_KV_RANK), BF16), _sds((t, LANES), BF16)],
        compiler_params=_params("parallel"), name=name)(zkv, kv_norm, rope_g, ctab, stab)


def _kv_pre_bwd(dck, dkr, zkv, kv_norm, rope_g, ctab, stab, tm, name):
    t = zkv.shape[0]
    w = MLA_KV_RANK + LANES

    def body(dck_ref, dkr_ref, z_ref, gk_ref, gr_ref, c_ref, s_ref, dz_ref, dgk_ref, dgr_ref):
        @pl.when(pl.program_id(0) == 0)
        def _():
            dgk_ref[...] = jnp.zeros_like(dgk_ref)
            dgr_ref[...] = jnp.zeros_like(dgr_ref)

        dx, dg = _rms_b(z_ref[:, :MLA_KV_RANK], gk_ref[...], dck_ref[...], MLA_KV_RANK)
        dz_ref[:, :MLA_KV_RANK] = dx.astype(dz_ref.dtype)
        dgk_ref[...] += _rowsum(dg)
        dy = _rope_t(dkr_ref[...], c_ref[...], s_ref[...])
        dx, dg = _rms_b(z_ref[:, MLA_KV_RANK:], gr_ref[...], dy, MLA_ROPE)
        dz_ref[:, MLA_KV_RANK:] = dx.astype(dz_ref.dtype)
        dgr_ref[...] += _rowsum(dg)

    return pl.pallas_call(
        body, grid=(t // tm,),
        in_specs=[_row(tm, MLA_KV_RANK), _row(tm, LANES), _row(tm, w), _const((1, MLA_KV_RANK)), _const((1, LANES)),
                  _row(tm, LANES), _row(tm, LANES)],
        out_specs=[_row(tm, w), _const((1, MLA_KV_RANK)), _const((1, LANES))],
        out_shape=[_sds((t, w), BF16), _sds((1, MLA_KV_RANK)), _sds((1, LANES))],
        compiler_params=_params("arbitrary"), name=name)(dck, dkr, zkv, kv_norm, rope_g, ctab, stab)


def _q_pre_fwd(zq, q_norm, tm, name):
    t, w = zq.shape

    def body(z_ref, g_ref, o_ref):
        o_ref[...] = _rms_f(z_ref[...], g_ref[...], w).astype(o_ref.dtype)

    return pl.pallas_call(
        body, grid=(t // tm,), in_specs=[_row(tm, w), _const((1, w))], out_specs=_row(tm, w),
        out_shape=_sds((t, w), BF16), compiler_params=_params("parallel"), name=name)(zq, q_norm)


def _q_pre_bwd(dcq, zq, q_norm, tm, name):
    t, w = zq.shape

    def body(d_ref, z_ref, g_ref, dz_ref, dg_ref):
        @pl.when(pl.program_id(0) == 0)
        def _():
            dg_ref[...] = jnp.zeros_like(dg_ref)

        dx, dg = _rms_b(z_ref[...], g_ref[...], d_ref[...], w)
        dz_ref[...] = dx.astype(dz_ref.dtype)
        dg_ref[...] += _rowsum(dg)

    return pl.pallas_call(
        body, grid=(t // tm,), in_specs=[_row(tm, w), _row(tm, w), _const((1, w))],
        out_specs=[_row(tm, w), _const((1, w))], out_shape=[_sds((t, w), BF16), _sds((1, w))],
        compiler_params=_params("arbitrary"), name=name)(dcq, zq, q_norm)


def _q_post_fwd(q, nope_g, rope_g, ctab, stab, tm, name):
    t, w = q.shape

    def body(q_ref, gn_ref, gr_ref, c_ref, s_ref, o_ref):
        for h in range(MLA_HEADS):
            a = h * HEAD_W
            o_ref[:, a:a + LANES] = _rms_f(q_ref[:, a:a + LANES], gn_ref[...], MLA_NOPE).astype(o_ref.dtype)
            qr = _rms_f(q_ref[:, a + LANES:a + HEAD_W], gr_ref[...], MLA_ROPE)
            o_ref[:, a + LANES:a + HEAD_W] = _rope(qr, c_ref[...], s_ref[...]).astype(o_ref.dtype)

    return pl.pallas_call(
        body, grid=(t // tm,),
        in_specs=[_row(tm, w), _const((1, LANES)), _const((1, LANES)), _row(tm, LANES), _row(tm, LANES)],
        out_specs=_row(tm, w), out_shape=_sds((t, w), BF16), compiler_params=_params("parallel"),
        name=name)(q, nope_g, rope_g, ctab, stab)


def _q_post_bwd(dqf, q, nope_g, rope_g, ctab, stab, tm, name):
    t, w = q.shape

    def body(d_ref, q_ref, gn_ref, gr_ref, c_ref, s_ref, dq_ref, dgn_ref, dgr_ref):
        @pl.when(pl.program_id(0) == 0)
        def _():
            dgn_ref[...] = jnp.zeros_like(dgn_ref)
            dgr_ref[...] = jnp.zeros_like(dgr_ref)

        for h in range(MLA_HEADS):
            a = h * HEAD_W
            dx, dg = _rms_b(q_ref[:, a:a + LANES], gn_ref[...], d_ref[:, a:a + LANES], MLA_NOPE)
            dq_ref[:, a:a + LANES] = dx.astype(dq_ref.dtype)
            dgn_ref[...] += _rowsum(dg)
            dy = _rope_t(d_ref[:, a + LANES:a + HEAD_W], c_ref[...], s_ref[...])
            dx, dg = _rms_b(q_ref[:, a + LANES:a + HEAD_W], gr_ref[...], dy, MLA_ROPE)
            dq_ref[:, a + LANES:a + HEAD_W] = dx.astype(dq_ref.dtype)
            dgr_ref[...] += _rowsum(dg)

    return pl.pallas_call(
        body, grid=(t // tm,),
        in_specs=[_row(tm, w), _row(tm, w), _const((1, LANES)), _const((1, LANES)), _row(tm, LANES), _row(tm, LANES)],
        out_specs=[_row(tm, w), _const((1, LANES)), _const((1, LANES))],
        out_shape=[_sds((t, w), BF16), _sds((1, LANES)), _sds((1, LANES))],
        compiler_params=_params("arbitrary"), name=name)(dqf, q, nope_g, rope_g, ctab, stab)


def _k_post_fwd(kv, krr, nope_g, tm, name):
    t, w = kv.shape

    def body(kv_ref, kr_ref, gn_ref, k_ref, v_ref):
        for h in range(MLA_HEADS):
            a = h * HEAD_W
            k_ref[:, a:a + LANES] = _rms_f(kv_ref[:, a:a + LANES], gn_ref[...], MLA_NOPE).astype(k_ref.dtype)
            k_ref[:, a + LANES:a + HEAD_W] = kr_ref[...]
            v_ref[:, h * LANES:(h + 1) * LANES] = kv_ref[:, a + LANES:a + HEAD_W].astype(v_ref.dtype)

    return pl.pallas_call(
        body, grid=(t // tm,), in_specs=[_row(tm, w), _row(tm, LANES), _const((1, LANES))],
        out_specs=[_row(tm, w), _row(tm, w // 2)], out_shape=[_sds((t, w), BF16), _sds((t, w // 2), BF16)],
        compiler_params=_params("parallel"), name=name)(kv, krr, nope_g)


def _k_post_bwd(dkf, dvf, kv, nope_g, tm, name):
    t, w = kv.shape

    def body(dk_ref, dv_ref, kv_ref, gn_ref, dkv_ref, dkr_ref, dgn_ref):
        @pl.when(pl.program_id(0) == 0)
        def _():
            dgn_ref[...] = jnp.zeros_like(dgn_ref)

        dkr = jnp.zeros(dkr_ref.shape, F32)
        for h in range(MLA_HEADS):
            a = h * HEAD_W
            dx, dg = _rms_b(kv_ref[:, a:a + LANES], gn_ref[...], dk_ref[:, a:a + LANES], MLA_NOPE)
            dkv_ref[:, a:a + LANES] = dx.astype(dkv_ref.dtype)
            dgn_ref[...] += _rowsum(dg)
            dkv_ref[:, a + LANES:a + HEAD_W] = dv_ref[:, h * LANES:(h + 1) * LANES].astype(dkv_ref.dtype)
            dkr = dkr + dk_ref[:, a + LANES:a + HEAD_W]
        dkr_ref[...] = dkr

    return pl.pallas_call(
        body, grid=(t // tm,), in_specs=[_row(tm, w), _row(tm, w // 2), _row(tm, w), _const((1, LANES))],
        out_specs=[_row(tm, w), _row(tm, LANES), _const((1, LANES))],
        out_shape=[_sds((t, w), BF16), _sds((t, LANES)), _sds((1, LANES))],
        compiler_params=_params("arbitrary"), name=name)(dkf, dvf, kv, nope_g)


def _attn_specs(nb, l, lc, tq):
    nq = l // tq
    ctx0 = nb * l // lc
    q_spec = lambda w: pl.BlockSpec((tq, w), lambda b, h, i: (b * nq + i, h))
    lat = lambda w: pl.BlockSpec((l, w), lambda b, h, i: (b, h))
    ctx = lambda w: pl.BlockSpec((lc, w), lambda b, h, i: (ctx0 + b, h))
    return nq, q_spec, lat, ctx


def _attn_fwd(qf, kf, vf, nb, l, lc, name):
    tq = _pick(l, (256, 128))
    nq, q_spec, lat, ctx = _attn_specs(nb, l, lc, tq)

    def body(q_ref, kl_ref, kc_ref, vl_ref, vc_ref, o_ref, lse_ref):
        q = q_ref[...]
        s1 = _dot_nt(q, kl_ref[...]) * MLA_SCALE
        s2 = _dot_nt(q, kc_ref[...]) * MLA_SCALE
        m = jnp.maximum(jnp.max(s1, axis=-1, keepdims=True), jnp.max(s2, axis=-1, keepdims=True))
        p1, p2 = jnp.exp(s1 - m), jnp.exp(s2 - m)
        den = jnp.sum(p1, axis=-1, keepdims=True) + jnp.sum(p2, axis=-1, keepdims=True)
        o_ref[...] = (_dot(p1, vl_ref[...]) + _dot(p2, vc_ref[...])) / den
        lse_ref[...] = jnp.broadcast_to(m + jnp.log(den), lse_ref.shape)

    return pl.pallas_call(
        body, grid=(nb, MLA_HEADS, nq),
        in_specs=[q_spec(HEAD_W), lat(HEAD_W), ctx(HEAD_W), lat(LANES), ctx(LANES)],
        out_specs=[q_spec(LANES), q_spec(LANES)],
        out_shape=[_sds((nb * l, MLA_HEADS * LANES)), _sds((nb * l, MLA_HEADS * LANES))],
        compiler_params=_params("parallel", "parallel", "arbitrary"), name=name)(qf, kf, kf, vf, vf)


def _attn_bwd(do, o, lse, qf, kf, vf, nb, l, lc, name):
    tq = _pick(l, (256, 128))
    nq, q_spec, lat, ctx = _attn_specs(nb, l, lc, tq)
    out_lat = lambda w: pl.BlockSpec((l, w), lambda b, h, i: (b, h))
    out_ctx = lambda w: pl.BlockSpec((lc, w), lambda b, h, i: (b, h))

    def body(do_ref, o_ref, lse_ref, q_ref, kl_ref, kc_ref, vl_ref, vc_ref, dq_ref, dkl_ref, dkc_ref, dvl_ref, dvc_ref):
        @pl.when(pl.program_id(2) == 0)
        def _():
            dkl_ref[...] = jnp.zeros_like(dkl_ref)
            dkc_ref[...] = jnp.zeros_like(dkc_ref)
            dvl_ref[...] = jnp.zeros_like(dvl_ref)
            dvc_ref[...] = jnp.zeros_like(dvc_ref)

        q, dof = q_ref[...], do_ref[...]
        delta = jnp.sum(dof * o_ref[...], axis=-1, keepdims=True)
        lse = lse_ref[:, :1]
        dq = jnp.zeros(dq_ref.shape, F32)
        for k_ref, v_ref, dk_ref, dv_ref in ((kl_ref, vl_ref, dkl_ref, dvl_ref), (kc_ref, vc_ref, dkc_ref, dvc_ref)):
            p = jnp.exp(_dot_nt(q, k_ref[...]) * MLA_SCALE - lse)
            ds = p * (_dot_nt(dof, v_ref[...]) - delta) * MLA_SCALE
            dq = dq + _dot(ds, k_ref[...])
            dk_ref[...] += _dot_tn(ds, q)
            dv_ref[...] += _dot_tn(p, dof)
        dq_ref[...] = dq

    kw, vw = MLA_HEADS * HEAD_W, MLA_HEADS * LANES
    return pl.pallas_call(
        body, grid=(nb, MLA_HEADS, nq),
        in_specs=[q_spec(LANES), q_spec(LANES), q_spec(LANES), q_spec(HEAD_W), lat(HEAD_W), ctx(HEAD_W), lat(LANES),
                  ctx(LANES)],
        out_specs=[q_spec(HEAD_W), out_lat(HEAD_W), out_ctx(HEAD_W), out_lat(LANES), out_ctx(LANES)],
        out_shape=[_sds((nb * l, kw)), _sds((nb * l, kw)), _sds((nb * lc, kw)), _sds((nb * l, vw)),
                   _sds((nb * lc, vw))],
        compiler_params=_params("parallel", "parallel", "arbitrary"), name=name)(do, o, lse, qf, kf, kf, vf, vf)


def _gate_fwd(o, g, tm, name):
    t, e = o.shape

    def body(o_ref, g_ref, y_ref):
        y_ref[...] = (o_ref[...] * _silu(g_ref[...])).astype(y_ref.dtype)

    return pl.pallas_call(
        body, grid=(t // tm,), in_specs=[_row(tm, e), _row(tm, e)], out_specs=_row(tm, e),
        out_shape=_sds((t, e), BF16), compiler_params=_params("parallel"), name=name)(o, g)


def _gate_bwd(dy, o, g, tm, name):
    t, e = o.shape

    def body(dy_ref, o_ref, g_ref, do_ref, dg_ref):
        dy, gz = dy_ref[...], g_ref[...]
        do_ref[...] = dy * _silu(gz)
        dg_ref[...] = (dy * o_ref[...] * _dsilu(gz)).astype(dg_ref.dtype)

    return pl.pallas_call(
        body, grid=(t // tm,), in_specs=[_row(tm, e), _row(tm, e), _row(tm, e)],
        out_specs=[_row(tm, e), _row(tm, e)], out_shape=[_sds((t, e)), _sds((t, e), BF16)],
        compiler_params=_params("parallel"), name=name)(dy, o, g)


def _chunk_fwd(z, ln_g, ln_b, w_s, bs_full, name):
    t, e = z.shape[0], z.shape[1] // 3

    def body(u_ref, v_ref, g_ref, lg_ref, lb_ref, w_ref, bs_ref, y_ref):
        xh, _ = _layernorm_parts(v_ref[...])
        vn = xh * lg_ref[...] + lb_ref[...]
        for k in range(CHUNK_GROUPS):
            cols = slice(k * LANES, (k + 1) * LANES)
            s = _dot(w_ref[k], vn[:, cols]) + bs_ref[:, cols]
            y_ref[:, cols] = (u_ref[:, cols] * s * _silu(g_ref[:, cols])).astype(y_ref.dtype)

    return pl.pallas_call(
        body, grid=(t // CHUNK,),
        in_specs=[_row(CHUNK, e, 0), _row(CHUNK, e, 1), _row(CHUNK, e, 2), _const((1, e)), _const((1, e)),
                  _const(w_s.shape), _const((CHUNK, e))],
        out_specs=_row(CHUNK, e), out_shape=_sds((t, e), BF16), compiler_params=_params("parallel"),
        name=name)(z, z, z, ln_g, ln_b, w_s, bs_full)


def _chunk_bwd(dy, z, ln_g, ln_b, w_s, bs_full, name):
    t, e = z.shape[0], z.shape[1] // 3

    def body(dy_ref, u_ref, v_ref, g_ref, lg_ref, lb_ref, w_ref, bs_ref, dz_ref, dw_ref, dbs_ref, dlg_ref, dlb_ref,
             acc_ref):
        i = pl.program_id(0)

        @pl.when(i == 0)
        def _():
            dw_ref[...] = jnp.zeros_like(dw_ref)
            dlg_ref[...] = jnp.zeros_like(dlg_ref)
            dlb_ref[...] = jnp.zeros_like(dlb_ref)
            acc_ref[...] = jnp.zeros_like(acc_ref)

        xh, r = _layernorm_parts(v_ref[...])
        vn = xh * lg_ref[...] + lb_ref[...]
        dvn = []
        for k in range(CHUNK_GROUPS):
            cols = slice(k * LANES, (k + 1) * LANES)
            dyk, u, gz = dy_ref[:, cols], u_ref[:, cols], g_ref[:, cols]
            s = _dot(w_ref[k], vn[:, cols]) + bs_ref[:, cols]
            sg = _silu(gz)
            dz_ref[:, cols] = (dyk * s * sg).astype(dz_ref.dtype)
            dz_ref[:, 2 * e + k * LANES:2 * e + (k + 1) * LANES] = (dyk * u * s * _dsilu(gz)).astype(dz_ref.dtype)
            ds = dyk * u * sg
            acc_ref[:, cols] += ds
            dw_ref[k] += _dot_nt(ds, vn[:, cols])
            dvn.append(_dot_tn(w_ref[k], ds))
        dvn = jnp.concatenate(dvn, axis=1)
        dlg_ref[...] += _rowsum(dvn * xh)
        dlb_ref[...] += _rowsum(dvn)
        dz_ref[:, e:2 * e] = _layernorm_bwd(dvn, xh, r, lg_ref[...]).astype(dz_ref.dtype)

        @pl.when(i == pl.num_programs(0) - 1)
        def _():
            lane = lax.broadcasted_iota(jnp.int32, dbs_ref.shape, 1)
            out = jnp.zeros(dbs_ref.shape, F32)
            for k in range(CHUNK_GROUPS):
                col = jnp.sum(acc_ref[:, k * LANES:(k + 1) * LANES], axis=1, keepdims=True)
                out = jnp.where(lane == k, col, out)
            dbs_ref[...] = out

    return pl.pallas_call(
        body, grid=(t // CHUNK,),
        in_specs=[_row(CHUNK, e), _row(CHUNK, e, 0), _row(CHUNK, e, 1), _row(CHUNK, e, 2), _const((1, e)),
                  _const((1, e)), _const(w_s.shape), _const((CHUNK, e))],
        out_specs=[_row(CHUNK, 3 * e), _const(w_s.shape), _const((CHUNK, CHUNK_GROUPS)), _const((1, e)),
                   _const((1, e))],
        out_shape=[_sds((t, 3 * e), BF16), _sds(w_s.shape), _sds((CHUNK, CHUNK_GROUPS)), _sds((1, e)), _sds((1, e))],
        scratch_shapes=[pltpu.VMEM((CHUNK, e), F32)],
        compiler_params=_params("arbitrary"), name=name)(dy, z, z, z, ln_g, ln_b, w_s, bs_full)


def _mod_rows(mods, layer, d, nseg):
    m = mods[layer, :nseg]
    return [m[:, None, k * d:(k + 1) * d] for k in range(3)]


def _local_step(x, ctx, tgt, w, mods):
    nb, l, d = x.shape
    lc = ctx.shape[1]
    e = d
    tl, ta = nb * l, nb * (l + lc)
    tm = _pick(lc, (256, 128))
    segs_a, segs_l = _Segs((l,) * nb + (lc,) * nb, tm), _Segs((l,) * nb, tm)
    norm_g = w['norm_g']
    g = {}

    xa0 = jnp.concatenate([x.reshape(tl, d), ctx.reshape(nb * lc, d)], axis=0)

    sh0, sc0, gt0 = _mod_rows(mods, 0, d, 2 * nb)
    h0 = _rms_mod_fwd(xa0, norm_g[0:1], sc0, sh0, segs_a, "l0_norm")
    z0 = _mm(h0, w['cv_w_in'], "l0_in")
    y2_0 = jnp.concatenate([_conv1_fwd(z0, w['cv_dw'], w['cv_db'], nb, l, 0, "l0_conv_lat"),
                            _conv1_fwd(z0, w['cv_dw'], w['cv_db'], nb, lc, tl, "l0_conv_ctx")], axis=0)
    y4_0 = _conv2_fwd(y2_0, z0, w['cv_ln_g'], w['cv_ln_b'], tm, "l0_gate")
    o0 = _mm(y4_0, w['cv_w_out'], "l0_out")
    xa1 = _resid_fwd(xa0, o0, gt0, segs_a, "l0_resid")

    sh1, sc1, gt1 = _mod_rows(mods, 1, d, 2 * nb)
    h1 = _rms_mod_fwd(xa1, norm_g[1:2], sc1, sh1, segs_a, "l1_norm")
    z1 = _mm(h1, w['pl_w_in'], "l1_in")
    taps_l, ic_l = _pool_tables(l, e)
    taps_c, ic_c = _pool_tables(lc, e)
    pm1 = jnp.concatenate([_pool1(z1, taps_l, ic_l, nb, l, 0, False, "l1_pool_lat", BF16),
                           _pool1(z1, taps_c, ic_c, nb, lc, tl, False, "l1_pool_ctx", BF16)], axis=0)
    y2_1 = _pool2_fwd(pm1, w['pl_w_grp'], w['pl_scale'], z1, tm, "l1_group")
    o1 = _mm(y2_1, w['pl_w_out'], "l1_out")
    xa2 = _resid_fwd(xa1, o1, gt1, segs_a, "l1_resid")

    sh2, sc2, gt2 = _mod_rows(mods, 2, d, 2 * nb)
    h2 = _rms_mod_fwd(xa2, norm_g[2:3], sc2, sh2, segs_a, "l2_norm")
    w_in = w['ml_w_in']
    kvc = MLA_KV_RANK + MLA_ROPE
    w_in_p = jnp.concatenate([w_in[:, :kvc], jnp.zeros((d, LANES - MLA_ROPE), w_in.dtype), w_in[:, kvc:]], axis=1)
    w_uq_p = jnp.pad(w['ml_w_uq'].reshape(MLA_Q_RANK, MLA_HEADS, MLA_NOPE + MLA_ROPE),
                     ((0, 0), (0, 0), (0, HEAD_W - MLA_NOPE - MLA_ROPE))).reshape(MLA_Q_RANK, MLA_HEADS * HEAD_W)
    rope_g = jnp.pad(w['ml_rope_norm'], ((0, 0), (0, LANES - MLA_ROPE)))
    nope_g = w['ml_nope_norm']
    ctab, stab = _rope_tables(l, lc, nb)
    kvw = MLA_KV_RANK + LANES
    z2 = _mm(h2, w_in_p, "l2_in")
    zkv, zq, zg = z2[:, :kvw], z2[:tl, kvw:kvw + MLA_Q_RANK], z2[:tl, kvw + MLA_Q_RANK:]
    ckvn, krr = _kv_pre_fwd(zkv, w['ml_kv_norm'], rope_g[1:2], ctab, stab, tm, "l2_kv_pre")
    cqn = _q_pre_fwd(zq, w['ml_q_norm'], tm, "l2_q_pre")
    q2 = _mm(cqn, w_uq_p, "l2_uq")
    kv2 = _mm(ckvn, w['ml_w_ukv'], "l2_ukv")
    qf = _q_post_fwd(q2, nope_g[0:1], rope_g[0:1], ctab, stab, tm, "l2_q_post")
    kf, vf = _k_post_fwd(kv2, krr, nope_g[1:2], tm, "l2_k_post")
    o_att, lse = _attn_fwd(qf, kf, vf, nb, l, lc, "l2_attn")
    og = _gate_fwd(o_att, zg, tm, "l2_gate")
    o2 = _mm(og, w['ml_w_out'], "l2_out")
    x3 = _resid_fwd(xa2, o2, gt2[:nb], segs_l, "l2_resid")

    sh3, sc3, gt3 = _mod_rows(mods, 3, d, nb)
    h3 = _rms_mod_fwd(x3, norm_g[3:4], sc3, sh3, segs_l, "l3_norm")
    z3 = _mm(h3, w['ch_w_in'], "l3_in")
    bs_full = jnp.repeat(w['ch_b_s'], e // CHUNK_GROUPS, axis=1)
    y3 = _chunk_fwd(z3, w['ch_ln_g'], w['ch_ln_b'], w['ch_w_s'], bs_full, "l3_chunk")
    o3 = _mm(y3, w['ch_w_out'], "l3_out")
    x4 = _resid_fwd(x3, o3, gt3, segs_l, "l3_resid")

    loss_vec, dx4 = _loss_head(x4, tgt.reshape(tl, d), tm, "loss")

    do3, dgt3 = _resid_bwd(dx4, o3, gt3, segs_l, "l3_resid_b")
    dy3 = _mm_nt(do3, w['ch_w_out'], "l3_out_bx")
    g['ch_w_out'] = _mm_tn(y3, do3, "l3_out_bw")
    dz3, g['ch_w_s'], g['ch_b_s'], g['ch_ln_g'], g['ch_ln_b'] = _chunk_bwd(
        dy3, z3, w['ch_ln_g'], w['ch_ln_b'], w['ch_w_s'], bs_full, "l3_chunk_b")
    dh3 = _mm_nt(dz3, w['ch_w_in'], "l3_in_bx")
    g['ch_w_in'] = _mm_tn(h3, dz3, "l3_in_bw", shards=N_DEV)
    dx3, dng3, dsc3, dsh3 = _rms_mod_bwd(x3, norm_g[3:4], sc3, sh3, dh3, dx4, segs_l, "l3_norm_b")

    do2, dgt2 = _resid_bwd(dx3, o2, gt2[:nb], segs_l, "l2_resid_b")
    dog = _mm_nt(do2, w['ml_w_out'], "l2_out_bx")
    g['ml_w_out'] = _mm_tn(og, do2, "l2_out_bw")
    d_att, dzg = _gate_bwd(dog, o_att, zg, tm, "l2_gate_b")
    dqf, dkl, dkc, dvl, dvc = _attn_bwd(d_att, o_att, lse, qf, kf, vf, nb, l, lc, "l2_attn_b")
    dkf, dvf = jnp.concatenate([dkl, dkc], axis=0), jnp.concatenate([dvl, dvc], axis=0)
    dq2, dnope_q, drope_q = _q_post_bwd(dqf, q2, nope_g[0:1], rope_g[0:1], ctab, stab, tm, "l2_q_post_b")
    dkv2, dkrr, dnope_k = _k_post_bwd(dkf, dvf, kv2, nope_g[1:2], tm, "l2_k_post_b")
    dcqn = _mm_nt(dq2, w_uq_p, "l2_uq_bx")
    g_uq_p = _mm_tn(cqn, dq2, "l2_uq_bw")
    dckvn = _mm_nt(dkv2, w['ml_w_ukv'], "l2_ukv_bx")
    g['ml_w_ukv'] = _mm_tn(ckvn, dkv2, "l2_ukv_bw", shards=N_DEV)
    dzq, g['ml_q_norm'] = _q_pre_bwd(dcqn, zq, w['ml_q_norm'], tm, "l2_q_pre_b")
    dzkv, g['ml_kv_norm'], drope_k = _kv_pre_bwd(dckvn, dkrr, zkv, w['ml_kv_norm'], rope_g[1:2], ctab, stab, tm,
                                                  "l2_kv_pre_b")
    dz2 = jnp.concatenate([dzkv, jnp.pad(jnp.concatenate([dzq, dzg], axis=1), ((0, ta - tl), (0, 0)))], axis=1)
    dh2 = _mm_nt(dz2, w_in_p, "l2_in_bx")
    g_in_p = _mm_tn(h2, dz2, "l2_in_bw")
    g['ml_w_in'] = jnp.concatenate([g_in_p[:, :kvc], g_in_p[:, kvw:]], axis=1)
    g['ml_w_uq'] = g_uq_p.reshape(MLA_Q_RANK, MLA_HEADS, HEAD_W)[:, :, :MLA_NOPE + MLA_ROPE].reshape(
        MLA_Q_RANK, MLA_HEADS * (MLA_NOPE + MLA_ROPE))
    g['ml_nope_norm'] = jnp.concatenate([dnope_q, dnope_k], axis=0)
    g['ml_rope_norm'] = jnp.concatenate([drope_q, drope_k], axis=0)[:, :MLA_ROPE]
    dx3a = jnp.pad(dx3, ((0, ta - tl), (0, 0)))
    dxa2, dng2, dsc2, dsh2 = _rms_mod_bwd(xa2, norm_g[2:3], sc2, sh2, dh2, dx3a, segs_a, "l2_norm_b")

    do1, dgt1 = _resid_bwd(dxa2, o1, gt1, segs_a, "l1_resid_b")
    dy2_1 = _mm_nt(do1, w['pl_w_out'], "l1_out_bx")
    g['pl_w_out'] = _mm_tn(y2_1, do1, "l1_out_bw")
    dpm, dgz1, g['pl_scale'], g['pl_w_grp'] = _pool2_bwd(dy2_1, pm1, w['pl_w_grp'], w['pl_scale'], z1, tm,
                                                          "l1_group_b")
    dv1 = jnp.concatenate([_pool1(dpm, taps_l, ic_l, nb, l, 0, True, "l1_pool_lat_b", BF16),
                           _pool1(dpm, taps_c, ic_c, nb, lc, tl, True, "l1_pool_ctx_b", BF16)], axis=0)
    dz1 = jnp.concatenate([dv1, dgz1], axis=1)
    dh1 = _mm_nt(dz1, w['pl_w_in'], "l1_in_bx")
    g['pl_w_in'] = _mm_tn(h1, dz1, "l1_in_bw", shards=N_DEV)
    dxa1, dng1, dsc1, dsh1 = _rms_mod_bwd(xa1, norm_g[1:2], sc1, sh1, dh1, dxa2, segs_a, "l1_norm_b")

    do0, dgt0 = _resid_bwd(dxa1, o0, gt0, segs_a, "l0_resid_b")
    dy4 = _mm_nt(do0, w['cv_w_out'], "l0_out_bx")
    g['cv_w_out'] = _mm_tn(y4_0, do0, "l0_out_bw")
    dy2, dgz0, g['cv_ln_g'], g['cv_ln_b'] = _conv2_bwd(dy4, y2_0, z0, w['cv_ln_g'], w['cv_ln_b'], tm, "l0_gate_b")
    da_l, db_l, ddw, ddb = _conv1_bwd(dy2, z0, w['cv_dw'], jnp.zeros((CONV_WIDTH, e), F32), jnp.zeros((1, e), F32),
                                      nb, l, 0, "l0_conv_lat_b")
    da_c, db_c, g['cv_dw'], g['cv_db'] = _conv1_bwd(dy2, z0, w['cv_dw'], ddw, ddb, nb, lc, tl, "l0_conv_ctx_b")
    dz0 = jnp.concatenate([jnp.concatenate([da_l, da_c], axis=0), jnp.concatenate([db_l, db_c], axis=0), dgz0],
                          axis=1)
    dh0 = _mm_nt(dz0, w['cv_w_in'], "l0_in_bx")
    g['cv_w_in'] = _mm_tn(h0, dz0, "l0_in_bw", shards=N_DEV)
    dxa0, dng0, dsc0, dsh0 = _rms_mod_bwd(xa0, norm_g[0:1], sc0, sh0, dh0, dxa1, segs_a, "l0_norm_b")

    def rows4(t):
        return jnp.pad(t[:, 0], ((0, 2 * nb - t.shape[0]), (0, 0)))

    dmods = jnp.stack([
        jnp.concatenate([rows4(dsh0), rows4(dsc0), rows4(dgt0)], axis=1),
        jnp.concatenate([rows4(dsh1), rows4(dsc1), rows4(dgt1)], axis=1),
        jnp.concatenate([rows4(dsh2), rows4(dsc2), rows4(dgt2)], axis=1),
        jnp.concatenate([rows4(dsh3), rows4(dsc3), rows4(dgt3)], axis=1)])
    dnorm_g = jnp.concatenate([dng0, dng1, dng2, dng3], axis=0)
    return loss_vec, dxa0[:tl].reshape(nb, l, d), g, dmods, dnorm_g


def _mesh_pos():
    return lax.axis_index("x"), lax.axis_index("y"), lax.axis_index("c")


def _remote(src, dst, send_sems, recv_sems, k, dev):
    return pltpu.make_async_remote_copy(src_ref=src, dst_ref=dst, send_sem=send_sems.at[k], recv_sem=recv_sems.at[k],
                                        device_id=dev, device_id_type=pl.DeviceIdType.MESH)


def _comm_call(body, xs, out_shapes, n_remote, n_local, name):
    hbm = pl.BlockSpec(memory_space=pltpu.HBM)
    return pl.pallas_call(
        body, in_specs=[hbm] * len(xs), out_specs=[hbm] * len(out_shapes), out_shape=out_shapes,
        scratch_shapes=[pltpu.SemaphoreType.DMA((n_remote,)), pltpu.SemaphoreType.DMA((n_remote,)),
                        pltpu.SemaphoreType.DMA((max(n_local, 1),))],
        compiler_params=pltpu.CompilerParams(has_side_effects=True), name=name)(*xs)


def _gather_all(xs, name):
    n = len(xs)

    def body(*refs):
        x_refs, o_refs, (send_sems, recv_sems, local_sems) = refs[:n], refs[n:2 * n], refs[2 * n:]
        x, y, c = _mesh_pos()
        me, sib = (x, y, c), (x, y, 1 - c)
        chips = [(1 - x, y), (x, 1 - y), (1 - x, 1 - y)]

        def slot(a, p):
            return o_refs[a].at[4 * p[0] + 2 * p[1] + p[2]]

        def copy(a, k, block, to, src=None):
            return _remote(slot(a, block) if src is None else src, slot(a, block), send_sems, recv_sems, 7 * a + k, to)

        mine = [pltpu.make_async_copy(x_refs[a], slot(a, me), local_sems.at[a]) for a in range(n)]
        for cp in mine:
            cp.start()
        first = []
        for a in range(n):
            first += [copy(a, 1 + j, me, chip + (c,), src=x_refs[a]) for j, chip in enumerate(chips)]
            first.append(copy(a, 0, me, sib, src=x_refs[a]))
        for cp in first:
            cp.start()
        passed = []
        for j, chip in enumerate(chips):
            for a in range(n):
                copy(a, 1 + j, chip + (c,), me).wait_recv()
                passed.append(copy(a, 4 + j, chip + (c,), sib))
                passed[-1].start()
        for a in range(n):
            copy(a, 0, sib, me).wait_recv()
        for j, chip in enumerate(chips):
            for a in range(n):
                copy(a, 4 + j, chip + (1 - c,), me).wait_recv()
        for cp in first + passed:
            cp.wait_send()
        for cp in mine:
            cp.wait()

    return _comm_call(body, xs, [_sds((N_DEV,) + x.shape, x.dtype) for x in xs], 7 * n, n, name)


def _swap_halves(xs, name):
    n = len(xs)

    def body(*refs):
        x_refs, o_refs, (send_sems, recv_sems, _) = refs[:n], refs[n:2 * n], refs[2 * n:]
        x, y, c = _mesh_pos()
        copies = [_remote(x_refs[a].at[q, 1 - c], o_refs[a].at[q], send_sems, recv_sems, 4 * a + q, (x, y, 1 - c))
                  for a in range(n) for q in range(4)]
        for cp in copies:
            cp.start()
        for cp in copies:
            cp.wait_recv()
        for cp in copies:
            cp.wait_send()

    return _comm_call(body, xs, [_sds((4,) + x.shape[2:], x.dtype) for x in xs], 4 * n, 0, name)


def _quad_exchange(xs, name):
    n = len(xs)

    def body(*refs):
        x_refs, o_refs, (send_sems, recv_sems, local_sems) = refs[:n], refs[n:2 * n], refs[2 * n:]
        x, y, c = _mesh_pos()
        q = 2 * x + y
        chips = [(1 - x, y), (x, 1 - y), (1 - x, 1 - y)]
        mine = [pltpu.make_async_copy(x_refs[a].at[q], o_refs[a].at[q], local_sems.at[a]) for a in range(n)]
        for cp in mine:
            cp.start()
        sends, arrivals = [], []
        for a in range(n):
            for j, chip in enumerate(chips):
                qj = 2 * chip[0] + chip[1]
                sends.append(_remote(x_refs[a].at[qj], o_refs[a].at[q], send_sems, recv_sems, 3 * a + j, chip + (c,)))
                arrivals.append(_remote(x_refs[a].at[qj], o_refs[a].at[qj], send_sems, recv_sems, 3 * a + j,
                                        chip + (c,)))
        for cp in sends:
            cp.start()
        for cp in arrivals:
            cp.wait_recv()
        for cp in sends:
            cp.wait_send()
        for cp in mine:
            cp.wait()

    return _comm_call(body, xs, [_sds(x.shape, x.dtype) for x in xs], 3 * n, n, name)


def _pair_add(x, r, name):
    _, _, rows, cols = x.shape
    tr = _pick(rows, (512, 256, 128, 64, 32, 16))

    def body(x_ref, r_ref, o_ref):
        c = lax.axis_index("c")
        o_ref[...] = (x_ref[c].astype(F32) + r_ref[...].astype(F32)).astype(o_ref.dtype)

    return pl.pallas_call(
        body, grid=(4, rows // tr),
        in_specs=[pl.BlockSpec((None, 2, tr, cols), lambda q, i: (q, 0, i, 0)),
                  pl.BlockSpec((None, tr, cols), lambda q, i: (q, i, 0))],
        out_specs=pl.BlockSpec((None, tr, cols), lambda q, i: (q, i, 0)), out_shape=_sds((4, rows, cols), x.dtype),
        compiler_params=_params("parallel", "parallel"), name=name)(x, r)


def _pack_rows(n):
    r = -(-n // PACK_COLS)
    return -(-r // 256) * 256 if r > 256 else -(-r // 16) * 16


def _pack(arrs, dtype):
    flat = jnp.concatenate([a.reshape(-1).astype(dtype) for a in arrs])
    rows = _pack_rows(flat.shape[0])
    return jnp.pad(flat, (0, rows * PACK_COLS - flat.shape[0])).reshape(rows, PACK_COLS)


def _pack_shards(arrs):
    flat = jnp.concatenate([a.astype(F32) for a in arrs], axis=1)
    rows = _pack_rows(flat.shape[1])
    return jnp.pad(flat, ((0, 0), (0, rows * PACK_COLS - flat.shape[1]))).reshape(N_DEV, rows, PACK_COLS)


def _unpack(packed, shapes, lead=()):
    flat = packed.reshape(tuple(lead) + (-1,))
    out, off = [], 0
    for s in shapes:
        n = 1
        for v in s:
            n *= v
        out.append(flat[..., off:off + n].reshape(tuple(lead) + tuple(s)))
        off += n
    return out


def _to_shards(full, ax):
    s = full.shape
    t = full.reshape(s[:ax] + (N_DEV, s[ax] // N_DEV) + s[ax + 1:])
    return jnp.moveaxis(t, ax, 0).reshape(N_DEV, -1)


def _from_shards(shards, local_shape, ax):
    t = jnp.moveaxis(shards.reshape((N_DEV,) + tuple(local_shape)), 0, ax)
    s = t.shape
    return t.reshape(s[:ax] + (s[ax] * s[ax + 1],) + s[ax + 2:])


def _mod_fwd(c_rows, w_mod, b_mod, name):
    nl, d, n = w_mod.shape
    r = c_rows.shape[0]

    def body(c_ref, w_ref, b_ref, o_ref):
        s = _silu(c_ref[...])
        for l in range(nl):
            o_ref[l] = _dot(s, w_ref[l]) + b_ref[l]

    return pl.pallas_call(body, out_shape=_sds((nl, r, n)),
                          compiler_params=pltpu.CompilerParams(vmem_limit_bytes=VMEM_LIMIT), name=name)(
        c_rows, w_mod, b_mod)


def _mod_bwd(c_rows, dcols, dall, w_mod, c_ctx, name):
    nl, d, n = w_mod.shape
    r = c_rows.shape[0]

    def body(c_ref, dc_ref, da_ref, w_ref, cc_ref, gw_ref, gb_ref, gc_ref):
        s = _silu(c_ref[...])
        ds = jnp.zeros((r, d), F32)
        for l in range(nl):
            gw_ref[l] = _dot_tn(s, dc_ref[l])
            gb_ref[l] = _rowsum(da_ref[l])
            ds = ds + _dot_nt(dc_ref[l], w_ref[l])
        row = lax.broadcasted_iota(jnp.int32, (r, d), 0)
        gc_ref[...] = _rowsum(jnp.where(row % 4 >= 2, ds, 0.0)) * _dsilu(cc_ref[...])

    return pl.pallas_call(body, out_shape=[_sds((nl, d, n)), _sds((nl, 1, 3 * d)), _sds((1, d))],
                          compiler_params=pltpu.CompilerParams(vmem_limit_bytes=VMEM_LIMIT), name=name)(
        c_rows, dcols, dall, w_mod, c_ctx)


def _adam_math(w, gsum, m, v):
    c1, c2 = 1.0 - ADAM_B1 ** ADAM_STEP, 1.0 - ADAM_B2 ** ADAM_STEP
    mn = ADAM_B1 * m + (1.0 - ADAM_B1) * gsum
    vn = ADAM_B2 * v + (1.0 - ADAM_B2) * (gsum * gsum)
    return -ADAM_LR * ((mn / c1) / (jnp.sqrt(vn / c2) + ADAM_EPS) + ADAM_WD * w), mn, vn


def _adam(w, gparts, row0, m, v, name):
    rows, cols = w.shape
    npart = gparts.shape[0]
    if rows % 8:
        tr = rows
        assert row0 == 0 and gparts.shape[1] == rows
    else:
        tr = max(t for t in (512, 256, 128, 64, 32, 16, 8) if rows % t == 0 and row0 % t == 0
                 and (t * cols <= 256 * 1024 or t == 8))

    def body(w_ref, g_ref, m_ref, v_ref, go_ref, d_ref, mo_ref, vo_ref):
        gsum = g_ref[0].astype(F32)
        for p in range(1, npart):
            gsum = gsum + g_ref[p].astype(F32)
        go_ref[...] = gsum
        d_ref[...], mo_ref[...], vo_ref[...] = _adam_math(w_ref[...], gsum, m_ref[...], v_ref[...])

    spec = _row(tr, cols)
    return pl.pallas_call(
        body, grid=(rows // tr,),
        in_specs=[spec, pl.BlockSpec((npart, tr, cols), lambda i: (0, row0 // tr + i, 0)), spec, spec],
        out_specs=[spec] * 4, out_shape=[_sds((rows, cols))] * 4, compiler_params=_params("parallel"),
        name=name)(w, gparts, m, v)


INPUTS = ['x', 'c', 'ctx'] + WEIGHTS + ['loss_target'] + ['m_' + n for n in WEIGHTS] + ['v_' + n for n in WEIGHTS]
AXES = ("x", "y", "c")
GROUPS = (('cv_w_in', 'ch_w_in'), ('pl_w_in', 'ml_w_ukv', 'pl_w_grp'), ('cv_w_out', 'pl_w_out', 'ml_w_out', 'ch_w_out'),
          ('ml_w_in',), ('ml_w_uq',))
KINDS = ('grad_', 'delta_', 'new_m_', 'new_v_')


def _squeeze_layer(name, a):
    return a if name == 'norm_g' or a.ndim < 3 else a[0]


def _as2d(a):
    return a.reshape(-1, a.shape[-1])


def _train_step(a):
    x, c, ctx, tgt = a['x'], a['c'], a['ctx'], a['loss_target']
    d = x.shape[-1]
    nb = x.shape[0]
    dev = 4 * lax.axis_index("x") + 2 * lax.axis_index("y") + lax.axis_index("c")
    local_shape = {n: a[n].shape for n in WEIGHTS}
    rows_of = {n: _as2d(a[n]).shape[0] for n in WEIGHTS}

    vec_names = ['c'] + VECTOR_WEIGHTS
    sends = [jnp.concatenate([_as2d(a[n]).astype(BF16) for n in grp], axis=0) for grp in GROUPS]
    gathered = _gather_all(sends + [_pack([a[n] for n in vec_names], F32)], "gather_weights")
    parts = dict(zip(vec_names, _unpack(gathered[-1], [a[n].shape for n in vec_names], lead=(N_DEV,))))
    for grp, buf in zip(GROUPS, gathered):
        off = 0
        for n in grp:
            parts[n] = buf[:, off:off + rows_of[n]]
            off += rows_of[n]
    w = {}
    for n in WEIGHTS:
        if n in parts:
            w[n] = _squeeze_layer(n, _from_shards(parts[n], local_shape[n], SHARD_AXIS[n]))
        elif SHARD_AXIS[n] is None:
            w[n] = _squeeze_layer(n, a[n])
    c_all = parts['c'].reshape(N_DEV * nb, d)
    c_ctx = a['c_ctx'].reshape(1, d)

    w_mod = a['w_mod']
    nl, ncol = w_mod.shape[0], w_mod.shape[2]
    mod_rows = -(-(N_DEV * nb + 1) // 8) * 8
    c_rows = jnp.concatenate([c_all, c_ctx, jnp.zeros((mod_rows - N_DEV * nb - 1, d), F32)], axis=0)
    b_loc = lax.dynamic_slice(a['b_mod'], (0, dev * ncol), (nl, ncol))[:, None, :]
    mod_loc = _mod_fwd(c_rows, w_mod, b_loc, "mod_fwd")
    mod_all, = _gather_all([mod_loc.reshape(nl * mod_rows, ncol)], "gather_mods")
    mod_all = mod_all.reshape(N_DEV, nl, mod_rows, ncol).transpose(1, 2, 0, 3).reshape(nl, mod_rows, N_DEV * ncol)
    ctx_row = mod_all[:, N_DEV * nb:N_DEV * nb + 1]
    mods = jnp.concatenate([lax.dynamic_slice(mod_all, (0, dev * nb, 0), (nl, nb, 3 * d))] + [ctx_row] * nb, axis=1)

    loss_vec, grad_x, g, dmods, dnorm_g = _local_step(x, ctx, tgt, w, mods)
    loss = lax.psum(jnp.sum(loss_vec), AXES)

    nseg = dmods.shape[1]
    dm_all, = _gather_all([dmods.reshape(nl * nseg, 3 * d)], "gather_dmods")
    dm_all = dm_all.reshape(N_DEV, nl, nseg, 3 * d).transpose(1, 0, 2, 3).reshape(nl, N_DEV * nseg, 3 * d)
    dcols = lax.dynamic_slice(dm_all, (0, 0, dev * ncol), (nl, N_DEV * nseg, ncol))
    c_rows_b = jnp.concatenate([c_all.reshape(N_DEV, nb, d), jnp.broadcast_to(c_ctx, (N_DEV, nb, d))], axis=1)
    g_w_mod, g_b_mod, g_c_ctx = _mod_bwd(c_rows_b.reshape(N_DEV * nseg, d), dcols, dm_all, w_mod, c_ctx, "mod_bwd")

    g['c_ctx'], g['norm_g'] = g_c_ctx, dnorm_g

    def shard_major(n):
        if g[n].ndim == 3 and g[n].shape[0] == N_DEV and g[n].dtype == BF16:
            return g[n]
        whole = tuple(N_DEV * s if i == SHARD_AXIS[n] else s for i, s in enumerate(local_shape[n]))
        return _to_shards(g[n].reshape(whole), SHARD_AXIS[n]).reshape((N_DEV,) + _as2d(a[n]).shape)

    bufs = [jnp.concatenate([shard_major(n).astype(BF16) for n in grp], axis=1) for grp in GROUPS]
    bufs.append(_pack_shards([shard_major(n).reshape(N_DEV, -1) for n in VECTOR_WEIGHTS]))
    bufs = [b.reshape((4, 2) + b.shape[1:]) for b in bufs]
    from_sibling = _swap_halves(bufs, "grads_swap_cores")
    chip_sums = [_pair_add(b, r, "grads_add_cores_%d" % i) for i, (b, r) in enumerate(zip(bufs, from_sibling))]
    quad = _quad_exchange(chip_sums, "grads_exchange_chips")
    rep_all, = _gather_all([_pack([g[n] for n in REPLICATED], F32)], "gather_replicated_grads")

    out = {}

    def keep(names, res, shapes=None):
        for kind, val in zip(KINDS, res):
            if shapes is None:
                out[kind + names[0]] = val.reshape(local_shape[names[0]])
            else:
                for n, leaf in zip(names, _unpack(val, shapes)):
                    out[kind + n] = leaf

    def update_packed(names, gparts, tag):
        res = _adam(_pack([a[n] for n in names], F32), gparts, 0, _pack([a['m_' + n] for n in names], F32),
                    _pack([a['v_' + n] for n in names], F32), "adam_" + tag)
        keep(names, res, [local_shape[n] for n in names])

    for grp, buf in zip(GROUPS, quad):
        off = 0
        for n in grp:
            keep([n], _adam(_as2d(a[n]), buf, off, _as2d(a['m_' + n]), _as2d(a['v_' + n]), "adam_" + n))
            off += rows_of[n]
    update_packed(VECTOR_WEIGHTS, quad[-1], "vectors")
    update_packed(REPLICATED, rep_all, "replicated")
    keep(['w_mod'], _adam(_as2d(w_mod), _as2d(g_w_mod)[None], 0, _as2d(a['m_w_mod']), _as2d(a['v_w_mod']),
                          "adam_w_mod"))
    keep(['b_mod'], _adam(a['b_mod'], g_b_mod.reshape((1,) + a['b_mod'].shape), 0, a['m_b_mod'], a['v_b_mod'],
                          "adam_b_mod"))
    return (loss, grad_x) + tuple(out[kind + n] for kind in KINDS for n in WEIGHTS)


def kernel(x, c, ctx, c_ctx, norm_g, w_mod, b_mod, cv_w_in, cv_dw, cv_db, cv_ln_g, cv_ln_b, cv_w_out, pl_w_in, pl_w_grp, pl_scale, pl_w_out, ml_w_in, ml_q_norm, ml_kv_norm, ml_w_uq, ml_w_ukv, ml_nope_norm, ml_rope_norm, ml_w_out, ch_w_in, ch_ln_g, ch_ln_b, ch_w_s, ch_b_s, ch_w_out, loss_target, m_c_ctx, m_norm_g, m_w_mod, m_b_mod, m_cv_w_in, m_cv_dw, m_cv_db, m_cv_ln_g, m_cv_ln_b, m_cv_w_out, m_pl_w_in, m_pl_w_grp, m_pl_scale, m_pl_w_out, m_ml_w_in, m_ml_q_norm, m_ml_kv_norm, m_ml_w_uq, m_ml_w_ukv, m_ml_nope_norm, m_ml_rope_norm, m_ml_w_out, m_ch_w_in, m_ch_ln_g, m_ch_ln_b, m_ch_w_s, m_ch_b_s, m_ch_w_out, v_c_ctx, v_norm_g, v_w_mod, v_b_mod, v_cv_w_in, v_cv_dw, v_cv_db, v_cv_ln_g, v_cv_ln_b, v_cv_w_out, v_pl_w_in, v_pl_w_grp, v_pl_scale, v_pl_w_out, v_ml_w_in, v_ml_q_norm, v_ml_kv_norm, v_ml_w_uq, v_ml_w_ukv, v_ml_nope_norm, v_ml_rope_norm, v_ml_w_out, v_ch_w_in, v_ch_ln_g, v_ch_ln_b, v_ch_w_s, v_ch_b_s, v_ch_w_out):
    return _train_step(dict(zip(INPUTS, (x, c, ctx, c_ctx, norm_g, w_mod, b_mod, cv_w_in, cv_dw, cv_db, cv_ln_g, cv_ln_b, cv_w_out, pl_w_in, pl_w_grp, pl_scale, pl_w_out, ml_w_in, ml_q_norm, ml_kv_norm, ml_w_uq, ml_w_ukv, ml_nope_norm, ml_rope_norm, ml_w_out, ch_w_in, ch_ln_g, ch_ln_b, ch_w_s, ch_b_s, ch_w_out, loss_target, m_c_ctx, m_norm_g, m_w_mod, m_b_mod, m_cv_w_in, m_cv_dw, m_cv_db, m_cv_ln_g, m_cv_ln_b, m_cv_w_out, m_pl_w_in, m_pl_w_grp, m_pl_scale, m_pl_w_out, m_ml_w_in, m_ml_q_norm, m_ml_kv_norm, m_ml_w_uq, m_ml_w_ukv, m_ml_nope_norm, m_ml_rope_norm, m_ml_w_out, m_ch_w_in, m_ch_ln_g, m_ch_ln_b, m_ch_w_s, m_ch_b_s, m_ch_w_out, v_c_ctx, v_norm_g, v_w_mod, v_b_mod, v_cv_w_in, v_cv_dw, v_cv_db, v_cv_ln_g, v_cv_ln_b, v_cv_w_out, v_pl_w_in, v_pl_w_grp, v_pl_scale, v_pl_w_out, v_ml_w_in, v_ml_q_norm, v_ml_kv_norm, v_ml_w_uq, v_ml_w_ukv, v_ml_nope_norm, v_ml_rope_norm, v_ml_w_out, v_ch_w_in, v_ch_ln_g, v_ch_ln_b, v_ch_w_s, v_ch_b_s, v_ch_w_out))))
```

```python
import functools

import jax
import jax.numpy as jnp
from jax import lax
from jax.experimental import pallas as pl
from jax.experimental.pallas import tpu as pltpu

F32 = jnp.float32
BF16 = jnp.bfloat16

N_DEV = 8
EPS = 1e-6
CONV_WIDTH = 31
CONV_PAD = 16
POOL_WINDOWS = (2, 4, 8, 16)
POOL_TAPS = 16
MLA_HEADS = 8
MLA_NOPE = 128
MLA_ROPE = 64
MLA_Q_RANK = 384
MLA_KV_RANK = 256
MLA_SCALE = (MLA_NOPE + MLA_ROPE) ** -0.5
ROPE_THETA = 10000.0
GRID_W = 64
HEAD_W = 256
CHUNK = 128
CHUNK_GROUPS = 8
ADAM_LR = 0.001
ADAM_B1 = 0.9
ADAM_B2 = 0.999
ADAM_EPS = 1e-08
ADAM_WD = 0.01
ADAM_STEP = 10
LANES = 128
VMEM_LIMIT = 56 * 1024 * 1024
PACK_COLS = 1024

WEIGHTS = ['c_ctx', 'norm_g', 'w_mod', 'b_mod', 'cv_w_in', 'cv_dw', 'cv_db', 'cv_ln_g', 'cv_ln_b', 'cv_w_out',
           'pl_w_in', 'pl_w_grp', 'pl_scale', 'pl_w_out', 'ml_w_in', 'ml_q_norm', 'ml_kv_norm', 'ml_w_uq',
           'ml_w_ukv', 'ml_nope_norm', 'ml_rope_norm', 'ml_w_out', 'ch_w_in', 'ch_ln_g', 'ch_ln_b', 'ch_w_s',
           'ch_b_s', 'ch_w_out']
SHARD_AXIS = {'c_ctx': None, 'norm_g': None, 'w_mod': 2, 'b_mod': None, 'cv_w_in': 2, 'cv_dw': 2, 'cv_db': None,
              'cv_ln_g': None, 'cv_ln_b': None, 'cv_w_out': 1, 'pl_w_in': 2, 'pl_w_grp': 2, 'pl_scale': 1,
              'pl_w_out': 1, 'ml_w_in': 2, 'ml_q_norm': 1, 'ml_kv_norm': 1, 'ml_w_uq': 2, 'ml_w_ukv': 2,
              'ml_nope_norm': None, 'ml_rope_norm': None, 'ml_w_out': 1, 'ch_w_in': 2, 'ch_ln_g': 1, 'ch_ln_b': 1,
              'ch_w_s': None, 'ch_b_s': None, 'ch_w_out': 1}
MATMUL_WEIGHTS = ['cv_w_in', 'cv_w_out', 'pl_w_in', 'pl_w_grp', 'pl_w_out', 'ml_w_in', 'ml_w_uq', 'ml_w_ukv',
                  'ml_w_out', 'ch_w_in', 'ch_w_out']
VECTOR_WEIGHTS = ['cv_dw', 'pl_scale', 'ml_q_norm', 'ml_kv_norm', 'ch_ln_g', 'ch_ln_b']
EXCHANGED = MATMUL_WEIGHTS[:1] + ['cv_dw'] + MATMUL_WEIGHTS[1:4] + ['pl_scale'] + MATMUL_WEIGHTS[4:6] + [
    'ml_q_norm', 'ml_kv_norm'] + MATMUL_WEIGHTS[6:10] + ['ch_ln_g', 'ch_ln_b', 'ch_w_out']
REPLICATED = ['c_ctx', 'norm_g', 'cv_db', 'cv_ln_g', 'cv_ln_b', 'ml_nope_norm', 'ml_rope_norm', 'ch_w_s', 'ch_b_s']


def _pick(n, cands):
    for c in cands:
        if n % c == 0:
            return c
    raise ValueError(f"no tile for {n} among {cands}")


def _params(*sem):
    return pltpu.CompilerParams(dimension_semantics=sem, vmem_limit_bytes=VMEM_LIMIT)


def _sig(x):
    return 1.0 / (1.0 + jnp.exp(-x))


def _silu(x):
    return x * _sig(x)


def _dsilu(x):
    s = _sig(x)
    return s * (1.0 + x * (1.0 - s))


def _rowsum(v):
    return jnp.sum(v, axis=0, keepdims=True)


def _dot(a, b):
    return jnp.dot(a.astype(BF16), b.astype(BF16), preferred_element_type=F32)


def _dot_nt(a, b):
    return lax.dot_general(a.astype(BF16), b.astype(BF16), (((1,), (1,)), ((), ())), preferred_element_type=F32)


def _dot_tn(a, b):
    return lax.dot_general(a.astype(BF16), b.astype(BF16), (((0,), (0,)), ((), ())), preferred_element_type=F32)


class _Segs:
    def __init__(self, lens, tm):
        self.lens, self.tm, self.n = tuple(lens), tm, len(lens)
        self.starts, s = [], 0
        for l in lens:
            assert l % tm == 0
            self.starts.append(s // tm)
            s += l
        self.rows, self.tiles = s, s // tm

    def seg(self, i):
        r = 0
        for st in self.starts[1:]:
            r = r + jnp.where(i >= st, 1, 0)
        return r

    def is_first(self, i):
        f = i == 0
        for st in self.starts[1:]:
            f = jnp.logical_or(f, i == st)
        return f

    def spec(self, cols):
        return pl.BlockSpec((None, 1, cols), lambda i: (self.seg(i), 0, 0))


def _row(tm, cols, cb=0):
    return pl.BlockSpec((tm, cols), lambda i: (i, cb))


def _const(shape):
    return pl.BlockSpec(shape, lambda *_: (0,) * len(shape))


def _sds(shape, dtype=F32):
    return jax.ShapeDtypeStruct(shape, dtype)


class _Side:
    def __init__(self, xs, out_shapes, n_remote, n_local, start, finish):
        self.xs, self.out_shapes, self.n_remote, self.n_local = list(xs), list(out_shapes), n_remote, n_local
        self.start, self.finish = start, finish


def _sem_shapes(n_remote, n_local):
    return [pltpu.SemaphoreType.DMA((n_remote,)), pltpu.SemaphoreType.DMA((n_remote,)),
            pltpu.SemaphoreType.DMA((max(n_local, 1),))]


def _pallas(body, args, *, grid, in_specs, out_specs, out_shape, sem, name, scratch_shapes=(), side=None):
    if side is None:
        return pl.pallas_call(body, grid=grid, in_specs=in_specs, out_specs=out_specs, out_shape=out_shape,
                              scratch_shapes=list(scratch_shapes), compiler_params=_params(*sem), name=name)(*args)
    multi = isinstance(out_shape, (list, tuple))
    out_specs, out_shape = (list(out_specs), list(out_shape)) if multi else ([out_specs], [out_shape])
    ni, no, ns, si, so = len(in_specs), len(out_specs), len(scratch_shapes), len(side.xs), len(side.out_shapes)
    hbm = pl.BlockSpec(memory_space=pltpu.HBM)

    def wrapped(*refs):
        ins, sins, refs = refs[:ni], refs[ni:ni + si], refs[ni + si:]
        outs, souts, refs = refs[:no], refs[no:no + so], refs[no + so:]
        scr, sems = refs[:ns], refs[ns:]
        ids = [pl.program_id(k) for k in range(len(grid))]
        first = functools.reduce(jnp.logical_and, [i == 0 for i in ids])
        last = functools.reduce(jnp.logical_and, [i == n - 1 for i, n in zip(ids, grid)])

        @pl.when(first)
        def _():
            side.start(sins, souts, *sems)

        body(*ins, *outs, *scr)

        @pl.when(last)
        def _():
            side.finish(sins, souts, *sems)

    res = pl.pallas_call(
        wrapped, grid=grid, in_specs=list(in_specs) + [hbm] * si, out_specs=out_specs + [hbm] * so,
        out_shape=out_shape + side.out_shapes,
        scratch_shapes=list(scratch_shapes) + _sem_shapes(side.n_remote, side.n_local),
        compiler_params=pltpu.CompilerParams(dimension_semantics=("arbitrary",) * len(grid),
                                             vmem_limit_bytes=VMEM_LIMIT, has_side_effects=True),
        name=name)(*args, *side.xs)
    return (list(res[:no]) if multi else res[0]), list(res[no:])


N_TILES = (1024, 896, 768, 512, 384, 256, 128)


def _mm(a, b, name, out_dtype=F32, rows=None, side=None):
    m, k, n = rows or a.shape[0], a.shape[1], b.shape[1]
    tm, tn = _pick(m, (512, 256, 128)), _pick(n, N_TILES)

    def body(a_ref, b_ref, o_ref):
        o_ref[...] = _dot(a_ref[...], b_ref[...]).astype(o_ref.dtype)

    return _pallas(
        body, (a, b), grid=(n // tn, m // tm),
        in_specs=[pl.BlockSpec((tm, k), lambda j, i: (i, 0)), pl.BlockSpec((k, tn), lambda j, i: (0, j))],
        out_specs=pl.BlockSpec((tm, tn), lambda j, i: (i, j)), out_shape=_sds((m, n), out_dtype),
        sem=("parallel", "parallel"), name=name, side=side)


def _mm_nt(a, b, name, out_dtype=F32):
    m, k, n = a.shape[0], a.shape[1], b.shape[0]
    tm, tn = _pick(m, (512, 256, 128)), _pick(n, N_TILES)

    def body(a_ref, b_ref, o_ref):
        o_ref[...] = _dot_nt(a_ref[...], b_ref[...]).astype(o_ref.dtype)

    return pl.pallas_call(
        body, grid=(n // tn, m // tm),
        in_specs=[pl.BlockSpec((tm, k), lambda j, i: (i, 0)), pl.BlockSpec((tn, k), lambda j, i: (j, 0))],
        out_specs=pl.BlockSpec((tm, tn), lambda j, i: (i, j)), out_shape=_sds((m, n), out_dtype),
        compiler_params=_params("parallel", "parallel"), name=name)(a, b)


def _mm_tn(a, b, name, rows=None, shards=None, side=None):
    t, k, n = rows or a.shape[0], a.shape[1], b.shape[1]
    tk, tt = _pick(k, N_TILES), _pick(t, (512, 256, 128))
    tn = n // shards if shards else _pick(n, N_TILES)
    assert tn % LANES == 0

    def body(a_ref, b_ref, o_ref, acc_ref):
        @pl.when(pl.program_id(2) == 0)
        def _():
            acc_ref[...] = jnp.zeros_like(acc_ref)

        acc_ref[...] += _dot_tn(a_ref[...], b_ref[...])

        @pl.when(pl.program_id(2) == pl.num_programs(2) - 1)
        def _():
            o_ref[...] = acc_ref[...].astype(o_ref.dtype)

    if shards:
        out_spec, out_shape = pl.BlockSpec((None, tk, tn), lambda i, j, s: (j, i, 0)), _sds((shards, k, tn), BF16)
    else:
        out_spec, out_shape = pl.BlockSpec((tk, tn), lambda i, j, s: (i, j)), _sds((k, n))
    return _pallas(
        body, (a, b), grid=(k // tk, n // tn, t // tt),
        in_specs=[pl.BlockSpec((tt, tk), lambda i, j, s: (s, i)), pl.BlockSpec((tt, tn), lambda i, j, s: (s, j))],
        out_specs=out_spec, out_shape=out_shape, scratch_shapes=[pltpu.VMEM((tk, tn), F32)],
        sem=("parallel", "parallel", "arbitrary"), name=name, side=side)


def _rms_mod_fwd(x, g, sc, sh, segs, name):
    d, tm = x.shape[1], segs.tm

    def body(x_ref, g_ref, sc_ref, sh_ref, h_ref):
        xf = x_ref[...]
        r = lax.rsqrt(jnp.mean(xf * xf, axis=-1, keepdims=True) + EPS)
        h_ref[...] = ((xf * r * g_ref[...]) * (1.0 + sc_ref[...]) + sh_ref[...]).astype(h_ref.dtype)

    return pl.pallas_call(
        body, grid=(segs.tiles,), in_specs=[_row(tm, d), _const((1, d)), segs.spec(d), segs.spec(d)],
        out_specs=_row(tm, d), out_shape=_sds((segs.rows, d), BF16), compiler_params=_params("parallel"),
        name=name)(x, g, sc, sh)


def _rms_mod_bwd(x, g, sc, sh, dh, dxr, segs, name):
    d, tm = x.shape[1], segs.tm

    def body(x_ref, g_ref, sc_ref, sh_ref, dh_ref, dxr_ref, dx_ref, dg_ref, dsc_ref, dsh_ref):
        i = pl.program_id(0)

        @pl.when(i == 0)
        def _():
            dg_ref[...] = jnp.zeros_like(dg_ref)

        @pl.when(segs.is_first(i))
        def _():
            dsc_ref[...] = jnp.zeros_like(dsc_ref)
            dsh_ref[...] = jnp.zeros_like(dsh_ref)

        xf, gg, dhf = x_ref[...], g_ref[...], dh_ref[...].astype(F32)
        r = lax.rsqrt(jnp.mean(xf * xf, axis=-1, keepdims=True) + EPS)
        xh = xf * r
        dsh_ref[...] += _rowsum(dhf)
        dsc_ref[...] += _rowsum(dhf * (xh * gg))
        du = dhf * (1.0 + sc_ref[...])
        dg_ref[...] += _rowsum(du * xh)
        dxh = du * gg
        dx_ref[...] = dxr_ref[...] + r * (dxh - xh * jnp.mean(dxh * xh, axis=-1, keepdims=True))

    return pl.pallas_call(
        body, grid=(segs.tiles,),
        in_specs=[_row(tm, d), _const((1, d)), segs.spec(d), segs.spec(d), _row(tm, d), _row(tm, d)],
        out_specs=[_row(tm, d), _const((1, d)), segs.spec(d), segs.spec(d)],
        out_shape=[_sds((segs.rows, d)), _sds((1, d)), _sds((segs.n, 1, d)), _sds((segs.n, 1, d))],
        compiler_params=_params("arbitrary"), name=name)(x, g, sc, sh, dh, dxr)


def _resid_fwd(x, o, gt, segs, name):
    d, tm = x.shape[1], segs.tm

    def body(x_ref, o_ref, gt_ref, y_ref):
        y_ref[...] = x_ref[...] + gt_ref[...] * o_ref[...]

    return pl.pallas_call(
        body, grid=(segs.tiles,), in_specs=[_row(tm, d), _row(tm, d), segs.spec(d)], out_specs=_row(tm, d),
        out_shape=_sds((segs.rows, d)), compiler_params=_params("parallel"), name=name)(x, o, gt)


def _resid_bwd(dxn, o, gt, segs, name):
    d, tm = o.shape[1], segs.tm

    def body(dxn_ref, o_ref, gt_ref, do_ref, dgt_ref):
        @pl.when(segs.is_first(pl.program_id(0)))
        def _():
            dgt_ref[...] = jnp.zeros_like(dgt_ref)

        dx = dxn_ref[...]
        do_ref[...] = (gt_ref[...] * dx).astype(do_ref.dtype)
        dgt_ref[...] += _rowsum(dx * o_ref[...])

    return pl.pallas_call(
        body, grid=(segs.tiles,), in_specs=[_row(tm, d), _row(tm, d), segs.spec(d)],
        out_specs=[_row(tm, d), segs.spec(d)], out_shape=[_sds((segs.rows, d), BF16), _sds((segs.n, 1, d))],
        compiler_params=_params("arbitrary"), name=name)(dxn, o, gt)


def _loss_head(y, tgt, tm, name):
    t, d = y.shape

    def body(y_ref, t_ref, l_ref, dy_ref):
        @pl.when(pl.program_id(0) == 0)
        def _():
            l_ref[...] = jnp.zeros_like(l_ref)

        e = y_ref[...] - t_ref[...]
        dy_ref[...] = e * (1.0 / d)
        l_ref[...] += _rowsum(e * e) * (0.5 / d)

    return pl.pallas_call(
        body, grid=(t // tm,), in_specs=[_row(tm, d), _row(tm, d)], out_specs=[_const((1, d)), _row(tm, d)],
        out_shape=[_sds((1, d)), _sds((t, d))], compiler_params=_params("arbitrary"), name=name)(y, tgt)


def _seq_spec(l, ce, row0, cb0=0):
    return pl.BlockSpec((l, ce), lambda j, s: (row0 // l + s, cb0 + j))


def _tap_sum(pad_ref, taps_ref, first_row, n_taps, l, ce, flip):
    out = []
    for r0 in range(0, l, CHUNK):
        rows = min(CHUNK, l - r0)
        acc = jnp.zeros((rows, ce), F32)
        for k in range(n_taps):
            kk = n_taps - 1 - k if flip else k
            acc = acc + pad_ref[pl.ds(first_row + r0 + k, rows), :] * taps_ref[kk:kk + 1, :]
        out.append(acc)
    return out


def _fill_pad(pad_ref, val, l, ce):
    pad_ref[pl.ds(0, CONV_PAD), :] = jnp.zeros((CONV_PAD, ce), F32)
    pad_ref[pl.ds(CONV_PAD + l, CONV_PAD), :] = jnp.zeros((CONV_PAD, ce), F32)
    pad_ref[pl.ds(CONV_PAD, l), :] = val


def _conv1_fwd(z, dw, db, nseq, l, row0, name, side=None):
    e = z.shape[1] // 3
    ce = LANES
    half = CONV_WIDTH // 2

    def body(a_ref, b_ref, dw_ref, db_ref, y_ref, pad_ref):
        _fill_pad(pad_ref, a_ref[...] * _sig(b_ref[...]), l, ce)
        pieces = _tap_sum(pad_ref, dw_ref, CONV_PAD - half, CONV_WIDTH, l, ce, False)
        for n, acc in enumerate(pieces):
            y_ref[pl.ds(n * CHUNK, acc.shape[0]), :] = acc + db_ref[...]

    return _pallas(
        body, (z, z, dw, db), grid=(e // ce, nseq),
        in_specs=[_seq_spec(l, ce, row0), _seq_spec(l, ce, row0, e // ce),
                  pl.BlockSpec((CONV_WIDTH, ce), lambda j, s: (0, j)), pl.BlockSpec((1, ce), lambda j, s: (0, j))],
        out_specs=pl.BlockSpec((l, ce), lambda j, s: (s, j)), out_shape=_sds((nseq * l, e)),
        scratch_shapes=[pltpu.VMEM((l + 2 * CONV_PAD, ce), F32)],
        sem=("parallel", "arbitrary"), name=name, side=side)


def _conv1_bwd(dy2, z, dw, acc_dw, acc_db, nseq, l, row0, name, side=None):
    e = z.shape[1] // 3
    ce = LANES
    half = CONV_WIDTH // 2

    def body(dy_ref, a_ref, b_ref, dw_ref, adw_ref, adb_ref, da_ref, dbb_ref, ddw_ref, ddb_ref, ypad_ref, dpad_ref):
        @pl.when(pl.program_id(1) == 0)
        def _():
            ddw_ref[...] = adw_ref[...]
            ddb_ref[...] = adb_ref[...]

        a, sb = a_ref[...], _sig(b_ref[...])
        dy = dy_ref[...]
        _fill_pad(ypad_ref, a * sb, l, ce)
        _fill_pad(dpad_ref, dy, l, ce)
        ddb_ref[...] += _rowsum(dy)
        for k in range(CONV_WIDTH):
            ddw_ref[k:k + 1, :] += _rowsum(dy * ypad_ref[pl.ds(CONV_PAD - half + k, l), :])
        pieces = _tap_sum(dpad_ref, dw_ref, CONV_PAD - half, CONV_WIDTH, l, ce, True)
        for n, dy1 in enumerate(pieces):
            rows = pl.ds(n * CHUNK, dy1.shape[0])
            sbn = sb[n * CHUNK:n * CHUNK + dy1.shape[0], :]
            da_ref[rows, :] = (dy1 * sbn).astype(da_ref.dtype)
            dbb_ref[rows, :] = (dy1 * a[n * CHUNK:n * CHUNK + dy1.shape[0], :] * sbn * (1.0 - sbn)).astype(dbb_ref.dtype)

    cw = lambda j, s: (0, j)
    return _pallas(
        body, (dy2, z, z, dw, acc_dw, acc_db), grid=(e // ce, nseq),
        in_specs=[_seq_spec(l, ce, row0), _seq_spec(l, ce, row0), _seq_spec(l, ce, row0, e // ce),
                  pl.BlockSpec((CONV_WIDTH, ce), cw), pl.BlockSpec((CONV_WIDTH, ce), cw), pl.BlockSpec((1, ce), cw)],
        out_specs=[pl.BlockSpec((l, ce), lambda j, s: (s, j)), pl.BlockSpec((l, ce), lambda j, s: (s, j)),
                   pl.BlockSpec((CONV_WIDTH, ce), cw), pl.BlockSpec((1, ce), cw)],
        out_shape=[_sds((nseq * l, e), BF16), _sds((nseq * l, e), BF16), _sds((CONV_WIDTH, e)), _sds((1, e))],
        scratch_shapes=[pltpu.VMEM((l + 2 * CONV_PAD, ce), F32), pltpu.VMEM((l + 2 * CONV_PAD, ce), F32)],
        sem=("parallel", "arbitrary"), name=name, side=side)


def _pool_tables(l, e):
    grp = e // len(POOL_WINDOWS)
    w = jnp.repeat(jnp.array(POOL_WINDOWS, jnp.int32), grp)[None, :]
    off = jnp.arange(POOL_TAPS, dtype=jnp.int32)[:, None] - POOL_TAPS // 2
    taps = jnp.logical_and(off >= -(w // 2), off < w - w // 2).astype(F32)
    t = jnp.arange(l, dtype=jnp.int32)[:, None]
    cnt = jnp.clip(t + (w - w // 2), 0, l) - jnp.clip(t - w // 2, 0, l)
    return taps, 1.0 / cnt.astype(F32)


def _pool1(v_src, taps, inv_cnt, nseq, l, row0, transpose, name, out_dtype):
    e = taps.shape[1]
    ce = LANES
    half = POOL_TAPS // 2

    def body(v_ref, taps_ref, ic_ref, o_ref, pad_ref):
        v = v_ref[...].astype(F32)
        if transpose:
            _fill_pad(pad_ref, v * ic_ref[...], l, ce)
            pieces = _tap_sum(pad_ref, taps_ref, CONV_PAD - half + 1, POOL_TAPS, l, ce, True)
        else:
            _fill_pad(pad_ref, v, l, ce)
            pieces = _tap_sum(pad_ref, taps_ref, CONV_PAD - half, POOL_TAPS, l, ce, False)
        for n, acc in enumerate(pieces):
            rows = pl.ds(n * CHUNK, acc.shape[0])
            vn = v[n * CHUNK:n * CHUNK + acc.shape[0], :]
            if transpose:
                o_ref[rows, :] = (acc - vn).astype(o_ref.dtype)
            else:
                o_ref[rows, :] = (acc * ic_ref[rows, :] - vn).astype(o_ref.dtype)

    return pl.pallas_call(
        body, grid=(e // ce, nseq),
        in_specs=[_seq_spec(l, ce, row0), pl.BlockSpec((POOL_TAPS, ce), lambda j, s: (0, j)),
                  pl.BlockSpec((l, ce), lambda j, s: (0, j))],
        out_specs=pl.BlockSpec((l, ce), lambda j, s: (s, j)), out_shape=_sds((nseq * l, e), out_dtype),
        scratch_shapes=[pltpu.VMEM((l + 2 * CONV_PAD, ce), F32)],
        compiler_params=_params("parallel", "arbitrary"), name=name)(v_src, taps, inv_cnt)


def _layernorm_parts(x, eps=EPS):
    mu = jnp.mean(x, axis=-1, keepdims=True)
    xc = x - mu
    r = lax.rsqrt(jnp.mean(xc * xc, axis=-1, keepdims=True) + eps)
    return xc * r, r


def _layernorm_bwd(dy, xh, r, g):
    dxh = dy * g
    return r * (dxh - jnp.mean(dxh, axis=-1, keepdims=True) - xh * jnp.mean(dxh * xh, axis=-1, keepdims=True))


def _conv2_fwd(y2, z, ln_g, ln_b, tm, name):
    t, e = y2.shape

    def body(y_ref, g_ref, lg_ref, lb_ref, o_ref):
        xh, _ = _layernorm_parts(y_ref[...])
        o_ref[...] = (_silu(xh * lg_ref[...] + lb_ref[...]) * _silu(g_ref[...])).astype(o_ref.dtype)

    return pl.pallas_call(
        body, grid=(t // tm,), in_specs=[_row(tm, e), _row(tm, e, 2), _const((1, e)), _const((1, e))],
        out_specs=_row(tm, e), out_shape=_sds((t, e), BF16), compiler_params=_params("parallel"),
        name=name)(y2, z, ln_g, ln_b)


def _conv2_bwd(dy4, y2, z, ln_g, ln_b, tm, name):
    t, e = y2.shape

    def body(dy_ref, y_ref, g_ref, lg_ref, lb_ref, dy2_ref, dg_ref, dlg_ref, dlb_ref):
        @pl.when(pl.program_id(0) == 0)
        def _():
            dlg_ref[...] = jnp.zeros_like(dlg_ref)
            dlb_ref[...] = jnp.zeros_like(dlb_ref)

        dy, gz = dy_ref[...], g_ref[...]
        xh, r = _layernorm_parts(y_ref[...])
        y3 = xh * lg_ref[...] + lb_ref[...]
        dg_ref[...] = (dy * _silu(y3) * _dsilu(gz)).astype(dg_ref.dtype)
        dy3 = dy * _silu(gz) * _dsilu(y3)
        dlg_ref[...] += _rowsum(dy3 * xh)
        dlb_ref[...] += _rowsum(dy3)
        dy2_ref[...] = _layernorm_bwd(dy3, xh, r, lg_ref[...])

    return pl.pallas_call(
        body, grid=(t // tm,),
        in_specs=[_row(tm, e), _row(tm, e), _row(tm, e, 2), _const((1, e)), _const((1, e))],
        out_specs=[_row(tm, e), _row(tm, e), _const((1, e)), _const((1, e))],
        out_shape=[_sds((t, e)), _sds((t, e), BF16), _sds((1, e)), _sds((1, e))],
        compiler_params=_params("arbitrary"), name=name)(dy4, y2, z, ln_g, ln_b)


def _pool2_fwd(pm, w_grp, scale, z, tm, name):
    t, e = pm.shape
    ng, gw = w_grp.shape[0], w_grp.shape[1]

    def body(pm_ref, w_ref, sc_ref, g_ref, o_ref):
        for k in range(ng):
            cols = slice(k * gw, (k + 1) * gw)
            y = _dot(pm_ref[:, cols], w_ref[k])
            o_ref[:, cols] = (y * sc_ref[:, cols] * _silu(g_ref[:, cols])).astype(o_ref.dtype)

    return pl.pallas_call(
        body, grid=(t // tm,), in_specs=[_row(tm, e), _const(w_grp.shape), _const((1, e)), _row(tm, e, 1)],
        out_specs=_row(tm, e), out_shape=_sds((t, e), BF16), compiler_params=_params("parallel"),
        name=name)(pm, w_grp, scale, z)


def _pool2_bwd(dy2, pm, w_grp, scale, z, tm, name):
    t, e = pm.shape
    ng, gw = w_grp.shape[0], w_grp.shape[1]

    def body(dy_ref, pm_ref, w_ref, sc_ref, g_ref, dpm_ref, dg_ref, dsc_ref, dw_ref):
        @pl.when(pl.program_id(0) == 0)
        def _():
            dsc_ref[...] = jnp.zeros_like(dsc_ref)
            dw_ref[...] = jnp.zeros_like(dw_ref)

        for k in range(ng):
            cols = slice(k * gw, (k + 1) * gw)
            dy, gz, sc, pmk = dy_ref[:, cols], g_ref[:, cols], sc_ref[:, cols], pm_ref[:, cols]
            y = _dot(pmk, w_ref[k])
            dg_ref[:, cols] = (dy * (y * sc) * _dsilu(gz)).astype(dg_ref.dtype)
            dys = dy * _silu(gz)
            dsc_ref[:, cols] += _rowsum(dys * y)
            dyk = dys * sc
            dpm_ref[:, cols] = _dot_nt(dyk, w_ref[k])
            dw_ref[k] += _dot_tn(pmk, dyk)

    return pl.pallas_call(
        body, grid=(t // tm,),
        in_specs=[_row(tm, e), _row(tm, e), _const(w_grp.shape), _const((1, e)), _row(tm, e, 1)],
        out_specs=[_row(tm, e), _row(tm, e), _const((1, e)), _const(w_grp.shape)],
        out_shape=[_sds((t, e)), _sds((t, e), BF16), _sds((1, e)), _sds(w_grp.shape)],
        compiler_params=_params("arbitrary"), name=name)(dy2, pm, w_grp, scale, z)


def _rms_f(x, g, n):
    r = lax.rsqrt(jnp.sum(x * x, axis=-1, keepdims=True) * (1.0 / n) + EPS)
    return x * r * g


def _rms_b(x, g, dy, n):
    r = lax.rsqrt(jnp.sum(x * x, axis=-1, keepdims=True) * (1.0 / n) + EPS)
    xh = x * r
    dxh = dy * g
    return r * (dxh - xh * (jnp.sum(dxh * xh, axis=-1, keepdims=True) * (1.0 / n))), dy * xh


def _swap16(x):
    lane = lax.broadcasted_iota(jnp.int32, x.shape, 1)
    return jnp.where(lane % 32 < 16, pltpu.roll(x, LANES - 16, 1), pltpu.roll(x, 16, 1))


def _rope(x, c, s):
    return x * c + _swap16(x) * s


def _rope_t(dy, c, s):
    return dy * c + _swap16(dy * s)


def _rope_tables(l, lc, nb):
    t = jnp.arange(l, dtype=jnp.int32)
    row_id, col_id = (t // GRID_W).astype(F32), (t % GRID_W).astype(F32)
    axis_dim = MLA_ROPE // 2
    freqs = ROPE_THETA ** (-jnp.arange(0, axis_dim, 2, dtype=F32) / axis_dim)
    ar, ac = row_id[:, None] * freqs, col_id[:, None] * freqs
    pad1, pad0 = jnp.ones((l, LANES - MLA_ROPE), F32), jnp.zeros((l, LANES - MLA_ROPE), F32)
    ctab = jnp.concatenate([jnp.cos(ar), jnp.cos(ar), jnp.cos(ac), jnp.cos(ac), pad1], axis=1)
    stab = jnp.concatenate([-jnp.sin(ar), jnp.sin(ar), -jnp.sin(ac), jnp.sin(ac), pad0], axis=1)
    ctab = jnp.concatenate([jnp.tile(ctab, (nb, 1)), jnp.ones((nb * lc, LANES), F32)], axis=0)
    stab = jnp.concatenate([jnp.tile(stab, (nb, 1)), jnp.zeros((nb * lc, LANES), F32)], axis=0)
    return ctab, stab


def _kv_pre_fwd(zkv, kv_norm, rope_g, ctab, stab, tm, name):
    t = zkv.shape[0]

    def body(z_ref, gk_ref, gr_ref, c_ref, s_ref, ck_ref, kr_ref):
        ck_ref[...] = _rms_f(z_ref[:, :MLA_KV_RANK], gk_ref[...], MLA_KV_RANK).astype(ck_ref.dtype)
        kr = _rms_f(z_ref[:, MLA_KV_RANK:], gr_ref[...], MLA_ROPE)
        kr_ref[...] = _rope(kr, c_ref[...], s_ref[...]).astype(kr_ref.dtype)

    w = MLA_KV_RANK + LANES
    return pl.pallas_call(
        body, grid=(t // tm,),
        in_specs=[_row(tm, w), _const((1, MLA_KV_RANK)), _const((1, LANES)), _row(tm, LANES), _row(tm, LANES)],
        out_specs=[_row(tm, MLA_KV_RANK), _row(tm, LANES)],
        out_shape=[_sds((t, MLA_KV_RANK), BF16), _sds((t, LANES), BF16)],
        compiler_params=_params("parallel"), name=name)(zkv, kv_norm, rope_g, ctab, stab)


def _kv_pre_bwd(dck, dkr, zkv, kv_norm, rope_g, ctab, stab, tm, name):
    t = zkv.shape[0]
    w = MLA_KV_RANK + LANES

    def body(dck_ref, dkr_ref, z_ref, gk_ref, gr_ref, c_ref, s_ref, dz_ref, dgk_ref, dgr_ref):
        @pl.when(pl.program_id(0) == 0)
        def _():
            dgk_ref[...] = jnp.zeros_like(dgk_ref)
            dgr_ref[...] = jnp.zeros_like(dgr_ref)

        dx, dg = _rms_b(z_ref[:, :MLA_KV_RANK], gk_ref[...], dck_ref[...], MLA_KV_RANK)
        dz_ref[:, :MLA_KV_RANK] = dx.astype(dz_ref.dtype)
        dgk_ref[...] += _rowsum(dg)
        dy = _rope_t(dkr_ref[...], c_ref[...], s_ref[...])
        dx, dg = _rms_b(z_ref[:, MLA_KV_RANK:], gr_ref[...], dy, MLA_ROPE)
        dz_ref[:, MLA_KV_RANK:] = dx.astype(dz_ref.dtype)
        dgr_ref[...] += _rowsum(dg)

    return pl.pallas_call(
        body, grid=(t // tm,),
        in_specs=[_row(tm, MLA_KV_RANK), _row(tm, LANES), _row(tm, w), _const((1, MLA_KV_RANK)), _const((1, LANES)),
                  _row(tm, LANES), _row(tm, LANES)],
        out_specs=[_row(tm, w), _const((1, MLA_KV_RANK)), _const((1, LANES))],
        out_shape=[_sds((t, w), BF16), _sds((1, MLA_KV_RANK)), _sds((1, LANES))],
        compiler_params=_params("arbitrary"), name=name)(dck, dkr, zkv, kv_norm, rope_g, ctab, stab)


def _q_pre_fwd(zq, q_norm, tm, name):
    t, w = zq.shape

    def body(z_ref, g_ref, o_ref):
        o_ref[...] = _rms_f(z_ref[...], g_ref[...], w).astype(o_ref.dtype)

    return pl.pallas_call(
        body, grid=(t // tm,), in_specs=[_row(tm, w), _const((1, w))], out_specs=_row(tm, w),
        out_shape=_sds((t, w), BF16), compiler_params=_params("parallel"), name=name)(zq, q_norm)


def _q_pre_bwd(dcq, zq, q_norm, tm, name):
    t, w = zq.shape

    def body(d_ref, z_ref, g_ref, dz_ref, dg_ref):
        @pl.when(pl.program_id(0) == 0)
        def _():
            dg_ref[...] = jnp.zeros_like(dg_ref)

        dx, dg = _rms_b(z_ref[...], g_ref[...], d_ref[...], w)
        dz_ref[...] = dx.astype(dz_ref.dtype)
        dg_ref[...] += _rowsum(dg)

    return pl.pallas_call(
        body, grid=(t // tm,), in_specs=[_row(tm, w), _row(tm, w), _const((1, w))],
        out_specs=[_row(tm, w), _const((1, w))], out_shape=[_sds((t, w), BF16), _sds((1, w))],
        compiler_params=_params("arbitrary"), name=name)(dcq, zq, q_norm)


def _q_post_fwd(q, nope_g, rope_g, ctab, stab, tm, name):
    t, w = q.shape

    def body(q_ref, gn_ref, gr_ref, c_ref, s_ref, o_ref):
        for h in range(MLA_HEADS):
            a = h * HEAD_W
            o_ref[:, a:a + LANES] = _rms_f(q_ref[:, a:a + LANES], gn_ref[...], MLA_NOPE).astype(o_ref.dtype)
            qr = _rms_f(q_ref[:, a + LANES:a + HEAD_W], gr_ref[...], MLA_ROPE)
            o_ref[:, a + LANES:a + HEAD_W] = _rope(qr, c_ref[...], s_ref[...]).astype(o_ref.dtype)

    return pl.pallas_call(
        body, grid=(t // tm,),
        in_specs=[_row(tm, w), _const((1, LANES)), _const((1, LANES)), _row(tm, LANES), _row(tm, LANES)],
        out_specs=_row(tm, w), out_shape=_sds((t, w), BF16), compiler_params=_params("parallel"),
        name=name)(q, nope_g, rope_g, ctab, stab)


def _q_post_bwd(dqf, q, nope_g, rope_g, ctab, stab, tm, name):
    t, w = q.shape

    def body(d_ref, q_ref, gn_ref, gr_ref, c_ref, s_ref, dq_ref, dgn_ref, dgr_ref):
        @pl.when(pl.program_id(0) == 0)
        def _():
            dgn_ref[...] = jnp.zeros_like(dgn_ref)
            dgr_ref[...] = jnp.zeros_like(dgr_ref)

        for h in range(MLA_HEADS):
            a = h * HEAD_W
            dx, dg = _rms_b(q_ref[:, a:a + LANES], gn_ref[...], d_ref[:, a:a + LANES], MLA_NOPE)
            dq_ref[:, a:a + LANES] = dx.astype(dq_ref.dtype)
            dgn_ref[...] += _rowsum(dg)
            dy = _rope_t(d_ref[:, a + LANES:a + HEAD_W], c_ref[...], s_ref[...])
            dx, dg = _rms_b(q_ref[:, a + LANES:a + HEAD_W], gr_ref[...], dy, MLA_ROPE)
            dq_ref[:, a + LANES:a + HEAD_W] = dx.astype(dq_ref.dtype)
            dgr_ref[...] += _rowsum(dg)

    return pl.pallas_call(
        body, grid=(t // tm,),
        in_specs=[_row(tm, w), _row(tm, w), _const((1, LANES)), _const((1, LANES)), _row(tm, LANES), _row(tm, LANES)],
        out_specs=[_row(tm, w), _const((1, LANES)), _const((1, LANES))],
        out_shape=[_sds((t, w), BF16), _sds((1, LANES)), _sds((1, LANES))],
        compiler_params=_params("arbitrary"), name=name)(dqf, q, nope_g, rope_g, ctab, stab)


def _k_post_fwd(kv, krr, nope_g, tm, name):
    t, w = kv.shape

    def body(kv_ref, kr_ref, gn_ref, k_ref, v_ref):
        for h in range(MLA_HEADS):
            a = h * HEAD_W
            k_ref[:, a:a + LANES] = _rms_f(kv_ref[:, a:a + LANES], gn_ref[...], MLA_NOPE).astype(k_ref.dtype)
            k_ref[:, a + LANES:a + HEAD_W] = kr_ref[...]
            v_ref[:, h * LANES:(h + 1) * LANES] = kv_ref[:, a + LANES:a + HEAD_W].astype(v_ref.dtype)

    return pl.pallas_call(
        body, grid=(t // tm,), in_specs=[_row(tm, w), _row(tm, LANES), _const((1, LANES))],
        out_specs=[_row(tm, w), _row(tm, w // 2)], out_shape=[_sds((t, w), BF16), _sds((t, w // 2), BF16)],
        compiler_params=_params("parallel"), name=name)(kv, krr, nope_g)


def _k_post_bwd(dkf, dvf, kv, nope_g, tm, name):
    t, w = kv.shape

    def body(dk_ref, dv_ref, kv_ref, gn_ref, dkv_ref, dkr_ref, dgn_ref):
        @pl.when(pl.program_id(0) == 0)
        def _():
            dgn_ref[...] = jnp.zeros_like(dgn_ref)

        dkr = jnp.zeros(dkr_ref.shape, F32)
        for h in range(MLA_HEADS):
            a = h * HEAD_W
            dx, dg = _rms_b(kv_ref[:, a:a + LANES], gn_ref[...], dk_ref[:, a:a + LANES], MLA_NOPE)
            dkv_ref[:, a:a + LANES] = dx.astype(dkv_ref.dtype)
            dgn_ref[...] += _rowsum(dg)
            dkv_ref[:, a + LANES:a + HEAD_W] = dv_ref[:, h * LANES:(h + 1) * LANES].astype(dkv_ref.dtype)
            dkr = dkr + dk_ref[:, a + LANES:a + HEAD_W]
        dkr_ref[...] = dkr

    return pl.pallas_call(
        body, grid=(t // tm,), in_specs=[_row(tm, w), _row(tm, w // 2), _row(tm, w), _const((1, LANES))],
        out_specs=[_row(tm, w), _row(tm, LANES), _const((1, LANES))],
        out_shape=[_sds((t, w), BF16), _sds((t, LANES)), _sds((1, LANES))],
        compiler_params=_params("arbitrary"), name=name)(dkf, dvf, kv, nope_g)


def _attn_specs(nb, l, lc, tq):
    nq = l // tq
    ctx0 = nb * l // lc
    q_spec = lambda w: pl.BlockSpec((tq, w), lambda b, h, i: (b * nq + i, h))
    lat = lambda w: pl.BlockSpec((l, w), lambda b, h, i: (b, h))
    ctx = lambda w: pl.BlockSpec((lc, w), lambda b, h, i: (ctx0 + b, h))
    return nq, q_spec, lat, ctx


def _attn_fwd(qf, kf, vf, nb, l, lc, name):
    tq = _pick(l, (256, 128))
    nq, q_spec, lat, ctx = _attn_specs(nb, l, lc, tq)

    def body(q_ref, kl_ref, kc_ref, vl_ref, vc_ref, o_ref, lse_ref):
        q = q_ref[...]
        s1 = _dot_nt(q, kl_ref[...]) * MLA_SCALE
        s2 = _dot_nt(q, kc_ref[...]) * MLA_SCALE
        m = jnp.maximum(jnp.max(s1, axis=-1, keepdims=True), jnp.max(s2, axis=-1, keepdims=True))
        p1, p2 = jnp.exp(s1 - m), jnp.exp(s2 - m)
        den = jnp.sum(p1, axis=-1, keepdims=True) + jnp.sum(p2, axis=-1, keepdims=True)
        o_ref[...] = (_dot(p1, vl_ref[...]) + _dot(p2, vc_ref[...])) / den
        lse_ref[...] = jnp.broadcast_to(m + jnp.log(den), lse_ref.shape)

    return pl.pallas_call(
        body, grid=(nb, MLA_HEADS, nq),
        in_specs=[q_spec(HEAD_W), lat(HEAD_W), ctx(HEAD_W), lat(LANES), ctx(LANES)],
        out_specs=[q_spec(LANES), q_spec(LANES)],
        out_shape=[_sds((nb * l, MLA_HEADS * LANES)), _sds((nb * l, MLA_HEADS * LANES))],
        compiler_params=_params("parallel", "parallel", "arbitrary"), name=name)(qf, kf, kf, vf, vf)


def _attn_bwd(do, o, lse, qf, kf, vf, nb, l, lc, name, side=None):
    tq = _pick(l, (256, 128))
    nq, q_spec, lat, ctx = _attn_specs(nb, l, lc, tq)
    out_lat = lambda w: pl.BlockSpec((l, w), lambda b, h, i: (b, h))
    out_ctx = lambda w: pl.BlockSpec((lc, w), lambda b, h, i: (b, h))

    def body(do_ref, o_ref, lse_ref, q_ref, kl_ref, kc_ref, vl_ref, vc_ref, dq_ref, dkl_ref, dkc_ref, dvl_ref, dvc_ref):
        @pl.when(pl.program_id(2) == 0)
        def _():
            dkl_ref[...] = jnp.zeros_like(dkl_ref)
            dkc_ref[...] = jnp.zeros_like(dkc_ref)
            dvl_ref[...] = jnp.zeros_like(dvl_ref)
            dvc_ref[...] = jnp.zeros_like(dvc_ref)

        q, dof = q_ref[...], do_ref[...]
        delta = jnp.sum(dof * o_ref[...], axis=-1, keepdims=True)
        lse = lse_ref[:, :1]
        dq = jnp.zeros(dq_ref.shape, F32)
        for k_ref, v_ref, dk_ref, dv_ref in ((kl_ref, vl_ref, dkl_ref, dvl_ref), (kc_ref, vc_ref, dkc_ref, dvc_ref)):
            p = jnp.exp(_dot_nt(q, k_ref[...]) * MLA_SCALE - lse)
            ds = p * (_dot_nt(dof, v_ref[...]) - delta) * MLA_SCALE
            dq = dq + _dot(ds, k_ref[...])
            dk_ref[...] += _dot_tn(ds, q)
            dv_ref[...] += _dot_tn(p, dof)
        dq_ref[...] = dq

    kw, vw = MLA_HEADS * HEAD_W, MLA_HEADS * LANES
    return _pallas(
        body, (do, o, lse, qf, kf, kf, vf, vf), grid=(nb, MLA_HEADS, nq),
        in_specs=[q_spec(LANES), q_spec(LANES), q_spec(LANES), q_spec(HEAD_W), lat(HEAD_W), ctx(HEAD_W), lat(LANES),
                  ctx(LANES)],
        out_specs=[q_spec(HEAD_W), out_lat(HEAD_W), out_ctx(HEAD_W), out_lat(LANES), out_ctx(LANES)],
        out_shape=[_sds((nb * l, kw)), _sds((nb * l, kw)), _sds((nb * lc, kw)), _sds((nb * l, vw)),
                   _sds((nb * lc, vw))],
        sem=("parallel", "parallel", "arbitrary"), name=name, side=side)


def _gate_fwd(o, g, tm, name):
    t, e = o.shape

    def body(o_ref, g_ref, y_ref):
        y_ref[...] = (o_ref[...] * _silu(g_ref[...])).astype(y_ref.dtype)

    return pl.pallas_call(
        body, grid=(t // tm,), in_specs=[_row(tm, e), _row(tm, e)], out_specs=_row(tm, e),
        out_shape=_sds((t, e), BF16), compiler_params=_params("parallel"), name=name)(o, g)


def _gate_bwd(dy, o, g, tm, name):
    t, e = o.shape

    def body(dy_ref, o_ref, g_ref, do_ref, dg_ref):
        dy, gz = dy_ref[...], g_ref[...]
        do_ref[...] = dy * _silu(gz)
        dg_ref[...] = (dy * o_ref[...] * _dsilu(gz)).astype(dg_ref.dtype)

    return pl.pallas_call(
        body, grid=(t // tm,), in_specs=[_row(tm, e), _row(tm, e), _row(tm, e)],
        out_specs=[_row(tm, e), _row(tm, e)], out_shape=[_sds((t, e)), _sds((t, e), BF16)],
        compiler_params=_params("parallel"), name=name)(dy, o, g)


def _chunk_fwd(z, ln_g, ln_b, w_s, bs_full, name):
    t, e = z.shape[0], z.shape[1] // 3

    def body(u_ref, v_ref, g_ref, lg_ref, lb_ref, w_ref, bs_ref, y_ref):
        xh, _ = _layernorm_parts(v_ref[...])
        vn = xh * lg_ref[...] + lb_ref[...]
        for k in range(CHUNK_GROUPS):
            cols = slice(k * LANES, (k + 1) * LANES)
            s = _dot(w_ref[k], vn[:, cols]) + bs_ref[:, cols]
            y_ref[:, cols] = (u_ref[:, cols] * s * _silu(g_ref[:, cols])).astype(y_ref.dtype)

    return pl.pallas_call(
        body, grid=(t // CHUNK,),
        in_specs=[_row(CHUNK, e, 0), _row(CHUNK, e, 1), _row(CHUNK, e, 2), _const((1, e)), _const((1, e)),
                  _const(w_s.shape), _const((CHUNK, e))],
        out_specs=_row(CHUNK, e), out_shape=_sds((t, e), BF16), compiler_params=_params("parallel"),
        name=name)(z, z, z, ln_g, ln_b, w_s, bs_full)


def _chunk_bwd(dy, z, ln_g, ln_b, w_s, bs_full, name):
    t, e = z.shape[0], z.shape[1] // 3

    def body(dy_ref, u_ref, v_ref, g_ref, lg_ref, lb_ref, w_ref, bs_ref, dz_ref, dw_ref, dbs_ref, dlg_ref, dlb_ref,
             acc_ref):
        i = pl.program_id(0)

        @pl.when(i == 0)
        def _():
            dw_ref[...] = jnp.zeros_like(dw_ref)
            dlg_ref[...] = jnp.zeros_like(dlg_ref)
            dlb_ref[...] = jnp.zeros_like(dlb_ref)
            acc_ref[...] = jnp.zeros_like(acc_ref)

        xh, r = _layernorm_parts(v_ref[...])
        vn = xh * lg_ref[...] + lb_ref[...]
        dvn = []
        for k in range(CHUNK_GROUPS):
            cols = slice(k * LANES, (k + 1) * LANES)
            dyk, u, gz = dy_ref[:, cols], u_ref[:, cols], g_ref[:, cols]
            s = _dot(w_ref[k], vn[:, cols]) + bs_ref[:, cols]
            sg = _silu(gz)
            dz_ref[:, cols] = (dyk * s * sg).astype(dz_ref.dtype)
            dz_ref[:, 2 * e + k * LANES:2 * e + (k + 1) * LANES] = (dyk * u * s * _dsilu(gz)).astype(dz_ref.dtype)
            ds = dyk * u * sg
            acc_ref[:, cols] += ds
            dw_ref[k] += _dot_nt(ds, vn[:, cols])
            dvn.append(_dot_tn(w_ref[k], ds))
        dvn = jnp.concatenate(dvn, axis=1)
        dlg_ref[...] += _rowsum(dvn * xh)
        dlb_ref[...] += _rowsum(dvn)
        dz_ref[:, e:2 * e] = _layernorm_bwd(dvn, xh, r, lg_ref[...]).astype(dz_ref.dtype)

        @pl.when(i == pl.num_programs(0) - 1)
        def _():
            lane = lax.broadcasted_iota(jnp.int32, dbs_ref.shape, 1)
            out = jnp.zeros(dbs_ref.shape, F32)
            for k in range(CHUNK_GROUPS):
                col = jnp.sum(acc_ref[:, k * LANES:(k + 1) * LANES], axis=1, keepdims=True)
                out = jnp.where(lane == k, col, out)
            dbs_ref[...] = out

    return pl.pallas_call(
        body, grid=(t // CHUNK,),
        in_specs=[_row(CHUNK, e), _row(CHUNK, e, 0), _row(CHUNK, e, 1), _row(CHUNK, e, 2), _const((1, e)),
                  _const((1, e)), _const(w_s.shape), _const((CHUNK, e))],
        out_specs=[_row(CHUNK, 3 * e), _const(w_s.shape), _const((CHUNK, CHUNK_GROUPS)), _const((1, e)),
                   _const((1, e))],
        out_shape=[_sds((t, 3 * e), BF16), _sds(w_s.shape), _sds((CHUNK, CHUNK_GROUPS)), _sds((1, e)), _sds((1, e))],
        scratch_shapes=[pltpu.VMEM((CHUNK, e), F32)],
        compiler_params=_params("arbitrary"), name=name)(dy, z, z, z, ln_g, ln_b, w_s, bs_full)


def _mod_rows(mods, layer, d, nseg):
    m = mods[layer, :nseg]
    return [m[:, None, k * d:(k + 1) * d] for k in range(3)]


def _local_step(x, ctx, tgt, w, mods, comm=None):
    nb, l, d = x.shape
    lc = ctx.shape[1]
    e = d
    tl, ta = nb * l, nb * (l + lc)
    tm = _pick(lc, (256, 128))
    segs_a, segs_l = _Segs((l,) * nb + (lc,) * nb, tm), _Segs((l,) * nb, tm)
    norm_g = w['norm_g']
    g = {}

    def carried(tag, fn, *args, **kw):
        if comm is None:
            return fn(*args, **kw)
        res, brought = fn(*args, side=comm.side(tag), **kw)
        comm.done(tag, brought)
        return res

    xa0 = jnp.concatenate([x.reshape(tl, d), ctx.reshape(nb * lc, d)], axis=0)

    sh0, sc0, gt0 = _mod_rows(mods, 0, d, 2 * nb)
    h0 = _rms_mod_fwd(xa0, norm_g[0:1], sc0, sh0, segs_a, "l0_norm")
    z0 = carried('fwd1', _mm, h0, w['cv_w_in'], "l0_in")
    y2_0 = jnp.concatenate([carried('fwd2', _conv1_fwd, z0, w['cv_dw'], w['cv_db'], nb, l, 0, "l0_conv_lat"),
                            _conv1_fwd(z0, w['cv_dw'], w['cv_db'], nb, lc, tl, "l0_conv_ctx")], axis=0)
    y4_0 = _conv2_fwd(y2_0, z0, w['cv_ln_g'], w['cv_ln_b'], tm, "l0_gate")
    o0 = _mm(y4_0, w['cv_w_out'], "l0_out")
    xa1 = _resid_fwd(xa0, o0, gt0, segs_a, "l0_resid")

    sh1, sc1, gt1 = _mod_rows(mods, 1, d, 2 * nb)
    h1 = _rms_mod_fwd(xa1, norm_g[1:2], sc1, sh1, segs_a, "l1_norm")
    z1 = carried('fwd3', _mm, h1, w['pl_w_in'], "l1_in")
    taps_l, ic_l = _pool_tables(l, e)
    taps_c, ic_c = _pool_tables(lc, e)
    pm1 = jnp.concatenate([_pool1(z1, taps_l, ic_l, nb, l, 0, False, "l1_pool_lat", BF16),
                           _pool1(z1, taps_c, ic_c, nb, lc, tl, False, "l1_pool_ctx", BF16)], axis=0)
    y2_1 = _pool2_fwd(pm1, w['pl_w_grp'], w['pl_scale'], z1, tm, "l1_group")
    o1 = _mm(y2_1, w['pl_w_out'], "l1_out")
    xa2 = _resid_fwd(xa1, o1, gt1, segs_a, "l1_resid")

    sh2, sc2, gt2 = _mod_rows(mods, 2, d, 2 * nb)
    h2 = _rms_mod_fwd(xa2, norm_g[2:3], sc2, sh2, segs_a, "l2_norm")
    w_in = w['ml_w_in']
    kvc = MLA_KV_RANK + MLA_ROPE
    w_in_p = jnp.concatenate([w_in[:, :kvc], jnp.zeros((d, LANES - MLA_ROPE), w_in.dtype), w_in[:, kvc:]], axis=1)
    w_uq_p = jnp.pad(w['ml_w_uq'].reshape(MLA_Q_RANK, MLA_HEADS, MLA_NOPE + MLA_ROPE),
                     ((0, 0), (0, 0), (0, HEAD_W - MLA_NOPE - MLA_ROPE))).reshape(MLA_Q_RANK, MLA_HEADS * HEAD_W)
    rope_g = jnp.pad(w['ml_rope_norm'], ((0, 0), (0, LANES - MLA_ROPE)))
    nope_g = w['ml_nope_norm']
    ctab, stab = _rope_tables(l, lc, nb)
    kvw = MLA_KV_RANK + LANES
    z2 = _mm(h2, w_in_p, "l2_in")
    zkv, zq, zg = z2[:, :kvw], z2[:tl, kvw:kvw + MLA_Q_RANK], z2[:tl, kvw + MLA_Q_RANK:]
    ckvn, krr = _kv_pre_fwd(zkv, w['ml_kv_norm'], rope_g[1:2], ctab, stab, tm, "l2_kv_pre")
    cqn = _q_pre_fwd(zq, w['ml_q_norm'], tm, "l2_q_pre")
    q2 = _mm(cqn, w_uq_p, "l2_uq")
    kv2 = _mm(ckvn, w['ml_w_ukv'], "l2_ukv")
    qf = _q_post_fwd(q2, nope_g[0:1], rope_g[0:1], ctab, stab, tm, "l2_q_post")
    kf, vf = _k_post_fwd(kv2, krr, nope_g[1:2], tm, "l2_k_post")
    o_att, lse = _attn_fwd(qf, kf, vf, nb, l, lc, "l2_attn")
    og = _gate_fwd(o_att, zg, tm, "l2_gate")
    o2 = _mm(og, w['ml_w_out'], "l2_out")
    x3 = _resid_fwd(xa2, o2, gt2[:nb], segs_l, "l2_resid")

    sh3, sc3, gt3 = _mod_rows(mods, 3, d, nb)
    h3 = _rms_mod_fwd(x3, norm_g[3:4], sc3, sh3, segs_l, "l3_norm")
    z3 = _mm(h3, w['ch_w_in'], "l3_in")
    bs_full = jnp.repeat(w['ch_b_s'], e // CHUNK_GROUPS, axis=1)
    y3 = _chunk_fwd(z3, w['ch_ln_g'], w['ch_ln_b'], w['ch_w_s'], bs_full, "l3_chunk")
    o3 = _mm(y3, w['ch_w_out'], "l3_out")
    x4 = _resid_fwd(x3, o3, gt3, segs_l, "l3_resid")

    loss_vec, dx4 = _loss_head(x4, tgt.reshape(tl, d), tm, "loss")

    do3, dgt3 = _resid_bwd(dx4, o3, gt3, segs_l, "l3_resid_b")
    dy3 = _mm_nt(do3, w['ch_w_out'], "l3_out_bx")
    g['ch_w_out'] = _mm_tn(y3, do3, "l3_out_bw")
    dz3, g['ch_w_s'], g['ch_b_s'], g['ch_ln_g'], g['ch_ln_b'] = _chunk_bwd(
        dy3, z3, w['ch_ln_g'], w['ch_ln_b'], w['ch_w_s'], bs_full, "l3_chunk_b")
    dh3 = _mm_nt(dz3, w['ch_w_in'], "l3_in_bx")
    g['ch_w_in'] = _mm_tn(h3, dz3, "l3_in_bw", shards=N_DEV)
    dx3, dng3, dsc3, dsh3 = _rms_mod_bwd(x3, norm_g[3:4], sc3, sh3, dh3, dx4, segs_l, "l3_norm_b")
    if comm is not None:
        comm.grads_ready(3, g)

    do2, dgt2 = _resid_bwd(dx3, o2, gt2[:nb], segs_l, "l2_resid_b")
    dog = _mm_nt(do2, w['ml_w_out'], "l2_out_bx")
    g['ml_w_out'] = _mm_tn(og, do2, "l2_out_bw")
    d_att, dzg = _gate_bwd(dog, o_att, zg, tm, "l2_gate_b")
    dqf, dkl, dkc, dvl, dvc = carried('quad3', _attn_bwd, d_att, o_att, lse, qf, kf, vf, nb, l, lc, "l2_attn_b")
    dkf, dvf = jnp.concatenate([dkl, dkc], axis=0), jnp.concatenate([dvl, dvc], axis=0)
    dq2, dnope_q, drope_q = _q_post_bwd(dqf, q2, nope_g[0:1], rope_g[0:1], ctab, stab, tm, "l2_q_post_b")
    dkv2, dkrr, dnope_k = _k_post_bwd(dkf, dvf, kv2, nope_g[1:2], tm, "l2_k_post_b")
    dcqn = _mm_nt(dq2, w_uq_p, "l2_uq_bx")
    g_uq_p = _mm_tn(cqn, dq2, "l2_uq_bw")
    dckvn = _mm_nt(dkv2, w['ml_w_ukv'], "l2_ukv_bx")
    g['ml_w_ukv'] = _mm_tn(ckvn, dkv2, "l2_ukv_bw", shards=N_DEV)
    dzq, g['ml_q_norm'] = _q_pre_bwd(dcqn, zq, w['ml_q_norm'], tm, "l2_q_pre_b")
    dzkv, g['ml_kv_norm'], drope_k = _kv_pre_bwd(dckvn, dkrr, zkv, w['ml_kv_norm'], rope_g[1:2], ctab, stab, tm,
                                                  "l2_kv_pre_b")
    dz2 = jnp.concatenate([dzkv, jnp.pad(jnp.concatenate([dzq, dzg], axis=1), ((0, ta - tl), (0, 0)))], axis=1)
    dh2 = _mm_nt(dz2, w_in_p, "l2_in_bx")
    g_in_p = _mm_tn(h2, dz2, "l2_in_bw")
    g['ml_w_in'] = jnp.concatenate([g_in_p[:, :kvc], g_in_p[:, kvw:]], axis=1)
    g['ml_w_uq'] = g_uq_p.reshape(MLA_Q_RANK, MLA_HEADS, HEAD_W)[:, :, :MLA_NOPE + MLA_ROPE].reshape(
        MLA_Q_RANK, MLA_HEADS * (MLA_NOPE + MLA_ROPE))
    g['ml_nope_norm'] = jnp.concatenate([dnope_q, dnope_k], axis=0)
    g['ml_rope_norm'] = jnp.concatenate([drope_q, drope_k], axis=0)[:, :MLA_ROPE]
    dx3a = jnp.pad(dx3, ((0, ta - tl), (0, 0)))
    dxa2, dng2, dsc2, dsh2 = _rms_mod_bwd(xa2, norm_g[2:3], sc2, sh2, dh2, dx3a, segs_a, "l2_norm_b")
    if comm is not None:
        comm.grads_ready(2, g)

    do1, dgt1 = _resid_bwd(dxa2, o1, gt1, segs_a, "l1_resid_b")
    dy2_1 = _mm_nt(do1, w['pl_w_out'], "l1_out_bx")
    g['pl_w_out'] = _mm_tn(y2_1, do1, "l1_out_bw")
    dpm, dgz1, g['pl_scale'], g['pl_w_grp'] = _pool2_bwd(dy2_1, pm1, w['pl_w_grp'], w['pl_scale'], z1, tm,
                                                          "l1_group_b")
    dv1 = jnp.concatenate([_pool1(dpm, taps_l, ic_l, nb, l, 0, True, "l1_pool_lat_b", BF16),
                           _pool1(dpm, taps_c, ic_c, nb, lc, tl, True, "l1_pool_ctx_b", BF16)], axis=0)
    dz1 = jnp.concatenate([dv1, dgz1], axis=1)
    dh1 = _mm_nt(dz1, w['pl_w_in'], "l1_in_bx")
    g['pl_w_in'] = carried('quad2', _mm_tn, h1, dz1, "l1_in_bw", shards=N_DEV)
    dxa1, dng1, dsc1, dsh1 = _rms_mod_bwd(xa1, norm_g[1:2], sc1, sh1, dh1, dxa2, segs_a, "l1_norm_b")
    if comm is not None:
        comm.grads_ready(1, g)

    do0, dgt0 = _resid_bwd(dxa1, o0, gt0, segs_a, "l0_resid_b")
    dy4 = _mm_nt(do0, w['cv_w_out'], "l0_out_bx")
    g['cv_w_out'] = _mm_tn(y4_0, do0, "l0_out_bw")
    dy2, dgz0, g['cv_ln_g'], g['cv_ln_b'] = _conv2_bwd(dy4, y2_0, z0, w['cv_ln_g'], w['cv_ln_b'], tm, "l0_gate_b")
    da_l, db_l, ddw, ddb = carried('quad1', _conv1_bwd, dy2, z0, w['cv_dw'], jnp.zeros((CONV_WIDTH, e), F32),
                                   jnp.zeros((1, e), F32), nb, l, 0, "l0_conv_lat_b")
    da_c, db_c, g['cv_dw'], g['cv_db'] = _conv1_bwd(dy2, z0, w['cv_dw'], ddw, ddb, nb, lc, tl, "l0_conv_ctx_b")
    dz0 = jnp.concatenate([jnp.concatenate([da_l, da_c], axis=0), jnp.concatenate([db_l, db_c], axis=0), dgz0],
                          axis=1)
    dh0 = _mm_nt(dz0, w['cv_w_in'], "l0_in_bx")
    g['cv_w_in'] = _mm_tn(h0, dz0, "l0_in_bw", shards=N_DEV)
    dxa0, dng0, dsc0, dsh0 = _rms_mod_bwd(xa0, norm_g[0:1], sc0, sh0, dh0, dxa1, segs_a, "l0_norm_b")
    if comm is not None:
        comm.grads_ready(0, g)

    def rows4(t):
        return jnp.pad(t[:, 0], ((0, 2 * nb - t.shape[0]), (0, 0)))

    dmods = jnp.stack([
        jnp.concatenate([rows4(dsh0), rows4(dsc0), rows4(dgt0)], axis=1),
        jnp.concatenate([rows4(dsh1), rows4(dsc1), rows4(dgt1)], axis=1),
        jnp.concatenate([rows4(dsh2), rows4(dsc2), rows4(dgt2)], axis=1),
        jnp.concatenate([rows4(dsh3), rows4(dsc3), rows4(dgt3)], axis=1)])
    dnorm_g = jnp.concatenate([dng0, dng1, dng2, dng3], axis=0)
    return loss_vec, dxa0[:tl].reshape(nb, l, d), g, dmods, dnorm_g


def _mesh_pos():
    return lax.axis_index("x"), lax.axis_index("y"), lax.axis_index("c")


def _remote(src, dst, send_sems, recv_sems, k, dev):
    return pltpu.make_async_remote_copy(src_ref=src, dst_ref=dst, send_sem=send_sems.at[k], recv_sem=recv_sems.at[k],
                                        device_id=dev, device_id_type=pl.DeviceIdType.MESH)


def _comm_call(body, xs, out_shapes, n_remote, n_local, name):
    hbm = pl.BlockSpec(memory_space=pltpu.HBM)
    return pl.pallas_call(
        body, in_specs=[hbm] * len(xs), out_specs=[hbm] * len(out_shapes), out_shape=out_shapes,
        scratch_shapes=_sem_shapes(n_remote, n_local),
        compiler_params=pltpu.CompilerParams(has_side_effects=True), name=name)(*xs)


def _run_side(side, name):
    n = len(side.xs)

    def body(*refs):
        side.start(refs[:n], refs[n:n + len(side.out_shapes)], *refs[n + len(side.out_shapes):])
        side.finish(refs[:n], refs[n:n + len(side.out_shapes)], *refs[n + len(side.out_shapes):])

    return _comm_call(body, side.xs, side.out_shapes, side.n_remote, side.n_local, name)


def _gather_side(xs):
    n = len(xs)

    def plan(x_refs, o_refs, send_sems, recv_sems, local_sems):
        x, y, c = _mesh_pos()
        me, sib = (x, y, c), (x, y, 1 - c)
        chips = [(1 - x, y), (x, 1 - y), (1 - x, 1 - y)]

        def slot(a, p):
            return o_refs[a].at[4 * p[0] + 2 * p[1] + p[2]]

        def copy(a, k, block, to, src=None):
            return _remote(slot(a, block) if src is None else src, slot(a, block), send_sems, recv_sems, 7 * a + k, to)

        mine = [pltpu.make_async_copy(x_refs[a], slot(a, me), local_sems.at[a]) for a in range(n)]
        first = []
        for a in range(n):
            first += [copy(a, 1 + j, me, chip + (c,), src=x_refs[a]) for j, chip in enumerate(chips)]
            first.append(copy(a, 0, me, sib, src=x_refs[a]))
        return me, sib, c, chips, copy, mine, first

    def start(x_refs, o_refs, send_sems, recv_sems, local_sems):
        _, _, _, _, _, mine, first = plan(x_refs, o_refs, send_sems, recv_sems, local_sems)
        for cp in mine + first:
            cp.start()

    def finish(x_refs, o_refs, send_sems, recv_sems, local_sems):
        me, sib, c, chips, copy, mine, first = plan(x_refs, o_refs, send_sems, recv_sems, local_sems)
        passed = []
        for j, chip in enumerate(chips):
            for a in range(n):
                copy(a, 1 + j, chip + (c,), me).wait_recv()
                passed.append(copy(a, 4 + j, chip + (c,), sib))
                passed[-1].start()
        for a in range(n):
            copy(a, 0, sib, me).wait_recv()
        for j, chip in enumerate(chips):
            for a in range(n):
                copy(a, 4 + j, chip + (1 - c,), me).wait_recv()
        for cp in first + passed:
            cp.wait_send()
        for cp in mine:
            cp.wait()

    return _Side(xs, [_sds((N_DEV,) + x.shape, x.dtype) for x in xs], 7 * n, n, start, finish)


def _gather_all(xs, name):
    return _run_side(_gather_side(xs), name)


def _swap_halves(xs, name):
    n = len(xs)

    def body(*refs):
        x_refs, o_refs, (send_sems, recv_sems, _) = refs[:n], refs[n:2 * n], refs[2 * n:]
        x, y, c = _mesh_pos()
        copies = [_remote(x_refs[a].at[q, 1 - c], o_refs[a].at[q], send_sems, recv_sems, 4 * a + q, (x, y, 1 - c))
                  for a in range(n) for q in range(4)]
        for cp in copies:
            cp.start()
        for cp in copies:
            cp.wait_recv()
        for cp in copies:
            cp.wait_send()

    return _comm_call(body, xs, [_sds((4,) + x.shape[2:], x.dtype) for x in xs], 4 * n, 0, name)


def _quad_side(xs):
    n = len(xs)

    def plan(x_refs, o_refs, send_sems, recv_sems, local_sems):
        x, y, c = _mesh_pos()
        q = 2 * x + y
        chips = [(1 - x, y), (x, 1 - y), (1 - x, 1 - y)]
        mine = [pltpu.make_async_copy(x_refs[a].at[q], o_refs[a].at[q], local_sems.at[a]) for a in range(n)]
        sends, arrivals = [], []
        for a in range(n):
            for j, chip in enumerate(chips):
                qj = 2 * chip[0] + chip[1]
                sends.append(_remote(x_refs[a].at[qj], o_refs[a].at[q], send_sems, recv_sems, 3 * a + j, chip + (c,)))
                arrivals.append(_remote(x_refs[a].at[qj], o_refs[a].at[qj], send_sems, recv_sems, 3 * a + j,
                                        chip + (c,)))
        return mine, sends, arrivals

    def start(*refs):
        mine, sends, _ = plan(*refs)
        for cp in mine + sends:
            cp.start()

    def finish(*refs):
        mine, sends, arrivals = plan(*refs)
        for cp in arrivals:
            cp.wait_recv()
        for cp in sends:
            cp.wait_send()
        for cp in mine:
            cp.wait()

    return _Side(xs, [_sds(x.shape, x.dtype) for x in xs], 3 * n, n, start, finish)


def _pair_add(xs, rs, name):
    n = len(xs)

    def body(*refs):
        c = lax.axis_index("c")
        for x_ref, r_ref, o_ref in zip(refs[:n], refs[n:2 * n], refs[2 * n:]):
            o_ref[...] = (x_ref[c].astype(F32) + r_ref[...].astype(F32)).astype(o_ref.dtype)

    slot = lambda x: pl.BlockSpec((None,) + x.shape[2:], lambda q: (q, 0, 0))
    return pl.pallas_call(
        body, grid=(4,),
        in_specs=[pl.BlockSpec((None, 2) + x.shape[2:], lambda q: (q, 0, 0, 0)) for x in xs] + [slot(x) for x in xs],
        out_specs=[slot(x) for x in xs], out_shape=[_sds((4,) + x.shape[2:], x.dtype) for x in xs],
        compiler_params=_params("parallel"), name=name)(*xs, *rs)


def _pack_rows(n):
    r = -(-n // PACK_COLS)
    return -(-r // 256) * 256 if r > 256 else -(-r // 16) * 16


def _pack(arrs, dtype):
    flat = jnp.concatenate([a.reshape(-1).astype(dtype) for a in arrs])
    rows = _pack_rows(flat.shape[0])
    return jnp.pad(flat, (0, rows * PACK_COLS - flat.shape[0])).reshape(rows, PACK_COLS)


def _pack_shards(arrs):
    flat = jnp.concatenate([a.astype(F32) for a in arrs], axis=1)
    rows = _pack_rows(flat.shape[1])
    return jnp.pad(flat, ((0, 0), (0, rows * PACK_COLS - flat.shape[1]))).reshape(N_DEV, rows, PACK_COLS)


def _unpack(packed, shapes, lead=()):
    flat = packed.reshape(tuple(lead) + (-1,))
    out, off = [], 0
    for s in shapes:
        n = 1
        for v in s:
            n *= v
        out.append(flat[..., off:off + n].reshape(tuple(lead) + tuple(s)))
        off += n
    return out


def _to_shards(full, ax):
    s = full.shape
    t = full.reshape(s[:ax] + (N_DEV, s[ax] // N_DEV) + s[ax + 1:])
    return jnp.moveaxis(t, ax, 0).reshape(N_DEV, -1)


def _from_shards(shards, local_shape, ax):
    t = jnp.moveaxis(shards.reshape((N_DEV,) + tuple(local_shape)), 0, ax)
    s = t.shape
    return t.reshape(s[:ax] + (s[ax] * s[ax + 1],) + s[ax + 2:])


def _mod_fwd(c_rows, w_mod, b_mod, name):
    nl, d, n = w_mod.shape
    r = c_rows.shape[0]

    def body(c_ref, w_ref, b_ref, o_ref):
        s = _silu(c_ref[...])
        for l in range(nl):
            o_ref[l] = _dot(s, w_ref[l]) + b_ref[l]

    return pl.pallas_call(body, out_shape=_sds((nl, r, n)),
                          compiler_params=pltpu.CompilerParams(vmem_limit_bytes=VMEM_LIMIT), name=name)(
        c_rows, w_mod, b_mod)


def _mod_bwd(c_rows, dcols, dall, w_mod, c_ctx, name):
    nl, d, n = w_mod.shape
    r = c_rows.shape[0]

    def body(c_ref, dc_ref, da_ref, w_ref, cc_ref, gw_ref, gb_ref, gc_ref):
        s = _silu(c_ref[...])
        ds = jnp.zeros((r, d), F32)
        for l in range(nl):
            gw_ref[l] = _dot_tn(s, dc_ref[l])
            gb_ref[l] = _rowsum(da_ref[l])
            ds = ds + _dot_nt(dc_ref[l], w_ref[l])
        row = lax.broadcasted_iota(jnp.int32, (r, d), 0)
        gc_ref[...] = _rowsum(jnp.where(row % 4 >= 2, ds, 0.0)) * _dsilu(cc_ref[...])

    return pl.pallas_call(body, out_shape=[_sds((nl, d, n)), _sds((nl, 1, 3 * d)), _sds((1, d))],
                          compiler_params=pltpu.CompilerParams(vmem_limit_bytes=VMEM_LIMIT), name=name)(
        c_rows, dcols, dall, w_mod, c_ctx)


def _adam_math(w, gsum, m, v):
    c1, c2 = 1.0 - ADAM_B1 ** ADAM_STEP, 1.0 - ADAM_B2 ** ADAM_STEP
    mn = ADAM_B1 * m + (1.0 - ADAM_B1) * gsum
    vn = ADAM_B2 * v + (1.0 - ADAM_B2) * (gsum * gsum)
    return -ADAM_LR * ((mn / c1) / (jnp.sqrt(vn / c2) + ADAM_EPS) + ADAM_WD * w), mn, vn


def _adam(w, gparts, row0, m, v, name):
    rows, cols = w.shape
    npart = gparts.shape[0]
    if rows % 8:
        tr = rows
        assert row0 == 0 and gparts.shape[1] == rows
    else:
        tr = max(t for t in (512, 256, 128, 64, 32, 16, 8) if rows % t == 0 and row0 % t == 0
                 and (t * cols <= 256 * 1024 or t == 8))

    def body(w_ref, g_ref, m_ref, v_ref, go_ref, d_ref, mo_ref, vo_ref):
        gsum = g_ref[0].astype(F32)
        for p in range(1, npart):
            gsum = gsum + g_ref[p].astype(F32)
        go_ref[...] = gsum
        d_ref[...], mo_ref[...], vo_ref[...] = _adam_math(w_ref[...], gsum, m_ref[...], v_ref[...])

    spec = _row(tr, cols)
    return pl.pallas_call(
        body, grid=(rows // tr,),
        in_specs=[spec, pl.BlockSpec((npart, tr, cols), lambda i: (0, row0 // tr + i, 0)), spec, spec],
        out_specs=[spec] * 4, out_shape=[_sds((rows, cols))] * 4, compiler_params=_params("parallel"),
        name=name)(w, gparts, m, v)


INPUTS = ['x', 'c', 'ctx'] + WEIGHTS + ['loss_target'] + ['m_' + n for n in WEIGHTS] + ['v_' + n for n in WEIGHTS]
AXES = ("x", "y", "c")
LAYER_MATS = (('cv_w_in', 'cv_w_out'), ('pl_w_in', 'pl_w_grp', 'pl_w_out'),
              ('ml_w_in', 'ml_w_uq', 'ml_w_ukv', 'ml_w_out'), ('ch_w_in', 'ch_w_out'))
KINDS = ('grad_', 'delta_', 'new_m_', 'new_v_')


def _squeeze_layer(name, a):
    return a if name == 'norm_g' or a.ndim < 3 else a[0]


def _as2d(a):
    return a.reshape(-1, a.shape[-1])


class _Exchanges:
    def __init__(self, a, w):
        self.a, self.w, self.sums, self.quad = a, w, {}, {}

    def mats(self, layer):
        return [_as2d(self.a[n]).astype(BF16) for n in LAYER_MATS[layer]]

    def take_weights(self, layer, bufs):
        for n, buf in zip(LAYER_MATS[layer], bufs):
            self.w[n] = _squeeze_layer(n, _from_shards(buf, self.a[n].shape, SHARD_AXIS[n]))

    def side(self, tag):
        layer = int(tag[-1])
        return _gather_side(self.mats(layer)) if tag.startswith('fwd') else _quad_side(self.sums[layer])

    def done(self, tag, brought):
        layer = int(tag[-1])
        if tag.startswith('fwd'):
            self.take_weights(layer, brought)
        else:
            self.quad[layer] = brought

    def shard_major(self, n, gn):
        if gn.ndim == 3 and gn.shape[0] == N_DEV and gn.dtype == BF16:
            return gn
        whole = tuple(N_DEV * s if i == SHARD_AXIS[n] else s for i, s in enumerate(self.a[n].shape))
        return _to_shards(gn.reshape(whole), SHARD_AXIS[n]).reshape((N_DEV,) + _as2d(self.a[n]).shape)

    def grads_ready(self, layer, g):
        bufs = [self.shard_major(n, g[n]).astype(BF16) for n in LAYER_MATS[layer]]
        if layer == 0:
            bufs.append(_pack_shards([self.shard_major(n, g[n]).reshape(N_DEV, -1) for n in VECTOR_WEIGHTS]))
        bufs = [b.reshape((4, 2) + b.shape[1:]) for b in bufs]
        got = _swap_halves(bufs, "grads_swap_cores_%d" % layer)
        self.sums[layer] = _pair_add(bufs, got, "grads_add_cores_%d" % layer)
        if layer == 0:
            self.quad[0] = _run_side(_quad_side(self.sums[0]), "grads_exchange_chips_0")


def _train_step(a):
    x, c, ctx, tgt = a['x'], a['c'], a['ctx'], a['loss_target']
    d = x.shape[-1]
    nb = x.shape[0]
    dev = 4 * lax.axis_index("x") + 2 * lax.axis_index("y") + lax.axis_index("c")
    local_shape = {n: a[n].shape for n in WEIGHTS}

    w = {n: _squeeze_layer(n, a[n]) for n in WEIGHTS if SHARD_AXIS[n] is None}
    comm = _Exchanges(a, w)
    vec_names = ['c'] + VECTOR_WEIGHTS
    gathered = _gather_all(comm.mats(0) + [_pack([a[n] for n in vec_names], F32)], "gather_first")
    comm.take_weights(0, gathered[:-1])
    parts = dict(zip(vec_names, _unpack(gathered[-1], [a[n].shape for n in vec_names], lead=(N_DEV,))))
    for n in VECTOR_WEIGHTS:
        w[n] = _squeeze_layer(n, _from_shards(parts[n], local_shape[n], SHARD_AXIS[n]))
    c_all = parts['c'].reshape(N_DEV * nb, d)
    c_ctx = a['c_ctx'].reshape(1, d)

    w_mod = a['w_mod']
    nl, ncol = w_mod.shape[0], w_mod.shape[2]
    mod_rows = -(-(N_DEV * nb + 1) // 8) * 8
    c_rows = jnp.concatenate([c_all, c_ctx, jnp.zeros((mod_rows - N_DEV * nb - 1, d), F32)], axis=0)
    b_loc = lax.dynamic_slice(a['b_mod'], (0, dev * ncol), (nl, ncol))[:, None, :]
    mod_loc = _mod_fwd(c_rows, w_mod, b_loc, "mod_fwd")
    mod_all, = _gather_all([mod_loc.reshape(nl * mod_rows, ncol)], "gather_mods")
    mod_all = mod_all.reshape(N_DEV, nl, mod_rows, ncol).transpose(1, 2, 0, 3).reshape(nl, mod_rows, N_DEV * ncol)
    ctx_row = mod_all[:, N_DEV * nb:N_DEV * nb + 1]
    mods = jnp.concatenate([lax.dynamic_slice(mod_all, (0, dev * nb, 0), (nl, nb, 3 * d))] + [ctx_row] * nb, axis=1)

    loss_vec, grad_x, g, dmods, dnorm_g = _local_step(x, ctx, tgt, w, mods, comm)
    loss = lax.psum(jnp.sum(loss_vec), AXES)

    nseg = dmods.shape[1]
    dm_all, = _gather_all([dmods.reshape(nl * nseg, 3 * d)], "gather_dmods")
    dm_all = dm_all.reshape(N_DEV, nl, nseg, 3 * d).transpose(1, 0, 2, 3).reshape(nl, N_DEV * nseg, 3 * d)
    dcols = lax.dynamic_slice(dm_all, (0, 0, dev * ncol), (nl, N_DEV * nseg, ncol))
    c_rows_b = jnp.concatenate([c_all.reshape(N_DEV, nb, d), jnp.broadcast_to(c_ctx, (N_DEV, nb, d))], axis=1)
    g_w_mod, g_b_mod, g_c_ctx = _mod_bwd(c_rows_b.reshape(N_DEV * nseg, d), dcols, dm_all, w_mod, c_ctx, "mod_bwd")
    g['c_ctx'], g['norm_g'] = g_c_ctx, dnorm_g
    rep_all, = _gather_all([_pack([g[n] for n in REPLICATED], F32)], "gather_replicated_grads")

    out = {}

    def keep(names, res, shapes=None):
        for kind, val in zip(KINDS, res):
            if shapes is None:
                out[kind + names[0]] = val.reshape(local_shape[names[0]])
            else:
                for n, leaf in zip(names, _unpack(val, shapes)):
                    out[kind + n] = leaf

    def update_packed(names, gparts, tag):
        res = _adam(_pack([a[n] for n in names], F32), gparts, 0, _pack([a['m_' + n] for n in names], F32),
                    _pack([a['v_' + n] for n in names], F32), "adam_" + tag)
        keep(names, res, [local_shape[n] for n in names])

    for layer, names in enumerate(LAYER_MATS):
        for n, gparts in zip(names, comm.quad[layer]):
            keep([n], _adam(_as2d(a[n]), gparts, 0, _as2d(a['m_' + n]), _as2d(a['v_' + n]), "adam_" + n))
    update_packed(VECTOR_WEIGHTS, comm.quad[0][-1], "vectors")
    update_packed(REPLICATED, rep_all, "replicated")
    keep(['w_mod'], _adam(_as2d(w_mod), _as2d(g_w_mod)[None], 0, _as2d(a['m_w_mod']), _as2d(a['v_w_mod']),
                          "adam_w_mod"))
    keep(['b_mod'], _adam(a['b_mod'], g_b_mod.reshape((1,) + a['b_mod'].shape), 0, a['m_b_mod'], a['v_b_mod'],
                          "adam_b_mod"))
    return (loss, grad_x) + tuple(out[kind + n] for kind in KINDS for n in WEIGHTS)


def kernel(x, c, ctx, c_ctx, norm_g, w_mod, b_mod, cv_w_in, cv_dw, cv_db, cv_ln_g, cv_ln_b, cv_w_out, pl_w_in, pl_w_grp, pl_scale, pl_w_out, ml_w_in, ml_q_norm, ml_kv_norm, ml_w_uq, ml_w_ukv, ml_nope_norm, ml_rope_norm, ml_w_out, ch_w_in, ch_ln_g, ch_ln_b, ch_w_s, ch_b_s, ch_w_out, loss_target, m_c_ctx, m_norm_g, m_w_mod, m_b_mod, m_cv_w_in, m_cv_dw, m_cv_db, m_cv_ln_g, m_cv_ln_b, m_cv_w_out, m_pl_w_in, m_pl_w_grp, m_pl_scale, m_pl_w_out, m_ml_w_in, m_ml_q_norm, m_ml_kv_norm, m_ml_w_uq, m_ml_w_ukv, m_ml_nope_norm, m_ml_rope_norm, m_ml_w_out, m_ch_w_in, m_ch_ln_g, m_ch_ln_b, m_ch_w_s, m_ch_b_s, m_ch_w_out, v_c_ctx, v_norm_g, v_w_mod, v_b_mod, v_cv_w_in, v_cv_dw, v_cv_db, v_cv_ln_g, v_cv_ln_b, v_cv_w_out, v_pl_w_in, v_pl_w_grp, v_pl_scale, v_pl_w_out, v_ml_w_in, v_ml_q_norm, v_ml_kv_norm, v_ml_w_uq, v_ml_w_ukv, v_ml_nope_norm, v_ml_rope_norm, v_ml_w_out, v_ch_w_in, v_ch_ln_g, v_ch_ln_b, v_ch_w_s, v_ch_b_s, v_ch_w_out):
    return _train_step(dict(zip(INPUTS, (x, c, ctx, c_ctx, norm_g, w_mod, b_mod, cv_w_in, cv_dw, cv_db, cv_ln_g, cv_ln_b, cv_w_out, pl_w_in, pl_w_grp, pl_scale, pl_w_out, ml_w_in, ml_q_norm, ml_kv_norm, ml_w_uq, ml_w_ukv, ml_nope_norm, ml_rope_norm, ml_w_out, ch_w_in, ch_ln_g, ch_ln_b, ch_w_s, ch_b_s, ch_w_out, loss_target, m_c_ctx, m_norm_g, m_w_mod, m_b_mod, m_cv_w_in, m_cv_dw, m_cv_db, m_cv_ln_g, m_cv_ln_b, m_cv_w_out, m_pl_w_in, m_pl_w_grp, m_pl_scale, m_pl_w_out, m_ml_w_in, m_ml_q_norm, m_ml_kv_norm, m_ml_w_uq, m_ml_w_ukv, m_ml_nope_norm, m_ml_rope_norm, m_ml_w_out, m_ch_w_in, m_ch_ln_g, m_ch_ln_b, m_ch_w_s, m_ch_b_s, m_ch_w_out, v_c_ctx, v_norm_g, v_w_mod, v_b_mod, v_cv_w_in, v_cv_dw, v_cv_db, v_cv_ln_g, v_cv_ln_b, v_cv_w_out, v_pl_w_in, v_pl_w_grp, v_pl_scale, v_pl_w_out, v_ml_w_in, v_ml_q_norm, v_ml_kv_norm, v_ml_w_uq, v_ml_w_ukv, v_ml_nope_norm, v_ml_rope_norm, v_ml_w_out, v_ch_w_in, v_ch_ln_g, v_ch_ln_b, v_ch_w_s, v_ch_b_s, v_ch_w_out))))
```

```python
import functools

import jax
import jax.numpy as jnp
from jax import lax
from jax.experimental import pallas as pl
from jax.experimental.pallas import tpu as pltpu

F32 = jnp.float32
BF16 = jnp.bfloat16

N_DEV = 8
EPS = 1e-6
CONV_WIDTH = 31
CONV_PAD = 16
POOL_WINDOWS = (2, 4, 8, 16)
POOL_TAPS = 16
MLA_HEADS = 8
MLA_NOPE = 128
MLA_ROPE = 64
MLA_Q_RANK = 384
MLA_KV_RANK = 256
MLA_SCALE = (MLA_NOPE + MLA_ROPE) ** -0.5
ROPE_THETA = 10000.0
GRID_W = 64
HEAD_W = 256
CHUNK = 128
CHUNK_GROUPS = 8
ADAM_LR = 0.001
ADAM_B1 = 0.9
ADAM_B2 = 0.999
ADAM_EPS = 1e-08
ADAM_WD = 0.01
ADAM_STEP = 10
LANES = 128
VMEM_LIMIT = 56 * 1024 * 1024
PACK_COLS = 1024

WEIGHTS = ['c_ctx', 'norm_g', 'w_mod', 'b_mod', 'cv_w_in', 'cv_dw', 'cv_db', 'cv_ln_g', 'cv_ln_b', 'cv_w_out',
           'pl_w_in', 'pl_w_grp', 'pl_scale', 'pl_w_out', 'ml_w_in', 'ml_q_norm', 'ml_kv_norm', 'ml_w_uq',
           'ml_w_ukv', 'ml_nope_norm', 'ml_rope_norm', 'ml_w_out', 'ch_w_in', 'ch_ln_g', 'ch_ln_b', 'ch_w_s',
           'ch_b_s', 'ch_w_out']
SHARD_AXIS = {'c_ctx': None, 'norm_g': None, 'w_mod': 2, 'b_mod': None, 'cv_w_in': 2, 'cv_dw': 2, 'cv_db': None,
              'cv_ln_g': None, 'cv_ln_b': None, 'cv_w_out': 1, 'pl_w_in': 2, 'pl_w_grp': 2, 'pl_scale': 1,
              'pl_w_out': 1, 'ml_w_in': 2, 'ml_q_norm': 1, 'ml_kv_norm': 1, 'ml_w_uq': 2, 'ml_w_ukv': 2,
              'ml_nope_norm': None, 'ml_rope_norm': None, 'ml_w_out': 1, 'ch_w_in': 2, 'ch_ln_g': 1, 'ch_ln_b': 1,
              'ch_w_s': None, 'ch_b_s': None, 'ch_w_out': 1}
MATMUL_WEIGHTS = ['cv_w_in', 'cv_w_out', 'pl_w_in', 'pl_w_grp', 'pl_w_out', 'ml_w_in', 'ml_w_uq', 'ml_w_ukv',
                  'ml_w_out', 'ch_w_in', 'ch_w_out']
VECTOR_WEIGHTS = ['cv_dw', 'pl_scale', 'ml_q_norm', 'ml_kv_norm', 'ch_ln_g', 'ch_ln_b']
EXCHANGED = MATMUL_WEIGHTS[:1] + ['cv_dw'] + MATMUL_WEIGHTS[1:4] + ['pl_scale'] + MATMUL_WEIGHTS[4:6] + [
    'ml_q_norm', 'ml_kv_norm'] + MATMUL_WEIGHTS[6:10] + ['ch_ln_g', 'ch_ln_b', 'ch_w_out']
REPLICATED = ['c_ctx', 'norm_g', 'cv_db', 'cv_ln_g', 'cv_ln_b', 'ml_nope_norm', 'ml_rope_norm', 'ch_w_s', 'ch_b_s']


def _pick(n, cands):
    for c in cands:
        if n % c == 0:
            return c
    raise ValueError(f"no tile for {n} among {cands}")


def _params(*sem):
    return pltpu.CompilerParams(dimension_semantics=sem, vmem_limit_bytes=VMEM_LIMIT)


def _sig(x):
    return 1.0 / (1.0 + jnp.exp(-x))


def _silu(x):
    return x * _sig(x)


def _dsilu(x):
    s = _sig(x)
    return s * (1.0 + x * (1.0 - s))


def _rowsum(v):
    return jnp.sum(v, axis=0, keepdims=True)


def _dot(a, b):
    return jnp.dot(a.astype(BF16), b.astype(BF16), preferred_element_type=F32)


def _dot_nt(a, b):
    return lax.dot_general(a.astype(BF16), b.astype(BF16), (((1,), (1,)), ((), ())), preferred_element_type=F32)


def _dot_tn(a, b):
    return lax.dot_general(a.astype(BF16), b.astype(BF16), (((0,), (0,)), ((), ())), preferred_element_type=F32)


class _Segs:
    def __init__(self, lens, tm):
        self.lens, self.tm, self.n = tuple(lens), tm, len(lens)
        self.starts, s = [], 0
        for l in lens:
            assert l % tm == 0
            self.starts.append(s // tm)
            s += l
        self.rows, self.tiles = s, s // tm

    def seg(self, i):
        r = 0
        for st in self.starts[1:]:
            r = r + jnp.where(i >= st, 1, 0)
        return r

    def is_first(self, i):
        f = i == 0
        for st in self.starts[1:]:
            f = jnp.logical_or(f, i == st)
        return f

    def spec(self, cols):
        return pl.BlockSpec((None, 1, cols), lambda i: (self.seg(i), 0, 0))


def _row(tm, cols, cb=0):
    return pl.BlockSpec((tm, cols), lambda i: (i, cb))


def _const(shape):
    return pl.BlockSpec(shape, lambda *_: (0,) * len(shape))


def _sds(shape, dtype=F32):
    return jax.ShapeDtypeStruct(shape, dtype)


class _Side:
    def __init__(self, xs, out_shapes, n_remote, n_local, start, finish):
        self.xs, self.out_shapes, self.n_remote, self.n_local = list(xs), list(out_shapes), n_remote, n_local
        self.start, self.finish = start, finish


def _sem_shapes(n_remote, n_local):
    return [pltpu.SemaphoreType.DMA((n_remote,)), pltpu.SemaphoreType.DMA((n_remote,)),
            pltpu.SemaphoreType.DMA((max(n_local, 1),))]


def _pallas(body, args, *, grid, in_specs, out_specs, out_shape, sem, name, scratch_shapes=(), side=None, into=None):
    aliases = {}
    if into:
        inner, n_args = body, len(args)

        def body(*refs):
            inner(*refs[:n_args], *refs[n_args + len(into):])

        aliases = {n_args + k: o for k, o in enumerate(sorted(into))}
        args = tuple(args) + tuple(into[o] for o in sorted(into))
        in_specs = list(in_specs) + [pl.BlockSpec(memory_space=pl.ANY)] * len(into)
    if side is None:
        return pl.pallas_call(body, grid=grid, in_specs=in_specs, out_specs=out_specs, out_shape=out_shape,
                              scratch_shapes=list(scratch_shapes), input_output_aliases=aliases,
                              compiler_params=_params(*sem), name=name)(*args)
    multi = isinstance(out_shape, (list, tuple))
    out_specs, out_shape = (list(out_specs), list(out_shape)) if multi else ([out_specs], [out_shape])
    ni, no, ns, si, so = len(in_specs), len(out_specs), len(scratch_shapes), len(side.xs), len(side.out_shapes)
    hbm = pl.BlockSpec(memory_space=pltpu.HBM)

    def wrapped(*refs):
        ins, sins, refs = refs[:ni], refs[ni:ni + si], refs[ni + si:]
        outs, souts, refs = refs[:no], refs[no:no + so], refs[no + so:]
        scr, sems = refs[:ns], refs[ns:]
        ids = [pl.program_id(k) for k in range(len(grid))]
        first = functools.reduce(jnp.logical_and, [i == 0 for i in ids])
        last = functools.reduce(jnp.logical_and, [i == n - 1 for i, n in zip(ids, grid)])

        @pl.when(first)
        def _():
            side.start(sins, souts, *sems)

        body(*ins, *outs, *scr)

        @pl.when(last)
        def _():
            side.finish(sins, souts, *sems)

    res = pl.pallas_call(
        wrapped, grid=grid, in_specs=list(in_specs) + [hbm] * si, out_specs=out_specs + [hbm] * so,
        out_shape=out_shape + side.out_shapes,
        scratch_shapes=list(scratch_shapes) + _sem_shapes(side.n_remote, side.n_local), input_output_aliases=aliases,
        compiler_params=pltpu.CompilerParams(dimension_semantics=("arbitrary",) * len(grid),
                                             vmem_limit_bytes=VMEM_LIMIT, has_side_effects=True),
        name=name)(*args, *side.xs)
    return (list(res[:no]) if multi else res[0]), list(res[no:])


N_TILES = (1024, 896, 768, 512, 384, 256, 128)


def _mm(a, b, name, out_dtype=F32, rows=None, side=None, resid=None):
    m, k, n = rows or a.shape[0], a.shape[1], b.shape[1]
    tm, tn = _pick(m, (512, 256, 128)), _pick(n, N_TILES)
    if resid is not None:
        x, gt, segs = resid
        tm = segs.tm

        def body(a_ref, b_ref, x_ref, gt_ref, o_ref, y_ref):
            o = _dot(a_ref[...], b_ref[...])
            o_ref[...] = o
            y_ref[...] = x_ref[...] + gt_ref[...] * o

        tile = pl.BlockSpec((tm, tn), lambda j, i: (i, j))
        return pl.pallas_call(
            body, grid=(n // tn, m // tm),
            in_specs=[pl.BlockSpec((tm, k), lambda j, i: (i, 0)), pl.BlockSpec((k, tn), lambda j, i: (0, j)), tile,
                      pl.BlockSpec((None, 1, tn), lambda j, i: (segs.seg(i), 0, j))],
            out_specs=[tile, tile], out_shape=[_sds((m, n)), _sds((m, n))],
            compiler_params=_params("parallel", "parallel"), name=name)(a, b, x, gt)

    def body(a_ref, b_ref, o_ref):
        o_ref[...] = _dot(a_ref[...], b_ref[...]).astype(o_ref.dtype)

    return _pallas(
        body, (a, b), grid=(n // tn, m // tm),
        in_specs=[pl.BlockSpec((tm, k), lambda j, i: (i, 0)), pl.BlockSpec((k, tn), lambda j, i: (0, j))],
        out_specs=pl.BlockSpec((tm, tn), lambda j, i: (i, j)), out_shape=_sds((m, n), out_dtype),
        sem=("parallel", "parallel"), name=name, side=side)


def _mm_nt(a, b, name, out_dtype=F32):
    m, k, n = a.shape[0], a.shape[1], b.shape[0]
    tm, tn = _pick(m, (512, 256, 128)), _pick(n, N_TILES)

    def body(a_ref, b_ref, o_ref):
        o_ref[...] = _dot_nt(a_ref[...], b_ref[...]).astype(o_ref.dtype)

    return pl.pallas_call(
        body, grid=(n // tn, m // tm),
        in_specs=[pl.BlockSpec((tm, k), lambda j, i: (i, 0)), pl.BlockSpec((tn, k), lambda j, i: (j, 0))],
        out_specs=pl.BlockSpec((tm, tn), lambda j, i: (i, j)), out_shape=_sds((m, n), out_dtype),
        compiler_params=_params("parallel", "parallel"), name=name)(a, b)


def _mm_nt_sum(pairs, name):
    m, n = pairs[0][0].shape[0], pairs[0][1].shape[0]
    tm, tn = _pick(m, (512, 256, 128)), _pick(n, N_TILES)
    tiles = [a.shape[0] // tm for a, _ in pairs]
    assert all(a.shape[0] % tm == 0 for a, _ in pairs)

    def body(*refs):
        o_ref, i = refs[-1], pl.program_id(1)
        acc = _dot_nt(refs[0][...], refs[1][...])
        for p in range(1, len(pairs)):
            acc = acc + jnp.where(i < tiles[p], _dot_nt(refs[2 * p][...], refs[2 * p + 1][...]), 0.0)
        o_ref[...] = acc

    in_specs = []
    for (a, b), nt in zip(pairs, tiles):
        in_specs += [pl.BlockSpec((tm, a.shape[1]), lambda j, i, nt=nt: (jnp.minimum(i, nt - 1), 0)),
                     pl.BlockSpec((tn, a.shape[1]), lambda j, i: (j, 0))]
    return pl.pallas_call(
        body, grid=(n // tn, m // tm), in_specs=in_specs, out_specs=pl.BlockSpec((tm, tn), lambda j, i: (i, j)),
        out_shape=_sds((m, n)), compiler_params=_params("parallel", "parallel"),
        name=name)(*[t for pair in pairs for t in pair])


def _mm_tn(a, b, name, rows=None, shards=None, side=None):
    t, k, n = rows or a.shape[0], a.shape[1], b.shape[1]
    tk, tt = _pick(k, N_TILES), _pick(t, (512, 256, 128))
    tn = n // shards if shards else _pick(n, N_TILES)
    assert tn % LANES == 0

    def body(a_ref, b_ref, o_ref, acc_ref):
        @pl.when(pl.program_id(2) == 0)
        def _():
            acc_ref[...] = jnp.zeros_like(acc_ref)

        acc_ref[...] += _dot_tn(a_ref[...], b_ref[...])

        @pl.when(pl.program_id(2) == pl.num_programs(2) - 1)
        def _():
            o_ref[...] = acc_ref[...].astype(o_ref.dtype)

    if shards:
        out_spec, out_shape = pl.BlockSpec((None, tk, tn), lambda i, j, s: (j, i, 0)), _sds((shards, k, tn), BF16)
    else:
        out_spec, out_shape = pl.BlockSpec((tk, tn), lambda i, j, s: (i, j)), _sds((k, n))
    return _pallas(
        body, (a, b), grid=(k // tk, n // tn, t // tt),
        in_specs=[pl.BlockSpec((tt, tk), lambda i, j, s: (s, i)), pl.BlockSpec((tt, tn), lambda i, j, s: (s, j))],
        out_specs=out_spec, out_shape=out_shape, scratch_shapes=[pltpu.VMEM((tk, tn), F32)],
        sem=("parallel", "parallel", "arbitrary"), name=name, side=side)


def _rms_mod_fwd(x, g, sc, sh, segs, name):
    d, tm = x.shape[1], segs.tm

    def body(x_ref, g_ref, sc_ref, sh_ref, h_ref):
        xf = x_ref[...]
        r = lax.rsqrt(jnp.mean(xf * xf, axis=-1, keepdims=True) + EPS)
        h_ref[...] = ((xf * r * g_ref[...]) * (1.0 + sc_ref[...]) + sh_ref[...]).astype(h_ref.dtype)

    return pl.pallas_call(
        body, grid=(segs.tiles,), in_specs=[_row(tm, d), _const((1, d)), segs.spec(d), segs.spec(d)],
        out_specs=_row(tm, d), out_shape=_sds((segs.rows, d), BF16), compiler_params=_params("parallel"),
        name=name)(x, g, sc, sh)


def _rms_mod_bwd(x, g, sc, sh, dh, dxr, segs, name):
    d, tm = x.shape[1], segs.tm
    dxr_tiles = dxr.shape[0] // tm

    def body(x_ref, g_ref, sc_ref, sh_ref, dh_ref, dxr_ref, dx_ref, dg_ref, dsc_ref, dsh_ref):
        i = pl.program_id(0)

        @pl.when(i == 0)
        def _():
            dg_ref[...] = jnp.zeros_like(dg_ref)

        @pl.when(segs.is_first(i))
        def _():
            dsc_ref[...] = jnp.zeros_like(dsc_ref)
            dsh_ref[...] = jnp.zeros_like(dsh_ref)

        xf, gg, dhf = x_ref[...], g_ref[...], dh_ref[...].astype(F32)
        dxr = jnp.where(i < dxr_tiles, dxr_ref[...], 0.0)
        r = lax.rsqrt(jnp.mean(xf * xf, axis=-1, keepdims=True) + EPS)
        xh = xf * r
        dsh_ref[...] += _rowsum(dhf)
        dsc_ref[...] += _rowsum(dhf * (xh * gg))
        du = dhf * (1.0 + sc_ref[...])
        dg_ref[...] += _rowsum(du * xh)
        dxh = du * gg
        dx_ref[...] = dxr + r * (dxh - xh * jnp.mean(dxh * xh, axis=-1, keepdims=True))

    return pl.pallas_call(
        body, grid=(segs.tiles,),
        in_specs=[_row(tm, d), _const((1, d)), segs.spec(d), segs.spec(d), _row(tm, d),
                  pl.BlockSpec((tm, d), lambda i: (jnp.minimum(i, dxr_tiles - 1), 0))],
        out_specs=[_row(tm, d), _const((1, d)), segs.spec(d), segs.spec(d)],
        out_shape=[_sds((segs.rows, d)), _sds((1, d)), _sds((segs.n, 1, d)), _sds((segs.n, 1, d))],
        compiler_params=_params("arbitrary"), name=name)(x, g, sc, sh, dh, dxr)


def _resid_fwd(x, o, gt, segs, name):
    d, tm = x.shape[1], segs.tm

    def body(x_ref, o_ref, gt_ref, y_ref):
        y_ref[...] = x_ref[...] + gt_ref[...] * o_ref[...]

    return pl.pallas_call(
        body, grid=(segs.tiles,), in_specs=[_row(tm, d), _row(tm, d), segs.spec(d)], out_specs=_row(tm, d),
        out_shape=_sds((segs.rows, d)), compiler_params=_params("parallel"), name=name)(x, o, gt)


def _resid_bwd(dxn, o, gt, segs, name):
    d, tm = o.shape[1], segs.tm

    def body(dxn_ref, o_ref, gt_ref, do_ref, dgt_ref):
        @pl.when(segs.is_first(pl.program_id(0)))
        def _():
            dgt_ref[...] = jnp.zeros_like(dgt_ref)

        dx = dxn_ref[...]
        do_ref[...] = (gt_ref[...] * dx).astype(do_ref.dtype)
        dgt_ref[...] += _rowsum(dx * o_ref[...])

    return pl.pallas_call(
        body, grid=(segs.tiles,), in_specs=[_row(tm, d), _row(tm, d), segs.spec(d)],
        out_specs=[_row(tm, d), segs.spec(d)], out_shape=[_sds((segs.rows, d), BF16), _sds((segs.n, 1, d))],
        compiler_params=_params("arbitrary"), name=name)(dxn, o, gt)


def _loss_head(y, tgt, tm, name):
    t, d = y.shape

    def body(y_ref, t_ref, l_ref, dy_ref):
        @pl.when(pl.program_id(0) == 0)
        def _():
            l_ref[...] = jnp.zeros_like(l_ref)

        e = y_ref[...] - t_ref[...]
        dy_ref[...] = e * (1.0 / d)
        l_ref[...] += _rowsum(e * e) * (0.5 / d)

    return pl.pallas_call(
        body, grid=(t // tm,), in_specs=[_row(tm, d), _row(tm, d)], out_specs=[_const((1, d)), _row(tm, d)],
        out_shape=[_sds((1, d)), _sds((t, d))], compiler_params=_params("arbitrary"), name=name)(y, tgt)


def _seq_spec(l, ce, row0, cb0=0):
    return pl.BlockSpec((l, ce), lambda j, s: (row0 // l + s, cb0 + j))


def _tap_sum(pad_ref, taps_ref, first_row, n_taps, l, ce, flip):
    out = []
    for r0 in range(0, l, CHUNK):
        rows = min(CHUNK, l - r0)
        acc = jnp.zeros((rows, ce), F32)
        for k in range(n_taps):
            kk = n_taps - 1 - k if flip else k
            acc = acc + pad_ref[pl.ds(first_row + r0 + k, rows), :] * taps_ref[kk:kk + 1, :]
        out.append(acc)
    return out


def _fill_pad(pad_ref, val, l, ce):
    pad_ref[pl.ds(0, CONV_PAD), :] = jnp.zeros((CONV_PAD, ce), F32)
    pad_ref[pl.ds(CONV_PAD + l, CONV_PAD), :] = jnp.zeros((CONV_PAD, ce), F32)
    pad_ref[pl.ds(CONV_PAD, l), :] = val


def _conv1_fwd(z, dw, db, nseq, l, row0, name, side=None, into=None):
    e = z.shape[1] // 3
    ce = LANES
    half = CONV_WIDTH // 2

    def body(a_ref, b_ref, dw_ref, db_ref, y_ref, pad_ref):
        _fill_pad(pad_ref, a_ref[...] * _sig(b_ref[...]), l, ce)
        pieces = _tap_sum(pad_ref, dw_ref, CONV_PAD - half, CONV_WIDTH, l, ce, False)
        for n, acc in enumerate(pieces):
            y_ref[pl.ds(n * CHUNK, acc.shape[0]), :] = acc + db_ref[...]

    return _pallas(
        body, (z, z, dw, db), grid=(e // ce, nseq),
        in_specs=[_seq_spec(l, ce, row0), _seq_spec(l, ce, row0, e // ce),
                  pl.BlockSpec((CONV_WIDTH, ce), lambda j, s: (0, j)), pl.BlockSpec((1, ce), lambda j, s: (0, j))],
        out_specs=_seq_spec(l, ce, row0), out_shape=_sds((z.shape[0], e)),
        scratch_shapes=[pltpu.VMEM((l + 2 * CONV_PAD, ce), F32)],
        sem=("parallel", "arbitrary"), name=name, side=side, into=None if into is None else {0: into})


def _conv1_bwd(dy2, z, dw, acc_dw, acc_db, nseq, l, row0, name, side=None, into=None):
    e = z.shape[1] // 3
    ce = LANES
    half = CONV_WIDTH // 2

    def body(dy_ref, a_ref, b_ref, dw_ref, adw_ref, adb_ref, da_ref, dbb_ref, ddw_ref, ddb_ref, ypad_ref, dpad_ref):
        @pl.when(pl.program_id(1) == 0)
        def _():
            ddw_ref[...] = adw_ref[...]
            ddb_ref[...] = adb_ref[...]

        a, sb = a_ref[...], _sig(b_ref[...])
        dy = dy_ref[...]
        _fill_pad(ypad_ref, a * sb, l, ce)
        _fill_pad(dpad_ref, dy, l, ce)
        ddb_ref[...] += _rowsum(dy)
        for k in range(CONV_WIDTH):
            ddw_ref[k:k + 1, :] += _rowsum(dy * ypad_ref[pl.ds(CONV_PAD - half + k, l), :])
        pieces = _tap_sum(dpad_ref, dw_ref, CONV_PAD - half, CONV_WIDTH, l, ce, True)
        for n, dy1 in enumerate(pieces):
            rows = pl.ds(n * CHUNK, dy1.shape[0])
            sbn = sb[n * CHUNK:n * CHUNK + dy1.shape[0], :]
            da_ref[rows, :] = (dy1 * sbn).astype(da_ref.dtype)
            dbb_ref[rows, :] = (dy1 * a[n * CHUNK:n * CHUNK + dy1.shape[0], :] * sbn * (1.0 - sbn)).astype(dbb_ref.dtype)

    cw = lambda j, s: (0, j)
    return _pallas(
        body, (dy2, z, z, dw, acc_dw, acc_db), grid=(e // ce, nseq),
        in_specs=[_seq_spec(l, ce, row0), _seq_spec(l, ce, row0), _seq_spec(l, ce, row0, e // ce),
                  pl.BlockSpec((CONV_WIDTH, ce), cw), pl.BlockSpec((CONV_WIDTH, ce), cw), pl.BlockSpec((1, ce), cw)],
        out_specs=[_seq_spec(l, ce, row0), _seq_spec(l, ce, row0),
                   pl.BlockSpec((CONV_WIDTH, ce), cw), pl.BlockSpec((1, ce), cw)],
        out_shape=[_sds((z.shape[0], e), BF16), _sds((z.shape[0], e), BF16), _sds((CONV_WIDTH, e)), _sds((1, e))],
        scratch_shapes=[pltpu.VMEM((l + 2 * CONV_PAD, ce), F32), pltpu.VMEM((l + 2 * CONV_PAD, ce), F32)],
        sem=("parallel", "arbitrary"), name=name, side=side,
        into=None if into is None else {0: into[0], 1: into[1]})


def _pool_tables(l, e):
    grp = e // len(POOL_WINDOWS)
    w = jnp.repeat(jnp.array(POOL_WINDOWS, jnp.int32), grp)[None, :]
    off = jnp.arange(POOL_TAPS, dtype=jnp.int32)[:, None] - POOL_TAPS // 2
    taps = jnp.logical_and(off >= -(w // 2), off < w - w // 2).astype(F32)
    t = jnp.arange(l, dtype=jnp.int32)[:, None]
    cnt = jnp.clip(t + (w - w // 2), 0, l) - jnp.clip(t - w // 2, 0, l)
    return taps, 1.0 / cnt.astype(F32)


def _pool1(v_src, taps, inv_cnt, nseq, l, row0, transpose, name, out_dtype, into=None):
    e = taps.shape[1]
    ce = LANES
    half = POOL_TAPS // 2

    def body(v_ref, taps_ref, ic_ref, o_ref, pad_ref):
        v = v_ref[...].astype(F32)
        if transpose:
            _fill_pad(pad_ref, v * ic_ref[...], l, ce)
            pieces = _tap_sum(pad_ref, taps_ref, CONV_PAD - half + 1, POOL_TAPS, l, ce, True)
        else:
            _fill_pad(pad_ref, v, l, ce)
            pieces = _tap_sum(pad_ref, taps_ref, CONV_PAD - half, POOL_TAPS, l, ce, False)
        for n, acc in enumerate(pieces):
            rows = pl.ds(n * CHUNK, acc.shape[0])
            vn = v[n * CHUNK:n * CHUNK + acc.shape[0], :]
            if transpose:
                o_ref[rows, :] = (acc - vn).astype(o_ref.dtype)
            else:
                o_ref[rows, :] = (acc * ic_ref[rows, :] - vn).astype(o_ref.dtype)

    return _pallas(
        body, (v_src, taps, inv_cnt), grid=(e // ce, nseq),
        in_specs=[_seq_spec(l, ce, row0), pl.BlockSpec((POOL_TAPS, ce), lambda j, s: (0, j)),
                  pl.BlockSpec((l, ce), lambda j, s: (0, j))],
        out_specs=_seq_spec(l, ce, row0), out_shape=_sds((v_src.shape[0], e), out_dtype),
        scratch_shapes=[pltpu.VMEM((l + 2 * CONV_PAD, ce), F32)],
        sem=("parallel", "arbitrary"), name=name, into=None if into is None else {0: into})


def _layernorm_parts(x, eps=EPS):
    mu = jnp.mean(x, axis=-1, keepdims=True)
    xc = x - mu
    r = lax.rsqrt(jnp.mean(xc * xc, axis=-1, keepdims=True) + eps)
    return xc * r, r


def _layernorm_bwd(dy, xh, r, g):
    dxh = dy * g
    return r * (dxh - jnp.mean(dxh, axis=-1, keepdims=True) - xh * jnp.mean(dxh * xh, axis=-1, keepdims=True))


def _conv2_fwd(y2, z, ln_g, ln_b, tm, name):
    t, e = y2.shape

    def body(y_ref, g_ref, lg_ref, lb_ref, o_ref):
        xh, _ = _layernorm_parts(y_ref[...])
        o_ref[...] = (_silu(xh * lg_ref[...] + lb_ref[...]) * _silu(g_ref[...])).astype(o_ref.dtype)

    return pl.pallas_call(
        body, grid=(t // tm,), in_specs=[_row(tm, e), _row(tm, e, 2), _const((1, e)), _const((1, e))],
        out_specs=_row(tm, e), out_shape=_sds((t, e), BF16), compiler_params=_params("parallel"),
        name=name)(y2, z, ln_g, ln_b)


def _conv2_bwd(dy4, y2, z, ln_g, ln_b, tm, name):
    t, e = y2.shape

    def body(dy_ref, y_ref, g_ref, lg_ref, lb_ref, dy2_ref, dg_ref, dlg_ref, dlb_ref):
        @pl.when(pl.program_id(0) == 0)
        def _():
            dlg_ref[...] = jnp.zeros_like(dlg_ref)
            dlb_ref[...] = jnp.zeros_like(dlb_ref)

        dy, gz = dy_ref[...], g_ref[...]
        xh, r = _layernorm_parts(y_ref[...])
        y3 = xh * lg_ref[...] + lb_ref[...]
        dg_ref[...] = (dy * _silu(y3) * _dsilu(gz)).astype(dg_ref.dtype)
        dy3 = dy * _silu(gz) * _dsilu(y3)
        dlg_ref[...] += _rowsum(dy3 * xh)
        dlb_ref[...] += _rowsum(dy3)
        dy2_ref[...] = _layernorm_bwd(dy3, xh, r, lg_ref[...])

    return pl.pallas_call(
        body, grid=(t // tm,),
        in_specs=[_row(tm, e), _row(tm, e), _row(tm, e, 2), _const((1, e)), _const((1, e))],
        out_specs=[_row(tm, e), _row(tm, e), _const((1, e)), _const((1, e))],
        out_shape=[_sds((t, e)), _sds((t, e), BF16), _sds((1, e)), _sds((1, e))],
        compiler_params=_params("arbitrary"), name=name)(dy4, y2, z, ln_g, ln_b)


def _pool2_fwd(pm, w_grp, scale, z, tm, name):
    t, e = pm.shape
    ng, gw = w_grp.shape[0], w_grp.shape[1]

    def body(pm_ref, w_ref, sc_ref, g_ref, o_ref):
        for k in range(ng):
            cols = slice(k * gw, (k + 1) * gw)
            y = _dot(pm_ref[:, cols], w_ref[k])
            o_ref[:, cols] = (y * sc_ref[:, cols] * _silu(g_ref[:, cols])).astype(o_ref.dtype)

    return pl.pallas_call(
        body, grid=(t // tm,), in_specs=[_row(tm, e), _const(w_grp.shape), _const((1, e)), _row(tm, e, 1)],
        out_specs=_row(tm, e), out_shape=_sds((t, e), BF16), compiler_params=_params("parallel"),
        name=name)(pm, w_grp, scale, z)


def _pool2_bwd(dy2, pm, w_grp, scale, z, tm, name):
    t, e = pm.shape
    ng, gw = w_grp.shape[0], w_grp.shape[1]

    def body(dy_ref, pm_ref, w_ref, sc_ref, g_ref, dpm_ref, dg_ref, dsc_ref, dw_ref):
        @pl.when(pl.program_id(0) == 0)
        def _():
            dsc_ref[...] = jnp.zeros_like(dsc_ref)
            dw_ref[...] = jnp.zeros_like(dw_ref)

        for k in range(ng):
            cols = slice(k * gw, (k + 1) * gw)
            dy, gz, sc, pmk = dy_ref[:, cols], g_ref[:, cols], sc_ref[:, cols], pm_ref[:, cols]
            y = _dot(pmk, w_ref[k])
            dg_ref[:, cols] = (dy * (y * sc) * _dsilu(gz)).astype(dg_ref.dtype)
            dys = dy * _silu(gz)
            dsc_ref[:, cols] += _rowsum(dys * y)
            dyk = dys * sc
            dpm_ref[:, cols] = _dot_nt(dyk, w_ref[k])
            dw_ref[k] += _dot_tn(pmk, dyk)

    return pl.pallas_call(
        body, grid=(t // tm,),
        in_specs=[_row(tm, e), _row(tm, e), _const(w_grp.shape), _const((1, e)), _row(tm, e, 1)],
        out_specs=[_row(tm, e), _row(tm, e), _const((1, e)), _const(w_grp.shape)],
        out_shape=[_sds((t, e)), _sds((t, e), BF16), _sds((1, e)), _sds(w_grp.shape)],
        compiler_params=_params("arbitrary"), name=name)(dy2, pm, w_grp, scale, z)


def _rms_f(x, g, n):
    r = lax.rsqrt(jnp.sum(x * x, axis=-1, keepdims=True) * (1.0 / n) + EPS)
    return x * r * g


def _rms_b(x, g, dy, n):
    r = lax.rsqrt(jnp.sum(x * x, axis=-1, keepdims=True) * (1.0 / n) + EPS)
    xh = x * r
    dxh = dy * g
    return r * (dxh - xh * (jnp.sum(dxh * xh, axis=-1, keepdims=True) * (1.0 / n))), dy * xh


def _swap16(x):
    lane = lax.broadcasted_iota(jnp.int32, x.shape, 1)
    return jnp.where(lane % 32 < 16, pltpu.roll(x, LANES - 16, 1), pltpu.roll(x, 16, 1))


def _rope(x, c, s):
    return x * c + _swap16(x) * s


def _rope_t(dy, c, s):
    return dy * c + _swap16(dy * s)


def _rope_tables(l, lc, nb):
    t = jnp.arange(l, dtype=jnp.int32)
    row_id, col_id = (t // GRID_W).astype(F32), (t % GRID_W).astype(F32)
    axis_dim = MLA_ROPE // 2
    freqs = ROPE_THETA ** (-jnp.arange(0, axis_dim, 2, dtype=F32) / axis_dim)
    ar, ac = row_id[:, None] * freqs, col_id[:, None] * freqs
    pad1, pad0 = jnp.ones((l, LANES - MLA_ROPE), F32), jnp.zeros((l, LANES - MLA_ROPE), F32)
    ctab = jnp.concatenate([jnp.cos(ar), jnp.cos(ar), jnp.cos(ac), jnp.cos(ac), pad1], axis=1)
    stab = jnp.concatenate([-jnp.sin(ar), jnp.sin(ar), -jnp.sin(ac), jnp.sin(ac), pad0], axis=1)
    ctab = jnp.concatenate([jnp.tile(ctab, (nb, 1)), jnp.ones((nb * lc, LANES), F32)], axis=0)
    stab = jnp.concatenate([jnp.tile(stab, (nb, 1)), jnp.zeros((nb * lc, LANES), F32)], axis=0)
    return ctab, stab


def _kv_pre_fwd(zkv, kv_norm, rope_g, ctab, stab, tm, name):
    t = zkv.shape[0]

    def body(z_ref, gk_ref, gr_ref, c_ref, s_ref, ck_ref, kr_ref):
        ck_ref[...] = _rms_f(z_ref[:, :MLA_KV_RANK], gk_ref[...], MLA_KV_RANK).astype(ck_ref.dtype)
        kr = _rms_f(z_ref[:, MLA_KV_RANK:], gr_ref[...], MLA_ROPE)
        kr_ref[...] = _rope(kr, c_ref[...], s_ref[...]).astype(kr_ref.dtype)

    w = MLA_KV_RANK + LANES
    return pl.pallas_call(
        body, grid=(t // tm,),
        in_specs=[_row(tm, w), _const((1, MLA_KV_RANK)), _const((1, LANES)), _row(tm, LANES), _row(tm, LANES)],
        out_specs=[_row(tm, MLA_KV_RANK), _row(tm, LANES)],
        out_shape=[_sds((t, MLA_KV_RANK), BF16), _sds((t, LANES), BF16)],
        compiler_params=_params("parallel"), name=name)(zkv, kv_norm, rope_g, ctab, stab)


def _kv_pre_bwd(dck, dkr, zkv, kv_norm, rope_g, ctab, stab, tm, name):
    t = zkv.shape[0]
    w = MLA_KV_RANK + LANES

    def body(dck_ref, dkr_ref, z_ref, gk_ref, gr_ref, c_ref, s_ref, dz_ref, dgk_ref, dgr_ref):
        @pl.when(pl.program_id(0) == 0)
        def _():
            dgk_ref[...] = jnp.zeros_like(dgk_ref)
            dgr_ref[...] = jnp.zeros_like(dgr_ref)

        dx, dg = _rms_b(z_ref[:, :MLA_KV_RANK], gk_ref[...], dck_ref[...], MLA_KV_RANK)
        dz_ref[:, :MLA_KV_RANK] = dx.astype(dz_ref.dtype)
        dgk_ref[...] += _rowsum(dg)
        dy = _rope_t(dkr_ref[...], c_ref[...], s_ref[...])
        dx, dg = _rms_b(z_ref[:, MLA_KV_RANK:], gr_ref[...], dy, MLA_ROPE)
        dz_ref[:, MLA_KV_RANK:] = dx.astype(dz_ref.dtype)
        dgr_ref[...] += _rowsum(dg)

    return pl.pallas_call(
        body, grid=(t // tm,),
        in_specs=[_row(tm, MLA_KV_RANK), _row(tm, LANES), _row(tm, w), _const((1, MLA_KV_RANK)), _const((1, LANES)),
                  _row(tm, LANES), _row(tm, LANES)],
        out_specs=[_row(tm, w), _const((1, MLA_KV_RANK)), _const((1, LANES))],
        out_shape=[_sds((t, w), BF16), _sds((1, MLA_KV_RANK)), _sds((1, LANES))],
        compiler_params=_params("arbitrary"), name=name)(dck, dkr, zkv, kv_norm, rope_g, ctab, stab)


def _q_pre_fwd(zq, q_norm, tm, name):
    t, w = zq.shape

    def body(z_ref, g_ref, o_ref):
        o_ref[...] = _rms_f(z_ref[...], g_ref[...], w).astype(o_ref.dtype)

    return pl.pallas_call(
        body, grid=(t // tm,), in_specs=[_row(tm, w), _const((1, w))], out_specs=_row(tm, w),
        out_shape=_sds((t, w), BF16), compiler_params=_params("parallel"), name=name)(zq, q_norm)


def _q_pre_bwd(dcq, zq, q_norm, tm, name):
    t, w = zq.shape

    def body(d_ref, z_ref, g_ref, dz_ref, dg_ref):
        @pl.when(pl.program_id(0) == 0)
        def _():
            dg_ref[...] = jnp.zeros_like(dg_ref)

        dx, dg = _rms_b(z_ref[...], g_ref[...], d_ref[...], w)
        dz_ref[...] = dx.astype(dz_ref.dtype)
        dg_ref[...] += _rowsum(dg)

    return pl.pallas_call(
        body, grid=(t // tm,), in_specs=[_row(tm, w), _row(tm, w), _const((1, w))],
        out_specs=[_row(tm, w), _const((1, w))], out_shape=[_sds((t, w), BF16), _sds((1, w))],
        compiler_params=_params("arbitrary"), name=name)(dcq, zq, q_norm)


def _q_post_fwd(q, nope_g, rope_g, ctab, stab, tm, name):
    t, w = q.shape

    def body(q_ref, gn_ref, gr_ref, c_ref, s_ref, o_ref):
        for h in range(MLA_HEADS):
            a = h * HEAD_W
            qn = _rms_f(q_ref[:, a:a + LANES], gn_ref[...], MLA_NOPE)
            o_ref[:, a:a + LANES] = (qn * MLA_SCALE).astype(o_ref.dtype)
            qr = _rms_f(q_ref[:, a + LANES:a + HEAD_W], gr_ref[...], MLA_ROPE)
            o_ref[:, a + LANES:a + HEAD_W] = (_rope(qr, c_ref[...], s_ref[...]) * MLA_SCALE).astype(o_ref.dtype)

    return pl.pallas_call(
        body, grid=(t // tm,),
        in_specs=[_row(tm, w), _const((1, LANES)), _const((1, LANES)), _row(tm, LANES), _row(tm, LANES)],
        out_specs=_row(tm, w), out_shape=_sds((t, w), BF16), compiler_params=_params("parallel"),
        name=name)(q, nope_g, rope_g, ctab, stab)


def _q_post_bwd(dqf, q, nope_g, rope_g, ctab, stab, tm, name):
    t, w = q.shape

    def body(d_ref, q_ref, gn_ref, gr_ref, c_ref, s_ref, dq_ref, dgn_ref, dgr_ref):
        @pl.when(pl.program_id(0) == 0)
        def _():
            dgn_ref[...] = jnp.zeros_like(dgn_ref)
            dgr_ref[...] = jnp.zeros_like(dgr_ref)

        for h in range(MLA_HEADS):
            a = h * HEAD_W
            dx, dg = _rms_b(q_ref[:, a:a + LANES], gn_ref[...], d_ref[:, a:a + LANES] * MLA_SCALE, MLA_NOPE)
            dq_ref[:, a:a + LANES] = dx.astype(dq_ref.dtype)
            dgn_ref[...] += _rowsum(dg)
            dy = _rope_t(d_ref[:, a + LANES:a + HEAD_W] * MLA_SCALE, c_ref[...], s_ref[...])
            dx, dg = _rms_b(q_ref[:, a + LANES:a + HEAD_W], gr_ref[...], dy, MLA_ROPE)
            dq_ref[:, a + LANES:a + HEAD_W] = dx.astype(dq_ref.dtype)
            dgr_ref[...] += _rowsum(dg)

    return pl.pallas_call(
        body, grid=(t // tm,),
        in_specs=[_row(tm, w), _row(tm, w), _const((1, LANES)), _const((1, LANES)), _row(tm, LANES), _row(tm, LANES)],
        out_specs=[_row(tm, w), _const((1, LANES)), _const((1, LANES))],
        out_shape=[_sds((t, w), BF16), _sds((1, LANES)), _sds((1, LANES))],
        compiler_params=_params("arbitrary"), name=name)(dqf, q, nope_g, rope_g, ctab, stab)


def _k_post_fwd(kv, krr, nope_g, tm, name):
    t, w = kv.shape

    def body(kv_ref, kr_ref, gn_ref, k_ref, v_ref):
        for h in range(MLA_HEADS):
            a = h * HEAD_W
            k_ref[:, a:a + LANES] = _rms_f(kv_ref[:, a:a + LANES], gn_ref[...], MLA_NOPE).astype(k_ref.dtype)
            k_ref[:, a + LANES:a + HEAD_W] = kr_ref[...]
            v_ref[:, a:a + LANES] = kv_ref[:, a + LANES:a + HEAD_W].astype(v_ref.dtype)
            v_ref[:, a + LANES:a + HEAD_W] = jnp.ones((tm, LANES), v_ref.dtype)

    return pl.pallas_call(
        body, grid=(t // tm,), in_specs=[_row(tm, w), _row(tm, LANES), _const((1, LANES))],
        out_specs=[_row(tm, w), _row(tm, w)], out_shape=[_sds((t, w), BF16), _sds((t, w), BF16)],
        compiler_params=_params("parallel"), name=name)(kv, krr, nope_g)


def _k_post_bwd(dkl, dkc, dvl, dvc, kv, nope_g, tm, name):
    t, w = kv.shape
    nl = dkl.shape[0] // tm

    def body(dkl_ref, dkc_ref, dvl_ref, dvc_ref, kv_ref, gn_ref, dkv_ref, dkr_ref, dgn_ref):
        i = pl.program_id(0)

        @pl.when(i == 0)
        def _():
            dgn_ref[...] = jnp.zeros_like(dgn_ref)

        dkr = jnp.zeros(dkr_ref.shape, F32)
        for h in range(MLA_HEADS):
            a = h * HEAD_W
            dk = jnp.where(i < nl, dkl_ref[:, a:a + HEAD_W], dkc_ref[:, a:a + HEAD_W])
            dv = jnp.where(i < nl, dvl_ref[:, h * LANES:(h + 1) * LANES], dvc_ref[:, h * LANES:(h + 1) * LANES])
            dx, dg = _rms_b(kv_ref[:, a:a + LANES], gn_ref[...], dk[:, :LANES], MLA_NOPE)
            dkv_ref[:, a:a + LANES] = dx.astype(dkv_ref.dtype)
            dgn_ref[...] += _rowsum(dg)
            dkv_ref[:, a + LANES:a + HEAD_W] = dv.astype(dkv_ref.dtype)
            dkr = dkr + dk[:, LANES:]
        dkr_ref[...] = dkr

    lat = lambda cols: pl.BlockSpec((tm, cols), lambda i: (jnp.minimum(i, nl - 1), 0))
    ctx = lambda cols: pl.BlockSpec((tm, cols), lambda i: (jnp.maximum(i - nl, 0), 0))
    return pl.pallas_call(
        body, grid=(t // tm,),
        in_specs=[lat(w), ctx(w), lat(w // 2), ctx(w // 2), _row(tm, w), _const((1, LANES))],
        out_specs=[_row(tm, w), _row(tm, LANES), _const((1, LANES))],
        out_shape=[_sds((t, w), BF16), _sds((t, LANES)), _sds((1, LANES))],
        compiler_params=_params("arbitrary"), name=name)(dkl, dkc, dvl, dvc, kv, nope_g)


def _attn_specs(nb, l, lc, tq):
    nq = l // tq
    ctx0 = nb * l // lc
    q_spec = lambda w: pl.BlockSpec((tq, w), lambda b, h, i: (b * nq + i, h))
    lat = lambda w, step=1: pl.BlockSpec((l, w), lambda b, h, i: (b, step * h))
    ctx = lambda w, step=1: pl.BlockSpec((lc, w), lambda b, h, i: (ctx0 + b, step * h))
    return nq, q_spec, lat, ctx


def _attn_fwd(qf, kf, vf, nb, l, lc, name):
    tq = _pick(l, (256, 128))
    nq, q_spec, lat, ctx = _attn_specs(nb, l, lc, tq)

    def body(q_ref, kl_ref, kc_ref, vl_ref, vc_ref, o_ref, lse_ref):
        q = q_ref[...]
        s1, s2 = _dot_nt(q, kl_ref[...]), _dot_nt(q, kc_ref[...])
        m = jnp.maximum(jnp.max(s1, axis=-1, keepdims=True), jnp.max(s2, axis=-1, keepdims=True))
        p1, p2 = jnp.exp((s1 - m).astype(BF16)), jnp.exp((s2 - m).astype(BF16))
        acc = _dot(p1, vl_ref[...]) + _dot(p2, vc_ref[...])
        den = acc[:, LANES:LANES + 1]
        o_ref[...] = acc[:, :LANES] / den
        lse_ref[...] = jnp.broadcast_to(m + jnp.log(den), lse_ref.shape)

    return pl.pallas_call(
        body, grid=(nb, MLA_HEADS, nq),
        in_specs=[q_spec(HEAD_W), lat(HEAD_W), ctx(HEAD_W), lat(HEAD_W), ctx(HEAD_W)],
        out_specs=[q_spec(LANES), q_spec(LANES)],
        out_shape=[_sds((nb * l, MLA_HEADS * LANES)), _sds((nb * l, MLA_HEADS * LANES))],
        compiler_params=_params("parallel", "parallel", "arbitrary"), name=name)(qf, kf, kf, vf, vf)


def _attn_bwd(do, o, lse, qf, kf, vf, nb, l, lc, name, side=None):
    tq = _pick(l, (256, 128))
    nq, q_spec, lat, ctx = _attn_specs(nb, l, lc, tq)
    out_lat = lambda w: pl.BlockSpec((l, w), lambda b, h, i: (b, h))
    out_ctx = lambda w: pl.BlockSpec((lc, w), lambda b, h, i: (b, h))

    def body(do_ref, o_ref, lse_ref, q_ref, kl_ref, kc_ref, vl_ref, vc_ref, dq_ref, dkl_ref, dkc_ref, dvl_ref, dvc_ref):
        @pl.when(pl.program_id(2) == 0)
        def _():
            dkl_ref[...] = jnp.zeros_like(dkl_ref)
            dkc_ref[...] = jnp.zeros_like(dkc_ref)
            dvl_ref[...] = jnp.zeros_like(dvl_ref)
            dvc_ref[...] = jnp.zeros_like(dvc_ref)

        q, dof = q_ref[...], do_ref[...]
        delta = jnp.sum(dof * o_ref[...], axis=-1, keepdims=True)
        lse = lse_ref[:, :1]
        dq = jnp.zeros(dq_ref.shape, F32)
        for k_ref, v_ref, dk_ref, dv_ref in ((kl_ref, vl_ref, dkl_ref, dvl_ref), (kc_ref, vc_ref, dkc_ref, dvc_ref)):
            p = jnp.exp((_dot_nt(q, k_ref[...]) - lse).astype(BF16))
            ds = p * (_dot_nt(dof, v_ref[...]) - delta).astype(BF16)
            dq = dq + _dot(ds, k_ref[...])
            dk_ref[...] += _dot_tn(ds, q)
            dv_ref[...] += _dot_tn(p, dof)
        dq_ref[...] = dq

    kw, vw = MLA_HEADS * HEAD_W, MLA_HEADS * LANES
    return _pallas(
        body, (do, o, lse, qf, kf, kf, vf, vf), grid=(nb, MLA_HEADS, nq),
        in_specs=[q_spec(LANES), q_spec(LANES), q_spec(LANES), q_spec(HEAD_W), lat(HEAD_W), ctx(HEAD_W),
                  lat(LANES, 2), ctx(LANES, 2)],
        out_specs=[q_spec(HEAD_W), out_lat(HEAD_W), out_ctx(HEAD_W), out_lat(LANES), out_ctx(LANES)],
        out_shape=[_sds((nb * l, kw)), _sds((nb * l, kw)), _sds((nb * lc, kw)), _sds((nb * l, vw)),
                   _sds((nb * lc, vw))],
        sem=("parallel", "parallel", "arbitrary"), name=name, side=side)


def _gate_fwd(o, g, tm, name):
    t, e = o.shape

    def body(o_ref, g_ref, y_ref):
        y_ref[...] = (o_ref[...] * _silu(g_ref[...])).astype(y_ref.dtype)

    return pl.pallas_call(
        body, grid=(t // tm,), in_specs=[_row(tm, e), _row(tm, e)], out_specs=_row(tm, e),
        out_shape=_sds((t, e), BF16), compiler_params=_params("parallel"), name=name)(o, g)


def _gate_bwd(dy, o, g, tm, name):
    t, e = o.shape

    def body(dy_ref, o_ref, g_ref, do_ref, dg_ref):
        dy, gz = dy_ref[...], g_ref[...]
        do_ref[...] = dy * _silu(gz)
        dg_ref[...] = (dy * o_ref[...] * _dsilu(gz)).astype(dg_ref.dtype)

    return pl.pallas_call(
        body, grid=(t // tm,), in_specs=[_row(tm, e), _row(tm, e), _row(tm, e)],
        out_specs=[_row(tm, e), _row(tm, e)], out_shape=[_sds((t, e)), _sds((t, e), BF16)],
        compiler_params=_params("parallel"), name=name)(dy, o, g)


def _chunk_fwd(z, ln_g, ln_b, w_s, bs_full, name):
    t, e = z.shape[0], z.shape[1] // 3

    def body(u_ref, v_ref, g_ref, lg_ref, lb_ref, w_ref, bs_ref, y_ref):
        xh, _ = _layernorm_parts(v_ref[...])
        vn = xh * lg_ref[...] + lb_ref[...]
        for k in range(CHUNK_GROUPS):
            cols = slice(k * LANES, (k + 1) * LANES)
            s = _dot(w_ref[k], vn[:, cols]) + bs_ref[:, cols]
            y_ref[:, cols] = (u_ref[:, cols] * s * _silu(g_ref[:, cols])).astype(y_ref.dtype)

    return pl.pallas_call(
        body, grid=(t // CHUNK,),
        in_specs=[_row(CHUNK, e, 0), _row(CHUNK, e, 1), _row(CHUNK, e, 2), _const((1, e)), _const((1, e)),
                  _const(w_s.shape), _const((CHUNK, e))],
        out_specs=_row(CHUNK, e), out_shape=_sds((t, e), BF16), compiler_params=_params("parallel"),
        name=name)(z, z, z, ln_g, ln_b, w_s, bs_full)


def _chunk_bwd(dy, z, ln_g, ln_b, w_s, bs_full, name):
    t, e = z.shape[0], z.shape[1] // 3

    def body(dy_ref, u_ref, v_ref, g_ref, lg_ref, lb_ref, w_ref, bs_ref, dz_ref, dw_ref, dbs_ref, dlg_ref, dlb_ref,
             acc_ref):
        i = pl.program_id(0)

        @pl.when(i == 0)
        def _():
            dw_ref[...] = jnp.zeros_like(dw_ref)
            dlg_ref[...] = jnp.zeros_like(dlg_ref)
            dlb_ref[...] = jnp.zeros_like(dlb_ref)
            acc_ref[...] = jnp.zeros_like(acc_ref)

        xh, r = _layernorm_parts(v_ref[...])
        vn = xh * lg_ref[...] + lb_ref[...]
        dvn = []
        for k in range(CHUNK_GROUPS):
            cols = slice(k * LANES, (k + 1) * LANES)
            dyk, u, gz = dy_ref[:, cols], u_ref[:, cols], g_ref[:, cols]
            s = _dot(w_ref[k], vn[:, cols]) + bs_ref[:, cols]
            sg = _silu(gz)
            dz_ref[:, cols] = (dyk * s * sg).astype(dz_ref.dtype)
            dz_ref[:, 2 * e + k * LANES:2 * e + (k + 1) * LANES] = (dyk * u * s * _dsilu(gz)).astype(dz_ref.dtype)
            ds = dyk * u * sg
            acc_ref[:, cols] += ds
            dw_ref[k] += _dot_nt(ds, vn[:, cols])
            dvn.append(_dot_tn(w_ref[k], ds))
        dvn = jnp.concatenate(dvn, axis=1)
        dlg_ref[...] += _rowsum(dvn * xh)
        dlb_ref[...] += _rowsum(dvn)
        dz_ref[:, e:2 * e] = _layernorm_bwd(dvn, xh, r, lg_ref[...]).astype(dz_ref.dtype)

        @pl.when(i == pl.num_programs(0) - 1)
        def _():
            lane = lax.broadcasted_iota(jnp.int32, dbs_ref.shape, 1)
            out = jnp.zeros(dbs_ref.shape, F32)
            for k in range(CHUNK_GROUPS):
                col = jnp.sum(acc_ref[:, k * LANES:(k + 1) * LANES], axis=1, keepdims=True)
                out = jnp.where(lane == k, col, out)
            dbs_ref[...] = out

    return pl.pallas_call(
        body, grid=(t // CHUNK,),
        in_specs=[_row(CHUNK, e), _row(CHUNK, e, 0), _row(CHUNK, e, 1), _row(CHUNK, e, 2), _const((1, e)),
                  _const((1, e)), _const(w_s.shape), _const((CHUNK, e))],
        out_specs=[_row(CHUNK, 3 * e), _const(w_s.shape), _const((CHUNK, CHUNK_GROUPS)), _const((1, e)),
                   _const((1, e))],
        out_shape=[_sds((t, 3 * e), BF16), _sds(w_s.shape), _sds((CHUNK, CHUNK_GROUPS)), _sds((1, e)), _sds((1, e))],
        scratch_shapes=[pltpu.VMEM((CHUNK, e), F32)],
        compiler_params=_params("arbitrary"), name=name)(dy, z, z, z, ln_g, ln_b, w_s, bs_full)


def _mod_rows(mods, layer, d, nseg):
    m = mods[layer, :nseg]
    return [m[:, None, k * d:(k + 1) * d] for k in range(3)]


def _local_step(x, ctx, tgt, w, mods, comm=None):
    nb, l, d = x.shape
    lc = ctx.shape[1]
    e = d
    tl, ta = nb * l, nb * (l + lc)
    tm = _pick(lc, (256, 128))
    segs_a, segs_l = _Segs((l,) * nb + (lc,) * nb, tm), _Segs((l,) * nb, tm)
    norm_g = w['norm_g']
    g = {}

    def carried(tag, fn, *args, **kw):
        if comm is None:
            return fn(*args, **kw)
        res, brought = fn(*args, side=comm.side(tag), **kw)
        comm.done(tag, brought)
        return res

    xa0 = jnp.concatenate([x.reshape(tl, d), ctx.reshape(nb * lc, d)], axis=0)

    sh0, sc0, gt0 = _mod_rows(mods, 0, d, 2 * nb)
    h0 = _rms_mod_fwd(xa0, norm_g[0:1], sc0, sh0, segs_a, "l0_norm")
    z0 = carried('fwd1', _mm, h0, w['cv_w_in'], "l0_in")
    y2_0 = carried('fwd2', _conv1_fwd, z0, w['cv_dw'], w['cv_db'], nb, l, 0, "l0_conv_lat")
    y2_0 = _conv1_fwd(z0, w['cv_dw'], w['cv_db'], nb, lc, tl, "l0_conv_ctx", into=y2_0)
    y4_0 = _conv2_fwd(y2_0, z0, w['cv_ln_g'], w['cv_ln_b'], tm, "l0_gate")
    o0, xa1 = _mm(y4_0, w['cv_w_out'], "l0_out", resid=(xa0, gt0, segs_a))

    sh1, sc1, gt1 = _mod_rows(mods, 1, d, 2 * nb)
    h1 = _rms_mod_fwd(xa1, norm_g[1:2], sc1, sh1, segs_a, "l1_norm")
    z1 = carried('fwd3', _mm, h1, w['pl_w_in'], "l1_in")
    taps_l, ic_l = _pool_tables(l, e)
    taps_c, ic_c = _pool_tables(lc, e)
    pm1 = _pool1(z1, taps_l, ic_l, nb, l, 0, False, "l1_pool_lat", BF16)
    pm1 = _pool1(z1, taps_c, ic_c, nb, lc, tl, False, "l1_pool_ctx", BF16, into=pm1)
    y2_1 = _pool2_fwd(pm1, w['pl_w_grp'], w['pl_scale'], z1, tm, "l1_group")
    o1, xa2 = _mm(y2_1, w['pl_w_out'], "l1_out", resid=(xa1, gt1, segs_a))

    sh2, sc2, gt2 = _mod_rows(mods, 2, d, 2 * nb)
    h2 = _rms_mod_fwd(xa2, norm_g[2:3], sc2, sh2, segs_a, "l2_norm")
    w_in = w['ml_w_in']
    kvc = MLA_KV_RANK + MLA_ROPE
    w_in_p = jnp.concatenate([w_in[:, :kvc], jnp.zeros((d, LANES - MLA_ROPE), w_in.dtype), w_in[:, kvc:]], axis=1)
    w_uq_p = jnp.pad(w['ml_w_uq'].reshape(MLA_Q_RANK, MLA_HEADS, MLA_NOPE + MLA_ROPE),
                     ((0, 0), (0, 0), (0, HEAD_W - MLA_NOPE - MLA_ROPE))).reshape(MLA_Q_RANK, MLA_HEADS * HEAD_W)
    rope_g = jnp.pad(w['ml_rope_norm'], ((0, 0), (0, LANES - MLA_ROPE)))
    nope_g = w['ml_nope_norm']
    ctab, stab = _rope_tables(l, lc, nb)
    kvw = MLA_KV_RANK + LANES
    w_kv, w_q, w_g = w_in_p[:, :kvw], w_in_p[:, kvw:kvw + MLA_Q_RANK], w_in_p[:, kvw + MLA_Q_RANK:]
    zkv = _mm(h2, w_kv, "l2_in_kv")
    zq, zg = _mm(h2, w_q, "l2_in_q", rows=tl), _mm(h2, w_g, "l2_in_g", rows=tl)
    ckvn, krr = _kv_pre_fwd(zkv, w['ml_kv_norm'], rope_g[1:2], ctab, stab, tm, "l2_kv_pre")
    cqn = _q_pre_fwd(zq, w['ml_q_norm'], tm, "l2_q_pre")
    q2 = _mm(cqn, w_uq_p, "l2_uq")
    kv2 = _mm(ckvn, w['ml_w_ukv'], "l2_ukv")
    qf = _q_post_fwd(q2, nope_g[0:1], rope_g[0:1], ctab, stab, tm, "l2_q_post")
    kf, vf = _k_post_fwd(kv2, krr, nope_g[1:2], tm, "l2_k_post")
    o_att, lse = _attn_fwd(qf, kf, vf, nb, l, lc, "l2_attn")
    og = _gate_fwd(o_att, zg, tm, "l2_gate")
    o2, x3 = _mm(og, w['ml_w_out'], "l2_out", resid=(xa2, gt2[:nb], segs_l))

    sh3, sc3, gt3 = _mod_rows(mods, 3, d, nb)
    h3 = _rms_mod_fwd(x3, norm_g[3:4], sc3, sh3, segs_l, "l3_norm")
    z3 = _mm(h3, w['ch_w_in'], "l3_in")
    bs_full = jnp.repeat(w['ch_b_s'], e // CHUNK_GROUPS, axis=1)
    y3 = _chunk_fwd(z3, w['ch_ln_g'], w['ch_ln_b'], w['ch_w_s'], bs_full, "l3_chunk")
    o3, x4 = _mm(y3, w['ch_w_out'], "l3_out", resid=(x3, gt3, segs_l))

    loss_vec, dx4 = _loss_head(x4, tgt.reshape(tl, d), tm, "loss")

    do3, dgt3 = _resid_bwd(dx4, o3, gt3, segs_l, "l3_resid_b")
    dy3 = _mm_nt(do3, w['ch_w_out'], "l3_out_bx")
    g['ch_w_out'] = _mm_tn(y3, do3, "l3_out_bw")
    dz3, g['ch_w_s'], g['ch_b_s'], g['ch_ln_g'], g['ch_ln_b'] = _chunk_bwd(
        dy3, z3, w['ch_ln_g'], w['ch_ln_b'], w['ch_w_s'], bs_full, "l3_chunk_b")
    dh3 = _mm_nt(dz3, w['ch_w_in'], "l3_in_bx")
    g['ch_w_in'] = _mm_tn(h3, dz3, "l3_in_bw", shards=N_DEV)
    dx3, dng3, dsc3, dsh3 = _rms_mod_bwd(x3, norm_g[3:4], sc3, sh3, dh3, dx4, segs_l, "l3_norm_b")
    if comm is not None:
        comm.grads_ready(3, g)

    do2, dgt2 = _resid_bwd(dx3, o2, gt2[:nb], segs_l, "l2_resid_b")
    dog = _mm_nt(do2, w['ml_w_out'], "l2_out_bx")
    g['ml_w_out'] = _mm_tn(og, do2, "l2_out_bw")
    d_att, dzg = _gate_bwd(dog, o_att, zg, tm, "l2_gate_b")
    dqf, dkl, dkc, dvl, dvc = carried('quad3', _attn_bwd, d_att, o_att, lse, qf, kf, vf, nb, l, lc, "l2_attn_b")
    dq2, dnope_q, drope_q = _q_post_bwd(dqf, q2, nope_g[0:1], rope_g[0:1], ctab, stab, tm, "l2_q_post_b")
    dkv2, dkrr, dnope_k = _k_post_bwd(dkl, dkc, dvl, dvc, kv2, nope_g[1:2], tm, "l2_k_post_b")
    dcqn = _mm_nt(dq2, w_uq_p, "l2_uq_bx")
    g_uq_p = _mm_tn(cqn, dq2, "l2_uq_bw")
    dckvn = _mm_nt(dkv2, w['ml_w_ukv'], "l2_ukv_bx")
    g['ml_w_ukv'] = _mm_tn(ckvn, dkv2, "l2_ukv_bw", shards=N_DEV)
    dzq, g['ml_q_norm'] = _q_pre_bwd(dcqn, zq, w['ml_q_norm'], tm, "l2_q_pre_b")
    dzkv, g['ml_kv_norm'], drope_k = _kv_pre_bwd(dckvn, dkrr, zkv, w['ml_kv_norm'], rope_g[1:2], ctab, stab, tm,
                                                  "l2_kv_pre_b")
    dh2 = _mm_nt_sum([(dzkv, w_kv), (dzq, w_q), (dzg, w_g)], "l2_in_bx")
    g['ml_w_in'] = jnp.concatenate([_mm_tn(h2, dzkv, "l2_in_kv_bw")[:, :kvc], _mm_tn(h2, dzq, "l2_in_q_bw", rows=tl),
                                    _mm_tn(h2, dzg, "l2_in_g_bw", rows=tl)], axis=1)
    g['ml_w_uq'] = g_uq_p.reshape(MLA_Q_RANK, MLA_HEADS, HEAD_W)[:, :, :MLA_NOPE + MLA_ROPE].reshape(
        MLA_Q_RANK, MLA_HEADS * (MLA_NOPE + MLA_ROPE))
    g['ml_nope_norm'] = jnp.concatenate([dnope_q, dnope_k], axis=0)
    g['ml_rope_norm'] = jnp.concatenate([drope_q, drope_k], axis=0)[:, :MLA_ROPE]
    dxa2, dng2, dsc2, dsh2 = _rms_mod_bwd(xa2, norm_g[2:3], sc2, sh2, dh2, dx3, segs_a, "l2_norm_b")
    if comm is not None:
        comm.grads_ready(2, g)

    do1, dgt1 = _resid_bwd(dxa2, o1, gt1, segs_a, "l1_resid_b")
    dy2_1 = _mm_nt(do1, w['pl_w_out'], "l1_out_bx")
    g['pl_w_out'] = _mm_tn(y2_1, do1, "l1_out_bw")
    dpm, dgz1, g['pl_scale'], g['pl_w_grp'] = _pool2_bwd(dy2_1, pm1, w['pl_w_grp'], w['pl_scale'], z1, tm,
                                                          "l1_group_b")
    dv1 = _pool1(dpm, taps_l, ic_l, nb, l, 0, True, "l1_pool_lat_b", BF16)
    dv1 = _pool1(dpm, taps_c, ic_c, nb, lc, tl, True, "l1_pool_ctx_b", BF16, into=dv1)
    dz1 = jnp.concatenate([dv1, dgz1], axis=1)
    dh1 = _mm_nt(dz1, w['pl_w_in'], "l1_in_bx")
    g['pl_w_in'] = carried('quad2', _mm_tn, h1, dz1, "l1_in_bw", shards=N_DEV)
    dxa1, dng1, dsc1, dsh1 = _rms_mod_bwd(xa1, norm_g[1:2], sc1, sh1, dh1, dxa2, segs_a, "l1_norm_b")
    if comm is not None:
        comm.grads_ready(1, g)

    do0, dgt0 = _resid_bwd(dxa1, o0, gt0, segs_a, "l0_resid_b")
    dy4 = _mm_nt(do0, w['cv_w_out'], "l0_out_bx")
    g['cv_w_out'] = _mm_tn(y4_0, do0, "l0_out_bw")
    dy2, dgz0, g['cv_ln_g'], g['cv_ln_b'] = _conv2_bwd(dy4, y2_0, z0, w['cv_ln_g'], w['cv_ln_b'], tm, "l0_gate_b")
    da_l, db_l, ddw, ddb = carried('quad1', _conv1_bwd, dy2, z0, w['cv_dw'], jnp.zeros((CONV_WIDTH, e), F32),
                                   jnp.zeros((1, e), F32), nb, l, 0, "l0_conv_lat_b")
    da, db_, g['cv_dw'], g['cv_db'] = _conv1_bwd(dy2, z0, w['cv_dw'], ddw, ddb, nb, lc, tl, "l0_conv_ctx_b",
                                                 into=(da_l, db_l))
    dz0 = jnp.concatenate([da, db_, dgz0], axis=1)
    dh0 = _mm_nt(dz0, w['cv_w_in'], "l0_in_bx")
    g['cv_w_in'] = _mm_tn(h0, dz0, "l0_in_bw", shards=N_DEV)
    dxa0, dng0, dsc0, dsh0 = _rms_mod_bwd(xa0, norm_g[0:1], sc0, sh0, dh0, dxa1, segs_a, "l0_norm_b")
    if comm is not None:
        comm.grads_ready(0, g)

    def rows4(t):
        return jnp.pad(t[:, 0], ((0, 2 * nb - t.shape[0]), (0, 0)))

    dmods = jnp.stack([
        jnp.concatenate([rows4(dsh0), rows4(dsc0), rows4(dgt0)], axis=1),
        jnp.concatenate([rows4(dsh1), rows4(dsc1), rows4(dgt1)], axis=1),
        jnp.concatenate([rows4(dsh2), rows4(dsc2), rows4(dgt2)], axis=1),
        jnp.concatenate([rows4(dsh3), rows4(dsc3), rows4(dgt3)], axis=1)])
    dnorm_g = jnp.concatenate([dng0, dng1, dng2, dng3], axis=0)
    return loss_vec, dxa0[:tl].reshape(nb, l, d), g, dmods, dnorm_g


def _mesh_pos():
    return lax.axis_index("x"), lax.axis_index("y"), lax.axis_index("c")


def _remote(src, dst, send_sems, recv_sems, k, dev):
    return pltpu.make_async_remote_copy(src_ref=src, dst_ref=dst, send_sem=send_sems.at[k], recv_sem=recv_sems.at[k],
                                        device_id=dev, device_id_type=pl.DeviceIdType.MESH)


def _comm_call(body, xs, out_shapes, n_remote, n_local, name):
    hbm = pl.BlockSpec(memory_space=pltpu.HBM)
    return pl.pallas_call(
        body, in_specs=[hbm] * len(xs), out_specs=[hbm] * len(out_shapes), out_shape=out_shapes,
        scratch_shapes=_sem_shapes(n_remote, n_local),
        compiler_params=pltpu.CompilerParams(has_side_effects=True), name=name)(*xs)


def _run_side(side, name):
    n = len(side.xs)

    def body(*refs):
        side.start(refs[:n], refs[n:n + len(side.out_shapes)], *refs[n + len(side.out_shapes):])
        side.finish(refs[:n], refs[n:n + len(side.out_shapes)], *refs[n + len(side.out_shapes):])

    return _comm_call(body, side.xs, side.out_shapes, side.n_remote, side.n_local, name)


def _gather_side(xs):
    n = len(xs)

    def plan(x_refs, o_refs, send_sems, recv_sems, local_sems):
        x, y, c = _mesh_pos()
        me, sib = (x, y, c), (x, y, 1 - c)
        chips = [(1 - x, y), (x, 1 - y), (1 - x, 1 - y)]

        def slot(a, p):
            return o_refs[a].at[4 * p[0] + 2 * p[1] + p[2]]

        def copy(a, k, block, to, src=None):
            return _remote(slot(a, block) if src is None else src, slot(a, block), send_sems, recv_sems, 7 * a + k, to)

        mine = [pltpu.make_async_copy(x_refs[a], slot(a, me), local_sems.at[a]) for a in range(n)]
        first = []
        for a in range(n):
            first += [copy(a, 1 + j, me, chip + (c,), src=x_refs[a]) for j, chip in enumerate(chips)]
            first.append(copy(a, 0, me, sib, src=x_refs[a]))
        return me, sib, c, chips, copy, mine, first

    def start(x_refs, o_refs, send_sems, recv_sems, local_sems):
        _, _, _, _, _, mine, first = plan(x_refs, o_refs, send_sems, recv_sems, local_sems)
        for cp in mine + first:
            cp.start()

    def finish(x_refs, o_refs, send_sems, recv_sems, local_sems):
        me, sib, c, chips, copy, mine, first = plan(x_refs, o_refs, send_sems, recv_sems, local_sems)
        passed = []
        for j, chip in enumerate(chips):
            for a in range(n):
                copy(a, 1 + j, chip + (c,), me).wait_recv()
                passed.append(copy(a, 4 + j, chip + (c,), sib))
                passed[-1].start()
        for a in range(n):
            copy(a, 0, sib, me).wait_recv()
        for j, chip in enumerate(chips):
            for a in range(n):
                copy(a, 4 + j, chip + (1 - c,), me).wait_recv()
        for cp in first + passed:
            cp.wait_send()
        for cp in mine:
            cp.wait()

    return _Side(xs, [_sds((N_DEV,) + x.shape, x.dtype) for x in xs], 7 * n, n, start, finish)


def _gather_all(xs, name):
    return _run_side(_gather_side(xs), name)


def _swap_halves(xs, name):
    n = len(xs)

    def body(*refs):
        x_refs, o_refs, (send_sems, recv_sems, _) = refs[:n], refs[n:2 * n], refs[2 * n:]
        x, y, c = _mesh_pos()
        copies = [_remote(x_refs[a].at[q, 1 - c], o_refs[a].at[q], send_sems, recv_sems, 4 * a + q, (x, y, 1 - c))
                  for a in range(n) for q in range(4)]
        for cp in copies:
            cp.start()
        for cp in copies:
            cp.wait_recv()
        for cp in copies:
            cp.wait_send()

    return _comm_call(body, xs, [_sds((4,) + x.shape[2:], x.dtype) for x in xs], 4 * n, 0, name)


def _quad_side(xs):
    n = len(xs)

    def plan(x_refs, o_refs, send_sems, recv_sems, local_sems):
        x, y, c = _mesh_pos()
        q = 2 * x + y
        chips = [(1 - x, y), (x, 1 - y), (1 - x, 1 - y)]
        mine = [pltpu.make_async_copy(x_refs[a].at[q], o_refs[a].at[q], local_sems.at[a]) for a in range(n)]
        sends, arrivals = [], []
        for a in range(n):
            for j, chip in enumerate(chips):
                qj = 2 * chip[0] + chip[1]
                sends.append(_remote(x_refs[a].at[qj], o_refs[a].at[q], send_sems, recv_sems, 3 * a + j, chip + (c,)))
                arrivals.append(_remote(x_refs[a].at[qj], o_refs[a].at[qj], send_sems, recv_sems, 3 * a + j,
                                        chip + (c,)))
        return mine, sends, arrivals

    def start(*refs):
        mine, sends, _ = plan(*refs)
        for cp in mine + sends:
            cp.start()

    def finish(*refs):
        mine, sends, arrivals = plan(*refs)
        for cp in arrivals:
            cp.wait_recv()
        for cp in sends:
            cp.wait_send()
        for cp in mine:
            cp.wait()

    return _Side(xs, [_sds(x.shape, x.dtype) for x in xs], 3 * n, n, start, finish)


def _pair_add(xs, rs, name):
    n = len(xs)

    def body(*refs):
        c = lax.axis_index("c")
        for x_ref, r_ref, o_ref in zip(refs[:n], refs[n:2 * n], refs[2 * n:]):
            o_ref[...] = (x_ref[c].astype(F32) + r_ref[...].astype(F32)).astype(o_ref.dtype)

    slot = lambda x: pl.BlockSpec((None,) + x.shape[2:], lambda q: (q, 0, 0))
    return pl.pallas_call(
        body, grid=(4,),
        in_specs=[pl.BlockSpec((None, 2) + x.shape[2:], lambda q: (q, 0, 0, 0)) for x in xs] + [slot(x) for x in xs],
        out_specs=[slot(x) for x in xs], out_shape=[_sds((4,) + x.shape[2:], x.dtype) for x in xs],
        compiler_params=_params("parallel"), name=name)(*xs, *rs)


def _pack_rows(n):
    r = -(-n // PACK_COLS)
    return -(-r // 256) * 256 if r > 256 else -(-r // 16) * 16


def _pack(arrs, dtype):
    flat = jnp.concatenate([a.reshape(-1).astype(dtype) for a in arrs])
    rows = _pack_rows(flat.shape[0])
    return jnp.pad(flat, (0, rows * PACK_COLS - flat.shape[0])).reshape(rows, PACK_COLS)


def _pack_shards(arrs):
    flat = jnp.concatenate([a.astype(F32) for a in arrs], axis=1)
    rows = _pack_rows(flat.shape[1])
    return jnp.pad(flat, ((0, 0), (0, rows * PACK_COLS - flat.shape[1]))).reshape(N_DEV, rows, PACK_COLS)


def _unpack(packed, shapes, lead=()):
    flat = packed.reshape(tuple(lead) + (-1,))
    out, off = [], 0
    for s in shapes:
        n = 1
        for v in s:
            n *= v
        out.append(flat[..., off:off + n].reshape(tuple(lead) + tuple(s)))
        off += n
    return out


def _to_shards(full, ax):
    s = full.shape
    t = full.reshape(s[:ax] + (N_DEV, s[ax] // N_DEV) + s[ax + 1:])
    return jnp.moveaxis(t, ax, 0).reshape(N_DEV, -1)


def _from_shards(shards, local_shape, ax):
    t = jnp.moveaxis(shards.reshape((N_DEV,) + tuple(local_shape)), 0, ax)
    s = t.shape
    return t.reshape(s[:ax] + (s[ax] * s[ax + 1],) + s[ax + 2:])


def _mod_fwd(c_rows, w_mod, b_mod, name):
    nl, d, n = w_mod.shape
    r = c_rows.shape[0]

    def body(c_ref, w_ref, b_ref, o_ref):
        s = _silu(c_ref[...])
        for l in range(nl):
            o_ref[l] = _dot(s, w_ref[l]) + b_ref[l]

    return pl.pallas_call(body, out_shape=_sds((nl, r, n)),
                          compiler_params=pltpu.CompilerParams(vmem_limit_bytes=VMEM_LIMIT), name=name)(
        c_rows, w_mod, b_mod)


def _mod_bwd(c_rows, dcols, dall, w_mod, c_ctx, name):
    nl, d, n = w_mod.shape
    r = c_rows.shape[0]

    def body(c_ref, dc_ref, da_ref, w_ref, cc_ref, gw_ref, gb_ref, gc_ref):
        s = _silu(c_ref[...])
        ds = jnp.zeros((r, d), F32)
        for l in range(nl):
            gw_ref[l] = _dot_tn(s, dc_ref[l])
            gb_ref[l] = _rowsum(da_ref[l])
            ds = ds + _dot_nt(dc_ref[l], w_ref[l])
        row = lax.broadcasted_iota(jnp.int32, (r, d), 0)
        gc_ref[...] = _rowsum(jnp.where(row % 4 >= 2, ds, 0.0)) * _dsilu(cc_ref[...])

    return pl.pallas_call(body, out_shape=[_sds((nl, d, n)), _sds((nl, 1, 3 * d)), _sds((1, d))],
                          compiler_params=pltpu.CompilerParams(vmem_limit_bytes=VMEM_LIMIT), name=name)(
        c_rows, dcols, dall, w_mod, c_ctx)


def _adam_math(w, gsum, m, v):
    c1, c2 = 1.0 - ADAM_B1 ** ADAM_STEP, 1.0 - ADAM_B2 ** ADAM_STEP
    mn = ADAM_B1 * m + (1.0 - ADAM_B1) * gsum
    vn = ADAM_B2 * v + (1.0 - ADAM_B2) * (gsum * gsum)
    return -ADAM_LR * ((mn / c1) / (jnp.sqrt(vn / c2) + ADAM_EPS) + ADAM_WD * w), mn, vn


def _adam(w, gparts, row0, m, v, name):
    rows, cols = w.shape
    npart = gparts.shape[0]
    if rows % 8:
        tr = rows
        assert row0 == 0 and gparts.shape[1] == rows
    else:
        tr = max(t for t in (512, 256, 128, 64, 32, 16, 8) if rows % t == 0 and row0 % t == 0
                 and (t * cols <= 256 * 1024 or t == 8))

    def body(w_ref, g_ref, m_ref, v_ref, go_ref, d_ref, mo_ref, vo_ref):
        gsum = g_ref[0].astype(F32)
        for p in range(1, npart):
            gsum = gsum + g_ref[p].astype(F32)
        go_ref[...] = gsum
        d_ref[...], mo_ref[...], vo_ref[...] = _adam_math(w_ref[...], gsum, m_ref[...], v_ref[...])

    spec = _row(tr, cols)
    return pl.pallas_call(
        body, grid=(rows // tr,),
        in_specs=[spec, pl.BlockSpec((npart, tr, cols), lambda i: (0, row0 // tr + i, 0)), spec, spec],
        out_specs=[spec] * 4, out_shape=[_sds((rows, cols))] * 4, compiler_params=_params("parallel"),
        name=name)(w, gparts, m, v)


INPUTS = ['x', 'c', 'ctx'] + WEIGHTS + ['loss_target'] + ['m_' + n for n in WEIGHTS] + ['v_' + n for n in WEIGHTS]
AXES = ("x", "y", "c")
LAYER_MATS = (('cv_w_in', 'cv_w_out'), ('pl_w_in', 'pl_w_grp', 'pl_w_out'),
              ('ml_w_in', 'ml_w_uq', 'ml_w_ukv', 'ml_w_out'), ('ch_w_in', 'ch_w_out'))
KINDS = ('grad_', 'delta_', 'new_m_', 'new_v_')


def _squeeze_layer(name, a):
    return a if name == 'norm_g' or a.ndim < 3 else a[0]


def _as2d(a):
    return a.reshape(-1, a.shape[-1])


class _Exchanges:
    def __init__(self, a, w):
        self.a, self.w, self.sums, self.quad = a, w, {}, {}

    def mats(self, layer):
        return [_as2d(self.a[n]).astype(BF16) for n in LAYER_MATS[layer]]

    def take_weights(self, layer, bufs):
        for n, buf in zip(LAYER_MATS[layer], bufs):
            self.w[n] = _squeeze_layer(n, _from_shards(buf, self.a[n].shape, SHARD_AXIS[n]))

    def side(self, tag):
        layer = int(tag[-1])
        return _gather_side(self.mats(layer)) if tag.startswith('fwd') else _quad_side(self.sums[layer])

    def done(self, tag, brought):
        layer = int(tag[-1])
        if tag.startswith('fwd'):
            self.take_weights(layer, brought)
        else:
            self.quad[layer] = brought

    def shard_major(self, n, gn):
        if gn.ndim == 3 and gn.shape[0] == N_DEV and gn.dtype == BF16:
            return gn
        whole = tuple(N_DEV * s if i == SHARD_AXIS[n] else s for i, s in enumerate(self.a[n].shape))
        return _to_shards(gn.reshape(whole), SHARD_AXIS[n]).reshape((N_DEV,) + _as2d(self.a[n]).shape)

    def grads_ready(self, layer, g):
        bufs = [self.shard_major(n, g[n]).astype(BF16) for n in LAYER_MATS[layer]]
        if layer == 0:
            bufs.append(_pack_shards([self.shard_major(n, g[n]).reshape(N_DEV, -1) for n in VECTOR_WEIGHTS]))
        bufs = [b.reshape((4, 2) + b.shape[1:]) for b in bufs]
        got = _swap_halves(bufs, "grads_swap_cores_%d" % layer)
        self.sums[layer] = _pair_add(bufs, got, "grads_add_cores_%d" % layer)
        if layer == 0:
            self.quad[0] = _run_side(_quad_side(self.sums[0]), "grads_exchange_chips_0")


def _train_step(a):
    x, c, ctx, tgt = a['x'], a['c'], a['ctx'], a['loss_target']
    d = x.shape[-1]
    nb = x.shape[0]
    dev = 4 * lax.axis_index("x") + 2 * lax.axis_index("y") + lax.axis_index("c")
    local_shape = {n: a[n].shape for n in WEIGHTS}

    w = {n: _squeeze_layer(n, a[n]) for n in WEIGHTS if SHARD_AXIS[n] is None}
    comm = _Exchanges(a, w)
    vec_names = ['c'] + VECTOR_WEIGHTS
    gathered = _gather_all(comm.mats(0) + [_pack([a[n] for n in vec_names], F32)], "gather_first")
    comm.take_weights(0, gathered[:-1])
    parts = dict(zip(vec_names, _unpack(gathered[-1], [a[n].shape for n in vec_names], lead=(N_DEV,))))
    for n in VECTOR_WEIGHTS:
        w[n] = _squeeze_layer(n, _from_shards(parts[n], local_shape[n], SHARD_AXIS[n]))
    c_all = parts['c'].reshape(N_DEV * nb, d)
    c_ctx = a['c_ctx'].reshape(1, d)

    w_mod = a['w_mod']
    nl, ncol = w_mod.shape[0], w_mod.shape[2]
    mod_rows = -(-(N_DEV * nb + 1) // 8) * 8
    c_rows = jnp.concatenate([c_all, c_ctx, jnp.zeros((mod_rows - N_DEV * nb - 1, d), F32)], axis=0)
    b_loc = lax.dynamic_slice(a['b_mod'], (0, dev * ncol), (nl, ncol))[:, None, :]
    mod_loc = _mod_fwd(c_rows, w_mod, b_loc, "mod_fwd")
    mod_all, = _gather_all([mod_loc.reshape(nl * mod_rows, ncol)], "gather_mods")
    mod_all = mod_all.reshape(N_DEV, nl, mod_rows, ncol).transpose(1, 2, 0, 3).reshape(nl, mod_rows, N_DEV * ncol)
    ctx_row = mod_all[:, N_DEV * nb:N_DEV * nb + 1]
    mods = jnp.concatenate([lax.dynamic_slice(mod_all, (0, dev * nb, 0), (nl, nb, 3 * d))] + [ctx_row] * nb, axis=1)

    loss_vec, grad_x, g, dmods, dnorm_g = _local_step(x, ctx, tgt, w, mods, comm)
    loss = lax.psum(jnp.sum(loss_vec), AXES)

    nseg = dmods.shape[1]
    dm_all, = _gather_all([dmods.reshape(nl * nseg, 3 * d)], "gather_dmods")
    dm_all = dm_all.reshape(N_DEV, nl, nseg, 3 * d).transpose(1, 0, 2, 3).reshape(nl, N_DEV * nseg, 3 * d)
    dcols = lax.dynamic_slice(dm_all, (0, 0, dev * ncol), (nl, N_DEV * nseg, ncol))
    c_rows_b = jnp.concatenate([c_all.reshape(N_DEV, nb, d), jnp.broadcast_to(c_ctx, (N_DEV, nb, d))], axis=1)
    g_w_mod, g_b_mod, g_c_ctx = _mod_bwd(c_rows_b.reshape(N_DEV * nseg, d), dcols, dm_all, w_mod, c_ctx, "mod_bwd")
    g['c_ctx'], g['norm_g'] = g_c_ctx, dnorm_g
    rep_all, = _gather_all([_pack([g[n] for n in REPLICATED], F32)], "gather_replicated_grads")

    out = {}

    def keep(names, res, shapes=None):
        for kind, val in zip(KINDS, res):
            if shapes is None:
                out[kind + names[0]] = val.reshape(local_shape[names[0]])
            else:
                for n, leaf in zip(names, _unpack(val, shapes)):
                    out[kind + n] = leaf

    def update_packed(names, gparts, tag):
        res = _adam(_pack([a[n] for n in names], F32), gparts, 0, _pack([a['m_' + n] for n in names], F32),
                    _pack([a['v_' + n] for n in names], F32), "adam_" + tag)
        keep(names, res, [local_shape[n] for n in names])

    for layer, names in enumerate(LAYER_MATS):
        for n, gparts in zip(names, comm.quad[layer]):
            keep([n], _adam(_as2d(a[n]), gparts, 0, _as2d(a['m_' + n]), _as2d(a['v_' + n]), "adam_" + n))
    update_packed(VECTOR_WEIGHTS, comm.quad[0][-1], "vectors")
    update_packed(REPLICATED, rep_all, "replicated")
    keep(['w_mod'], _adam(_as2d(w_mod), _as2d(g_w_mod)[None], 0, _as2d(a['m_w_mod']), _as2d(a['v_w_mod']),
                          "adam_w_mod"))
    keep(['b_mod'], _adam(a['b_mod'], g_b_mod.reshape((1,) + a['b_mod'].shape), 0, a['m_b_mod'], a['v_b_mod'],
                          "adam_b_mod"))
    return (loss, grad_x) + tuple(out[kind + n] for kind in KINDS for n in WEIGHTS)


def kernel(x, c, ctx, c_ctx, norm_g, w_mod, b_mod, cv_w_in, cv_dw, cv_db, cv_ln_g, cv_ln_b, cv_w_out, pl_w_in, pl_w_grp, pl_scale, pl_w_out, ml_w_in, ml_q_norm, ml_kv_norm, ml_w_uq, ml_w_ukv, ml_nope_norm, ml_rope_norm, ml_w_out, ch_w_in, ch_ln_g, ch_ln_b, ch_w_s, ch_b_s, ch_w_out, loss_target, m_c_ctx, m_norm_g, m_w_mod, m_b_mod, m_cv_w_in, m_cv_dw, m_cv_db, m_cv_ln_g, m_cv_ln_b, m_cv_w_out, m_pl_w_in, m_pl_w_grp, m_pl_scale, m_pl_w_out, m_ml_w_in, m_ml_q_norm, m_ml_kv_norm, m_ml_w_uq, m_ml_w_ukv, m_ml_nope_norm, m_ml_rope_norm, m_ml_w_out, m_ch_w_in, m_ch_ln_g, m_ch_ln_b, m_ch_w_s, m_ch_b_s, m_ch_w_out, v_c_ctx, v_norm_g, v_w_mod, v_b_mod, v_cv_w_in, v_cv_dw, v_cv_db, v_cv_ln_g, v_cv_ln_b, v_cv_w_out, v_pl_w_in, v_pl_w_grp, v_pl_scale, v_pl_w_out, v_ml_w_in, v_ml_q_norm, v_ml_kv_norm, v_ml_w_uq, v_ml_w_ukv, v_ml_nope_norm, v_ml_rope_norm, v_ml_w_out, v_ch_w_in, v_ch_ln_g, v_ch_ln_b, v_ch_w_s, v_ch_b_s, v_ch_w_out):
    return _train_step(dict(zip(INPUTS, (x, c, ctx, c_ctx, norm_g, w_mod, b_mod, cv_w_in, cv_dw, cv_db, cv_ln_g, cv_ln_b, cv_w_out, pl_w_in, pl_w_grp, pl_scale, pl_w_out, ml_w_in, ml_q_norm, ml_kv_norm, ml_w_uq, ml_w_ukv, ml_nope_norm, ml_rope_norm, ml_w_out, ch_w_in, ch_ln_g, ch_ln_b, ch_w_s, ch_b_s, ch_w_out, loss_target, m_c_ctx, m_norm_g, m_w_mod, m_b_mod, m_cv_w_in, m_cv_dw, m_cv_db, m_cv_ln_g, m_cv_ln_b, m_cv_w_out, m_pl_w_in, m_pl_w_grp, m_pl_scale, m_pl_w_out, m_ml_w_in, m_ml_q_norm, m_ml_kv_norm, m_ml_w_uq, m_ml_w_ukv, m_ml_nope_norm, m_ml_rope_norm, m_ml_w_out, m_ch_w_in, m_ch_ln_g, m_ch_ln_b, m_ch_w_s, m_ch_b_s, m_ch_w_out, v_c_ctx, v_norm_g, v_w_mod, v_b_mod, v_cv_w_in, v_cv_dw, v_cv_db, v_cv_ln_g, v_cv_ln_b, v_cv_w_out, v_pl_w_in, v_pl_w_grp, v_pl_scale, v_pl_w_out, v_ml_w_in, v_ml_q_norm, v_ml_kv_norm, v_ml_w_uq, v_ml_w_ukv, v_ml_nope_norm, v_ml_rope_norm, v_ml_w_out, v_ch_w_in, v_ch_ln_g, v_ch_ln_b, v_ch_w_s, v_ch_b_s, v_ch_w_out))))
```

```python
import functools

import jax
import jax.numpy as jnp
from jax import lax
from jax.experimental import pallas as pl
from jax.experimental.pallas import tpu as pltpu

F32 = jnp.float32
BF16 = jnp.bfloat16

N_DEV = 8
EPS = 1e-6
CONV_WIDTH = 31
CONV_PAD = 16
POOL_WINDOWS = (2, 4, 8, 16)
POOL_TAPS = 16
MLA_HEADS = 8
MLA_NOPE = 128
MLA_ROPE = 64
MLA_Q_RANK = 384
MLA_KV_RANK = 256
MLA_SCALE = (MLA_NOPE + MLA_ROPE) ** -0.5
ROPE_THETA = 10000.0
GRID_W = 64
HEAD_W = 256
ATTN_CHAINS = 2
CHUNK = 128
CHUNK_GROUPS = 8
ADAM_LR = 0.001
ADAM_B1 = 0.9
ADAM_B2 = 0.999
ADAM_EPS = 1e-08
ADAM_WD = 0.01
ADAM_STEP = 10
LANES = 128
VMEM_LIMIT = 56 * 1024 * 1024
PACK_COLS = 1024

WEIGHTS = ['c_ctx', 'norm_g', 'w_mod', 'b_mod', 'cv_w_in', 'cv_dw', 'cv_db', 'cv_ln_g', 'cv_ln_b', 'cv_w_out',
           'pl_w_in', 'pl_w_grp', 'pl_scale', 'pl_w_out', 'ml_w_in', 'ml_q_norm', 'ml_kv_norm', 'ml_w_uq',
           'ml_w_ukv', 'ml_nope_norm', 'ml_rope_norm', 'ml_w_out', 'ch_w_in', 'ch_ln_g', 'ch_ln_b', 'ch_w_s',
           'ch_b_s', 'ch_w_out']
SHARD_AXIS = {'c_ctx': None, 'norm_g': None, 'w_mod': 2, 'b_mod': None, 'cv_w_in': 2, 'cv_dw': 2, 'cv_db': None,
              'cv_ln_g': None, 'cv_ln_b': None, 'cv_w_out': 1, 'pl_w_in': 2, 'pl_w_grp': 2, 'pl_scale': 1,
              'pl_w_out': 1, 'ml_w_in': 2, 'ml_q_norm': 1, 'ml_kv_norm': 1, 'ml_w_uq': 2, 'ml_w_ukv': 2,
              'ml_nope_norm': None, 'ml_rope_norm': None, 'ml_w_out': 1, 'ch_w_in': 2, 'ch_ln_g': 1, 'ch_ln_b': 1,
              'ch_w_s': None, 'ch_b_s': None, 'ch_w_out': 1}
MATMUL_WEIGHTS = ['cv_w_in', 'cv_w_out', 'pl_w_in', 'pl_w_grp', 'pl_w_out', 'ml_w_in', 'ml_w_uq', 'ml_w_ukv',
                  'ml_w_out', 'ch_w_in', 'ch_w_out']
VECTOR_WEIGHTS = ['cv_dw', 'pl_scale', 'ml_q_norm', 'ml_kv_norm', 'ch_ln_g', 'ch_ln_b']
EXCHANGED = MATMUL_WEIGHTS[:1] + ['cv_dw'] + MATMUL_WEIGHTS[1:4] + ['pl_scale'] + MATMUL_WEIGHTS[4:6] + [
    'ml_q_norm', 'ml_kv_norm'] + MATMUL_WEIGHTS[6:10] + ['ch_ln_g', 'ch_ln_b', 'ch_w_out']
REPLICATED = ['c_ctx', 'norm_g', 'cv_db', 'cv_ln_g', 'cv_ln_b', 'ml_nope_norm', 'ml_rope_norm', 'ch_w_s', 'ch_b_s']


def _pick(n, cands):
    for c in cands:
        if n % c == 0:
            return c
    raise ValueError(f"no tile for {n} among {cands}")


def _params(*sem):
    return pltpu.CompilerParams(dimension_semantics=sem, vmem_limit_bytes=VMEM_LIMIT)


def _sig(x):
    return 1.0 / (1.0 + jnp.exp(-x))


def _silu(x):
    return x * _sig(x)


def _dsilu(x):
    s = _sig(x)
    return s * (1.0 + x * (1.0 - s))


def _rowsum(v):
    return jnp.sum(v, axis=0, keepdims=True)


def _dot(a, b):
    return jnp.dot(a.astype(BF16), b.astype(BF16), preferred_element_type=F32)


def _dot_nt(a, b):
    return lax.dot_general(a.astype(BF16), b.astype(BF16), (((1,), (1,)), ((), ())), preferred_element_type=F32)


def _dot_tn(a, b):
    return lax.dot_general(a.astype(BF16), b.astype(BF16), (((0,), (0,)), ((), ())), preferred_element_type=F32)


class _Segs:
    def __init__(self, lens, tm):
        self.lens, self.tm, self.n = tuple(lens), tm, len(lens)
        self.starts, s = [], 0
        for l in lens:
            assert l % tm == 0
            self.starts.append(s // tm)
            s += l
        self.rows, self.tiles = s, s // tm

    def seg(self, i):
        r = 0
        for st in self.starts[1:]:
            r = r + jnp.where(i >= st, 1, 0)
        return r

    def is_first(self, i):
        f = i == 0
        for st in self.starts[1:]:
            f = jnp.logical_or(f, i == st)
        return f

    def spec(self, cols):
        return pl.BlockSpec((None, 1, cols), lambda i: (self.seg(i), 0, 0))


def _row(tm, cols, cb=0):
    return pl.BlockSpec((tm, cols), lambda i: (i, cb))


def _const(shape):
    return pl.BlockSpec(shape, lambda *_: (0,) * len(shape))


def _sds(shape, dtype=F32):
    return jax.ShapeDtypeStruct(shape, dtype)


class _Side:
    def __init__(self, xs, out_shapes, n_remote, n_local, start, finish):
        self.xs, self.out_shapes, self.n_remote, self.n_local = list(xs), list(out_shapes), n_remote, n_local
        self.start, self.finish = start, finish


def _sem_shapes(n_remote, n_local):
    return [pltpu.SemaphoreType.DMA((n_remote,)), pltpu.SemaphoreType.DMA((n_remote,)),
            pltpu.SemaphoreType.DMA((max(n_local, 1),))]


def _pallas(body, args, *, grid, in_specs, out_specs, out_shape, sem, name, scratch_shapes=(), side=None, into=None):
    aliases = {}
    if into:
        inner, n_args = body, len(args)

        def body(*refs):
            inner(*refs[:n_args], *refs[n_args + len(into):])

        aliases = {n_args + k: o for k, o in enumerate(sorted(into))}
        args = tuple(args) + tuple(into[o] for o in sorted(into))
        in_specs = list(in_specs) + [pl.BlockSpec(memory_space=pl.ANY)] * len(into)
    if side is None:
        return pl.pallas_call(body, grid=grid, in_specs=in_specs, out_specs=out_specs, out_shape=out_shape,
                              scratch_shapes=list(scratch_shapes), input_output_aliases=aliases,
                              compiler_params=_params(*sem), name=name)(*args)
    multi = isinstance(out_shape, (list, tuple))
    out_specs, out_shape = (list(out_specs), list(out_shape)) if multi else ([out_specs], [out_shape])
    ni, no, ns, si, so = len(in_specs), len(out_specs), len(scratch_shapes), len(side.xs), len(side.out_shapes)
    hbm = pl.BlockSpec(memory_space=pltpu.HBM)

    def wrapped(*refs):
        ins, sins, refs = refs[:ni], refs[ni:ni + si], refs[ni + si:]
        outs, souts, refs = refs[:no], refs[no:no + so], refs[no + so:]
        scr, sems = refs[:ns], refs[ns:]
        ids = [pl.program_id(k) for k in range(len(grid))]
        first = functools.reduce(jnp.logical_and, [i == 0 for i in ids])
        last = functools.reduce(jnp.logical_and, [i == n - 1 for i, n in zip(ids, grid)])

        @pl.when(first)
        def _():
            side.start(sins, souts, *sems)

        body(*ins, *outs, *scr)

        @pl.when(last)
        def _():
            side.finish(sins, souts, *sems)

    res = pl.pallas_call(
        wrapped, grid=grid, in_specs=list(in_specs) + [hbm] * si, out_specs=out_specs + [hbm] * so,
        out_shape=out_shape + side.out_shapes,
        scratch_shapes=list(scratch_shapes) + _sem_shapes(side.n_remote, side.n_local), input_output_aliases=aliases,
        compiler_params=pltpu.CompilerParams(dimension_semantics=("arbitrary",) * len(grid),
                                             vmem_limit_bytes=VMEM_LIMIT, has_side_effects=True),
        name=name)(*args, *side.xs)
    return (list(res[:no]) if multi else res[0]), list(res[no:])


N_TILES = (1024, 896, 768, 512, 384, 256, 128)
M_TILES = (1536, 1024, 768, 512, 256, 128)


def _mm(a, b, name, out_dtype=F32, rows=None, side=None, resid=None):
    m, k, n = rows or a.shape[0], a.shape[1], b.shape[1]
    tm, tn = _pick(m, M_TILES), _pick(n, N_TILES)
    if resid is not None:
        x, gt, segs = resid
        pieces = tm // segs.tm

        def body(a_ref, b_ref, x_ref, *rest):
            gt_refs, (o_ref, y_ref) = rest[:pieces], rest[pieces:]
            o = _dot(a_ref[...], b_ref[...])
            o_ref[...] = o
            for c in range(pieces):
                rows = slice(c * segs.tm, (c + 1) * segs.tm)
                y_ref[rows, :] = x_ref[rows, :] + gt_refs[c][...] * o[rows, :]

        tile = pl.BlockSpec((tm, tn), lambda j, i: (i, j))
        gt_specs = [pl.BlockSpec((None, 1, tn), lambda j, i, c=c: (segs.seg(i * pieces + c), 0, j))
                    for c in range(pieces)]
        return pl.pallas_call(
            body, grid=(n // tn, m // tm),
            in_specs=[pl.BlockSpec((tm, k), lambda j, i: (i, 0)), pl.BlockSpec((k, tn), lambda j, i: (0, j)), tile]
            + gt_specs, out_specs=[tile, tile], out_shape=[_sds((m, n)), _sds((m, n))],
            compiler_params=_params("parallel", "parallel"), name=name)(a, b, x, *([gt] * pieces))

    def body(a_ref, b_ref, o_ref):
        o_ref[...] = _dot(a_ref[...], b_ref[...]).astype(o_ref.dtype)

    return _pallas(
        body, (a, b), grid=(n // tn, m // tm),
        in_specs=[pl.BlockSpec((tm, k), lambda j, i: (i, 0)), pl.BlockSpec((k, tn), lambda j, i: (0, j))],
        out_specs=pl.BlockSpec((tm, tn), lambda j, i: (i, j)), out_shape=_sds((m, n), out_dtype),
        sem=("parallel", "parallel"), name=name, side=side)


def _mm_nt(a, b, name, out_dtype=F32):
    m, k, n = a.shape[0], a.shape[1], b.shape[0]
    tm, tn = _pick(m, (512, 256, 128)), _pick(n, N_TILES)

    def body(a_ref, b_ref, o_ref):
        o_ref[...] = _dot_nt(a_ref[...], b_ref[...]).astype(o_ref.dtype)

    return pl.pallas_call(
        body, grid=(n // tn, m // tm),
        in_specs=[pl.BlockSpec((tm, k), lambda j, i: (i, 0)), pl.BlockSpec((tn, k), lambda j, i: (j, 0))],
        out_specs=pl.BlockSpec((tm, tn), lambda j, i: (i, j)), out_shape=_sds((m, n), out_dtype),
        compiler_params=_params("parallel", "parallel"), name=name)(a, b)


def _mm_nt_sum(pairs, name):
    m, n = pairs[0][0].shape[0], pairs[0][1].shape[0]
    tm, tn = _pick(m, (512, 256, 128)), _pick(n, N_TILES)
    tiles = [a.shape[0] // tm for a, _ in pairs]
    assert all(a.shape[0] % tm == 0 for a, _ in pairs)

    def body(*refs):
        o_ref, i = refs[-1], pl.program_id(1)
        acc = _dot_nt(refs[0][...], refs[1][...])
        for p in range(1, len(pairs)):
            acc = acc + jnp.where(i < tiles[p], _dot_nt(refs[2 * p][...], refs[2 * p + 1][...]), 0.0)
        o_ref[...] = acc

    in_specs = []
    for (a, b), nt in zip(pairs, tiles):
        in_specs += [pl.BlockSpec((tm, a.shape[1]), lambda j, i, nt=nt: (jnp.minimum(i, nt - 1), 0)),
                     pl.BlockSpec((tn, a.shape[1]), lambda j, i: (j, 0))]
    return pl.pallas_call(
        body, grid=(n // tn, m // tm), in_specs=in_specs, out_specs=pl.BlockSpec((tm, tn), lambda j, i: (i, j)),
        out_shape=_sds((m, n)), compiler_params=_params("parallel", "parallel"),
        name=name)(*[t for pair in pairs for t in pair])


def _mm_tn(a, b, name, rows=None, shards=None, side=None):
    t, k, n = rows or a.shape[0], a.shape[1], b.shape[1]
    tk, tt = _pick(k, N_TILES), _pick(t, M_TILES)
    if shards:
        width = n // shards
        per_tile = max(c for c in (8, 4, 2, 1) if shards % c == 0 and c * width <= N_TILES[0])
        tn = per_tile * width
    else:
        tn = _pick(n, N_TILES)
    assert tn % LANES == 0

    def body(a_ref, b_ref, o_ref, acc_ref):
        @pl.when(pl.program_id(2) == 0)
        def _():
            acc_ref[...] = jnp.zeros_like(acc_ref)

        acc_ref[...] += _dot_tn(a_ref[...], b_ref[...])

        @pl.when(pl.program_id(2) == pl.num_programs(2) - 1)
        def _():
            if shards:
                for c in range(per_tile):
                    o_ref[c] = acc_ref[:, c * width:(c + 1) * width].astype(o_ref.dtype)
            else:
                o_ref[...] = acc_ref[...]

    if shards:
        out_spec = pl.BlockSpec((per_tile, tk, width), lambda i, j, s: (j, i, 0))
        out_shape = _sds((shards, k, width), BF16)
    else:
        out_spec, out_shape = pl.BlockSpec((tk, tn), lambda i, j, s: (i, j)), _sds((k, n))
    return _pallas(
        body, (a, b), grid=(k // tk, n // tn, t // tt),
        in_specs=[pl.BlockSpec((tt, tk), lambda i, j, s: (s, i)), pl.BlockSpec((tt, tn), lambda i, j, s: (s, j))],
        out_specs=out_spec, out_shape=out_shape, scratch_shapes=[pltpu.VMEM((tk, tn), F32)],
        sem=("parallel", "parallel", "arbitrary"), name=name, side=side)


def _rms_mod_fwd(x, g, sc, sh, segs, name):
    d, tm = x.shape[1], segs.tm

    def body(x_ref, g_ref, sc_ref, sh_ref, h_ref):
        xf = x_ref[...]
        r = lax.rsqrt(jnp.mean(xf * xf, axis=-1, keepdims=True) + EPS)
        h_ref[...] = ((xf * r * g_ref[...]) * (1.0 + sc_ref[...]) + sh_ref[...]).astype(h_ref.dtype)

    return pl.pallas_call(
        body, grid=(segs.tiles,), in_specs=[_row(tm, d), _const((1, d)), segs.spec(d), segs.spec(d)],
        out_specs=_row(tm, d), out_shape=_sds((segs.rows, d), BF16), compiler_params=_params("parallel"),
        name=name)(x, g, sc, sh)


def _rms_mod_bwd(x, g, sc, sh, dh, dxr, segs, name):
    d, tm = x.shape[1], segs.tm
    dxr_tiles = dxr.shape[0] // tm

    def body(x_ref, g_ref, sc_ref, sh_ref, dh_ref, dxr_ref, dx_ref, dg_ref, dsc_ref, dsh_ref):
        i = pl.program_id(0)

        @pl.when(i == 0)
        def _():
            dg_ref[...] = jnp.zeros_like(dg_ref)

        @pl.when(segs.is_first(i))
        def _():
            dsc_ref[...] = jnp.zeros_like(dsc_ref)
            dsh_ref[...] = jnp.zeros_like(dsh_ref)

        xf, gg, dhf = x_ref[...], g_ref[...], dh_ref[...].astype(F32)
        dxr = jnp.where(i < dxr_tiles, dxr_ref[...], 0.0)
        r = lax.rsqrt(jnp.mean(xf * xf, axis=-1, keepdims=True) + EPS)
        xh = xf * r
        dsh_ref[...] += _rowsum(dhf)
        dsc_ref[...] += _rowsum(dhf * (xh * gg))
        du = dhf * (1.0 + sc_ref[...])
        dg_ref[...] += _rowsum(du * xh)
        dxh = du * gg
        dx_ref[...] = dxr + r * (dxh - xh * jnp.mean(dxh * xh, axis=-1, keepdims=True))

    return pl.pallas_call(
        body, grid=(segs.tiles,),
        in_specs=[_row(tm, d), _const((1, d)), segs.spec(d), segs.spec(d), _row(tm, d),
                  pl.BlockSpec((tm, d), lambda i: (jnp.minimum(i, dxr_tiles - 1), 0))],
        out_specs=[_row(tm, d), _const((1, d)), segs.spec(d), segs.spec(d)],
        out_shape=[_sds((segs.rows, d)), _sds((1, d)), _sds((segs.n, 1, d)), _sds((segs.n, 1, d))],
        compiler_params=_params("arbitrary"), name=name)(x, g, sc, sh, dh, dxr)


def _resid_fwd(x, o, gt, segs, name):
    d, tm = x.shape[1], segs.tm

    def body(x_ref, o_ref, gt_ref, y_ref):
        y_ref[...] = x_ref[...] + gt_ref[...] * o_ref[...]

    return pl.pallas_call(
        body, grid=(segs.tiles,), in_specs=[_row(tm, d), _row(tm, d), segs.spec(d)], out_specs=_row(tm, d),
        out_shape=_sds((segs.rows, d)), compiler_params=_params("parallel"), name=name)(x, o, gt)


def _resid_bwd(dxn, o, gt, segs, name):
    d, tm = o.shape[1], segs.tm

    def body(dxn_ref, o_ref, gt_ref, do_ref, dgt_ref):
        @pl.when(segs.is_first(pl.program_id(0)))
        def _():
            dgt_ref[...] = jnp.zeros_like(dgt_ref)

        dx = dxn_ref[...]
        do_ref[...] = (gt_ref[...] * dx).astype(do_ref.dtype)
        dgt_ref[...] += _rowsum(dx * o_ref[...])

    return pl.pallas_call(
        body, grid=(segs.tiles,), in_specs=[_row(tm, d), _row(tm, d), segs.spec(d)],
        out_specs=[_row(tm, d), segs.spec(d)], out_shape=[_sds((segs.rows, d), BF16), _sds((segs.n, 1, d))],
        compiler_params=_params("arbitrary"), name=name)(dxn, o, gt)


def _loss_head(y, tgt, tm, name):
    t, d = y.shape

    def body(y_ref, t_ref, l_ref, dy_ref):
        @pl.when(pl.program_id(0) == 0)
        def _():
            l_ref[...] = jnp.zeros_like(l_ref)

        e = y_ref[...] - t_ref[...]
        dy_ref[...] = e * (1.0 / d)
        l_ref[...] += _rowsum(e * e) * (0.5 / d)

    return pl.pallas_call(
        body, grid=(t // tm,), in_specs=[_row(tm, d), _row(tm, d)], out_specs=[_const((1, d)), _row(tm, d)],
        out_shape=[_sds((1, d)), _sds((t, d))], compiler_params=_params("arbitrary"), name=name)(y, tgt)


def _seq_spec(l, ce, row0, cb0=0):
    return pl.BlockSpec((l, ce), lambda j, s: (row0 // l + s, cb0 + j))


def _tap_sum(pad_ref, taps_ref, first_row, n_taps, l, ce, flip):
    out = []
    for r0 in range(0, l, CHUNK):
        rows = min(CHUNK, l - r0)
        acc = jnp.zeros((rows, ce), F32)
        for k in range(n_taps):
            kk = n_taps - 1 - k if flip else k
            acc = acc + pad_ref[pl.ds(first_row + r0 + k, rows), :] * taps_ref[kk:kk + 1, :]
        out.append(acc)
    return out


def _fill_pad(pad_ref, val, l, ce):
    pad_ref[pl.ds(0, CONV_PAD), :] = jnp.zeros((CONV_PAD, ce), F32)
    pad_ref[pl.ds(CONV_PAD + l, CONV_PAD), :] = jnp.zeros((CONV_PAD, ce), F32)
    pad_ref[pl.ds(CONV_PAD, l), :] = val


def _conv1_fwd(z, dw, db, nseq, l, row0, name, side=None, into=None):
    e = z.shape[1] // 3
    ce = LANES
    half = CONV_WIDTH // 2

    def body(a_ref, b_ref, dw_ref, db_ref, y_ref, pad_ref):
        _fill_pad(pad_ref, a_ref[...] * _sig(b_ref[...]), l, ce)
        pieces = _tap_sum(pad_ref, dw_ref, CONV_PAD - half, CONV_WIDTH, l, ce, False)
        for n, acc in enumerate(pieces):
            y_ref[pl.ds(n * CHUNK, acc.shape[0]), :] = acc + db_ref[...]

    return _pallas(
        body, (z, z, dw, db), grid=(e // ce, nseq),
        in_specs=[_seq_spec(l, ce, row0), _seq_spec(l, ce, row0, e // ce),
                  pl.BlockSpec((CONV_WIDTH, ce), lambda j, s: (0, j)), pl.BlockSpec((1, ce), lambda j, s: (0, j))],
        out_specs=_seq_spec(l, ce, row0), out_shape=_sds((z.shape[0], e)),
        scratch_shapes=[pltpu.VMEM((l + 2 * CONV_PAD, ce), F32)],
        sem=("parallel", "arbitrary"), name=name, side=side, into=None if into is None else {0: into})


def _conv1_bwd(dy2, z, dw, acc_dw, acc_db, nseq, l, row0, name, side=None, into=None):
    e = z.shape[1] // 3
    ce = LANES
    half = CONV_WIDTH // 2

    def body(dy_ref, a_ref, b_ref, dw_ref, adw_ref, adb_ref, da_ref, dbb_ref, ddw_ref, ddb_ref, ypad_ref, dpad_ref):
        @pl.when(pl.program_id(1) == 0)
        def _():
            ddw_ref[...] = adw_ref[...]
            ddb_ref[...] = adb_ref[...]

        a, sb = a_ref[...], _sig(b_ref[...])
        dy = dy_ref[...]
        _fill_pad(ypad_ref, a * sb, l, ce)
        _fill_pad(dpad_ref, dy, l, ce)
        ddb_ref[...] += _rowsum(dy)
        for k in range(CONV_WIDTH):
            ddw_ref[k:k + 1, :] += _rowsum(dy * ypad_ref[pl.ds(CONV_PAD - half + k, l), :])
        pieces = _tap_sum(dpad_ref, dw_ref, CONV_PAD - half, CONV_WIDTH, l, ce, True)
        for n, dy1 in enumerate(pieces):
            rows = pl.ds(n * CHUNK, dy1.shape[0])
            sbn = sb[n * CHUNK:n * CHUNK + dy1.shape[0], :]
            da_ref[rows, :] = (dy1 * sbn).astype(da_ref.dtype)
            dbb_ref[rows, :] = (dy1 * a[n * CHUNK:n * CHUNK + dy1.shape[0], :] * sbn * (1.0 - sbn)).astype(dbb_ref.dtype)

    cw = lambda j, s: (0, j)
    return _pallas(
        body, (dy2, z, z, dw, acc_dw, acc_db), grid=(e // ce, nseq),
        in_specs=[_seq_spec(l, ce, row0), _seq_spec(l, ce, row0), _seq_spec(l, ce, row0, e // ce),
                  pl.BlockSpec((CONV_WIDTH, ce), cw), pl.BlockSpec((CONV_WIDTH, ce), cw), pl.BlockSpec((1, ce), cw)],
        out_specs=[_seq_spec(l, ce, row0), _seq_spec(l, ce, row0),
                   pl.BlockSpec((CONV_WIDTH, ce), cw), pl.BlockSpec((1, ce), cw)],
        out_shape=[_sds((z.shape[0], e), BF16), _sds((z.shape[0], e), BF16), _sds((CONV_WIDTH, e)), _sds((1, e))],
        scratch_shapes=[pltpu.VMEM((l + 2 * CONV_PAD, ce), F32), pltpu.VMEM((l + 2 * CONV_PAD, ce), F32)],
        sem=("parallel", "arbitrary"), name=name, side=side,
        into=None if into is None else {0: into[0], 1: into[1]})


def _pool_tables(l, e):
    grp = e // len(POOL_WINDOWS)
    w = jnp.repeat(jnp.array(POOL_WINDOWS, jnp.int32), grp)[None, :]
    off = jnp.arange(POOL_TAPS, dtype=jnp.int32)[:, None] - POOL_TAPS // 2
    taps = jnp.logical_and(off >= -(w // 2), off < w - w // 2).astype(F32)
    t = jnp.arange(l, dtype=jnp.int32)[:, None]
    cnt = jnp.clip(t + (w - w // 2), 0, l) - jnp.clip(t - w // 2, 0, l)
    return taps, 1.0 / cnt.astype(F32)


def _pool1(v_src, taps, inv_cnt, nseq, l, row0, transpose, name, out_dtype, into=None):
    e = taps.shape[1]
    ce = LANES
    half = POOL_TAPS // 2

    def body(v_ref, taps_ref, ic_ref, o_ref, pad_ref):
        v = v_ref[...].astype(F32)
        if transpose:
            _fill_pad(pad_ref, v * ic_ref[...], l, ce)
            pieces = _tap_sum(pad_ref, taps_ref, CONV_PAD - half + 1, POOL_TAPS, l, ce, True)
        else:
            _fill_pad(pad_ref, v, l, ce)
            pieces = _tap_sum(pad_ref, taps_ref, CONV_PAD - half, POOL_TAPS, l, ce, False)
        for n, acc in enumerate(pieces):
            rows = pl.ds(n * CHUNK, acc.shape[0])
            vn = v[n * CHUNK:n * CHUNK + acc.shape[0], :]
            if transpose:
                o_ref[rows, :] = (acc - vn).astype(o_ref.dtype)
            else:
                o_ref[rows, :] = (acc * ic_ref[rows, :] - vn).astype(o_ref.dtype)

    return _pallas(
        body, (v_src, taps, inv_cnt), grid=(e // ce, nseq),
        in_specs=[_seq_spec(l, ce, row0), pl.BlockSpec((POOL_TAPS, ce), lambda j, s: (0, j)),
                  pl.BlockSpec((l, ce), lambda j, s: (0, j))],
        out_specs=_seq_spec(l, ce, row0), out_shape=_sds((v_src.shape[0], e), out_dtype),
        scratch_shapes=[pltpu.VMEM((l + 2 * CONV_PAD, ce), F32)],
        sem=("parallel", "arbitrary"), name=name, into=None if into is None else {0: into})


def _layernorm_parts(x, eps=EPS):
    mu = jnp.mean(x, axis=-1, keepdims=True)
    xc = x - mu
    r = lax.rsqrt(jnp.mean(xc * xc, axis=-1, keepdims=True) + eps)
    return xc * r, r


def _layernorm_bwd(dy, xh, r, g):
    dxh = dy * g
    return r * (dxh - jnp.mean(dxh, axis=-1, keepdims=True) - xh * jnp.mean(dxh * xh, axis=-1, keepdims=True))


def _conv2_fwd(y2, z, ln_g, ln_b, tm, name):
    t, e = y2.shape

    def body(y_ref, g_ref, lg_ref, lb_ref, o_ref):
        xh, _ = _layernorm_parts(y_ref[...])
        o_ref[...] = (_silu(xh * lg_ref[...] + lb_ref[...]) * _silu(g_ref[...])).astype(o_ref.dtype)

    return pl.pallas_call(
        body, grid=(t // tm,), in_specs=[_row(tm, e), _row(tm, e, 2), _const((1, e)), _const((1, e))],
        out_specs=_row(tm, e), out_shape=_sds((t, e), BF16), compiler_params=_params("parallel"),
        name=name)(y2, z, ln_g, ln_b)


def _conv2_bwd(dy4, y2, z, ln_g, ln_b, tm, name):
    t, e = y2.shape

    def body(dy_ref, y_ref, g_ref, lg_ref, lb_ref, dy2_ref, dg_ref, dlg_ref, dlb_ref):
        @pl.when(pl.program_id(0) == 0)
        def _():
            dlg_ref[...] = jnp.zeros_like(dlg_ref)
            dlb_ref[...] = jnp.zeros_like(dlb_ref)

        dy, gz = dy_ref[...], g_ref[...]
        xh, r = _layernorm_parts(y_ref[...])
        y3 = xh * lg_ref[...] + lb_ref[...]
        dg_ref[...] = (dy * _silu(y3) * _dsilu(gz)).astype(dg_ref.dtype)
        dy3 = dy * _silu(gz) * _dsilu(y3)
        dlg_ref[...] += _rowsum(dy3 * xh)
        dlb_ref[...] += _rowsum(dy3)
        dy2_ref[...] = _layernorm_bwd(dy3, xh, r, lg_ref[...])

    return pl.pallas_call(
        body, grid=(t // tm,),
        in_specs=[_row(tm, e), _row(tm, e), _row(tm, e, 2), _const((1, e)), _const((1, e))],
        out_specs=[_row(tm, e), _row(tm, e), _const((1, e)), _const((1, e))],
        out_shape=[_sds((t, e)), _sds((t, e), BF16), _sds((1, e)), _sds((1, e))],
        compiler_params=_params("arbitrary"), name=name)(dy4, y2, z, ln_g, ln_b)


def _pool2_fwd(pm, w_grp, scale, z, tm, name):
    t, e = pm.shape
    ng, gw = w_grp.shape[0], w_grp.shape[1]

    def body(pm_ref, w_ref, sc_ref, g_ref, o_ref):
        for k in range(ng):
            cols = slice(k * gw, (k + 1) * gw)
            y = _dot(pm_ref[:, cols], w_ref[k])
            o_ref[:, cols] = (y * sc_ref[:, cols] * _silu(g_ref[:, cols])).astype(o_ref.dtype)

    return pl.pallas_call(
        body, grid=(t // tm,), in_specs=[_row(tm, e), _const(w_grp.shape), _const((1, e)), _row(tm, e, 1)],
        out_specs=_row(tm, e), out_shape=_sds((t, e), BF16), compiler_params=_params("parallel"),
        name=name)(pm, w_grp, scale, z)


def _pool2_bwd(dy2, pm, w_grp, scale, z, tm, name):
    t, e = pm.shape
    ng, gw = w_grp.shape[0], w_grp.shape[1]

    def body(dy_ref, pm_ref, w_ref, sc_ref, g_ref, dpm_ref, dg_ref, dsc_ref, dw_ref):
        @pl.when(pl.program_id(0) == 0)
        def _():
            dsc_ref[...] = jnp.zeros_like(dsc_ref)
            dw_ref[...] = jnp.zeros_like(dw_ref)

        for k in range(ng):
            cols = slice(k * gw, (k + 1) * gw)
            dy, gz, sc, pmk = dy_ref[:, cols], g_ref[:, cols], sc_ref[:, cols], pm_ref[:, cols]
            y = _dot(pmk, w_ref[k])
            dg_ref[:, cols] = (dy * (y * sc) * _dsilu(gz)).astype(dg_ref.dtype)
            dys = dy * _silu(gz)
            dsc_ref[:, cols] += _rowsum(dys * y)
            dyk = dys * sc
            dpm_ref[:, cols] = _dot_nt(dyk, w_ref[k])
            dw_ref[k] += _dot_tn(pmk, dyk)

    return pl.pallas_call(
        body, grid=(t // tm,),
        in_specs=[_row(tm, e), _row(tm, e), _const(w_grp.shape), _const((1, e)), _row(tm, e, 1)],
        out_specs=[_row(tm, e), _row(tm, e), _const((1, e)), _const(w_grp.shape)],
        out_shape=[_sds((t, e)), _sds((t, e), BF16), _sds((1, e)), _sds(w_grp.shape)],
        compiler_params=_params("arbitrary"), name=name)(dy2, pm, w_grp, scale, z)


def _rms_f(x, g, n):
    r = lax.rsqrt(jnp.sum(x * x, axis=-1, keepdims=True) * (1.0 / n) + EPS)
    return x * r * g


def _rms_b(x, g, dy, n):
    r = lax.rsqrt(jnp.sum(x * x, axis=-1, keepdims=True) * (1.0 / n) + EPS)
    xh = x * r
    dxh = dy * g
    return r * (dxh - xh * (jnp.sum(dxh * xh, axis=-1, keepdims=True) * (1.0 / n))), dy * xh


def _swap16(x):
    lane = lax.broadcasted_iota(jnp.int32, x.shape, 1)
    return jnp.where(lane % 32 < 16, pltpu.roll(x, LANES - 16, 1), pltpu.roll(x, 16, 1))


def _rope(x, c, s):
    return x * c + _swap16(x) * s


def _rope_t(dy, c, s):
    return dy * c + _swap16(dy * s)


def _rope_tables(l, lc, nb):
    t = jnp.arange(l, dtype=jnp.int32)
    row_id, col_id = (t // GRID_W).astype(F32), (t % GRID_W).astype(F32)
    axis_dim = MLA_ROPE // 2
    freqs = ROPE_THETA ** (-jnp.arange(0, axis_dim, 2, dtype=F32) / axis_dim)
    ar, ac = row_id[:, None] * freqs, col_id[:, None] * freqs
    pad1, pad0 = jnp.ones((l, LANES - MLA_ROPE), F32), jnp.zeros((l, LANES - MLA_ROPE), F32)
    ctab = jnp.concatenate([jnp.cos(ar), jnp.cos(ar), jnp.cos(ac), jnp.cos(ac), pad1], axis=1)
    stab = jnp.concatenate([-jnp.sin(ar), jnp.sin(ar), -jnp.sin(ac), jnp.sin(ac), pad0], axis=1)
    ctab = jnp.concatenate([jnp.tile(ctab, (nb, 1)), jnp.ones((nb * lc, LANES), F32)], axis=0)
    stab = jnp.concatenate([jnp.tile(stab, (nb, 1)), jnp.zeros((nb * lc, LANES), F32)], axis=0)
    return ctab, stab


def _kv_pre_fwd(zkv, kv_norm, rope_g, ctab, stab, tm, name):
    t = zkv.shape[0]

    def body(z_ref, gk_ref, gr_ref, c_ref, s_ref, ck_ref, kr_ref):
        ck_ref[...] = _rms_f(z_ref[:, :MLA_KV_RANK], gk_ref[...], MLA_KV_RANK).astype(ck_ref.dtype)
        kr = _rms_f(z_ref[:, MLA_KV_RANK:], gr_ref[...], MLA_ROPE)
        kr_ref[...] = _rope(kr, c_ref[...], s_ref[...]).astype(kr_ref.dtype)

    w = MLA_KV_RANK + LANES
    return pl.pallas_call(
        body, grid=(t // tm,),
        in_specs=[_row(tm, w), _const((1, MLA_KV_RANK)), _const((1, LANES)), _row(tm, LANES), _row(tm, LANES)],
        out_specs=[_row(tm, MLA_KV_RANK), _row(tm, LANES)],
        out_shape=[_sds((t, MLA_KV_RANK), BF16), _sds((t, LANES), BF16)],
        compiler_params=_params("parallel"), name=name)(zkv, kv_norm, rope_g, ctab, stab)


def _kv_pre_bwd(dck, dkr, zkv, kv_norm, rope_g, ctab, stab, tm, name):
    t = zkv.shape[0]
    w = MLA_KV_RANK + LANES

    def body(dck_ref, dkr_ref, z_ref, gk_ref, gr_ref, c_ref, s_ref, dz_ref, dgk_ref, dgr_ref):
        @pl.when(pl.program_id(0) == 0)
        def _():
            dgk_ref[...] = jnp.zeros_like(dgk_ref)
            dgr_ref[...] = jnp.zeros_like(dgr_ref)

        dx, dg = _rms_b(z_ref[:, :MLA_KV_RANK], gk_ref[...], dck_ref[...], MLA_KV_RANK)
        dz_ref[:, :MLA_KV_RANK] = dx.astype(dz_ref.dtype)
        dgk_ref[...] += _rowsum(dg)
        dy = _rope_t(dkr_ref[...], c_ref[...], s_ref[...])
        dx, dg = _rms_b(z_ref[:, MLA_KV_RANK:], gr_ref[...], dy, MLA_ROPE)
        dz_ref[:, MLA_KV_RANK:] = dx.astype(dz_ref.dtype)
        dgr_ref[...] += _rowsum(dg)

    return pl.pallas_call(
        body, grid=(t // tm,),
        in_specs=[_row(tm, MLA_KV_RANK), _row(tm, LANES), _row(tm, w), _const((1, MLA_KV_RANK)), _const((1, LANES)),
                  _row(tm, LANES), _row(tm, LANES)],
        out_specs=[_row(tm, w), _const((1, MLA_KV_RANK)), _const((1, LANES))],
        out_shape=[_sds((t, w), BF16), _sds((1, MLA_KV_RANK)), _sds((1, LANES))],
        compiler_params=_params("arbitrary"), name=name)(dck, dkr, zkv, kv_norm, rope_g, ctab, stab)


def _q_pre_fwd(zq, q_norm, tm, name):
    t, w = zq.shape

    def body(z_ref, g_ref, o_ref):
        o_ref[...] = _rms_f(z_ref[...], g_ref[...], w).astype(o_ref.dtype)

    return pl.pallas_call(
        body, grid=(t // tm,), in_specs=[_row(tm, w), _const((1, w))], out_specs=_row(tm, w),
        out_shape=_sds((t, w), BF16), compiler_params=_params("parallel"), name=name)(zq, q_norm)


def _q_pre_bwd(dcq, zq, q_norm, tm, name):
    t, w = zq.shape

    def body(d_ref, z_ref, g_ref, dz_ref, dg_ref):
        @pl.when(pl.program_id(0) == 0)
        def _():
            dg_ref[...] = jnp.zeros_like(dg_ref)

        dx, dg = _rms_b(z_ref[...], g_ref[...], d_ref[...], w)
        dz_ref[...] = dx.astype(dz_ref.dtype)
        dg_ref[...] += _rowsum(dg)

    return pl.pallas_call(
        body, grid=(t // tm,), in_specs=[_row(tm, w), _row(tm, w), _const((1, w))],
        out_specs=[_row(tm, w), _const((1, w))], out_shape=[_sds((t, w), BF16), _sds((1, w))],
        compiler_params=_params("arbitrary"), name=name)(dcq, zq, q_norm)


def _q_post_fwd(q, nope_g, rope_g, ctab, stab, tm, name):
    t, w = q.shape

    def body(q_ref, gn_ref, gr_ref, c_ref, s_ref, o_ref):
        for h in range(MLA_HEADS):
            a = h * HEAD_W
            qn = _rms_f(q_ref[:, a:a + LANES], gn_ref[...], MLA_NOPE)
            o_ref[:, a:a + LANES] = (qn * MLA_SCALE).astype(o_ref.dtype)
            qr = _rms_f(q_ref[:, a + LANES:a + HEAD_W], gr_ref[...], MLA_ROPE)
            o_ref[:, a + LANES:a + HEAD_W] = (_rope(qr, c_ref[...], s_ref[...]) * MLA_SCALE).astype(o_ref.dtype)

    return pl.pallas_call(
        body, grid=(t // tm,),
        in_specs=[_row(tm, w), _const((1, LANES)), _const((1, LANES)), _row(tm, LANES), _row(tm, LANES)],
        out_specs=_row(tm, w), out_shape=_sds((t, w), BF16), compiler_params=_params("parallel"),
        name=name)(q, nope_g, rope_g, ctab, stab)


def _q_post_bwd(dqf, q, nope_g, rope_g, ctab, stab, tm, name):
    t, w = q.shape

    def body(d_ref, q_ref, gn_ref, gr_ref, c_ref, s_ref, dq_ref, dgn_ref, dgr_ref):
        @pl.when(pl.program_id(0) == 0)
        def _():
            dgn_ref[...] = jnp.zeros_like(dgn_ref)
            dgr_ref[...] = jnp.zeros_like(dgr_ref)

        for h in range(MLA_HEADS):
            a = h * HEAD_W
            dx, dg = _rms_b(q_ref[:, a:a + LANES], gn_ref[...], d_ref[:, a:a + LANES] * MLA_SCALE, MLA_NOPE)
            dq_ref[:, a:a + LANES] = dx.astype(dq_ref.dtype)
            dgn_ref[...] += _rowsum(dg)
            dy = _rope_t(d_ref[:, a + LANES:a + HEAD_W] * MLA_SCALE, c_ref[...], s_ref[...])
            dx, dg = _rms_b(q_ref[:, a + LANES:a + HEAD_W], gr_ref[...], dy, MLA_ROPE)
            dq_ref[:, a + LANES:a + HEAD_W] = dx.astype(dq_ref.dtype)
            dgr_ref[...] += _rowsum(dg)

    return pl.pallas_call(
        body, grid=(t // tm,),
        in_specs=[_row(tm, w), _row(tm, w), _const((1, LANES)), _const((1, LANES)), _row(tm, LANES), _row(tm, LANES)],
        out_specs=[_row(tm, w), _const((1, LANES)), _const((1, LANES))],
        out_shape=[_sds((t, w), BF16), _sds((1, LANES)), _sds((1, LANES))],
        compiler_params=_params("arbitrary"), name=name)(dqf, q, nope_g, rope_g, ctab, stab)


def _k_post_fwd(kv, krr, nope_g, tm, name):
    t, w = kv.shape

    def body(kv_ref, kr_ref, gn_ref, k_ref, v_ref):
        for h in range(MLA_HEADS):
            a = h * HEAD_W
            k_ref[:, a:a + LANES] = _rms_f(kv_ref[:, a:a + LANES], gn_ref[...], MLA_NOPE).astype(k_ref.dtype)
            k_ref[:, a + LANES:a + HEAD_W] = kr_ref[...]
            v_ref[:, h * LANES:(h + 1) * LANES] = kv_ref[:, a + LANES:a + HEAD_W].astype(v_ref.dtype)

    return pl.pallas_call(
        body, grid=(t // tm,), in_specs=[_row(tm, w), _row(tm, LANES), _const((1, LANES))],
        out_specs=[_row(tm, w), _row(tm, w // 2)], out_shape=[_sds((t, w), BF16), _sds((t, w // 2), BF16)],
        compiler_params=_params("parallel"), name=name)(kv, krr, nope_g)


def _k_post_bwd(dkl, dkc, dvl, dvc, kv, nope_g, tm, name):
    t, w = kv.shape
    nl = dkl.shape[0] // tm

    def body(dkl_ref, dkc_ref, dvl_ref, dvc_ref, kv_ref, gn_ref, dkv_ref, dkr_ref, dgn_ref):
        i = pl.program_id(0)

        @pl.when(i == 0)
        def _():
            dgn_ref[...] = jnp.zeros_like(dgn_ref)

        dkr = jnp.zeros(dkr_ref.shape, F32)
        for h in range(MLA_HEADS):
            a = h * HEAD_W
            dk = jnp.where(i < nl, dkl_ref[:, a:a + HEAD_W], dkc_ref[:, a:a + HEAD_W])
            dv = jnp.where(i < nl, dvl_ref[:, h * LANES:(h + 1) * LANES], dvc_ref[:, h * LANES:(h + 1) * LANES])
            dx, dg = _rms_b(kv_ref[:, a:a + LANES], gn_ref[...], dk[:, :LANES], MLA_NOPE)
            dkv_ref[:, a:a + LANES] = dx.astype(dkv_ref.dtype)
            dgn_ref[...] += _rowsum(dg)
            dkv_ref[:, a + LANES:a + HEAD_W] = dv.astype(dkv_ref.dtype)
            dkr = dkr + dk[:, LANES:]
        dkr_ref[...] = dkr

    lat = lambda cols: pl.BlockSpec((tm, cols), lambda i: (jnp.minimum(i, nl - 1), 0))
    ctx = lambda cols: pl.BlockSpec((tm, cols), lambda i: (jnp.maximum(i - nl, 0), 0))
    return pl.pallas_call(
        body, grid=(t // tm,),
        in_specs=[lat(w), ctx(w), lat(w // 2), ctx(w // 2), _row(tm, w), _const((1, LANES))],
        out_specs=[_row(tm, w), _row(tm, LANES), _const((1, LANES))],
        out_shape=[_sds((t, w), BF16), _sds((t, LANES)), _sds((1, LANES))],
        compiler_params=_params("arbitrary"), name=name)(dkl, dkc, dvl, dvc, kv, nope_g)


def _attn_specs(nb, l, lc, tq):
    nq = l // tq
    ctx0 = nb * l // lc
    q_spec = lambda w: pl.BlockSpec((tq, w), lambda b, h, i: (b * nq + i, h))
    lat = lambda w: pl.BlockSpec((l, w), lambda b, h, i: (b, h))
    ctx = lambda w: pl.BlockSpec((lc, w), lambda b, h, i: (ctx0 + b, h))
    return nq, q_spec, lat, ctx


def _attn_fwd(qf, kf, vf, nb, l, lc, name):
    tq = _pick(l, (512, 256, 128))
    nq, q_spec, lat, ctx = _attn_specs(nb, l, lc, tq)
    sub = tq // ATTN_CHAINS

    def body(q_ref, kl_ref, kc_ref, vl_ref, vc_ref, o_ref, lse_ref):
        for r in range(0, tq, sub):
            q = q_ref[r:r + sub, :]
            s1, s2 = _dot_nt(q, kl_ref[...]), _dot_nt(q, kc_ref[...])
            m = jnp.maximum(jnp.max(s1, axis=-1, keepdims=True), jnp.max(s2, axis=-1, keepdims=True))
            p1, p2 = jnp.exp(s1 - m), jnp.exp(s2 - m)
            den = jnp.sum(p1, axis=-1, keepdims=True) + jnp.sum(p2, axis=-1, keepdims=True)
            o_ref[r:r + sub, :] = (_dot(p1, vl_ref[...]) + _dot(p2, vc_ref[...])) / den
            lse_ref[r:r + sub, :] = jnp.broadcast_to(m + jnp.log(den), (sub, LANES))

    return pl.pallas_call(
        body, grid=(nb, MLA_HEADS, nq),
        in_specs=[q_spec(HEAD_W), lat(HEAD_W), ctx(HEAD_W), lat(LANES), ctx(LANES)],
        out_specs=[q_spec(LANES), q_spec(LANES)],
        out_shape=[_sds((nb * l, MLA_HEADS * LANES)), _sds((nb * l, MLA_HEADS * LANES))],
        compiler_params=_params("parallel", "parallel", "arbitrary"), name=name)(qf, kf, kf, vf, vf)


def _attn_bwd(do, o, lse, qf, kf, vf, nb, l, lc, name, side=None):
    tq = _pick(l, (1024, 512, 256, 128))
    nq, q_spec, lat, ctx = _attn_specs(nb, l, lc, tq)
    out_lat = lambda w: pl.BlockSpec((l, w), lambda b, h, i: (b, h))
    out_ctx = lambda w: pl.BlockSpec((lc, w), lambda b, h, i: (b, h))
    sub = tq // ATTN_CHAINS

    def body(do_ref, o_ref, lse_ref, q_ref, kl_ref, kc_ref, vl_ref, vc_ref, dq_ref, dkl_ref, dkc_ref, dvl_ref, dvc_ref):
        @pl.when(pl.program_id(2) == 0)
        def _():
            dkl_ref[...] = jnp.zeros_like(dkl_ref)
            dkc_ref[...] = jnp.zeros_like(dkc_ref)
            dvl_ref[...] = jnp.zeros_like(dvl_ref)
            dvc_ref[...] = jnp.zeros_like(dvc_ref)

        parts = []
        for r in range(0, tq, sub):
            q, dof = q_ref[r:r + sub, :], do_ref[r:r + sub, :]
            delta = jnp.sum(dof * o_ref[r:r + sub, :], axis=-1, keepdims=True)
            lse = lse_ref[r:r + sub, :1]
            dq, part = jnp.zeros((sub, HEAD_W), F32), []
            for k_ref, v_ref in ((kl_ref, vl_ref), (kc_ref, vc_ref)):
                p = jnp.exp(_dot_nt(q, k_ref[...]) - lse)
                ds = p * (_dot_nt(dof, v_ref[...]) - delta)
                dq = dq + _dot(ds, k_ref[...])
                part += [_dot_tn(ds, q), _dot_tn(p, dof)]
            dq_ref[r:r + sub, :] = dq
            parts.append(part)
        for n, ref in enumerate((dkl_ref, dvl_ref, dkc_ref, dvc_ref)):
            ref[...] += functools.reduce(lambda u, v: u + v, [part[n] for part in parts])

    kw, vw = MLA_HEADS * HEAD_W, MLA_HEADS * LANES
    return _pallas(
        body, (do, o, lse, qf, kf, kf, vf, vf), grid=(nb, MLA_HEADS, nq),
        in_specs=[q_spec(LANES), q_spec(LANES), q_spec(LANES), q_spec(HEAD_W), lat(HEAD_W), ctx(HEAD_W), lat(LANES),
                  ctx(LANES)],
        out_specs=[q_spec(HEAD_W), out_lat(HEAD_W), out_ctx(HEAD_W), out_lat(LANES), out_ctx(LANES)],
        out_shape=[_sds((nb * l, kw)), _sds((nb * l, kw)), _sds((nb * lc, kw)), _sds((nb * l, vw)),
                   _sds((nb * lc, vw))],
        sem=("parallel", "parallel", "arbitrary"), name=name, side=side)


def _gate_fwd(o, g, tm, name):
    t, e = o.shape

    def body(o_ref, g_ref, y_ref):
        y_ref[...] = (o_ref[...] * _silu(g_ref[...])).astype(y_ref.dtype)

    return pl.pallas_call(
        body, grid=(t // tm,), in_specs=[_row(tm, e), _row(tm, e)], out_specs=_row(tm, e),
        out_shape=_sds((t, e), BF16), compiler_params=_params("parallel"), name=name)(o, g)


def _gate_bwd(dy, o, g, tm, name):
    t, e = o.shape

    def body(dy_ref, o_ref, g_ref, do_ref, dg_ref):
        dy, gz = dy_ref[...], g_ref[...]
        do_ref[...] = dy * _silu(gz)
        dg_ref[...] = (dy * o_ref[...] * _dsilu(gz)).astype(dg_ref.dtype)

    return pl.pallas_call(
        body, grid=(t // tm,), in_specs=[_row(tm, e), _row(tm, e), _row(tm, e)],
        out_specs=[_row(tm, e), _row(tm, e)], out_shape=[_sds((t, e)), _sds((t, e), BF16)],
        compiler_params=_params("parallel"), name=name)(dy, o, g)


def _chunk_fwd(z, ln_g, ln_b, w_s, bs_full, name):
    t, e = z.shape[0], z.shape[1] // 3

    def body(u_ref, v_ref, g_ref, lg_ref, lb_ref, w_ref, bs_ref, y_ref):
        xh, _ = _layernorm_parts(v_ref[...])
        vn = xh * lg_ref[...] + lb_ref[...]
        for k in range(CHUNK_GROUPS):
            cols = slice(k * LANES, (k + 1) * LANES)
            s = _dot(w_ref[k], vn[:, cols]) + bs_ref[:, cols]
            y_ref[:, cols] = (u_ref[:, cols] * s * _silu(g_ref[:, cols])).astype(y_ref.dtype)

    return pl.pallas_call(
        body, grid=(t // CHUNK,),
        in_specs=[_row(CHUNK, e, 0), _row(CHUNK, e, 1), _row(CHUNK, e, 2), _const((1, e)), _const((1, e)),
                  _const(w_s.shape), _const((CHUNK, e))],
        out_specs=_row(CHUNK, e), out_shape=_sds((t, e), BF16), compiler_params=_params("parallel"),
        name=name)(z, z, z, ln_g, ln_b, w_s, bs_full)


def _chunk_bwd(dy, z, ln_g, ln_b, w_s, bs_full, name):
    t, e = z.shape[0], z.shape[1] // 3

    def body(dy_ref, u_ref, v_ref, g_ref, lg_ref, lb_ref, w_ref, bs_ref, dz_ref, dw_ref, dbs_ref, dlg_ref, dlb_ref,
             acc_ref):
        i = pl.program_id(0)

        @pl.when(i == 0)
        def _():
            dw_ref[...] = jnp.zeros_like(dw_ref)
            dlg_ref[...] = jnp.zeros_like(dlg_ref)
            dlb_ref[...] = jnp.zeros_like(dlb_ref)
            acc_ref[...] = jnp.zeros_like(acc_ref)

        xh, r = _layernorm_parts(v_ref[...])
        vn = xh * lg_ref[...] + lb_ref[...]
        dvn = []
        for k in range(CHUNK_GROUPS):
            cols = slice(k * LANES, (k + 1) * LANES)
            dyk, u, gz = dy_ref[:, cols], u_ref[:, cols], g_ref[:, cols]
            s = _dot(w_ref[k], vn[:, cols]) + bs_ref[:, cols]
            sg = _silu(gz)
            dz_ref[:, cols] = (dyk * s * sg).astype(dz_ref.dtype)
            dz_ref[:, 2 * e + k * LANES:2 * e + (k + 1) * LANES] = (dyk * u * s * _dsilu(gz)).astype(dz_ref.dtype)
            ds = dyk * u * sg
            acc_ref[:, cols] += ds
            dw_ref[k] += _dot_nt(ds, vn[:, cols])
            dvn.append(_dot_tn(w_ref[k], ds))
        dvn = jnp.concatenate(dvn, axis=1)
        dlg_ref[...] += _rowsum(dvn * xh)
        dlb_ref[...] += _rowsum(dvn)
        dz_ref[:, e:2 * e] = _layernorm_bwd(dvn, xh, r, lg_ref[...]).astype(dz_ref.dtype)

        @pl.when(i == pl.num_programs(0) - 1)
        def _():
            lane = lax.broadcasted_iota(jnp.int32, dbs_ref.shape, 1)
            out = jnp.zeros(dbs_ref.shape, F32)
            for k in range(CHUNK_GROUPS):
                col = jnp.sum(acc_ref[:, k * LANES:(k + 1) * LANES], axis=1, keepdims=True)
                out = jnp.where(lane == k, col, out)
            dbs_ref[...] = out

    return pl.pallas_call(
        body, grid=(t // CHUNK,),
        in_specs=[_row(CHUNK, e), _row(CHUNK, e, 0), _row(CHUNK, e, 1), _row(CHUNK, e, 2), _const((1, e)),
                  _const((1, e)), _const(w_s.shape), _const((CHUNK, e))],
        out_specs=[_row(CHUNK, 3 * e), _const(w_s.shape), _const((CHUNK, CHUNK_GROUPS)), _const((1, e)),
                   _const((1, e))],
        out_shape=[_sds((t, 3 * e), BF16), _sds(w_s.shape), _sds((CHUNK, CHUNK_GROUPS)), _sds((1, e)), _sds((1, e))],
        scratch_shapes=[pltpu.VMEM((CHUNK, e), F32)],
        compiler_params=_params("arbitrary"), name=name)(dy, z, z, z, ln_g, ln_b, w_s, bs_full)


def _mod_rows(mods, layer, d, nseg):
    m = mods[layer, :nseg]
    return [m[:, None, k * d:(k + 1) * d] for k in range(3)]


def _local_step(x, ctx, tgt, w, mods, comm=None):
    nb, l, d = x.shape
    lc = ctx.shape[1]
    e = d
    tl, ta = nb * l, nb * (l + lc)
    tm = _pick(lc, (256, 128))
    segs_a, segs_l = _Segs((l,) * nb + (lc,) * nb, tm), _Segs((l,) * nb, tm)
    norm_g = w['norm_g']
    g = {}

    def carried(tag, fn, *args, **kw):
        if comm is None:
            return fn(*args, **kw)
        res, brought = fn(*args, side=comm.side(tag), **kw)
        comm.done(tag, brought)
        return res

    xa0 = jnp.concatenate([x.reshape(tl, d), ctx.reshape(nb * lc, d)], axis=0)

    sh0, sc0, gt0 = _mod_rows(mods, 0, d, 2 * nb)
    h0 = _rms_mod_fwd(xa0, norm_g[0:1], sc0, sh0, segs_a, "l0_norm")
    z0 = carried('fwd1', _mm, h0, w['cv_w_in'], "l0_in")
    y2_0 = carried('fwd2', _conv1_fwd, z0, w['cv_dw'], w['cv_db'], nb, l, 0, "l0_conv_lat")
    y2_0 = _conv1_fwd(z0, w['cv_dw'], w['cv_db'], nb, lc, tl, "l0_conv_ctx", into=y2_0)
    y4_0 = _conv2_fwd(y2_0, z0, w['cv_ln_g'], w['cv_ln_b'], tm, "l0_gate")
    o0, xa1 = _mm(y4_0, w['cv_w_out'], "l0_out", resid=(xa0, gt0, segs_a))

    sh1, sc1, gt1 = _mod_rows(mods, 1, d, 2 * nb)
    h1 = _rms_mod_fwd(xa1, norm_g[1:2], sc1, sh1, segs_a, "l1_norm")
    z1 = carried('fwd3', _mm, h1, w['pl_w_in'], "l1_in")
    taps_l, ic_l = _pool_tables(l, e)
    taps_c, ic_c = _pool_tables(lc, e)
    pm1 = _pool1(z1, taps_l, ic_l, nb, l, 0, False, "l1_pool_lat", BF16)
    pm1 = _pool1(z1, taps_c, ic_c, nb, lc, tl, False, "l1_pool_ctx", BF16, into=pm1)
    y2_1 = _pool2_fwd(pm1, w['pl_w_grp'], w['pl_scale'], z1, tm, "l1_group")
    o1, xa2 = _mm(y2_1, w['pl_w_out'], "l1_out", resid=(xa1, gt1, segs_a))

    sh2, sc2, gt2 = _mod_rows(mods, 2, d, 2 * nb)
    h2 = _rms_mod_fwd(xa2, norm_g[2:3], sc2, sh2, segs_a, "l2_norm")
    w_in = w['ml_w_in']
    kvc = MLA_KV_RANK + MLA_ROPE
    w_in_p = jnp.concatenate([w_in[:, :kvc], jnp.zeros((d, LANES - MLA_ROPE), w_in.dtype), w_in[:, kvc:]], axis=1)
    w_uq_p = jnp.pad(w['ml_w_uq'].reshape(MLA_Q_RANK, MLA_HEADS, MLA_NOPE + MLA_ROPE),
                     ((0, 0), (0, 0), (0, HEAD_W - MLA_NOPE - MLA_ROPE))).reshape(MLA_Q_RANK, MLA_HEADS * HEAD_W)
    rope_g = jnp.pad(w['ml_rope_norm'], ((0, 0), (0, LANES - MLA_ROPE)))
    nope_g = w['ml_nope_norm']
    ctab, stab = _rope_tables(l, lc, nb)
    kvw = MLA_KV_RANK + LANES
    w_kv, w_q, w_g = w_in_p[:, :kvw], w_in_p[:, kvw:kvw + MLA_Q_RANK], w_in_p[:, kvw + MLA_Q_RANK:]
    zkv = _mm(h2, w_kv, "l2_in_kv")
    zq, zg = _mm(h2, w_q, "l2_in_q", rows=tl), _mm(h2, w_g, "l2_in_g", rows=tl)
    ckvn, krr = _kv_pre_fwd(zkv, w['ml_kv_norm'], rope_g[1:2], ctab, stab, tm, "l2_kv_pre")
    cqn = _q_pre_fwd(zq, w['ml_q_norm'], tm, "l2_q_pre")
    q2 = _mm(cqn, w_uq_p, "l2_uq")
    kv2 = _mm(ckvn, w['ml_w_ukv'], "l2_ukv")
    qf = _q_post_fwd(q2, nope_g[0:1], rope_g[0:1], ctab, stab, tm, "l2_q_post")
    kf, vf = _k_post_fwd(kv2, krr, nope_g[1:2], tm, "l2_k_post")
    o_att, lse = _attn_fwd(qf, kf, vf, nb, l, lc, "l2_attn")
    og = _gate_fwd(o_att, zg, tm, "l2_gate")
    o2, x3 = _mm(og, w['ml_w_out'], "l2_out", resid=(xa2, gt2[:nb], segs_l))

    sh3, sc3, gt3 = _mod_rows(mods, 3, d, nb)
    h3 = _rms_mod_fwd(x3, norm_g[3:4], sc3, sh3, segs_l, "l3_norm")
    z3 = _mm(h3, w['ch_w_in'], "l3_in")
    bs_full = jnp.repeat(w['ch_b_s'], e // CHUNK_GROUPS, axis=1)
    y3 = _chunk_fwd(z3, w['ch_ln_g'], w['ch_ln_b'], w['ch_w_s'], bs_full, "l3_chunk")
    o3, x4 = _mm(y3, w['ch_w_out'], "l3_out", resid=(x3, gt3, segs_l))

    loss_vec, dx4 = _loss_head(x4, tgt.reshape(tl, d), tm, "loss")

    do3, dgt3 = _resid_bwd(dx4, o3, gt3, segs_l, "l3_resid_b")
    dy3 = _mm_nt(do3, w['ch_w_out'], "l3_out_bx")
    g['ch_w_out'] = _mm_tn(y3, do3, "l3_out_bw")
    dz3, g['ch_w_s'], g['ch_b_s'], g['ch_ln_g'], g['ch_ln_b'] = _chunk_bwd(
        dy3, z3, w['ch_ln_g'], w['ch_ln_b'], w['ch_w_s'], bs_full, "l3_chunk_b")
    dh3 = _mm_nt(dz3, w['ch_w_in'], "l3_in_bx")
    g['ch_w_in'] = _mm_tn(h3, dz3, "l3_in_bw", shards=N_DEV)
    dx3, dng3, dsc3, dsh3 = _rms_mod_bwd(x3, norm_g[3:4], sc3, sh3, dh3, dx4, segs_l, "l3_norm_b")
    if comm is not None:
        comm.grads_ready(3, g)

    do2, dgt2 = _resid_bwd(dx3, o2, gt2[:nb], segs_l, "l2_resid_b")
    dog = _mm_nt(do2, w['ml_w_out'], "l2_out_bx")
    g['ml_w_out'] = _mm_tn(og, do2, "l2_out_bw")
    d_att, dzg = _gate_bwd(dog, o_att, zg, tm, "l2_gate_b")
    dqf, dkl, dkc, dvl, dvc = carried('quad3', _attn_bwd, d_att, o_att, lse, qf, kf, vf, nb, l, lc, "l2_attn_b")
    dq2, dnope_q, drope_q = _q_post_bwd(dqf, q2, nope_g[0:1], rope_g[0:1], ctab, stab, tm, "l2_q_post_b")
    dkv2, dkrr, dnope_k = _k_post_bwd(dkl, dkc, dvl, dvc, kv2, nope_g[1:2], tm, "l2_k_post_b")
    dcqn = _mm_nt(dq2, w_uq_p, "l2_uq_bx")
    g_uq_p = _mm_tn(cqn, dq2, "l2_uq_bw")
    dckvn = _mm_nt(dkv2, w['ml_w_ukv'], "l2_ukv_bx")
    g['ml_w_ukv'] = _mm_tn(ckvn, dkv2, "l2_ukv_bw", shards=N_DEV)
    dzq, g['ml_q_norm'] = _q_pre_bwd(dcqn, zq, w['ml_q_norm'], tm, "l2_q_pre_b")
    dzkv, g['ml_kv_norm'], drope_k = _kv_pre_bwd(dckvn, dkrr, zkv, w['ml_kv_norm'], rope_g[1:2], ctab, stab, tm,
                                                  "l2_kv_pre_b")
    dh2 = _mm_nt_sum([(dzkv, w_kv), (dzq, w_q), (dzg, w_g)], "l2_in_bx")
    g['ml_w_in'] = jnp.concatenate([_mm_tn(h2, dzkv, "l2_in_kv_bw")[:, :kvc], _mm_tn(h2, dzq, "l2_in_q_bw", rows=tl),
                                    _mm_tn(h2, dzg, "l2_in_g_bw", rows=tl)], axis=1)
    g['ml_w_uq'] = g_uq_p.reshape(MLA_Q_RANK, MLA_HEADS, HEAD_W)[:, :, :MLA_NOPE + MLA_ROPE].reshape(
        MLA_Q_RANK, MLA_HEADS * (MLA_NOPE + MLA_ROPE))
    g['ml_nope_norm'] = jnp.concatenate([dnope_q, dnope_k], axis=0)
    g['ml_rope_norm'] = jnp.concatenate([drope_q, drope_k], axis=0)[:, :MLA_ROPE]
    dxa2, dng2, dsc2, dsh2 = _rms_mod_bwd(xa2, norm_g[2:3], sc2, sh2, dh2, dx3, segs_a, "l2_norm_b")
    if comm is not None:
        comm.grads_ready(2, g)

    do1, dgt1 = _resid_bwd(dxa2, o1, gt1, segs_a, "l1_resid_b")
    dy2_1 = _mm_nt(do1, w['pl_w_out'], "l1_out_bx")
    g['pl_w_out'] = _mm_tn(y2_1, do1, "l1_out_bw")
    dpm, dgz1, g['pl_scale'], g['pl_w_grp'] = _pool2_bwd(dy2_1, pm1, w['pl_w_grp'], w['pl_scale'], z1, tm,
                                                          "l1_group_b")
    dv1 = _pool1(dpm, taps_l, ic_l, nb, l, 0, True, "l1_pool_lat_b", BF16)
    dv1 = _pool1(dpm, taps_c, ic_c, nb, lc, tl, True, "l1_pool_ctx_b", BF16, into=dv1)
    dz1 = jnp.concatenate([dv1, dgz1], axis=1)
    dh1 = _mm_nt(dz1, w['pl_w_in'], "l1_in_bx")
    g['pl_w_in'] = carried('quad2', _mm_tn, h1, dz1, "l1_in_bw", shards=N_DEV)
    dxa1, dng1, dsc1, dsh1 = _rms_mod_bwd(xa1, norm_g[1:2], sc1, sh1, dh1, dxa2, segs_a, "l1_norm_b")
    if comm is not None:
        comm.grads_ready(1, g)

    do0, dgt0 = _resid_bwd(dxa1, o0, gt0, segs_a, "l0_resid_b")
    dy4 = _mm_nt(do0, w['cv_w_out'], "l0_out_bx")
    g['cv_w_out'] = _mm_tn(y4_0, do0, "l0_out_bw")
    dy2, dgz0, g['cv_ln_g'], g['cv_ln_b'] = _conv2_bwd(dy4, y2_0, z0, w['cv_ln_g'], w['cv_ln_b'], tm, "l0_gate_b")
    da_l, db_l, ddw, ddb = carried('quad1', _conv1_bwd, dy2, z0, w['cv_dw'], jnp.zeros((CONV_WIDTH, e), F32),
                                   jnp.zeros((1, e), F32), nb, l, 0, "l0_conv_lat_b")
    da, db_, g['cv_dw'], g['cv_db'] = _conv1_bwd(dy2, z0, w['cv_dw'], ddw, ddb, nb, lc, tl, "l0_conv_ctx_b",
                                                 into=(da_l, db_l))
    dz0 = jnp.concatenate([da, db_, dgz0], axis=1)
    dh0 = _mm_nt(dz0, w['cv_w_in'], "l0_in_bx")
    g['cv_w_in'] = _mm_tn(h0, dz0, "l0_in_bw", shards=N_DEV)
    dxa0, dng0, dsc0, dsh0 = _rms_mod_bwd(xa0, norm_g[0:1], sc0, sh0, dh0, dxa1, segs_a, "l0_norm_b")
    if comm is not None:
        comm.grads_ready(0, g)

    def rows4(t):
        return jnp.pad(t[:, 0], ((0, 2 * nb - t.shape[0]), (0, 0)))

    dmods = jnp.stack([
        jnp.concatenate([rows4(dsh0), rows4(dsc0), rows4(dgt0)], axis=1),
        jnp.concatenate([rows4(dsh1), rows4(dsc1), rows4(dgt1)], axis=1),
        jnp.concatenate([rows4(dsh2), rows4(dsc2), rows4(dgt2)], axis=1),
        jnp.concatenate([rows4(dsh3), rows4(dsc3), rows4(dgt3)], axis=1)])
    dnorm_g = jnp.concatenate([dng0, dng1, dng2, dng3], axis=0)
    return loss_vec, dxa0[:tl].reshape(nb, l, d), g, dmods, dnorm_g


def _mesh_pos():
    return lax.axis_index("x"), lax.axis_index("y"), lax.axis_index("c")


def _remote(src, dst, send_sems, recv_sems, k, dev):
    return pltpu.make_async_remote_copy(src_ref=src, dst_ref=dst, send_sem=send_sems.at[k], recv_sem=recv_sems.at[k],
                                        device_id=dev, device_id_type=pl.DeviceIdType.MESH)


def _comm_call(body, xs, out_shapes, n_remote, n_local, name):
    hbm = pl.BlockSpec(memory_space=pltpu.HBM)
    return pl.pallas_call(
        body, in_specs=[hbm] * len(xs), out_specs=[hbm] * len(out_shapes), out_shape=out_shapes,
        scratch_shapes=_sem_shapes(n_remote, n_local),
        compiler_params=pltpu.CompilerParams(has_side_effects=True), name=name)(*xs)


def _run_side(side, name):
    n = len(side.xs)

    def body(*refs):
        side.start(refs[:n], refs[n:n + len(side.out_shapes)], *refs[n + len(side.out_shapes):])
        side.finish(refs[:n], refs[n:n + len(side.out_shapes)], *refs[n + len(side.out_shapes):])

    return _comm_call(body, side.xs, side.out_shapes, side.n_remote, side.n_local, name)


def _gather_side(xs):
    n = len(xs)

    def plan(x_refs, o_refs, send_sems, recv_sems, local_sems):
        x, y, c = _mesh_pos()
        me, sib = (x, y, c), (x, y, 1 - c)
        chips = [(1 - x, y), (x, 1 - y), (1 - x, 1 - y)]

        def slot(a, p):
            return o_refs[a].at[4 * p[0] + 2 * p[1] + p[2]]

        def copy(a, k, block, to, src=None):
            return _remote(slot(a, block) if src is None else src, slot(a, block), send_sems, recv_sems, 7 * a + k, to)

        mine = [pltpu.make_async_copy(x_refs[a], slot(a, me), local_sems.at[a]) for a in range(n)]
        first = []
        for a in range(n):
            first += [copy(a, 1 + j, me, chip + (c,), src=x_refs[a]) for j, chip in enumerate(chips)]
            first.append(copy(a, 0, me, sib, src=x_refs[a]))
        return me, sib, c, chips, copy, mine, first

    def start(x_refs, o_refs, send_sems, recv_sems, local_sems):
        _, _, _, _, _, mine, first = plan(x_refs, o_refs, send_sems, recv_sems, local_sems)
        for cp in mine + first:
            cp.start()

    def finish(x_refs, o_refs, send_sems, recv_sems, local_sems):
        me, sib, c, chips, copy, mine, first = plan(x_refs, o_refs, send_sems, recv_sems, local_sems)
        passed = []
        for j, chip in enumerate(chips):
            for a in range(n):
                copy(a, 1 + j, chip + (c,), me).wait_recv()
                passed.append(copy(a, 4 + j, chip + (c,), sib))
                passed[-1].start()
        for a in range(n):
            copy(a, 0, sib, me).wait_recv()
        for j, chip in enumerate(chips):
            for a in range(n):
                copy(a, 4 + j, chip + (1 - c,), me).wait_recv()
        for cp in first + passed:
            cp.wait_send()
        for cp in mine:
            cp.wait()

    return _Side(xs, [_sds((N_DEV,) + x.shape, x.dtype) for x in xs], 7 * n, n, start, finish)


def _gather_all(xs, name):
    return _run_side(_gather_side(xs), name)


def _swap_halves(xs, name):
    n = len(xs)

    def body(*refs):
        x_refs, o_refs, (send_sems, recv_sems, _) = refs[:n], refs[n:2 * n], refs[2 * n:]
        x, y, c = _mesh_pos()
        copies = [_remote(x_refs[a].at[q, 1 - c], o_refs[a].at[q], send_sems, recv_sems, 4 * a + q, (x, y, 1 - c))
                  for a in range(n) for q in range(4)]
        for cp in copies:
            cp.start()
        for cp in copies:
            cp.wait_recv()
        for cp in copies:
            cp.wait_send()

    return _comm_call(body, xs, [_sds((4,) + x.shape[2:], x.dtype) for x in xs], 4 * n, 0, name)


def _quad_side(xs):
    n = len(xs)

    def plan(x_refs, o_refs, send_sems, recv_sems, local_sems):
        x, y, c = _mesh_pos()
        q = 2 * x + y
        chips = [(1 - x, y), (x, 1 - y), (1 - x, 1 - y)]
        mine = [pltpu.make_async_copy(x_refs[a].at[q], o_refs[a].at[q], local_sems.at[a]) for a in range(n)]
        sends, arrivals = [], []
        for a in range(n):
            for j, chip in enumerate(chips):
                qj = 2 * chip[0] + chip[1]
                sends.append(_remote(x_refs[a].at[qj], o_refs[a].at[q], send_sems, recv_sems, 3 * a + j, chip + (c,)))
                arrivals.append(_remote(x_refs[a].at[qj], o_refs[a].at[qj], send_sems, recv_sems, 3 * a + j,
                                        chip + (c,)))
        return mine, sends, arrivals

    def start(*refs):
        mine, sends, _ = plan(*refs)
        for cp in mine + sends:
            cp.start()

    def finish(*refs):
        mine, sends, arrivals = plan(*refs)
        for cp in arrivals:
            cp.wait_recv()
        for cp in sends:
            cp.wait_send()
        for cp in mine:
            cp.wait()

    return _Side(xs, [_sds(x.shape, x.dtype) for x in xs], 3 * n, n, start, finish)


def _pair_add(xs, rs, name):
    n = len(xs)

    def body(*refs):
        c = lax.axis_index("c")
        for x_ref, r_ref, o_ref in zip(refs[:n], refs[n:2 * n], refs[2 * n:]):
            o_ref[...] = (x_ref[c].astype(F32) + r_ref[...].astype(F32)).astype(o_ref.dtype)

    slot = lambda x: pl.BlockSpec((None,) + x.shape[2:], lambda q: (q, 0, 0))
    return pl.pallas_call(
        body, grid=(4,),
        in_specs=[pl.BlockSpec((None, 2) + x.shape[2:], lambda q: (q, 0, 0, 0)) for x in xs] + [slot(x) for x in xs],
        out_specs=[slot(x) for x in xs], out_shape=[_sds((4,) + x.shape[2:], x.dtype) for x in xs],
        compiler_params=_params("parallel"), name=name)(*xs, *rs)


def _pack_rows(n):
    r = -(-n // PACK_COLS)
    return -(-r // 256) * 256 if r > 256 else -(-r // 16) * 16


def _pack(arrs, dtype):
    flat = jnp.concatenate([a.reshape(-1).astype(dtype) for a in arrs])
    rows = _pack_rows(flat.shape[0])
    return jnp.pad(flat, (0, rows * PACK_COLS - flat.shape[0])).reshape(rows, PACK_COLS)


def _pack_shards(arrs):
    flat = jnp.concatenate([a.astype(F32) for a in arrs], axis=1)
    rows = _pack_rows(flat.shape[1])
    return jnp.pad(flat, ((0, 0), (0, rows * PACK_COLS - flat.shape[1]))).reshape(N_DEV, rows, PACK_COLS)


def _unpack(packed, shapes, lead=()):
    flat = packed.reshape(tuple(lead) + (-1,))
    out, off = [], 0
    for s in shapes:
        n = 1
        for v in s:
            n *= v
        out.append(flat[..., off:off + n].reshape(tuple(lead) + tuple(s)))
        off += n
    return out


def _to_shards(full, ax):
    s = full.shape
    t = full.reshape(s[:ax] + (N_DEV, s[ax] // N_DEV) + s[ax + 1:])
    return jnp.moveaxis(t, ax, 0).reshape(N_DEV, -1)


def _from_shards(shards, local_shape, ax):
    t = jnp.moveaxis(shards.reshape((N_DEV,) + tuple(local_shape)), 0, ax)
    s = t.shape
    return t.reshape(s[:ax] + (s[ax] * s[ax + 1],) + s[ax + 2:])


def _mod_fwd(c_rows, w_mod, b_mod, name):
    nl, d, n = w_mod.shape
    r = c_rows.shape[0]

    def body(c_ref, w_ref, b_ref, o_ref):
        s = _silu(c_ref[...])
        for l in range(nl):
            o_ref[l] = _dot(s, w_ref[l]) + b_ref[l]

    return pl.pallas_call(body, out_shape=_sds((nl, r, n)),
                          compiler_params=pltpu.CompilerParams(vmem_limit_bytes=VMEM_LIMIT), name=name)(
        c_rows, w_mod, b_mod)


def _mod_bwd(c_rows, dcols, dall, w_mod, c_ctx, name):
    nl, d, n = w_mod.shape
    r = c_rows.shape[0]

    def body(c_ref, dc_ref, da_ref, w_ref, cc_ref, gw_ref, gb_ref, gc_ref):
        s = _silu(c_ref[...])
        ds = jnp.zeros((r, d), F32)
        for l in range(nl):
            gw_ref[l] = _dot_tn(s, dc_ref[l])
            gb_ref[l] = _rowsum(da_ref[l])
            ds = ds + _dot_nt(dc_ref[l], w_ref[l])
        row = lax.broadcasted_iota(jnp.int32, (r, d), 0)
        gc_ref[...] = _rowsum(jnp.where(row % 4 >= 2, ds, 0.0)) * _dsilu(cc_ref[...])

    return pl.pallas_call(body, out_shape=[_sds((nl, d, n)), _sds((nl, 1, 3 * d)), _sds((1, d))],
                          compiler_params=pltpu.CompilerParams(vmem_limit_bytes=VMEM_LIMIT), name=name)(
        c_rows, dcols, dall, w_mod, c_ctx)


def _adam_math(w, gsum, m, v):
    c1, c2 = 1.0 - ADAM_B1 ** ADAM_STEP, 1.0 - ADAM_B2 ** ADAM_STEP
    mn = ADAM_B1 * m + (1.0 - ADAM_B1) * gsum
    vn = ADAM_B2 * v + (1.0 - ADAM_B2) * (gsum * gsum)
    return -ADAM_LR * ((mn / c1) / (jnp.sqrt(vn / c2) + ADAM_EPS) + ADAM_WD * w), mn, vn


def _adam(w, gparts, row0, m, v, name):
    rows, cols = w.shape
    npart = gparts.shape[0]
    if rows % 8:
        tr = rows
        assert row0 == 0 and gparts.shape[1] == rows
    else:
        tr = max(t for t in (512, 256, 128, 64, 32, 16, 8) if rows % t == 0 and row0 % t == 0
                 and (t * cols <= 256 * 1024 or t == 8))

    def body(w_ref, g_ref, m_ref, v_ref, go_ref, d_ref, mo_ref, vo_ref):
        gsum = g_ref[0].astype(F32)
        for p in range(1, npart):
            gsum = gsum + g_ref[p].astype(F32)
        go_ref[...] = gsum
        d_ref[...], mo_ref[...], vo_ref[...] = _adam_math(w_ref[...], gsum, m_ref[...], v_ref[...])

    spec = _row(tr, cols)
    return pl.pallas_call(
        body, grid=(rows // tr,),
        in_specs=[spec, pl.BlockSpec((npart, tr, cols), lambda i: (0, row0 // tr + i, 0)), spec, spec],
        out_specs=[spec] * 4, out_shape=[_sds((rows, cols))] * 4, compiler_params=_params("parallel"),
        name=name)(w, gparts, m, v)


INPUTS = ['x', 'c', 'ctx'] + WEIGHTS + ['loss_target'] + ['m_' + n for n in WEIGHTS] + ['v_' + n for n in WEIGHTS]
AXES = ("x", "y", "c")
LAYER_MATS = (('cv_w_in', 'cv_w_out'), ('pl_w_in', 'pl_w_grp', 'pl_w_out'),
              ('ml_w_in', 'ml_w_uq', 'ml_w_ukv', 'ml_w_out'), ('ch_w_in', 'ch_w_out'))
KINDS = ('grad_', 'delta_', 'new_m_', 'new_v_')


def _squeeze_layer(name, a):
    return a if name == 'norm_g' or a.ndim < 3 else a[0]


def _as2d(a):
    return a.reshape(-1, a.shape[-1])


class _Exchanges:
    def __init__(self, a, w):
        self.a, self.w, self.sums, self.quad = a, w, {}, {}

    def mats(self, layer):
        return [_as2d(self.a[n]).astype(BF16) for n in LAYER_MATS[layer]]

    def take_weights(self, layer, bufs):
        for n, buf in zip(LAYER_MATS[layer], bufs):
            self.w[n] = _squeeze_layer(n, _from_shards(buf, self.a[n].shape, SHARD_AXIS[n]))

    def side(self, tag):
        layer = int(tag[-1])
        return _gather_side(self.mats(layer)) if tag.startswith('fwd') else _quad_side(self.sums[layer])

    def done(self, tag, brought):
        layer = int(tag[-1])
        if tag.startswith('fwd'):
            self.take_weights(layer, brought)
        else:
            self.quad[layer] = brought

    def shard_major(self, n, gn):
        if gn.ndim == 3 and gn.shape[0] == N_DEV and gn.dtype == BF16:
            return gn
        whole = tuple(N_DEV * s if i == SHARD_AXIS[n] else s for i, s in enumerate(self.a[n].shape))
        return _to_shards(gn.reshape(whole), SHARD_AXIS[n]).reshape((N_DEV,) + _as2d(self.a[n]).shape)

    def grads_ready(self, layer, g):
        bufs = [self.shard_major(n, g[n]).astype(BF16) for n in LAYER_MATS[layer]]
        if layer == 0:
            bufs.append(_pack_shards([self.shard_major(n, g[n]).reshape(N_DEV, -1) for n in VECTOR_WEIGHTS]))
        bufs = [b.reshape((4, 2) + b.shape[1:]) for b in bufs]
        got = _swap_halves(bufs, "grads_swap_cores_%d" % layer)
        self.sums[layer] = _pair_add(bufs, got, "grads_add_cores_%d" % layer)
        if layer == 0:
            self.quad[0] = _run_side(_quad_side(self.sums[0]), "grads_exchange_chips_0")


def _train_step(a):
    x, c, ctx, tgt = a['x'], a['c'], a['ctx'], a['loss_target']
    d = x.shape[-1]
    nb = x.shape[0]
    dev = 4 * lax.axis_index("x") + 2 * lax.axis_index("y") + lax.axis_index("c")
    local_shape = {n: a[n].shape for n in WEIGHTS}

    w = {n: _squeeze_layer(n, a[n]) for n in WEIGHTS if SHARD_AXIS[n] is None}
    comm = _Exchanges(a, w)
    vec_names = ['c'] + VECTOR_WEIGHTS
    gathered = _gather_all(comm.mats(0) + [_pack([a[n] for n in vec_names], F32)], "gather_first")
    comm.take_weights(0, gathered[:-1])
    parts = dict(zip(vec_names, _unpack(gathered[-1], [a[n].shape for n in vec_names], lead=(N_DEV,))))
    for n in VECTOR_WEIGHTS:
        w[n] = _squeeze_layer(n, _from_shards(parts[n], local_shape[n], SHARD_AXIS[n]))
    c_all = parts['c'].reshape(N_DEV * nb, d)
    c_ctx = a['c_ctx'].reshape(1, d)

    w_mod = a['w_mod']
    nl, ncol = w_mod.shape[0], w_mod.shape[2]
    mod_rows = -(-(N_DEV * nb + 1) // 8) * 8
    c_rows = jnp.concatenate([c_all, c_ctx, jnp.zeros((mod_rows - N_DEV * nb - 1, d), F32)], axis=0)
    b_loc = lax.dynamic_slice(a['b_mod'], (0, dev * ncol), (nl, ncol))[:, None, :]
    mod_loc = _mod_fwd(c_rows, w_mod, b_loc, "mod_fwd")
    mod_all, = _gather_all([mod_loc.reshape(nl * mod_rows, ncol)], "gather_mods")
    mod_all = mod_all.reshape(N_DEV, nl, mod_rows, ncol).transpose(1, 2, 0, 3).reshape(nl, mod_rows, N_DEV * ncol)
    ctx_row = mod_all[:, N_DEV * nb:N_DEV * nb + 1]
    mods = jnp.concatenate([lax.dynamic_slice(mod_all, (0, dev * nb, 0), (nl, nb, 3 * d))] + [ctx_row] * nb, axis=1)

    loss_vec, grad_x, g, dmods, dnorm_g = _local_step(x, ctx, tgt, w, mods, comm)
    loss = lax.psum(jnp.sum(loss_vec), AXES)

    nseg = dmods.shape[1]
    dm_all, = _gather_all([dmods.reshape(nl * nseg, 3 * d)], "gather_dmods")
    dm_all = dm_all.reshape(N_DEV, nl, nseg, 3 * d).transpose(1, 0, 2, 3).reshape(nl, N_DEV * nseg, 3 * d)
    dcols = lax.dynamic_slice(dm_all, (0, 0, dev * ncol), (nl, N_DEV * nseg, ncol))
    c_rows_b = jnp.concatenate([c_all.reshape(N_DEV, nb, d), jnp.broadcast_to(c_ctx, (N_DEV, nb, d))], axis=1)
    g_w_mod, g_b_mod, g_c_ctx = _mod_bwd(c_rows_b.reshape(N_DEV * nseg, d), dcols, dm_all, w_mod, c_ctx, "mod_bwd")
    g['c_ctx'], g['norm_g'] = g_c_ctx, dnorm_g
    rep_all, = _gather_all([_pack([g[n] for n in REPLICATED], F32)], "gather_replicated_grads")

    out = {}

    def keep(names, res, shapes=None):
        for kind, val in zip(KINDS, res):
            if shapes is None:
                out[kind + names[0]] = val.reshape(local_shape[names[0]])
            else:
                for n, leaf in zip(names, _unpack(val, shapes)):
                    out[kind + n] = leaf

    def update_packed(names, gparts, tag):
        res = _adam(_pack([a[n] for n in names], F32), gparts, 0, _pack([a['m_' + n] for n in names], F32),
                    _pack([a['v_' + n] for n in names], F32), "adam_" + tag)
        keep(names, res, [local_shape[n] for n in names])

    for layer, names in enumerate(LAYER_MATS):
        for n, gparts in zip(names, comm.quad[layer]):
            keep([n], _adam(_as2d(a[n]), gparts, 0, _as2d(a['m_' + n]), _as2d(a['v_' + n]), "adam_" + n))
    update_packed(VECTOR_WEIGHTS, comm.quad[0][-1], "vectors")
    update_packed(REPLICATED, rep_all, "replicated")
    keep(['w_mod'], _adam(_as2d(w_mod), _as2d(g_w_mod)[None], 0, _as2d(a['m_w_mod']), _as2d(a['v_w_mod']),
                          "adam_w_mod"))
    keep(['b_mod'], _adam(a['b_mod'], g_b_mod.reshape((1,) + a['b_mod'].shape), 0, a['m_b_mod'], a['v_b_mod'],
                          "adam_b_mod"))
    return (loss, grad_x) + tuple(out[kind + n] for kind in KINDS for n in WEIGHTS)


def kernel(x, c, ctx, c_ctx, norm_g, w_mod, b_mod, cv_w_in, cv_dw, cv_db, cv_ln_g, cv_ln_b, cv_w_out, pl_w_in, pl_w_grp, pl_scale, pl_w_out, ml_w_in, ml_q_norm, ml_kv_norm, ml_w_uq, ml_w_ukv, ml_nope_norm, ml_rope_norm, ml_w_out, ch_w_in, ch_ln_g, ch_ln_b, ch_w_s, ch_b_s, ch_w_out, loss_target, m_c_ctx, m_norm_g, m_w_mod, m_b_mod, m_cv_w_in, m_cv_dw, m_cv_db, m_cv_ln_g, m_cv_ln_b, m_cv_w_out, m_pl_w_in, m_pl_w_grp, m_pl_scale, m_pl_w_out, m_ml_w_in, m_ml_q_norm, m_ml_kv_norm, m_ml_w_uq, m_ml_w_ukv, m_ml_nope_norm, m_ml_rope_norm, m_ml_w_out, m_ch_w_in, m_ch_ln_g, m_ch_ln_b, m_ch_w_s, m_ch_b_s, m_ch_w_out, v_c_ctx, v_norm_g, v_w_mod, v_b_mod, v_cv_w_in, v_cv_dw, v_cv_db, v_cv_ln_g, v_cv_ln_b, v_cv_w_out, v_pl_w_in, v_pl_w_grp, v_pl_scale, v_pl_w_out, v_ml_w_in, v_ml_q_norm, v_ml_kv_norm, v_ml_w_uq, v_ml_w_ukv, v_ml_nope_norm, v_ml_rope_norm, v_ml_w_out, v_ch_w_in, v_ch_ln_g, v_ch_ln_b, v_ch_w_s, v_ch_b_s, v_ch_w_out):
    return _train_step(dict(zip(INPUTS, (x, c, ctx, c_ctx, norm_g, w_mod, b_mod, cv_w_in, cv_dw, cv_db, cv_ln_g, cv_ln_b, cv_w_out, pl_w_in, pl_w_grp, pl_scale, pl_w_out, ml_w_in, ml_q_norm, ml_kv_norm, ml_w_uq, ml_w_ukv, ml_nope_norm, ml_rope_norm, ml_w_out, ch_w_in, ch_ln_g, ch_ln_b, ch_w_s, ch_b_s, ch_w_out, loss_target, m_c_ctx, m_norm_g, m_w_mod, m_b_mod, m_cv_w_in, m_cv_dw, m_cv_db, m_cv_ln_g, m_cv_ln_b, m_cv_w_out, m_pl_w_in, m_pl_w_grp, m_pl_scale, m_pl_w_out, m_ml_w_in, m_ml_q_norm, m_ml_kv_norm, m_ml_w_uq, m_ml_w_ukv, m_ml_nope_norm, m_ml_rope_norm, m_ml_w_out, m_ch_w_in, m_ch_ln_g, m_ch_ln_b, m_ch_w_s, m_ch_b_s, m_ch_w_out, v_c_ctx, v_norm_g, v_w_mod, v_b_mod, v_cv_w_in, v_cv_dw, v_cv_db, v_cv_ln_g, v_cv_ln_b, v_cv_w_out, v_pl_w_in, v_pl_w_grp, v_pl_scale, v_pl_w_out, v_ml_w_in, v_ml_q_norm, v_ml_kv_norm, v_ml_w_uq, v_ml_w_ukv, v_ml_nope_norm, v_ml_rope_norm, v_ml_w_out, v_ch_w_in, v_ch_ln_g, v_ch_ln_b, v_ch_w_s, v_ch_b_s, v_ch_w_out))))
```

```python
import functools

import jax
import jax.numpy as jnp
from jax import lax
from jax.experimental import pallas as pl
from jax.experimental.pallas import tpu as pltpu

F32 = jnp.float32
BF16 = jnp.bfloat16

N_DEV = 8
EPS = 1e-6
CONV_WIDTH = 31
CONV_PAD = 16
POOL_WINDOWS = (2, 4, 8, 16)
POOL_TAPS = 16
MLA_HEADS = 8
MLA_NOPE = 128
MLA_ROPE = 64
MLA_Q_RANK = 384
MLA_KV_RANK = 256
MLA_SCALE = (MLA_NOPE + MLA_ROPE) ** -0.5
ROPE_THETA = 10000.0
GRID_W = 64
HEAD_W = 256
ATTN_CHAINS = 2
CHUNK = 128
CHUNK_GROUPS = 8
CHUNK_TILES = (512, 256, 128)
ADAM_LR = 0.001
ADAM_B1 = 0.9
ADAM_B2 = 0.999
ADAM_EPS = 1e-08
ADAM_WD = 0.01
ADAM_STEP = 10
LANES = 128
VMEM_LIMIT = 56 * 1024 * 1024
PACK_COLS = 1024

WEIGHTS = ['c_ctx', 'norm_g', 'w_mod', 'b_mod', 'cv_w_in', 'cv_dw', 'cv_db', 'cv_ln_g', 'cv_ln_b', 'cv_w_out',
           'pl_w_in', 'pl_w_grp', 'pl_scale', 'pl_w_out', 'ml_w_in', 'ml_q_norm', 'ml_kv_norm', 'ml_w_uq',
           'ml_w_ukv', 'ml_nope_norm', 'ml_rope_norm', 'ml_w_out', 'ch_w_in', 'ch_ln_g', 'ch_ln_b', 'ch_w_s',
           'ch_b_s', 'ch_w_out']
SHARD_AXIS = {'c_ctx': None, 'norm_g': None, 'w_mod': 2, 'b_mod': None, 'cv_w_in': 2, 'cv_dw': 2, 'cv_db': None,
              'cv_ln_g': None, 'cv_ln_b': None, 'cv_w_out': 1, 'pl_w_in': 2, 'pl_w_grp': 2, 'pl_scale': 1,
              'pl_w_out': 1, 'ml_w_in': 2, 'ml_q_norm': 1, 'ml_kv_norm': 1, 'ml_w_uq': 2, 'ml_w_ukv': 2,
              'ml_nope_norm': None, 'ml_rope_norm': None, 'ml_w_out': 1, 'ch_w_in': 2, 'ch_ln_g': 1, 'ch_ln_b': 1,
              'ch_w_s': None, 'ch_b_s': None, 'ch_w_out': 1}
MATMUL_WEIGHTS = ['cv_w_in', 'cv_w_out', 'pl_w_in', 'pl_w_grp', 'pl_w_out', 'ml_w_in', 'ml_w_uq', 'ml_w_ukv',
                  'ml_w_out', 'ch_w_in', 'ch_w_out']
VECTOR_WEIGHTS = ['cv_dw', 'pl_scale', 'ml_q_norm', 'ml_kv_norm', 'ch_ln_g', 'ch_ln_b']
EXCHANGED = MATMUL_WEIGHTS[:1] + ['cv_dw'] + MATMUL_WEIGHTS[1:4] + ['pl_scale'] + MATMUL_WEIGHTS[4:6] + [
    'ml_q_norm', 'ml_kv_norm'] + MATMUL_WEIGHTS[6:10] + ['ch_ln_g', 'ch_ln_b', 'ch_w_out']
REPLICATED = ['c_ctx', 'norm_g', 'cv_db', 'cv_ln_g', 'cv_ln_b', 'ml_nope_norm', 'ml_rope_norm', 'ch_w_s', 'ch_b_s']


def _pick(n, cands):
    for c in cands:
        if n % c == 0:
            return c
    raise ValueError(f"no tile for {n} among {cands}")


def _params(*sem):
    return pltpu.CompilerParams(dimension_semantics=sem, vmem_limit_bytes=VMEM_LIMIT)


def _sig(x):
    return 1.0 / (1.0 + jnp.exp(-x))


def _silu(x):
    return x * _sig(x)


def _dsilu(x):
    s = _sig(x)
    return s * (1.0 + x * (1.0 - s))


def _rowsum(v):
    return jnp.sum(v, axis=0, keepdims=True)


def _dot(a, b):
    return jnp.dot(a.astype(BF16), b.astype(BF16), preferred_element_type=F32)


def _dot_nt(a, b):
    return lax.dot_general(a.astype(BF16), b.astype(BF16), (((1,), (1,)), ((), ())), preferred_element_type=F32)


def _dot_tn(a, b):
    return lax.dot_general(a.astype(BF16), b.astype(BF16), (((0,), (0,)), ((), ())), preferred_element_type=F32)


class _Segs:
    def __init__(self, lens, tm):
        self.lens, self.tm, self.n = tuple(lens), tm, len(lens)
        self.starts, s = [], 0
        for l in lens:
            assert l % tm == 0
            self.starts.append(s // tm)
            s += l
        self.rows, self.tiles = s, s // tm

    def seg(self, i):
        r = 0
        for st in self.starts[1:]:
            r = r + jnp.where(i >= st, 1, 0)
        return r

    def is_first(self, i):
        f = i == 0
        for st in self.starts[1:]:
            f = jnp.logical_or(f, i == st)
        return f

    def spec(self, cols):
        return pl.BlockSpec((None, 1, cols), lambda i: (self.seg(i), 0, 0))


def _row(tm, cols, cb=0):
    return pl.BlockSpec((tm, cols), lambda i: (i, cb))


def _const(shape):
    return pl.BlockSpec(shape, lambda *_: (0,) * len(shape))


def _sds(shape, dtype=F32):
    return jax.ShapeDtypeStruct(shape, dtype)


class _Side:
    def __init__(self, xs, out_shapes, n_remote, n_local, start, finish):
        self.xs, self.out_shapes, self.n_remote, self.n_local = list(xs), list(out_shapes), n_remote, n_local
        self.start, self.finish = start, finish


def _sem_shapes(n_remote, n_local):
    return [pltpu.SemaphoreType.DMA((n_remote,)), pltpu.SemaphoreType.DMA((n_remote,)),
            pltpu.SemaphoreType.DMA((max(n_local, 1),))]


def _pallas(body, args, *, grid, in_specs, out_specs, out_shape, sem, name, scratch_shapes=(), side=None, into=None):
    aliases = {}
    if into:
        inner, n_args = body, len(args)

        def body(*refs):
            inner(*refs[:n_args], *refs[n_args + len(into):])

        aliases = {n_args + k: o for k, o in enumerate(sorted(into))}
        args = tuple(args) + tuple(into[o] for o in sorted(into))
        in_specs = list(in_specs) + [pl.BlockSpec(memory_space=pl.ANY)] * len(into)
    if side is None:
        return pl.pallas_call(body, grid=grid, in_specs=in_specs, out_specs=out_specs, out_shape=out_shape,
                              scratch_shapes=list(scratch_shapes), input_output_aliases=aliases,
                              compiler_params=_params(*sem), name=name)(*args)
    multi = isinstance(out_shape, (list, tuple))
    out_specs, out_shape = (list(out_specs), list(out_shape)) if multi else ([out_specs], [out_shape])
    ni, no, ns, si, so = len(in_specs), len(out_specs), len(scratch_shapes), len(side.xs), len(side.out_shapes)
    hbm = pl.BlockSpec(memory_space=pltpu.HBM)

    def wrapped(*refs):
        ins, sins, refs = refs[:ni], refs[ni:ni + si], refs[ni + si:]
        outs, souts, refs = refs[:no], refs[no:no + so], refs[no + so:]
        scr, sems = refs[:ns], refs[ns:]
        ids = [pl.program_id(k) for k in range(len(grid))]
        first = functools.reduce(jnp.logical_and, [i == 0 for i in ids])
        last = functools.reduce(jnp.logical_and, [i == n - 1 for i, n in zip(ids, grid)])

        @pl.when(first)
        def _():
            side.start(sins, souts, *sems)

        body(*ins, *outs, *scr)

        @pl.when(last)
        def _():
            side.finish(sins, souts, *sems)

    res = pl.pallas_call(
        wrapped, grid=grid, in_specs=list(in_specs) + [hbm] * si, out_specs=out_specs + [hbm] * so,
        out_shape=out_shape + side.out_shapes,
        scratch_shapes=list(scratch_shapes) + _sem_shapes(side.n_remote, side.n_local), input_output_aliases=aliases,
        compiler_params=pltpu.CompilerParams(dimension_semantics=("arbitrary",) * len(grid),
                                             vmem_limit_bytes=VMEM_LIMIT, has_side_effects=True),
        name=name)(*args, *side.xs)
    return (list(res[:no]) if multi else res[0]), list(res[no:])


N_TILES = (1024, 896, 768, 512, 384, 256, 128)
M_TILES = (1536, 1024, 768, 512, 256, 128)


def _mm(a, b, name, out_dtype=F32, rows=None, side=None, resid=None):
    m, k, n = rows or a.shape[0], a.shape[1], b.shape[1]
    tm, tn = _pick(m, M_TILES), _pick(n, N_TILES)
    if resid is not None:
        x, gt, segs = resid
        pieces = tm // segs.tm

        def body(a_ref, b_ref, x_ref, *rest):
            gt_refs, (o_ref, y_ref) = rest[:pieces], rest[pieces:]
            o = _dot(a_ref[...], b_ref[...])
            o_ref[...] = o
            for c in range(pieces):
                rows = slice(c * segs.tm, (c + 1) * segs.tm)
                y_ref[rows, :] = x_ref[rows, :] + gt_refs[c][...] * o[rows, :]

        tile = pl.BlockSpec((tm, tn), lambda j, i: (i, j))
        gt_specs = [pl.BlockSpec((None, 1, tn), lambda j, i, c=c: (segs.seg(i * pieces + c), 0, j))
                    for c in range(pieces)]
        return pl.pallas_call(
            body, grid=(n // tn, m // tm),
            in_specs=[pl.BlockSpec((tm, k), lambda j, i: (i, 0)), pl.BlockSpec((k, tn), lambda j, i: (0, j)), tile]
            + gt_specs, out_specs=[tile, tile], out_shape=[_sds((m, n)), _sds((m, n))],
            compiler_params=_params("parallel", "parallel"), name=name)(a, b, x, *([gt] * pieces))

    def body(a_ref, b_ref, o_ref):
        o_ref[...] = _dot(a_ref[...], b_ref[...]).astype(o_ref.dtype)

    return _pallas(
        body, (a, b), grid=(n // tn, m // tm),
        in_specs=[pl.BlockSpec((tm, k), lambda j, i: (i, 0)), pl.BlockSpec((k, tn), lambda j, i: (0, j))],
        out_specs=pl.BlockSpec((tm, tn), lambda j, i: (i, j)), out_shape=_sds((m, n), out_dtype),
        sem=("parallel", "parallel"), name=name, side=side)


def _mm_nt(a, b, name, out_dtype=F32, side=None):
    m, k, n = a.shape[0], a.shape[1], b.shape[0]
    tm, tn = _pick(m, (512, 256, 128)), _pick(n, N_TILES)

    def body(a_ref, b_ref, o_ref):
        o_ref[...] = _dot_nt(a_ref[...], b_ref[...]).astype(o_ref.dtype)

    return _pallas(
        body, (a, b), grid=(n // tn, m // tm),
        in_specs=[pl.BlockSpec((tm, k), lambda j, i: (i, 0)), pl.BlockSpec((tn, k), lambda j, i: (j, 0))],
        out_specs=pl.BlockSpec((tm, tn), lambda j, i: (i, j)), out_shape=_sds((m, n), out_dtype),
        sem=("parallel", "parallel"), name=name, side=side)


def _mm_nt_sum(pairs, name):
    m, n = pairs[0][0].shape[0], pairs[0][1].shape[0]
    tm, tn = _pick(m, (512, 256, 128)), _pick(n, N_TILES)
    tiles = [a.shape[0] // tm for a, _ in pairs]
    assert all(a.shape[0] % tm == 0 for a, _ in pairs)

    def body(*refs):
        o_ref, i = refs[-1], pl.program_id(1)
        acc = _dot_nt(refs[0][...], refs[1][...])
        for p in range(1, len(pairs)):
            acc = acc + jnp.where(i < tiles[p], _dot_nt(refs[2 * p][...], refs[2 * p + 1][...]), 0.0)
        o_ref[...] = acc

    in_specs = []
    for (a, b), nt in zip(pairs, tiles):
        in_specs += [pl.BlockSpec((tm, a.shape[1]), lambda j, i, nt=nt: (jnp.minimum(i, nt - 1), 0)),
                     pl.BlockSpec((tn, a.shape[1]), lambda j, i: (j, 0))]
    return pl.pallas_call(
        body, grid=(n // tn, m // tm), in_specs=in_specs, out_specs=pl.BlockSpec((tm, tn), lambda j, i: (i, j)),
        out_shape=_sds((m, n)), compiler_params=_params("parallel", "parallel"),
        name=name)(*[t for pair in pairs for t in pair])


def _mm_tn(a, b, name, rows=None, shards=None, side=None):
    t, k, n = rows or a.shape[0], a.shape[1], b.shape[1]
    tk, tt = _pick(k, N_TILES), _pick(t, M_TILES)
    if shards:
        width = n // shards
        per_tile = max(c for c in (8, 4, 2, 1) if shards % c == 0 and c * width <= N_TILES[0])
        tn = per_tile * width
    else:
        tn = _pick(n, N_TILES)
    assert tn % LANES == 0

    def body(a_ref, b_ref, o_ref, acc_ref):
        @pl.when(pl.program_id(2) == 0)
        def _():
            acc_ref[...] = jnp.zeros_like(acc_ref)

        acc_ref[...] += _dot_tn(a_ref[...], b_ref[...])

        @pl.when(pl.program_id(2) == pl.num_programs(2) - 1)
        def _():
            if shards:
                for c in range(per_tile):
                    o_ref[c] = acc_ref[:, c * width:(c + 1) * width].astype(o_ref.dtype)
            else:
                o_ref[...] = acc_ref[...]

    if shards:
        out_spec = pl.BlockSpec((per_tile, tk, width), lambda i, j, s: (j, i, 0))
        out_shape = _sds((shards, k, width), BF16)
    else:
        out_spec, out_shape = pl.BlockSpec((tk, tn), lambda i, j, s: (i, j)), _sds((k, n))
    return _pallas(
        body, (a, b), grid=(k // tk, n // tn, t // tt),
        in_specs=[pl.BlockSpec((tt, tk), lambda i, j, s: (s, i)), pl.BlockSpec((tt, tn), lambda i, j, s: (s, j))],
        out_specs=out_spec, out_shape=out_shape, scratch_shapes=[pltpu.VMEM((tk, tn), F32)],
        sem=("parallel", "parallel", "arbitrary"), name=name, side=side)


def _rms_mod_fwd(x, g, sc, sh, segs, name):
    d, tm = x.shape[1], segs.tm

    def body(x_ref, g_ref, sc_ref, sh_ref, h_ref):
        xf = x_ref[...]
        r = lax.rsqrt(jnp.mean(xf * xf, axis=-1, keepdims=True) + EPS)
        h_ref[...] = ((xf * r * g_ref[...]) * (1.0 + sc_ref[...]) + sh_ref[...]).astype(h_ref.dtype)

    return pl.pallas_call(
        body, grid=(segs.tiles,), in_specs=[_row(tm, d), _const((1, d)), segs.spec(d), segs.spec(d)],
        out_specs=_row(tm, d), out_shape=_sds((segs.rows, d), BF16), compiler_params=_params("parallel"),
        name=name)(x, g, sc, sh)


def _rms_mod_bwd(x, g, sc, sh, dh, dxr, segs, name):
    d, tm = x.shape[1], segs.tm
    dxr_tiles = dxr.shape[0] // tm

    def body(x_ref, g_ref, sc_ref, sh_ref, dh_ref, dxr_ref, dx_ref, dg_ref, dsc_ref, dsh_ref):
        i = pl.program_id(0)

        @pl.when(i == 0)
        def _():
            dg_ref[...] = jnp.zeros_like(dg_ref)

        @pl.when(segs.is_first(i))
        def _():
            dsc_ref[...] = jnp.zeros_like(dsc_ref)
            dsh_ref[...] = jnp.zeros_like(dsh_ref)

        xf, gg, dhf = x_ref[...], g_ref[...], dh_ref[...].astype(F32)
        dxr = jnp.where(i < dxr_tiles, dxr_ref[...], 0.0)
        r = lax.rsqrt(jnp.mean(xf * xf, axis=-1, keepdims=True) + EPS)
        xh = xf * r
        dsh_ref[...] += _rowsum(dhf)
        dsc_ref[...] += _rowsum(dhf * (xh * gg))
        du = dhf * (1.0 + sc_ref[...])
        dg_ref[...] += _rowsum(du * xh)
        dxh = du * gg
        dx_ref[...] = dxr + r * (dxh - xh * jnp.mean(dxh * xh, axis=-1, keepdims=True))

    return pl.pallas_call(
        body, grid=(segs.tiles,),
        in_specs=[_row(tm, d), _const((1, d)), segs.spec(d), segs.spec(d), _row(tm, d),
                  pl.BlockSpec((tm, d), lambda i: (jnp.minimum(i, dxr_tiles - 1), 0))],
        out_specs=[_row(tm, d), _const((1, d)), segs.spec(d), segs.spec(d)],
        out_shape=[_sds((segs.rows, d)), _sds((1, d)), _sds((segs.n, 1, d)), _sds((segs.n, 1, d))],
        compiler_params=_params("arbitrary"), name=name)(x, g, sc, sh, dh, dxr)


def _resid_fwd(x, o, gt, segs, name):
    d, tm = x.shape[1], segs.tm

    def body(x_ref, o_ref, gt_ref, y_ref):
        y_ref[...] = x_ref[...] + gt_ref[...] * o_ref[...]

    return pl.pallas_call(
        body, grid=(segs.tiles,), in_specs=[_row(tm, d), _row(tm, d), segs.spec(d)], out_specs=_row(tm, d),
        out_shape=_sds((segs.rows, d)), compiler_params=_params("parallel"), name=name)(x, o, gt)


def _resid_bwd(dxn, o, gt, segs, name):
    d, tm = o.shape[1], segs.tm

    def body(dxn_ref, o_ref, gt_ref, do_ref, dgt_ref):
        @pl.when(segs.is_first(pl.program_id(0)))
        def _():
            dgt_ref[...] = jnp.zeros_like(dgt_ref)

        dx = dxn_ref[...]
        do_ref[...] = (gt_ref[...] * dx).astype(do_ref.dtype)
        dgt_ref[...] += _rowsum(dx * o_ref[...])

    return pl.pallas_call(
        body, grid=(segs.tiles,), in_specs=[_row(tm, d), _row(tm, d), segs.spec(d)],
        out_specs=[_row(tm, d), segs.spec(d)], out_shape=[_sds((segs.rows, d), BF16), _sds((segs.n, 1, d))],
        compiler_params=_params("arbitrary"), name=name)(dxn, o, gt)


def _loss_head(y, tgt, tm, name):
    t, d = y.shape

    def body(y_ref, t_ref, l_ref, dy_ref):
        @pl.when(pl.program_id(0) == 0)
        def _():
            l_ref[...] = jnp.zeros_like(l_ref)

        e = y_ref[...] - t_ref[...]
        dy_ref[...] = e * (1.0 / d)
        l_ref[...] += _rowsum(e * e) * (0.5 / d)

    return pl.pallas_call(
        body, grid=(t // tm,), in_specs=[_row(tm, d), _row(tm, d)], out_specs=[_const((1, d)), _row(tm, d)],
        out_shape=[_sds((1, d)), _sds((t, d))], compiler_params=_params("arbitrary"), name=name)(y, tgt)


def _seq_spec(l, ce, row0, cb0=0):
    return pl.BlockSpec((l, ce), lambda j, s: (row0 // l + s, cb0 + j))


def _tap_sum(pad_ref, taps_ref, first_row, n_taps, l, ce, flip):
    out = []
    for r0 in range(0, l, CHUNK):
        rows = min(CHUNK, l - r0)
        acc = jnp.zeros((rows, ce), F32)
        for k in range(n_taps):
            kk = n_taps - 1 - k if flip else k
            acc = acc + pad_ref[pl.ds(first_row + r0 + k, rows), :] * taps_ref[kk:kk + 1, :]
        out.append(acc)
    return out


def _fill_pad(pad_ref, val, l, ce):
    pad_ref[pl.ds(0, CONV_PAD), :] = jnp.zeros((CONV_PAD, ce), F32)
    pad_ref[pl.ds(CONV_PAD + l, CONV_PAD), :] = jnp.zeros((CONV_PAD, ce), F32)
    pad_ref[pl.ds(CONV_PAD, l), :] = val


def _conv1_fwd(z, dw, db, nseq, l, row0, name, side=None, into=None):
    e = z.shape[1] // 3
    ce = LANES
    half = CONV_WIDTH // 2

    def body(a_ref, b_ref, dw_ref, db_ref, y_ref, pad_ref):
        _fill_pad(pad_ref, a_ref[...] * _sig(b_ref[...]), l, ce)
        pieces = _tap_sum(pad_ref, dw_ref, CONV_PAD - half, CONV_WIDTH, l, ce, False)
        for n, acc in enumerate(pieces):
            y_ref[pl.ds(n * CHUNK, acc.shape[0]), :] = acc + db_ref[...]

    return _pallas(
        body, (z, z, dw, db), grid=(e // ce, nseq),
        in_specs=[_seq_spec(l, ce, row0), _seq_spec(l, ce, row0, e // ce),
                  pl.BlockSpec((CONV_WIDTH, ce), lambda j, s: (0, j)), pl.BlockSpec((1, ce), lambda j, s: (0, j))],
        out_specs=_seq_spec(l, ce, row0), out_shape=_sds((z.shape[0], e)),
        scratch_shapes=[pltpu.VMEM((l + 2 * CONV_PAD, ce), F32)],
        sem=("parallel", "arbitrary"), name=name, side=side, into=None if into is None else {0: into})


def _conv1_bwd(dy2, z, dw, acc_dw, acc_db, nseq, l, row0, name, side=None, into=None):
    e = z.shape[1] // 3
    ce = LANES
    half = CONV_WIDTH // 2

    def body(dy_ref, a_ref, b_ref, dw_ref, adw_ref, adb_ref, da_ref, dbb_ref, ddw_ref, ddb_ref, ypad_ref, dpad_ref):
        @pl.when(pl.program_id(1) == 0)
        def _():
            ddw_ref[...] = adw_ref[...]
            ddb_ref[...] = adb_ref[...]

        a, sb = a_ref[...], _sig(b_ref[...])
        dy = dy_ref[...]
        _fill_pad(ypad_ref, a * sb, l, ce)
        _fill_pad(dpad_ref, dy, l, ce)
        ddb_ref[...] += _rowsum(dy)
        for k in range(CONV_WIDTH):
            ddw_ref[k:k + 1, :] += _rowsum(dy * ypad_ref[pl.ds(CONV_PAD - half + k, l), :])
        pieces = _tap_sum(dpad_ref, dw_ref, CONV_PAD - half, CONV_WIDTH, l, ce, True)
        for n, dy1 in enumerate(pieces):
            rows = pl.ds(n * CHUNK, dy1.shape[0])
            sbn = sb[n * CHUNK:n * CHUNK + dy1.shape[0], :]
            da_ref[rows, :] = (dy1 * sbn).astype(da_ref.dtype)
            dbb_ref[rows, :] = (dy1 * a[n * CHUNK:n * CHUNK + dy1.shape[0], :] * sbn * (1.0 - sbn)).astype(dbb_ref.dtype)

    cw = lambda j, s: (0, j)
    return _pallas(
        body, (dy2, z, z, dw, acc_dw, acc_db), grid=(e // ce, nseq),
        in_specs=[_seq_spec(l, ce, row0), _seq_spec(l, ce, row0), _seq_spec(l, ce, row0, e // ce),
                  pl.BlockSpec((CONV_WIDTH, ce), cw), pl.BlockSpec((CONV_WIDTH, ce), cw), pl.BlockSpec((1, ce), cw)],
        out_specs=[_seq_spec(l, ce, row0), _seq_spec(l, ce, row0),
                   pl.BlockSpec((CONV_WIDTH, ce), cw), pl.BlockSpec((1, ce), cw)],
        out_shape=[_sds((z.shape[0], e), BF16), _sds((z.shape[0], e), BF16), _sds((CONV_WIDTH, e)), _sds((1, e))],
        scratch_shapes=[pltpu.VMEM((l + 2 * CONV_PAD, ce), F32), pltpu.VMEM((l + 2 * CONV_PAD, ce), F32)],
        sem=("parallel", "arbitrary"), name=name, side=side,
        into=None if into is None else {0: into[0], 1: into[1]})


def _pool_tables(l, e):
    grp = e // len(POOL_WINDOWS)
    w = jnp.repeat(jnp.array(POOL_WINDOWS, jnp.int32), grp)[None, :]
    off = jnp.arange(POOL_TAPS, dtype=jnp.int32)[:, None] - POOL_TAPS // 2
    taps = jnp.logical_and(off >= -(w // 2), off < w - w // 2).astype(F32)
    t = jnp.arange(l, dtype=jnp.int32)[:, None]
    cnt = jnp.clip(t + (w - w // 2), 0, l) - jnp.clip(t - w // 2, 0, l)
    return taps, 1.0 / cnt.astype(F32)


def _pool1(v_src, taps, inv_cnt, nseq, l, row0, transpose, name, out_dtype, into=None):
    e = taps.shape[1]
    ce = LANES
    half = POOL_TAPS // 2

    def body(v_ref, taps_ref, ic_ref, o_ref, pad_ref):
        v = v_ref[...].astype(F32)
        if transpose:
            _fill_pad(pad_ref, v * ic_ref[...], l, ce)
            pieces = _tap_sum(pad_ref, taps_ref, CONV_PAD - half + 1, POOL_TAPS, l, ce, True)
        else:
            _fill_pad(pad_ref, v, l, ce)
            pieces = _tap_sum(pad_ref, taps_ref, CONV_PAD - half, POOL_TAPS, l, ce, False)
        for n, acc in enumerate(pieces):
            rows = pl.ds(n * CHUNK, acc.shape[0])
            vn = v[n * CHUNK:n * CHUNK + acc.shape[0], :]
            if transpose:
                o_ref[rows, :] = (acc - vn).astype(o_ref.dtype)
            else:
                o_ref[rows, :] = (acc * ic_ref[rows, :] - vn).astype(o_ref.dtype)

    return _pallas(
        body, (v_src, taps, inv_cnt), grid=(e // ce, nseq),
        in_specs=[_seq_spec(l, ce, row0), pl.BlockSpec((POOL_TAPS, ce), lambda j, s: (0, j)),
                  pl.BlockSpec((l, ce), lambda j, s: (0, j))],
        out_specs=_seq_spec(l, ce, row0), out_shape=_sds((v_src.shape[0], e), out_dtype),
        scratch_shapes=[pltpu.VMEM((l + 2 * CONV_PAD, ce), F32)],
        sem=("parallel", "arbitrary"), name=name, into=None if into is None else {0: into})


def _layernorm_parts(x, eps=EPS):
    mu = jnp.mean(x, axis=-1, keepdims=True)
    xc = x - mu
    r = lax.rsqrt(jnp.mean(xc * xc, axis=-1, keepdims=True) + eps)
    return xc * r, r


def _layernorm_bwd(dy, xh, r, g):
    dxh = dy * g
    return r * (dxh - jnp.mean(dxh, axis=-1, keepdims=True) - xh * jnp.mean(dxh * xh, axis=-1, keepdims=True))


def _conv2_fwd(y2, z, ln_g, ln_b, tm, name):
    t, e = y2.shape

    def body(y_ref, g_ref, lg_ref, lb_ref, o_ref):
        xh, _ = _layernorm_parts(y_ref[...])
        o_ref[...] = (_silu(xh * lg_ref[...] + lb_ref[...]) * _silu(g_ref[...])).astype(o_ref.dtype)

    return pl.pallas_call(
        body, grid=(t // tm,), in_specs=[_row(tm, e), _row(tm, e, 2), _const((1, e)), _const((1, e))],
        out_specs=_row(tm, e), out_shape=_sds((t, e), BF16), compiler_params=_params("parallel"),
        name=name)(y2, z, ln_g, ln_b)


def _conv2_bwd(dy4, y2, z, ln_g, ln_b, tm, name):
    t, e = y2.shape

    def body(dy_ref, y_ref, g_ref, lg_ref, lb_ref, dy2_ref, dg_ref, dlg_ref, dlb_ref):
        @pl.when(pl.program_id(0) == 0)
        def _():
            dlg_ref[...] = jnp.zeros_like(dlg_ref)
            dlb_ref[...] = jnp.zeros_like(dlb_ref)

        dy, gz = dy_ref[...], g_ref[...]
        xh, r = _layernorm_parts(y_ref[...])
        y3 = xh * lg_ref[...] + lb_ref[...]
        dg_ref[...] = (dy * _silu(y3) * _dsilu(gz)).astype(dg_ref.dtype)
        dy3 = dy * _silu(gz) * _dsilu(y3)
        dlg_ref[...] += _rowsum(dy3 * xh)
        dlb_ref[...] += _rowsum(dy3)
        dy2_ref[...] = _layernorm_bwd(dy3, xh, r, lg_ref[...])

    return pl.pallas_call(
        body, grid=(t // tm,),
        in_specs=[_row(tm, e), _row(tm, e), _row(tm, e, 2), _const((1, e)), _const((1, e))],
        out_specs=[_row(tm, e), _row(tm, e), _const((1, e)), _const((1, e))],
        out_shape=[_sds((t, e)), _sds((t, e), BF16), _sds((1, e)), _sds((1, e))],
        compiler_params=_params("arbitrary"), name=name)(dy4, y2, z, ln_g, ln_b)


def _pool2_fwd(pm, w_grp, scale, z, tm, name):
    t, e = pm.shape
    ng, gw = w_grp.shape[0], w_grp.shape[1]

    def body(pm_ref, w_ref, sc_ref, g_ref, o_ref):
        for k in range(ng):
            cols = slice(k * gw, (k + 1) * gw)
            y = _dot(pm_ref[:, cols], w_ref[k])
            o_ref[:, cols] = (y * sc_ref[:, cols] * _silu(g_ref[:, cols])).astype(o_ref.dtype)

    return pl.pallas_call(
        body, grid=(t // tm,), in_specs=[_row(tm, e), _const(w_grp.shape), _const((1, e)), _row(tm, e, 1)],
        out_specs=_row(tm, e), out_shape=_sds((t, e), BF16), compiler_params=_params("parallel"),
        name=name)(pm, w_grp, scale, z)


def _pool2_bwd(dy2, pm, w_grp, scale, z, tm, name):
    t, e = pm.shape
    ng, gw = w_grp.shape[0], w_grp.shape[1]

    def body(dy_ref, pm_ref, w_ref, sc_ref, g_ref, dpm_ref, dg_ref, dsc_ref, dw_ref):
        @pl.when(pl.program_id(0) == 0)
        def _():
            dsc_ref[...] = jnp.zeros_like(dsc_ref)
            dw_ref[...] = jnp.zeros_like(dw_ref)

        for k in range(ng):
            cols = slice(k * gw, (k + 1) * gw)
            dy, gz, sc, pmk = dy_ref[:, cols], g_ref[:, cols], sc_ref[:, cols], pm_ref[:, cols]
            y = _dot(pmk, w_ref[k])
            dg_ref[:, cols] = (dy * (y * sc) * _dsilu(gz)).astype(dg_ref.dtype)
            dys = dy * _silu(gz)
            dsc_ref[:, cols] += _rowsum(dys * y)
            dyk = dys * sc
            dpm_ref[:, cols] = _dot_nt(dyk, w_ref[k])
            dw_ref[k] += _dot_tn(pmk, dyk)

    return pl.pallas_call(
        body, grid=(t // tm,),
        in_specs=[_row(tm, e), _row(tm, e), _const(w_grp.shape), _const((1, e)), _row(tm, e, 1)],
        out_specs=[_row(tm, e), _row(tm, e), _const((1, e)), _const(w_grp.shape)],
        out_shape=[_sds((t, e)), _sds((t, e), BF16), _sds((1, e)), _sds(w_grp.shape)],
        compiler_params=_params("arbitrary"), name=name)(dy2, pm, w_grp, scale, z)


def _rms_f(x, g, n):
    r = lax.rsqrt(jnp.sum(x * x, axis=-1, keepdims=True) * (1.0 / n) + EPS)
    return x * r * g


def _rms_b(x, g, dy, n):
    r = lax.rsqrt(jnp.sum(x * x, axis=-1, keepdims=True) * (1.0 / n) + EPS)
    xh = x * r
    dxh = dy * g
    return r * (dxh - xh * (jnp.sum(dxh * xh, axis=-1, keepdims=True) * (1.0 / n))), dy * xh


def _swap16(x):
    lane = lax.broadcasted_iota(jnp.int32, x.shape, 1)
    return jnp.where(lane % 32 < 16, pltpu.roll(x, LANES - 16, 1), pltpu.roll(x, 16, 1))


def _rope(x, c, s):
    return x * c + _swap16(x) * s


def _rope_t(dy, c, s):
    return dy * c + _swap16(dy * s)


def _rope_tables(l, lc, nb):
    t = jnp.arange(l, dtype=jnp.int32)
    row_id, col_id = (t // GRID_W).astype(F32), (t % GRID_W).astype(F32)
    axis_dim = MLA_ROPE // 2
    freqs = ROPE_THETA ** (-jnp.arange(0, axis_dim, 2, dtype=F32) / axis_dim)
    ar, ac = row_id[:, None] * freqs, col_id[:, None] * freqs
    pad1, pad0 = jnp.ones((l, LANES - MLA_ROPE), F32), jnp.zeros((l, LANES - MLA_ROPE), F32)
    ctab = jnp.concatenate([jnp.cos(ar), jnp.cos(ar), jnp.cos(ac), jnp.cos(ac), pad1], axis=1)
    stab = jnp.concatenate([-jnp.sin(ar), jnp.sin(ar), -jnp.sin(ac), jnp.sin(ac), pad0], axis=1)
    ctab = jnp.concatenate([jnp.tile(ctab, (nb, 1)), jnp.ones((nb * lc, LANES), F32)], axis=0)
    stab = jnp.concatenate([jnp.tile(stab, (nb, 1)), jnp.zeros((nb * lc, LANES), F32)], axis=0)
    return ctab, stab


def _kv_pre_fwd(zkv, kv_norm, rope_g, ctab, stab, tm, name):
    t = zkv.shape[0]

    def body(z_ref, gk_ref, gr_ref, c_ref, s_ref, ck_ref, kr_ref):
        ck_ref[...] = _rms_f(z_ref[:, :MLA_KV_RANK], gk_ref[...], MLA_KV_RANK).astype(ck_ref.dtype)
        kr = _rms_f(z_ref[:, MLA_KV_RANK:], gr_ref[...], MLA_ROPE)
        kr_ref[...] = _rope(kr, c_ref[...], s_ref[...]).astype(kr_ref.dtype)

    w = MLA_KV_RANK + LANES
    return pl.pallas_call(
        body, grid=(t // tm,),
        in_specs=[_row(tm, w), _const((1, MLA_KV_RANK)), _const((1, LANES)), _row(tm, LANES), _row(tm, LANES)],
        out_specs=[_row(tm, MLA_KV_RANK), _row(tm, LANES)],
        out_shape=[_sds((t, MLA_KV_RANK), BF16), _sds((t, LANES), BF16)],
        compiler_params=_params("parallel"), name=name)(zkv, kv_norm, rope_g, ctab, stab)


def _kv_pre_bwd(dck, dkr, zkv, kv_norm, rope_g, ctab, stab, tm, name):
    t = zkv.shape[0]
    w = MLA_KV_RANK + LANES

    def body(dck_ref, dkr_ref, z_ref, gk_ref, gr_ref, c_ref, s_ref, dz_ref, dgk_ref, dgr_ref):
        @pl.when(pl.program_id(0) == 0)
        def _():
            dgk_ref[...] = jnp.zeros_like(dgk_ref)
            dgr_ref[...] = jnp.zeros_like(dgr_ref)

        dx, dg = _rms_b(z_ref[:, :MLA_KV_RANK], gk_ref[...], dck_ref[...], MLA_KV_RANK)
        dz_ref[:, :MLA_KV_RANK] = dx.astype(dz_ref.dtype)
        dgk_ref[...] += _rowsum(dg)
        dy = _rope_t(dkr_ref[...], c_ref[...], s_ref[...])
        dx, dg = _rms_b(z_ref[:, MLA_KV_RANK:], gr_ref[...], dy, MLA_ROPE)
        dz_ref[:, MLA_KV_RANK:] = dx.astype(dz_ref.dtype)
        dgr_ref[...] += _rowsum(dg)

    return pl.pallas_call(
        body, grid=(t // tm,),
        in_specs=[_row(tm, MLA_KV_RANK), _row(tm, LANES), _row(tm, w), _const((1, MLA_KV_RANK)), _const((1, LANES)),
                  _row(tm, LANES), _row(tm, LANES)],
        out_specs=[_row(tm, w), _const((1, MLA_KV_RANK)), _const((1, LANES))],
        out_shape=[_sds((t, w), BF16), _sds((1, MLA_KV_RANK)), _sds((1, LANES))],
        compiler_params=_params("arbitrary"), name=name)(dck, dkr, zkv, kv_norm, rope_g, ctab, stab)


def _q_pre_fwd(zq, q_norm, tm, name):
    t, w = zq.shape

    def body(z_ref, g_ref, o_ref):
        o_ref[...] = _rms_f(z_ref[...], g_ref[...], w).astype(o_ref.dtype)

    return pl.pallas_call(
        body, grid=(t // tm,), in_specs=[_row(tm, w), _const((1, w))], out_specs=_row(tm, w),
        out_shape=_sds((t, w), BF16), compiler_params=_params("parallel"), name=name)(zq, q_norm)


def _q_pre_bwd(dcq, zq, q_norm, tm, name):
    t, w = zq.shape

    def body(d_ref, z_ref, g_ref, dz_ref, dg_ref):
        @pl.when(pl.program_id(0) == 0)
        def _():
            dg_ref[...] = jnp.zeros_like(dg_ref)

        dx, dg = _rms_b(z_ref[...], g_ref[...], d_ref[...], w)
        dz_ref[...] = dx.astype(dz_ref.dtype)
        dg_ref[...] += _rowsum(dg)

    return pl.pallas_call(
        body, grid=(t // tm,), in_specs=[_row(tm, w), _row(tm, w), _const((1, w))],
        out_specs=[_row(tm, w), _const((1, w))], out_shape=[_sds((t, w), BF16), _sds((1, w))],
        compiler_params=_params("arbitrary"), name=name)(dcq, zq, q_norm)


def _q_post_fwd(q, nope_g, rope_g, ctab, stab, tm, name):
    t, w = q.shape

    def body(q_ref, gn_ref, gr_ref, c_ref, s_ref, o_ref):
        for h in range(MLA_HEADS):
            a = h * HEAD_W
            qn = _rms_f(q_ref[:, a:a + LANES], gn_ref[...], MLA_NOPE)
            o_ref[:, a:a + LANES] = (qn * MLA_SCALE).astype(o_ref.dtype)
            qr = _rms_f(q_ref[:, a + LANES:a + HEAD_W], gr_ref[...], MLA_ROPE)
            o_ref[:, a + LANES:a + HEAD_W] = (_rope(qr, c_ref[...], s_ref[...]) * MLA_SCALE).astype(o_ref.dtype)

    return pl.pallas_call(
        body, grid=(t // tm,),
        in_specs=[_row(tm, w), _const((1, LANES)), _const((1, LANES)), _row(tm, LANES), _row(tm, LANES)],
        out_specs=_row(tm, w), out_shape=_sds((t, w), BF16), compiler_params=_params("parallel"),
        name=name)(q, nope_g, rope_g, ctab, stab)


def _q_post_bwd(dqf, q, nope_g, rope_g, ctab, stab, tm, name):
    t, w = q.shape

    def body(d_ref, q_ref, gn_ref, gr_ref, c_ref, s_ref, dq_ref, dgn_ref, dgr_ref):
        @pl.when(pl.program_id(0) == 0)
        def _():
            dgn_ref[...] = jnp.zeros_like(dgn_ref)
            dgr_ref[...] = jnp.zeros_like(dgr_ref)

        for h in range(MLA_HEADS):
            a = h * HEAD_W
            dx, dg = _rms_b(q_ref[:, a:a + LANES], gn_ref[...], d_ref[:, a:a + LANES] * MLA_SCALE, MLA_NOPE)
            dq_ref[:, a:a + LANES] = dx.astype(dq_ref.dtype)
            dgn_ref[...] += _rowsum(dg)
            dy = _rope_t(d_ref[:, a + LANES:a + HEAD_W] * MLA_SCALE, c_ref[...], s_ref[...])
            dx, dg = _rms_b(q_ref[:, a + LANES:a + HEAD_W], gr_ref[...], dy, MLA_ROPE)
            dq_ref[:, a + LANES:a + HEAD_W] = dx.astype(dq_ref.dtype)
            dgr_ref[...] += _rowsum(dg)

    return pl.pallas_call(
        body, grid=(t // tm,),
        in_specs=[_row(tm, w), _row(tm, w), _const((1, LANES)), _const((1, LANES)), _row(tm, LANES), _row(tm, LANES)],
        out_specs=[_row(tm, w), _const((1, LANES)), _const((1, LANES))],
        out_shape=[_sds((t, w), BF16), _sds((1, LANES)), _sds((1, LANES))],
        compiler_params=_params("arbitrary"), name=name)(dqf, q, nope_g, rope_g, ctab, stab)


def _k_post_fwd(kv, krr, nope_g, tm, name):
    t, w = kv.shape

    def body(kv_ref, kr_ref, gn_ref, k_ref, v_ref):
        for h in range(MLA_HEADS):
            a = h * HEAD_W
            k_ref[:, a:a + LANES] = _rms_f(kv_ref[:, a:a + LANES], gn_ref[...], MLA_NOPE).astype(k_ref.dtype)
            k_ref[:, a + LANES:a + HEAD_W] = kr_ref[...]
            v_ref[:, h * LANES:(h + 1) * LANES] = kv_ref[:, a + LANES:a + HEAD_W].astype(v_ref.dtype)

    return pl.pallas_call(
        body, grid=(t // tm,), in_specs=[_row(tm, w), _row(tm, LANES), _const((1, LANES))],
        out_specs=[_row(tm, w), _row(tm, w // 2)], out_shape=[_sds((t, w), BF16), _sds((t, w // 2), BF16)],
        compiler_params=_params("parallel"), name=name)(kv, krr, nope_g)


def _k_post_bwd(dkl, dkc, dvl, dvc, kv, nope_g, tm, name):
    t, w = kv.shape
    nl = dkl.shape[0] // tm

    def body(dkl_ref, dkc_ref, dvl_ref, dvc_ref, kv_ref, gn_ref, dkv_ref, dkr_ref, dgn_ref):
        i = pl.program_id(0)

        @pl.when(i == 0)
        def _():
            dgn_ref[...] = jnp.zeros_like(dgn_ref)

        dkr = jnp.zeros(dkr_ref.shape, F32)
        for h in range(MLA_HEADS):
            a = h * HEAD_W
            dk = jnp.where(i < nl, dkl_ref[:, a:a + HEAD_W], dkc_ref[:, a:a + HEAD_W])
            dv = jnp.where(i < nl, dvl_ref[:, h * LANES:(h + 1) * LANES], dvc_ref[:, h * LANES:(h + 1) * LANES])
            dx, dg = _rms_b(kv_ref[:, a:a + LANES], gn_ref[...], dk[:, :LANES], MLA_NOPE)
            dkv_ref[:, a:a + LANES] = dx.astype(dkv_ref.dtype)
            dgn_ref[...] += _rowsum(dg)
            dkv_ref[:, a + LANES:a + HEAD_W] = dv.astype(dkv_ref.dtype)
            dkr = dkr + dk[:, LANES:]
        dkr_ref[...] = dkr

    lat = lambda cols: pl.BlockSpec((tm, cols), lambda i: (jnp.minimum(i, nl - 1), 0))
    ctx = lambda cols: pl.BlockSpec((tm, cols), lambda i: (jnp.maximum(i - nl, 0), 0))
    return pl.pallas_call(
        body, grid=(t // tm,),
        in_specs=[lat(w), ctx(w), lat(w // 2), ctx(w // 2), _row(tm, w), _const((1, LANES))],
        out_specs=[_row(tm, w), _row(tm, LANES), _const((1, LANES))],
        out_shape=[_sds((t, w), BF16), _sds((t, LANES)), _sds((1, LANES))],
        compiler_params=_params("arbitrary"), name=name)(dkl, dkc, dvl, dvc, kv, nope_g)


def _attn_specs(nb, l, lc, tq):
    nq = l // tq
    ctx0 = nb * l // lc
    q_spec = lambda w: pl.BlockSpec((tq, w), lambda b, h, i: (b * nq + i, h))
    lat = lambda w: pl.BlockSpec((l, w), lambda b, h, i: (b, h))
    ctx = lambda w: pl.BlockSpec((lc, w), lambda b, h, i: (ctx0 + b, h))
    return nq, q_spec, lat, ctx


def _attn_fwd(qf, kf, vf, nb, l, lc, name):
    tq = _pick(l, (1024, 512, 256, 128))
    nq, q_spec, lat, ctx = _attn_specs(nb, l, lc, tq)
    sub = min(tq // ATTN_CHAINS, 256)

    def body(q_ref, kl_ref, kc_ref, vl_ref, vc_ref, o_ref, lse_ref):
        for r in range(0, tq, sub):
            q = q_ref[r:r + sub, :]
            s1, s2 = _dot_nt(q, kl_ref[...]), _dot_nt(q, kc_ref[...])
            m = jnp.maximum(jnp.max(s1, axis=-1, keepdims=True), jnp.max(s2, axis=-1, keepdims=True))
            p1, p2 = jnp.exp(s1 - m), jnp.exp(s2 - m)
            den = jnp.sum(p1, axis=-1, keepdims=True) + jnp.sum(p2, axis=-1, keepdims=True)
            o_ref[r:r + sub, :] = (_dot(p1, vl_ref[...]) + _dot(p2, vc_ref[...])) / den
            lse_ref[r:r + sub, :] = jnp.broadcast_to(m + jnp.log(den), (sub, LANES))

    return pl.pallas_call(
        body, grid=(nb, MLA_HEADS, nq),
        in_specs=[q_spec(HEAD_W), lat(HEAD_W), ctx(HEAD_W), lat(LANES), ctx(LANES)],
        out_specs=[q_spec(LANES), q_spec(LANES)],
        out_shape=[_sds((nb * l, MLA_HEADS * LANES)), _sds((nb * l, MLA_HEADS * LANES))],
        compiler_params=_params("parallel", "parallel", "arbitrary"), name=name)(qf, kf, kf, vf, vf)


def _attn_bwd(do, o, lse, qf, kf, vf, nb, l, lc, name, side=None):
    tq = _pick(l, (1024, 512, 256, 128))
    nq, q_spec, lat, ctx = _attn_specs(nb, l, lc, tq)
    out_lat = lambda w: pl.BlockSpec((l, w), lambda b, h, i: (b, h))
    out_ctx = lambda w: pl.BlockSpec((lc, w), lambda b, h, i: (b, h))
    sub = tq // ATTN_CHAINS

    def body(do_ref, o_ref, lse_ref, q_ref, kl_ref, kc_ref, vl_ref, vc_ref, dq_ref, dkl_ref, dkc_ref, dvl_ref, dvc_ref):
        @pl.when(pl.program_id(2) == 0)
        def _():
            dkl_ref[...] = jnp.zeros_like(dkl_ref)
            dkc_ref[...] = jnp.zeros_like(dkc_ref)
            dvl_ref[...] = jnp.zeros_like(dvl_ref)
            dvc_ref[...] = jnp.zeros_like(dvc_ref)

        parts = []
        for r in range(0, tq, sub):
            q, dof = q_ref[r:r + sub, :], do_ref[r:r + sub, :]
            delta = jnp.sum(dof * o_ref[r:r + sub, :], axis=-1, keepdims=True)
            lse = lse_ref[r:r + sub, :1]
            dq, part = jnp.zeros((sub, HEAD_W), F32), []
            for k_ref, v_ref in ((kl_ref, vl_ref), (kc_ref, vc_ref)):
                p = jnp.exp(_dot_nt(q, k_ref[...]) - lse)
                ds = p * (_dot_nt(dof, v_ref[...]) - delta)
                dq = dq + _dot(ds, k_ref[...])
                part += [_dot_tn(ds, q), _dot_tn(p, dof)]
            dq_ref[r:r + sub, :] = dq
            parts.append(part)
        for n, ref in enumerate((dkl_ref, dvl_ref, dkc_ref, dvc_ref)):
            ref[...] += functools.reduce(lambda u, v: u + v, [part[n] for part in parts])

    kw, vw = MLA_HEADS * HEAD_W, MLA_HEADS * LANES
    return _pallas(
        body, (do, o, lse, qf, kf, kf, vf, vf), grid=(nb, MLA_HEADS, nq),
        in_specs=[q_spec(LANES), q_spec(LANES), q_spec(LANES), q_spec(HEAD_W), lat(HEAD_W), ctx(HEAD_W), lat(LANES),
                  ctx(LANES)],
        out_specs=[q_spec(HEAD_W), out_lat(HEAD_W), out_ctx(HEAD_W), out_lat(LANES), out_ctx(LANES)],
        out_shape=[_sds((nb * l, kw)), _sds((nb * l, kw)), _sds((nb * lc, kw)), _sds((nb * l, vw)),
                   _sds((nb * lc, vw))],
        sem=("parallel", "parallel", "arbitrary"), name=name, side=side)


def _gate_fwd(o, g, tm, name):
    t, e = o.shape

    def body(o_ref, g_ref, y_ref):
        y_ref[...] = (o_ref[...] * _silu(g_ref[...])).astype(y_ref.dtype)

    return pl.pallas_call(
        body, grid=(t // tm,), in_specs=[_row(tm, e), _row(tm, e)], out_specs=_row(tm, e),
        out_shape=_sds((t, e), BF16), compiler_params=_params("parallel"), name=name)(o, g)


def _gate_bwd(dy, o, g, tm, name):
    t, e = o.shape

    def body(dy_ref, o_ref, g_ref, do_ref, dg_ref):
        dy, gz = dy_ref[...], g_ref[...]
        do_ref[...] = dy * _silu(gz)
        dg_ref[...] = (dy * o_ref[...] * _dsilu(gz)).astype(dg_ref.dtype)

    return pl.pallas_call(
        body, grid=(t // tm,), in_specs=[_row(tm, e), _row(tm, e), _row(tm, e)],
        out_specs=[_row(tm, e), _row(tm, e)], out_shape=[_sds((t, e)), _sds((t, e), BF16)],
        compiler_params=_params("parallel"), name=name)(dy, o, g)


def _chunk_fwd(z, ln_g, ln_b, w_s, bs_full, name):
    t, e = z.shape[0], z.shape[1] // 3
    tm = _pick(t, CHUNK_TILES)

    def body(u_ref, v_ref, g_ref, lg_ref, lb_ref, w_ref, bs_ref, y_ref):
        for r0 in range(0, tm, CHUNK):
            rows = slice(r0, r0 + CHUNK)
            xh, _ = _layernorm_parts(v_ref[rows, :])
            vn = xh * lg_ref[...] + lb_ref[...]
            for k in range(CHUNK_GROUPS):
                cols = slice(k * LANES, (k + 1) * LANES)
                s = _dot(w_ref[k], vn[:, cols]) + bs_ref[:, cols]
                y_ref[rows, cols] = (u_ref[rows, cols] * s * _silu(g_ref[rows, cols])).astype(y_ref.dtype)

    return pl.pallas_call(
        body, grid=(t // tm,),
        in_specs=[_row(tm, e, 0), _row(tm, e, 1), _row(tm, e, 2), _const((1, e)), _const((1, e)),
                  _const(w_s.shape), _const((CHUNK, e))],
        out_specs=_row(tm, e), out_shape=_sds((t, e), BF16), compiler_params=_params("parallel"),
        name=name)(z, z, z, ln_g, ln_b, w_s, bs_full)


def _chunk_bwd(dy, z, ln_g, ln_b, w_s, bs_full, name):
    t, e = z.shape[0], z.shape[1] // 3
    tm = _pick(t, CHUNK_TILES)

    def body(dy_ref, u_ref, v_ref, g_ref, lg_ref, lb_ref, w_ref, bs_ref, dz_ref, dw_ref, dbs_ref, dlg_ref, dlb_ref,
             acc_ref):
        i = pl.program_id(0)

        @pl.when(i == 0)
        def _():
            dw_ref[...] = jnp.zeros_like(dw_ref)
            dlg_ref[...] = jnp.zeros_like(dlg_ref)
            dlb_ref[...] = jnp.zeros_like(dlb_ref)
            acc_ref[...] = jnp.zeros_like(acc_ref)

        per_chunk = []
        for r0 in range(0, tm, CHUNK):
            rows = slice(r0, r0 + CHUNK)
            xh, r = _layernorm_parts(v_ref[rows, :])
            vn = xh * lg_ref[...] + lb_ref[...]
            dvn, dss, dws = [], [], []
            for k in range(CHUNK_GROUPS):
                cols = slice(k * LANES, (k + 1) * LANES)
                dyk, u, gz = dy_ref[rows, cols], u_ref[rows, cols], g_ref[rows, cols]
                s = _dot(w_ref[k], vn[:, cols]) + bs_ref[:, cols]
                sg = _silu(gz)
                dz_ref[rows, cols] = (dyk * s * sg).astype(dz_ref.dtype)
                dz_ref[rows, 2 * e + k * LANES:2 * e + (k + 1) * LANES] = (dyk * u * s * _dsilu(gz)).astype(
                    dz_ref.dtype)
                ds = dyk * u * sg
                dss.append(ds)
                dws.append(_dot_nt(ds, vn[:, cols]))
                dvn.append(_dot_tn(w_ref[k], ds))
            dvn = jnp.concatenate(dvn, axis=1)
            dz_ref[rows, e:2 * e] = _layernorm_bwd(dvn, xh, r, lg_ref[...]).astype(dz_ref.dtype)
            per_chunk.append((dss, dws, _rowsum(dvn * xh), _rowsum(dvn)))
        total = lambda parts: functools.reduce(lambda a, b: a + b, parts)
        for k in range(CHUNK_GROUPS):
            acc_ref[:, k * LANES:(k + 1) * LANES] += total([c[0][k] for c in per_chunk])
            dw_ref[k] += total([c[1][k] for c in per_chunk])
        dlg_ref[...] += total([c[2] for c in per_chunk])
        dlb_ref[...] += total([c[3] for c in per_chunk])

        @pl.when(i == pl.num_programs(0) - 1)
        def _():
            lane = lax.broadcasted_iota(jnp.int32, dbs_ref.shape, 1)
            out = jnp.zeros(dbs_ref.shape, F32)
            for k in range(CHUNK_GROUPS):
                col = jnp.sum(acc_ref[:, k * LANES:(k + 1) * LANES], axis=1, keepdims=True)
                out = jnp.where(lane == k, col, out)
            dbs_ref[...] = out

    return pl.pallas_call(
        body, grid=(t // tm,),
        in_specs=[_row(tm, e), _row(tm, e, 0), _row(tm, e, 1), _row(tm, e, 2), _const((1, e)),
                  _const((1, e)), _const(w_s.shape), _const((CHUNK, e))],
        out_specs=[_row(tm, 3 * e), _const(w_s.shape), _const((CHUNK, CHUNK_GROUPS)), _const((1, e)),
                   _const((1, e))],
        out_shape=[_sds((t, 3 * e), BF16), _sds(w_s.shape), _sds((CHUNK, CHUNK_GROUPS)), _sds((1, e)), _sds((1, e))],
        scratch_shapes=[pltpu.VMEM((CHUNK, e), F32)],
        compiler_params=_params("arbitrary"), name=name)(dy, z, z, z, ln_g, ln_b, w_s, bs_full)


def _mod_rows(mods, layer, d, nseg):
    m = mods[layer, :nseg]
    return [m[:, None, k * d:(k + 1) * d] for k in range(3)]


def _local_step(x, ctx, tgt, w, mods, comm=None):
    nb, l, d = x.shape
    lc = ctx.shape[1]
    e = d
    tl, ta = nb * l, nb * (l + lc)
    tm = _pick(lc, (256, 128))
    segs_a, segs_l = _Segs((l,) * nb + (lc,) * nb, tm), _Segs((l,) * nb, tm)
    norm_g = w['norm_g']
    g = {}

    def carried(tag, fn, *args, **kw):
        if comm is None:
            return fn(*args, **kw)
        res, brought = fn(*args, side=comm.side(tag), **kw)
        comm.done(tag, brought)
        return res

    xa0 = jnp.concatenate([x.reshape(tl, d), ctx.reshape(nb * lc, d)], axis=0)

    sh0, sc0, gt0 = _mod_rows(mods, 0, d, 2 * nb)
    h0 = _rms_mod_fwd(xa0, norm_g[0:1], sc0, sh0, segs_a, "l0_norm")
    z0 = carried('fwd1', _mm, h0, w['cv_w_in'], "l0_in")
    y2_0 = carried('fwd2', _conv1_fwd, z0, w['cv_dw'], w['cv_db'], nb, l, 0, "l0_conv_lat")
    y2_0 = _conv1_fwd(z0, w['cv_dw'], w['cv_db'], nb, lc, tl, "l0_conv_ctx", into=y2_0)
    y4_0 = _conv2_fwd(y2_0, z0, w['cv_ln_g'], w['cv_ln_b'], tm, "l0_gate")
    o0, xa1 = _mm(y4_0, w['cv_w_out'], "l0_out", resid=(xa0, gt0, segs_a))

    sh1, sc1, gt1 = _mod_rows(mods, 1, d, 2 * nb)
    h1 = _rms_mod_fwd(xa1, norm_g[1:2], sc1, sh1, segs_a, "l1_norm")
    z1 = carried('fwd3', _mm, h1, w['pl_w_in'], "l1_in")
    taps_l, ic_l = _pool_tables(l, e)
    taps_c, ic_c = _pool_tables(lc, e)
    pm1 = _pool1(z1, taps_l, ic_l, nb, l, 0, False, "l1_pool_lat", BF16)
    pm1 = _pool1(z1, taps_c, ic_c, nb, lc, tl, False, "l1_pool_ctx", BF16, into=pm1)
    y2_1 = _pool2_fwd(pm1, w['pl_w_grp'], w['pl_scale'], z1, tm, "l1_group")
    o1, xa2 = _mm(y2_1, w['pl_w_out'], "l1_out", resid=(xa1, gt1, segs_a))

    sh2, sc2, gt2 = _mod_rows(mods, 2, d, 2 * nb)
    h2 = _rms_mod_fwd(xa2, norm_g[2:3], sc2, sh2, segs_a, "l2_norm")
    w_in = w['ml_w_in']
    kvc = MLA_KV_RANK + MLA_ROPE
    w_in_p = jnp.concatenate([w_in[:, :kvc], jnp.zeros((d, LANES - MLA_ROPE), w_in.dtype), w_in[:, kvc:]], axis=1)
    w_uq_p = jnp.pad(w['ml_w_uq'].reshape(MLA_Q_RANK, MLA_HEADS, MLA_NOPE + MLA_ROPE),
                     ((0, 0), (0, 0), (0, HEAD_W - MLA_NOPE - MLA_ROPE))).reshape(MLA_Q_RANK, MLA_HEADS * HEAD_W)
    rope_g = jnp.pad(w['ml_rope_norm'], ((0, 0), (0, LANES - MLA_ROPE)))
    nope_g = w['ml_nope_norm']
    ctab, stab = _rope_tables(l, lc, nb)
    kvw = MLA_KV_RANK + LANES
    w_kv, w_q, w_g = w_in_p[:, :kvw], w_in_p[:, kvw:kvw + MLA_Q_RANK], w_in_p[:, kvw + MLA_Q_RANK:]
    zkv = _mm(h2, w_kv, "l2_in_kv")
    zq, zg = _mm(h2, w_q, "l2_in_q", rows=tl), _mm(h2, w_g, "l2_in_g", rows=tl)
    ckvn, krr = _kv_pre_fwd(zkv, w['ml_kv_norm'], rope_g[1:2], ctab, stab, tm, "l2_kv_pre")
    cqn = _q_pre_fwd(zq, w['ml_q_norm'], tm, "l2_q_pre")
    q2 = _mm(cqn, w_uq_p, "l2_uq")
    kv2 = _mm(ckvn, w['ml_w_ukv'], "l2_ukv")
    qf = _q_post_fwd(q2, nope_g[0:1], rope_g[0:1], ctab, stab, tm, "l2_q_post")
    kf, vf = _k_post_fwd(kv2, krr, nope_g[1:2], tm, "l2_k_post")
    o_att, lse = _attn_fwd(qf, kf, vf, nb, l, lc, "l2_attn")
    og = _gate_fwd(o_att, zg, tm, "l2_gate")
    o2, x3 = _mm(og, w['ml_w_out'], "l2_out", resid=(xa2, gt2[:nb], segs_l))

    sh3, sc3, gt3 = _mod_rows(mods, 3, d, nb)
    h3 = _rms_mod_fwd(x3, norm_g[3:4], sc3, sh3, segs_l, "l3_norm")
    z3 = _mm(h3, w['ch_w_in'], "l3_in")
    bs_full = jnp.repeat(w['ch_b_s'], e // CHUNK_GROUPS, axis=1)
    y3 = _chunk_fwd(z3, w['ch_ln_g'], w['ch_ln_b'], w['ch_w_s'], bs_full, "l3_chunk")
    o3, x4 = _mm(y3, w['ch_w_out'], "l3_out", resid=(x3, gt3, segs_l))

    loss_vec, dx4 = _loss_head(x4, tgt.reshape(tl, d), tm, "loss")

    do3, dgt3 = _resid_bwd(dx4, o3, gt3, segs_l, "l3_resid_b")
    dy3 = _mm_nt(do3, w['ch_w_out'], "l3_out_bx")
    g['ch_w_out'] = _mm_tn(y3, do3, "l3_out_bw")
    dz3, g['ch_w_s'], g['ch_b_s'], g['ch_ln_g'], g['ch_ln_b'] = _chunk_bwd(
        dy3, z3, w['ch_ln_g'], w['ch_ln_b'], w['ch_w_s'], bs_full, "l3_chunk_b")
    dh3 = _mm_nt(dz3, w['ch_w_in'], "l3_in_bx")
    g['ch_w_in'] = _mm_tn(h3, dz3, "l3_in_bw", shards=N_DEV)
    dx3, dng3, dsc3, dsh3 = _rms_mod_bwd(x3, norm_g[3:4], sc3, sh3, dh3, dx4, segs_l, "l3_norm_b")
    if comm is not None:
        comm.grads_ready(3, g)

    do2, dgt2 = _resid_bwd(dx3, o2, gt2[:nb], segs_l, "l2_resid_b")
    dog = _mm_nt(do2, w['ml_w_out'], "l2_out_bx")
    g['ml_w_out'] = _mm_tn(og, do2, "l2_out_bw")
    d_att, dzg = _gate_bwd(dog, o_att, zg, tm, "l2_gate_b")
    dqf, dkl, dkc, dvl, dvc = carried('quad3', _attn_bwd, d_att, o_att, lse, qf, kf, vf, nb, l, lc, "l2_attn_b")
    dq2, dnope_q, drope_q = _q_post_bwd(dqf, q2, nope_g[0:1], rope_g[0:1], ctab, stab, tm, "l2_q_post_b")
    dkv2, dkrr, dnope_k = _k_post_bwd(dkl, dkc, dvl, dvc, kv2, nope_g[1:2], tm, "l2_k_post_b")
    dcqn = _mm_nt(dq2, w_uq_p, "l2_uq_bx")
    g_uq_p = _mm_tn(cqn, dq2, "l2_uq_bw")
    dckvn = _mm_nt(dkv2, w['ml_w_ukv'], "l2_ukv_bx")
    g['ml_w_ukv'] = _mm_tn(ckvn, dkv2, "l2_ukv_bw", shards=N_DEV)
    dzq, g['ml_q_norm'] = _q_pre_bwd(dcqn, zq, w['ml_q_norm'], tm, "l2_q_pre_b")
    dzkv, g['ml_kv_norm'], drope_k = _kv_pre_bwd(dckvn, dkrr, zkv, w['ml_kv_norm'], rope_g[1:2], ctab, stab, tm,
                                                  "l2_kv_pre_b")
    dh2 = _mm_nt_sum([(dzkv, w_kv), (dzq, w_q), (dzg, w_g)], "l2_in_bx")
    g['ml_w_in'] = jnp.concatenate([_mm_tn(h2, dzkv, "l2_in_kv_bw")[:, :kvc], _mm_tn(h2, dzq, "l2_in_q_bw", rows=tl),
                                    _mm_tn(h2, dzg, "l2_in_g_bw", rows=tl)], axis=1)
    g['ml_w_uq'] = g_uq_p.reshape(MLA_Q_RANK, MLA_HEADS, HEAD_W)[:, :, :MLA_NOPE + MLA_ROPE].reshape(
        MLA_Q_RANK, MLA_HEADS * (MLA_NOPE + MLA_ROPE))
    g['ml_nope_norm'] = jnp.concatenate([dnope_q, dnope_k], axis=0)
    g['ml_rope_norm'] = jnp.concatenate([drope_q, drope_k], axis=0)[:, :MLA_ROPE]
    dxa2, dng2, dsc2, dsh2 = _rms_mod_bwd(xa2, norm_g[2:3], sc2, sh2, dh2, dx3, segs_a, "l2_norm_b")
    if comm is not None:
        comm.grads_ready(2, g)

    do1, dgt1 = _resid_bwd(dxa2, o1, gt1, segs_a, "l1_resid_b")
    dy2_1 = _mm_nt(do1, w['pl_w_out'], "l1_out_bx")
    g['pl_w_out'] = _mm_tn(y2_1, do1, "l1_out_bw")
    dpm, dgz1, g['pl_scale'], g['pl_w_grp'] = _pool2_bwd(dy2_1, pm1, w['pl_w_grp'], w['pl_scale'], z1, tm,
                                                          "l1_group_b")
    dv1 = _pool1(dpm, taps_l, ic_l, nb, l, 0, True, "l1_pool_lat_b", BF16)
    dv1 = _pool1(dpm, taps_c, ic_c, nb, lc, tl, True, "l1_pool_ctx_b", BF16, into=dv1)
    dz1 = jnp.concatenate([dv1, dgz1], axis=1)
    dh1 = _mm_nt(dz1, w['pl_w_in'], "l1_in_bx")
    g['pl_w_in'] = carried('quad2', _mm_tn, h1, dz1, "l1_in_bw", shards=N_DEV)
    dxa1, dng1, dsc1, dsh1 = _rms_mod_bwd(xa1, norm_g[1:2], sc1, sh1, dh1, dxa2, segs_a, "l1_norm_b")

    do0, dgt0 = _resid_bwd(dxa1, o0, gt0, segs_a, "l0_resid_b")
    dy4 = _mm_nt(do0, w['cv_w_out'], "l0_out_bx")
    g['cv_w_out'] = _mm_tn(y4_0, do0, "l0_out_bw")
    if comm is not None:
        comm.grads_ready(1, g)
    dy2, dgz0, g['cv_ln_g'], g['cv_ln_b'] = _conv2_bwd(dy4, y2_0, z0, w['cv_ln_g'], w['cv_ln_b'], tm, "l0_gate_b")
    da_l, db_l, ddw, ddb = carried('quad1', _conv1_bwd, dy2, z0, w['cv_dw'], jnp.zeros((CONV_WIDTH, e), F32),
                                   jnp.zeros((1, e), F32), nb, l, 0, "l0_conv_lat_b")
    da, db_, g['cv_dw'], g['cv_db'] = _conv1_bwd(dy2, z0, w['cv_dw'], ddw, ddb, nb, lc, tl, "l0_conv_ctx_b",
                                                 into=(da_l, db_l))
    dz0 = jnp.concatenate([da, db_, dgz0], axis=1)
    g['cv_w_in'] = _mm_tn(h0, dz0, "l0_in_bw", shards=N_DEV)
    if comm is not None:
        comm.grads_ready(0, g)
    dh0 = carried('quad0', _mm_nt, dz0, w['cv_w_in'], "l0_in_bx")
    dxa0, dng0, dsc0, dsh0 = _rms_mod_bwd(xa0, norm_g[0:1], sc0, sh0, dh0, dxa1, segs_a, "l0_norm_b")

    def rows4(t):
        return jnp.pad(t[:, 0], ((0, 2 * nb - t.shape[0]), (0, 0)))

    dmods = jnp.stack([
        jnp.concatenate([rows4(dsh0), rows4(dsc0), rows4(dgt0)], axis=1),
        jnp.concatenate([rows4(dsh1), rows4(dsc1), rows4(dgt1)], axis=1),
        jnp.concatenate([rows4(dsh2), rows4(dsc2), rows4(dgt2)], axis=1),
        jnp.concatenate([rows4(dsh3), rows4(dsc3), rows4(dgt3)], axis=1)])
    dnorm_g = jnp.concatenate([dng0, dng1, dng2, dng3], axis=0)
    return loss_vec, dxa0[:tl].reshape(nb, l, d), g, dmods, dnorm_g


def _mesh_pos():
    return lax.axis_index("x"), lax.axis_index("y"), lax.axis_index("c")


def _remote(src, dst, send_sems, recv_sems, k, dev):
    return pltpu.make_async_remote_copy(src_ref=src, dst_ref=dst, send_sem=send_sems.at[k], recv_sem=recv_sems.at[k],
                                        device_id=dev, device_id_type=pl.DeviceIdType.MESH)


def _comm_call(body, xs, out_shapes, n_remote, n_local, name):
    hbm = pl.BlockSpec(memory_space=pltpu.HBM)
    return pl.pallas_call(
        body, in_specs=[hbm] * len(xs), out_specs=[hbm] * len(out_shapes), out_shape=out_shapes,
        scratch_shapes=_sem_shapes(n_remote, n_local),
        compiler_params=pltpu.CompilerParams(has_side_effects=True), name=name)(*xs)


def _run_side(side, name):
    n = len(side.xs)

    def body(*refs):
        side.start(refs[:n], refs[n:n + len(side.out_shapes)], *refs[n + len(side.out_shapes):])
        side.finish(refs[:n], refs[n:n + len(side.out_shapes)], *refs[n + len(side.out_shapes):])

    return _comm_call(body, side.xs, side.out_shapes, side.n_remote, side.n_local, name)


def _gather_side(xs):
    n = len(xs)

    def plan(x_refs, o_refs, send_sems, recv_sems, local_sems):
        x, y, c = _mesh_pos()
        me, sib = (x, y, c), (x, y, 1 - c)
        chips = [(1 - x, y), (x, 1 - y), (1 - x, 1 - y)]

        def slot(a, p):
            return o_refs[a].at[4 * p[0] + 2 * p[1] + p[2]]

        def copy(a, k, block, to, src=None):
            return _remote(slot(a, block) if src is None else src, slot(a, block), send_sems, recv_sems, 7 * a + k, to)

        mine = [pltpu.make_async_copy(x_refs[a], slot(a, me), local_sems.at[a]) for a in range(n)]
        first = []
        for a in range(n):
            first += [copy(a, 1 + j, me, chip + (c,), src=x_refs[a]) for j, chip in enumerate(chips)]
            first.append(copy(a, 0, me, sib, src=x_refs[a]))
        return me, sib, c, chips, copy, mine, first

    def start(x_refs, o_refs, send_sems, recv_sems, local_sems):
        _, _, _, _, _, mine, first = plan(x_refs, o_refs, send_sems, recv_sems, local_sems)
        for cp in mine + first:
            cp.start()

    def finish(x_refs, o_refs, send_sems, recv_sems, local_sems):
        me, sib, c, chips, copy, mine, first = plan(x_refs, o_refs, send_sems, recv_sems, local_sems)
        passed = []
        for j, chip in enumerate(chips):
            for a in range(n):
                copy(a, 1 + j, chip + (c,), me).wait_recv()
                passed.append(copy(a, 4 + j, chip + (c,), sib))
                passed[-1].start()
        for a in range(n):
            copy(a, 0, sib, me).wait_recv()
        for j, chip in enumerate(chips):
            for a in range(n):
                copy(a, 4 + j, chip + (1 - c,), me).wait_recv()
        for cp in first + passed:
            cp.wait_send()
        for cp in mine:
            cp.wait()

    return _Side(xs, [_sds((N_DEV,) + x.shape, x.dtype) for x in xs], 7 * n, n, start, finish)


def _gather_all(xs, name):
    return _run_side(_gather_side(xs), name)


def _swap_halves(xs, name):
    n = len(xs)

    def body(*refs):
        x_refs, o_refs, (send_sems, recv_sems, _) = refs[:n], refs[n:2 * n], refs[2 * n:]
        x, y, c = _mesh_pos()
        copies = [_remote(x_refs[a].at[q, 1 - c], o_refs[a].at[q], send_sems, recv_sems, 4 * a + q, (x, y, 1 - c))
                  for a in range(n) for q in range(4)]
        for cp in copies:
            cp.start()
        for cp in copies:
            cp.wait_recv()
        for cp in copies:
            cp.wait_send()

    return _comm_call(body, xs, [_sds((4,) + x.shape[2:], x.dtype) for x in xs], 4 * n, 0, name)


def _quad_side(xs):
    n = len(xs)

    def plan(x_refs, o_refs, send_sems, recv_sems, local_sems):
        x, y, c = _mesh_pos()
        q = 2 * x + y
        chips = [(1 - x, y), (x, 1 - y), (1 - x, 1 - y)]
        mine = [pltpu.make_async_copy(x_refs[a].at[q], o_refs[a].at[q], local_sems.at[a]) for a in range(n)]
        sends, arrivals = [], []
        for a in range(n):
            for j, chip in enumerate(chips):
                qj = 2 * chip[0] + chip[1]
                sends.append(_remote(x_refs[a].at[qj], o_refs[a].at[q], send_sems, recv_sems, 3 * a + j, chip + (c,)))
                arrivals.append(_remote(x_refs[a].at[qj], o_refs[a].at[qj], send_sems, recv_sems, 3 * a + j,
                                        chip + (c,)))
        return mine, sends, arrivals

    def start(*refs):
        mine, sends, _ = plan(*refs)
        for cp in mine + sends:
            cp.start()

    def finish(*refs):
        mine, sends, arrivals = plan(*refs)
        for cp in arrivals:
            cp.wait_recv()
        for cp in sends:
            cp.wait_send()
        for cp in mine:
            cp.wait()

    return _Side(xs, [_sds(x.shape, x.dtype) for x in xs], 3 * n, n, start, finish)


def _pair_add(xs, rs, name):
    n = len(xs)

    def body(*refs):
        c = lax.axis_index("c")
        for x_ref, r_ref, o_ref in zip(refs[:n], refs[n:2 * n], refs[2 * n:]):
            o_ref[...] = (x_ref[c].astype(F32) + r_ref[...].astype(F32)).astype(o_ref.dtype)

    slot = lambda x: pl.BlockSpec((None,) + x.shape[2:], lambda q: (q, 0, 0))
    return pl.pallas_call(
        body, grid=(4,),
        in_specs=[pl.BlockSpec((None, 2) + x.shape[2:], lambda q: (q, 0, 0, 0)) for x in xs] + [slot(x) for x in xs],
        out_specs=[slot(x) for x in xs], out_shape=[_sds((4,) + x.shape[2:], x.dtype) for x in xs],
        compiler_params=_params("parallel"), name=name)(*xs, *rs)


def _pack_rows(n):
    r = -(-n // PACK_COLS)
    return -(-r // 256) * 256 if r > 256 else -(-r // 16) * 16


def _pack(arrs, dtype):
    flat = jnp.concatenate([a.reshape(-1).astype(dtype) for a in arrs])
    rows = _pack_rows(flat.shape[0])
    return jnp.pad(flat, (0, rows * PACK_COLS - flat.shape[0])).reshape(rows, PACK_COLS)


def _pack_shards(arrs):
    flat = jnp.concatenate([a.astype(F32) for a in arrs], axis=1)
    rows = _pack_rows(flat.shape[1])
    return jnp.pad(flat, ((0, 0), (0, rows * PACK_COLS - flat.shape[1]))).reshape(N_DEV, rows, PACK_COLS)


def _unpack(packed, shapes, lead=()):
    flat = packed.reshape(tuple(lead) + (-1,))
    out, off = [], 0
    for s in shapes:
        n = 1
        for v in s:
            n *= v
        out.append(flat[..., off:off + n].reshape(tuple(lead) + tuple(s)))
        off += n
    return out


def _to_shards(full, ax):
    s = full.shape
    t = full.reshape(s[:ax] + (N_DEV, s[ax] // N_DEV) + s[ax + 1:])
    return jnp.moveaxis(t, ax, 0).reshape(N_DEV, -1)


def _from_shards(shards, local_shape, ax):
    t = jnp.moveaxis(shards.reshape((N_DEV,) + tuple(local_shape)), 0, ax)
    s = t.shape
    return t.reshape(s[:ax] + (s[ax] * s[ax + 1],) + s[ax + 2:])


def _mod_fwd(c_rows, w_mod, b_mod, name):
    nl, d, n = w_mod.shape
    r = c_rows.shape[0]

    def body(c_ref, w_ref, b_ref, o_ref):
        s = _silu(c_ref[...])
        for l in range(nl):
            o_ref[l] = _dot(s, w_ref[l]) + b_ref[l]

    return pl.pallas_call(body, out_shape=_sds((nl, r, n)),
                          compiler_params=pltpu.CompilerParams(vmem_limit_bytes=VMEM_LIMIT), name=name)(
        c_rows, w_mod, b_mod)


def _mod_bwd(c_rows, dcols, dall, w_mod, c_ctx, name):
    nl, d, n = w_mod.shape
    r = c_rows.shape[0]

    def body(c_ref, dc_ref, da_ref, w_ref, cc_ref, gw_ref, gb_ref, gc_ref):
        s = _silu(c_ref[...])
        ds = jnp.zeros((r, d), F32)
        for l in range(nl):
            gw_ref[l] = _dot_tn(s, dc_ref[l])
            gb_ref[l] = _rowsum(da_ref[l])
            ds = ds + _dot_nt(dc_ref[l], w_ref[l])
        row = lax.broadcasted_iota(jnp.int32, (r, d), 0)
        gc_ref[...] = _rowsum(jnp.where(row % 4 >= 2, ds, 0.0)) * _dsilu(cc_ref[...])

    return pl.pallas_call(body, out_shape=[_sds((nl, d, n)), _sds((nl, 1, 3 * d)), _sds((1, d))],
                          compiler_params=pltpu.CompilerParams(vmem_limit_bytes=VMEM_LIMIT), name=name)(
        c_rows, dcols, dall, w_mod, c_ctx)


def _adam_math(w, gsum, m, v):
    c1, c2 = 1.0 - ADAM_B1 ** ADAM_STEP, 1.0 - ADAM_B2 ** ADAM_STEP
    mn = ADAM_B1 * m + (1.0 - ADAM_B1) * gsum
    vn = ADAM_B2 * v + (1.0 - ADAM_B2) * (gsum * gsum)
    return -ADAM_LR * ((mn / c1) / (jnp.sqrt(vn / c2) + ADAM_EPS) + ADAM_WD * w), mn, vn


def _adam(w, gparts, row0, m, v, name):
    rows, cols = w.shape
    npart = gparts.shape[0]
    if rows % 8:
        tr = rows
        assert row0 == 0 and gparts.shape[1] == rows
    else:
        tr = max(t for t in (512, 256, 128, 64, 32, 16, 8) if rows % t == 0 and row0 % t == 0
                 and (t * cols <= 256 * 1024 or t == 8))

    def body(w_ref, g_ref, m_ref, v_ref, go_ref, d_ref, mo_ref, vo_ref):
        gsum = g_ref[0].astype(F32)
        for p in range(1, npart):
            gsum = gsum + g_ref[p].astype(F32)
        go_ref[...] = gsum
        d_ref[...], mo_ref[...], vo_ref[...] = _adam_math(w_ref[...], gsum, m_ref[...], v_ref[...])

    spec = _row(tr, cols)
    return pl.pallas_call(
        body, grid=(rows // tr,),
        in_specs=[spec, pl.BlockSpec((npart, tr, cols), lambda i: (0, row0 // tr + i, 0)), spec, spec],
        out_specs=[spec] * 4, out_shape=[_sds((rows, cols))] * 4, compiler_params=_params("parallel"),
        name=name)(w, gparts, m, v)


INPUTS = ['x', 'c', 'ctx'] + WEIGHTS + ['loss_target'] + ['m_' + n for n in WEIGHTS] + ['v_' + n for n in WEIGHTS]
AXES = ("x", "y", "c")
LAYER_MATS = (('cv_w_in', 'cv_w_out'), ('pl_w_in', 'pl_w_grp', 'pl_w_out'),
              ('ml_w_in', 'ml_w_uq', 'ml_w_ukv', 'ml_w_out'), ('ch_w_in', 'ch_w_out'))
GRAD_GROUPS = (('cv_w_in',), ('pl_w_in', 'pl_w_grp', 'pl_w_out', 'cv_w_out'), LAYER_MATS[2], LAYER_MATS[3])
KINDS = ('grad_', 'delta_', 'new_m_', 'new_v_')


def _squeeze_layer(name, a):
    return a if name == 'norm_g' or a.ndim < 3 else a[0]


def _as2d(a):
    return a.reshape(-1, a.shape[-1])


class _Exchanges:
    def __init__(self, a, w):
        self.a, self.w, self.sums, self.quad = a, w, {}, {}

    def mats(self, layer):
        return [_as2d(self.a[n]).astype(BF16) for n in LAYER_MATS[layer]]

    def take_weights(self, layer, bufs):
        for n, buf in zip(LAYER_MATS[layer], bufs):
            self.w[n] = _squeeze_layer(n, _from_shards(buf, self.a[n].shape, SHARD_AXIS[n]))

    def side(self, tag):
        layer = int(tag[-1])
        return _gather_side(self.mats(layer)) if tag.startswith('fwd') else _quad_side(self.sums[layer])

    def done(self, tag, brought):
        layer = int(tag[-1])
        if tag.startswith('fwd'):
            self.take_weights(layer, brought)
        else:
            self.quad[layer] = brought

    def shard_major(self, n, gn):
        if gn.ndim == 3 and gn.shape[0] == N_DEV and gn.dtype == BF16:
            return gn
        whole = tuple(N_DEV * s if i == SHARD_AXIS[n] else s for i, s in enumerate(self.a[n].shape))
        return _to_shards(gn.reshape(whole), SHARD_AXIS[n]).reshape((N_DEV,) + _as2d(self.a[n]).shape)

    def grads_ready(self, group, g):
        bufs = [self.shard_major(n, g[n]).astype(BF16) for n in GRAD_GROUPS[group]]
        if group == 0:
            bufs.append(_pack_shards([self.shard_major(n, g[n]).reshape(N_DEV, -1) for n in VECTOR_WEIGHTS]))
        bufs = [b.reshape((4, 2) + b.shape[1:]) for b in bufs]
        got = _swap_halves(bufs, "grads_swap_cores_%d" % group)
        self.sums[group] = _pair_add(bufs, got, "grads_add_cores_%d" % group)


def _train_step(a):
    x, c, ctx, tgt = a['x'], a['c'], a['ctx'], a['loss_target']
    d = x.shape[-1]
    nb = x.shape[0]
    dev = 4 * lax.axis_index("x") + 2 * lax.axis_index("y") + lax.axis_index("c")
    local_shape = {n: a[n].shape for n in WEIGHTS}

    w = {n: _squeeze_layer(n, a[n]) for n in WEIGHTS if SHARD_AXIS[n] is None}
    comm = _Exchanges(a, w)
    vec_names = ['c'] + VECTOR_WEIGHTS
    gathered = _gather_all(comm.mats(0) + [_pack([a[n] for n in vec_names], F32)], "gather_first")
    comm.take_weights(0, gathered[:-1])
    parts = dict(zip(vec_names, _unpack(gathered[-1], [a[n].shape for n in vec_names], lead=(N_DEV,))))
    for n in VECTOR_WEIGHTS:
        w[n] = _squeeze_layer(n, _from_shards(parts[n], local_shape[n], SHARD_AXIS[n]))
    c_all = parts['c'].reshape(N_DEV * nb, d)
    c_ctx = a['c_ctx'].reshape(1, d)

    w_mod = a['w_mod']
    nl, ncol = w_mod.shape[0], w_mod.shape[2]
    mod_rows = -(-(N_DEV * nb + 1) // 8) * 8
    c_rows = jnp.concatenate([c_all, c_ctx, jnp.zeros((mod_rows - N_DEV * nb - 1, d), F32)], axis=0)
    b_loc = lax.dynamic_slice(a['b_mod'], (0, dev * ncol), (nl, ncol))[:, None, :]
    mod_loc = _mod_fwd(c_rows, w_mod, b_loc, "mod_fwd")
    mod_all, = _gather_all([mod_loc.reshape(nl * mod_rows, ncol)], "gather_mods")
    mod_all = mod_all.reshape(N_DEV, nl, mod_rows, ncol).transpose(1, 2, 0, 3).reshape(nl, mod_rows, N_DEV * ncol)
    ctx_row = mod_all[:, N_DEV * nb:N_DEV * nb + 1]
    mods = jnp.concatenate([lax.dynamic_slice(mod_all, (0, dev * nb, 0), (nl, nb, 3 * d))] + [ctx_row] * nb, axis=1)

    loss_vec, grad_x, g, dmods, dnorm_g = _local_step(x, ctx, tgt, w, mods, comm)
    loss = lax.psum(jnp.sum(loss_vec), AXES)

    nseg = dmods.shape[1]
    dm_all, = _gather_all([dmods.reshape(nl * nseg, 3 * d)], "gather_dmods")
    dm_all = dm_all.reshape(N_DEV, nl, nseg, 3 * d).transpose(1, 0, 2, 3).reshape(nl, N_DEV * nseg, 3 * d)
    dcols = lax.dynamic_slice(dm_all, (0, 0, dev * ncol), (nl, N_DEV * nseg, ncol))
    c_rows_b = jnp.concatenate([c_all.reshape(N_DEV, nb, d), jnp.broadcast_to(c_ctx, (N_DEV, nb, d))], axis=1)
    g_w_mod, g_b_mod, g_c_ctx = _mod_bwd(c_rows_b.reshape(N_DEV * nseg, d), dcols, dm_all, w_mod, c_ctx, "mod_bwd")
    g['c_ctx'], g['norm_g'] = g_c_ctx, dnorm_g
    rep_all, = _gather_all([_pack([g[n] for n in REPLICATED], F32)], "gather_replicated_grads")

    out = {}

    def keep(names, res, shapes=None):
        for kind, val in zip(KINDS, res):
            if shapes is None:
                out[kind + names[0]] = val.reshape(local_shape[names[0]])
            else:
                for n, leaf in zip(names, _unpack(val, shapes)):
                    out[kind + n] = leaf

    def update_packed(names, gparts, tag):
        res = _adam(_pack([a[n] for n in names], F32), gparts, 0, _pack([a['m_' + n] for n in names], F32),
                    _pack([a['v_' + n] for n in names], F32), "adam_" + tag)
        keep(names, res, [local_shape[n] for n in names])

    for group, names in enumerate(GRAD_GROUPS):
        for n, gparts in zip(names, comm.quad[group]):
            keep([n], _adam(_as2d(a[n]), gparts, 0, _as2d(a['m_' + n]), _as2d(a['v_' + n]), "adam_" + n))
    update_packed(VECTOR_WEIGHTS, comm.quad[0][-1], "vectors")
    update_packed(REPLICATED, rep_all, "replicated")
    keep(['w_mod'], _adam(_as2d(w_mod), _as2d(g_w_mod)[None], 0, _as2d(a['m_w_mod']), _as2d(a['v_w_mod']),
                          "adam_w_mod"))
    keep(['b_mod'], _adam(a['b_mod'], g_b_mod.reshape((1,) + a['b_mod'].shape), 0, a['m_b_mod'], a['v_b_mod'],
                          "adam_b_mod"))
    return (loss, grad_x) + tuple(out[kind + n] for kind in KINDS for n in WEIGHTS)


def kernel(x, c, ctx, c_ctx, norm_g, w_mod, b_mod, cv_w_in, cv_dw, cv_db, cv_ln_g, cv_ln_b, cv_w_out, pl_w_in, pl_w_grp, pl_scale, pl_w_out, ml_w_in, ml_q_norm, ml_kv_norm, ml_w_uq, ml_w_ukv, ml_nope_norm, ml_rope_norm, ml_w_out, ch_w_in, ch_ln_g, ch_ln_b, ch_w_s, ch_b_s, ch_w_out, loss_target, m_c_ctx, m_norm_g, m_w_mod, m_b_mod, m_cv_w_in, m_cv_dw, m_cv_db, m_cv_ln_g, m_cv_ln_b, m_cv_w_out, m_pl_w_in, m_pl_w_grp, m_pl_scale, m_pl_w_out, m_ml_w_in, m_ml_q_norm, m_ml_kv_norm, m_ml_w_uq, m_ml_w_ukv, m_ml_nope_norm, m_ml_rope_norm, m_ml_w_out, m_ch_w_in, m_ch_ln_g, m_ch_ln_b, m_ch_w_s, m_ch_b_s, m_ch_w_out, v_c_ctx, v_norm_g, v_w_mod, v_b_mod, v_cv_w_in, v_cv_dw, v_cv_db, v_cv_ln_g, v_cv_ln_b, v_cv_w_out, v_pl_w_in, v_pl_w_grp, v_pl_scale, v_pl_w_out, v_ml_w_in, v_ml_q_norm, v_ml_kv_norm, v_ml_w_uq, v_ml_w_ukv, v_ml_nope_norm, v_ml_rope_norm, v_ml_w_out, v_ch_w_in, v_ch_ln_g, v_ch_ln_b, v_ch_w_s, v_ch_b_s, v_ch_w_out):
    return _train_step(dict(zip(INPUTS, (x, c, ctx, c_ctx, norm_g, w_mod, b_mod, cv_w_in, cv_dw, cv_db, cv_ln_g, cv_ln_b, cv_w_out, pl_w_in, pl_w_grp, pl_scale, pl_w_out, ml_w_in, ml_q_norm, ml_kv_norm, ml_w_uq, ml_w_ukv, ml_nope_norm, ml_rope_norm, ml_w_out, ch_w_in, ch_ln_g, ch_ln_b, ch_w_s, ch_b_s, ch_w_out, loss_target, m_c_ctx, m_norm_g, m_w_mod, m_b_mod, m_cv_w_in, m_cv_dw, m_cv_db, m_cv_ln_g, m_cv_ln_b, m_cv_w_out, m_pl_w_in, m_pl_w_grp, m_pl_scale, m_pl_w_out, m_ml_w_in, m_ml_q_norm, m_ml_kv_norm, m_ml_w_uq, m_ml_w_ukv, m_ml_nope_norm, m_ml_rope_norm, m_ml_w_out, m_ch_w_in, m_ch_ln_g, m_ch_ln_b, m_ch_w_s, m_ch_b_s, m_ch_w_out, v_c_ctx, v_norm_g, v_w_mod, v_b_mod, v_cv_w_in, v_cv_dw, v_cv_db, v_cv_ln_g, v_cv_ln_b, v_cv_w_out, v_pl_w_in, v_pl_w_grp, v_pl_scale, v_pl_w_out, v_ml_w_in, v_ml_q_norm, v_ml_kv_norm, v_ml_w_uq, v_ml_w_ukv, v_ml_nope_norm, v_ml_rope_norm, v_ml_w_out, v_ch_w_in, v_ch_ln_g, v_ch_ln_b, v_ch_w_s, v_ch_b_s, v_ch_w_out))))
```

```python
import functools

import jax
import jax.numpy as jnp
from jax import lax
from jax.experimental import pallas as pl
from jax.experimental.pallas import tpu as pltpu

F32 = jnp.float32
BF16 = jnp.bfloat16

N_DEV = 8
EPS = 1e-6
CONV_WIDTH = 31
CONV_PAD = 16
POOL_WINDOWS = (2, 4, 8, 16)
POOL_TAPS = 16
MLA_HEADS = 8
MLA_NOPE = 128
MLA_ROPE = 64
MLA_Q_RANK = 384
MLA_KV_RANK = 256
MLA_SCALE = (MLA_NOPE + MLA_ROPE) ** -0.5
ROPE_THETA = 10000.0
GRID_W = 64
HEAD_W = 256
ATTN_CHAINS = 2
CHUNK = 128
CHUNK_GROUPS = 8
CHUNK_TILES = (512, 256, 128)
ADAM_LR = 0.001
ADAM_B1 = 0.9
ADAM_B2 = 0.999
ADAM_EPS = 1e-08
ADAM_WD = 0.01
ADAM_STEP = 10
LANES = 128
VMEM_LIMIT = 56 * 1024 * 1024
PACK_COLS = 1024

WEIGHTS = ['c_ctx', 'norm_g', 'w_mod', 'b_mod', 'cv_w_in', 'cv_dw', 'cv_db', 'cv_ln_g', 'cv_ln_b', 'cv_w_out',
           'pl_w_in', 'pl_w_grp', 'pl_scale', 'pl_w_out', 'ml_w_in', 'ml_q_norm', 'ml_kv_norm', 'ml_w_uq',
           'ml_w_ukv', 'ml_nope_norm', 'ml_rope_norm', 'ml_w_out', 'ch_w_in', 'ch_ln_g', 'ch_ln_b', 'ch_w_s',
           'ch_b_s', 'ch_w_out']
SHARD_AXIS = {'c_ctx': None, 'norm_g': None, 'w_mod': 2, 'b_mod': None, 'cv_w_in': 2, 'cv_dw': 2, 'cv_db': None,
              'cv_ln_g': None, 'cv_ln_b': None, 'cv_w_out': 1, 'pl_w_in': 2, 'pl_w_grp': 2, 'pl_scale': 1,
              'pl_w_out': 1, 'ml_w_in': 2, 'ml_q_norm': 1, 'ml_kv_norm': 1, 'ml_w_uq': 2, 'ml_w_ukv': 2,
              'ml_nope_norm': None, 'ml_rope_norm': None, 'ml_w_out': 1, 'ch_w_in': 2, 'ch_ln_g': 1, 'ch_ln_b': 1,
              'ch_w_s': None, 'ch_b_s': None, 'ch_w_out': 1}
MATMUL_WEIGHTS = ['cv_w_in', 'cv_w_out', 'pl_w_in', 'pl_w_grp', 'pl_w_out', 'ml_w_in', 'ml_w_uq', 'ml_w_ukv',
                  'ml_w_out', 'ch_w_in', 'ch_w_out']
VECTOR_WEIGHTS = ['cv_dw', 'pl_scale', 'ml_q_norm', 'ml_kv_norm', 'ch_ln_g', 'ch_ln_b']
EXCHANGED = MATMUL_WEIGHTS[:1] + ['cv_dw'] + MATMUL_WEIGHTS[1:4] + ['pl_scale'] + MATMUL_WEIGHTS[4:6] + [
    'ml_q_norm', 'ml_kv_norm'] + MATMUL_WEIGHTS[6:10] + ['ch_ln_g', 'ch_ln_b', 'ch_w_out']
REPLICATED = ['c_ctx', 'norm_g', 'cv_db', 'cv_ln_g', 'cv_ln_b', 'ml_nope_norm', 'ml_rope_norm', 'ch_w_s', 'ch_b_s']


def _pick(n, cands):
    for c in cands:
        if n % c == 0:
            return c
    raise ValueError(f"no tile for {n} among {cands}")


def _params(*sem):
    return pltpu.CompilerParams(dimension_semantics=sem, vmem_limit_bytes=VMEM_LIMIT)


def _sig(x):
    return 1.0 / (1.0 + jnp.exp(-x))


def _silu(x):
    return x * _sig(x)


def _dsilu(x):
    s = _sig(x)
    return s * (1.0 + x * (1.0 - s))


def _rowsum(v):
    return jnp.sum(v, axis=0, keepdims=True)


def _dot(a, b):
    return jnp.dot(a.astype(BF16), b.astype(BF16), preferred_element_type=F32)


def _dot_nt(a, b):
    return lax.dot_general(a.astype(BF16), b.astype(BF16), (((1,), (1,)), ((), ())), preferred_element_type=F32)


def _dot_tn(a, b):
    return lax.dot_general(a.astype(BF16), b.astype(BF16), (((0,), (0,)), ((), ())), preferred_element_type=F32)


class _Segs:
    def __init__(self, lens, tm):
        self.lens, self.tm, self.n = tuple(lens), tm, len(lens)
        self.starts, s = [], 0
        for l in lens:
            assert l % tm == 0
            self.starts.append(s // tm)
            s += l
        self.rows, self.tiles = s, s // tm

    def seg(self, i):
        r = 0
        for st in self.starts[1:]:
            r = r + jnp.where(i >= st, 1, 0)
        return r

    def is_first(self, i):
        f = i == 0
        for st in self.starts[1:]:
            f = jnp.logical_or(f, i == st)
        return f

    def spec(self, cols):
        return pl.BlockSpec((None, 1, cols), lambda i: (self.seg(i), 0, 0))


def _row(tm, cols, cb=0):
    return pl.BlockSpec((tm, cols), lambda i: (i, cb))


def _const(shape):
    return pl.BlockSpec(shape, lambda *_: (0,) * len(shape))


def _sds(shape, dtype=F32):
    return jax.ShapeDtypeStruct(shape, dtype)


class _Side:
    def __init__(self, xs, out_shapes, n_remote, n_local, start, finish):
        self.xs, self.out_shapes, self.n_remote, self.n_local = list(xs), list(out_shapes), n_remote, n_local
        self.start, self.finish = start, finish


def _sem_shapes(n_remote, n_local):
    return [pltpu.SemaphoreType.DMA((n_remote,)), pltpu.SemaphoreType.DMA((n_remote,)),
            pltpu.SemaphoreType.DMA((max(n_local, 1),))]


def _pallas(body, args, *, grid, in_specs, out_specs, out_shape, sem, name, scratch_shapes=(), side=None, into=None):
    aliases = {}
    if into:
        inner, n_args = body, len(args)

        def body(*refs):
            inner(*refs[:n_args], *refs[n_args + len(into):])

        aliases = {n_args + k: o for k, o in enumerate(sorted(into))}
        args = tuple(args) + tuple(into[o] for o in sorted(into))
        in_specs = list(in_specs) + [pl.BlockSpec(memory_space=pl.ANY)] * len(into)
    if side is None:
        return pl.pallas_call(body, grid=grid, in_specs=in_specs, out_specs=out_specs, out_shape=out_shape,
                              scratch_shapes=list(scratch_shapes), input_output_aliases=aliases,
                              compiler_params=_params(*sem), name=name)(*args)
    multi = isinstance(out_shape, (list, tuple))
    out_specs, out_shape = (list(out_specs), list(out_shape)) if multi else ([out_specs], [out_shape])
    ni, no, ns, si, so = len(in_specs), len(out_specs), len(scratch_shapes), len(side.xs), len(side.out_shapes)
    hbm = pl.BlockSpec(memory_space=pltpu.HBM)

    def wrapped(*refs):
        ins, sins, refs = refs[:ni], refs[ni:ni + si], refs[ni + si:]
        outs, souts, refs = refs[:no], refs[no:no + so], refs[no + so:]
        scr, sems = refs[:ns], refs[ns:]
        ids = [pl.program_id(k) for k in range(len(grid))]
        first = functools.reduce(jnp.logical_and, [i == 0 for i in ids])
        last = functools.reduce(jnp.logical_and, [i == n - 1 for i, n in zip(ids, grid)])

        @pl.when(first)
        def _():
            side.start(sins, souts, *sems)

        body(*ins, *outs, *scr)

        @pl.when(last)
        def _():
            side.finish(sins, souts, *sems)

    res = pl.pallas_call(
        wrapped, grid=grid, in_specs=list(in_specs) + [hbm] * si, out_specs=out_specs + [hbm] * so,
        out_shape=out_shape + side.out_shapes,
        scratch_shapes=list(scratch_shapes) + _sem_shapes(side.n_remote, side.n_local), input_output_aliases=aliases,
        compiler_params=pltpu.CompilerParams(dimension_semantics=("arbitrary",) * len(grid),
                                             vmem_limit_bytes=VMEM_LIMIT, has_side_effects=True),
        name=name)(*args, *side.xs)
    return (list(res[:no]) if multi else res[0]), list(res[no:])


N_TILES = (1024, 896, 768, 512, 384, 256, 128)
M_TILES = (1536, 1024, 768, 512, 256, 128)


def _mm(a, b, name, out_dtype=F32, rows=None, side=None, resid=None):
    m, k, n = rows or a.shape[0], a.shape[1], b.shape[1]
    tm, tn = _pick(m, M_TILES), _pick(n, N_TILES)
    if resid is not None:
        x, gt, segs = resid
        pieces = tm // segs.tm

        def body(a_ref, b_ref, x_ref, *rest):
            gt_refs, (o_ref, y_ref) = rest[:pieces], rest[pieces:]
            o = _dot(a_ref[...], b_ref[...])
            o_ref[...] = o
            for c in range(pieces):
                rows = slice(c * segs.tm, (c + 1) * segs.tm)
                y_ref[rows, :] = x_ref[rows, :] + gt_refs[c][...] * o[rows, :]

        tile = pl.BlockSpec((tm, tn), lambda j, i: (i, j))
        gt_specs = [pl.BlockSpec((None, 1, tn), lambda j, i, c=c: (segs.seg(i * pieces + c), 0, j))
                    for c in range(pieces)]
        return pl.pallas_call(
            body, grid=(n // tn, m // tm),
            in_specs=[pl.BlockSpec((tm, k), lambda j, i: (i, 0)), pl.BlockSpec((k, tn), lambda j, i: (0, j)), tile]
            + gt_specs, out_specs=[tile, tile], out_shape=[_sds((m, n)), _sds((m, n))],
            compiler_params=_params("parallel", "parallel"), name=name)(a, b, x, *([gt] * pieces))

    def body(a_ref, b_ref, o_ref):
        o_ref[...] = _dot(a_ref[...], b_ref[...]).astype(o_ref.dtype)

    return _pallas(
        body, (a, b), grid=(n // tn, m // tm),
        in_specs=[pl.BlockSpec((tm, k), lambda j, i: (i, 0)), pl.BlockSpec((k, tn), lambda j, i: (0, j))],
        out_specs=pl.BlockSpec((tm, tn), lambda j, i: (i, j)), out_shape=_sds((m, n), out_dtype),
        sem=("parallel", "parallel"), name=name, side=side)


def _mm_nt(a, b, name, out_dtype=F32, side=None):
    m, k, n = a.shape[0], a.shape[1], b.shape[0]
    tm, tn = _pick(m, (512, 256, 128)), _pick(n, N_TILES)

    def body(a_ref, b_ref, o_ref):
        o_ref[...] = _dot_nt(a_ref[...], b_ref[...]).astype(o_ref.dtype)

    return _pallas(
        body, (a, b), grid=(n // tn, m // tm),
        in_specs=[pl.BlockSpec((tm, k), lambda j, i: (i, 0)), pl.BlockSpec((tn, k), lambda j, i: (j, 0))],
        out_specs=pl.BlockSpec((tm, tn), lambda j, i: (i, j)), out_shape=_sds((m, n), out_dtype),
        sem=("parallel", "parallel"), name=name, side=side)


def _mm_nt_sum(pairs, name):
    m, n = pairs[0][0].shape[0], pairs[0][1].shape[0]
    tm, tn = _pick(m, (512, 256, 128)), _pick(n, N_TILES)
    tiles = [a.shape[0] // tm for a, _ in pairs]
    assert all(a.shape[0] % tm == 0 for a, _ in pairs)

    def body(*refs):
        o_ref, i = refs[-1], pl.program_id(1)
        acc = _dot_nt(refs[0][...], refs[1][...])
        for p in range(1, len(pairs)):
            acc = acc + jnp.where(i < tiles[p], _dot_nt(refs[2 * p][...], refs[2 * p + 1][...]), 0.0)
        o_ref[...] = acc

    in_specs = []
    for (a, b), nt in zip(pairs, tiles):
        in_specs += [pl.BlockSpec((tm, a.shape[1]), lambda j, i, nt=nt: (jnp.minimum(i, nt - 1), 0)),
                     pl.BlockSpec((tn, a.shape[1]), lambda j, i: (j, 0))]
    return pl.pallas_call(
        body, grid=(n // tn, m // tm), in_specs=in_specs, out_specs=pl.BlockSpec((tm, tn), lambda j, i: (i, j)),
        out_shape=_sds((m, n)), compiler_params=_params("parallel", "parallel"),
        name=name)(*[t for pair in pairs for t in pair])


def _mm_tn(a, b, name, rows=None, shards=None, side=None):
    t, k, n = rows or a.shape[0], a.shape[1], b.shape[1]
    tk, tt = _pick(k, N_TILES), _pick(t, M_TILES)
    if shards:
        width = n // shards
        per_tile = max(c for c in (8, 4, 2, 1) if shards % c == 0 and c * width <= N_TILES[0])
        tn = per_tile * width
    else:
        tn = _pick(n, N_TILES)
    assert tn % LANES == 0

    def body(a_ref, b_ref, o_ref, acc_ref):
        @pl.when(pl.program_id(2) == 0)
        def _():
            acc_ref[...] = jnp.zeros_like(acc_ref)

        acc_ref[...] += _dot_tn(a_ref[...], b_ref[...])

        @pl.when(pl.program_id(2) == pl.num_programs(2) - 1)
        def _():
            if shards:
                for c in range(per_tile):
                    o_ref[c] = acc_ref[:, c * width:(c + 1) * width].astype(o_ref.dtype)
            else:
                o_ref[...] = acc_ref[...]

    if shards:
        out_spec = pl.BlockSpec((per_tile, tk, width), lambda i, j, s: (j, i, 0))
        out_shape = _sds((shards, k, width), BF16)
    else:
        out_spec, out_shape = pl.BlockSpec((tk, tn), lambda i, j, s: (i, j)), _sds((k, n))
    return _pallas(
        body, (a, b), grid=(k // tk, n // tn, t // tt),
        in_specs=[pl.BlockSpec((tt, tk), lambda i, j, s: (s, i)), pl.BlockSpec((tt, tn), lambda i, j, s: (s, j))],
        out_specs=out_spec, out_shape=out_shape, scratch_shapes=[pltpu.VMEM((tk, tn), F32)],
        sem=("parallel", "parallel", "arbitrary"), name=name, side=side)


def _rms_mod_fwd(x, g, sc, sh, segs, name):
    d, tm = x.shape[1], segs.tm

    def body(x_ref, g_ref, sc_ref, sh_ref, h_ref):
        xf = x_ref[...]
        r = lax.rsqrt(jnp.mean(xf * xf, axis=-1, keepdims=True) + EPS)
        h_ref[...] = ((xf * r * g_ref[...]) * (1.0 + sc_ref[...]) + sh_ref[...]).astype(h_ref.dtype)

    return pl.pallas_call(
        body, grid=(segs.tiles,), in_specs=[_row(tm, d), _const((1, d)), segs.spec(d), segs.spec(d)],
        out_specs=_row(tm, d), out_shape=_sds((segs.rows, d), BF16), compiler_params=_params("parallel"),
        name=name)(x, g, sc, sh)


def _rms_mod_bwd(x, g, sc, sh, dh, dxr, segs, name):
    d, tm = x.shape[1], segs.tm
    dxr_tiles = dxr.shape[0] // tm

    def body(x_ref, g_ref, sc_ref, sh_ref, dh_ref, dxr_ref, dx_ref, dg_ref, dsc_ref, dsh_ref):
        i = pl.program_id(0)

        @pl.when(i == 0)
        def _():
            dg_ref[...] = jnp.zeros_like(dg_ref)

        @pl.when(segs.is_first(i))
        def _():
            dsc_ref[...] = jnp.zeros_like(dsc_ref)
            dsh_ref[...] = jnp.zeros_like(dsh_ref)

        xf, gg, dhf = x_ref[...], g_ref[...], dh_ref[...].astype(F32)
        dxr = jnp.where(i < dxr_tiles, dxr_ref[...], 0.0)
        r = lax.rsqrt(jnp.mean(xf * xf, axis=-1, keepdims=True) + EPS)
        xh = xf * r
        dsh_ref[...] += _rowsum(dhf)
        dsc_ref[...] += _rowsum(dhf * (xh * gg))
        du = dhf * (1.0 + sc_ref[...])
        dg_ref[...] += _rowsum(du * xh)
        dxh = du * gg
        dx_ref[...] = dxr + r * (dxh - xh * jnp.mean(dxh * xh, axis=-1, keepdims=True))

    return pl.pallas_call(
        body, grid=(segs.tiles,),
        in_specs=[_row(tm, d), _const((1, d)), segs.spec(d), segs.spec(d), _row(tm, d),
                  pl.BlockSpec((tm, d), lambda i: (jnp.minimum(i, dxr_tiles - 1), 0))],
        out_specs=[_row(tm, d), _const((1, d)), segs.spec(d), segs.spec(d)],
        out_shape=[_sds((segs.rows, d)), _sds((1, d)), _sds((segs.n, 1, d)), _sds((segs.n, 1, d))],
        compiler_params=_params("arbitrary"), name=name)(x, g, sc, sh, dh, dxr)


def _resid_fwd(x, o, gt, segs, name):
    d, tm = x.shape[1], segs.tm

    def body(x_ref, o_ref, gt_ref, y_ref):
        y_ref[...] = x_ref[...] + gt_ref[...] * o_ref[...]

    return pl.pallas_call(
        body, grid=(segs.tiles,), in_specs=[_row(tm, d), _row(tm, d), segs.spec(d)], out_specs=_row(tm, d),
        out_shape=_sds((segs.rows, d)), compiler_params=_params("parallel"), name=name)(x, o, gt)


def _resid_bwd(dxn, o, gt, segs, name):
    d, tm = o.shape[1], segs.tm

    def body(dxn_ref, o_ref, gt_ref, do_ref, dgt_ref):
        @pl.when(segs.is_first(pl.program_id(0)))
        def _():
            dgt_ref[...] = jnp.zeros_like(dgt_ref)

        dx = dxn_ref[...]
        do_ref[...] = (gt_ref[...] * dx).astype(do_ref.dtype)
        dgt_ref[...] += _rowsum(dx * o_ref[...])

    return pl.pallas_call(
        body, grid=(segs.tiles,), in_specs=[_row(tm, d), _row(tm, d), segs.spec(d)],
        out_specs=[_row(tm, d), segs.spec(d)], out_shape=[_sds((segs.rows, d), BF16), _sds((segs.n, 1, d))],
        compiler_params=_params("arbitrary"), name=name)(dxn, o, gt)


def _loss_head(y, tgt, tm, name):
    t, d = y.shape

    def body(y_ref, t_ref, l_ref, dy_ref):
        @pl.when(pl.program_id(0) == 0)
        def _():
            l_ref[...] = jnp.zeros_like(l_ref)

        e = y_ref[...] - t_ref[...]
        dy_ref[...] = e * (1.0 / d)
        l_ref[...] += _rowsum(e * e) * (0.5 / d)

    return pl.pallas_call(
        body, grid=(t // tm,), in_specs=[_row(tm, d), _row(tm, d)], out_specs=[_const((1, d)), _row(tm, d)],
        out_shape=[_sds((1, d)), _sds((t, d))], compiler_params=_params("arbitrary"), name=name)(y, tgt)


def _seq_spec(l, ce, row0, cb0=0):
    return pl.BlockSpec((l, ce), lambda j, s: (row0 // l + s, cb0 + j))


def _tap_sum(pad_ref, taps_ref, first_row, n_taps, l, ce, flip):
    out = []
    for r0 in range(0, l, CHUNK):
        rows = min(CHUNK, l - r0)
        acc = jnp.zeros((rows, ce), F32)
        for k in range(n_taps):
            kk = n_taps - 1 - k if flip else k
            acc = acc + pad_ref[pl.ds(first_row + r0 + k, rows), :] * taps_ref[kk:kk + 1, :]
        out.append(acc)
    return out


def _fill_pad(pad_ref, val, l, ce):
    pad_ref[pl.ds(0, CONV_PAD), :] = jnp.zeros((CONV_PAD, ce), F32)
    pad_ref[pl.ds(CONV_PAD + l, CONV_PAD), :] = jnp.zeros((CONV_PAD, ce), F32)
    pad_ref[pl.ds(CONV_PAD, l), :] = val


def _conv1_fwd(z, dw, db, nseq, l, row0, name, side=None, into=None):
    e = z.shape[1] // 3
    ce = LANES
    half = CONV_WIDTH // 2

    def body(a_ref, b_ref, dw_ref, db_ref, y_ref, pad_ref):
        _fill_pad(pad_ref, a_ref[...] * _sig(b_ref[...]), l, ce)
        pieces = _tap_sum(pad_ref, dw_ref, CONV_PAD - half, CONV_WIDTH, l, ce, False)
        for n, acc in enumerate(pieces):
            y_ref[pl.ds(n * CHUNK, acc.shape[0]), :] = acc + db_ref[...]

    return _pallas(
        body, (z, z, dw, db), grid=(e // ce, nseq),
        in_specs=[_seq_spec(l, ce, row0), _seq_spec(l, ce, row0, e // ce),
                  pl.BlockSpec((CONV_WIDTH, ce), lambda j, s: (0, j)), pl.BlockSpec((1, ce), lambda j, s: (0, j))],
        out_specs=_seq_spec(l, ce, row0), out_shape=_sds((z.shape[0], e)),
        scratch_shapes=[pltpu.VMEM((l + 2 * CONV_PAD, ce), F32)],
        sem=("parallel", "arbitrary"), name=name, side=side, into=None if into is None else {0: into})


def _conv1_bwd(dy2, z, dw, acc_dw, acc_db, nseq, l, row0, name, side=None, into=None):
    e = z.shape[1] // 3
    ce = LANES
    half = CONV_WIDTH // 2

    def body(dy_ref, a_ref, b_ref, dw_ref, adw_ref, adb_ref, da_ref, dbb_ref, ddw_ref, ddb_ref, ypad_ref, dpad_ref):
        @pl.when(pl.program_id(1) == 0)
        def _():
            ddw_ref[...] = adw_ref[...]
            ddb_ref[...] = adb_ref[...]

        a, sb = a_ref[...], _sig(b_ref[...])
        dy = dy_ref[...]
        _fill_pad(ypad_ref, a * sb, l, ce)
        _fill_pad(dpad_ref, dy, l, ce)
        ddb_ref[...] += _rowsum(dy)
        for k in range(CONV_WIDTH):
            ddw_ref[k:k + 1, :] += _rowsum(dy * ypad_ref[pl.ds(CONV_PAD - half + k, l), :])
        pieces = _tap_sum(dpad_ref, dw_ref, CONV_PAD - half, CONV_WIDTH, l, ce, True)
        for n, dy1 in enumerate(pieces):
            rows = pl.ds(n * CHUNK, dy1.shape[0])
            sbn = sb[n * CHUNK:n * CHUNK + dy1.shape[0], :]
            da_ref[rows, :] = (dy1 * sbn).astype(da_ref.dtype)
            dbb_ref[rows, :] = (dy1 * a[n * CHUNK:n * CHUNK + dy1.shape[0], :] * sbn * (1.0 - sbn)).astype(dbb_ref.dtype)

    cw = lambda j, s: (0, j)
    return _pallas(
        body, (dy2, z, z, dw, acc_dw, acc_db), grid=(e // ce, nseq),
        in_specs=[_seq_spec(l, ce, row0), _seq_spec(l, ce, row0), _seq_spec(l, ce, row0, e // ce),
                  pl.BlockSpec((CONV_WIDTH, ce), cw), pl.BlockSpec((CONV_WIDTH, ce), cw), pl.BlockSpec((1, ce), cw)],
        out_specs=[_seq_spec(l, ce, row0), _seq_spec(l, ce, row0),
                   pl.BlockSpec((CONV_WIDTH, ce), cw), pl.BlockSpec((1, ce), cw)],
        out_shape=[_sds((z.shape[0], e), BF16), _sds((z.shape[0], e), BF16), _sds((CONV_WIDTH, e)), _sds((1, e))],
        scratch_shapes=[pltpu.VMEM((l + 2 * CONV_PAD, ce), F32), pltpu.VMEM((l + 2 * CONV_PAD, ce), F32)],
        sem=("parallel", "arbitrary"), name=name, side=side,
        into=None if into is None else {0: into[0], 1: into[1]})


def _pool_tables(l, e):
    grp = e // len(POOL_WINDOWS)
    w = jnp.repeat(jnp.array(POOL_WINDOWS, jnp.int32), grp)[None, :]
    off = jnp.arange(POOL_TAPS, dtype=jnp.int32)[:, None] - POOL_TAPS // 2
    taps = jnp.logical_and(off >= -(w // 2), off < w - w // 2).astype(F32)
    t = jnp.arange(l, dtype=jnp.int32)[:, None]
    cnt = jnp.clip(t + (w - w // 2), 0, l) - jnp.clip(t - w // 2, 0, l)
    return taps, 1.0 / cnt.astype(F32)


def _pool1(v_src, taps, inv_cnt, nseq, l, row0, transpose, name, out_dtype, into=None):
    e = taps.shape[1]
    ce = LANES
    half = POOL_TAPS // 2

    def body(v_ref, taps_ref, ic_ref, o_ref, pad_ref):
        v = v_ref[...].astype(F32)
        if transpose:
            _fill_pad(pad_ref, v * ic_ref[...], l, ce)
            pieces = _tap_sum(pad_ref, taps_ref, CONV_PAD - half + 1, POOL_TAPS, l, ce, True)
        else:
            _fill_pad(pad_ref, v, l, ce)
            pieces = _tap_sum(pad_ref, taps_ref, CONV_PAD - half, POOL_TAPS, l, ce, False)
        for n, acc in enumerate(pieces):
            rows = pl.ds(n * CHUNK, acc.shape[0])
            vn = v[n * CHUNK:n * CHUNK + acc.shape[0], :]
            if transpose:
                o_ref[rows, :] = (acc - vn).astype(o_ref.dtype)
            else:
                o_ref[rows, :] = (acc * ic_ref[rows, :] - vn).astype(o_ref.dtype)

    return _pallas(
        body, (v_src, taps, inv_cnt), grid=(e // ce, nseq),
        in_specs=[_seq_spec(l, ce, row0), pl.BlockSpec((POOL_TAPS, ce), lambda j, s: (0, j)),
                  pl.BlockSpec((l, ce), lambda j, s: (0, j))],
        out_specs=_seq_spec(l, ce, row0), out_shape=_sds((v_src.shape[0], e), out_dtype),
        scratch_shapes=[pltpu.VMEM((l + 2 * CONV_PAD, ce), F32)],
        sem=("parallel", "arbitrary"), name=name, into=None if into is None else {0: into})


def _layernorm_parts(x, eps=EPS):
    mu = jnp.mean(x, axis=-1, keepdims=True)
    xc = x - mu
    r = lax.rsqrt(jnp.mean(xc * xc, axis=-1, keepdims=True) + eps)
    return xc * r, r


def _layernorm_bwd(dy, xh, r, g):
    dxh = dy * g
    return r * (dxh - jnp.mean(dxh, axis=-1, keepdims=True) - xh * jnp.mean(dxh * xh, axis=-1, keepdims=True))


def _conv2_fwd(y2, z, ln_g, ln_b, tm, name):
    t, e = y2.shape

    def body(y_ref, g_ref, lg_ref, lb_ref, o_ref):
        xh, _ = _layernorm_parts(y_ref[...])
        o_ref[...] = (_silu(xh * lg_ref[...] + lb_ref[...]) * _silu(g_ref[...])).astype(o_ref.dtype)

    return pl.pallas_call(
        body, grid=(t // tm,), in_specs=[_row(tm, e), _row(tm, e, 2), _const((1, e)), _const((1, e))],
        out_specs=_row(tm, e), out_shape=_sds((t, e), BF16), compiler_params=_params("parallel"),
        name=name)(y2, z, ln_g, ln_b)


def _conv2_bwd(dy4, y2, z, ln_g, ln_b, tm, name):
    t, e = y2.shape

    def body(dy_ref, y_ref, g_ref, lg_ref, lb_ref, dy2_ref, dg_ref, dlg_ref, dlb_ref):
        @pl.when(pl.program_id(0) == 0)
        def _():
            dlg_ref[...] = jnp.zeros_like(dlg_ref)
            dlb_ref[...] = jnp.zeros_like(dlb_ref)

        dy, gz = dy_ref[...], g_ref[...]
        xh, r = _layernorm_parts(y_ref[...])
        y3 = xh * lg_ref[...] + lb_ref[...]
        dg_ref[...] = (dy * _silu(y3) * _dsilu(gz)).astype(dg_ref.dtype)
        dy3 = dy * _silu(gz) * _dsilu(y3)
        dlg_ref[...] += _rowsum(dy3 * xh)
        dlb_ref[...] += _rowsum(dy3)
        dy2_ref[...] = _layernorm_bwd(dy3, xh, r, lg_ref[...])

    return pl.pallas_call(
        body, grid=(t // tm,),
        in_specs=[_row(tm, e), _row(tm, e), _row(tm, e, 2), _const((1, e)), _const((1, e))],
        out_specs=[_row(tm, e), _row(tm, e), _const((1, e)), _const((1, e))],
        out_shape=[_sds((t, e)), _sds((t, e), BF16), _sds((1, e)), _sds((1, e))],
        compiler_params=_params("arbitrary"), name=name)(dy4, y2, z, ln_g, ln_b)


def _pool2_fwd(pm, w_grp, scale, z, tm, name):
    t, e = pm.shape
    ng, gw = w_grp.shape[0], w_grp.shape[1]

    def body(pm_ref, w_ref, sc_ref, g_ref, o_ref):
        for k in range(ng):
            cols = slice(k * gw, (k + 1) * gw)
            y = _dot(pm_ref[:, cols], w_ref[k])
            o_ref[:, cols] = (y * sc_ref[:, cols] * _silu(g_ref[:, cols])).astype(o_ref.dtype)

    return pl.pallas_call(
        body, grid=(t // tm,), in_specs=[_row(tm, e), _const(w_grp.shape), _const((1, e)), _row(tm, e, 1)],
        out_specs=_row(tm, e), out_shape=_sds((t, e), BF16), compiler_params=_params("parallel"),
        name=name)(pm, w_grp, scale, z)


def _pool2_bwd(dy2, pm, w_grp, scale, z, tm, name):
    t, e = pm.shape
    ng, gw = w_grp.shape[0], w_grp.shape[1]

    def body(dy_ref, pm_ref, w_ref, sc_ref, g_ref, dpm_ref, dg_ref, dsc_ref, dw_ref):
        @pl.when(pl.program_id(0) == 0)
        def _():
            dsc_ref[...] = jnp.zeros_like(dsc_ref)
            dw_ref[...] = jnp.zeros_like(dw_ref)

        for k in range(ng):
            cols = slice(k * gw, (k + 1) * gw)
            dy, gz, sc, pmk = dy_ref[:, cols], g_ref[:, cols], sc_ref[:, cols], pm_ref[:, cols]
            y = _dot(pmk, w_ref[k])
            dg_ref[:, cols] = (dy * (y * sc) * _dsilu(gz)).astype(dg_ref.dtype)
            dys = dy * _silu(gz)
            dsc_ref[:, cols] += _rowsum(dys * y)
            dyk = dys * sc
            dpm_ref[:, cols] = _dot_nt(dyk, w_ref[k])
            dw_ref[k] += _dot_tn(pmk, dyk)

    return pl.pallas_call(
        body, grid=(t // tm,),
        in_specs=[_row(tm, e), _row(tm, e), _const(w_grp.shape), _const((1, e)), _row(tm, e, 1)],
        out_specs=[_row(tm, e), _row(tm, e), _const((1, e)), _const(w_grp.shape)],
        out_shape=[_sds((t, e)), _sds((t, e), BF16), _sds((1, e)), _sds(w_grp.shape)],
        compiler_params=_params("arbitrary"), name=name)(dy2, pm, w_grp, scale, z)


def _rms_f(x, g, n):
    r = lax.rsqrt(jnp.sum(x * x, axis=-1, keepdims=True) * (1.0 / n) + EPS)
    return x * r * g


def _rms_b(x, g, dy, n):
    r = lax.rsqrt(jnp.sum(x * x, axis=-1, keepdims=True) * (1.0 / n) + EPS)
    xh = x * r
    dxh = dy * g
    return r * (dxh - xh * (jnp.sum(dxh * xh, axis=-1, keepdims=True) * (1.0 / n))), dy * xh


def _swap16(x):
    lane = lax.broadcasted_iota(jnp.int32, x.shape, 1)
    return jnp.where(lane % 32 < 16, pltpu.roll(x, LANES - 16, 1), pltpu.roll(x, 16, 1))


def _rope(x, c, s):
    return x * c + _swap16(x) * s


def _rope_t(dy, c, s):
    return dy * c + _swap16(dy * s)


def _rope_tables(l, lc, nb):
    t = jnp.arange(l, dtype=jnp.int32)
    row_id, col_id = (t // GRID_W).astype(F32), (t % GRID_W).astype(F32)
    axis_dim = MLA_ROPE // 2
    freqs = ROPE_THETA ** (-jnp.arange(0, axis_dim, 2, dtype=F32) / axis_dim)
    ar, ac = row_id[:, None] * freqs, col_id[:, None] * freqs
    pad1, pad0 = jnp.ones((l, LANES - MLA_ROPE), F32), jnp.zeros((l, LANES - MLA_ROPE), F32)
    ctab = jnp.concatenate([jnp.cos(ar), jnp.cos(ar), jnp.cos(ac), jnp.cos(ac), pad1], axis=1)
    stab = jnp.concatenate([-jnp.sin(ar), jnp.sin(ar), -jnp.sin(ac), jnp.sin(ac), pad0], axis=1)
    ctab = jnp.concatenate([jnp.tile(ctab, (nb, 1)), jnp.ones((nb * lc, LANES), F32)], axis=0)
    stab = jnp.concatenate([jnp.tile(stab, (nb, 1)), jnp.zeros((nb * lc, LANES), F32)], axis=0)
    return ctab, stab


def _kv_pre_fwd(zkv, kv_norm, rope_g, ctab, stab, tm, name):
    t = zkv.shape[0]

    def body(z_ref, gk_ref, gr_ref, c_ref, s_ref, ck_ref, kr_ref):
        ck_ref[...] = _rms_f(z_ref[:, :MLA_KV_RANK], gk_ref[...], MLA_KV_RANK).astype(ck_ref.dtype)
        kr = _rms_f(z_ref[:, MLA_KV_RANK:], gr_ref[...], MLA_ROPE)
        kr_ref[...] = _rope(kr, c_ref[...], s_ref[...]).astype(kr_ref.dtype)

    w = MLA_KV_RANK + LANES
    return pl.pallas_call(
        body, grid=(t // tm,),
        in_specs=[_row(tm, w), _const((1, MLA_KV_RANK)), _const((1, LANES)), _row(tm, LANES), _row(tm, LANES)],
        out_specs=[_row(tm, MLA_KV_RANK), _row(tm, LANES)],
        out_shape=[_sds((t, MLA_KV_RANK), BF16), _sds((t, LANES), BF16)],
        compiler_params=_params("parallel"), name=name)(zkv, kv_norm, rope_g, ctab, stab)


def _kv_pre_bwd(dck, dkr, zkv, kv_norm, rope_g, ctab, stab, tm, name):
    t = zkv.shape[0]
    w = MLA_KV_RANK + LANES

    def body(dck_ref, dkr_ref, z_ref, gk_ref, gr_ref, c_ref, s_ref, dz_ref, dgk_ref, dgr_ref):
        @pl.when(pl.program_id(0) == 0)
        def _():
            dgk_ref[...] = jnp.zeros_like(dgk_ref)
            dgr_ref[...] = jnp.zeros_like(dgr_ref)

        dx, dg = _rms_b(z_ref[:, :MLA_KV_RANK], gk_ref[...], dck_ref[...], MLA_KV_RANK)
        dz_ref[:, :MLA_KV_RANK] = dx.astype(dz_ref.dtype)
        dgk_ref[...] += _rowsum(dg)
        dy = _rope_t(dkr_ref[...], c_ref[...], s_ref[...])
        dx, dg = _rms_b(z_ref[:, MLA_KV_RANK:], gr_ref[...], dy, MLA_ROPE)
        dz_ref[:, MLA_KV_RANK:] = dx.astype(dz_ref.dtype)
        dgr_ref[...] += _rowsum(dg)

    return pl.pallas_call(
        body, grid=(t // tm,),
        in_specs=[_row(tm, MLA_KV_RANK), _row(tm, LANES), _row(tm, w), _const((1, MLA_KV_RANK)), _const((1, LANES)),
                  _row(tm, LANES), _row(tm, LANES)],
        out_specs=[_row(tm, w), _const((1, MLA_KV_RANK)), _const((1, LANES))],
        out_shape=[_sds((t, w), BF16), _sds((1, MLA_KV_RANK)), _sds((1, LANES))],
        compiler_params=_params("arbitrary"), name=name)(dck, dkr, zkv, kv_norm, rope_g, ctab, stab)


def _q_pre_fwd(zq, q_norm, tm, name):
    t, w = zq.shape

    def body(z_ref, g_ref, o_ref):
        o_ref[...] = _rms_f(z_ref[...], g_ref[...], w).astype(o_ref.dtype)

    return pl.pallas_call(
        body, grid=(t // tm,), in_specs=[_row(tm, w), _const((1, w))], out_specs=_row(tm, w),
        out_shape=_sds((t, w), BF16), compiler_params=_params("parallel"), name=name)(zq, q_norm)


def _q_pre_bwd(dcq, zq, q_norm, tm, name):
    t, w = zq.shape

    def body(d_ref, z_ref, g_ref, dz_ref, dg_ref):
        @pl.when(pl.program_id(0) == 0)
        def _():
            dg_ref[...] = jnp.zeros_like(dg_ref)

        dx, dg = _rms_b(z_ref[...], g_ref[...], d_ref[...], w)
        dz_ref[...] = dx.astype(dz_ref.dtype)
        dg_ref[...] += _rowsum(dg)

    return pl.pallas_call(
        body, grid=(t // tm,), in_specs=[_row(tm, w), _row(tm, w), _const((1, w))],
        out_specs=[_row(tm, w), _const((1, w))], out_shape=[_sds((t, w), BF16), _sds((1, w))],
        compiler_params=_params("arbitrary"), name=name)(dcq, zq, q_norm)


def _k_post_fwd(kv, krr, nope_g, tm, name):
    t, w = kv.shape

    def body(kv_ref, kr_ref, gn_ref, k_ref, v_ref):
        for h in range(MLA_HEADS):
            a = h * HEAD_W
            k_ref[:, a:a + LANES] = _rms_f(kv_ref[:, a:a + LANES], gn_ref[...], MLA_NOPE).astype(k_ref.dtype)
            k_ref[:, a + LANES:a + HEAD_W] = kr_ref[...]
            v_ref[:, h * LANES:(h + 1) * LANES] = kv_ref[:, a + LANES:a + HEAD_W].astype(v_ref.dtype)

    return pl.pallas_call(
        body, grid=(t // tm,), in_specs=[_row(tm, w), _row(tm, LANES), _const((1, LANES))],
        out_specs=[_row(tm, w), _row(tm, w // 2)], out_shape=[_sds((t, w), BF16), _sds((t, w // 2), BF16)],
        compiler_params=_params("parallel"), name=name)(kv, krr, nope_g)


def _k_post_bwd(dkl, dkc, dvl, dvc, kv, nope_g, tm, name):
    t, w = kv.shape
    nl = dkl.shape[0] // tm

    def body(dkl_ref, dkc_ref, dvl_ref, dvc_ref, kv_ref, gn_ref, dkv_ref, dkr_ref, dgn_ref):
        i = pl.program_id(0)

        @pl.when(i == 0)
        def _():
            dgn_ref[...] = jnp.zeros_like(dgn_ref)

        dkr = jnp.zeros(dkr_ref.shape, F32)
        for h in range(MLA_HEADS):
            a = h * HEAD_W
            dk = jnp.where(i < nl, dkl_ref[:, a:a + HEAD_W], dkc_ref[:, a:a + HEAD_W])
            dv = jnp.where(i < nl, dvl_ref[:, h * LANES:(h + 1) * LANES], dvc_ref[:, h * LANES:(h + 1) * LANES])
            dx, dg = _rms_b(kv_ref[:, a:a + LANES], gn_ref[...], dk[:, :LANES], MLA_NOPE)
            dkv_ref[:, a:a + LANES] = dx.astype(dkv_ref.dtype)
            dgn_ref[...] += _rowsum(dg)
            dkv_ref[:, a + LANES:a + HEAD_W] = dv.astype(dkv_ref.dtype)
            dkr = dkr + dk[:, LANES:]
        dkr_ref[...] = dkr

    lat = lambda cols: pl.BlockSpec((tm, cols), lambda i: (jnp.minimum(i, nl - 1), 0))
    ctx = lambda cols: pl.BlockSpec((tm, cols), lambda i: (jnp.maximum(i - nl, 0), 0))
    return pl.pallas_call(
        body, grid=(t // tm,),
        in_specs=[lat(w), ctx(w), lat(w // 2), ctx(w // 2), _row(tm, w), _const((1, LANES))],
        out_specs=[_row(tm, w), _row(tm, LANES), _const((1, LANES))],
        out_shape=[_sds((t, w), BF16), _sds((t, LANES)), _sds((1, LANES))],
        compiler_params=_params("arbitrary"), name=name)(dkl, dkc, dvl, dvc, kv, nope_g)


def _attn_specs(nb, l, lc, tq):
    nq = l // tq
    ctx0 = nb * l // lc
    q_spec = lambda w: pl.BlockSpec((tq, w), lambda b, h, i: (b * nq + i, h))
    lat = lambda w: pl.BlockSpec((l, w), lambda b, h, i: (b, h))
    ctx = lambda w: pl.BlockSpec((lc, w), lambda b, h, i: (ctx0 + b, h))
    return nq, q_spec, lat, ctx


def _attn_fwd(q, nope_g, rope_g, ctab, stab, kf, vf, nb, l, lc, name, side=None):
    tq = _pick(l, (1024, 512, 256, 128))
    nq, q_spec, lat, ctx = _attn_specs(nb, l, lc, tq)
    sub = min(tq // ATTN_CHAINS, 256)
    tab = pl.BlockSpec((tq, LANES), lambda b, h, i: (b * nq + i, 0))

    def body(q_ref, gn_ref, gr_ref, c_ref, s_ref, kl_ref, kc_ref, vl_ref, vc_ref, o_ref, lse_ref, qf_ref):
        for r in range(0, tq, sub):
            rows = slice(r, r + sub)
            qn = _rms_f(q_ref[rows, :LANES], gn_ref[...], MLA_NOPE)
            qr = _rope(_rms_f(q_ref[rows, LANES:], gr_ref[...], MLA_ROPE), c_ref[rows, :], s_ref[rows, :])
            q = (jnp.concatenate([qn, qr], axis=1) * MLA_SCALE).astype(BF16)
            qf_ref[rows, :] = q
            s1, s2 = _dot_nt(q, kl_ref[...]), _dot_nt(q, kc_ref[...])
            m = jnp.maximum(jnp.max(s1, axis=-1, keepdims=True), jnp.max(s2, axis=-1, keepdims=True))
            p1, p2 = jnp.exp(s1 - m), jnp.exp(s2 - m)
            den = jnp.sum(p1, axis=-1, keepdims=True) + jnp.sum(p2, axis=-1, keepdims=True)
            o_ref[rows, :] = (_dot(p1, vl_ref[...]) + _dot(p2, vc_ref[...])) / den
            lse_ref[rows, :] = jnp.broadcast_to(m + jnp.log(den), (sub, LANES))

    return _pallas(
        body, (q, nope_g, rope_g, ctab, stab, kf, kf, vf, vf), grid=(nb, MLA_HEADS, nq),
        in_specs=[q_spec(HEAD_W), _const((1, LANES)), _const((1, LANES)), tab, tab, lat(HEAD_W), ctx(HEAD_W),
                  lat(LANES), ctx(LANES)],
        out_specs=[q_spec(LANES), q_spec(LANES), q_spec(HEAD_W)],
        out_shape=[_sds((nb * l, MLA_HEADS * LANES)), _sds((nb * l, MLA_HEADS * LANES)),
                   _sds((nb * l, MLA_HEADS * HEAD_W), BF16)],
        sem=("parallel", "parallel", "arbitrary"), name=name, side=side)


def _attn_bwd(do, o, lse, qf, q, nope_g, rope_g, ctab, stab, kf, vf, nb, l, lc, name, side=None):
    tq = _pick(l, (1024, 512, 256, 128))
    nq, q_spec, lat, ctx = _attn_specs(nb, l, lc, tq)
    out_lat = lambda w: pl.BlockSpec((l, w), lambda b, h, i: (b, h))
    out_ctx = lambda w: pl.BlockSpec((lc, w), lambda b, h, i: (b, h))
    sub = tq // ATTN_CHAINS
    tab = pl.BlockSpec((tq, LANES), lambda b, h, i: (b * nq + i, 0))

    def body(do_ref, o_ref, lse_ref, qf_ref, q_ref, gn_ref, gr_ref, c_ref, s_ref, kl_ref, kc_ref, vl_ref, vc_ref,
             dq_ref, dkl_ref, dkc_ref, dvl_ref, dvc_ref, dgn_ref, dgr_ref):
        @pl.when(pl.program_id(2) == 0)
        def _():
            dkl_ref[...] = jnp.zeros_like(dkl_ref)
            dkc_ref[...] = jnp.zeros_like(dkc_ref)
            dvl_ref[...] = jnp.zeros_like(dvl_ref)
            dvc_ref[...] = jnp.zeros_like(dvc_ref)

        @pl.when((pl.program_id(0) == 0) & (pl.program_id(1) == 0) & (pl.program_id(2) == 0))
        def _():
            dgn_ref[...] = jnp.zeros_like(dgn_ref)
            dgr_ref[...] = jnp.zeros_like(dgr_ref)

        parts = []
        for r in range(0, tq, sub):
            rows = slice(r, r + sub)
            qs, dof = qf_ref[rows, :], do_ref[rows, :]
            delta = jnp.sum(dof * o_ref[rows, :], axis=-1, keepdims=True)
            lse = lse_ref[rows, :1]
            dqs, part = jnp.zeros((sub, HEAD_W), F32), []
            for k_ref, v_ref in ((kl_ref, vl_ref), (kc_ref, vc_ref)):
                p = jnp.exp(_dot_nt(qs, k_ref[...]) - lse)
                ds = p * (_dot_nt(dof, v_ref[...]) - delta)
                dqs = dqs + _dot(ds, k_ref[...])
                part += [_dot_tn(ds, qs), _dot_tn(p, dof)]
            dqs = dqs * MLA_SCALE
            dxn, dgn = _rms_b(q_ref[rows, :LANES], gn_ref[...], dqs[:, :LANES], MLA_NOPE)
            dxr, dgr = _rms_b(q_ref[rows, LANES:], gr_ref[...], _rope_t(dqs[:, LANES:], c_ref[rows, :], s_ref[rows, :]),
                              MLA_ROPE)
            dq_ref[rows, :] = jnp.concatenate([dxn, dxr], axis=1).astype(dq_ref.dtype)
            parts.append(part + [_rowsum(dgn), _rowsum(dgr)])
        for n, ref in enumerate((dkl_ref, dvl_ref, dkc_ref, dvc_ref, dgn_ref, dgr_ref)):
            ref[...] += functools.reduce(lambda u, v: u + v, [part[n] for part in parts])

    kw, vw = MLA_HEADS * HEAD_W, MLA_HEADS * LANES
    return _pallas(
        body, (do, o, lse, qf, q, nope_g, rope_g, ctab, stab, kf, kf, vf, vf), grid=(nb, MLA_HEADS, nq),
        in_specs=[q_spec(LANES), q_spec(LANES), q_spec(LANES), q_spec(HEAD_W), q_spec(HEAD_W), _const((1, LANES)),
                  _const((1, LANES)), tab, tab, lat(HEAD_W), ctx(HEAD_W), lat(LANES), ctx(LANES)],
        out_specs=[q_spec(HEAD_W), out_lat(HEAD_W), out_ctx(HEAD_W), out_lat(LANES), out_ctx(LANES),
                   _const((1, LANES)), _const((1, LANES))],
        out_shape=[_sds((nb * l, kw), BF16), _sds((nb * l, kw)), _sds((nb * lc, kw)), _sds((nb * l, vw)),
                   _sds((nb * lc, vw)), _sds((1, LANES)), _sds((1, LANES))],
        sem=("arbitrary", "arbitrary", "arbitrary"), name=name, side=side)


def _gate_fwd(o, g, tm, name):
    t, e = o.shape

    def body(o_ref, g_ref, y_ref):
        y_ref[...] = (o_ref[...] * _silu(g_ref[...])).astype(y_ref.dtype)

    return pl.pallas_call(
        body, grid=(t // tm,), in_specs=[_row(tm, e), _row(tm, e)], out_specs=_row(tm, e),
        out_shape=_sds((t, e), BF16), compiler_params=_params("parallel"), name=name)(o, g)


def _gate_bwd(dy, o, g, tm, name):
    t, e = o.shape

    def body(dy_ref, o_ref, g_ref, do_ref, dg_ref):
        dy, gz = dy_ref[...], g_ref[...]
        do_ref[...] = dy * _silu(gz)
        dg_ref[...] = (dy * o_ref[...] * _dsilu(gz)).astype(dg_ref.dtype)

    return pl.pallas_call(
        body, grid=(t // tm,), in_specs=[_row(tm, e), _row(tm, e), _row(tm, e)],
        out_specs=[_row(tm, e), _row(tm, e)], out_shape=[_sds((t, e)), _sds((t, e), BF16)],
        compiler_params=_params("parallel"), name=name)(dy, o, g)


def _chunk_fwd(z, ln_g, ln_b, w_s, bs_full, name):
    t, e = z.shape[0], z.shape[1] // 3
    tm = _pick(t, CHUNK_TILES)

    def body(u_ref, v_ref, g_ref, lg_ref, lb_ref, w_ref, bs_ref, y_ref):
        for r0 in range(0, tm, CHUNK):
            rows = slice(r0, r0 + CHUNK)
            xh, _ = _layernorm_parts(v_ref[rows, :])
            vn = xh * lg_ref[...] + lb_ref[...]
            for k in range(CHUNK_GROUPS):
                cols = slice(k * LANES, (k + 1) * LANES)
                s = _dot(w_ref[k], vn[:, cols]) + bs_ref[:, cols]
                y_ref[rows, cols] = (u_ref[rows, cols] * s * _silu(g_ref[rows, cols])).astype(y_ref.dtype)

    return pl.pallas_call(
        body, grid=(t // tm,),
        in_specs=[_row(tm, e, 0), _row(tm, e, 1), _row(tm, e, 2), _const((1, e)), _const((1, e)),
                  _const(w_s.shape), _const((CHUNK, e))],
        out_specs=_row(tm, e), out_shape=_sds((t, e), BF16), compiler_params=_params("parallel"),
        name=name)(z, z, z, ln_g, ln_b, w_s, bs_full)


def _chunk_bwd(dy, z, ln_g, ln_b, w_s, bs_full, name):
    t, e = z.shape[0], z.shape[1] // 3
    tm = _pick(t, CHUNK_TILES)

    def body(dy_ref, u_ref, v_ref, g_ref, lg_ref, lb_ref, w_ref, bs_ref, dz_ref, dw_ref, dbs_ref, dlg_ref, dlb_ref,
             acc_ref):
        i = pl.program_id(0)

        @pl.when(i == 0)
        def _():
            dw_ref[...] = jnp.zeros_like(dw_ref)
            dlg_ref[...] = jnp.zeros_like(dlg_ref)
            dlb_ref[...] = jnp.zeros_like(dlb_ref)
            acc_ref[...] = jnp.zeros_like(acc_ref)

        per_chunk = []
        for r0 in range(0, tm, CHUNK):
            rows = slice(r0, r0 + CHUNK)
            xh, r = _layernorm_parts(v_ref[rows, :])
            vn = xh * lg_ref[...] + lb_ref[...]
            dvn, dss, dws = [], [], []
            for k in range(CHUNK_GROUPS):
                cols = slice(k * LANES, (k + 1) * LANES)
                dyk, u, gz = dy_ref[rows, cols], u_ref[rows, cols], g_ref[rows, cols]
                s = _dot(w_ref[k], vn[:, cols]) + bs_ref[:, cols]
                sg = _silu(gz)
                dz_ref[rows, cols] = (dyk * s * sg).astype(dz_ref.dtype)
                dz_ref[rows, 2 * e + k * LANES:2 * e + (k + 1) * LANES] = (dyk * u * s * _dsilu(gz)).astype(
                    dz_ref.dtype)
                ds = dyk * u * sg
                dss.append(ds)
                dws.append(_dot_nt(ds, vn[:, cols]))
                dvn.append(_dot_tn(w_ref[k], ds))
            dvn = jnp.concatenate(dvn, axis=1)
            dz_ref[rows, e:2 * e] = _layernorm_bwd(dvn, xh, r, lg_ref[...]).astype(dz_ref.dtype)
            per_chunk.append((dss, dws, _rowsum(dvn * xh), _rowsum(dvn)))
        total = lambda parts: functools.reduce(lambda a, b: a + b, parts)
        for k in range(CHUNK_GROUPS):
            acc_ref[:, k * LANES:(k + 1) * LANES] += total([c[0][k] for c in per_chunk])
            dw_ref[k] += total([c[1][k] for c in per_chunk])
        dlg_ref[...] += total([c[2] for c in per_chunk])
        dlb_ref[...] += total([c[3] for c in per_chunk])

        @pl.when(i == pl.num_programs(0) - 1)
        def _():
            lane = lax.broadcasted_iota(jnp.int32, dbs_ref.shape, 1)
            out = jnp.zeros(dbs_ref.shape, F32)
            for k in range(CHUNK_GROUPS):
                col = jnp.sum(acc_ref[:, k * LANES:(k + 1) * LANES], axis=1, keepdims=True)
                out = jnp.where(lane == k, col, out)
            dbs_ref[...] = out

    return pl.pallas_call(
        body, grid=(t // tm,),
        in_specs=[_row(tm, e), _row(tm, e, 0), _row(tm, e, 1), _row(tm, e, 2), _const((1, e)),
                  _const((1, e)), _const(w_s.shape), _const((CHUNK, e))],
        out_specs=[_row(tm, 3 * e), _const(w_s.shape), _const((CHUNK, CHUNK_GROUPS)), _const((1, e)),
                   _const((1, e))],
        out_shape=[_sds((t, 3 * e), BF16), _sds(w_s.shape), _sds((CHUNK, CHUNK_GROUPS)), _sds((1, e)), _sds((1, e))],
        scratch_shapes=[pltpu.VMEM((CHUNK, e), F32)],
        compiler_params=_params("arbitrary"), name=name)(dy, z, z, z, ln_g, ln_b, w_s, bs_full)


def _mod_rows(mods, layer, d, nseg):
    m = mods[layer, :nseg]
    return [m[:, None, k * d:(k + 1) * d] for k in range(3)]


def _local_step(x, ctx, tgt, w, mods, comm=None):
    nb, l, d = x.shape
    lc = ctx.shape[1]
    e = d
    tl, ta = nb * l, nb * (l + lc)
    tm = _pick(lc, (256, 128))
    segs_a, segs_l = _Segs((l,) * nb + (lc,) * nb, tm), _Segs((l,) * nb, tm)
    norm_g = w['norm_g']
    g = {}

    def carried(tag, fn, *args, **kw):
        if comm is None:
            return fn(*args, **kw)
        res, brought = fn(*args, side=comm.side(tag), **kw)
        comm.done(tag, brought)
        return res

    xa0 = jnp.concatenate([x.reshape(tl, d), ctx.reshape(nb * lc, d)], axis=0)

    sh0, sc0, gt0 = _mod_rows(mods, 0, d, 2 * nb)
    h0 = _rms_mod_fwd(xa0, norm_g[0:1], sc0, sh0, segs_a, "l0_norm")
    z0 = carried('fwd1', _mm, h0, w['cv_w_in'], "l0_in")
    y2_0 = carried('fwd2', _conv1_fwd, z0, w['cv_dw'], w['cv_db'], nb, l, 0, "l0_conv_lat")
    y2_0 = _conv1_fwd(z0, w['cv_dw'], w['cv_db'], nb, lc, tl, "l0_conv_ctx", into=y2_0)
    y4_0 = _conv2_fwd(y2_0, z0, w['cv_ln_g'], w['cv_ln_b'], tm, "l0_gate")
    o0, xa1 = _mm(y4_0, w['cv_w_out'], "l0_out", resid=(xa0, gt0, segs_a))

    sh1, sc1, gt1 = _mod_rows(mods, 1, d, 2 * nb)
    h1 = _rms_mod_fwd(xa1, norm_g[1:2], sc1, sh1, segs_a, "l1_norm")
    z1 = carried('fwd3', _mm, h1, w['pl_w_in'], "l1_in")
    taps_l, ic_l = _pool_tables(l, e)
    taps_c, ic_c = _pool_tables(lc, e)
    pm1 = _pool1(z1, taps_l, ic_l, nb, l, 0, False, "l1_pool_lat", BF16)
    pm1 = _pool1(z1, taps_c, ic_c, nb, lc, tl, False, "l1_pool_ctx", BF16, into=pm1)
    y2_1 = _pool2_fwd(pm1, w['pl_w_grp'], w['pl_scale'], z1, tm, "l1_group")
    o1, xa2 = _mm(y2_1, w['pl_w_out'], "l1_out", resid=(xa1, gt1, segs_a))

    sh2, sc2, gt2 = _mod_rows(mods, 2, d, 2 * nb)
    h2 = _rms_mod_fwd(xa2, norm_g[2:3], sc2, sh2, segs_a, "l2_norm")
    w_in = w['ml_w_in']
    kvc = MLA_KV_RANK + MLA_ROPE
    w_in_p = jnp.concatenate([w_in[:, :kvc], jnp.zeros((d, LANES - MLA_ROPE), w_in.dtype), w_in[:, kvc:]], axis=1)
    w_uq_p = jnp.pad(w['ml_w_uq'].reshape(MLA_Q_RANK, MLA_HEADS, MLA_NOPE + MLA_ROPE),
                     ((0, 0), (0, 0), (0, HEAD_W - MLA_NOPE - MLA_ROPE))).reshape(MLA_Q_RANK, MLA_HEADS * HEAD_W)
    rope_g = jnp.pad(w['ml_rope_norm'], ((0, 0), (0, LANES - MLA_ROPE)))
    nope_g = w['ml_nope_norm']
    ctab, stab = _rope_tables(l, lc, nb)
    kvw = MLA_KV_RANK + LANES
    w_kv, w_q, w_g = w_in_p[:, :kvw], w_in_p[:, kvw:kvw + MLA_Q_RANK], w_in_p[:, kvw + MLA_Q_RANK:]
    zkv = _mm(h2, w_kv, "l2_in_kv")
    zq, zg = _mm(h2, w_q, "l2_in_q", rows=tl), _mm(h2, w_g, "l2_in_g", rows=tl)
    ckvn, krr = _kv_pre_fwd(zkv, w['ml_kv_norm'], rope_g[1:2], ctab, stab, tm, "l2_kv_pre")
    cqn = _q_pre_fwd(zq, w['ml_q_norm'], tm, "l2_q_pre")
    q2 = _mm(cqn, w_uq_p, "l2_uq")
    kv2 = _mm(ckvn, w['ml_w_ukv'], "l2_ukv")
    kf, vf = _k_post_fwd(kv2, krr, nope_g[1:2], tm, "l2_k_post")
    o_att, lse, qf = carried('fwd4', _attn_fwd, q2, nope_g[0:1], rope_g[0:1], ctab, stab, kf, vf, nb, l, lc,
                             "l2_attn")
    og = _gate_fwd(o_att, zg, tm, "l2_gate")
    o2, x3 = _mm(og, w['ml_w_out'], "l2_out", resid=(xa2, gt2[:nb], segs_l))

    sh3, sc3, gt3 = _mod_rows(mods, 3, d, nb)
    h3 = _rms_mod_fwd(x3, norm_g[3:4], sc3, sh3, segs_l, "l3_norm")
    z3 = _mm(h3, w['ch_w_in'], "l3_in")
    bs_full = jnp.repeat(w['ch_b_s'], e // CHUNK_GROUPS, axis=1)
    y3 = _chunk_fwd(z3, w['ch_ln_g'], w['ch_ln_b'], w['ch_w_s'], bs_full, "l3_chunk")
    o3, x4 = _mm(y3, w['ch_w_out'], "l3_out", resid=(x3, gt3, segs_l))

    loss_vec, dx4 = _loss_head(x4, tgt.reshape(tl, d), tm, "loss")

    do3, dgt3 = _resid_bwd(dx4, o3, gt3, segs_l, "l3_resid_b")
    dy3 = _mm_nt(do3, w['ch_w_out'], "l3_out_bx")
    g['ch_w_out'] = _mm_tn(y3, do3, "l3_out_bw")
    dz3, g['ch_w_s'], g['ch_b_s'], g['ch_ln_g'], g['ch_ln_b'] = _chunk_bwd(
        dy3, z3, w['ch_ln_g'], w['ch_ln_b'], w['ch_w_s'], bs_full, "l3_chunk_b")
    dh3 = _mm_nt(dz3, w['ch_w_in'], "l3_in_bx")
    g['ch_w_in'] = _mm_tn(h3, dz3, "l3_in_bw", shards=N_DEV)
    dx3, dng3, dsc3, dsh3 = _rms_mod_bwd(x3, norm_g[3:4], sc3, sh3, dh3, dx4, segs_l, "l3_norm_b")
    if comm is not None:
        comm.grads_ready(3, g)

    do2, dgt2 = _resid_bwd(dx3, o2, gt2[:nb], segs_l, "l2_resid_b")
    dog = _mm_nt(do2, w['ml_w_out'], "l2_out_bx")
    g['ml_w_out'] = _mm_tn(og, do2, "l2_out_bw")
    d_att, dzg = _gate_bwd(dog, o_att, zg, tm, "l2_gate_b")
    dq2, dkl, dkc, dvl, dvc, dnope_q, drope_q = carried(
        'quad3', _attn_bwd, d_att, o_att, lse, qf, q2, nope_g[0:1], rope_g[0:1], ctab, stab, kf, vf, nb, l, lc,
        "l2_attn_b")
    dkv2, dkrr, dnope_k = _k_post_bwd(dkl, dkc, dvl, dvc, kv2, nope_g[1:2], tm, "l2_k_post_b")
    dcqn = _mm_nt(dq2, w_uq_p, "l2_uq_bx")
    g_uq_p = _mm_tn(cqn, dq2, "l2_uq_bw")
    dckvn = _mm_nt(dkv2, w['ml_w_ukv'], "l2_ukv_bx")
    g['ml_w_ukv'] = _mm_tn(ckvn, dkv2, "l2_ukv_bw", shards=N_DEV)
    dzq, g['ml_q_norm'] = _q_pre_bwd(dcqn, zq, w['ml_q_norm'], tm, "l2_q_pre_b")
    dzkv, g['ml_kv_norm'], drope_k = _kv_pre_bwd(dckvn, dkrr, zkv, w['ml_kv_norm'], rope_g[1:2], ctab, stab, tm,
                                                  "l2_kv_pre_b")
    dh2 = _mm_nt_sum([(dzkv, w_kv), (dzq, w_q), (dzg, w_g)], "l2_in_bx")
    g['ml_w_in'] = jnp.concatenate([_mm_tn(h2, dzkv, "l2_in_kv_bw")[:, :kvc], _mm_tn(h2, dzq, "l2_in_q_bw", rows=tl),
                                    _mm_tn(h2, dzg, "l2_in_g_bw", rows=tl)], axis=1)
    g['ml_w_uq'] = g_uq_p.reshape(MLA_Q_RANK, MLA_HEADS, HEAD_W)[:, :, :MLA_NOPE + MLA_ROPE].reshape(
        MLA_Q_RANK, MLA_HEADS * (MLA_NOPE + MLA_ROPE))
    g['ml_nope_norm'] = jnp.concatenate([dnope_q, dnope_k], axis=0)
    g['ml_rope_norm'] = jnp.concatenate([drope_q, drope_k], axis=0)[:, :MLA_ROPE]
    dxa2, dng2, dsc2, dsh2 = _rms_mod_bwd(xa2, norm_g[2:3], sc2, sh2, dh2, dx3, segs_a, "l2_norm_b")
    if comm is not None:
        comm.grads_ready(2, g)

    do1, dgt1 = _resid_bwd(dxa2, o1, gt1, segs_a, "l1_resid_b")
    dy2_1 = _mm_nt(do1, w['pl_w_out'], "l1_out_bx")
    g['pl_w_out'] = _mm_tn(y2_1, do1, "l1_out_bw")
    dpm, dgz1, g['pl_scale'], g['pl_w_grp'] = _pool2_bwd(dy2_1, pm1, w['pl_w_grp'], w['pl_scale'], z1, tm,
                                                          "l1_group_b")
    dv1 = _pool1(dpm, taps_l, ic_l, nb, l, 0, True, "l1_pool_lat_b", BF16)
    dv1 = _pool1(dpm, taps_c, ic_c, nb, lc, tl, True, "l1_pool_ctx_b", BF16, into=dv1)
    dz1 = jnp.concatenate([dv1, dgz1], axis=1)
    dh1 = _mm_nt(dz1, w['pl_w_in'], "l1_in_bx")
    g['pl_w_in'] = carried('quad2', _mm_tn, h1, dz1, "l1_in_bw", shards=N_DEV)
    dxa1, dng1, dsc1, dsh1 = _rms_mod_bwd(xa1, norm_g[1:2], sc1, sh1, dh1, dxa2, segs_a, "l1_norm_b")

    do0, dgt0 = _resid_bwd(dxa1, o0, gt0, segs_a, "l0_resid_b")
    dy4 = _mm_nt(do0, w['cv_w_out'], "l0_out_bx")
    g['cv_w_out'] = _mm_tn(y4_0, do0, "l0_out_bw")
    if comm is not None:
        comm.grads_ready(1, g)
    dy2, dgz0, g['cv_ln_g'], g['cv_ln_b'] = _conv2_bwd(dy4, y2_0, z0, w['cv_ln_g'], w['cv_ln_b'], tm, "l0_gate_b")
    da_l, db_l, ddw, ddb = carried('quad1', _conv1_bwd, dy2, z0, w['cv_dw'], jnp.zeros((CONV_WIDTH, e), F32),
                                   jnp.zeros((1, e), F32), nb, l, 0, "l0_conv_lat_b")
    da, db_, g['cv_dw'], g['cv_db'] = _conv1_bwd(dy2, z0, w['cv_dw'], ddw, ddb, nb, lc, tl, "l0_conv_ctx_b",
                                                 into=(da_l, db_l))
    dz0 = jnp.concatenate([da, db_, dgz0], axis=1)
    g['cv_w_in'] = _mm_tn(h0, dz0, "l0_in_bw", shards=N_DEV)
    if comm is not None:
        comm.grads_ready(0, g)
    dh0 = carried('quad0', _mm_nt, dz0, w['cv_w_in'], "l0_in_bx")
    dxa0, dng0, dsc0, dsh0 = _rms_mod_bwd(xa0, norm_g[0:1], sc0, sh0, dh0, dxa1, segs_a, "l0_norm_b")

    def rows4(t):
        return jnp.pad(t[:, 0], ((0, 2 * nb - t.shape[0]), (0, 0)))

    dmods = jnp.stack([
        jnp.concatenate([rows4(dsh0), rows4(dsc0), rows4(dgt0)], axis=1),
        jnp.concatenate([rows4(dsh1), rows4(dsc1), rows4(dgt1)], axis=1),
        jnp.concatenate([rows4(dsh2), rows4(dsc2), rows4(dgt2)], axis=1),
        jnp.concatenate([rows4(dsh3), rows4(dsc3), rows4(dgt3)], axis=1)])
    dnorm_g = jnp.concatenate([dng0, dng1, dng2, dng3], axis=0)
    return loss_vec, dxa0[:tl].reshape(nb, l, d), g, dmods, dnorm_g


def _mesh_pos():
    return lax.axis_index("x"), lax.axis_index("y"), lax.axis_index("c")


def _remote(src, dst, send_sems, recv_sems, k, dev):
    return pltpu.make_async_remote_copy(src_ref=src, dst_ref=dst, send_sem=send_sems.at[k], recv_sem=recv_sems.at[k],
                                        device_id=dev, device_id_type=pl.DeviceIdType.MESH)


def _comm_call(body, xs, out_shapes, n_remote, n_local, name):
    hbm = pl.BlockSpec(memory_space=pltpu.HBM)
    return pl.pallas_call(
        body, in_specs=[hbm] * len(xs), out_specs=[hbm] * len(out_shapes), out_shape=out_shapes,
        scratch_shapes=_sem_shapes(n_remote, n_local),
        compiler_params=pltpu.CompilerParams(has_side_effects=True), name=name)(*xs)


def _run_side(side, name):
    n = len(side.xs)

    def body(*refs):
        side.start(refs[:n], refs[n:n + len(side.out_shapes)], *refs[n + len(side.out_shapes):])
        side.finish(refs[:n], refs[n:n + len(side.out_shapes)], *refs[n + len(side.out_shapes):])

    return _comm_call(body, side.xs, side.out_shapes, side.n_remote, side.n_local, name)


def _gather_side(xs):
    n = len(xs)

    def plan(x_refs, o_refs, send_sems, recv_sems, local_sems):
        x, y, c = _mesh_pos()
        me, sib = (x, y, c), (x, y, 1 - c)
        chips = [(1 - x, y), (x, 1 - y), (1 - x, 1 - y)]

        def slot(a, p):
            return o_refs[a].at[4 * p[0] + 2 * p[1] + p[2]]

        def copy(a, k, block, to, src=None):
            return _remote(slot(a, block) if src is None else src, slot(a, block), send_sems, recv_sems, 7 * a + k, to)

        mine = [pltpu.make_async_copy(x_refs[a], slot(a, me), local_sems.at[a]) for a in range(n)]
        first = []
        for a in range(n):
            first += [copy(a, 1 + j, me, chip + (c,), src=x_refs[a]) for j, chip in enumerate(chips)]
            first.append(copy(a, 0, me, sib, src=x_refs[a]))
        return me, sib, c, chips, copy, mine, first

    def start(x_refs, o_refs, send_sems, recv_sems, local_sems):
        _, _, _, _, _, mine, first = plan(x_refs, o_refs, send_sems, recv_sems, local_sems)
        for cp in mine + first:
            cp.start()

    def finish(x_refs, o_refs, send_sems, recv_sems, local_sems):
        me, sib, c, chips, copy, mine, first = plan(x_refs, o_refs, send_sems, recv_sems, local_sems)
        passed = []
        for j, chip in enumerate(chips):
            for a in range(n):
                copy(a, 1 + j, chip + (c,), me).wait_recv()
                passed.append(copy(a, 4 + j, chip + (c,), sib))
                passed[-1].start()
        for a in range(n):
            copy(a, 0, sib, me).wait_recv()
        for j, chip in enumerate(chips):
            for a in range(n):
                copy(a, 4 + j, chip + (1 - c,), me).wait_recv()
        for cp in first + passed:
            cp.wait_send()
        for cp in mine:
            cp.wait()

    return _Side(xs, [_sds((N_DEV,) + x.shape, x.dtype) for x in xs], 7 * n, n, start, finish)


def _gather_all(xs, name):
    return _run_side(_gather_side(xs), name)


def _swap_halves(xs, name):
    n = len(xs)

    def body(*refs):
        x_refs, o_refs, (send_sems, recv_sems, _) = refs[:n], refs[n:2 * n], refs[2 * n:]
        x, y, c = _mesh_pos()
        copies = [_remote(x_refs[a].at[q, 1 - c], o_refs[a].at[q], send_sems, recv_sems, 4 * a + q, (x, y, 1 - c))
                  for a in range(n) for q in range(4)]
        for cp in copies:
            cp.start()
        for cp in copies:
            cp.wait_recv()
        for cp in copies:
            cp.wait_send()

    return _comm_call(body, xs, [_sds((4,) + x.shape[2:], x.dtype) for x in xs], 4 * n, 0, name)


def _quad_side(xs):
    n = len(xs)

    def plan(x_refs, o_refs, send_sems, recv_sems, local_sems):
        x, y, c = _mesh_pos()
        q = 2 * x + y
        chips = [(1 - x, y), (x, 1 - y), (1 - x, 1 - y)]
        mine = [pltpu.make_async_copy(x_refs[a].at[q], o_refs[a].at[q], local_sems.at[a]) for a in range(n)]
        sends, arrivals = [], []
        for a in range(n):
            for j, chip in enumerate(chips):
                qj = 2 * chip[0] + chip[1]
                sends.append(_remote(x_refs[a].at[qj], o_refs[a].at[q], send_sems, recv_sems, 3 * a + j, chip + (c,)))
                arrivals.append(_remote(x_refs[a].at[qj], o_refs[a].at[qj], send_sems, recv_sems, 3 * a + j,
                                        chip + (c,)))
        return mine, sends, arrivals

    def start(*refs):
        mine, sends, _ = plan(*refs)
        for cp in mine + sends:
            cp.start()

    def finish(*refs):
        mine, sends, arrivals = plan(*refs)
        for cp in arrivals:
            cp.wait_recv()
        for cp in sends:
            cp.wait_send()
        for cp in mine:
            cp.wait()

    return _Side(xs, [_sds(x.shape, x.dtype) for x in xs], 3 * n, n, start, finish)


def _pair_add(xs, rs, name):
    n = len(xs)

    def body(*refs):
        c = lax.axis_index("c")
        for x_ref, r_ref, o_ref in zip(refs[:n], refs[n:2 * n], refs[2 * n:]):
            o_ref[...] = (x_ref[c].astype(F32) + r_ref[...].astype(F32)).astype(o_ref.dtype)

    slot = lambda x: pl.BlockSpec((None,) + x.shape[2:], lambda q: (q, 0, 0))
    return pl.pallas_call(
        body, grid=(4,),
        in_specs=[pl.BlockSpec((None, 2) + x.shape[2:], lambda q: (q, 0, 0, 0)) for x in xs] + [slot(x) for x in xs],
        out_specs=[slot(x) for x in xs], out_shape=[_sds((4,) + x.shape[2:], x.dtype) for x in xs],
        compiler_params=_params("parallel"), name=name)(*xs, *rs)


def _pack_rows(n):
    r = -(-n // PACK_COLS)
    return -(-r // 256) * 256 if r > 256 else -(-r // 16) * 16


def _pack(arrs, dtype):
    flat = jnp.concatenate([a.reshape(-1).astype(dtype) for a in arrs])
    rows = _pack_rows(flat.shape[0])
    return jnp.pad(flat, (0, rows * PACK_COLS - flat.shape[0])).reshape(rows, PACK_COLS)


def _pack_shards(arrs):
    flat = jnp.concatenate([a.astype(F32) for a in arrs], axis=1)
    rows = _pack_rows(flat.shape[1])
    return jnp.pad(flat, ((0, 0), (0, rows * PACK_COLS - flat.shape[1]))).reshape(N_DEV, rows, PACK_COLS)


def _unpack(packed, shapes, lead=()):
    flat = packed.reshape(tuple(lead) + (-1,))
    out, off = [], 0
    for s in shapes:
        n = 1
        for v in s:
            n *= v
        out.append(flat[..., off:off + n].reshape(tuple(lead) + tuple(s)))
        off += n
    return out


def _to_shards(full, ax):
    s = full.shape
    t = full.reshape(s[:ax] + (N_DEV, s[ax] // N_DEV) + s[ax + 1:])
    return jnp.moveaxis(t, ax, 0).reshape(N_DEV, -1)


def _from_shards(shards, local_shape, ax):
    t = jnp.moveaxis(shards.reshape((N_DEV,) + tuple(local_shape)), 0, ax)
    s = t.shape
    return t.reshape(s[:ax] + (s[ax] * s[ax + 1],) + s[ax + 2:])


def _mod_fwd(c_rows, w_mod, b_mod, name):
    nl, d, n = w_mod.shape
    r = c_rows.shape[0]

    def body(c_ref, w_ref, b_ref, o_ref):
        s = _silu(c_ref[...])
        for l in range(nl):
            o_ref[l] = _dot(s, w_ref[l]) + b_ref[l]

    return pl.pallas_call(body, out_shape=_sds((nl, r, n)),
                          compiler_params=pltpu.CompilerParams(vmem_limit_bytes=VMEM_LIMIT), name=name)(
        c_rows, w_mod, b_mod)


def _mod_bwd(c_rows, dcols, dall, w_mod, c_ctx, name):
    nl, d, n = w_mod.shape
    r = c_rows.shape[0]

    def body(c_ref, dc_ref, da_ref, w_ref, cc_ref, gw_ref, gb_ref, gc_ref):
        s = _silu(c_ref[...])
        ds = jnp.zeros((r, d), F32)
        for l in range(nl):
            gw_ref[l] = _dot_tn(s, dc_ref[l])
            gb_ref[l] = _rowsum(da_ref[l])
            ds = ds + _dot_nt(dc_ref[l], w_ref[l])
        row = lax.broadcasted_iota(jnp.int32, (r, d), 0)
        gc_ref[...] = _rowsum(jnp.where(row % 4 >= 2, ds, 0.0)) * _dsilu(cc_ref[...])

    return pl.pallas_call(body, out_shape=[_sds((nl, d, n)), _sds((nl, 1, 3 * d)), _sds((1, d))],
                          compiler_params=pltpu.CompilerParams(vmem_limit_bytes=VMEM_LIMIT), name=name)(
        c_rows, dcols, dall, w_mod, c_ctx)


def _adam_math(w, gsum, m, v):
    c1, c2 = 1.0 - ADAM_B1 ** ADAM_STEP, 1.0 - ADAM_B2 ** ADAM_STEP
    mn = ADAM_B1 * m + (1.0 - ADAM_B1) * gsum
    vn = ADAM_B2 * v + (1.0 - ADAM_B2) * (gsum * gsum)
    return -ADAM_LR * ((mn / c1) / (jnp.sqrt(vn / c2) + ADAM_EPS) + ADAM_WD * w), mn, vn


def _adam(w, gparts, row0, m, v, name):
    rows, cols = w.shape
    npart = gparts.shape[0]
    if rows % 8:
        tr = rows
        assert row0 == 0 and gparts.shape[1] == rows
    else:
        tr = max(t for t in (512, 256, 128, 64, 32, 16, 8) if rows % t == 0 and row0 % t == 0
                 and (t * cols <= 256 * 1024 or t == 8))

    def body(w_ref, g_ref, m_ref, v_ref, go_ref, d_ref, mo_ref, vo_ref):
        gsum = g_ref[0].astype(F32)
        for p in range(1, npart):
            gsum = gsum + g_ref[p].astype(F32)
        go_ref[...] = gsum
        d_ref[...], mo_ref[...], vo_ref[...] = _adam_math(w_ref[...], gsum, m_ref[...], v_ref[...])

    spec = _row(tr, cols)
    return pl.pallas_call(
        body, grid=(rows // tr,),
        in_specs=[spec, pl.BlockSpec((npart, tr, cols), lambda i: (0, row0 // tr + i, 0)), spec, spec],
        out_specs=[spec] * 4, out_shape=[_sds((rows, cols))] * 4, compiler_params=_params("parallel"),
        name=name)(w, gparts, m, v)


INPUTS = ['x', 'c', 'ctx'] + WEIGHTS + ['loss_target'] + ['m_' + n for n in WEIGHTS] + ['v_' + n for n in WEIGHTS]
AXES = ("x", "y", "c")
LAYER_MATS = (('cv_w_in', 'cv_w_out'), ('pl_w_in', 'pl_w_grp', 'pl_w_out'),
              ('ml_w_in', 'ml_w_uq', 'ml_w_ukv', 'ml_w_out'), ('ch_w_in', 'ch_w_out'))
GATHERS = {'fwd1': LAYER_MATS[1], 'fwd2': ('ml_w_in', 'ml_w_uq'), 'fwd3': ('ml_w_ukv', 'ml_w_out'), 'fwd4': LAYER_MATS[3]}
GRAD_GROUPS = (('cv_w_in',), ('pl_w_in', 'pl_w_grp', 'pl_w_out', 'cv_w_out'), LAYER_MATS[2], LAYER_MATS[3])
KINDS = ('grad_', 'delta_', 'new_m_', 'new_v_')


def _squeeze_layer(name, a):
    return a if name == 'norm_g' or a.ndim < 3 else a[0]


def _as2d(a):
    return a.reshape(-1, a.shape[-1])


class _Exchanges:
    def __init__(self, a, w):
        self.a, self.w, self.sums, self.quad = a, w, {}, {}

    def mats(self, names):
        return [_as2d(self.a[n]).astype(BF16) for n in names]

    def take_weights(self, names, bufs):
        for n, buf in zip(names, bufs):
            self.w[n] = _squeeze_layer(n, _from_shards(buf, self.a[n].shape, SHARD_AXIS[n]))

    def side(self, tag):
        return _gather_side(self.mats(GATHERS[tag])) if tag in GATHERS else _quad_side(self.sums[int(tag[-1])])

    def done(self, tag, brought):
        if tag in GATHERS:
            self.take_weights(GATHERS[tag], brought)
        else:
            self.quad[int(tag[-1])] = brought

    def shard_major(self, n, gn):
        if gn.ndim == 3 and gn.shape[0] == N_DEV and gn.dtype == BF16:
            return gn
        whole = tuple(N_DEV * s if i == SHARD_AXIS[n] else s for i, s in enumerate(self.a[n].shape))
        return _to_shards(gn.reshape(whole), SHARD_AXIS[n]).reshape((N_DEV,) + _as2d(self.a[n]).shape)

    def grads_ready(self, group, g):
        bufs = [self.shard_major(n, g[n]).astype(BF16) for n in GRAD_GROUPS[group]]
        if group == 0:
            bufs.append(_pack_shards([self.shard_major(n, g[n]).reshape(N_DEV, -1) for n in VECTOR_WEIGHTS]))
        bufs = [b.reshape((4, 2) + b.shape[1:]) for b in bufs]
        got = _swap_halves(bufs, "grads_swap_cores_%d" % group)
        self.sums[group] = _pair_add(bufs, got, "grads_add_cores_%d" % group)


def _train_step(a):
    x, c, ctx, tgt = a['x'], a['c'], a['ctx'], a['loss_target']
    d = x.shape[-1]
    nb = x.shape[0]
    dev = 4 * lax.axis_index("x") + 2 * lax.axis_index("y") + lax.axis_index("c")
    local_shape = {n: a[n].shape for n in WEIGHTS}

    w = {n: _squeeze_layer(n, a[n]) for n in WEIGHTS if SHARD_AXIS[n] is None}
    comm = _Exchanges(a, w)
    vec_names = ['c'] + VECTOR_WEIGHTS
    gathered = _gather_all(comm.mats(LAYER_MATS[0]) + [_pack([a[n] for n in vec_names], F32)], "gather_first")
    comm.take_weights(LAYER_MATS[0], gathered[:-1])
    parts = dict(zip(vec_names, _unpack(gathered[-1], [a[n].shape for n in vec_names], lead=(N_DEV,))))
    for n in VECTOR_WEIGHTS:
        w[n] = _squeeze_layer(n, _from_shards(parts[n], local_shape[n], SHARD_AXIS[n]))
    c_all = parts['c'].reshape(N_DEV * nb, d)
    c_ctx = a['c_ctx'].reshape(1, d)

    w_mod = a['w_mod']
    nl, ncol = w_mod.shape[0], w_mod.shape[2]
    mod_rows = -(-(N_DEV * nb + 1) // 8) * 8
    c_rows = jnp.concatenate([c_all, c_ctx, jnp.zeros((mod_rows - N_DEV * nb - 1, d), F32)], axis=0)
    b_loc = lax.dynamic_slice(a['b_mod'], (0, dev * ncol), (nl, ncol))[:, None, :]
    mod_loc = _mod_fwd(c_rows, w_mod, b_loc, "mod_fwd")
    mod_all, = _gather_all([mod_loc.reshape(nl * mod_rows, ncol)], "gather_mods")
    mod_all = mod_all.reshape(N_DEV, nl, mod_rows, ncol).transpose(1, 2, 0, 3).reshape(nl, mod_rows, N_DEV * ncol)
    ctx_row = mod_all[:, N_DEV * nb:N_DEV * nb + 1]
    mods = jnp.concatenate([lax.dynamic_slice(mod_all, (0, dev * nb, 0), (nl, nb, 3 * d))] + [ctx_row] * nb, axis=1)

    loss_vec, grad_x, g, dmods, dnorm_g = _local_step(x, ctx, tgt, w, mods, comm)
    loss = lax.psum(jnp.sum(loss_vec), AXES)

    nseg = dmods.shape[1]
    dm_all, = _gather_all([dmods.reshape(nl * nseg, 3 * d)], "gather_dmods")
    dm_all = dm_all.reshape(N_DEV, nl, nseg, 3 * d).transpose(1, 0, 2, 3).reshape(nl, N_DEV * nseg, 3 * d)
    dcols = lax.dynamic_slice(dm_all, (0, 0, dev * ncol), (nl, N_DEV * nseg, ncol))
    c_rows_b = jnp.concatenate([c_all.reshape(N_DEV, nb, d), jnp.broadcast_to(c_ctx, (N_DEV, nb, d))], axis=1)
    g_w_mod, g_b_mod, g_c_ctx = _mod_bwd(c_rows_b.reshape(N_DEV * nseg, d), dcols, dm_all, w_mod, c_ctx, "mod_bwd")
    g['c_ctx'], g['norm_g'] = g_c_ctx, dnorm_g
    rep_all, = _gather_all([_pack([g[n] for n in REPLICATED], F32)], "gather_replicated_grads")

    out = {}

    def keep(names, res, shapes=None):
        for kind, val in zip(KINDS, res):
            if shapes is None:
                out[kind + names[0]] = val.reshape(local_shape[names[0]])
            else:
                for n, leaf in zip(names, _unpack(val, shapes)):
                    out[kind + n] = leaf

    def update_packed(names, gparts, tag):
        res = _adam(_pack([a[n] for n in names], F32), gparts, 0, _pack([a['m_' + n] for n in names], F32),
                    _pack([a['v_' + n] for n in names], F32), "adam_" + tag)
        keep(names, res, [local_shape[n] for n in names])

    for group, names in enumerate(GRAD_GROUPS):
        for n, gparts in zip(names, comm.quad[group]):
            keep([n], _adam(_as2d(a[n]), gparts, 0, _as2d(a['m_' + n]), _as2d(a['v_' + n]), "adam_" + n))
    update_packed(VECTOR_WEIGHTS, comm.quad[0][-1], "vectors")
    update_packed(REPLICATED, rep_all, "replicated")
    keep(['w_mod'], _adam(_as2d(w_mod), _as2d(g_w_mod)[None], 0, _as2d(a['m_w_mod']), _as2d(a['v_w_mod']),
                          "adam_w_mod"))
    keep(['b_mod'], _adam(a['b_mod'], g_b_mod.reshape((1,) + a['b_mod'].shape), 0, a['m_b_mod'], a['v_b_mod'],
                          "adam_b_mod"))
    return (loss, grad_x) + tuple(out[kind + n] for kind in KINDS for n in WEIGHTS)


def kernel(x, c, ctx, c_ctx, norm_g, w_mod, b_mod, cv_w_in, cv_dw, cv_db, cv_ln_g, cv_ln_b, cv_w_out, pl_w_in, pl_w_grp, pl_scale, pl_w_out, ml_w_in, ml_q_norm, ml_kv_norm, ml_w_uq, ml_w_ukv, ml_nope_norm, ml_rope_norm, ml_w_out, ch_w_in, ch_ln_g, ch_ln_b, ch_w_s, ch_b_s, ch_w_out, loss_target, m_c_ctx, m_norm_g, m_w_mod, m_b_mod, m_cv_w_in, m_cv_dw, m_cv_db, m_cv_ln_g, m_cv_ln_b, m_cv_w_out, m_pl_w_in, m_pl_w_grp, m_pl_scale, m_pl_w_out, m_ml_w_in, m_ml_q_norm, m_ml_kv_norm, m_ml_w_uq, m_ml_w_ukv, m_ml_nope_norm, m_ml_rope_norm, m_ml_w_out, m_ch_w_in, m_ch_ln_g, m_ch_ln_b, m_ch_w_s, m_ch_b_s, m_ch_w_out, v_c_ctx, v_norm_g, v_w_mod, v_b_mod, v_cv_w_in, v_cv_dw, v_cv_db, v_cv_ln_g, v_cv_ln_b, v_cv_w_out, v_pl_w_in, v_pl_w_grp, v_pl_scale, v_pl_w_out, v_ml_w_in, v_ml_q_norm, v_ml_kv_norm, v_ml_w_uq, v_ml_w_ukv, v_ml_nope_norm, v_ml_rope_norm, v_ml_w_out, v_ch_w_in, v_ch_ln_g, v_ch_ln_b, v_ch_w_s, v_ch_b_s, v_ch_w_out):
    return _train_step(dict(zip(INPUTS, (x, c, ctx, c_ctx, norm_g, w_mod, b_mod, cv_w_in, cv_dw, cv_db, cv_ln_g, cv_ln_b, cv_w_out, pl_w_in, pl_w_grp, pl_scale, pl_w_out, ml_w_in, ml_q_norm, ml_kv_norm, ml_w_uq, ml_w_ukv, ml_nope_norm, ml_rope_norm, ml_w_out, ch_w_in, ch_ln_g, ch_ln_b, ch_w_s, ch_b_s, ch_w_out, loss_target, m_c_ctx, m_norm_g, m_w_mod, m_b_mod, m_cv_w_in, m_cv_dw, m_cv_db, m_cv_ln_g, m_cv_ln_b, m_cv_w_out, m_pl_w_in, m_pl_w_grp, m_pl_scale, m_pl_w_out, m_ml_w_in, m_ml_q_norm, m_ml_kv_norm, m_ml_w_uq, m_ml_w_ukv, m_ml_nope_norm, m_ml_rope_norm, m_ml_w_out, m_ch_w_in, m_ch_ln_g, m_ch_ln_b, m_ch_w_s, m_ch_b_s, m_ch_w_out, v_c_ctx, v_norm_g, v_w_mod, v_b_mod, v_cv_w_in, v_cv_dw, v_cv_db, v_cv_ln_g, v_cv_ln_b, v_cv_w_out, v_pl_w_in, v_pl_w_grp, v_pl_scale, v_pl_w_out, v_ml_w_in, v_ml_q_norm, v_ml_kv_norm, v_ml_w_uq, v_ml_w_ukv, v_ml_nope_norm, v_ml_rope_norm, v_ml_w_out, v_ch_w_in, v_ch_ln_g, v_ch_ln_b, v_ch_w_s, v_ch_b_s, v_ch_w_out))))
```

```python
import functools

import jax
import jax.numpy as jnp
from jax import lax
from jax.experimental import pallas as pl
from jax.experimental.pallas import tpu as pltpu

F32 = jnp.float32
BF16 = jnp.bfloat16

N_DEV = 8
EPS = 1e-6
CONV_WIDTH = 31
CONV_PAD = 16
POOL_WINDOWS = (2, 4, 8, 16)
POOL_TAPS = 16
MLA_HEADS = 8
MLA_NOPE = 128
MLA_ROPE = 64
MLA_Q_RANK = 384
MLA_KV_RANK = 256
MLA_SCALE = (MLA_NOPE + MLA_ROPE) ** -0.5
ROPE_THETA = 10000.0
GRID_W = 64
HEAD_W = 256
ATTN_CHAINS = 2
CHUNK = 128
CHUNK_GROUPS = 8
CHUNK_TILES = (512, 256, 128)
ADAM_LR = 0.001
ADAM_B1 = 0.9
ADAM_B2 = 0.999
ADAM_EPS = 1e-08
ADAM_WD = 0.01
ADAM_STEP = 10
LANES = 128
VMEM_LIMIT = 56 * 1024 * 1024
PACK_COLS = 1024

WEIGHTS = ['c_ctx', 'norm_g', 'w_mod', 'b_mod', 'cv_w_in', 'cv_dw', 'cv_db', 'cv_ln_g', 'cv_ln_b', 'cv_w_out',
           'pl_w_in', 'pl_w_grp', 'pl_scale', 'pl_w_out', 'ml_w_in', 'ml_q_norm', 'ml_kv_norm', 'ml_w_uq',
           'ml_w_ukv', 'ml_nope_norm', 'ml_rope_norm', 'ml_w_out', 'ch_w_in', 'ch_ln_g', 'ch_ln_b', 'ch_w_s',
           'ch_b_s', 'ch_w_out']
SHARD_AXIS = {'c_ctx': None, 'norm_g': None, 'w_mod': 2, 'b_mod': None, 'cv_w_in': 2, 'cv_dw': 2, 'cv_db': None,
              'cv_ln_g': None, 'cv_ln_b': None, 'cv_w_out': 1, 'pl_w_in': 2, 'pl_w_grp': 2, 'pl_scale': 1,
              'pl_w_out': 1, 'ml_w_in': 2, 'ml_q_norm': 1, 'ml_kv_norm': 1, 'ml_w_uq': 2, 'ml_w_ukv': 2,
              'ml_nope_norm': None, 'ml_rope_norm': None, 'ml_w_out': 1, 'ch_w_in': 2, 'ch_ln_g': 1, 'ch_ln_b': 1,
              'ch_w_s': None, 'ch_b_s': None, 'ch_w_out': 1}
MATMUL_WEIGHTS = ['cv_w_in', 'cv_w_out', 'pl_w_in', 'pl_w_grp', 'pl_w_out', 'ml_w_in', 'ml_w_uq', 'ml_w_ukv',
                  'ml_w_out', 'ch_w_in', 'ch_w_out']
VECTOR_WEIGHTS = ['cv_dw', 'pl_scale', 'ml_q_norm', 'ml_kv_norm', 'ch_ln_g', 'ch_ln_b']
EXCHANGED = MATMUL_WEIGHTS[:1] + ['cv_dw'] + MATMUL_WEIGHTS[1:4] + ['pl_scale'] + MATMUL_WEIGHTS[4:6] + [
    'ml_q_norm', 'ml_kv_norm'] + MATMUL_WEIGHTS[6:10] + ['ch_ln_g', 'ch_ln_b', 'ch_w_out']
REPLICATED = ['c_ctx', 'norm_g', 'cv_db', 'cv_ln_g', 'cv_ln_b', 'ml_nope_norm', 'ml_rope_norm', 'ch_w_s', 'ch_b_s']


def _pick(n, cands):
    for c in cands:
        if n % c == 0:
            return c
    raise ValueError(f"no tile for {n} among {cands}")


def _params(*sem):
    return pltpu.CompilerParams(dimension_semantics=sem, vmem_limit_bytes=VMEM_LIMIT)


def _sig(x):
    return 1.0 / (1.0 + jnp.exp(-x))


def _silu(x):
    return x * _sig(x)


def _dsilu(x):
    s = _sig(x)
    return s * (1.0 + x * (1.0 - s))


def _rowsum(v):
    return jnp.sum(v, axis=0, keepdims=True)


def _dot(a, b):
    return jnp.dot(a.astype(BF16), b.astype(BF16), preferred_element_type=F32)


def _dot_nt(a, b):
    return lax.dot_general(a.astype(BF16), b.astype(BF16), (((1,), (1,)), ((), ())), preferred_element_type=F32)


def _dot_tn(a, b):
    return lax.dot_general(a.astype(BF16), b.astype(BF16), (((0,), (0,)), ((), ())), preferred_element_type=F32)


class _Segs:
    def __init__(self, lens, tm):
        self.lens, self.tm, self.n = tuple(lens), tm, len(lens)
        self.starts, s = [], 0
        for l in lens:
            assert l % tm == 0
            self.starts.append(s // tm)
            s += l
        self.rows, self.tiles = s, s // tm

    def seg(self, i):
        r = 0
        for st in self.starts[1:]:
            r = r + jnp.where(i >= st, 1, 0)
        return r

    def is_first(self, i):
        f = i == 0
        for st in self.starts[1:]:
            f = jnp.logical_or(f, i == st)
        return f

    def spec(self, cols):
        return pl.BlockSpec((None, 1, cols), lambda i: (self.seg(i), 0, 0))


def _row(tm, cols, cb=0):
    return pl.BlockSpec((tm, cols), lambda i: (i, cb))


def _const(shape):
    return pl.BlockSpec(shape, lambda *_: (0,) * len(shape))


def _sds(shape, dtype=F32):
    return jax.ShapeDtypeStruct(shape, dtype)


class _Side:
    def __init__(self, xs, out_shapes, n_remote, n_local, start, finish):
        self.xs, self.out_shapes, self.n_remote, self.n_local = list(xs), list(out_shapes), n_remote, n_local
        self.start, self.finish = start, finish


def _sem_shapes(n_remote, n_local):
    return [pltpu.SemaphoreType.DMA((n_remote,)), pltpu.SemaphoreType.DMA((n_remote,)),
            pltpu.SemaphoreType.DMA((max(n_local, 1),))]


def _pallas(body, args, *, grid, in_specs, out_specs, out_shape, sem, name, scratch_shapes=(), side=None, into=None):
    aliases = {}
    if into:
        inner, n_args = body, len(args)

        def body(*refs):
            inner(*refs[:n_args], *refs[n_args + len(into):])

        aliases = {n_args + k: o for k, o in enumerate(sorted(into))}
        args = tuple(args) + tuple(into[o] for o in sorted(into))
        in_specs = list(in_specs) + [pl.BlockSpec(memory_space=pl.ANY)] * len(into)
    if side is None:
        return pl.pallas_call(body, grid=grid, in_specs=in_specs, out_specs=out_specs, out_shape=out_shape,
                              scratch_shapes=list(scratch_shapes), input_output_aliases=aliases,
                              compiler_params=_params(*sem), name=name)(*args)
    multi = isinstance(out_shape, (list, tuple))
    out_specs, out_shape = (list(out_specs), list(out_shape)) if multi else ([out_specs], [out_shape])
    ni, no, ns, si, so = len(in_specs), len(out_specs), len(scratch_shapes), len(side.xs), len(side.out_shapes)
    hbm = pl.BlockSpec(memory_space=pltpu.HBM)

    def wrapped(*refs):
        ins, sins, refs = refs[:ni], refs[ni:ni + si], refs[ni + si:]
        outs, souts, refs = refs[:no], refs[no:no + so], refs[no + so:]
        scr, sems = refs[:ns], refs[ns:]
        ids = [pl.program_id(k) for k in range(len(grid))]
        first = functools.reduce(jnp.logical_and, [i == 0 for i in ids])
        last = functools.reduce(jnp.logical_and, [i == n - 1 for i, n in zip(ids, grid)])

        @pl.when(first)
        def _():
            side.start(sins, souts, *sems)

        body(*ins, *outs, *scr)

        @pl.when(last)
        def _():
            side.finish(sins, souts, *sems)

    res = pl.pallas_call(
        wrapped, grid=grid, in_specs=list(in_specs) + [hbm] * si, out_specs=out_specs + [hbm] * so,
        out_shape=out_shape + side.out_shapes,
        scratch_shapes=list(scratch_shapes) + _sem_shapes(side.n_remote, side.n_local), input_output_aliases=aliases,
        compiler_params=pltpu.CompilerParams(dimension_semantics=("arbitrary",) * len(grid),
                                             vmem_limit_bytes=VMEM_LIMIT, has_side_effects=True),
        name=name)(*args, *side.xs)
    return (list(res[:no]) if multi else res[0]), list(res[no:])


N_TILES = (1024, 896, 768, 512, 384, 256, 128)
M_TILES = (1536, 1024, 768, 512, 256, 128)


def _mm(a, b, name, out_dtype=F32, rows=None, side=None, resid=None):
    m, k, n = rows or a.shape[0], a.shape[1], b.shape[1]
    tm, tn = _pick(m, M_TILES), _pick(n, N_TILES)
    if resid is not None:
        x, gt, segs = resid
        pieces = tm // segs.tm

        def body(a_ref, b_ref, x_ref, *rest):
            gt_refs, (o_ref, y_ref) = rest[:pieces], rest[pieces:]
            o = _dot(a_ref[...], b_ref[...])
            o_ref[...] = o
            for c in range(pieces):
                rows = slice(c * segs.tm, (c + 1) * segs.tm)
                y_ref[rows, :] = x_ref[rows, :] + gt_refs[c][...] * o[rows, :]

        tile = pl.BlockSpec((tm, tn), lambda j, i: (i, j))
        gt_specs = [pl.BlockSpec((None, 1, tn), lambda j, i, c=c: (segs.seg(i * pieces + c), 0, j))
                    for c in range(pieces)]
        return pl.pallas_call(
            body, grid=(n // tn, m // tm),
            in_specs=[pl.BlockSpec((tm, k), lambda j, i: (i, 0)), pl.BlockSpec((k, tn), lambda j, i: (0, j)), tile]
            + gt_specs, out_specs=[tile, tile], out_shape=[_sds((m, n)), _sds((m, n))],
            compiler_params=_params("parallel", "parallel"), name=name)(a, b, x, *([gt] * pieces))

    def body(a_ref, b_ref, o_ref):
        o_ref[...] = _dot(a_ref[...], b_ref[...]).astype(o_ref.dtype)

    return _pallas(
        body, (a, b), grid=(n // tn, m // tm),
        in_specs=[pl.BlockSpec((tm, k), lambda j, i: (i, 0)), pl.BlockSpec((k, tn), lambda j, i: (0, j))],
        out_specs=pl.BlockSpec((tm, tn), lambda j, i: (i, j)), out_shape=_sds((m, n), out_dtype),
        sem=("parallel", "parallel"), name=name, side=side)


def _mm_nt(a, b, name, out_dtype=F32, side=None):
    m, k, n = a.shape[0], a.shape[1], b.shape[0]
    tm, tn = _pick(m, (512, 256, 128)), _pick(n, N_TILES)

    def body(a_ref, b_ref, o_ref):
        o_ref[...] = _dot_nt(a_ref[...], b_ref[...]).astype(o_ref.dtype)

    return _pallas(
        body, (a, b), grid=(n // tn, m // tm),
        in_specs=[pl.BlockSpec((tm, k), lambda j, i: (i, 0)), pl.BlockSpec((tn, k), lambda j, i: (j, 0))],
        out_specs=pl.BlockSpec((tm, tn), lambda j, i: (i, j)), out_shape=_sds((m, n), out_dtype),
        sem=("parallel", "parallel"), name=name, side=side)


def _mm_nt_sum(pairs, name):
    m, n = pairs[0][0].shape[0], pairs[0][1].shape[0]
    tm, tn = _pick(m, (512, 256, 128)), _pick(n, N_TILES)
    tiles = [a.shape[0] // tm for a, _ in pairs]
    assert all(a.shape[0] % tm == 0 for a, _ in pairs)

    def body(*refs):
        o_ref, i = refs[-1], pl.program_id(1)
        acc = _dot_nt(refs[0][...], refs[1][...])
        for p in range(1, len(pairs)):
            acc = acc + jnp.where(i < tiles[p], _dot_nt(refs[2 * p][...], refs[2 * p + 1][...]), 0.0)
        o_ref[...] = acc

    in_specs = []
    for (a, b), nt in zip(pairs, tiles):
        in_specs += [pl.BlockSpec((tm, a.shape[1]), lambda j, i, nt=nt: (jnp.minimum(i, nt - 1), 0)),
                     pl.BlockSpec((tn, a.shape[1]), lambda j, i: (j, 0))]
    return pl.pallas_call(
        body, grid=(n // tn, m // tm), in_specs=in_specs, out_specs=pl.BlockSpec((tm, tn), lambda j, i: (i, j)),
        out_shape=_sds((m, n)), compiler_params=_params("parallel", "parallel"),
        name=name)(*[t for pair in pairs for t in pair])


def _mm_tn(a, b, name, rows=None, shards=None, side=None):
    t, k, n = rows or a.shape[0], a.shape[1], b.shape[1]
    tk, tt = _pick(k, N_TILES), _pick(t, M_TILES)
    if shards:
        width = n // shards
        per_tile = max(c for c in (8, 4, 2, 1) if shards % c == 0 and c * width <= N_TILES[0])
        tn = per_tile * width
    else:
        tn = _pick(n, N_TILES)
    assert tn % LANES == 0

    def body(a_ref, b_ref, o_ref, acc_ref):
        @pl.when(pl.program_id(2) == 0)
        def _():
            acc_ref[...] = jnp.zeros_like(acc_ref)

        acc_ref[...] += _dot_tn(a_ref[...], b_ref[...])

        @pl.when(pl.program_id(2) == pl.num_programs(2) - 1)
        def _():
            if shards:
                for c in range(per_tile):
                    o_ref[c] = acc_ref[:, c * width:(c + 1) * width].astype(o_ref.dtype)
            else:
                o_ref[...] = acc_ref[...]

    if shards:
        out_spec = pl.BlockSpec((per_tile, tk, width), lambda i, j, s: (j, i, 0))
        out_shape = _sds((shards, k, width), BF16)
    else:
        out_spec, out_shape = pl.BlockSpec((tk, tn), lambda i, j, s: (i, j)), _sds((k, n))
    return _pallas(
        body, (a, b), grid=(k // tk, n // tn, t // tt),
        in_specs=[pl.BlockSpec((tt, tk), lambda i, j, s: (s, i)), pl.BlockSpec((tt, tn), lambda i, j, s: (s, j))],
        out_specs=out_spec, out_shape=out_shape, scratch_shapes=[pltpu.VMEM((tk, tn), F32)],
        sem=("parallel", "parallel", "arbitrary"), name=name, side=side)


def _rms_mod_fwd(x, g, sc, sh, segs, name):
    d, tm = x.shape[1], segs.tm

    def body(x_ref, g_ref, sc_ref, sh_ref, h_ref):
        xf = x_ref[...]
        r = lax.rsqrt(jnp.mean(xf * xf, axis=-1, keepdims=True) + EPS)
        h_ref[...] = ((xf * r * g_ref[...]) * (1.0 + sc_ref[...]) + sh_ref[...]).astype(h_ref.dtype)

    return pl.pallas_call(
        body, grid=(segs.tiles,), in_specs=[_row(tm, d), _const((1, d)), segs.spec(d), segs.spec(d)],
        out_specs=_row(tm, d), out_shape=_sds((segs.rows, d), BF16), compiler_params=_params("parallel"),
        name=name)(x, g, sc, sh)


def _rms_mod_bwd(x, g, sc, sh, dh, dxr, segs, name, dx_rows=None):
    d, tm = x.shape[1], segs.tm
    dxr_tiles = dxr.shape[0] // tm
    dx_tiles = (dx_rows or segs.rows) // tm

    def body(x_ref, g_ref, sc_ref, sh_ref, dh_ref, dxr_ref, dx_ref, dg_ref, dsc_ref, dsh_ref):
        i = pl.program_id(0)

        @pl.when(i == 0)
        def _():
            dg_ref[...] = jnp.zeros_like(dg_ref)

        @pl.when(segs.is_first(i))
        def _():
            dsc_ref[...] = jnp.zeros_like(dsc_ref)
            dsh_ref[...] = jnp.zeros_like(dsh_ref)

        xf, gg, dhf = x_ref[...], g_ref[...], dh_ref[...].astype(F32)
        dxr = jnp.where(i < dxr_tiles, dxr_ref[...], 0.0)
        r = lax.rsqrt(jnp.mean(xf * xf, axis=-1, keepdims=True) + EPS)
        xh = xf * r
        dsh_ref[...] += _rowsum(dhf)
        dsc_ref[...] += _rowsum(dhf * (xh * gg))
        du = dhf * (1.0 + sc_ref[...])
        dg_ref[...] += _rowsum(du * xh)
        dxh = du * gg
        @pl.when(i < dx_tiles)
        def _():
            dx_ref[...] = dxr + r * (dxh - xh * jnp.mean(dxh * xh, axis=-1, keepdims=True))

    return pl.pallas_call(
        body, grid=(segs.tiles,),
        in_specs=[_row(tm, d), _const((1, d)), segs.spec(d), segs.spec(d), _row(tm, d),
                  pl.BlockSpec((tm, d), lambda i: (jnp.minimum(i, dxr_tiles - 1), 0))],
        out_specs=[pl.BlockSpec((tm, d), lambda i: (jnp.minimum(i, dx_tiles - 1), 0)), _const((1, d)), segs.spec(d),
                   segs.spec(d)],
        out_shape=[_sds((dx_tiles * tm, d)), _sds((1, d)), _sds((segs.n, 1, d)), _sds((segs.n, 1, d))],
        compiler_params=_params("arbitrary"), name=name)(x, g, sc, sh, dh, dxr)


def _resid_fwd(x, o, gt, segs, name):
    d, tm = x.shape[1], segs.tm

    def body(x_ref, o_ref, gt_ref, y_ref):
        y_ref[...] = x_ref[...] + gt_ref[...] * o_ref[...]

    return pl.pallas_call(
        body, grid=(segs.tiles,), in_specs=[_row(tm, d), _row(tm, d), segs.spec(d)], out_specs=_row(tm, d),
        out_shape=_sds((segs.rows, d)), compiler_params=_params("parallel"), name=name)(x, o, gt)


def _resid_bwd(dxn, o, gt, segs, name):
    d, tm = o.shape[1], segs.tm

    def body(dxn_ref, o_ref, gt_ref, do_ref, dgt_ref):
        @pl.when(segs.is_first(pl.program_id(0)))
        def _():
            dgt_ref[...] = jnp.zeros_like(dgt_ref)

        dx = dxn_ref[...]
        do_ref[...] = (gt_ref[...] * dx).astype(do_ref.dtype)
        dgt_ref[...] += _rowsum(dx * o_ref[...])

    return pl.pallas_call(
        body, grid=(segs.tiles,), in_specs=[_row(tm, d), _row(tm, d), segs.spec(d)],
        out_specs=[_row(tm, d), segs.spec(d)], out_shape=[_sds((segs.rows, d), BF16), _sds((segs.n, 1, d))],
        compiler_params=_params("arbitrary"), name=name)(dxn, o, gt)


def _loss_head(y, tgt, tm, name):
    t, d = y.shape

    def body(y_ref, t_ref, l_ref, dy_ref):
        @pl.when(pl.program_id(0) == 0)
        def _():
            l_ref[...] = jnp.zeros_like(l_ref)

        e = y_ref[...] - t_ref[...]
        dy_ref[...] = e * (1.0 / d)
        l_ref[...] += _rowsum(e * e) * (0.5 / d)

    return pl.pallas_call(
        body, grid=(t // tm,), in_specs=[_row(tm, d), _row(tm, d)], out_specs=[_const((1, d)), _row(tm, d)],
        out_shape=[_sds((1, d)), _sds((t, d))], compiler_params=_params("arbitrary"), name=name)(y, tgt)


def _seq_spec(l, ce, row0, cb0=0):
    return pl.BlockSpec((l, ce), lambda j, s: (row0 // l + s, cb0 + j))


def _tap_sum(pad_ref, taps_ref, first_row, n_taps, l, ce, flip):
    out = []
    for r0 in range(0, l, CHUNK):
        rows = min(CHUNK, l - r0)
        acc = jnp.zeros((rows, ce), F32)
        for k in range(n_taps):
            kk = n_taps - 1 - k if flip else k
            acc = acc + pad_ref[pl.ds(first_row + r0 + k, rows), :] * taps_ref[kk:kk + 1, :]
        out.append(acc)
    return out


def _fill_pad(pad_ref, val, l, ce):
    pad_ref[pl.ds(0, CONV_PAD), :] = jnp.zeros((CONV_PAD, ce), F32)
    pad_ref[pl.ds(CONV_PAD + l, CONV_PAD), :] = jnp.zeros((CONV_PAD, ce), F32)
    pad_ref[pl.ds(CONV_PAD, l), :] = val


def _conv1_fwd(z, dw, db, nseq, l, row0, name, side=None, into=None):
    e = z.shape[1] // 3
    ce = LANES
    half = CONV_WIDTH // 2

    def body(a_ref, b_ref, dw_ref, db_ref, y_ref, pad_ref):
        _fill_pad(pad_ref, a_ref[...] * _sig(b_ref[...]), l, ce)
        pieces = _tap_sum(pad_ref, dw_ref, CONV_PAD - half, CONV_WIDTH, l, ce, False)
        for n, acc in enumerate(pieces):
            y_ref[pl.ds(n * CHUNK, acc.shape[0]), :] = acc + db_ref[...]

    return _pallas(
        body, (z, z, dw, db), grid=(e // ce, nseq),
        in_specs=[_seq_spec(l, ce, row0), _seq_spec(l, ce, row0, e // ce),
                  pl.BlockSpec((CONV_WIDTH, ce), lambda j, s: (0, j)), pl.BlockSpec((1, ce), lambda j, s: (0, j))],
        out_specs=_seq_spec(l, ce, row0), out_shape=_sds((z.shape[0], e)),
        scratch_shapes=[pltpu.VMEM((l + 2 * CONV_PAD, ce), F32)],
        sem=("parallel", "arbitrary"), name=name, side=side, into=None if into is None else {0: into})


def _conv1_bwd(dy2, z, dw, acc_dw, acc_db, nseq, l, row0, name, side=None, into=None):
    e = z.shape[1] // 3
    ce = LANES
    half = CONV_WIDTH // 2

    def body(dy_ref, a_ref, b_ref, dw_ref, adw_ref, adb_ref, da_ref, dbb_ref, ddw_ref, ddb_ref, ypad_ref, dpad_ref):
        @pl.when(pl.program_id(1) == 0)
        def _():
            ddw_ref[...] = adw_ref[...]
            ddb_ref[...] = adb_ref[...]

        a, sb = a_ref[...], _sig(b_ref[...])
        dy = dy_ref[...]
        _fill_pad(ypad_ref, a * sb, l, ce)
        _fill_pad(dpad_ref, dy, l, ce)
        ddb_ref[...] += _rowsum(dy)
        for k in range(CONV_WIDTH):
            ddw_ref[k:k + 1, :] += _rowsum(dy * ypad_ref[pl.ds(CONV_PAD - half + k, l), :])
        pieces = _tap_sum(dpad_ref, dw_ref, CONV_PAD - half, CONV_WIDTH, l, ce, True)
        for n, dy1 in enumerate(pieces):
            rows = pl.ds(n * CHUNK, dy1.shape[0])
            sbn = sb[n * CHUNK:n * CHUNK + dy1.shape[0], :]
            da_ref[rows, :] = (dy1 * sbn).astype(da_ref.dtype)
            dbb_ref[rows, :] = (dy1 * a[n * CHUNK:n * CHUNK + dy1.shape[0], :] * sbn * (1.0 - sbn)).astype(dbb_ref.dtype)

    cw = lambda j, s: (0, j)
    return _pallas(
        body, (dy2, z, z, dw, acc_dw, acc_db), grid=(e // ce, nseq),
        in_specs=[_seq_spec(l, ce, row0), _seq_spec(l, ce, row0), _seq_spec(l, ce, row0, e // ce),
                  pl.BlockSpec((CONV_WIDTH, ce), cw), pl.BlockSpec((CONV_WIDTH, ce), cw), pl.BlockSpec((1, ce), cw)],
        out_specs=[_seq_spec(l, ce, row0), _seq_spec(l, ce, row0),
                   pl.BlockSpec((CONV_WIDTH, ce), cw), pl.BlockSpec((1, ce), cw)],
        out_shape=[_sds((z.shape[0], e), BF16), _sds((z.shape[0], e), BF16), _sds((CONV_WIDTH, e)), _sds((1, e))],
        scratch_shapes=[pltpu.VMEM((l + 2 * CONV_PAD, ce), F32), pltpu.VMEM((l + 2 * CONV_PAD, ce), F32)],
        sem=("parallel", "arbitrary"), name=name, side=side,
        into=None if into is None else {0: into[0], 1: into[1]})


def _pool_tables(l, e):
    grp = e // len(POOL_WINDOWS)
    w = jnp.repeat(jnp.array(POOL_WINDOWS, jnp.int32), grp)[None, :]
    off = jnp.arange(POOL_TAPS, dtype=jnp.int32)[:, None] - POOL_TAPS // 2
    taps = jnp.logical_and(off >= -(w // 2), off < w - w // 2).astype(F32)
    t = jnp.arange(l, dtype=jnp.int32)[:, None]
    cnt = jnp.clip(t + (w - w // 2), 0, l) - jnp.clip(t - w // 2, 0, l)
    return taps, 1.0 / cnt.astype(F32)


def _pool1(v_src, taps, inv_cnt, nseq, l, row0, transpose, name, out_dtype, into=None):
    e = taps.shape[1]
    ce = LANES
    half = POOL_TAPS // 2

    def body(v_ref, taps_ref, ic_ref, o_ref, pad_ref):
        v = v_ref[...].astype(F32)
        if transpose:
            _fill_pad(pad_ref, v * ic_ref[...], l, ce)
            pieces = _tap_sum(pad_ref, taps_ref, CONV_PAD - half + 1, POOL_TAPS, l, ce, True)
        else:
            _fill_pad(pad_ref, v, l, ce)
            pieces = _tap_sum(pad_ref, taps_ref, CONV_PAD - half, POOL_TAPS, l, ce, False)
        for n, acc in enumerate(pieces):
            rows = pl.ds(n * CHUNK, acc.shape[0])
            vn = v[n * CHUNK:n * CHUNK + acc.shape[0], :]
            if transpose:
                o_ref[rows, :] = (acc - vn).astype(o_ref.dtype)
            else:
                o_ref[rows, :] = (acc * ic_ref[rows, :] - vn).astype(o_ref.dtype)

    return _pallas(
        body, (v_src, taps, inv_cnt), grid=(e // ce, nseq),
        in_specs=[_seq_spec(l, ce, row0), pl.BlockSpec((POOL_TAPS, ce), lambda j, s: (0, j)),
                  pl.BlockSpec((l, ce), lambda j, s: (0, j))],
        out_specs=_seq_spec(l, ce, row0), out_shape=_sds((v_src.shape[0], e), out_dtype),
        scratch_shapes=[pltpu.VMEM((l + 2 * CONV_PAD, ce), F32)],
        sem=("parallel", "arbitrary"), name=name, into=None if into is None else {0: into})


def _layernorm_parts(x, eps=EPS):
    mu = jnp.mean(x, axis=-1, keepdims=True)
    xc = x - mu
    r = lax.rsqrt(jnp.mean(xc * xc, axis=-1, keepdims=True) + eps)
    return xc * r, r


def _layernorm_bwd(dy, xh, r, g):
    dxh = dy * g
    return r * (dxh - jnp.mean(dxh, axis=-1, keepdims=True) - xh * jnp.mean(dxh * xh, axis=-1, keepdims=True))


def _conv2_fwd(y2, z, ln_g, ln_b, tm, name):
    t, e = y2.shape

    def body(y_ref, g_ref, lg_ref, lb_ref, o_ref):
        xh, _ = _layernorm_parts(y_ref[...])
        o_ref[...] = (_silu(xh * lg_ref[...] + lb_ref[...]) * _silu(g_ref[...])).astype(o_ref.dtype)

    return pl.pallas_call(
        body, grid=(t // tm,), in_specs=[_row(tm, e), _row(tm, e, 2), _const((1, e)), _const((1, e))],
        out_specs=_row(tm, e), out_shape=_sds((t, e), BF16), compiler_params=_params("parallel"),
        name=name)(y2, z, ln_g, ln_b)


def _conv2_bwd(dy4, y2, z, ln_g, ln_b, tm, name):
    t, e = y2.shape

    def body(dy_ref, y_ref, g_ref, lg_ref, lb_ref, dy2_ref, dg_ref, dlg_ref, dlb_ref):
        @pl.when(pl.program_id(0) == 0)
        def _():
            dlg_ref[...] = jnp.zeros_like(dlg_ref)
            dlb_ref[...] = jnp.zeros_like(dlb_ref)

        dy, gz = dy_ref[...], g_ref[...]
        xh, r = _layernorm_parts(y_ref[...])
        y3 = xh * lg_ref[...] + lb_ref[...]
        dg_ref[...] = (dy * _silu(y3) * _dsilu(gz)).astype(dg_ref.dtype)
        dy3 = dy * _silu(gz) * _dsilu(y3)
        dlg_ref[...] += _rowsum(dy3 * xh)
        dlb_ref[...] += _rowsum(dy3)
        dy2_ref[...] = _layernorm_bwd(dy3, xh, r, lg_ref[...])

    return pl.pallas_call(
        body, grid=(t // tm,),
        in_specs=[_row(tm, e), _row(tm, e), _row(tm, e, 2), _const((1, e)), _const((1, e))],
        out_specs=[_row(tm, e), _row(tm, e), _const((1, e)), _const((1, e))],
        out_shape=[_sds((t, e)), _sds((t, e), BF16), _sds((1, e)), _sds((1, e))],
        compiler_params=_params("arbitrary"), name=name)(dy4, y2, z, ln_g, ln_b)


def _pool2_fwd(pm, w_grp, scale, z, tm, name):
    t, e = pm.shape
    ng, gw = w_grp.shape[0], w_grp.shape[1]

    def body(pm_ref, w_ref, sc_ref, g_ref, o_ref):
        for k in range(ng):
            cols = slice(k * gw, (k + 1) * gw)
            y = _dot(pm_ref[:, cols], w_ref[k])
            o_ref[:, cols] = (y * sc_ref[:, cols] * _silu(g_ref[:, cols])).astype(o_ref.dtype)

    return pl.pallas_call(
        body, grid=(t // tm,), in_specs=[_row(tm, e), _const(w_grp.shape), _const((1, e)), _row(tm, e, 1)],
        out_specs=_row(tm, e), out_shape=_sds((t, e), BF16), compiler_params=_params("parallel"),
        name=name)(pm, w_grp, scale, z)


def _pool2_bwd(dy2, pm, w_grp, scale, z, tm, name):
    t, e = pm.shape
    ng, gw = w_grp.shape[0], w_grp.shape[1]

    def body(dy_ref, pm_ref, w_ref, sc_ref, g_ref, dpm_ref, dg_ref, dsc_ref, dw_ref):
        @pl.when(pl.program_id(0) == 0)
        def _():
            dsc_ref[...] = jnp.zeros_like(dsc_ref)
            dw_ref[...] = jnp.zeros_like(dw_ref)

        for k in range(ng):
            cols = slice(k * gw, (k + 1) * gw)
            dy, gz, sc, pmk = dy_ref[:, cols], g_ref[:, cols], sc_ref[:, cols], pm_ref[:, cols]
            y = _dot(pmk, w_ref[k])
            dg_ref[:, cols] = (dy * (y * sc) * _dsilu(gz)).astype(dg_ref.dtype)
            dys = dy * _silu(gz)
            dsc_ref[:, cols] += _rowsum(dys * y)
            dyk = dys * sc
            dpm_ref[:, cols] = _dot_nt(dyk, w_ref[k])
            dw_ref[k] += _dot_tn(pmk, dyk)

    return pl.pallas_call(
        body, grid=(t // tm,),
        in_specs=[_row(tm, e), _row(tm, e), _const(w_grp.shape), _const((1, e)), _row(tm, e, 1)],
        out_specs=[_row(tm, e), _row(tm, e), _const((1, e)), _const(w_grp.shape)],
        out_shape=[_sds((t, e)), _sds((t, e), BF16), _sds((1, e)), _sds(w_grp.shape)],
        compiler_params=_params("arbitrary"), name=name)(dy2, pm, w_grp, scale, z)


def _rms_f(x, g, n):
    r = lax.rsqrt(jnp.sum(x * x, axis=-1, keepdims=True) * (1.0 / n) + EPS)
    return x * r * g


def _rms_b(x, g, dy, n):
    r = lax.rsqrt(jnp.sum(x * x, axis=-1, keepdims=True) * (1.0 / n) + EPS)
    xh = x * r
    dxh = dy * g
    return r * (dxh - xh * (jnp.sum(dxh * xh, axis=-1, keepdims=True) * (1.0 / n))), dy * xh


def _swap16(x):
    lane = lax.broadcasted_iota(jnp.int32, x.shape, 1)
    return jnp.where(lane % 32 < 16, pltpu.roll(x, LANES - 16, 1), pltpu.roll(x, 16, 1))


def _rope(x, c, s):
    return x * c + _swap16(x) * s


def _rope_t(dy, c, s):
    return dy * c + _swap16(dy * s)


def _rope_tables(l, lc, nb):
    t = jnp.arange(l, dtype=jnp.int32)
    row_id, col_id = (t // GRID_W).astype(F32), (t % GRID_W).astype(F32)
    axis_dim = MLA_ROPE // 2
    freqs = ROPE_THETA ** (-jnp.arange(0, axis_dim, 2, dtype=F32) / axis_dim)
    ar, ac = row_id[:, None] * freqs, col_id[:, None] * freqs
    pad1, pad0 = jnp.ones((l, LANES - MLA_ROPE), F32), jnp.zeros((l, LANES - MLA_ROPE), F32)
    ctab = jnp.concatenate([jnp.cos(ar), jnp.cos(ar), jnp.cos(ac), jnp.cos(ac), pad1], axis=1)
    stab = jnp.concatenate([-jnp.sin(ar), jnp.sin(ar), -jnp.sin(ac), jnp.sin(ac), pad0], axis=1)
    ctab = jnp.concatenate([jnp.tile(ctab, (nb, 1)), jnp.ones((nb * lc, LANES), F32)], axis=0)
    stab = jnp.concatenate([jnp.tile(stab, (nb, 1)), jnp.zeros((nb * lc, LANES), F32)], axis=0)
    return ctab, stab


def _kv_pre_fwd(zkv, kv_norm, rope_g, ctab, stab, tm, name):
    t = zkv.shape[0]

    def body(z_ref, gk_ref, gr_ref, c_ref, s_ref, ck_ref, kr_ref):
        ck_ref[...] = _rms_f(z_ref[:, :MLA_KV_RANK], gk_ref[...], MLA_KV_RANK).astype(ck_ref.dtype)
        kr = _rms_f(z_ref[:, MLA_KV_RANK:], gr_ref[...], MLA_ROPE)
        kr_ref[...] = _rope(kr, c_ref[...], s_ref[...]).astype(kr_ref.dtype)

    w = MLA_KV_RANK + LANES
    return pl.pallas_call(
        body, grid=(t // tm,),
        in_specs=[_row(tm, w), _const((1, MLA_KV_RANK)), _const((1, LANES)), _row(tm, LANES), _row(tm, LANES)],
        out_specs=[_row(tm, MLA_KV_RANK), _row(tm, LANES)],
        out_shape=[_sds((t, MLA_KV_RANK), BF16), _sds((t, LANES), BF16)],
        compiler_params=_params("parallel"), name=name)(zkv, kv_norm, rope_g, ctab, stab)


def _kv_pre_bwd(dck, dkr, zkv, kv_norm, rope_g, ctab, stab, tm, name):
    t = zkv.shape[0]
    w = MLA_KV_RANK + LANES

    def body(dck_ref, dkr_ref, z_ref, gk_ref, gr_ref, c_ref, s_ref, dz_ref, dgk_ref, dgr_ref):
        @pl.when(pl.program_id(0) == 0)
        def _():
            dgk_ref[...] = jnp.zeros_like(dgk_ref)
            dgr_ref[...] = jnp.zeros_like(dgr_ref)

        dx, dg = _rms_b(z_ref[:, :MLA_KV_RANK], gk_ref[...], dck_ref[...], MLA_KV_RANK)
        dz_ref[:, :MLA_KV_RANK] = dx.astype(dz_ref.dtype)
        dgk_ref[...] += _rowsum(dg)
        dy = _rope_t(dkr_ref[...], c_ref[...], s_ref[...])
        dx, dg = _rms_b(z_ref[:, MLA_KV_RANK:], gr_ref[...], dy, MLA_ROPE)
        dz_ref[:, MLA_KV_RANK:] = dx.astype(dz_ref.dtype)
        dgr_ref[...] += _rowsum(dg)

    return pl.pallas_call(
        body, grid=(t // tm,),
        in_specs=[_row(tm, MLA_KV_RANK), _row(tm, LANES), _row(tm, w), _const((1, MLA_KV_RANK)), _const((1, LANES)),
                  _row(tm, LANES), _row(tm, LANES)],
        out_specs=[_row(tm, w), _const((1, MLA_KV_RANK)), _const((1, LANES))],
        out_shape=[_sds((t, w), BF16), _sds((1, MLA_KV_RANK)), _sds((1, LANES))],
        compiler_params=_params("arbitrary"), name=name)(dck, dkr, zkv, kv_norm, rope_g, ctab, stab)


def _q_pre_fwd(zq, q_norm, tm, name):
    t, w = zq.shape

    def body(z_ref, g_ref, o_ref):
        o_ref[...] = _rms_f(z_ref[...], g_ref[...], w).astype(o_ref.dtype)

    return pl.pallas_call(
        body, grid=(t // tm,), in_specs=[_row(tm, w), _const((1, w))], out_specs=_row(tm, w),
        out_shape=_sds((t, w), BF16), compiler_params=_params("parallel"), name=name)(zq, q_norm)


def _q_pre_bwd(dcq, zq, q_norm, tm, name):
    t, w = zq.shape

    def body(d_ref, z_ref, g_ref, dz_ref, dg_ref):
        @pl.when(pl.program_id(0) == 0)
        def _():
            dg_ref[...] = jnp.zeros_like(dg_ref)

        dx, dg = _rms_b(z_ref[...], g_ref[...], d_ref[...], w)
        dz_ref[...] = dx.astype(dz_ref.dtype)
        dg_ref[...] += _rowsum(dg)

    return pl.pallas_call(
        body, grid=(t // tm,), in_specs=[_row(tm, w), _row(tm, w), _const((1, w))],
        out_specs=[_row(tm, w), _const((1, w))], out_shape=[_sds((t, w), BF16), _sds((1, w))],
        compiler_params=_params("arbitrary"), name=name)(dcq, zq, q_norm)


def _k_post_fwd(kv, krr, nope_g, tm, name):
    t, w = kv.shape

    def body(kv_ref, kr_ref, gn_ref, k_ref, v_ref):
        for h in range(MLA_HEADS):
            a = h * HEAD_W
            k_ref[:, a:a + LANES] = _rms_f(kv_ref[:, a:a + LANES], gn_ref[...], MLA_NOPE).astype(k_ref.dtype)
            k_ref[:, a + LANES:a + HEAD_W] = kr_ref[...]
            v_ref[:, h * LANES:(h + 1) * LANES] = kv_ref[:, a + LANES:a + HEAD_W].astype(v_ref.dtype)

    return pl.pallas_call(
        body, grid=(t // tm,), in_specs=[_row(tm, w), _row(tm, LANES), _const((1, LANES))],
        out_specs=[_row(tm, w), _row(tm, w // 2)], out_shape=[_sds((t, w), BF16), _sds((t, w // 2), BF16)],
        compiler_params=_params("parallel"), name=name)(kv, krr, nope_g)


def _k_post_bwd(dkl, dkc, dvl, dvc, kv, nope_g, tm, name):
    t, w = kv.shape
    nl = dkl.shape[0] // tm

    def body(dkl_ref, dkc_ref, dvl_ref, dvc_ref, kv_ref, gn_ref, dkv_ref, dkr_ref, dgn_ref):
        i = pl.program_id(0)

        @pl.when(i == 0)
        def _():
            dgn_ref[...] = jnp.zeros_like(dgn_ref)

        dkr = jnp.zeros(dkr_ref.shape, F32)
        for h in range(MLA_HEADS):
            a = h * HEAD_W
            dk = jnp.where(i < nl, dkl_ref[:, a:a + HEAD_W], dkc_ref[:, a:a + HEAD_W])
            dv = jnp.where(i < nl, dvl_ref[:, h * LANES:(h + 1) * LANES], dvc_ref[:, h * LANES:(h + 1) * LANES])
            dx, dg = _rms_b(kv_ref[:, a:a + LANES], gn_ref[...], dk[:, :LANES], MLA_NOPE)
            dkv_ref[:, a:a + LANES] = dx.astype(dkv_ref.dtype)
            dgn_ref[...] += _rowsum(dg)
            dkv_ref[:, a + LANES:a + HEAD_W] = dv.astype(dkv_ref.dtype)
            dkr = dkr + dk[:, LANES:]
        dkr_ref[...] = dkr

    lat = lambda cols: pl.BlockSpec((tm, cols), lambda i: (jnp.minimum(i, nl - 1), 0))
    ctx = lambda cols: pl.BlockSpec((tm, cols), lambda i: (jnp.maximum(i - nl, 0), 0))
    return pl.pallas_call(
        body, grid=(t // tm,),
        in_specs=[lat(w), ctx(w), lat(w // 2), ctx(w // 2), _row(tm, w), _const((1, LANES))],
        out_specs=[_row(tm, w), _row(tm, LANES), _const((1, LANES))],
        out_shape=[_sds((t, w), BF16), _sds((t, LANES)), _sds((1, LANES))],
        compiler_params=_params("arbitrary"), name=name)(dkl, dkc, dvl, dvc, kv, nope_g)


def _attn_specs(nb, l, lc, tq):
    nq = l // tq
    ctx0 = nb * l // lc
    q_spec = lambda w: pl.BlockSpec((tq, w), lambda b, h, i: (b * nq + i, h))
    lat = lambda w: pl.BlockSpec((l, w), lambda b, h, i: (b, h))
    ctx = lambda w: pl.BlockSpec((lc, w), lambda b, h, i: (ctx0 + b, h))
    return nq, q_spec, lat, ctx


def _attn_fwd(q, nope_g, rope_g, ctab, stab, kf, vf, nb, l, lc, name, side=None):
    tq = _pick(l, (1024, 512, 256, 128))
    nq, q_spec, lat, ctx = _attn_specs(nb, l, lc, tq)
    sub = min(tq // ATTN_CHAINS, 256)
    tab = pl.BlockSpec((tq, LANES), lambda b, h, i: (b * nq + i, 0))

    def body(q_ref, gn_ref, gr_ref, c_ref, s_ref, kl_ref, kc_ref, vl_ref, vc_ref, o_ref, lse_ref, qf_ref):
        for r in range(0, tq, sub):
            rows = slice(r, r + sub)
            qn = _rms_f(q_ref[rows, :LANES], gn_ref[...], MLA_NOPE)
            qr = _rope(_rms_f(q_ref[rows, LANES:], gr_ref[...], MLA_ROPE), c_ref[rows, :], s_ref[rows, :])
            q = (jnp.concatenate([qn, qr], axis=1) * MLA_SCALE).astype(BF16)
            qf_ref[rows, :] = q
            s1, s2 = _dot_nt(q, kl_ref[...]), _dot_nt(q, kc_ref[...])
            m = jnp.maximum(jnp.max(s1, axis=-1, keepdims=True), jnp.max(s2, axis=-1, keepdims=True))
            p1, p2 = jnp.exp(s1 - m), jnp.exp(s2 - m)
            den = jnp.sum(p1, axis=-1, keepdims=True) + jnp.sum(p2, axis=-1, keepdims=True)
            o_ref[rows, :] = (_dot(p1, vl_ref[...]) + _dot(p2, vc_ref[...])) / den
            lse_ref[rows, :] = jnp.broadcast_to(m + jnp.log(den), (sub, LANES))

    return _pallas(
        body, (q, nope_g, rope_g, ctab, stab, kf, kf, vf, vf), grid=(nb, MLA_HEADS, nq),
        in_specs=[q_spec(HEAD_W), _const((1, LANES)), _const((1, LANES)), tab, tab, lat(HEAD_W), ctx(HEAD_W),
                  lat(LANES), ctx(LANES)],
        out_specs=[q_spec(LANES), q_spec(LANES), q_spec(HEAD_W)],
        out_shape=[_sds((nb * l, MLA_HEADS * LANES)), _sds((nb * l, MLA_HEADS * LANES)),
                   _sds((nb * l, MLA_HEADS * HEAD_W), BF16)],
        sem=("parallel", "parallel", "arbitrary"), name=name, side=side)


def _attn_bwd(do, o, lse, qf, q, nope_g, rope_g, ctab, stab, kf, vf, nb, l, lc, name, side=None):
    tq = _pick(l, (1024, 512, 256, 128))
    nq, q_spec, lat, ctx = _attn_specs(nb, l, lc, tq)
    out_lat = lambda w: pl.BlockSpec((l, w), lambda b, h, i: (b, h))
    out_ctx = lambda w: pl.BlockSpec((lc, w), lambda b, h, i: (b, h))
    sub = tq // ATTN_CHAINS
    tab = pl.BlockSpec((tq, LANES), lambda b, h, i: (b * nq + i, 0))

    def body(do_ref, o_ref, lse_ref, qf_ref, q_ref, gn_ref, gr_ref, c_ref, s_ref, kl_ref, kc_ref, vl_ref, vc_ref,
             dq_ref, dkl_ref, dkc_ref, dvl_ref, dvc_ref, dgn_ref, dgr_ref):
        @pl.when(pl.program_id(2) == 0)
        def _():
            dkl_ref[...] = jnp.zeros_like(dkl_ref)
            dkc_ref[...] = jnp.zeros_like(dkc_ref)
            dvl_ref[...] = jnp.zeros_like(dvl_ref)
            dvc_ref[...] = jnp.zeros_like(dvc_ref)

        @pl.when((pl.program_id(0) == 0) & (pl.program_id(1) == 0) & (pl.program_id(2) == 0))
        def _():
            dgn_ref[...] = jnp.zeros_like(dgn_ref)
            dgr_ref[...] = jnp.zeros_like(dgr_ref)

        parts = []
        for r in range(0, tq, sub):
            rows = slice(r, r + sub)
            qs, dof = qf_ref[rows, :], do_ref[rows, :]
            delta = jnp.sum(dof * o_ref[rows, :], axis=-1, keepdims=True)
            lse = lse_ref[rows, :1]
            dqs, part = jnp.zeros((sub, HEAD_W), F32), []
            for k_ref, v_ref in ((kl_ref, vl_ref), (kc_ref, vc_ref)):
                p = jnp.exp(_dot_nt(qs, k_ref[...]) - lse)
                ds = p * (_dot_nt(dof, v_ref[...]) - delta)
                dqs = dqs + _dot(ds, k_ref[...])
                part += [_dot_tn(ds, qs), _dot_tn(p, dof)]
            dqs = dqs * MLA_SCALE
            dxn, dgn = _rms_b(q_ref[rows, :LANES], gn_ref[...], dqs[:, :LANES], MLA_NOPE)
            dxr, dgr = _rms_b(q_ref[rows, LANES:], gr_ref[...], _rope_t(dqs[:, LANES:], c_ref[rows, :], s_ref[rows, :]),
                              MLA_ROPE)
            dq_ref[rows, :] = jnp.concatenate([dxn, dxr], axis=1).astype(dq_ref.dtype)
            parts.append(part + [_rowsum(dgn), _rowsum(dgr)])
        for n, ref in enumerate((dkl_ref, dvl_ref, dkc_ref, dvc_ref, dgn_ref, dgr_ref)):
            ref[...] += functools.reduce(lambda u, v: u + v, [part[n] for part in parts])

    kw, vw = MLA_HEADS * HEAD_W, MLA_HEADS * LANES
    return _pallas(
        body, (do, o, lse, qf, q, nope_g, rope_g, ctab, stab, kf, kf, vf, vf), grid=(nb, MLA_HEADS, nq),
        in_specs=[q_spec(LANES), q_spec(LANES), q_spec(LANES), q_spec(HEAD_W), q_spec(HEAD_W), _const((1, LANES)),
                  _const((1, LANES)), tab, tab, lat(HEAD_W), ctx(HEAD_W), lat(LANES), ctx(LANES)],
        out_specs=[q_spec(HEAD_W), out_lat(HEAD_W), out_ctx(HEAD_W), out_lat(LANES), out_ctx(LANES),
                   _const((1, LANES)), _const((1, LANES))],
        out_shape=[_sds((nb * l, kw), BF16), _sds((nb * l, kw)), _sds((nb * lc, kw)), _sds((nb * l, vw)),
                   _sds((nb * lc, vw)), _sds((1, LANES)), _sds((1, LANES))],
        sem=("arbitrary", "arbitrary", "arbitrary"), name=name, side=side)


def _gate_fwd(o, g, tm, name):
    t, e = o.shape

    def body(o_ref, g_ref, y_ref):
        y_ref[...] = (o_ref[...] * _silu(g_ref[...])).astype(y_ref.dtype)

    return pl.pallas_call(
        body, grid=(t // tm,), in_specs=[_row(tm, e), _row(tm, e)], out_specs=_row(tm, e),
        out_shape=_sds((t, e), BF16), compiler_params=_params("parallel"), name=name)(o, g)


def _gate_bwd(dy, o, g, tm, name):
    t, e = o.shape

    def body(dy_ref, o_ref, g_ref, do_ref, dg_ref):
        dy, gz = dy_ref[...], g_ref[...]
        do_ref[...] = dy * _silu(gz)
        dg_ref[...] = (dy * o_ref[...] * _dsilu(gz)).astype(dg_ref.dtype)

    return pl.pallas_call(
        body, grid=(t // tm,), in_specs=[_row(tm, e), _row(tm, e), _row(tm, e)],
        out_specs=[_row(tm, e), _row(tm, e)], out_shape=[_sds((t, e)), _sds((t, e), BF16)],
        compiler_params=_params("parallel"), name=name)(dy, o, g)


def _chunk_fwd(z, ln_g, ln_b, w_s, bs_full, name):
    t, e = z.shape[0], z.shape[1] // 3
    tm = _pick(t, CHUNK_TILES)

    def body(u_ref, v_ref, g_ref, lg_ref, lb_ref, w_ref, bs_ref, y_ref):
        for r0 in range(0, tm, CHUNK):
            rows = slice(r0, r0 + CHUNK)
            xh, _ = _layernorm_parts(v_ref[rows, :])
            vn = xh * lg_ref[...] + lb_ref[...]
            for k in range(CHUNK_GROUPS):
                cols = slice(k * LANES, (k + 1) * LANES)
                s = _dot(w_ref[k], vn[:, cols]) + bs_ref[:, cols]
                y_ref[rows, cols] = (u_ref[rows, cols] * s * _silu(g_ref[rows, cols])).astype(y_ref.dtype)

    return pl.pallas_call(
        body, grid=(t // tm,),
        in_specs=[_row(tm, e, 0), _row(tm, e, 1), _row(tm, e, 2), _const((1, e)), _const((1, e)),
                  _const(w_s.shape), _const((CHUNK, e))],
        out_specs=_row(tm, e), out_shape=_sds((t, e), BF16), compiler_params=_params("parallel"),
        name=name)(z, z, z, ln_g, ln_b, w_s, bs_full)


def _chunk_bwd(dy, z, ln_g, ln_b, w_s, bs_full, name):
    t, e = z.shape[0], z.shape[1] // 3
    tm = _pick(t, CHUNK_TILES)

    def body(dy_ref, u_ref, v_ref, g_ref, lg_ref, lb_ref, w_ref, bs_ref, dz_ref, dw_ref, dbs_ref, dlg_ref, dlb_ref,
             acc_ref):
        i = pl.program_id(0)

        @pl.when(i == 0)
        def _():
            dw_ref[...] = jnp.zeros_like(dw_ref)
            dlg_ref[...] = jnp.zeros_like(dlg_ref)
            dlb_ref[...] = jnp.zeros_like(dlb_ref)
            acc_ref[...] = jnp.zeros_like(acc_ref)

        per_chunk = []
        for r0 in range(0, tm, CHUNK):
            rows = slice(r0, r0 + CHUNK)
            xh, r = _layernorm_parts(v_ref[rows, :])
            vn = xh * lg_ref[...] + lb_ref[...]
            dvn, dss, dws = [], [], []
            for k in range(CHUNK_GROUPS):
                cols = slice(k * LANES, (k + 1) * LANES)
                dyk, u, gz = dy_ref[rows, cols], u_ref[rows, cols], g_ref[rows, cols]
                s = _dot(w_ref[k], vn[:, cols]) + bs_ref[:, cols]
                sg = _silu(gz)
                dz_ref[rows, cols] = (dyk * s * sg).astype(dz_ref.dtype)
                dz_ref[rows, 2 * e + k * LANES:2 * e + (k + 1) * LANES] = (dyk * u * s * _dsilu(gz)).astype(
                    dz_ref.dtype)
                ds = dyk * u * sg
                dss.append(ds)
                dws.append(_dot_nt(ds, vn[:, cols]))
                dvn.append(_dot_tn(w_ref[k], ds))
            dvn = jnp.concatenate(dvn, axis=1)
            dz_ref[rows, e:2 * e] = _layernorm_bwd(dvn, xh, r, lg_ref[...]).astype(dz_ref.dtype)
            per_chunk.append((dss, dws, _rowsum(dvn * xh), _rowsum(dvn)))
        total = lambda parts: functools.reduce(lambda a, b: a + b, parts)
        for k in range(CHUNK_GROUPS):
            acc_ref[:, k * LANES:(k + 1) * LANES] += total([c[0][k] for c in per_chunk])
            dw_ref[k] += total([c[1][k] for c in per_chunk])
        dlg_ref[...] += total([c[2] for c in per_chunk])
        dlb_ref[...] += total([c[3] for c in per_chunk])

        @pl.when(i == pl.num_programs(0) - 1)
        def _():
            lane = lax.broadcasted_iota(jnp.int32, dbs_ref.shape, 1)
            out = jnp.zeros(dbs_ref.shape, F32)
            for k in range(CHUNK_GROUPS):
                col = jnp.sum(acc_ref[:, k * LANES:(k + 1) * LANES], axis=1, keepdims=True)
                out = jnp.where(lane == k, col, out)
            dbs_ref[...] = out

    return pl.pallas_call(
        body, grid=(t // tm,),
        in_specs=[_row(tm, e), _row(tm, e, 0), _row(tm, e, 1), _row(tm, e, 2), _const((1, e)),
                  _const((1, e)), _const(w_s.shape), _const((CHUNK, e))],
        out_specs=[_row(tm, 3 * e), _const(w_s.shape), _const((CHUNK, CHUNK_GROUPS)), _const((1, e)),
                   _const((1, e))],
        out_shape=[_sds((t, 3 * e), BF16), _sds(w_s.shape), _sds((CHUNK, CHUNK_GROUPS)), _sds((1, e)), _sds((1, e))],
        scratch_shapes=[pltpu.VMEM((CHUNK, e), F32)],
        compiler_params=_params("arbitrary"), name=name)(dy, z, z, z, ln_g, ln_b, w_s, bs_full)


def _mod_rows(mods, layer, d, nseg):
    m = mods[layer, :nseg]
    return [m[:, None, k * d:(k + 1) * d] for k in range(3)]


def _local_step(x, ctx, tgt, w, mods, comm=None):
    nb, l, d = x.shape
    lc = ctx.shape[1]
    e = d
    tl, ta = nb * l, nb * (l + lc)
    tm = _pick(lc, (256, 128))
    segs_a, segs_l = _Segs((l,) * nb + (lc,) * nb, tm), _Segs((l,) * nb, tm)
    norm_g = w['norm_g']
    g = {}

    def carried(tag, fn, *args, **kw):
        if comm is None:
            return fn(*args, **kw)
        res, brought = fn(*args, side=comm.side(tag), **kw)
        comm.done(tag, brought)
        return res

    xa0 = jnp.concatenate([x.reshape(tl, d), ctx.reshape(nb * lc, d)], axis=0)

    sh0, sc0, gt0 = _mod_rows(mods, 0, d, 2 * nb)
    h0 = _rms_mod_fwd(xa0, norm_g[0:1], sc0, sh0, segs_a, "l0_norm")
    z0 = carried('fwd1', _mm, h0, w['cv_w_in'], "l0_in")
    y2_0 = carried('fwd2', _conv1_fwd, z0, w['cv_dw'], w['cv_db'], nb, l, 0, "l0_conv_lat")
    y2_0 = _conv1_fwd(z0, w['cv_dw'], w['cv_db'], nb, lc, tl, "l0_conv_ctx", into=y2_0)
    y4_0 = _conv2_fwd(y2_0, z0, w['cv_ln_g'], w['cv_ln_b'], tm, "l0_gate")
    o0, xa1 = _mm(y4_0, w['cv_w_out'], "l0_out", resid=(xa0, gt0, segs_a))

    sh1, sc1, gt1 = _mod_rows(mods, 1, d, 2 * nb)
    h1 = _rms_mod_fwd(xa1, norm_g[1:2], sc1, sh1, segs_a, "l1_norm")
    z1 = carried('fwd3', _mm, h1, w['pl_w_in'], "l1_in")
    taps_l, ic_l = _pool_tables(l, e)
    taps_c, ic_c = _pool_tables(lc, e)
    pm1 = _pool1(z1, taps_l, ic_l, nb, l, 0, False, "l1_pool_lat", BF16)
    pm1 = _pool1(z1, taps_c, ic_c, nb, lc, tl, False, "l1_pool_ctx", BF16, into=pm1)
    y2_1 = _pool2_fwd(pm1, w['pl_w_grp'], w['pl_scale'], z1, tm, "l1_group")
    o1, xa2 = _mm(y2_1, w['pl_w_out'], "l1_out", resid=(xa1, gt1, segs_a))

    sh2, sc2, gt2 = _mod_rows(mods, 2, d, 2 * nb)
    h2 = _rms_mod_fwd(xa2, norm_g[2:3], sc2, sh2, segs_a, "l2_norm")
    w_in = w['ml_w_in']
    kvc = MLA_KV_RANK + MLA_ROPE
    w_in_p = jnp.concatenate([w_in[:, :kvc], jnp.zeros((d, LANES - MLA_ROPE), w_in.dtype), w_in[:, kvc:]], axis=1)
    w_uq_p = jnp.pad(w['ml_w_uq'].reshape(MLA_Q_RANK, MLA_HEADS, MLA_NOPE + MLA_ROPE),
                     ((0, 0), (0, 0), (0, HEAD_W - MLA_NOPE - MLA_ROPE))).reshape(MLA_Q_RANK, MLA_HEADS * HEAD_W)
    rope_g = jnp.pad(w['ml_rope_norm'], ((0, 0), (0, LANES - MLA_ROPE)))
    nope_g = w['ml_nope_norm']
    ctab, stab = _rope_tables(l, lc, nb)
    kvw = MLA_KV_RANK + LANES
    w_kv, w_q, w_g = w_in_p[:, :kvw], w_in_p[:, kvw:kvw + MLA_Q_RANK], w_in_p[:, kvw + MLA_Q_RANK:]
    zkv = _mm(h2, w_kv, "l2_in_kv")
    zq, zg = _mm(h2, w_q, "l2_in_q", rows=tl), _mm(h2, w_g, "l2_in_g", rows=tl)
    ckvn, krr = _kv_pre_fwd(zkv, w['ml_kv_norm'], rope_g[1:2], ctab, stab, tm, "l2_kv_pre")
    cqn = _q_pre_fwd(zq, w['ml_q_norm'], tm, "l2_q_pre")
    q2 = _mm(cqn, w_uq_p, "l2_uq")
    kv2 = _mm(ckvn, w['ml_w_ukv'], "l2_ukv")
    kf, vf = _k_post_fwd(kv2, krr, nope_g[1:2], tm, "l2_k_post")
    o_att, lse, qf = carried('fwd4', _attn_fwd, q2, nope_g[0:1], rope_g[0:1], ctab, stab, kf, vf, nb, l, lc,
                             "l2_attn")
    og = _gate_fwd(o_att, zg, tm, "l2_gate")
    o2, x3 = _mm(og, w['ml_w_out'], "l2_out", resid=(xa2, gt2[:nb], segs_l))

    sh3, sc3, gt3 = _mod_rows(mods, 3, d, nb)
    h3 = _rms_mod_fwd(x3, norm_g[3:4], sc3, sh3, segs_l, "l3_norm")
    z3 = _mm(h3, w['ch_w_in'], "l3_in")
    bs_full = jnp.repeat(w['ch_b_s'], e // CHUNK_GROUPS, axis=1)
    y3 = _chunk_fwd(z3, w['ch_ln_g'], w['ch_ln_b'], w['ch_w_s'], bs_full, "l3_chunk")
    o3, x4 = _mm(y3, w['ch_w_out'], "l3_out", resid=(x3, gt3, segs_l))

    loss_vec, dx4 = _loss_head(x4, tgt.reshape(tl, d), tm, "loss")

    do3, dgt3 = _resid_bwd(dx4, o3, gt3, segs_l, "l3_resid_b")
    dy3 = _mm_nt(do3, w['ch_w_out'], "l3_out_bx")
    g['ch_w_out'] = _mm_tn(y3, do3, "l3_out_bw")
    dz3, g['ch_w_s'], g['ch_b_s'], g['ch_ln_g'], g['ch_ln_b'] = _chunk_bwd(
        dy3, z3, w['ch_ln_g'], w['ch_ln_b'], w['ch_w_s'], bs_full, "l3_chunk_b")
    dh3 = _mm_nt(dz3, w['ch_w_in'], "l3_in_bx")
    g['ch_w_in'] = _mm_tn(h3, dz3, "l3_in_bw", shards=N_DEV)
    dx3, dng3, dsc3, dsh3 = _rms_mod_bwd(x3, norm_g[3:4], sc3, sh3, dh3, dx4, segs_l, "l3_norm_b")
    if comm is not None:
        comm.grads_ready(3, g)

    do2, dgt2 = _resid_bwd(dx3, o2, gt2[:nb], segs_l, "l2_resid_b")
    dog = carried('swap3', _mm_nt, do2, w['ml_w_out'], "l2_out_bx")
    g['ml_w_out'] = _mm_tn(og, do2, "l2_out_bw")
    d_att, dzg = _gate_bwd(dog, o_att, zg, tm, "l2_gate_b")
    dq2, dkl, dkc, dvl, dvc, dnope_q, drope_q = carried(
        'quad3', _attn_bwd, d_att, o_att, lse, qf, q2, nope_g[0:1], rope_g[0:1], ctab, stab, kf, vf, nb, l, lc,
        "l2_attn_b")
    dkv2, dkrr, dnope_k = _k_post_bwd(dkl, dkc, dvl, dvc, kv2, nope_g[1:2], tm, "l2_k_post_b")
    dcqn = _mm_nt(dq2, w_uq_p, "l2_uq_bx")
    g_uq_p = _mm_tn(cqn, dq2, "l2_uq_bw")
    dckvn = _mm_nt(dkv2, w['ml_w_ukv'], "l2_ukv_bx")
    g['ml_w_ukv'] = _mm_tn(ckvn, dkv2, "l2_ukv_bw", shards=N_DEV)
    dzq, g['ml_q_norm'] = _q_pre_bwd(dcqn, zq, w['ml_q_norm'], tm, "l2_q_pre_b")
    dzkv, g['ml_kv_norm'], drope_k = _kv_pre_bwd(dckvn, dkrr, zkv, w['ml_kv_norm'], rope_g[1:2], ctab, stab, tm,
                                                  "l2_kv_pre_b")
    dh2 = _mm_nt_sum([(dzkv, w_kv), (dzq, w_q), (dzg, w_g)], "l2_in_bx")
    g['ml_w_in'] = jnp.concatenate([_mm_tn(h2, dzkv, "l2_in_kv_bw")[:, :kvc], _mm_tn(h2, dzq, "l2_in_q_bw", rows=tl),
                                    _mm_tn(h2, dzg, "l2_in_g_bw", rows=tl)], axis=1)
    g['ml_w_uq'] = g_uq_p.reshape(MLA_Q_RANK, MLA_HEADS, HEAD_W)[:, :, :MLA_NOPE + MLA_ROPE].reshape(
        MLA_Q_RANK, MLA_HEADS * (MLA_NOPE + MLA_ROPE))
    g['ml_nope_norm'] = jnp.concatenate([dnope_q, dnope_k], axis=0)
    g['ml_rope_norm'] = jnp.concatenate([drope_q, drope_k], axis=0)[:, :MLA_ROPE]
    dxa2, dng2, dsc2, dsh2 = _rms_mod_bwd(xa2, norm_g[2:3], sc2, sh2, dh2, dx3, segs_a, "l2_norm_b")
    if comm is not None:
        comm.grads_ready(2, g)

    do1, dgt1 = _resid_bwd(dxa2, o1, gt1, segs_a, "l1_resid_b")
    dy2_1 = carried('swap2', _mm_nt, do1, w['pl_w_out'], "l1_out_bx")
    g['pl_w_out'] = _mm_tn(y2_1, do1, "l1_out_bw")
    dpm, dgz1, g['pl_scale'], g['pl_w_grp'] = _pool2_bwd(dy2_1, pm1, w['pl_w_grp'], w['pl_scale'], z1, tm,
                                                          "l1_group_b")
    dv1 = _pool1(dpm, taps_l, ic_l, nb, l, 0, True, "l1_pool_lat_b", BF16)
    dv1 = _pool1(dpm, taps_c, ic_c, nb, lc, tl, True, "l1_pool_ctx_b", BF16, into=dv1)
    dz1 = jnp.concatenate([dv1, dgz1], axis=1)
    dh1 = _mm_nt(dz1, w['pl_w_in'], "l1_in_bx")
    g['pl_w_in'] = carried('quad2', _mm_tn, h1, dz1, "l1_in_bw", shards=N_DEV)
    dxa1, dng1, dsc1, dsh1 = _rms_mod_bwd(xa1, norm_g[1:2], sc1, sh1, dh1, dxa2, segs_a, "l1_norm_b")

    do0, dgt0 = _resid_bwd(dxa1, o0, gt0, segs_a, "l0_resid_b")
    g['cv_w_out'] = _mm_tn(y4_0, do0, "l0_out_bw")
    if comm is not None:
        comm.grads_ready(1, g)
    dy4 = carried('swap1', _mm_nt, do0, w['cv_w_out'], "l0_out_bx")
    dy2, dgz0, g['cv_ln_g'], g['cv_ln_b'] = _conv2_bwd(dy4, y2_0, z0, w['cv_ln_g'], w['cv_ln_b'], tm, "l0_gate_b")
    da_l, db_l, ddw, ddb = carried('quad1', _conv1_bwd, dy2, z0, w['cv_dw'], jnp.zeros((CONV_WIDTH, e), F32),
                                   jnp.zeros((1, e), F32), nb, l, 0, "l0_conv_lat_b")
    da, db_, g['cv_dw'], g['cv_db'] = _conv1_bwd(dy2, z0, w['cv_dw'], ddw, ddb, nb, lc, tl, "l0_conv_ctx_b",
                                                 into=(da_l, db_l))
    dz0 = jnp.concatenate([da, db_, dgz0], axis=1)
    g['cv_w_in'] = _mm_tn(h0, dz0, "l0_in_bw", shards=N_DEV)
    if comm is not None:
        comm.grads_ready(0, g)
    dh0 = carried('quad0', _mm_nt, dz0, w['cv_w_in'], "l0_in_bx")
    dx0, dng0, dsc0, dsh0 = _rms_mod_bwd(xa0, norm_g[0:1], sc0, sh0, dh0, dxa1, segs_a, "l0_norm_b", dx_rows=tl)

    def rows4(t):
        return jnp.pad(t[:, 0], ((0, 2 * nb - t.shape[0]), (0, 0)))

    dmods = jnp.stack([
        jnp.concatenate([rows4(dsh0), rows4(dsc0), rows4(dgt0)], axis=1),
        jnp.concatenate([rows4(dsh1), rows4(dsc1), rows4(dgt1)], axis=1),
        jnp.concatenate([rows4(dsh2), rows4(dsc2), rows4(dgt2)], axis=1),
        jnp.concatenate([rows4(dsh3), rows4(dsc3), rows4(dgt3)], axis=1)])
    dnorm_g = jnp.concatenate([dng0, dng1, dng2, dng3], axis=0)
    return loss_vec, dx0.reshape(nb, l, d), g, dmods, dnorm_g


def _mesh_pos():
    return lax.axis_index("x"), lax.axis_index("y"), lax.axis_index("c")


def _remote(src, dst, send_sems, recv_sems, k, dev):
    return pltpu.make_async_remote_copy(src_ref=src, dst_ref=dst, send_sem=send_sems.at[k], recv_sem=recv_sems.at[k],
                                        device_id=dev, device_id_type=pl.DeviceIdType.MESH)


def _comm_call(body, xs, out_shapes, n_remote, n_local, name):
    hbm = pl.BlockSpec(memory_space=pltpu.HBM)
    return pl.pallas_call(
        body, in_specs=[hbm] * len(xs), out_specs=[hbm] * len(out_shapes), out_shape=out_shapes,
        scratch_shapes=_sem_shapes(n_remote, n_local),
        compiler_params=pltpu.CompilerParams(has_side_effects=True), name=name)(*xs)


def _run_side(side, name):
    n = len(side.xs)

    def body(*refs):
        side.start(refs[:n], refs[n:n + len(side.out_shapes)], *refs[n + len(side.out_shapes):])
        side.finish(refs[:n], refs[n:n + len(side.out_shapes)], *refs[n + len(side.out_shapes):])

    return _comm_call(body, side.xs, side.out_shapes, side.n_remote, side.n_local, name)


def _gather_side(xs):
    n = len(xs)

    def plan(x_refs, o_refs, send_sems, recv_sems, local_sems):
        x, y, c = _mesh_pos()
        me, sib = (x, y, c), (x, y, 1 - c)
        chips = [(1 - x, y), (x, 1 - y), (1 - x, 1 - y)]

        def slot(a, p):
            return o_refs[a].at[4 * p[0] + 2 * p[1] + p[2]]

        def copy(a, k, block, to, src=None):
            return _remote(slot(a, block) if src is None else src, slot(a, block), send_sems, recv_sems, 7 * a + k, to)

        mine = [pltpu.make_async_copy(x_refs[a], slot(a, me), local_sems.at[a]) for a in range(n)]
        first = []
        for a in range(n):
            first += [copy(a, 1 + j, me, chip + (c,), src=x_refs[a]) for j, chip in enumerate(chips)]
            first.append(copy(a, 0, me, sib, src=x_refs[a]))
        return me, sib, c, chips, copy, mine, first

    def start(x_refs, o_refs, send_sems, recv_sems, local_sems):
        _, _, _, _, _, mine, first = plan(x_refs, o_refs, send_sems, recv_sems, local_sems)
        for cp in mine + first:
            cp.start()

    def finish(x_refs, o_refs, send_sems, recv_sems, local_sems):
        me, sib, c, chips, copy, mine, first = plan(x_refs, o_refs, send_sems, recv_sems, local_sems)
        passed = []
        for j, chip in enumerate(chips):
            for a in range(n):
                copy(a, 1 + j, chip + (c,), me).wait_recv()
                passed.append(copy(a, 4 + j, chip + (c,), sib))
                passed[-1].start()
        for a in range(n):
            copy(a, 0, sib, me).wait_recv()
        for j, chip in enumerate(chips):
            for a in range(n):
                copy(a, 4 + j, chip + (1 - c,), me).wait_recv()
        for cp in first + passed:
            cp.wait_send()
        for cp in mine:
            cp.wait()

    return _Side(xs, [_sds((N_DEV,) + x.shape, x.dtype) for x in xs], 7 * n, n, start, finish)


def _gather_all(xs, name):
    return _run_side(_gather_side(xs), name)


def _swap_side(xs):
    n = len(xs)

    def plan(x_refs, o_refs, send_sems, recv_sems, _):
        x, y, c = _mesh_pos()
        return [_remote(x_refs[a].at[q, 1 - c], o_refs[a].at[q], send_sems, recv_sems, 4 * a + q, (x, y, 1 - c))
                for a in range(n) for q in range(4)]

    def start(*refs):
        for cp in plan(*refs):
            cp.start()

    def finish(*refs):
        copies = plan(*refs)
        for cp in copies:
            cp.wait_recv()
        for cp in copies:
            cp.wait_send()

    return _Side(xs, [_sds((4,) + x.shape[2:], x.dtype) for x in xs], 4 * n, 0, start, finish)


def _quad_side(xs):
    n = len(xs)

    def plan(x_refs, o_refs, send_sems, recv_sems, local_sems):
        x, y, c = _mesh_pos()
        q = 2 * x + y
        chips = [(1 - x, y), (x, 1 - y), (1 - x, 1 - y)]
        mine = [pltpu.make_async_copy(x_refs[a].at[q], o_refs[a].at[q], local_sems.at[a]) for a in range(n)]
        sends, arrivals = [], []
        for a in range(n):
            for j, chip in enumerate(chips):
                qj = 2 * chip[0] + chip[1]
                sends.append(_remote(x_refs[a].at[qj], o_refs[a].at[q], send_sems, recv_sems, 3 * a + j, chip + (c,)))
                arrivals.append(_remote(x_refs[a].at[qj], o_refs[a].at[qj], send_sems, recv_sems, 3 * a + j,
                                        chip + (c,)))
        return mine, sends, arrivals

    def start(*refs):
        mine, sends, _ = plan(*refs)
        for cp in mine + sends:
            cp.start()

    def finish(*refs):
        mine, sends, arrivals = plan(*refs)
        for cp in arrivals:
            cp.wait_recv()
        for cp in sends:
            cp.wait_send()
        for cp in mine:
            cp.wait()

    return _Side(xs, [_sds(x.shape, x.dtype) for x in xs], 3 * n, n, start, finish)


def _pair_add(xs, rs, name):
    n = len(xs)

    def body(*refs):
        c = lax.axis_index("c")
        for x_ref, r_ref, o_ref in zip(refs[:n], refs[n:2 * n], refs[2 * n:]):
            o_ref[...] = (x_ref[c].astype(F32) + r_ref[...].astype(F32)).astype(o_ref.dtype)

    slot = lambda x: pl.BlockSpec((None,) + x.shape[2:], lambda q: (q, 0, 0))
    return pl.pallas_call(
        body, grid=(4,),
        in_specs=[pl.BlockSpec((None, 2) + x.shape[2:], lambda q: (q, 0, 0, 0)) for x in xs] + [slot(x) for x in xs],
        out_specs=[slot(x) for x in xs], out_shape=[_sds((4,) + x.shape[2:], x.dtype) for x in xs],
        compiler_params=_params("parallel"), name=name)(*xs, *rs)


def _pack_rows(n):
    r = -(-n // PACK_COLS)
    return -(-r // 256) * 256 if r > 256 else -(-r // 16) * 16


def _pack(arrs, dtype):
    flat = jnp.concatenate([a.reshape(-1).astype(dtype) for a in arrs])
    rows = _pack_rows(flat.shape[0])
    return jnp.pad(flat, (0, rows * PACK_COLS - flat.shape[0])).reshape(rows, PACK_COLS)


def _pack_shards(arrs):
    flat = jnp.concatenate([a.astype(F32) for a in arrs], axis=1)
    rows = _pack_rows(flat.shape[1])
    return jnp.pad(flat, ((0, 0), (0, rows * PACK_COLS - flat.shape[1]))).reshape(N_DEV, rows, PACK_COLS)


def _unpack(packed, shapes, lead=()):
    flat = packed.reshape(tuple(lead) + (-1,))
    out, off = [], 0
    for s in shapes:
        n = 1
        for v in s:
            n *= v
        out.append(flat[..., off:off + n].reshape(tuple(lead) + tuple(s)))
        off += n
    return out


def _to_shards(full, ax):
    s = full.shape
    t = full.reshape(s[:ax] + (N_DEV, s[ax] // N_DEV) + s[ax + 1:])
    return jnp.moveaxis(t, ax, 0).reshape(N_DEV, -1)


def _from_shards(shards, local_shape, ax):
    t = jnp.moveaxis(shards.reshape((N_DEV,) + tuple(local_shape)), 0, ax)
    s = t.shape
    return t.reshape(s[:ax] + (s[ax] * s[ax + 1],) + s[ax + 2:])


def _mod_fwd(c_rows, w_mod, b_mod, name):
    nl, d, n = w_mod.shape
    r = c_rows.shape[0]

    def body(c_ref, w_ref, b_ref, o_ref):
        s = _silu(c_ref[...])
        for l in range(nl):
            o_ref[l] = _dot(s, w_ref[l]) + b_ref[l]

    return pl.pallas_call(body, out_shape=_sds((nl, r, n)),
                          compiler_params=pltpu.CompilerParams(vmem_limit_bytes=VMEM_LIMIT), name=name)(
        c_rows, w_mod, b_mod)


def _mod_bwd(c_rows, dcols, dall, w_mod, c_ctx, name):
    nl, d, n = w_mod.shape
    r = c_rows.shape[0]

    def body(c_ref, dc_ref, da_ref, w_ref, cc_ref, gw_ref, gb_ref, gc_ref):
        s = _silu(c_ref[...])
        ds = jnp.zeros((r, d), F32)
        for l in range(nl):
            gw_ref[l] = _dot_tn(s, dc_ref[l])
            gb_ref[l] = _rowsum(da_ref[l])
            ds = ds + _dot_nt(dc_ref[l], w_ref[l])
        row = lax.broadcasted_iota(jnp.int32, (r, d), 0)
        gc_ref[...] = _rowsum(jnp.where(row % 4 >= 2, ds, 0.0)) * _dsilu(cc_ref[...])

    return pl.pallas_call(body, out_shape=[_sds((nl, d, n)), _sds((nl, 1, 3 * d)), _sds((1, d))],
                          compiler_params=pltpu.CompilerParams(vmem_limit_bytes=VMEM_LIMIT), name=name)(
        c_rows, dcols, dall, w_mod, c_ctx)


def _adam_math(w, gsum, m, v):
    c1, c2 = 1.0 - ADAM_B1 ** ADAM_STEP, 1.0 - ADAM_B2 ** ADAM_STEP
    mn = ADAM_B1 * m + (1.0 - ADAM_B1) * gsum
    vn = ADAM_B2 * v + (1.0 - ADAM_B2) * (gsum * gsum)
    return -ADAM_LR * ((mn / c1) / (jnp.sqrt(vn / c2) + ADAM_EPS) + ADAM_WD * w), mn, vn


def _adam(w, gparts, row0, m, v, name):
    rows, cols = w.shape
    npart = gparts.shape[0]
    if rows % 8:
        tr = rows
        assert row0 == 0 and gparts.shape[1] == rows
    else:
        tr = max(t for t in (512, 256, 128, 64, 32, 16, 8) if rows % t == 0 and row0 % t == 0
                 and (t * cols <= 256 * 1024 or t == 8))

    def body(w_ref, g_ref, m_ref, v_ref, go_ref, d_ref, mo_ref, vo_ref):
        gsum = g_ref[0].astype(F32)
        for p in range(1, npart):
            gsum = gsum + g_ref[p].astype(F32)
        go_ref[...] = gsum
        d_ref[...], mo_ref[...], vo_ref[...] = _adam_math(w_ref[...], gsum, m_ref[...], v_ref[...])

    spec = _row(tr, cols)
    return pl.pallas_call(
        body, grid=(rows // tr,),
        in_specs=[spec, pl.BlockSpec((npart, tr, cols), lambda i: (0, row0 // tr + i, 0)), spec, spec],
        out_specs=[spec] * 4, out_shape=[_sds((rows, cols))] * 4, compiler_params=_params("parallel"),
        name=name)(w, gparts, m, v)


INPUTS = ['x', 'c', 'ctx'] + WEIGHTS + ['loss_target'] + ['m_' + n for n in WEIGHTS] + ['v_' + n for n in WEIGHTS]
AXES = ("x", "y", "c")
LAYER_MATS = (('cv_w_in', 'cv_w_out'), ('pl_w_in', 'pl_w_grp', 'pl_w_out'),
              ('ml_w_in', 'ml_w_uq', 'ml_w_ukv', 'ml_w_out'), ('ch_w_in', 'ch_w_out'))
GATHERS = {'fwd1': LAYER_MATS[1], 'fwd2': ('ml_w_in', 'ml_w_uq'), 'fwd3': ('ml_w_ukv', 'ml_w_out'), 'fwd4': LAYER_MATS[3]}
GRAD_GROUPS = (('cv_w_in',), ('pl_w_in', 'pl_w_grp', 'pl_w_out', 'cv_w_out'), LAYER_MATS[2], LAYER_MATS[3])
KINDS = ('grad_', 'delta_', 'new_m_', 'new_v_')


def _squeeze_layer(name, a):
    return a if name == 'norm_g' or a.ndim < 3 else a[0]


def _as2d(a):
    return a.reshape(-1, a.shape[-1])


class _Exchanges:
    def __init__(self, a, w):
        self.a, self.w, self.bufs, self.sums, self.quad = a, w, {}, {}, {}

    def mats(self, names):
        return [_as2d(self.a[n]).astype(BF16) for n in names]

    def take_weights(self, names, bufs):
        for n, buf in zip(names, bufs):
            self.w[n] = _squeeze_layer(n, _from_shards(buf, self.a[n].shape, SHARD_AXIS[n]))

    def side(self, tag):
        if tag in GATHERS:
            return _gather_side(self.mats(GATHERS[tag]))
        group = int(tag[-1])
        return _swap_side(self.bufs[group]) if tag.startswith('swap') else _quad_side(self.sums[group])

    def done(self, tag, brought):
        if tag in GATHERS:
            self.take_weights(GATHERS[tag], brought)
        elif tag.startswith('swap'):
            group = int(tag[-1])
            self.sums[group] = _pair_add(self.bufs[group], brought, "grads_add_cores_%d" % group)
        else:
            self.quad[int(tag[-1])] = brought

    def shard_major(self, n, gn):
        if gn.ndim == 3 and gn.shape[0] == N_DEV and gn.dtype == BF16:
            return gn
        whole = tuple(N_DEV * s if i == SHARD_AXIS[n] else s for i, s in enumerate(self.a[n].shape))
        return _to_shards(gn.reshape(whole), SHARD_AXIS[n]).reshape((N_DEV,) + _as2d(self.a[n]).shape)

    def grads_ready(self, group, g):
        bufs = [self.shard_major(n, g[n]).astype(BF16) for n in GRAD_GROUPS[group]]
        if group == 0:
            bufs.append(_pack_shards([self.shard_major(n, g[n]).reshape(N_DEV, -1) for n in VECTOR_WEIGHTS]))
        self.bufs[group] = [b.reshape((4, 2) + b.shape[1:]) for b in bufs]
        if group == 0:
            self.done('swap0', _run_side(self.side('swap0'), "grads_swap_cores_0"))


def _train_step(a):
    x, c, ctx, tgt = a['x'], a['c'], a['ctx'], a['loss_target']
    d = x.shape[-1]
    nb = x.shape[0]
    dev = 4 * lax.axis_index("x") + 2 * lax.axis_index("y") + lax.axis_index("c")
    local_shape = {n: a[n].shape for n in WEIGHTS}

    w = {n: _squeeze_layer(n, a[n]) for n in WEIGHTS if SHARD_AXIS[n] is None}
    comm = _Exchanges(a, w)
    vec_names = ['c'] + VECTOR_WEIGHTS
    vec_all, = _gather_all([_pack([a[n] for n in vec_names], F32)], "gather_vectors")
    parts = dict(zip(vec_names, _unpack(vec_all, [a[n].shape for n in vec_names], lead=(N_DEV,))))
    for n in VECTOR_WEIGHTS:
        w[n] = _squeeze_layer(n, _from_shards(parts[n], local_shape[n], SHARD_AXIS[n]))
    c_all = parts['c'].reshape(N_DEV * nb, d)
    c_ctx = a['c_ctx'].reshape(1, d)

    w_mod = a['w_mod']
    nl, ncol = w_mod.shape[0], w_mod.shape[2]
    mod_rows = -(-(N_DEV * nb + 1) // 8) * 8
    c_rows = jnp.concatenate([c_all, c_ctx, jnp.zeros((mod_rows - N_DEV * nb - 1, d), F32)], axis=0)
    b_loc = lax.dynamic_slice(a['b_mod'], (0, dev * ncol), (nl, ncol))[:, None, :]
    mod_loc = _mod_fwd(c_rows, w_mod, b_loc, "mod_fwd")
    gathered = _gather_all([mod_loc.reshape(nl * mod_rows, ncol)] + comm.mats(LAYER_MATS[0]), "gather_first")
    comm.take_weights(LAYER_MATS[0], gathered[1:])
    mod_all = gathered[0].reshape(N_DEV, nl, mod_rows, ncol).transpose(1, 2, 0, 3).reshape(nl, mod_rows, N_DEV * ncol)
    ctx_row = mod_all[:, N_DEV * nb:N_DEV * nb + 1]
    mods = jnp.concatenate([lax.dynamic_slice(mod_all, (0, dev * nb, 0), (nl, nb, 3 * d))] + [ctx_row] * nb, axis=1)

    loss_vec, grad_x, g, dmods, dnorm_g = _local_step(x, ctx, tgt, w, mods, comm)
    loss = lax.psum(jnp.sum(loss_vec), AXES)

    nseg = dmods.shape[1]
    dm_all, = _gather_all([dmods.reshape(nl * nseg, 3 * d)], "gather_dmods")
    dm_all = dm_all.reshape(N_DEV, nl, nseg, 3 * d).transpose(1, 0, 2, 3).reshape(nl, N_DEV * nseg, 3 * d)
    dcols = lax.dynamic_slice(dm_all, (0, 0, dev * ncol), (nl, N_DEV * nseg, ncol))
    c_rows_b = jnp.concatenate([c_all.reshape(N_DEV, nb, d), jnp.broadcast_to(c_ctx, (N_DEV, nb, d))], axis=1)
    g_w_mod, g_b_mod, g_c_ctx = _mod_bwd(c_rows_b.reshape(N_DEV * nseg, d), dcols, dm_all, w_mod, c_ctx, "mod_bwd")
    g['c_ctx'], g['norm_g'] = g_c_ctx, dnorm_g
    rep_all, = _gather_all([_pack([g[n] for n in REPLICATED], F32)], "gather_replicated_grads")

    out = {}

    def keep(names, res, shapes=None):
        for kind, val in zip(KINDS, res):
            if shapes is None:
                out[kind + names[0]] = val.reshape(local_shape[names[0]])
            else:
                for n, leaf in zip(names, _unpack(val, shapes)):
                    out[kind + n] = leaf

    def update_packed(names, gparts, tag):
        res = _adam(_pack([a[n] for n in names], F32), gparts, 0, _pack([a['m_' + n] for n in names], F32),
                    _pack([a['v_' + n] for n in names], F32), "adam_" + tag)
        keep(names, res, [local_shape[n] for n in names])

    for group, names in enumerate(GRAD_GROUPS):
        for n, gparts in zip(names, comm.quad[group]):
            keep([n], _adam(_as2d(a[n]), gparts, 0, _as2d(a['m_' + n]), _as2d(a['v_' + n]), "adam_" + n))
    update_packed(VECTOR_WEIGHTS, comm.quad[0][-1], "vectors")
    update_packed(REPLICATED, rep_all, "replicated")
    keep(['w_mod'], _adam(_as2d(w_mod), _as2d(g_w_mod)[None], 0, _as2d(a['m_w_mod']), _as2d(a['v_w_mod']),
                          "adam_w_mod"))
    keep(['b_mod'], _adam(a['b_mod'], g_b_mod.reshape((1,) + a['b_mod'].shape), 0, a['m_b_mod'], a['v_b_mod'],
                          "adam_b_mod"))
    return (loss, grad_x) + tuple(out[kind + n] for kind in KINDS for n in WEIGHTS)


def kernel(x, c, ctx, c_ctx, norm_g, w_mod, b_mod, cv_w_in, cv_dw, cv_db, cv_ln_g, cv_ln_b, cv_w_out, pl_w_in, pl_w_grp, pl_scale, pl_w_out, ml_w_in, ml_q_norm, ml_kv_norm, ml_w_uq, ml_w_ukv, ml_nope_norm, ml_rope_norm, ml_w_out, ch_w_in, ch_ln_g, ch_ln_b, ch_w_s, ch_b_s, ch_w_out, loss_target, m_c_ctx, m_norm_g, m_w_mod, m_b_mod, m_cv_w_in, m_cv_dw, m_cv_db, m_cv_ln_g, m_cv_ln_b, m_cv_w_out, m_pl_w_in, m_pl_w_grp, m_pl_scale, m_pl_w_out, m_ml_w_in, m_ml_q_norm, m_ml_kv_norm, m_ml_w_uq, m_ml_w_ukv, m_ml_nope_norm, m_ml_rope_norm, m_ml_w_out, m_ch_w_in, m_ch_ln_g, m_ch_ln_b, m_ch_w_s, m_ch_b_s, m_ch_w_out, v_c_ctx, v_norm_g, v_w_mod, v_b_mod, v_cv_w_in, v_cv_dw, v_cv_db, v_cv_ln_g, v_cv_ln_b, v_cv_w_out, v_pl_w_in, v_pl_w_grp, v_pl_scale, v_pl_w_out, v_ml_w_in, v_ml_q_norm, v_ml_kv_norm, v_ml_w_uq, v_ml_w_ukv, v_ml_nope_norm, v_ml_rope_norm, v_ml_w_out, v_ch_w_in, v_ch_ln_g, v_ch_ln_b, v_ch_w_s, v_ch_b_s, v_ch_w_out):
    return _train_step(dict(zip(INPUTS, (x, c, ctx, c_ctx, norm_g, w_mod, b_mod, cv_w_in, cv_dw, cv_db, cv_ln_g, cv_ln_b, cv_w_out, pl_w_in, pl_w_grp, pl_scale, pl_w_out, ml_w_in, ml_q_norm, ml_kv_norm, ml_w_uq, ml_w_ukv, ml_nope_norm, ml_rope_norm, ml_w_out, ch_w_in, ch_ln_g, ch_ln_b, ch_w_s, ch_b_s, ch_w_out, loss_target, m_c_ctx, m_norm_g, m_w_mod, m_b_mod, m_cv_w_in, m_cv_dw, m_cv_db, m_cv_ln_g, m_cv_ln_b, m_cv_w_out, m_pl_w_in, m_pl_w_grp, m_pl_scale, m_pl_w_out, m_ml_w_in, m_ml_q_norm, m_ml_kv_norm, m_ml_w_uq, m_ml_w_ukv, m_ml_nope_norm, m_ml_rope_norm, m_ml_w_out, m_ch_w_in, m_ch_ln_g, m_ch_ln_b, m_ch_w_s, m_ch_b_s, m_ch_w_out, v_c_ctx, v_norm_g, v_w_mod, v_b_mod, v_cv_w_in, v_cv_dw, v_cv_db, v_cv_ln_g, v_cv_ln_b, v_cv_w_out, v_pl_w_in, v_pl_w_grp, v_pl_scale, v_pl_w_out, v_ml_w_in, v_ml_q_norm, v_ml_kv_norm, v_ml_w_uq, v_ml_w_ukv, v_ml_nope_norm, v_ml_rope_norm, v_ml_w_out, v_ch_w_in, v_ch_ln_g, v_ch_ln_b, v_ch_w_s, v_ch_b_s, v_ch_w_out))))
```

```python
import functools

import jax
import jax.numpy as jnp
from jax import lax
from jax.experimental import pallas as pl
from jax.experimental.pallas import tpu as pltpu

F32 = jnp.float32
BF16 = jnp.bfloat16

N_DEV = 8
EPS = 1e-6
CONV_WIDTH = 31
CONV_PAD = 16
POOL_WINDOWS = (2, 4, 8, 16)
POOL_TAPS = 16
MLA_HEADS = 8
MLA_NOPE = 128
MLA_ROPE = 64
MLA_Q_RANK = 384
MLA_KV_RANK = 256
MLA_SCALE = (MLA_NOPE + MLA_ROPE) ** -0.5
ROPE_THETA = 10000.0
GRID_W = 64
HEAD_W = 256
ATTN_CHAINS = 2
CHUNK = 128
CHUNK_GROUPS = 8
CHUNK_TILES = (512, 256, 128)
ADAM_LR = 0.001
ADAM_B1 = 0.9
ADAM_B2 = 0.999
ADAM_EPS = 1e-08
ADAM_WD = 0.01
ADAM_STEP = 10
LANES = 128
VMEM_LIMIT = 56 * 1024 * 1024
PACK_COLS = 1024

WEIGHTS = ['c_ctx', 'norm_g', 'w_mod', 'b_mod', 'cv_w_in', 'cv_dw', 'cv_db', 'cv_ln_g', 'cv_ln_b', 'cv_w_out',
           'pl_w_in', 'pl_w_grp', 'pl_scale', 'pl_w_out', 'ml_w_in', 'ml_q_norm', 'ml_kv_norm', 'ml_w_uq',
           'ml_w_ukv', 'ml_nope_norm', 'ml_rope_norm', 'ml_w_out', 'ch_w_in', 'ch_ln_g', 'ch_ln_b', 'ch_w_s',
           'ch_b_s', 'ch_w_out']
SHARD_AXIS = {'c_ctx': None, 'norm_g': None, 'w_mod': 2, 'b_mod': None, 'cv_w_in': 2, 'cv_dw': 2, 'cv_db': None,
              'cv_ln_g': None, 'cv_ln_b': None, 'cv_w_out': 1, 'pl_w_in': 2, 'pl_w_grp': 2, 'pl_scale': 1,
              'pl_w_out': 1, 'ml_w_in': 2, 'ml_q_norm': 1, 'ml_kv_norm': 1, 'ml_w_uq': 2, 'ml_w_ukv': 2,
              'ml_nope_norm': None, 'ml_rope_norm': None, 'ml_w_out': 1, 'ch_w_in': 2, 'ch_ln_g': 1, 'ch_ln_b': 1,
              'ch_w_s': None, 'ch_b_s': None, 'ch_w_out': 1}
VECTOR_WEIGHTS = ['cv_dw', 'pl_scale', 'ml_q_norm', 'ml_kv_norm', 'ch_ln_g', 'ch_ln_b']
REPLICATED = ['c_ctx', 'norm_g', 'cv_db', 'cv_ln_g', 'cv_ln_b', 'ml_nope_norm', 'ml_rope_norm', 'ch_w_s', 'ch_b_s']


def _pick(n, cands):
    for c in cands:
        if n % c == 0:
            return c
    raise ValueError(f"no tile for {n} among {cands}")


def _params(*sem):
    return pltpu.CompilerParams(dimension_semantics=sem, vmem_limit_bytes=VMEM_LIMIT)


def _sig(x):
    return 1.0 / (1.0 + jnp.exp(-x))


def _silu(x):
    return x * _sig(x)


def _dsilu(x):
    s = _sig(x)
    return s * (1.0 + x * (1.0 - s))


def _rowsum(v):
    return jnp.sum(v, axis=0, keepdims=True)


def _dot(a, b):
    return jnp.dot(a.astype(BF16), b.astype(BF16), preferred_element_type=F32)


def _dot_nt(a, b):
    return lax.dot_general(a.astype(BF16), b.astype(BF16), (((1,), (1,)), ((), ())), preferred_element_type=F32)


def _dot_tn(a, b):
    return lax.dot_general(a.astype(BF16), b.astype(BF16), (((0,), (0,)), ((), ())), preferred_element_type=F32)


class _Segs:
    def __init__(self, lens, tm):
        self.lens, self.tm, self.n = tuple(lens), tm, len(lens)
        self.starts, s = [], 0
        for l in lens:
            assert l % tm == 0
            self.starts.append(s // tm)
            s += l
        self.rows, self.tiles = s, s // tm

    def seg(self, i):
        r = 0
        for st in self.starts[1:]:
            r = r + jnp.where(i >= st, 1, 0)
        return r

    def is_first(self, i):
        f = i == 0
        for st in self.starts[1:]:
            f = jnp.logical_or(f, i == st)
        return f

    def spec(self, cols):
        return pl.BlockSpec((None, 1, cols), lambda i: (self.seg(i), 0, 0))


def _row(tm, cols, cb=0):
    return pl.BlockSpec((tm, cols), lambda i: (i, cb))


def _const(shape):
    return pl.BlockSpec(shape, lambda *_: (0,) * len(shape))


def _sds(shape, dtype=F32):
    return jax.ShapeDtypeStruct(shape, dtype)


class _Side:
    def __init__(self, xs, out_shapes, n_remote, n_local, start, finish):
        self.xs, self.out_shapes, self.n_remote, self.n_local = list(xs), list(out_shapes), n_remote, n_local
        self.start, self.finish = start, finish


def _sem_shapes(n_remote, n_local):
    return [pltpu.SemaphoreType.DMA((n_remote,)), pltpu.SemaphoreType.DMA((n_remote,)),
            pltpu.SemaphoreType.DMA((max(n_local, 1),))]


def _pallas(body, args, *, grid, in_specs, out_specs, out_shape, sem, name, scratch_shapes=(), side=None, into=None):
    aliases = {}
    if into:
        inner, n_args = body, len(args)

        def body(*refs):
            inner(*refs[:n_args], *refs[n_args + len(into):])

        aliases = {n_args + k: o for k, o in enumerate(sorted(into))}
        args = tuple(args) + tuple(into[o] for o in sorted(into))
        in_specs = list(in_specs) + [pl.BlockSpec(memory_space=pl.ANY)] * len(into)
    if side is None:
        return pl.pallas_call(body, grid=grid, in_specs=in_specs, out_specs=out_specs, out_shape=out_shape,
                              scratch_shapes=list(scratch_shapes), input_output_aliases=aliases,
                              compiler_params=_params(*sem), name=name)(*args)
    multi = isinstance(out_shape, (list, tuple))
    out_specs, out_shape = (list(out_specs), list(out_shape)) if multi else ([out_specs], [out_shape])
    ni, no, ns, si, so = len(in_specs), len(out_specs), len(scratch_shapes), len(side.xs), len(side.out_shapes)
    hbm = pl.BlockSpec(memory_space=pltpu.HBM)

    def wrapped(*refs):
        ins, sins, refs = refs[:ni], refs[ni:ni + si], refs[ni + si:]
        outs, souts, refs = refs[:no], refs[no:no + so], refs[no + so:]
        scr, sems = refs[:ns], refs[ns:]
        ids = [pl.program_id(k) for k in range(len(grid))]
        first = functools.reduce(jnp.logical_and, [i == 0 for i in ids])
        last = functools.reduce(jnp.logical_and, [i == n - 1 for i, n in zip(ids, grid)])

        @pl.when(first)
        def _():
            side.start(sins, souts, *sems)

        body(*ins, *outs, *scr)

        @pl.when(last)
        def _():
            side.finish(sins, souts, *sems)

    res = pl.pallas_call(
        wrapped, grid=grid, in_specs=list(in_specs) + [hbm] * si, out_specs=out_specs + [hbm] * so,
        out_shape=out_shape + side.out_shapes,
        scratch_shapes=list(scratch_shapes) + _sem_shapes(side.n_remote, side.n_local), input_output_aliases=aliases,
        compiler_params=pltpu.CompilerParams(dimension_semantics=("arbitrary",) * len(grid),
                                             vmem_limit_bytes=VMEM_LIMIT, has_side_effects=True),
        name=name)(*args, *side.xs)
    return (list(res[:no]) if multi else res[0]), list(res[no:])


N_TILES = (1024, 896, 768, 512, 384, 256, 128)
M_TILES = (1536, 1024, 768, 512, 256, 128)


def _mm(a, b, name, out_dtype=F32, rows=None, side=None, resid=None):
    m, k, n = rows or a.shape[0], a.shape[1], b.shape[1]
    tm, tn = _pick(m, M_TILES), _pick(n, N_TILES)
    if resid is not None:
        x, gt, segs = resid
        pieces = tm // segs.tm

        def body(a_ref, b_ref, x_ref, *rest):
            gt_refs, (o_ref, y_ref) = rest[:pieces], rest[pieces:]
            o = _dot(a_ref[...], b_ref[...])
            o_ref[...] = o
            for c in range(pieces):
                rows = slice(c * segs.tm, (c + 1) * segs.tm)
                y_ref[rows, :] = x_ref[rows, :] + gt_refs[c][...] * o[rows, :]

        tile = pl.BlockSpec((tm, tn), lambda j, i: (i, j))
        gt_specs = [pl.BlockSpec((None, 1, tn), lambda j, i, c=c: (segs.seg(i * pieces + c), 0, j))
                    for c in range(pieces)]
        return pl.pallas_call(
            body, grid=(n // tn, m // tm),
            in_specs=[pl.BlockSpec((tm, k), lambda j, i: (i, 0)), pl.BlockSpec((k, tn), lambda j, i: (0, j)), tile]
            + gt_specs, out_specs=[tile, tile], out_shape=[_sds((m, n)), _sds((m, n))],
            compiler_params=_params("parallel", "parallel"), name=name)(a, b, x, *([gt] * pieces))

    def body(a_ref, b_ref, o_ref):
        o_ref[...] = _dot(a_ref[...], b_ref[...]).astype(o_ref.dtype)

    return _pallas(
        body, (a, b), grid=(n // tn, m // tm),
        in_specs=[pl.BlockSpec((tm, k), lambda j, i: (i, 0)), pl.BlockSpec((k, tn), lambda j, i: (0, j))],
        out_specs=pl.BlockSpec((tm, tn), lambda j, i: (i, j)), out_shape=_sds((m, n), out_dtype),
        sem=("parallel", "parallel"), name=name, side=side)


def _mm_nt(a, b, name, out_dtype=F32, side=None):
    m, k, n = a.shape[0], a.shape[1], b.shape[0]
    tm, tn = _pick(m, M_TILES[1:] if k > N_TILES[0] else M_TILES), _pick(n, N_TILES)

    def body(a_ref, b_ref, o_ref):
        o_ref[...] = _dot_nt(a_ref[...], b_ref[...]).astype(o_ref.dtype)

    return _pallas(
        body, (a, b), grid=(n // tn, m // tm),
        in_specs=[pl.BlockSpec((tm, k), lambda j, i: (i, 0)), pl.BlockSpec((tn, k), lambda j, i: (j, 0))],
        out_specs=pl.BlockSpec((tm, tn), lambda j, i: (i, j)), out_shape=_sds((m, n), out_dtype),
        sem=("parallel", "parallel"), name=name, side=side)


def _mm_nt_sum(pairs, name):
    m, n = pairs[0][0].shape[0], pairs[0][1].shape[0]
    tm, tn = _pick(m, (512, 256, 128)), _pick(n, N_TILES)
    tiles = [a.shape[0] // tm for a, _ in pairs]
    assert all(a.shape[0] % tm == 0 for a, _ in pairs)

    def body(*refs):
        o_ref, i = refs[-1], pl.program_id(1)
        acc = _dot_nt(refs[0][...], refs[1][...])
        for p in range(1, len(pairs)):
            acc = acc + jnp.where(i < tiles[p], _dot_nt(refs[2 * p][...], refs[2 * p + 1][...]), 0.0)
        o_ref[...] = acc

    in_specs = []
    for (a, b), nt in zip(pairs, tiles):
        in_specs += [pl.BlockSpec((tm, a.shape[1]), lambda j, i, nt=nt: (jnp.minimum(i, nt - 1), 0)),
                     pl.BlockSpec((tn, a.shape[1]), lambda j, i: (j, 0))]
    return pl.pallas_call(
        body, grid=(n // tn, m // tm), in_specs=in_specs, out_specs=pl.BlockSpec((tm, tn), lambda j, i: (i, j)),
        out_shape=_sds((m, n)), compiler_params=_params("parallel", "parallel"),
        name=name)(*[t for pair in pairs for t in pair])


def _mm_tn(a, b, name, rows=None, shards=None, side=None):
    t, k, n = rows or a.shape[0], a.shape[1], b.shape[1]
    tk, tt = _pick(k, N_TILES), _pick(t, M_TILES)
    if shards:
        width = n // shards
        per_tile = max(c for c in (8, 4, 2, 1) if shards % c == 0 and c * width <= N_TILES[0])
        tn = per_tile * width
    else:
        tn = _pick(n, N_TILES)
    assert tn % LANES == 0

    def body(a_ref, b_ref, o_ref, acc_ref):
        @pl.when(pl.program_id(2) == 0)
        def _():
            acc_ref[...] = jnp.zeros_like(acc_ref)

        acc_ref[...] += _dot_tn(a_ref[...], b_ref[...])

        @pl.when(pl.program_id(2) == pl.num_programs(2) - 1)
        def _():
            if shards:
                for c in range(per_tile):
                    o_ref[c] = acc_ref[:, c * width:(c + 1) * width].astype(o_ref.dtype)
            else:
                o_ref[...] = acc_ref[...]

    if shards:
        out_spec = pl.BlockSpec((per_tile, tk, width), lambda i, j, s: (j, i, 0))
        out_shape = _sds((shards, k, width), BF16)
    else:
        out_spec, out_shape = pl.BlockSpec((tk, tn), lambda i, j, s: (i, j)), _sds((k, n))
    return _pallas(
        body, (a, b), grid=(k // tk, n // tn, t // tt),
        in_specs=[pl.BlockSpec((tt, tk), lambda i, j, s: (s, i)), pl.BlockSpec((tt, tn), lambda i, j, s: (s, j))],
        out_specs=out_spec, out_shape=out_shape, scratch_shapes=[pltpu.VMEM((tk, tn), F32)],
        sem=("parallel", "parallel", "arbitrary"), name=name, side=side)


def _rms_mod_fwd(x, g, sc, sh, segs, name):
    d, tm = x.shape[1], segs.tm

    def body(x_ref, g_ref, sc_ref, sh_ref, h_ref):
        xf = x_ref[...]
        r = lax.rsqrt(jnp.mean(xf * xf, axis=-1, keepdims=True) + EPS)
        h_ref[...] = ((xf * r * g_ref[...]) * (1.0 + sc_ref[...]) + sh_ref[...]).astype(h_ref.dtype)

    return pl.pallas_call(
        body, grid=(segs.tiles,), in_specs=[_row(tm, d), _const((1, d)), segs.spec(d), segs.spec(d)],
        out_specs=_row(tm, d), out_shape=_sds((segs.rows, d), BF16), compiler_params=_params("parallel"),
        name=name)(x, g, sc, sh)


def _rms_mod_bwd(x, g, sc, sh, dh, dxr, segs, name, dx_rows=None):
    d, tm = x.shape[1], segs.tm
    dxr_tiles = dxr.shape[0] // tm
    dx_tiles = (dx_rows or segs.rows) // tm

    def body(x_ref, g_ref, sc_ref, sh_ref, dh_ref, dxr_ref, dx_ref, dg_ref, dsc_ref, dsh_ref):
        i = pl.program_id(0)

        @pl.when(i == 0)
        def _():
            dg_ref[...] = jnp.zeros_like(dg_ref)

        @pl.when(segs.is_first(i))
        def _():
            dsc_ref[...] = jnp.zeros_like(dsc_ref)
            dsh_ref[...] = jnp.zeros_like(dsh_ref)

        xf, gg, dhf = x_ref[...], g_ref[...], dh_ref[...].astype(F32)
        dxr = jnp.where(i < dxr_tiles, dxr_ref[...], 0.0)
        r = lax.rsqrt(jnp.mean(xf * xf, axis=-1, keepdims=True) + EPS)
        xh = xf * r
        dsh_ref[...] += _rowsum(dhf)
        dsc_ref[...] += _rowsum(dhf * (xh * gg))
        du = dhf * (1.0 + sc_ref[...])
        dg_ref[...] += _rowsum(du * xh)
        dxh = du * gg
        @pl.when(i < dx_tiles)
        def _():
            dx_ref[...] = dxr + r * (dxh - xh * jnp.mean(dxh * xh, axis=-1, keepdims=True))

    return pl.pallas_call(
        body, grid=(segs.tiles,),
        in_specs=[_row(tm, d), _const((1, d)), segs.spec(d), segs.spec(d), _row(tm, d),
                  pl.BlockSpec((tm, d), lambda i: (jnp.minimum(i, dxr_tiles - 1), 0))],
        out_specs=[pl.BlockSpec((tm, d), lambda i: (jnp.minimum(i, dx_tiles - 1), 0)), _const((1, d)), segs.spec(d),
                   segs.spec(d)],
        out_shape=[_sds((dx_tiles * tm, d)), _sds((1, d)), _sds((segs.n, 1, d)), _sds((segs.n, 1, d))],
        compiler_params=_params("arbitrary"), name=name)(x, g, sc, sh, dh, dxr)


def _resid_bwd(dxn, o, gt, segs, name):
    d, tm = o.shape[1], segs.tm

    def body(dxn_ref, o_ref, gt_ref, do_ref, dgt_ref):
        @pl.when(segs.is_first(pl.program_id(0)))
        def _():
            dgt_ref[...] = jnp.zeros_like(dgt_ref)

        dx = dxn_ref[...]
        do_ref[...] = (gt_ref[...] * dx).astype(do_ref.dtype)
        dgt_ref[...] += _rowsum(dx * o_ref[...])

    return pl.pallas_call(
        body, grid=(segs.tiles,), in_specs=[_row(tm, d), _row(tm, d), segs.spec(d)],
        out_specs=[_row(tm, d), segs.spec(d)], out_shape=[_sds((segs.rows, d), BF16), _sds((segs.n, 1, d))],
        compiler_params=_params("arbitrary"), name=name)(dxn, o, gt)


def _loss_head(y, tgt, tm, name):
    t, d = y.shape

    def body(y_ref, t_ref, l_ref, dy_ref):
        @pl.when(pl.program_id(0) == 0)
        def _():
            l_ref[...] = jnp.zeros_like(l_ref)

        e = y_ref[...] - t_ref[...]
        dy_ref[...] = e * (1.0 / d)
        l_ref[...] += _rowsum(e * e) * (0.5 / d)

    return pl.pallas_call(
        body, grid=(t // tm,), in_specs=[_row(tm, d), _row(tm, d)], out_specs=[_const((1, d)), _row(tm, d)],
        out_shape=[_sds((1, d)), _sds((t, d))], compiler_params=_params("arbitrary"), name=name)(y, tgt)


def _seq_spec(l, ce, row0, cb0=0):
    return pl.BlockSpec((l, ce), lambda j, s: (row0 // l + s, cb0 + j))


def _tap_sum(pad_ref, taps_ref, first_row, n_taps, l, ce, flip):
    out = []
    for r0 in range(0, l, CHUNK):
        rows = min(CHUNK, l - r0)
        acc = jnp.zeros((rows, ce), F32)
        for k in range(n_taps):
            kk = n_taps - 1 - k if flip else k
            acc = acc + pad_ref[pl.ds(first_row + r0 + k, rows), :] * taps_ref[kk:kk + 1, :]
        out.append(acc)
    return out


def _fill_pad(pad_ref, val, l, ce):
    pad_ref[pl.ds(0, CONV_PAD), :] = jnp.zeros((CONV_PAD, ce), F32)
    pad_ref[pl.ds(CONV_PAD + l, CONV_PAD), :] = jnp.zeros((CONV_PAD, ce), F32)
    pad_ref[pl.ds(CONV_PAD, l), :] = val


def _conv1_fwd(z, dw, db, nseq, l, row0, name, side=None, into=None):
    e = z.shape[1] // 3
    ce = LANES
    half = CONV_WIDTH // 2

    def body(a_ref, b_ref, dw_ref, db_ref, y_ref, pad_ref):
        _fill_pad(pad_ref, a_ref[...] * _sig(b_ref[...]), l, ce)
        pieces = _tap_sum(pad_ref, dw_ref, CONV_PAD - half, CONV_WIDTH, l, ce, False)
        for n, acc in enumerate(pieces):
            y_ref[pl.ds(n * CHUNK, acc.shape[0]), :] = acc + db_ref[...]

    return _pallas(
        body, (z, z, dw, db), grid=(e // ce, nseq),
        in_specs=[_seq_spec(l, ce, row0), _seq_spec(l, ce, row0, e // ce),
                  pl.BlockSpec((CONV_WIDTH, ce), lambda j, s: (0, j)), pl.BlockSpec((1, ce), lambda j, s: (0, j))],
        out_specs=_seq_spec(l, ce, row0), out_shape=_sds((z.shape[0], e)),
        scratch_shapes=[pltpu.VMEM((l + 2 * CONV_PAD, ce), F32)],
        sem=("parallel", "arbitrary"), name=name, side=side, into=None if into is None else {0: into})


def _conv1_bwd(dy2, z, dw, acc_dw, acc_db, nseq, l, row0, name, side=None, into=None):
    e = z.shape[1] // 3
    ce = LANES
    half = CONV_WIDTH // 2

    def body(dy_ref, a_ref, b_ref, dw_ref, adw_ref, adb_ref, da_ref, dbb_ref, ddw_ref, ddb_ref, ypad_ref, dpad_ref):
        @pl.when(pl.program_id(1) == 0)
        def _():
            ddw_ref[...] = adw_ref[...]
            ddb_ref[...] = adb_ref[...]

        a, sb = a_ref[...], _sig(b_ref[...])
        dy = dy_ref[...]
        _fill_pad(ypad_ref, a * sb, l, ce)
        _fill_pad(dpad_ref, dy, l, ce)
        ddb_ref[...] += _rowsum(dy)
        for k in range(CONV_WIDTH):
            ddw_ref[k:k + 1, :] += _rowsum(dy * ypad_ref[pl.ds(CONV_PAD - half + k, l), :])
        pieces = _tap_sum(dpad_ref, dw_ref, CONV_PAD - half, CONV_WIDTH, l, ce, True)
        for n, dy1 in enumerate(pieces):
            rows = pl.ds(n * CHUNK, dy1.shape[0])
            sbn = sb[n * CHUNK:n * CHUNK + dy1.shape[0], :]
            da_ref[rows, :] = (dy1 * sbn).astype(da_ref.dtype)
            dbb_ref[rows, :] = (dy1 * a[n * CHUNK:n * CHUNK + dy1.shape[0], :] * sbn * (1.0 - sbn)).astype(dbb_ref.dtype)

    cw = lambda j, s: (0, j)
    return _pallas(
        body, (dy2, z, z, dw, acc_dw, acc_db), grid=(e // ce, nseq),
        in_specs=[_seq_spec(l, ce, row0), _seq_spec(l, ce, row0), _seq_spec(l, ce, row0, e // ce),
                  pl.BlockSpec((CONV_WIDTH, ce), cw), pl.BlockSpec((CONV_WIDTH, ce), cw), pl.BlockSpec((1, ce), cw)],
        out_specs=[_seq_spec(l, ce, row0), _seq_spec(l, ce, row0),
                   pl.BlockSpec((CONV_WIDTH, ce), cw), pl.BlockSpec((1, ce), cw)],
        out_shape=[_sds((z.shape[0], e), BF16), _sds((z.shape[0], e), BF16), _sds((CONV_WIDTH, e)), _sds((1, e))],
        scratch_shapes=[pltpu.VMEM((l + 2 * CONV_PAD, ce), F32), pltpu.VMEM((l + 2 * CONV_PAD, ce), F32)],
        sem=("parallel", "arbitrary"), name=name, side=side,
        into=None if into is None else {0: into[0], 1: into[1]})


def _pool_tables(l, e):
    grp = e // len(POOL_WINDOWS)
    w = jnp.repeat(jnp.array(POOL_WINDOWS, jnp.int32), grp)[None, :]
    off = jnp.arange(POOL_TAPS, dtype=jnp.int32)[:, None] - POOL_TAPS // 2
    taps = jnp.logical_and(off >= -(w // 2), off < w - w // 2).astype(F32)
    t = jnp.arange(l, dtype=jnp.int32)[:, None]
    cnt = jnp.clip(t + (w - w // 2), 0, l) - jnp.clip(t - w // 2, 0, l)
    return taps, 1.0 / cnt.astype(F32)


def _pool1(v_src, taps, inv_cnt, nseq, l, row0, transpose, name, out_dtype, into=None):
    e = taps.shape[1]
    ce = LANES
    half = POOL_TAPS // 2

    def body(v_ref, taps_ref, ic_ref, o_ref, pad_ref):
        v = v_ref[...].astype(F32)
        if transpose:
            _fill_pad(pad_ref, v * ic_ref[...], l, ce)
            pieces = _tap_sum(pad_ref, taps_ref, CONV_PAD - half + 1, POOL_TAPS, l, ce, True)
        else:
            _fill_pad(pad_ref, v, l, ce)
            pieces = _tap_sum(pad_ref, taps_ref, CONV_PAD - half, POOL_TAPS, l, ce, False)
        for n, acc in enumerate(pieces):
            rows = pl.ds(n * CHUNK, acc.shape[0])
            vn = v[n * CHUNK:n * CHUNK + acc.shape[0], :]
            if transpose:
                o_ref[rows, :] = (acc - vn).astype(o_ref.dtype)
            else:
                o_ref[rows, :] = (acc * ic_ref[rows, :] - vn).astype(o_ref.dtype)

    return _pallas(
        body, (v_src, taps, inv_cnt), grid=(e // ce, nseq),
        in_specs=[_seq_spec(l, ce, row0), pl.BlockSpec((POOL_TAPS, ce), lambda j, s: (0, j)),
                  pl.BlockSpec((l, ce), lambda j, s: (0, j))],
        out_specs=_seq_spec(l, ce, row0), out_shape=_sds((v_src.shape[0], e), out_dtype),
        scratch_shapes=[pltpu.VMEM((l + 2 * CONV_PAD, ce), F32)],
        sem=("parallel", "arbitrary"), name=name, into=None if into is None else {0: into})


def _layernorm_parts(x, eps=EPS):
    mu = jnp.mean(x, axis=-1, keepdims=True)
    xc = x - mu
    r = lax.rsqrt(jnp.mean(xc * xc, axis=-1, keepdims=True) + eps)
    return xc * r, r


def _layernorm_bwd(dy, xh, r, g):
    dxh = dy * g
    return r * (dxh - jnp.mean(dxh, axis=-1, keepdims=True) - xh * jnp.mean(dxh * xh, axis=-1, keepdims=True))


def _conv2_fwd(y2, z, ln_g, ln_b, tm, name):
    t, e = y2.shape

    def body(y_ref, g_ref, lg_ref, lb_ref, o_ref):
        xh, _ = _layernorm_parts(y_ref[...])
        o_ref[...] = (_silu(xh * lg_ref[...] + lb_ref[...]) * _silu(g_ref[...])).astype(o_ref.dtype)

    return pl.pallas_call(
        body, grid=(t // tm,), in_specs=[_row(tm, e), _row(tm, e, 2), _const((1, e)), _const((1, e))],
        out_specs=_row(tm, e), out_shape=_sds((t, e), BF16), compiler_params=_params("parallel"),
        name=name)(y2, z, ln_g, ln_b)


def _conv2_bwd(dy4, y2, z, ln_g, ln_b, tm, name):
    t, e = y2.shape

    def body(dy_ref, y_ref, g_ref, lg_ref, lb_ref, dy2_ref, dg_ref, dlg_ref, dlb_ref):
        @pl.when(pl.program_id(0) == 0)
        def _():
            dlg_ref[...] = jnp.zeros_like(dlg_ref)
            dlb_ref[...] = jnp.zeros_like(dlb_ref)

        dy, gz = dy_ref[...], g_ref[...]
        xh, r = _layernorm_parts(y_ref[...])
        y3 = xh * lg_ref[...] + lb_ref[...]
        dg_ref[...] = (dy * _silu(y3) * _dsilu(gz)).astype(dg_ref.dtype)
        dy3 = dy * _silu(gz) * _dsilu(y3)
        dlg_ref[...] += _rowsum(dy3 * xh)
        dlb_ref[...] += _rowsum(dy3)
        dy2_ref[...] = _layernorm_bwd(dy3, xh, r, lg_ref[...])

    return pl.pallas_call(
        body, grid=(t // tm,),
        in_specs=[_row(tm, e), _row(tm, e), _row(tm, e, 2), _const((1, e)), _const((1, e))],
        out_specs=[_row(tm, e), _row(tm, e), _const((1, e)), _const((1, e))],
        out_shape=[_sds((t, e)), _sds((t, e), BF16), _sds((1, e)), _sds((1, e))],
        compiler_params=_params("arbitrary"), name=name)(dy4, y2, z, ln_g, ln_b)


def _pool2_fwd(pm, w_grp, scale, z, tm, name):
    t, e = pm.shape
    ng, gw = w_grp.shape[0], w_grp.shape[1]

    def body(pm_ref, w_ref, sc_ref, g_ref, o_ref):
        for k in range(ng):
            cols = slice(k * gw, (k + 1) * gw)
            y = _dot(pm_ref[:, cols], w_ref[k])
            o_ref[:, cols] = (y * sc_ref[:, cols] * _silu(g_ref[:, cols])).astype(o_ref.dtype)

    return pl.pallas_call(
        body, grid=(t // tm,), in_specs=[_row(tm, e), _const(w_grp.shape), _const((1, e)), _row(tm, e, 1)],
        out_specs=_row(tm, e), out_shape=_sds((t, e), BF16), compiler_params=_params("parallel"),
        name=name)(pm, w_grp, scale, z)


def _pool2_bwd(dy2, pm, w_grp, scale, z, tm, name):
    t, e = pm.shape
    ng, gw = w_grp.shape[0], w_grp.shape[1]

    def body(dy_ref, pm_ref, w_ref, sc_ref, g_ref, dpm_ref, dg_ref, dsc_ref, dw_ref):
        @pl.when(pl.program_id(0) == 0)
        def _():
            dsc_ref[...] = jnp.zeros_like(dsc_ref)
            dw_ref[...] = jnp.zeros_like(dw_ref)

        for k in range(ng):
            cols = slice(k * gw, (k + 1) * gw)
            dy, gz, sc, pmk = dy_ref[:, cols], g_ref[:, cols], sc_ref[:, cols], pm_ref[:, cols]
            y = _dot(pmk, w_ref[k])
            dg_ref[:, cols] = (dy * (y * sc) * _dsilu(gz)).astype(dg_ref.dtype)
            dys = dy * _silu(gz)
            dsc_ref[:, cols] += _rowsum(dys * y)
            dyk = dys * sc
            dpm_ref[:, cols] = _dot_nt(dyk, w_ref[k])
            dw_ref[k] += _dot_tn(pmk, dyk)

    return pl.pallas_call(
        body, grid=(t // tm,),
        in_specs=[_row(tm, e), _row(tm, e), _const(w_grp.shape), _const((1, e)), _row(tm, e, 1)],
        out_specs=[_row(tm, e), _row(tm, e), _const((1, e)), _const(w_grp.shape)],
        out_shape=[_sds((t, e)), _sds((t, e), BF16), _sds((1, e)), _sds(w_grp.shape)],
        compiler_params=_params("arbitrary"), name=name)(dy2, pm, w_grp, scale, z)


def _rms_f(x, g, n):
    r = lax.rsqrt(jnp.sum(x * x, axis=-1, keepdims=True) * (1.0 / n) + EPS)
    return x * r * g


def _rms_b(x, g, dy, n):
    r = lax.rsqrt(jnp.sum(x * x, axis=-1, keepdims=True) * (1.0 / n) + EPS)
    xh = x * r
    dxh = dy * g
    return r * (dxh - xh * (jnp.sum(dxh * xh, axis=-1, keepdims=True) * (1.0 / n))), dy * xh


def _swap16(x):
    lane = lax.broadcasted_iota(jnp.int32, x.shape, 1)
    return jnp.where(lane % 32 < 16, pltpu.roll(x, LANES - 16, 1), pltpu.roll(x, 16, 1))


def _rope(x, c, s):
    return x * c + _swap16(x) * s


def _rope_t(dy, c, s):
    return dy * c + _swap16(dy * s)


def _rope_tables(l, lc, nb):
    t = jnp.arange(l, dtype=jnp.int32)
    row_id, col_id = (t // GRID_W).astype(F32), (t % GRID_W).astype(F32)
    axis_dim = MLA_ROPE // 2
    freqs = ROPE_THETA ** (-jnp.arange(0, axis_dim, 2, dtype=F32) / axis_dim)
    ar, ac = row_id[:, None] * freqs, col_id[:, None] * freqs
    pad1, pad0 = jnp.ones((l, LANES - MLA_ROPE), F32), jnp.zeros((l, LANES - MLA_ROPE), F32)
    ctab = jnp.concatenate([jnp.cos(ar), jnp.cos(ar), jnp.cos(ac), jnp.cos(ac), pad1], axis=1)
    stab = jnp.concatenate([-jnp.sin(ar), jnp.sin(ar), -jnp.sin(ac), jnp.sin(ac), pad0], axis=1)
    ctab = jnp.concatenate([jnp.tile(ctab, (nb, 1)), jnp.ones((nb * lc, LANES), F32)], axis=0)
    stab = jnp.concatenate([jnp.tile(stab, (nb, 1)), jnp.zeros((nb * lc, LANES), F32)], axis=0)
    return ctab, stab


def _kv_pre_fwd(zkv, kv_norm, rope_g, ctab, stab, tm, name):
    t = zkv.shape[0]

    def body(z_ref, gk_ref, gr_ref, c_ref, s_ref, ck_ref, kr_ref):
        ck_ref[...] = _rms_f(z_ref[:, :MLA_KV_RANK], gk_ref[...], MLA_KV_RANK).astype(ck_ref.dtype)
        kr = _rms_f(z_ref[:, MLA_KV_RANK:], gr_ref[...], MLA_ROPE)
        kr_ref[...] = _rope(kr, c_ref[...], s_ref[...]).astype(kr_ref.dtype)

    w = MLA_KV_RANK + LANES
    return pl.pallas_call(
        body, grid=(t // tm,),
        in_specs=[_row(tm, w), _const((1, MLA_KV_RANK)), _const((1, LANES)), _row(tm, LANES), _row(tm, LANES)],
        out_specs=[_row(tm, MLA_KV_RANK), _row(tm, LANES)],
        out_shape=[_sds((t, MLA_KV_RANK), BF16), _sds((t, LANES), BF16)],
        compiler_params=_params("parallel"), name=name)(zkv, kv_norm, rope_g, ctab, stab)


def _kv_pre_bwd(dck, dkr, zkv, kv_norm, rope_g, ctab, stab, tm, name):
    t = zkv.shape[0]
    w = MLA_KV_RANK + LANES

    def body(dck_ref, dkr_ref, z_ref, gk_ref, gr_ref, c_ref, s_ref, dz_ref, dgk_ref, dgr_ref):
        @pl.when(pl.program_id(0) == 0)
        def _():
            dgk_ref[...] = jnp.zeros_like(dgk_ref)
            dgr_ref[...] = jnp.zeros_like(dgr_ref)

        dx, dg = _rms_b(z_ref[:, :MLA_KV_RANK], gk_ref[...], dck_ref[...], MLA_KV_RANK)
        dz_ref[:, :MLA_KV_RANK] = dx.astype(dz_ref.dtype)
        dgk_ref[...] += _rowsum(dg)
        dy = _rope_t(dkr_ref[...], c_ref[...], s_ref[...])
        dx, dg = _rms_b(z_ref[:, MLA_KV_RANK:], gr_ref[...], dy, MLA_ROPE)
        dz_ref[:, MLA_KV_RANK:] = dx.astype(dz_ref.dtype)
        dgr_ref[...] += _rowsum(dg)

    return pl.pallas_call(
        body, grid=(t // tm,),
        in_specs=[_row(tm, MLA_KV_RANK), _row(tm, LANES), _row(tm, w), _const((1, MLA_KV_RANK)), _const((1, LANES)),
                  _row(tm, LANES), _row(tm, LANES)],
        out_specs=[_row(tm, w), _const((1, MLA_KV_RANK)), _const((1, LANES))],
        out_shape=[_sds((t, w), BF16), _sds((1, MLA_KV_RANK)), _sds((1, LANES))],
        compiler_params=_params("arbitrary"), name=name)(dck, dkr, zkv, kv_norm, rope_g, ctab, stab)


def _q_pre_fwd(zq, q_norm, tm, name):
    t, w = zq.shape

    def body(z_ref, g_ref, o_ref):
        o_ref[...] = _rms_f(z_ref[...], g_ref[...], w).astype(o_ref.dtype)

    return pl.pallas_call(
        body, grid=(t // tm,), in_specs=[_row(tm, w), _const((1, w))], out_specs=_row(tm, w),
        out_shape=_sds((t, w), BF16), compiler_params=_params("parallel"), name=name)(zq, q_norm)


def _q_pre_bwd(dcq, zq, q_norm, tm, name):
    t, w = zq.shape

    def body(d_ref, z_ref, g_ref, dz_ref, dg_ref):
        @pl.when(pl.program_id(0) == 0)
        def _():
            dg_ref[...] = jnp.zeros_like(dg_ref)

        dx, dg = _rms_b(z_ref[...], g_ref[...], d_ref[...], w)
        dz_ref[...] = dx.astype(dz_ref.dtype)
        dg_ref[...] += _rowsum(dg)

    return pl.pallas_call(
        body, grid=(t // tm,), in_specs=[_row(tm, w), _row(tm, w), _const((1, w))],
        out_specs=[_row(tm, w), _const((1, w))], out_shape=[_sds((t, w), BF16), _sds((1, w))],
        compiler_params=_params("arbitrary"), name=name)(dcq, zq, q_norm)


def _k_post_fwd(kv, krr, nope_g, tm, name):
    t, w = kv.shape

    def body(kv_ref, kr_ref, gn_ref, k_ref, v_ref):
        for h in range(MLA_HEADS):
            a = h * HEAD_W
            k_ref[:, a:a + LANES] = _rms_f(kv_ref[:, a:a + LANES], gn_ref[...], MLA_NOPE).astype(k_ref.dtype)
            k_ref[:, a + LANES:a + HEAD_W] = kr_ref[...]
            v_ref[:, h * LANES:(h + 1) * LANES] = kv_ref[:, a + LANES:a + HEAD_W].astype(v_ref.dtype)

    return pl.pallas_call(
        body, grid=(t // tm,), in_specs=[_row(tm, w), _row(tm, LANES), _const((1, LANES))],
        out_specs=[_row(tm, w), _row(tm, w // 2)], out_shape=[_sds((t, w), BF16), _sds((t, w // 2), BF16)],
        compiler_params=_params("parallel"), name=name)(kv, krr, nope_g)


def _k_post_bwd(dkl, dkc, dvl, dvc, kv, nope_g, tm, name):
    t, w = kv.shape
    nl = dkl.shape[0] // tm

    def body(dkl_ref, dkc_ref, dvl_ref, dvc_ref, kv_ref, gn_ref, dkv_ref, dkr_ref, dgn_ref):
        i = pl.program_id(0)

        @pl.when(i == 0)
        def _():
            dgn_ref[...] = jnp.zeros_like(dgn_ref)

        dkr = jnp.zeros(dkr_ref.shape, F32)
        for h in range(MLA_HEADS):
            a = h * HEAD_W
            dk = jnp.where(i < nl, dkl_ref[:, a:a + HEAD_W], dkc_ref[:, a:a + HEAD_W])
            dv = jnp.where(i < nl, dvl_ref[:, h * LANES:(h + 1) * LANES], dvc_ref[:, h * LANES:(h + 1) * LANES])
            dx, dg = _rms_b(kv_ref[:, a:a + LANES], gn_ref[...], dk[:, :LANES], MLA_NOPE)
            dkv_ref[:, a:a + LANES] = dx.astype(dkv_ref.dtype)
            dgn_ref[...] += _rowsum(dg)
            dkv_ref[:, a + LANES:a + HEAD_W] = dv.astype(dkv_ref.dtype)
            dkr = dkr + dk[:, LANES:]
        dkr_ref[...] = dkr

    lat = lambda cols: pl.BlockSpec((tm, cols), lambda i: (jnp.minimum(i, nl - 1), 0))
    ctx = lambda cols: pl.BlockSpec((tm, cols), lambda i: (jnp.maximum(i - nl, 0), 0))
    return pl.pallas_call(
        body, grid=(t // tm,),
        in_specs=[lat(w), ctx(w), lat(w // 2), ctx(w // 2), _row(tm, w), _const((1, LANES))],
        out_specs=[_row(tm, w), _row(tm, LANES), _const((1, LANES))],
        out_shape=[_sds((t, w), BF16), _sds((t, LANES)), _sds((1, LANES))],
        compiler_params=_params("arbitrary"), name=name)(dkl, dkc, dvl, dvc, kv, nope_g)


def _attn_specs(nb, l, lc, tq):
    nq = l // tq
    ctx0 = nb * l // lc
    q_spec = lambda w: pl.BlockSpec((tq, w), lambda b, h, i: (b * nq + i, h))
    lat = lambda w: pl.BlockSpec((l, w), lambda b, h, i: (b, h))
    ctx = lambda w: pl.BlockSpec((lc, w), lambda b, h, i: (ctx0 + b, h))
    return nq, q_spec, lat, ctx


def _attn_fwd(q, nope_g, rope_g, ctab, stab, kf, vf, nb, l, lc, name, side=None):
    tq = _pick(l, (1024, 512, 256, 128))
    nq, q_spec, lat, ctx = _attn_specs(nb, l, lc, tq)
    sub = min(tq // ATTN_CHAINS, 256)
    tab = pl.BlockSpec((tq, LANES), lambda b, h, i: (b * nq + i, 0))

    def body(q_ref, gn_ref, gr_ref, c_ref, s_ref, kl_ref, kc_ref, vl_ref, vc_ref, o_ref, lse_ref, qf_ref):
        for r in range(0, tq, sub):
            rows = slice(r, r + sub)
            qn = _rms_f(q_ref[rows, :LANES], gn_ref[...], MLA_NOPE)
            qr = _rope(_rms_f(q_ref[rows, LANES:], gr_ref[...], MLA_ROPE), c_ref[rows, :], s_ref[rows, :])
            q = (jnp.concatenate([qn, qr], axis=1) * MLA_SCALE).astype(BF16)
            qf_ref[rows, :] = q
            s1, s2 = _dot_nt(q, kl_ref[...]), _dot_nt(q, kc_ref[...])
            m = jnp.maximum(jnp.max(s1, axis=-1, keepdims=True), jnp.max(s2, axis=-1, keepdims=True))
            p1, p2 = jnp.exp(s1 - m), jnp.exp(s2 - m)
            den = jnp.sum(p1, axis=-1, keepdims=True) + jnp.sum(p2, axis=-1, keepdims=True)
            o_ref[rows, :] = (_dot(p1, vl_ref[...]) + _dot(p2, vc_ref[...])) / den
            lse_ref[rows, :] = jnp.broadcast_to(m + jnp.log(den), (sub, LANES))

    return _pallas(
        body, (q, nope_g, rope_g, ctab, stab, kf, kf, vf, vf), grid=(nb, MLA_HEADS, nq),
        in_specs=[q_spec(HEAD_W), _const((1, LANES)), _const((1, LANES)), tab, tab, lat(HEAD_W), ctx(HEAD_W),
                  lat(LANES), ctx(LANES)],
        out_specs=[q_spec(LANES), q_spec(LANES), q_spec(HEAD_W)],
        out_shape=[_sds((nb * l, MLA_HEADS * LANES)), _sds((nb * l, MLA_HEADS * LANES)),
                   _sds((nb * l, MLA_HEADS * HEAD_W), BF16)],
        sem=("parallel", "parallel", "arbitrary"), name=name, side=side)


def _attn_bwd(do, o, lse, qf, q, nope_g, rope_g, ctab, stab, kf, vf, nb, l, lc, name, side=None):
    tq = _pick(l, (1024, 512, 256, 128))
    nq, q_spec, lat, ctx = _attn_specs(nb, l, lc, tq)
    out_lat = lambda w: pl.BlockSpec((l, w), lambda b, h, i: (b, h))
    out_ctx = lambda w: pl.BlockSpec((lc, w), lambda b, h, i: (b, h))
    sub = tq // ATTN_CHAINS
    tab = pl.BlockSpec((tq, LANES), lambda b, h, i: (b * nq + i, 0))

    def body(do_ref, o_ref, lse_ref, qf_ref, q_ref, gn_ref, gr_ref, c_ref, s_ref, kl_ref, kc_ref, vl_ref, vc_ref,
             dq_ref, dkl_ref, dkc_ref, dvl_ref, dvc_ref, dgn_ref, dgr_ref):
        @pl.when(pl.program_id(2) == 0)
        def _():
            dkl_ref[...] = jnp.zeros_like(dkl_ref)
            dkc_ref[...] = jnp.zeros_like(dkc_ref)
            dvl_ref[...] = jnp.zeros_like(dvl_ref)
            dvc_ref[...] = jnp.zeros_like(dvc_ref)

        @pl.when((pl.program_id(0) == 0) & (pl.program_id(1) == 0) & (pl.program_id(2) == 0))
        def _():
            dgn_ref[...] = jnp.zeros_like(dgn_ref)
            dgr_ref[...] = jnp.zeros_like(dgr_ref)

        parts = []
        for r in range(0, tq, sub):
            rows = slice(r, r + sub)
            qs, dof = qf_ref[rows, :], do_ref[rows, :]
            delta = jnp.sum(dof * o_ref[rows, :], axis=-1, keepdims=True)
            lse = lse_ref[rows, :1]
            dqs, part = jnp.zeros((sub, HEAD_W), F32), []
            for k_ref, v_ref in ((kl_ref, vl_ref), (kc_ref, vc_ref)):
                p = jnp.exp(_dot_nt(qs, k_ref[...]) - lse)
                ds = p * (_dot_nt(dof, v_ref[...]) - delta)
                dqs = dqs + _dot(ds, k_ref[...])
                part += [_dot_tn(ds, qs), _dot_tn(p, dof)]
            dqs = dqs * MLA_SCALE
            dxn, dgn = _rms_b(q_ref[rows, :LANES], gn_ref[...], dqs[:, :LANES], MLA_NOPE)
            dxr, dgr = _rms_b(q_ref[rows, LANES:], gr_ref[...], _rope_t(dqs[:, LANES:], c_ref[rows, :], s_ref[rows, :]),
                              MLA_ROPE)
            dq_ref[rows, :] = jnp.concatenate([dxn, dxr], axis=1).astype(dq_ref.dtype)
            parts.append(part + [_rowsum(dgn), _rowsum(dgr)])
        for n, ref in enumerate((dkl_ref, dvl_ref, dkc_ref, dvc_ref, dgn_ref, dgr_ref)):
            ref[...] += functools.reduce(lambda u, v: u + v, [part[n] for part in parts])

    kw, vw = MLA_HEADS * HEAD_W, MLA_HEADS * LANES
    return _pallas(
        body, (do, o, lse, qf, q, nope_g, rope_g, ctab, stab, kf, kf, vf, vf), grid=(nb, MLA_HEADS, nq),
        in_specs=[q_spec(LANES), q_spec(LANES), q_spec(LANES), q_spec(HEAD_W), q_spec(HEAD_W), _const((1, LANES)),
                  _const((1, LANES)), tab, tab, lat(HEAD_W), ctx(HEAD_W), lat(LANES), ctx(LANES)],
        out_specs=[q_spec(HEAD_W), out_lat(HEAD_W), out_ctx(HEAD_W), out_lat(LANES), out_ctx(LANES),
                   _const((1, LANES)), _const((1, LANES))],
        out_shape=[_sds((nb * l, kw), BF16), _sds((nb * l, kw)), _sds((nb * lc, kw)), _sds((nb * l, vw)),
                   _sds((nb * lc, vw)), _sds((1, LANES)), _sds((1, LANES))],
        sem=("arbitrary", "arbitrary", "arbitrary"), name=name, side=side)


def _gate_fwd(o, g, tm, name):
    t, e = o.shape

    def body(o_ref, g_ref, y_ref):
        y_ref[...] = (o_ref[...] * _silu(g_ref[...])).astype(y_ref.dtype)

    return pl.pallas_call(
        body, grid=(t // tm,), in_specs=[_row(tm, e), _row(tm, e)], out_specs=_row(tm, e),
        out_shape=_sds((t, e), BF16), compiler_params=_params("parallel"), name=name)(o, g)


def _gate_bwd(dy, o, g, tm, name):
    t, e = o.shape

    def body(dy_ref, o_ref, g_ref, do_ref, dg_ref):
        dy, gz = dy_ref[...], g_ref[...]
        do_ref[...] = dy * _silu(gz)
        dg_ref[...] = (dy * o_ref[...] * _dsilu(gz)).astype(dg_ref.dtype)

    return pl.pallas_call(
        body, grid=(t // tm,), in_specs=[_row(tm, e), _row(tm, e), _row(tm, e)],
        out_specs=[_row(tm, e), _row(tm, e)], out_shape=[_sds((t, e)), _sds((t, e), BF16)],
        compiler_params=_params("parallel"), name=name)(dy, o, g)


def _chunk_fwd(z, ln_g, ln_b, w_s, bs_full, name):
    t, e = z.shape[0], z.shape[1] // 3
    tm = _pick(t, CHUNK_TILES)

    def body(u_ref, v_ref, g_ref, lg_ref, lb_ref, w_ref, bs_ref, y_ref):
        for r0 in range(0, tm, CHUNK):
            rows = slice(r0, r0 + CHUNK)
            xh, _ = _layernorm_parts(v_ref[rows, :])
            vn = xh * lg_ref[...] + lb_ref[...]
            for k in range(CHUNK_GROUPS):
                cols = slice(k * LANES, (k + 1) * LANES)
                s = _dot(w_ref[k], vn[:, cols]) + bs_ref[:, cols]
                y_ref[rows, cols] = (u_ref[rows, cols] * s * _silu(g_ref[rows, cols])).astype(y_ref.dtype)

    return pl.pallas_call(
        body, grid=(t // tm,),
        in_specs=[_row(tm, e, 0), _row(tm, e, 1), _row(tm, e, 2), _const((1, e)), _const((1, e)),
                  _const(w_s.shape), _const((CHUNK, e))],
        out_specs=_row(tm, e), out_shape=_sds((t, e), BF16), compiler_params=_params("parallel"),
        name=name)(z, z, z, ln_g, ln_b, w_s, bs_full)


def _chunk_bwd(dy, z, ln_g, ln_b, w_s, bs_full, name):
    t, e = z.shape[0], z.shape[1] // 3
    tm = _pick(t, CHUNK_TILES)

    def body(dy_ref, u_ref, v_ref, g_ref, lg_ref, lb_ref, w_ref, bs_ref, dz_ref, dw_ref, dbs_ref, dlg_ref, dlb_ref,
             acc_ref):
        i = pl.program_id(0)

        @pl.when(i == 0)
        def _():
            dw_ref[...] = jnp.zeros_like(dw_ref)
            dlg_ref[...] = jnp.zeros_like(dlg_ref)
            dlb_ref[...] = jnp.zeros_like(dlb_ref)
            acc_ref[...] = jnp.zeros_like(acc_ref)

        per_chunk = []
        for r0 in range(0, tm, CHUNK):
            rows = slice(r0, r0 + CHUNK)
            xh, r = _layernorm_parts(v_ref[rows, :])
            vn = xh * lg_ref[...] + lb_ref[...]
            dvn, dss, dws = [], [], []
            for k in range(CHUNK_GROUPS):
                cols = slice(k * LANES, (k + 1) * LANES)
                dyk, u, gz = dy_ref[rows, cols], u_ref[rows, cols], g_ref[rows, cols]
                s = _dot(w_ref[k], vn[:, cols]) + bs_ref[:, cols]
                sg = _silu(gz)
                dz_ref[rows, cols] = (dyk * s * sg).astype(dz_ref.dtype)
                dz_ref[rows, 2 * e + k * LANES:2 * e + (k + 1) * LANES] = (dyk * u * s * _dsilu(gz)).astype(
                    dz_ref.dtype)
                ds = dyk * u * sg
                dss.append(ds)
                dws.append(_dot_nt(ds, vn[:, cols]))
                dvn.append(_dot_tn(w_ref[k], ds))
            dvn = jnp.concatenate(dvn, axis=1)
            dz_ref[rows, e:2 * e] = _layernorm_bwd(dvn, xh, r, lg_ref[...]).astype(dz_ref.dtype)
            per_chunk.append((dss, dws, _rowsum(dvn * xh), _rowsum(dvn)))
        total = lambda parts: functools.reduce(lambda a, b: a + b, parts)
        for k in range(CHUNK_GROUPS):
            acc_ref[:, k * LANES:(k + 1) * LANES] += total([c[0][k] for c in per_chunk])
            dw_ref[k] += total([c[1][k] for c in per_chunk])
        dlg_ref[...] += total([c[2] for c in per_chunk])
        dlb_ref[...] += total([c[3] for c in per_chunk])

        @pl.when(i == pl.num_programs(0) - 1)
        def _():
            lane = lax.broadcasted_iota(jnp.int32, dbs_ref.shape, 1)
            out = jnp.zeros(dbs_ref.shape, F32)
            for k in range(CHUNK_GROUPS):
                col = jnp.sum(acc_ref[:, k * LANES:(k + 1) * LANES], axis=1, keepdims=True)
                out = jnp.where(lane == k, col, out)
            dbs_ref[...] = out

    return pl.pallas_call(
        body, grid=(t // tm,),
        in_specs=[_row(tm, e), _row(tm, e, 0), _row(tm, e, 1), _row(tm, e, 2), _const((1, e)),
                  _const((1, e)), _const(w_s.shape), _const((CHUNK, e))],
        out_specs=[_row(tm, 3 * e), _const(w_s.shape), _const((CHUNK, CHUNK_GROUPS)), _const((1, e)),
                   _const((1, e))],
        out_shape=[_sds((t, 3 * e), BF16), _sds(w_s.shape), _sds((CHUNK, CHUNK_GROUPS)), _sds((1, e)), _sds((1, e))],
        scratch_shapes=[pltpu.VMEM((CHUNK, e), F32)],
        compiler_params=_params("arbitrary"), name=name)(dy, z, z, z, ln_g, ln_b, w_s, bs_full)


def _mod_rows(mods, layer, d, nseg):
    m = mods[layer, :nseg]
    return [m[:, None, k * d:(k + 1) * d] for k in range(3)]


def _local_step(x, ctx, tgt, w, mods, comm=None):
    nb, l, d = x.shape
    lc = ctx.shape[1]
    e = d
    tl, ta = nb * l, nb * (l + lc)
    tm = _pick(lc, (256, 128))
    segs_a, segs_l = _Segs((l,) * nb + (lc,) * nb, tm), _Segs((l,) * nb, tm)
    norm_g = w['norm_g']
    g = {}

    def carried(tag, fn, *args, **kw):
        if comm is None:
            return fn(*args, **kw)
        res, brought = fn(*args, side=comm.side(tag), **kw)
        comm.done(tag, brought)
        return res

    xa0 = jnp.concatenate([x.reshape(tl, d), ctx.reshape(nb * lc, d)], axis=0)

    sh0, sc0, gt0 = _mod_rows(mods, 0, d, 2 * nb)
    h0 = _rms_mod_fwd(xa0, norm_g[0:1], sc0, sh0, segs_a, "l0_norm")
    z0 = carried('fwd1', _mm, h0, w['cv_w_in'], "l0_in")
    y2_0 = carried('fwd2', _conv1_fwd, z0, w['cv_dw'], w['cv_db'], nb, l, 0, "l0_conv_lat")
    y2_0 = _conv1_fwd(z0, w['cv_dw'], w['cv_db'], nb, lc, tl, "l0_conv_ctx", into=y2_0)
    y4_0 = _conv2_fwd(y2_0, z0, w['cv_ln_g'], w['cv_ln_b'], tm, "l0_gate")
    o0, xa1 = _mm(y4_0, w['cv_w_out'], "l0_out", resid=(xa0, gt0, segs_a))

    sh1, sc1, gt1 = _mod_rows(mods, 1, d, 2 * nb)
    h1 = _rms_mod_fwd(xa1, norm_g[1:2], sc1, sh1, segs_a, "l1_norm")
    z1 = carried('fwd3', _mm, h1, w['pl_w_in'], "l1_in")
    taps_l, ic_l = _pool_tables(l, e)
    taps_c, ic_c = _pool_tables(lc, e)
    pm1 = _pool1(z1, taps_l, ic_l, nb, l, 0, False, "l1_pool_lat", BF16)
    pm1 = _pool1(z1, taps_c, ic_c, nb, lc, tl, False, "l1_pool_ctx", BF16, into=pm1)
    y2_1 = _pool2_fwd(pm1, w['pl_w_grp'], w['pl_scale'], z1, tm, "l1_group")
    o1, xa2 = _mm(y2_1, w['pl_w_out'], "l1_out", resid=(xa1, gt1, segs_a))

    sh2, sc2, gt2 = _mod_rows(mods, 2, d, 2 * nb)
    h2 = _rms_mod_fwd(xa2, norm_g[2:3], sc2, sh2, segs_a, "l2_norm")
    w_in = w['ml_w_in']
    kvc = MLA_KV_RANK + MLA_ROPE
    w_in_p = jnp.concatenate([w_in[:, :kvc], jnp.zeros((d, LANES - MLA_ROPE), w_in.dtype), w_in[:, kvc:]], axis=1)
    w_uq_p = jnp.pad(w['ml_w_uq'].reshape(MLA_Q_RANK, MLA_HEADS, MLA_NOPE + MLA_ROPE),
                     ((0, 0), (0, 0), (0, HEAD_W - MLA_NOPE - MLA_ROPE))).reshape(MLA_Q_RANK, MLA_HEADS * HEAD_W)
    rope_g = jnp.pad(w['ml_rope_norm'], ((0, 0), (0, LANES - MLA_ROPE)))
    nope_g = w['ml_nope_norm']
    ctab, stab = _rope_tables(l, lc, nb)
    kvw = MLA_KV_RANK + LANES
    w_kv, w_q, w_g = w_in_p[:, :kvw], w_in_p[:, kvw:kvw + MLA_Q_RANK], w_in_p[:, kvw + MLA_Q_RANK:]
    zkv = _mm(h2, w_kv, "l2_in_kv")
    zq, zg = _mm(h2, w_q, "l2_in_q", rows=tl), _mm(h2, w_g, "l2_in_g", rows=tl)
    ckvn, krr = _kv_pre_fwd(zkv, w['ml_kv_norm'], rope_g[1:2], ctab, stab, tm, "l2_kv_pre")
    cqn = _q_pre_fwd(zq, w['ml_q_norm'], tm, "l2_q_pre")
    q2 = _mm(cqn, w_uq_p, "l2_uq")
    kv2 = _mm(ckvn, w['ml_w_ukv'], "l2_ukv")
    kf, vf = _k_post_fwd(kv2, krr, nope_g[1:2], tm, "l2_k_post")
    o_att, lse, qf = carried('fwd4', _attn_fwd, q2, nope_g[0:1], rope_g[0:1], ctab, stab, kf, vf, nb, l, lc,
                             "l2_attn")
    og = _gate_fwd(o_att, zg, tm, "l2_gate")
    o2, x3 = _mm(og, w['ml_w_out'], "l2_out", resid=(xa2, gt2[:nb], segs_l))

    sh3, sc3, gt3 = _mod_rows(mods, 3, d, nb)
    h3 = _rms_mod_fwd(x3, norm_g[3:4], sc3, sh3, segs_l, "l3_norm")
    z3 = _mm(h3, w['ch_w_in'], "l3_in")
    bs_full = jnp.repeat(w['ch_b_s'], e // CHUNK_GROUPS, axis=1)
    y3 = _chunk_fwd(z3, w['ch_ln_g'], w['ch_ln_b'], w['ch_w_s'], bs_full, "l3_chunk")
    o3, x4 = _mm(y3, w['ch_w_out'], "l3_out", resid=(x3, gt3, segs_l))

    loss_vec, dx4 = _loss_head(x4, tgt.reshape(tl, d), tm, "loss")

    do3, dgt3 = _resid_bwd(dx4, o3, gt3, segs_l, "l3_resid_b")
    dy3 = _mm_nt(do3, w['ch_w_out'], "l3_out_bx")
    g['ch_w_out'] = _mm_tn(y3, do3, "l3_out_bw")
    dz3, g['ch_w_s'], g['ch_b_s'], g['ch_ln_g'], g['ch_ln_b'] = _chunk_bwd(
        dy3, z3, w['ch_ln_g'], w['ch_ln_b'], w['ch_w_s'], bs_full, "l3_chunk_b")
    dh3 = _mm_nt(dz3, w['ch_w_in'], "l3_in_bx")
    g['ch_w_in'] = _mm_tn(h3, dz3, "l3_in_bw", shards=N_DEV)
    dx3, dng3, dsc3, dsh3 = _rms_mod_bwd(x3, norm_g[3:4], sc3, sh3, dh3, dx4, segs_l, "l3_norm_b")
    if comm is not None:
        comm.grads_ready(3, g)

    do2, dgt2 = _resid_bwd(dx3, o2, gt2[:nb], segs_l, "l2_resid_b")
    dog = carried('swap3', _mm_nt, do2, w['ml_w_out'], "l2_out_bx")
    g['ml_w_out'] = _mm_tn(og, do2, "l2_out_bw")
    d_att, dzg = _gate_bwd(dog, o_att, zg, tm, "l2_gate_b")
    dq2, dkl, dkc, dvl, dvc, dnope_q, drope_q = carried(
        'quad3', _attn_bwd, d_att, o_att, lse, qf, q2, nope_g[0:1], rope_g[0:1], ctab, stab, kf, vf, nb, l, lc,
        "l2_attn_b")
    dkv2, dkrr, dnope_k = _k_post_bwd(dkl, dkc, dvl, dvc, kv2, nope_g[1:2], tm, "l2_k_post_b")
    dcqn = _mm_nt(dq2, w_uq_p, "l2_uq_bx")
    g_uq_p = _mm_tn(cqn, dq2, "l2_uq_bw")
    dckvn = _mm_nt(dkv2, w['ml_w_ukv'], "l2_ukv_bx")
    g['ml_w_ukv'] = _mm_tn(ckvn, dkv2, "l2_ukv_bw", shards=N_DEV)
    dzq, g['ml_q_norm'] = _q_pre_bwd(dcqn, zq, w['ml_q_norm'], tm, "l2_q_pre_b")
    dzkv, g['ml_kv_norm'], drope_k = _kv_pre_bwd(dckvn, dkrr, zkv, w['ml_kv_norm'], rope_g[1:2], ctab, stab, tm,
                                                  "l2_kv_pre_b")
    dh2 = _mm_nt_sum([(dzkv, w_kv), (dzq, w_q), (dzg, w_g)], "l2_in_bx")
    g['ml_w_in'] = jnp.concatenate([_mm_tn(h2, dzkv, "l2_in_kv_bw")[:, :kvc], _mm_tn(h2, dzq, "l2_in_q_bw", rows=tl),
                                    _mm_tn(h2, dzg, "l2_in_g_bw", rows=tl)], axis=1)
    g['ml_w_uq'] = g_uq_p.reshape(MLA_Q_RANK, MLA_HEADS, HEAD_W)[:, :, :MLA_NOPE + MLA_ROPE].reshape(
        MLA_Q_RANK, MLA_HEADS * (MLA_NOPE + MLA_ROPE))
    g['ml_nope_norm'] = jnp.concatenate([dnope_q, dnope_k], axis=0)
    g['ml_rope_norm'] = jnp.concatenate([drope_q, drope_k], axis=0)[:, :MLA_ROPE]
    dxa2, dng2, dsc2, dsh2 = _rms_mod_bwd(xa2, norm_g[2:3], sc2, sh2, dh2, dx3, segs_a, "l2_norm_b")
    if comm is not None:
        comm.grads_ready(2, g)

    do1, dgt1 = _resid_bwd(dxa2, o1, gt1, segs_a, "l1_resid_b")
    dy2_1 = carried('swap2', _mm_nt, do1, w['pl_w_out'], "l1_out_bx")
    g['pl_w_out'] = _mm_tn(y2_1, do1, "l1_out_bw")
    dpm, dgz1, g['pl_scale'], g['pl_w_grp'] = _pool2_bwd(dy2_1, pm1, w['pl_w_grp'], w['pl_scale'], z1, tm,
                                                          "l1_group_b")
    dv1 = _pool1(dpm, taps_l, ic_l, nb, l, 0, True, "l1_pool_lat_b", BF16)
    dv1 = _pool1(dpm, taps_c, ic_c, nb, lc, tl, True, "l1_pool_ctx_b", BF16, into=dv1)
    dz1 = jnp.concatenate([dv1, dgz1], axis=1)
    dh1 = _mm_nt(dz1, w['pl_w_in'], "l1_in_bx")
    g['pl_w_in'] = carried('quad2', _mm_tn, h1, dz1, "l1_in_bw", shards=N_DEV)
    dxa1, dng1, dsc1, dsh1 = _rms_mod_bwd(xa1, norm_g[1:2], sc1, sh1, dh1, dxa2, segs_a, "l1_norm_b")

    do0, dgt0 = _resid_bwd(dxa1, o0, gt0, segs_a, "l0_resid_b")
    g['cv_w_out'] = _mm_tn(y4_0, do0, "l0_out_bw")
    if comm is not None:
        comm.grads_ready(1, g)
    dy4 = carried('swap1', _mm_nt, do0, w['cv_w_out'], "l0_out_bx")
    dy2, dgz0, g['cv_ln_g'], g['cv_ln_b'] = _conv2_bwd(dy4, y2_0, z0, w['cv_ln_g'], w['cv_ln_b'], tm, "l0_gate_b")
    da_l, db_l, ddw, ddb = carried('quad1', _conv1_bwd, dy2, z0, w['cv_dw'], jnp.zeros((CONV_WIDTH, e), F32),
                                   jnp.zeros((1, e), F32), nb, l, 0, "l0_conv_lat_b")
    da, db_, g['cv_dw'], g['cv_db'] = _conv1_bwd(dy2, z0, w['cv_dw'], ddw, ddb, nb, lc, tl, "l0_conv_ctx_b",
                                                 into=(da_l, db_l))
    dz0 = jnp.concatenate([da, db_, dgz0], axis=1)
    g['cv_w_in'] = _mm_tn(h0, dz0, "l0_in_bw", shards=N_DEV)
    if comm is not None:
        comm.grads_ready(0, g)
    dh0 = carried('quad0', _mm_nt, dz0, w['cv_w_in'], "l0_in_bx")
    dx0, dng0, dsc0, dsh0 = _rms_mod_bwd(xa0, norm_g[0:1], sc0, sh0, dh0, dxa1, segs_a, "l0_norm_b", dx_rows=tl)

    def rows4(t):
        return jnp.pad(t[:, 0], ((0, 2 * nb - t.shape[0]), (0, 0)))

    dmods = jnp.stack([
        jnp.concatenate([rows4(dsh0), rows4(dsc0), rows4(dgt0)], axis=1),
        jnp.concatenate([rows4(dsh1), rows4(dsc1), rows4(dgt1)], axis=1),
        jnp.concatenate([rows4(dsh2), rows4(dsc2), rows4(dgt2)], axis=1),
        jnp.concatenate([rows4(dsh3), rows4(dsc3), rows4(dgt3)], axis=1)])
    dnorm_g = jnp.concatenate([dng0, dng1, dng2, dng3], axis=0)
    return loss_vec, dx0.reshape(nb, l, d), g, dmods, dnorm_g


def _mesh_pos():
    return lax.axis_index("x"), lax.axis_index("y"), lax.axis_index("c")


def _remote(src, dst, send_sems, recv_sems, k, dev):
    return pltpu.make_async_remote_copy(src_ref=src, dst_ref=dst, send_sem=send_sems.at[k], recv_sem=recv_sems.at[k],
                                        device_id=dev, device_id_type=pl.DeviceIdType.MESH)


def _comm_call(body, xs, out_shapes, n_remote, n_local, name):
    hbm = pl.BlockSpec(memory_space=pltpu.HBM)
    return pl.pallas_call(
        body, in_specs=[hbm] * len(xs), out_specs=[hbm] * len(out_shapes), out_shape=out_shapes,
        scratch_shapes=_sem_shapes(n_remote, n_local),
        compiler_params=pltpu.CompilerParams(has_side_effects=True), name=name)(*xs)


def _run_side(side, name):
    n = len(side.xs)

    def body(*refs):
        side.start(refs[:n], refs[n:n + len(side.out_shapes)], *refs[n + len(side.out_shapes):])
        side.finish(refs[:n], refs[n:n + len(side.out_shapes)], *refs[n + len(side.out_shapes):])

    return _comm_call(body, side.xs, side.out_shapes, side.n_remote, side.n_local, name)


def _gather_side(xs):
    n = len(xs)

    def plan(x_refs, o_refs, send_sems, recv_sems, local_sems):
        x, y, c = _mesh_pos()
        me, sib = (x, y, c), (x, y, 1 - c)
        chips = [(1 - x, y), (x, 1 - y), (1 - x, 1 - y)]

        def slot(a, p):
            return o_refs[a].at[4 * p[0] + 2 * p[1] + p[2]]

        def copy(a, k, block, to, src=None):
            return _remote(slot(a, block) if src is None else src, slot(a, block), send_sems, recv_sems, 7 * a + k, to)

        mine = [pltpu.make_async_copy(x_refs[a], slot(a, me), local_sems.at[a]) for a in range(n)]
        first = []
        for a in range(n):
            first += [copy(a, 1 + j, me, chip + (c,), src=x_refs[a]) for j, chip in enumerate(chips)]
            first.append(copy(a, 0, me, sib, src=x_refs[a]))
        return me, sib, c, chips, copy, mine, first

    def start(x_refs, o_refs, send_sems, recv_sems, local_sems):
        _, _, _, _, _, mine, first = plan(x_refs, o_refs, send_sems, recv_sems, local_sems)
        for cp in mine + first:
            cp.start()

    def finish(x_refs, o_refs, send_sems, recv_sems, local_sems):
        me, sib, c, chips, copy, mine, first = plan(x_refs, o_refs, send_sems, recv_sems, local_sems)
        passed = []
        for j, chip in enumerate(chips):
            for a in range(n):
                copy(a, 1 + j, chip + (c,), me).wait_recv()
                passed.append(copy(a, 4 + j, chip + (c,), sib))
                passed[-1].start()
        for a in range(n):
            copy(a, 0, sib, me).wait_recv()
        for j, chip in enumerate(chips):
            for a in range(n):
                copy(a, 4 + j, chip + (1 - c,), me).wait_recv()
        for cp in first + passed:
            cp.wait_send()
        for cp in mine:
            cp.wait()

    return _Side(xs, [_sds((N_DEV,) + x.shape, x.dtype) for x in xs], 7 * n, n, start, finish)


def _gather_all(xs, name):
    return _run_side(_gather_side(xs), name)


def _swap_side(xs):
    n = len(xs)

    def plan(x_refs, o_refs, send_sems, recv_sems, _):
        x, y, c = _mesh_pos()
        return [_remote(x_refs[a].at[q, 1 - c], o_refs[a].at[q], send_sems, recv_sems, 4 * a + q, (x, y, 1 - c))
                for a in range(n) for q in range(4)]

    def start(*refs):
        for cp in plan(*refs):
            cp.start()

    def finish(*refs):
        copies = plan(*refs)
        for cp in copies:
            cp.wait_recv()
        for cp in copies:
            cp.wait_send()

    return _Side(xs, [_sds((4,) + x.shape[2:], x.dtype) for x in xs], 4 * n, 0, start, finish)


def _quad_side(xs):
    n = len(xs)

    def plan(x_refs, o_refs, send_sems, recv_sems, local_sems):
        x, y, c = _mesh_pos()
        q = 2 * x + y
        chips = [(1 - x, y), (x, 1 - y), (1 - x, 1 - y)]
        mine = [pltpu.make_async_copy(x_refs[a].at[q], o_refs[a].at[q], local_sems.at[a]) for a in range(n)]
        sends, arrivals = [], []
        for a in range(n):
            for j, chip in enumerate(chips):
                qj = 2 * chip[0] + chip[1]
                sends.append(_remote(x_refs[a].at[qj], o_refs[a].at[q], send_sems, recv_sems, 3 * a + j, chip + (c,)))
                arrivals.append(_remote(x_refs[a].at[qj], o_refs[a].at[qj], send_sems, recv_sems, 3 * a + j,
                                        chip + (c,)))
        return mine, sends, arrivals

    def start(*refs):
        mine, sends, _ = plan(*refs)
        for cp in mine + sends:
            cp.start()

    def finish(*refs):
        mine, sends, arrivals = plan(*refs)
        for cp in arrivals:
            cp.wait_recv()
        for cp in sends:
            cp.wait_send()
        for cp in mine:
            cp.wait()

    return _Side(xs, [_sds(x.shape, x.dtype) for x in xs], 3 * n, n, start, finish)


def _pair_add(xs, rs, name):
    n = len(xs)

    def body(*refs):
        c = lax.axis_index("c")
        for x_ref, r_ref, o_ref in zip(refs[:n], refs[n:2 * n], refs[2 * n:]):
            o_ref[...] = (x_ref[c].astype(F32) + r_ref[...].astype(F32)).astype(o_ref.dtype)

    slot = lambda x: pl.BlockSpec((None,) + x.shape[2:], lambda q: (q, 0, 0))
    return pl.pallas_call(
        body, grid=(4,),
        in_specs=[pl.BlockSpec((None, 2) + x.shape[2:], lambda q: (q, 0, 0, 0)) for x in xs] + [slot(x) for x in xs],
        out_specs=[slot(x) for x in xs], out_shape=[_sds((4,) + x.shape[2:], x.dtype) for x in xs],
        compiler_params=_params("parallel"), name=name)(*xs, *rs)


def _pack_rows(n):
    r = -(-n // PACK_COLS)
    return -(-r // 256) * 256 if r > 256 else -(-r // 16) * 16


def _pack(arrs, dtype):
    flat = jnp.concatenate([a.reshape(-1).astype(dtype) for a in arrs])
    rows = _pack_rows(flat.shape[0])
    return jnp.pad(flat, (0, rows * PACK_COLS - flat.shape[0])).reshape(rows, PACK_COLS)


def _pack_shards(arrs):
    flat = jnp.concatenate([a.astype(F32) for a in arrs], axis=1)
    rows = _pack_rows(flat.shape[1])
    return jnp.pad(flat, ((0, 0), (0, rows * PACK_COLS - flat.shape[1]))).reshape(N_DEV, rows, PACK_COLS)


def _unpack(packed, shapes, lead=()):
    flat = packed.reshape(tuple(lead) + (-1,))
    out, off = [], 0
    for s in shapes:
        n = 1
        for v in s:
            n *= v
        out.append(flat[..., off:off + n].reshape(tuple(lead) + tuple(s)))
        off += n
    return out


def _to_shards(full, ax):
    s = full.shape
    t = full.reshape(s[:ax] + (N_DEV, s[ax] // N_DEV) + s[ax + 1:])
    return jnp.moveaxis(t, ax, 0).reshape(N_DEV, -1)


def _from_shards(shards, local_shape, ax):
    t = jnp.moveaxis(shards.reshape((N_DEV,) + tuple(local_shape)), 0, ax)
    s = t.shape
    return t.reshape(s[:ax] + (s[ax] * s[ax + 1],) + s[ax + 2:])


def _mod_fwd(c_rows, w_mod, b_mod, name):
    nl, d, n = w_mod.shape
    r = c_rows.shape[0]

    def body(c_ref, w_ref, b_ref, o_ref):
        s = _silu(c_ref[...])
        for l in range(nl):
            o_ref[l] = _dot(s, w_ref[l]) + b_ref[l]

    return pl.pallas_call(body, out_shape=_sds((nl, r, n)),
                          compiler_params=pltpu.CompilerParams(vmem_limit_bytes=VMEM_LIMIT), name=name)(
        c_rows, w_mod, b_mod)


def _mod_bwd(c_rows, dcols, dall, w_mod, c_ctx, name):
    nl, d, n = w_mod.shape
    r = c_rows.shape[0]

    def body(c_ref, dc_ref, da_ref, w_ref, cc_ref, gw_ref, gb_ref, gc_ref):
        s = _silu(c_ref[...])
        ds = jnp.zeros((r, d), F32)
        for l in range(nl):
            gw_ref[l] = _dot_tn(s, dc_ref[l])
            gb_ref[l] = _rowsum(da_ref[l])
            ds = ds + _dot_nt(dc_ref[l], w_ref[l])
        row = lax.broadcasted_iota(jnp.int32, (r, d), 0)
        gc_ref[...] = _rowsum(jnp.where(row % 4 >= 2, ds, 0.0)) * _dsilu(cc_ref[...])

    return pl.pallas_call(body, out_shape=[_sds((nl, d, n)), _sds((nl, 1, 3 * d)), _sds((1, d))],
                          compiler_params=pltpu.CompilerParams(vmem_limit_bytes=VMEM_LIMIT), name=name)(
        c_rows, dcols, dall, w_mod, c_ctx)


def _adam_math(w, gsum, m, v):
    c1, c2 = 1.0 - ADAM_B1 ** ADAM_STEP, 1.0 - ADAM_B2 ** ADAM_STEP
    mn = ADAM_B1 * m + (1.0 - ADAM_B1) * gsum
    vn = ADAM_B2 * v + (1.0 - ADAM_B2) * (gsum * gsum)
    return -ADAM_LR * ((mn / c1) / (jnp.sqrt(vn / c2) + ADAM_EPS) + ADAM_WD * w), mn, vn


def _adam(w, gparts, row0, m, v, name):
    rows, cols = w.shape
    npart = gparts.shape[0]
    if rows % 8:
        tr = rows
        assert row0 == 0 and gparts.shape[1] == rows
    else:
        tr = max(t for t in (512, 256, 128, 64, 32, 16, 8) if rows % t == 0 and row0 % t == 0
                 and (t * cols <= 256 * 1024 or t == 8))

    def body(w_ref, g_ref, m_ref, v_ref, go_ref, d_ref, mo_ref, vo_ref):
        gsum = g_ref[0].astype(F32)
        for p in range(1, npart):
            gsum = gsum + g_ref[p].astype(F32)
        go_ref[...] = gsum
        d_ref[...], mo_ref[...], vo_ref[...] = _adam_math(w_ref[...], gsum, m_ref[...], v_ref[...])

    spec = _row(tr, cols)
    return pl.pallas_call(
        body, grid=(rows // tr,),
        in_specs=[spec, pl.BlockSpec((npart, tr, cols), lambda i: (0, row0 // tr + i, 0)), spec, spec],
        out_specs=[spec] * 4, out_shape=[_sds((rows, cols))] * 4, compiler_params=_params("parallel"),
        name=name)(w, gparts, m, v)


INPUTS = ['x', 'c', 'ctx'] + WEIGHTS + ['loss_target'] + ['m_' + n for n in WEIGHTS] + ['v_' + n for n in WEIGHTS]
AXES = ("x", "y", "c")
LAYER_MATS = (('cv_w_in', 'cv_w_out'), ('pl_w_in', 'pl_w_grp', 'pl_w_out'),
              ('ml_w_in', 'ml_w_uq', 'ml_w_ukv', 'ml_w_out'), ('ch_w_in', 'ch_w_out'))
GATHERS = {'fwd1': LAYER_MATS[1], 'fwd2': ('ml_w_in', 'ml_w_uq'), 'fwd3': ('ml_w_ukv', 'ml_w_out'), 'fwd4': LAYER_MATS[3]}
GRAD_GROUPS = (('cv_w_in',), ('pl_w_in', 'pl_w_grp', 'pl_w_out', 'cv_w_out'), LAYER_MATS[2], LAYER_MATS[3])
KINDS = ('grad_', 'delta_', 'new_m_', 'new_v_')


def _squeeze_layer(name, a):
    return a if name == 'norm_g' or a.ndim < 3 else a[0]


def _as2d(a):
    return a.reshape(-1, a.shape[-1])


class _Exchanges:
    def __init__(self, a, w):
        self.a, self.w, self.bufs, self.sums, self.quad = a, w, {}, {}, {}

    def mats(self, names):
        return [_as2d(self.a[n]).astype(BF16) for n in names]

    def take_weights(self, names, bufs):
        for n, buf in zip(names, bufs):
            self.w[n] = _squeeze_layer(n, _from_shards(buf, self.a[n].shape, SHARD_AXIS[n]))

    def side(self, tag):
        if tag in GATHERS:
            return _gather_side(self.mats(GATHERS[tag]))
        group = int(tag[-1])
        return _swap_side(self.bufs[group]) if tag.startswith('swap') else _quad_side(self.sums[group])

    def done(self, tag, brought):
        if tag in GATHERS:
            self.take_weights(GATHERS[tag], brought)
        elif tag.startswith('swap'):
            group = int(tag[-1])
            self.sums[group] = _pair_add(self.bufs[group], brought, "grads_add_cores_%d" % group)
        else:
            self.quad[int(tag[-1])] = brought

    def shard_major(self, n, gn):
        if gn.ndim == 3 and gn.shape[0] == N_DEV and gn.dtype == BF16:
            return gn
        whole = tuple(N_DEV * s if i == SHARD_AXIS[n] else s for i, s in enumerate(self.a[n].shape))
        return _to_shards(gn.reshape(whole), SHARD_AXIS[n]).reshape((N_DEV,) + _as2d(self.a[n]).shape)

    def grads_ready(self, group, g):
        bufs = [self.shard_major(n, g[n]).astype(BF16) for n in GRAD_GROUPS[group]]
        if group == 0:
            bufs.append(_pack_shards([self.shard_major(n, g[n]).reshape(N_DEV, -1) for n in VECTOR_WEIGHTS]))
        self.bufs[group] = [b.reshape((4, 2) + b.shape[1:]) for b in bufs]
        if group == 0:
            self.done('swap0', _run_side(self.side('swap0'), "grads_swap_cores_0"))


def _train_step(a):
    x, c, ctx, tgt = a['x'], a['c'], a['ctx'], a['loss_target']
    d = x.shape[-1]
    nb = x.shape[0]
    dev = 4 * lax.axis_index("x") + 2 * lax.axis_index("y") + lax.axis_index("c")
    local_shape = {n: a[n].shape for n in WEIGHTS}

    w = {n: _squeeze_layer(n, a[n]) for n in WEIGHTS if SHARD_AXIS[n] is None}
    comm = _Exchanges(a, w)
    vec_names = ['c'] + VECTOR_WEIGHTS
    vec_all, = _gather_all([_pack([a[n] for n in vec_names], F32)], "gather_vectors")
    parts = dict(zip(vec_names, _unpack(vec_all, [a[n].shape for n in vec_names], lead=(N_DEV,))))
    for n in VECTOR_WEIGHTS:
        w[n] = _squeeze_layer(n, _from_shards(parts[n], local_shape[n], SHARD_AXIS[n]))
    c_all = parts['c'].reshape(N_DEV * nb, d)
    c_ctx = a['c_ctx'].reshape(1, d)

    w_mod = a['w_mod']
    nl, ncol = w_mod.shape[0], w_mod.shape[2]
    mod_rows = -(-(N_DEV * nb + 1) // 8) * 8
    c_rows = jnp.concatenate([c_all, c_ctx, jnp.zeros((mod_rows - N_DEV * nb - 1, d), F32)], axis=0)
    b_loc = lax.dynamic_slice(a['b_mod'], (0, dev * ncol), (nl, ncol))[:, None, :]
    mod_loc = _mod_fwd(c_rows, w_mod, b_loc, "mod_fwd")
    gathered = _gather_all([mod_loc.reshape(nl * mod_rows, ncol)] + comm.mats(LAYER_MATS[0]), "gather_first")
    comm.take_weights(LAYER_MATS[0], gathered[1:])
    mod_all = gathered[0].reshape(N_DEV, nl, mod_rows, ncol).transpose(1, 2, 0, 3).reshape(nl, mod_rows, N_DEV * ncol)
    ctx_row = mod_all[:, N_DEV * nb:N_DEV * nb + 1]
    mods = jnp.concatenate([lax.dynamic_slice(mod_all, (0, dev * nb, 0), (nl, nb, 3 * d))] + [ctx_row] * nb, axis=1)

    loss_vec, grad_x, g, dmods, dnorm_g = _local_step(x, ctx, tgt, w, mods, comm)
    loss = lax.psum(jnp.sum(loss_vec), AXES)

    nseg = dmods.shape[1]
    dm_all, = _gather_all([dmods.reshape(nl * nseg, 3 * d)], "gather_dmods")
    dm_all = dm_all.reshape(N_DEV, nl, nseg, 3 * d).transpose(1, 0, 2, 3).reshape(nl, N_DEV * nseg, 3 * d)
    dcols = lax.dynamic_slice(dm_all, (0, 0, dev * ncol), (nl, N_DEV * nseg, ncol))
    c_rows_b = jnp.concatenate([c_all.reshape(N_DEV, nb, d), jnp.broadcast_to(c_ctx, (N_DEV, nb, d))], axis=1)
    g_w_mod, g_b_mod, g_c_ctx = _mod_bwd(c_rows_b.reshape(N_DEV * nseg, d), dcols, dm_all, w_mod, c_ctx, "mod_bwd")
    g['c_ctx'], g['norm_g'] = g_c_ctx, dnorm_g
    rep_all, = _gather_all([_pack([g[n] for n in REPLICATED], F32)], "gather_replicated_grads")

    out = {}

    def keep(names, res, shapes=None):
        for kind, val in zip(KINDS, res):
            if shapes is None:
                out[kind + names[0]] = val.reshape(local_shape[names[0]])
            else:
                for n, leaf in zip(names, _unpack(val, shapes)):
                    out[kind + n] = leaf

    def update_packed(names, gparts, tag):
        res = _adam(_pack([a[n] for n in names], F32), gparts, 0, _pack([a['m_' + n] for n in names], F32),
                    _pack([a['v_' + n] for n in names], F32), "adam_" + tag)
        keep(names, res, [local_shape[n] for n in names])

    for group, names in enumerate(GRAD_GROUPS):
        for n, gparts in zip(names, comm.quad[group]):
            keep([n], _adam(_as2d(a[n]), gparts, 0, _as2d(a['m_' + n]), _as2d(a['v_' + n]), "adam_" + n))
    update_packed(VECTOR_WEIGHTS, comm.quad[0][-1], "vectors")
    update_packed(REPLICATED, rep_all, "replicated")
    keep(['w_mod'], _adam(_as2d(w_mod), _as2d(g_w_mod)[None], 0, _as2d(a['m_w_mod']), _as2d(a['v_w_mod']),
                          "adam_w_mod"))
    keep(['b_mod'], _adam(a['b_mod'], g_b_mod.reshape((1,) + a['b_mod'].shape), 0, a['m_b_mod'], a['v_b_mod'],
                          "adam_b_mod"))
    return (loss, grad_x) + tuple(out[kind + n] for kind in KINDS for n in WEIGHTS)


def kernel(x, c, ctx, c_ctx, norm_g, w_mod, b_mod, cv_w_in, cv_dw, cv_db, cv_ln_g, cv_ln_b, cv_w_out, pl_w_in, pl_w_grp, pl_scale, pl_w_out, ml_w_in, ml_q_norm, ml_kv_norm, ml_w_uq, ml_w_ukv, ml_nope_norm, ml_rope_norm, ml_w_out, ch_w_in, ch_ln_g, ch_ln_b, ch_w_s, ch_b_s, ch_w_out, loss_target, m_c_ctx, m_norm_g, m_w_mod, m_b_mod, m_cv_w_in, m_cv_dw, m_cv_db, m_cv_ln_g, m_cv_ln_b, m_cv_w_out, m_pl_w_in, m_pl_w_grp, m_pl_scale, m_pl_w_out, m_ml_w_in, m_ml_q_norm, m_ml_kv_norm, m_ml_w_uq, m_ml_w_ukv, m_ml_nope_norm, m_ml_rope_norm, m_ml_w_out, m_ch_w_in, m_ch_ln_g, m_ch_ln_b, m_ch_w_s, m_ch_b_s, m_ch_w_out, v_c_ctx, v_norm_g, v_w_mod, v_b_mod, v_cv_w_in, v_cv_dw, v_cv_db, v_cv_ln_g, v_cv_ln_b, v_cv_w_out, v_pl_w_in, v_pl_w_grp, v_pl_scale, v_pl_w_out, v_ml_w_in, v_ml_q_norm, v_ml_kv_norm, v_ml_w_uq, v_ml_w_ukv, v_ml_nope_norm, v_ml_rope_norm, v_ml_w_out, v_ch_w_in, v_ch_ln_g, v_ch_ln_b, v_ch_w_s, v_ch_b_s, v_ch_w_out):
    return _train_step(dict(zip(INPUTS, (x, c, ctx, c_ctx, norm_g, w_mod, b_mod, cv_w_in, cv_dw, cv_db, cv_ln_g, cv_ln_b, cv_w_out, pl_w_in, pl_w_grp, pl_scale, pl_w_out, ml_w_in, ml_q_norm, ml_kv_norm, ml_w_uq, ml_w_ukv, ml_nope_norm, ml_rope_norm, ml_w_out, ch_w_in, ch_ln_g, ch_ln_b, ch_w_s, ch_b_s, ch_w_out, loss_target, m_c_ctx, m_norm_g, m_w_mod, m_b_mod, m_cv_w_in, m_cv_dw, m_cv_db, m_cv_ln_g, m_cv_ln_b, m_cv_w_out, m_pl_w_in, m_pl_w_grp, m_pl_scale, m_pl_w_out, m_ml_w_in, m_ml_q_norm, m_ml_kv_norm, m_ml_w_uq, m_ml_w_ukv, m_ml_nope_norm, m_ml_rope_norm, m_ml_w_out, m_ch_w_in, m_ch_ln_g, m_ch_ln_b, m_ch_w_s, m_ch_b_s, m_ch_w_out, v_c_ctx, v_norm_g, v_w_mod, v_b_mod, v_cv_w_in, v_cv_dw, v_cv_db, v_cv_ln_g, v_cv_ln_b, v_cv_w_out, v_pl_w_in, v_pl_w_grp, v_pl_scale, v_pl_w_out, v_ml_w_in, v_ml_q_norm, v_ml_kv_norm, v_ml_w_uq, v_ml_w_ukv, v_ml_nope_norm, v_ml_rope_norm, v_ml_w_out, v_ch_w_in, v_ch_ln_g, v_ch_ln_b, v_ch_w_s, v_ch_b_s, v_ch_w_out))))
```

```python
import functools

import jax
import jax.numpy as jnp
from jax import lax
from jax.experimental import pallas as pl
from jax.experimental.pallas import tpu as pltpu

F32 = jnp.float32
BF16 = jnp.bfloat16

N_DEV = 8
EPS = 1e-6
CONV_WIDTH = 31
CONV_PAD = 16
POOL_WINDOWS = (2, 4, 8, 16)
POOL_TAPS = 16
MLA_HEADS = 8
MLA_NOPE = 128
MLA_ROPE = 64
MLA_Q_RANK = 384
MLA_KV_RANK = 256
MLA_SCALE = (MLA_NOPE + MLA_ROPE) ** -0.5
ROPE_THETA = 10000.0
GRID_W = 64
HEAD_W = 256
ATTN_CHAINS = 2
CHUNK = 128
CHUNK_GROUPS = 8
CHUNK_TILES = (512, 256, 128)
ADAM_LR = 0.001
ADAM_B1 = 0.9
ADAM_B2 = 0.999
ADAM_EPS = 1e-08
ADAM_WD = 0.01
ADAM_STEP = 10
LANES = 128
VMEM_LIMIT = 56 * 1024 * 1024
PACK_COLS = 1024

WEIGHTS = ['c_ctx', 'norm_g', 'w_mod', 'b_mod', 'cv_w_in', 'cv_dw', 'cv_db', 'cv_ln_g', 'cv_ln_b', 'cv_w_out',
           'pl_w_in', 'pl_w_grp', 'pl_scale', 'pl_w_out', 'ml_w_in', 'ml_q_norm', 'ml_kv_norm', 'ml_w_uq',
           'ml_w_ukv', 'ml_nope_norm', 'ml_rope_norm', 'ml_w_out', 'ch_w_in', 'ch_ln_g', 'ch_ln_b', 'ch_w_s',
           'ch_b_s', 'ch_w_out']
SHARD_AXIS = {'c_ctx': None, 'norm_g': None, 'w_mod': 2, 'b_mod': None, 'cv_w_in': 2, 'cv_dw': 2, 'cv_db': None,
              'cv_ln_g': None, 'cv_ln_b': None, 'cv_w_out': 1, 'pl_w_in': 2, 'pl_w_grp': 2, 'pl_scale': 1,
              'pl_w_out': 1, 'ml_w_in': 2, 'ml_q_norm': 1, 'ml_kv_norm': 1, 'ml_w_uq': 2, 'ml_w_ukv': 2,
              'ml_nope_norm': None, 'ml_rope_norm': None, 'ml_w_out': 1, 'ch_w_in': 2, 'ch_ln_g': 1, 'ch_ln_b': 1,
              'ch_w_s': None, 'ch_b_s': None, 'ch_w_out': 1}
VECTOR_WEIGHTS = ['cv_dw', 'pl_scale', 'ml_q_norm', 'ml_kv_norm', 'ch_ln_g', 'ch_ln_b']
REPLICATED = ['c_ctx', 'norm_g', 'cv_db', 'cv_ln_g', 'cv_ln_b', 'ml_nope_norm', 'ml_rope_norm', 'ch_w_s', 'ch_b_s']


def _pick(n, cands):
    for c in cands:
        if n % c == 0:
            return c
    raise ValueError(f"no tile for {n} among {cands}")


def _params(*sem):
    return pltpu.CompilerParams(dimension_semantics=sem, vmem_limit_bytes=VMEM_LIMIT)


def _sig(x):
    return 1.0 / (1.0 + jnp.exp(-x))


def _silu(x):
    return x * _sig(x)


def _dsilu(x):
    s = _sig(x)
    return s * (1.0 + x * (1.0 - s))


def _rowsum(v):
    return jnp.sum(v, axis=0, keepdims=True)


def _dot(a, b):
    return jnp.dot(a.astype(BF16), b.astype(BF16), preferred_element_type=F32)


def _dot_nt(a, b):
    return lax.dot_general(a.astype(BF16), b.astype(BF16), (((1,), (1,)), ((), ())), preferred_element_type=F32)


def _dot_tn(a, b):
    return lax.dot_general(a.astype(BF16), b.astype(BF16), (((0,), (0,)), ((), ())), preferred_element_type=F32)


class _Segs:
    def __init__(self, lens, tm):
        self.lens, self.tm, self.n = tuple(lens), tm, len(lens)
        self.starts, s = [], 0
        for l in lens:
            assert l % tm == 0
            self.starts.append(s // tm)
            s += l
        self.rows, self.tiles = s, s // tm

    def seg(self, i):
        r = 0
        for st in self.starts[1:]:
            r = r + jnp.where(i >= st, 1, 0)
        return r

    def is_first(self, i):
        f = i == 0
        for st in self.starts[1:]:
            f = jnp.logical_or(f, i == st)
        return f

    def spec(self, cols):
        return pl.BlockSpec((None, 1, cols), lambda i: (self.seg(i), 0, 0))


def _row(tm, cols, cb=0):
    return pl.BlockSpec((tm, cols), lambda i: (i, cb))


def _const(shape):
    return pl.BlockSpec(shape, lambda *_: (0,) * len(shape))


def _sds(shape, dtype=F32):
    return jax.ShapeDtypeStruct(shape, dtype)


class _Side:
    def __init__(self, xs, out_shapes, n_remote, n_local, start, finish):
        self.xs, self.out_shapes, self.n_remote, self.n_local = list(xs), list(out_shapes), n_remote, n_local
        self.start, self.finish = start, finish


def _sem_shapes(n_remote, n_local):
    return [pltpu.SemaphoreType.DMA((n_remote,)), pltpu.SemaphoreType.DMA((n_remote,)),
            pltpu.SemaphoreType.DMA((max(n_local, 1),))]


def _pallas(body, args, *, grid, in_specs, out_specs, out_shape, sem, name, scratch_shapes=(), side=None, into=None):
    aliases = {}
    if into:
        inner, n_args = body, len(args)

        def body(*refs):
            inner(*refs[:n_args], *refs[n_args + len(into):])

        aliases = {n_args + k: o for k, o in enumerate(sorted(into))}
        args = tuple(args) + tuple(into[o] for o in sorted(into))
        in_specs = list(in_specs) + [pl.BlockSpec(memory_space=pl.ANY)] * len(into)
    if side is None:
        return pl.pallas_call(body, grid=grid, in_specs=in_specs, out_specs=out_specs, out_shape=out_shape,
                              scratch_shapes=list(scratch_shapes), input_output_aliases=aliases,
                              compiler_params=_params(*sem), name=name)(*args)
    multi = isinstance(out_shape, (list, tuple))
    out_specs, out_shape = (list(out_specs), list(out_shape)) if multi else ([out_specs], [out_shape])
    ni, no, ns, si, so = len(in_specs), len(out_specs), len(scratch_shapes), len(side.xs), len(side.out_shapes)
    hbm = pl.BlockSpec(memory_space=pltpu.HBM)

    def wrapped(*refs):
        ins, sins, refs = refs[:ni], refs[ni:ni + si], refs[ni + si:]
        outs, souts, refs = refs[:no], refs[no:no + so], refs[no + so:]
        scr, sems = refs[:ns], refs[ns:]
        ids = [pl.program_id(k) for k in range(len(grid))]
        first = functools.reduce(jnp.logical_and, [i == 0 for i in ids])
        last = functools.reduce(jnp.logical_and, [i == n - 1 for i, n in zip(ids, grid)])

        @pl.when(first)
        def _():
            side.start(sins, souts, *sems)

        body(*ins, *outs, *scr)

        @pl.when(last)
        def _():
            side.finish(sins, souts, *sems)

    res = pl.pallas_call(
        wrapped, grid=grid, in_specs=list(in_specs) + [hbm] * si, out_specs=out_specs + [hbm] * so,
        out_shape=out_shape + side.out_shapes,
        scratch_shapes=list(scratch_shapes) + _sem_shapes(side.n_remote, side.n_local), input_output_aliases=aliases,
        compiler_params=pltpu.CompilerParams(dimension_semantics=("arbitrary",) * len(grid),
                                             vmem_limit_bytes=VMEM_LIMIT, has_side_effects=True),
        name=name)(*args, *side.xs)
    return (list(res[:no]) if multi else res[0]), list(res[no:])


N_TILES = (1024, 896, 768, 512, 384, 256, 128)
M_TILES = (1536, 1024, 768, 512, 256, 128)


def _mm(a, b, name, out_dtype=F32, rows=None, side=None, resid=None):
    m, k, n = rows or a.shape[0], a.shape[1], b.shape[1]
    tm, tn = _pick(m, M_TILES), _pick(n, N_TILES)
    if resid is not None:
        x, gt, segs = resid
        pieces = tm // segs.tm

        def body(a_ref, b_ref, x_ref, *rest):
            gt_refs, (o_ref, y_ref) = rest[:pieces], rest[pieces:]
            o = _dot(a_ref[...], b_ref[...])
            o_ref[...] = o
            for c in range(pieces):
                rows = slice(c * segs.tm, (c + 1) * segs.tm)
                y_ref[rows, :] = x_ref[rows, :] + gt_refs[c][...] * o[rows, :]

        tile = pl.BlockSpec((tm, tn), lambda j, i: (i, j))
        gt_specs = [pl.BlockSpec((None, 1, tn), lambda j, i, c=c: (segs.seg(i * pieces + c), 0, j))
                    for c in range(pieces)]
        return pl.pallas_call(
            body, grid=(n // tn, m // tm),
            in_specs=[pl.BlockSpec((tm, k), lambda j, i: (i, 0)), pl.BlockSpec((k, tn), lambda j, i: (0, j)), tile]
            + gt_specs, out_specs=[tile, tile], out_shape=[_sds((m, n)), _sds((m, n))],
            compiler_params=_params("parallel", "parallel"), name=name)(a, b, x, *([gt] * pieces))

    def body(a_ref, b_ref, o_ref):
        o_ref[...] = _dot(a_ref[...], b_ref[...]).astype(o_ref.dtype)

    return _pallas(
        body, (a, b), grid=(n // tn, m // tm),
        in_specs=[pl.BlockSpec((tm, k), lambda j, i: (i, 0)), pl.BlockSpec((k, tn), lambda j, i: (0, j))],
        out_specs=pl.BlockSpec((tm, tn), lambda j, i: (i, j)), out_shape=_sds((m, n), out_dtype),
        sem=("parallel", "parallel"), name=name, side=side)


def _mm_nt(a, b, name, out_dtype=F32, side=None):
    m, k, n = a.shape[0], a.shape[1], b.shape[0]
    tm, tn = _pick(m, (512, 256, 128)), _pick(n, N_TILES)

    def body(a_ref, b_ref, o_ref):
        o_ref[...] = _dot_nt(a_ref[...], b_ref[...]).astype(o_ref.dtype)

    return _pallas(
        body, (a, b), grid=(n // tn, m // tm),
        in_specs=[pl.BlockSpec((tm, k), lambda j, i: (i, 0)), pl.BlockSpec((tn, k), lambda j, i: (j, 0))],
        out_specs=pl.BlockSpec((tm, tn), lambda j, i: (i, j)), out_shape=_sds((m, n), out_dtype),
        sem=("parallel", "parallel"), name=name, side=side)


def _mm_nt_sum(pairs, name):
    m, n = pairs[0][0].shape[0], pairs[0][1].shape[0]
    tm, tn = _pick(m, (512, 256, 128)), _pick(n, N_TILES)
    tiles = [a.shape[0] // tm for a, _ in pairs]
    assert all(a.shape[0] % tm == 0 for a, _ in pairs)

    def body(*refs):
        o_ref, i = refs[-1], pl.program_id(1)
        acc = _dot_nt(refs[0][...], refs[1][...])
        for p in range(1, len(pairs)):
            acc = acc + jnp.where(i < tiles[p], _dot_nt(refs[2 * p][...], refs[2 * p + 1][...]), 0.0)
        o_ref[...] = acc

    in_specs = []
    for (a, b), nt in zip(pairs, tiles):
        in_specs += [pl.BlockSpec((tm, a.shape[1]), lambda j, i, nt=nt: (jnp.minimum(i, nt - 1), 0)),
                     pl.BlockSpec((tn, a.shape[1]), lambda j, i: (j, 0))]
    return pl.pallas_call(
        body, grid=(n // tn, m // tm), in_specs=in_specs, out_specs=pl.BlockSpec((tm, tn), lambda j, i: (i, j)),
        out_shape=_sds((m, n)), compiler_params=_params("parallel", "parallel"),
        name=name)(*[t for pair in pairs for t in pair])


def _mm_tn(a, b, name, rows=None, shards=None, side=None):
    t, k, n = rows or a.shape[0], a.shape[1], b.shape[1]
    tk, tt = _pick(k, N_TILES), _pick(t, M_TILES)
    if shards:
        width = n // shards
        per_tile = max(c for c in (8, 4, 2, 1) if shards % c == 0 and c * width <= N_TILES[0])
        tn = per_tile * width
    else:
        tn = _pick(n, N_TILES)
    assert tn % LANES == 0

    def body(a_ref, b_ref, o_ref, acc_ref):
        @pl.when(pl.program_id(2) == 0)
        def _():
            acc_ref[...] = jnp.zeros_like(acc_ref)

        acc_ref[...] += _dot_tn(a_ref[...], b_ref[...])

        @pl.when(pl.program_id(2) == pl.num_programs(2) - 1)
        def _():
            if shards:
                for c in range(per_tile):
                    o_ref[c] = acc_ref[:, c * width:(c + 1) * width].astype(o_ref.dtype)
            else:
                o_ref[...] = acc_ref[...]

    if shards:
        out_spec = pl.BlockSpec((per_tile, tk, width), lambda i, j, s: (j, i, 0))
        out_shape = _sds((shards, k, width), BF16)
    else:
        out_spec, out_shape = pl.BlockSpec((tk, tn), lambda i, j, s: (i, j)), _sds((k, n))
    return _pallas(
        body, (a, b), grid=(k // tk, n // tn, t // tt),
        in_specs=[pl.BlockSpec((tt, tk), lambda i, j, s: (s, i)), pl.BlockSpec((tt, tn), lambda i, j, s: (s, j))],
        out_specs=out_spec, out_shape=out_shape, scratch_shapes=[pltpu.VMEM((tk, tn), F32)],
        sem=("parallel", "parallel", "arbitrary"), name=name, side=side)


def _rms_mod_fwd(x, g, sc, sh, segs, name):
    d, tm = x.shape[1], segs.tm

    def body(x_ref, g_ref, sc_ref, sh_ref, h_ref):
        xf = x_ref[...]
        r = lax.rsqrt(jnp.mean(xf * xf, axis=-1, keepdims=True) + EPS)
        h_ref[...] = ((xf * r * g_ref[...]) * (1.0 + sc_ref[...]) + sh_ref[...]).astype(h_ref.dtype)

    return pl.pallas_call(
        body, grid=(segs.tiles,), in_specs=[_row(tm, d), _const((1, d)), segs.spec(d), segs.spec(d)],
        out_specs=_row(tm, d), out_shape=_sds((segs.rows, d), BF16), compiler_params=_params("parallel"),
        name=name)(x, g, sc, sh)


def _rms_mod_bwd(x, g, sc, sh, dh, dxr, segs, name, dx_rows=None):
    d, tm = x.shape[1], segs.tm
    dxr_tiles = dxr.shape[0] // tm
    dx_tiles = (dx_rows or segs.rows) // tm

    def body(x_ref, g_ref, sc_ref, sh_ref, dh_ref, dxr_ref, dx_ref, dg_ref, dsc_ref, dsh_ref):
        i = pl.program_id(0)

        @pl.when(i == 0)
        def _():
            dg_ref[...] = jnp.zeros_like(dg_ref)

        @pl.when(segs.is_first(i))
        def _():
            dsc_ref[...] = jnp.zeros_like(dsc_ref)
            dsh_ref[...] = jnp.zeros_like(dsh_ref)

        xf, gg, dhf = x_ref[...], g_ref[...], dh_ref[...].astype(F32)
        dxr = jnp.where(i < dxr_tiles, dxr_ref[...], 0.0)
        r = lax.rsqrt(jnp.mean(xf * xf, axis=-1, keepdims=True) + EPS)
        xh = xf * r
        dsh_ref[...] += _rowsum(dhf)
        dsc_ref[...] += _rowsum(dhf * (xh * gg))
        du = dhf * (1.0 + sc_ref[...])
        dg_ref[...] += _rowsum(du * xh)
        dxh = du * gg
        @pl.when(i < dx_tiles)
        def _():
            dx_ref[...] = dxr + r * (dxh - xh * jnp.mean(dxh * xh, axis=-1, keepdims=True))

    return pl.pallas_call(
        body, grid=(segs.tiles,),
        in_specs=[_row(tm, d), _const((1, d)), segs.spec(d), segs.spec(d), _row(tm, d),
                  pl.BlockSpec((tm, d), lambda i: (jnp.minimum(i, dxr_tiles - 1), 0))],
        out_specs=[pl.BlockSpec((tm, d), lambda i: (jnp.minimum(i, dx_tiles - 1), 0)), _const((1, d)), segs.spec(d),
                   segs.spec(d)],
        out_shape=[_sds((dx_tiles * tm, d)), _sds((1, d)), _sds((segs.n, 1, d)), _sds((segs.n, 1, d))],
        compiler_params=_params("arbitrary"), name=name)(x, g, sc, sh, dh, dxr)


def _resid_bwd(dxn, o, gt, segs, name):
    d, tm = o.shape[1], segs.tm

    def body(dxn_ref, o_ref, gt_ref, do_ref, dgt_ref):
        @pl.when(segs.is_first(pl.program_id(0)))
        def _():
            dgt_ref[...] = jnp.zeros_like(dgt_ref)

        dx = dxn_ref[...]
        do_ref[...] = (gt_ref[...] * dx).astype(do_ref.dtype)
        dgt_ref[...] += _rowsum(dx * o_ref[...])

    return pl.pallas_call(
        body, grid=(segs.tiles,), in_specs=[_row(tm, d), _row(tm, d), segs.spec(d)],
        out_specs=[_row(tm, d), segs.spec(d)], out_shape=[_sds((segs.rows, d), BF16), _sds((segs.n, 1, d))],
        compiler_params=_params("arbitrary"), name=name)(dxn, o, gt)


def _loss_head(y, tgt, tm, name):
    t, d = y.shape

    def body(y_ref, t_ref, l_ref, dy_ref):
        @pl.when(pl.program_id(0) == 0)
        def _():
            l_ref[...] = jnp.zeros_like(l_ref)

        e = y_ref[...] - t_ref[...]
        dy_ref[...] = e * (1.0 / d)
        l_ref[...] += _rowsum(e * e) * (0.5 / d)

    return pl.pallas_call(
        body, grid=(t // tm,), in_specs=[_row(tm, d), _row(tm, d)], out_specs=[_const((1, d)), _row(tm, d)],
        out_shape=[_sds((1, d)), _sds((t, d))], compiler_params=_params("arbitrary"), name=name)(y, tgt)


def _seq_spec(l, ce, row0, cb0=0):
    return pl.BlockSpec((l, ce), lambda j, s: (row0 // l + s, cb0 + j))


def _tap_sum(pad_ref, taps_ref, first_row, n_taps, l, ce, flip):
    out = []
    for r0 in range(0, l, CHUNK):
        rows = min(CHUNK, l - r0)
        acc = jnp.zeros((rows, ce), F32)
        for k in range(n_taps):
            kk = n_taps - 1 - k if flip else k
            acc = acc + pad_ref[pl.ds(first_row + r0 + k, rows), :] * taps_ref[kk:kk + 1, :]
        out.append(acc)
    return out


def _fill_pad(pad_ref, val, l, ce):
    pad_ref[pl.ds(0, CONV_PAD), :] = jnp.zeros((CONV_PAD, ce), F32)
    pad_ref[pl.ds(CONV_PAD + l, CONV_PAD), :] = jnp.zeros((CONV_PAD, ce), F32)
    pad_ref[pl.ds(CONV_PAD, l), :] = val


def _conv1_fwd(z, dw, db, nseq, l, row0, name, side=None, into=None):
    e = z.shape[1] // 3
    ce = LANES
    half = CONV_WIDTH // 2

    def body(a_ref, b_ref, dw_ref, db_ref, y_ref, pad_ref):
        _fill_pad(pad_ref, a_ref[...] * _sig(b_ref[...]), l, ce)
        pieces = _tap_sum(pad_ref, dw_ref, CONV_PAD - half, CONV_WIDTH, l, ce, False)
        for n, acc in enumerate(pieces):
            y_ref[pl.ds(n * CHUNK, acc.shape[0]), :] = acc + db_ref[...]

    return _pallas(
        body, (z, z, dw, db), grid=(e // ce, nseq),
        in_specs=[_seq_spec(l, ce, row0), _seq_spec(l, ce, row0, e // ce),
                  pl.BlockSpec((CONV_WIDTH, ce), lambda j, s: (0, j)), pl.BlockSpec((1, ce), lambda j, s: (0, j))],
        out_specs=_seq_spec(l, ce, row0), out_shape=_sds((z.shape[0], e)),
        scratch_shapes=[pltpu.VMEM((l + 2 * CONV_PAD, ce), F32)],
        sem=("parallel", "arbitrary"), name=name, side=side, into=None if into is None else {0: into})


def _conv1_bwd(dy2, z, dw, acc_dw, acc_db, nseq, l, row0, name, side=None, into=None):
    e = z.shape[1] // 3
    ce = LANES
    half = CONV_WIDTH // 2

    def body(dy_ref, a_ref, b_ref, dw_ref, adw_ref, adb_ref, da_ref, dbb_ref, ddw_ref, ddb_ref, ypad_ref, dpad_ref):
        @pl.when(pl.program_id(1) == 0)
        def _():
            ddw_ref[...] = adw_ref[...]
            ddb_ref[...] = adb_ref[...]

        a, sb = a_ref[...], _sig(b_ref[...])
        dy = dy_ref[...]
        _fill_pad(ypad_ref, a * sb, l, ce)
        _fill_pad(dpad_ref, dy, l, ce)
        ddb_ref[...] += _rowsum(dy)
        for k in range(CONV_WIDTH):
            ddw_ref[k:k + 1, :] += _rowsum(dy * ypad_ref[pl.ds(CONV_PAD - half + k, l), :])
        pieces = _tap_sum(dpad_ref, dw_ref, CONV_PAD - half, CONV_WIDTH, l, ce, True)
        for n, dy1 in enumerate(pieces):
            rows = pl.ds(n * CHUNK, dy1.shape[0])
            sbn = sb[n * CHUNK:n * CHUNK + dy1.shape[0], :]
            da_ref[rows, :] = (dy1 * sbn).astype(da_ref.dtype)
            dbb_ref[rows, :] = (dy1 * a[n * CHUNK:n * CHUNK + dy1.shape[0], :] * sbn * (1.0 - sbn)).astype(dbb_ref.dtype)

    cw = lambda j, s: (0, j)
    return _pallas(
        body, (dy2, z, z, dw, acc_dw, acc_db), grid=(e // ce, nseq),
        in_specs=[_seq_spec(l, ce, row0), _seq_spec(l, ce, row0), _seq_spec(l, ce, row0, e // ce),
                  pl.BlockSpec((CONV_WIDTH, ce), cw), pl.BlockSpec((CONV_WIDTH, ce), cw), pl.BlockSpec((1, ce), cw)],
        out_specs=[_seq_spec(l, ce, row0), _seq_spec(l, ce, row0),
                   pl.BlockSpec((CONV_WIDTH, ce), cw), pl.BlockSpec((1, ce), cw)],
        out_shape=[_sds((z.shape[0], e), BF16), _sds((z.shape[0], e), BF16), _sds((CONV_WIDTH, e)), _sds((1, e))],
        scratch_shapes=[pltpu.VMEM((l + 2 * CONV_PAD, ce), F32), pltpu.VMEM((l + 2 * CONV_PAD, ce), F32)],
        sem=("parallel", "arbitrary"), name=name, side=side,
        into=None if into is None else {0: into[0], 1: into[1]})


def _pool_tables(l, e):
    grp = e // len(POOL_WINDOWS)
    w = jnp.repeat(jnp.array(POOL_WINDOWS, jnp.int32), grp)[None, :]
    off = jnp.arange(POOL_TAPS, dtype=jnp.int32)[:, None] - POOL_TAPS // 2
    taps = jnp.logical_and(off >= -(w // 2), off < w - w // 2).astype(F32)
    t = jnp.arange(l, dtype=jnp.int32)[:, None]
    cnt = jnp.clip(t + (w - w // 2), 0, l) - jnp.clip(t - w // 2, 0, l)
    return taps, 1.0 / cnt.astype(F32)


def _pool1(v_src, taps, inv_cnt, nseq, l, row0, transpose, name, out_dtype, into=None):
    e = taps.shape[1]
    ce = LANES
    half = POOL_TAPS // 2

    def body(v_ref, taps_ref, ic_ref, o_ref, pad_ref):
        v = v_ref[...].astype(F32)
        if transpose:
            _fill_pad(pad_ref, v * ic_ref[...], l, ce)
            pieces = _tap_sum(pad_ref, taps_ref, CONV_PAD - half + 1, POOL_TAPS, l, ce, True)
        else:
            _fill_pad(pad_ref, v, l, ce)
            pieces = _tap_sum(pad_ref, taps_ref, CONV_PAD - half, POOL_TAPS, l, ce, False)
        for n, acc in enumerate(pieces):
            rows = pl.ds(n * CHUNK, acc.shape[0])
            vn = v[n * CHUNK:n * CHUNK + acc.shape[0], :]
            if transpose:
                o_ref[rows, :] = (acc - vn).astype(o_ref.dtype)
            else:
                o_ref[rows, :] = (acc * ic_ref[rows, :] - vn).astype(o_ref.dtype)

    return _pallas(
        body, (v_src, taps, inv_cnt), grid=(e // ce, nseq),
        in_specs=[_seq_spec(l, ce, row0), pl.BlockSpec((POOL_TAPS, ce), lambda j, s: (0, j)),
                  pl.BlockSpec((l, ce), lambda j, s: (0, j))],
        out_specs=_seq_spec(l, ce, row0), out_shape=_sds((v_src.shape[0], e), out_dtype),
        scratch_shapes=[pltpu.VMEM((l + 2 * CONV_PAD, ce), F32)],
        sem=("parallel", "arbitrary"), name=name, into=None if into is None else {0: into})


def _layernorm_parts(x, eps=EPS):
    mu = jnp.mean(x, axis=-1, keepdims=True)
    xc = x - mu
    r = lax.rsqrt(jnp.mean(xc * xc, axis=-1, keepdims=True) + eps)
    return xc * r, r


def _layernorm_bwd(dy, xh, r, g):
    dxh = dy * g
    return r * (dxh - jnp.mean(dxh, axis=-1, keepdims=True) - xh * jnp.mean(dxh * xh, axis=-1, keepdims=True))


def _conv2_fwd(y2, z, ln_g, ln_b, tm, name):
    t, e = y2.shape

    def body(y_ref, g_ref, lg_ref, lb_ref, o_ref):
        xh, _ = _layernorm_parts(y_ref[...])
        o_ref[...] = (_silu(xh * lg_ref[...] + lb_ref[...]) * _silu(g_ref[...])).astype(o_ref.dtype)

    return pl.pallas_call(
        body, grid=(t // tm,), in_specs=[_row(tm, e), _row(tm, e, 2), _const((1, e)), _const((1, e))],
        out_specs=_row(tm, e), out_shape=_sds((t, e), BF16), compiler_params=_params("parallel"),
        name=name)(y2, z, ln_g, ln_b)


def _conv2_bwd(dy4, y2, z, ln_g, ln_b, tm, name):
    t, e = y2.shape

    def body(dy_ref, y_ref, g_ref, lg_ref, lb_ref, dy2_ref, dg_ref, dlg_ref, dlb_ref):
        @pl.when(pl.program_id(0) == 0)
        def _():
            dlg_ref[...] = jnp.zeros_like(dlg_ref)
            dlb_ref[...] = jnp.zeros_like(dlb_ref)

        dy, gz = dy_ref[...], g_ref[...]
        xh, r = _layernorm_parts(y_ref[...])
        y3 = xh * lg_ref[...] + lb_ref[...]
        dg_ref[...] = (dy * _silu(y3) * _dsilu(gz)).astype(dg_ref.dtype)
        dy3 = dy * _silu(gz) * _dsilu(y3)
        dlg_ref[...] += _rowsum(dy3 * xh)
        dlb_ref[...] += _rowsum(dy3)
        dy2_ref[...] = _layernorm_bwd(dy3, xh, r, lg_ref[...])

    return pl.pallas_call(
        body, grid=(t // tm,),
        in_specs=[_row(tm, e), _row(tm, e), _row(tm, e, 2), _const((1, e)), _const((1, e))],
        out_specs=[_row(tm, e), _row(tm, e), _const((1, e)), _const((1, e))],
        out_shape=[_sds((t, e)), _sds((t, e), BF16), _sds((1, e)), _sds((1, e))],
        compiler_params=_params("arbitrary"), name=name)(dy4, y2, z, ln_g, ln_b)


def _pool2_fwd(pm, w_grp, scale, z, tm, name):
    t, e = pm.shape
    ng, gw = w_grp.shape[0], w_grp.shape[1]

    def body(pm_ref, w_ref, sc_ref, g_ref, o_ref):
        for k in range(ng):
            cols = slice(k * gw, (k + 1) * gw)
            y = _dot(pm_ref[:, cols], w_ref[k])
            o_ref[:, cols] = (y * sc_ref[:, cols] * _silu(g_ref[:, cols])).astype(o_ref.dtype)

    return pl.pallas_call(
        body, grid=(t // tm,), in_specs=[_row(tm, e), _const(w_grp.shape), _const((1, e)), _row(tm, e, 1)],
        out_specs=_row(tm, e), out_shape=_sds((t, e), BF16), compiler_params=_params("parallel"),
        name=name)(pm, w_grp, scale, z)


def _pool2_bwd(dy2, pm, w_grp, scale, z, tm, name):
    t, e = pm.shape
    ng, gw = w_grp.shape[0], w_grp.shape[1]

    def body(dy_ref, pm_ref, w_ref, sc_ref, g_ref, dpm_ref, dg_ref, dsc_ref, dw_ref):
        @pl.when(pl.program_id(0) == 0)
        def _():
            dsc_ref[...] = jnp.zeros_like(dsc_ref)
            dw_ref[...] = jnp.zeros_like(dw_ref)

        for k in range(ng):
            cols = slice(k * gw, (k + 1) * gw)
            dy, gz, sc, pmk = dy_ref[:, cols], g_ref[:, cols], sc_ref[:, cols], pm_ref[:, cols]
            y = _dot(pmk, w_ref[k])
            dg_ref[:, cols] = (dy * (y * sc) * _dsilu(gz)).astype(dg_ref.dtype)
            dys = dy * _silu(gz)
            dsc_ref[:, cols] += _rowsum(dys * y)
            dyk = dys * sc
            dpm_ref[:, cols] = _dot_nt(dyk, w_ref[k])
            dw_ref[k] += _dot_tn(pmk, dyk)

    return pl.pallas_call(
        body, grid=(t // tm,),
        in_specs=[_row(tm, e), _row(tm, e), _const(w_grp.shape), _const((1, e)), _row(tm, e, 1)],
        out_specs=[_row(tm, e), _row(tm, e), _const((1, e)), _const(w_grp.shape)],
        out_shape=[_sds((t, e)), _sds((t, e), BF16), _sds((1, e)), _sds(w_grp.shape)],
        compiler_params=_params("arbitrary"), name=name)(dy2, pm, w_grp, scale, z)


def _rms_f(x, g, n):
    r = lax.rsqrt(jnp.sum(x * x, axis=-1, keepdims=True) * (1.0 / n) + EPS)
    return x * r * g


def _rms_b(x, g, dy, n):
    r = lax.rsqrt(jnp.sum(x * x, axis=-1, keepdims=True) * (1.0 / n) + EPS)
    xh = x * r
    dxh = dy * g
    return r * (dxh - xh * (jnp.sum(dxh * xh, axis=-1, keepdims=True) * (1.0 / n))), dy * xh


def _swap16(x):
    lane = lax.broadcasted_iota(jnp.int32, x.shape, 1)
    return jnp.where(lane % 32 < 16, pltpu.roll(x, LANES - 16, 1), pltpu.roll(x, 16, 1))


def _rope(x, c, s):
    return x * c + _swap16(x) * s


def _rope_t(dy, c, s):
    return dy * c + _swap16(dy * s)


def _rope_tables(l, lc, nb):
    t = jnp.arange(l, dtype=jnp.int32)
    row_id, col_id = (t // GRID_W).astype(F32), (t % GRID_W).astype(F32)
    axis_dim = MLA_ROPE // 2
    freqs = ROPE_THETA ** (-jnp.arange(0, axis_dim, 2, dtype=F32) / axis_dim)
    ar, ac = row_id[:, None] * freqs, col_id[:, None] * freqs
    pad1, pad0 = jnp.ones((l, LANES - MLA_ROPE), F32), jnp.zeros((l, LANES - MLA_ROPE), F32)
    ctab = jnp.concatenate([jnp.cos(ar), jnp.cos(ar), jnp.cos(ac), jnp.cos(ac), pad1], axis=1)
    stab = jnp.concatenate([-jnp.sin(ar), jnp.sin(ar), -jnp.sin(ac), jnp.sin(ac), pad0], axis=1)
    ctab = jnp.concatenate([jnp.tile(ctab, (nb, 1)), jnp.ones((nb * lc, LANES), F32)], axis=0)
    stab = jnp.concatenate([jnp.tile(stab, (nb, 1)), jnp.zeros((nb * lc, LANES), F32)], axis=0)
    return ctab, stab


def _kv_pre_fwd(zkv, kv_norm, rope_g, ctab, stab, tm, name):
    t = zkv.shape[0]

    def body(z_ref, gk_ref, gr_ref, c_ref, s_ref, ck_ref, kr_ref):
        ck_ref[...] = _rms_f(z_ref[:, :MLA_KV_RANK], gk_ref[...], MLA_KV_RANK).astype(ck_ref.dtype)
        kr = _rms_f(z_ref[:, MLA_KV_RANK:], gr_ref[...], MLA_ROPE)
        kr_ref[...] = _rope(kr, c_ref[...], s_ref[...]).astype(kr_ref.dtype)

    w = MLA_KV_RANK + LANES
    return pl.pallas_call(
        body, grid=(t // tm,),
        in_specs=[_row(tm, w), _const((1, MLA_KV_RANK)), _const((1, LANES)), _row(tm, LANES), _row(tm, LANES)],
        out_specs=[_row(tm, MLA_KV_RANK), _row(tm, LANES)],
        out_shape=[_sds((t, MLA_KV_RANK), BF16), _sds((t, LANES), BF16)],
        compiler_params=_params("parallel"), name=name)(zkv, kv_norm, rope_g, ctab, stab)


def _kv_pre_bwd(dck, dkr, zkv, kv_norm, rope_g, ctab, stab, tm, name):
    t = zkv.shape[0]
    w = MLA_KV_RANK + LANES

    def body(dck_ref, dkr_ref, z_ref, gk_ref, gr_ref, c_ref, s_ref, dz_ref, dgk_ref, dgr_ref):
        @pl.when(pl.program_id(0) == 0)
        def _():
            dgk_ref[...] = jnp.zeros_like(dgk_ref)
            dgr_ref[...] = jnp.zeros_like(dgr_ref)

        dx, dg = _rms_b(z_ref[:, :MLA_KV_RANK], gk_ref[...], dck_ref[...], MLA_KV_RANK)
        dz_ref[:, :MLA_KV_RANK] = dx.astype(dz_ref.dtype)
        dgk_ref[...] += _rowsum(dg)
        dy = _rope_t(dkr_ref[...], c_ref[...], s_ref[...])
        dx, dg = _rms_b(z_ref[:, MLA_KV_RANK:], gr_ref[...], dy, MLA_ROPE)
        dz_ref[:, MLA_KV_RANK:] = dx.astype(dz_ref.dtype)
        dgr_ref[...] += _rowsum(dg)

    return pl.pallas_call(
        body, grid=(t // tm,),
        in_specs=[_row(tm, MLA_KV_RANK), _row(tm, LANES), _row(tm, w), _const((1, MLA_KV_RANK)), _const((1, LANES)),
                  _row(tm, LANES), _row(tm, LANES)],
        out_specs=[_row(tm, w), _const((1, MLA_KV_RANK)), _const((1, LANES))],
        out_shape=[_sds((t, w), BF16), _sds((1, MLA_KV_RANK)), _sds((1, LANES))],
        compiler_params=_params("arbitrary"), name=name)(dck, dkr, zkv, kv_norm, rope_g, ctab, stab)


def _q_pre_fwd(zq, q_norm, tm, name):
    t, w = zq.shape

    def body(z_ref, g_ref, o_ref):
        o_ref[...] = _rms_f(z_ref[...], g_ref[...], w).astype(o_ref.dtype)

    return pl.pallas_call(
        body, grid=(t // tm,), in_specs=[_row(tm, w), _const((1, w))], out_specs=_row(tm, w),
        out_shape=_sds((t, w), BF16), compiler_params=_params("parallel"), name=name)(zq, q_norm)


def _q_pre_bwd(dcq, zq, q_norm, tm, name):
    t, w = zq.shape

    def body(d_ref, z_ref, g_ref, dz_ref, dg_ref):
        @pl.when(pl.program_id(0) == 0)
        def _():
            dg_ref[...] = jnp.zeros_like(dg_ref)

        dx, dg = _rms_b(z_ref[...], g_ref[...], d_ref[...], w)
        dz_ref[...] = dx.astype(dz_ref.dtype)
        dg_ref[...] += _rowsum(dg)

    return pl.pallas_call(
        body, grid=(t // tm,), in_specs=[_row(tm, w), _row(tm, w), _const((1, w))],
        out_specs=[_row(tm, w), _const((1, w))], out_shape=[_sds((t, w), BF16), _sds((1, w))],
        compiler_params=_params("arbitrary"), name=name)(dcq, zq, q_norm)


def _k_post_fwd(kv, krr, nope_g, tm, name):
    t, w = kv.shape

    def body(kv_ref, kr_ref, gn_ref, k_ref, v_ref):
        for h in range(MLA_HEADS):
            a = h * HEAD_W
            k_ref[:, a:a + LANES] = _rms_f(kv_ref[:, a:a + LANES], gn_ref[...], MLA_NOPE).astype(k_ref.dtype)
            k_ref[:, a + LANES:a + HEAD_W] = kr_ref[...]
            v_ref[:, h * LANES:(h + 1) * LANES] = kv_ref[:, a + LANES:a + HEAD_W].astype(v_ref.dtype)

    return pl.pallas_call(
        body, grid=(t // tm,), in_specs=[_row(tm, w), _row(tm, LANES), _const((1, LANES))],
        out_specs=[_row(tm, w), _row(tm, w // 2)], out_shape=[_sds((t, w), BF16), _sds((t, w // 2), BF16)],
        compiler_params=_params("parallel"), name=name)(kv, krr, nope_g)


def _k_post_bwd(dkl, dkc, dvl, dvc, kv, nope_g, tm, name):
    t, w = kv.shape
    nl = dkl.shape[0] // tm

    def body(dkl_ref, dkc_ref, dvl_ref, dvc_ref, kv_ref, gn_ref, dkv_ref, dkr_ref, dgn_ref):
        i = pl.program_id(0)

        @pl.when(i == 0)
        def _():
            dgn_ref[...] = jnp.zeros_like(dgn_ref)

        dkr = jnp.zeros(dkr_ref.shape, F32)
        for h in range(MLA_HEADS):
            a = h * HEAD_W
            dk = jnp.where(i < nl, dkl_ref[:, a:a + HEAD_W], dkc_ref[:, a:a + HEAD_W])
            dv = jnp.where(i < nl, dvl_ref[:, h * LANES:(h + 1) * LANES], dvc_ref[:, h * LANES:(h + 1) * LANES])
            dx, dg = _rms_b(kv_ref[:, a:a + LANES], gn_ref[...], dk[:, :LANES], MLA_NOPE)
            dkv_ref[:, a:a + LANES] = dx.astype(dkv_ref.dtype)
            dgn_ref[...] += _rowsum(dg)
            dkv_ref[:, a + LANES:a + HEAD_W] = dv.astype(dkv_ref.dtype)
            dkr = dkr + dk[:, LANES:]
        dkr_ref[...] = dkr

    lat = lambda cols: pl.BlockSpec((tm, cols), lambda i: (jnp.minimum(i, nl - 1), 0))
    ctx = lambda cols: pl.BlockSpec((tm, cols), lambda i: (jnp.maximum(i - nl, 0), 0))
    return pl.pallas_call(
        body, grid=(t // tm,),
        in_specs=[lat(w), ctx(w), lat(w // 2), ctx(w // 2), _row(tm, w), _const((1, LANES))],
        out_specs=[_row(tm, w), _row(tm, LANES), _const((1, LANES))],
        out_shape=[_sds((t, w), BF16), _sds((t, LANES)), _sds((1, LANES))],
        compiler_params=_params("arbitrary"), name=name)(dkl, dkc, dvl, dvc, kv, nope_g)


def _attn_specs(nb, l, lc, tq):
    nq = l // tq
    ctx0 = nb * l // lc
    q_spec = lambda w: pl.BlockSpec((tq, w), lambda b, h, i: (b * nq + i, h))
    lat = lambda w: pl.BlockSpec((l, w), lambda b, h, i: (b, h))
    ctx = lambda w: pl.BlockSpec((lc, w), lambda b, h, i: (ctx0 + b, h))
    return nq, q_spec, lat, ctx


def _attn_fwd(q, nope_g, rope_g, ctab, stab, kf, vf, gate, nb, l, lc, name, side=None):
    tq = _pick(l, (1024, 512, 256, 128))
    nq, q_spec, lat, ctx = _attn_specs(nb, l, lc, tq)
    sub = min(tq // ATTN_CHAINS, 256)
    tab = pl.BlockSpec((tq, LANES), lambda b, h, i: (b * nq + i, 0))

    def body(q_ref, gn_ref, gr_ref, c_ref, s_ref, kl_ref, kc_ref, vl_ref, vc_ref, g_ref, o_ref, lse_ref, qf_ref, og_ref):
        for r in range(0, tq, sub):
            rows = slice(r, r + sub)
            qn = _rms_f(q_ref[rows, :LANES], gn_ref[...], MLA_NOPE)
            qr = _rope(_rms_f(q_ref[rows, LANES:], gr_ref[...], MLA_ROPE), c_ref[rows, :], s_ref[rows, :])
            q = (jnp.concatenate([qn, qr], axis=1) * MLA_SCALE).astype(BF16)
            qf_ref[rows, :] = q
            s1, s2 = _dot_nt(q, kl_ref[...]), _dot_nt(q, kc_ref[...])
            m = jnp.maximum(jnp.max(s1, axis=-1, keepdims=True), jnp.max(s2, axis=-1, keepdims=True))
            p1, p2 = jnp.exp(s1 - m), jnp.exp(s2 - m)
            den = jnp.sum(p1, axis=-1, keepdims=True) + jnp.sum(p2, axis=-1, keepdims=True)
            o = (_dot(p1, vl_ref[...]) + _dot(p2, vc_ref[...])) / den
            o_ref[rows, :] = o
            og_ref[rows, :] = (o * _silu(g_ref[rows, :])).astype(og_ref.dtype)
            lse_ref[rows, :] = jnp.broadcast_to(m + jnp.log(den), (sub, LANES))

    return _pallas(
        body, (q, nope_g, rope_g, ctab, stab, kf, kf, vf, vf, gate), grid=(nb, MLA_HEADS, nq),
        in_specs=[q_spec(HEAD_W), _const((1, LANES)), _const((1, LANES)), tab, tab, lat(HEAD_W), ctx(HEAD_W),
                  lat(LANES), ctx(LANES), q_spec(LANES)],
        out_specs=[q_spec(LANES), q_spec(LANES), q_spec(HEAD_W), q_spec(LANES)],
        out_shape=[_sds((nb * l, MLA_HEADS * LANES)), _sds((nb * l, MLA_HEADS * LANES)),
                   _sds((nb * l, MLA_HEADS * HEAD_W), BF16), _sds((nb * l, MLA_HEADS * LANES), BF16)],
        sem=("parallel", "parallel", "arbitrary"), name=name, side=side)


def _attn_bwd(do, o, lse, qf, q, nope_g, rope_g, ctab, stab, kf, vf, nb, l, lc, name, side=None):
    tq = _pick(l, (1024, 512, 256, 128))
    nq, q_spec, lat, ctx = _attn_specs(nb, l, lc, tq)
    out_lat = lambda w: pl.BlockSpec((l, w), lambda b, h, i: (b, h))
    out_ctx = lambda w: pl.BlockSpec((lc, w), lambda b, h, i: (b, h))
    sub = tq // ATTN_CHAINS
    tab = pl.BlockSpec((tq, LANES), lambda b, h, i: (b * nq + i, 0))

    def body(do_ref, o_ref, lse_ref, qf_ref, q_ref, gn_ref, gr_ref, c_ref, s_ref, kl_ref, kc_ref, vl_ref, vc_ref,
             dq_ref, dkl_ref, dkc_ref, dvl_ref, dvc_ref, dgn_ref, dgr_ref):
        @pl.when(pl.program_id(2) == 0)
        def _():
            dkl_ref[...] = jnp.zeros_like(dkl_ref)
            dkc_ref[...] = jnp.zeros_like(dkc_ref)
            dvl_ref[...] = jnp.zeros_like(dvl_ref)
            dvc_ref[...] = jnp.zeros_like(dvc_ref)

        @pl.when((pl.program_id(0) == 0) & (pl.program_id(1) == 0) & (pl.program_id(2) == 0))
        def _():
            dgn_ref[...] = jnp.zeros_like(dgn_ref)
            dgr_ref[...] = jnp.zeros_like(dgr_ref)

        parts = []
        for r in range(0, tq, sub):
            rows = slice(r, r + sub)
            qs, dof = qf_ref[rows, :], do_ref[rows, :]
            delta = jnp.sum(dof * o_ref[rows, :], axis=-1, keepdims=True)
            lse = lse_ref[rows, :1]
            dqs, part = jnp.zeros((sub, HEAD_W), F32), []
            for k_ref, v_ref in ((kl_ref, vl_ref), (kc_ref, vc_ref)):
                p = jnp.exp(_dot_nt(qs, k_ref[...]) - lse)
                ds = p * (_dot_nt(dof, v_ref[...]) - delta)
                dqs = dqs + _dot(ds, k_ref[...])
                part += [_dot_tn(ds, qs), _dot_tn(p, dof)]
            dqs = dqs * MLA_SCALE
            dxn, dgn = _rms_b(q_ref[rows, :LANES], gn_ref[...], dqs[:, :LANES], MLA_NOPE)
            dxr, dgr = _rms_b(q_ref[rows, LANES:], gr_ref[...], _rope_t(dqs[:, LANES:], c_ref[rows, :], s_ref[rows, :]),
                              MLA_ROPE)
            dq_ref[rows, :] = jnp.concatenate([dxn, dxr], axis=1).astype(dq_ref.dtype)
            parts.append(part + [_rowsum(dgn), _rowsum(dgr)])
        for n, ref in enumerate((dkl_ref, dvl_ref, dkc_ref, dvc_ref, dgn_ref, dgr_ref)):
            ref[...] += functools.reduce(lambda u, v: u + v, [part[n] for part in parts])

    kw, vw = MLA_HEADS * HEAD_W, MLA_HEADS * LANES
    return _pallas(
        body, (do, o, lse, qf, q, nope_g, rope_g, ctab, stab, kf, kf, vf, vf), grid=(nb, MLA_HEADS, nq),
        in_specs=[q_spec(LANES), q_spec(LANES), q_spec(LANES), q_spec(HEAD_W), q_spec(HEAD_W), _const((1, LANES)),
                  _const((1, LANES)), tab, tab, lat(HEAD_W), ctx(HEAD_W), lat(LANES), ctx(LANES)],
        out_specs=[q_spec(HEAD_W), out_lat(HEAD_W), out_ctx(HEAD_W), out_lat(LANES), out_ctx(LANES),
                   _const((1, LANES)), _const((1, LANES))],
        out_shape=[_sds((nb * l, kw), BF16), _sds((nb * l, kw)), _sds((nb * lc, kw)), _sds((nb * l, vw)),
                   _sds((nb * lc, vw)), _sds((1, LANES)), _sds((1, LANES))],
        sem=("arbitrary", "arbitrary", "arbitrary"), name=name, side=side)


def _gate_fwd(o, g, tm, name):
    t, e = o.shape

    def body(o_ref, g_ref, y_ref):
        y_ref[...] = (o_ref[...] * _silu(g_ref[...])).astype(y_ref.dtype)

    return pl.pallas_call(
        body, grid=(t // tm,), in_specs=[_row(tm, e), _row(tm, e)], out_specs=_row(tm, e),
        out_shape=_sds((t, e), BF16), compiler_params=_params("parallel"), name=name)(o, g)


def _gate_bwd(dy, o, g, tm, name):
    t, e = o.shape

    def body(dy_ref, o_ref, g_ref, do_ref, dg_ref):
        dy, gz = dy_ref[...], g_ref[...]
        do_ref[...] = dy * _silu(gz)
        dg_ref[...] = (dy * o_ref[...] * _dsilu(gz)).astype(dg_ref.dtype)

    return pl.pallas_call(
        body, grid=(t // tm,), in_specs=[_row(tm, e), _row(tm, e), _row(tm, e)],
        out_specs=[_row(tm, e), _row(tm, e)], out_shape=[_sds((t, e)), _sds((t, e), BF16)],
        compiler_params=_params("parallel"), name=name)(dy, o, g)


def _chunk_fwd(z, ln_g, ln_b, w_s, bs_full, name):
    t, e = z.shape[0], z.shape[1] // 3
    tm = _pick(t, CHUNK_TILES)

    def body(u_ref, v_ref, g_ref, lg_ref, lb_ref, w_ref, bs_ref, y_ref):
        for r0 in range(0, tm, CHUNK):
            rows = slice(r0, r0 + CHUNK)
            xh, _ = _layernorm_parts(v_ref[rows, :])
            vn = xh * lg_ref[...] + lb_ref[...]
            for k in range(CHUNK_GROUPS):
                cols = slice(k * LANES, (k + 1) * LANES)
                s = _dot(w_ref[k], vn[:, cols]) + bs_ref[:, cols]
                y_ref[rows, cols] = (u_ref[rows, cols] * s * _silu(g_ref[rows, cols])).astype(y_ref.dtype)

    return pl.pallas_call(
        body, grid=(t // tm,),
        in_specs=[_row(tm, e, 0), _row(tm, e, 1), _row(tm, e, 2), _const((1, e)), _const((1, e)),
                  _const(w_s.shape), _const((CHUNK, e))],
        out_specs=_row(tm, e), out_shape=_sds((t, e), BF16), compiler_params=_params("parallel"),
        name=name)(z, z, z, ln_g, ln_b, w_s, bs_full)


def _chunk_bwd(dy, z, ln_g, ln_b, w_s, bs_full, name):
    t, e = z.shape[0], z.shape[1] // 3
    tm = _pick(t, CHUNK_TILES)

    def body(dy_ref, u_ref, v_ref, g_ref, lg_ref, lb_ref, w_ref, bs_ref, dz_ref, dw_ref, dbs_ref, dlg_ref, dlb_ref,
             acc_ref):
        i = pl.program_id(0)

        @pl.when(i == 0)
        def _():
            dw_ref[...] = jnp.zeros_like(dw_ref)
            dlg_ref[...] = jnp.zeros_like(dlg_ref)
            dlb_ref[...] = jnp.zeros_like(dlb_ref)
            acc_ref[...] = jnp.zeros_like(acc_ref)

        per_chunk = []
        for r0 in range(0, tm, CHUNK):
            rows = slice(r0, r0 + CHUNK)
            xh, r = _layernorm_parts(v_ref[rows, :])
            vn = xh * lg_ref[...] + lb_ref[...]
            dvn, dss, dws = [], [], []
            for k in range(CHUNK_GROUPS):
                cols = slice(k * LANES, (k + 1) * LANES)
                dyk, u, gz = dy_ref[rows, cols], u_ref[rows, cols], g_ref[rows, cols]
                s = _dot(w_ref[k], vn[:, cols]) + bs_ref[:, cols]
                sg = _silu(gz)
                dz_ref[rows, cols] = (dyk * s * sg).astype(dz_ref.dtype)
                dz_ref[rows, 2 * e + k * LANES:2 * e + (k + 1) * LANES] = (dyk * u * s * _dsilu(gz)).astype(
                    dz_ref.dtype)
                ds = dyk * u * sg
                dss.append(ds)
                dws.append(_dot_nt(ds, vn[:, cols]))
                dvn.append(_dot_tn(w_ref[k], ds))
            dvn = jnp.concatenate(dvn, axis=1)
            dz_ref[rows, e:2 * e] = _layernorm_bwd(dvn, xh, r, lg_ref[...]).astype(dz_ref.dtype)
            per_chunk.append((dss, dws, _rowsum(dvn * xh), _rowsum(dvn)))
        total = lambda parts: functools.reduce(lambda a, b: a + b, parts)
        for k in range(CHUNK_GROUPS):
            acc_ref[:, k * LANES:(k + 1) * LANES] += total([c[0][k] for c in per_chunk])
            dw_ref[k] += total([c[1][k] for c in per_chunk])
        dlg_ref[...] += total([c[2] for c in per_chunk])
        dlb_ref[...] += total([c[3] for c in per_chunk])

        @pl.when(i == pl.num_programs(0) - 1)
        def _():
            lane = lax.broadcasted_iota(jnp.int32, dbs_ref.shape, 1)
            out = jnp.zeros(dbs_ref.shape, F32)
            for k in range(CHUNK_GROUPS):
                col = jnp.sum(acc_ref[:, k * LANES:(k + 1) * LANES], axis=1, keepdims=True)
                out = jnp.where(lane == k, col, out)
            dbs_ref[...] = out

    return pl.pallas_call(
        body, grid=(t // tm,),
        in_specs=[_row(tm, e), _row(tm, e, 0), _row(tm, e, 1), _row(tm, e, 2), _const((1, e)),
                  _const((1, e)), _const(w_s.shape), _const((CHUNK, e))],
        out_specs=[_row(tm, 3 * e), _const(w_s.shape), _const((CHUNK, CHUNK_GROUPS)), _const((1, e)),
                   _const((1, e))],
        out_shape=[_sds((t, 3 * e), BF16), _sds(w_s.shape), _sds((CHUNK, CHUNK_GROUPS)), _sds((1, e)), _sds((1, e))],
        scratch_shapes=[pltpu.VMEM((CHUNK, e), F32)],
        compiler_params=_params("arbitrary"), name=name)(dy, z, z, z, ln_g, ln_b, w_s, bs_full)


def _mod_rows(mods, layer, d, nseg):
    m = mods[layer, :nseg]
    return [m[:, None, k * d:(k + 1) * d] for k in range(3)]


def _local_step(x, ctx, tgt, w, mods, comm=None):
    nb, l, d = x.shape
    lc = ctx.shape[1]
    e = d
    tl, ta = nb * l, nb * (l + lc)
    tm = _pick(lc, (256, 128))
    segs_a, segs_l = _Segs((l,) * nb + (lc,) * nb, tm), _Segs((l,) * nb, tm)
    norm_g = w['norm_g']
    g = {}

    def carried(tag, fn, *args, **kw):
        if comm is None:
            return fn(*args, **kw)
        res, brought = fn(*args, side=comm.side(tag), **kw)
        comm.done(tag, brought)
        return res

    xa0 = jnp.concatenate([x.reshape(tl, d), ctx.reshape(nb * lc, d)], axis=0)

    sh0, sc0, gt0 = _mod_rows(mods, 0, d, 2 * nb)
    h0 = _rms_mod_fwd(xa0, norm_g[0:1], sc0, sh0, segs_a, "l0_norm")
    z0 = carried('fwd1', _mm, h0, w['cv_w_in'], "l0_in")
    y2_0 = carried('fwd2', _conv1_fwd, z0, w['cv_dw'], w['cv_db'], nb, l, 0, "l0_conv_lat")
    y2_0 = _conv1_fwd(z0, w['cv_dw'], w['cv_db'], nb, lc, tl, "l0_conv_ctx", into=y2_0)
    y4_0 = _conv2_fwd(y2_0, z0, w['cv_ln_g'], w['cv_ln_b'], tm, "l0_gate")
    o0, xa1 = _mm(y4_0, w['cv_w_out'], "l0_out", resid=(xa0, gt0, segs_a))

    sh1, sc1, gt1 = _mod_rows(mods, 1, d, 2 * nb)
    h1 = _rms_mod_fwd(xa1, norm_g[1:2], sc1, sh1, segs_a, "l1_norm")
    z1 = carried('fwd3', _mm, h1, w['pl_w_in'], "l1_in")
    taps_l, ic_l = _pool_tables(l, e)
    taps_c, ic_c = _pool_tables(lc, e)
    pm1 = _pool1(z1, taps_l, ic_l, nb, l, 0, False, "l1_pool_lat", BF16)
    pm1 = _pool1(z1, taps_c, ic_c, nb, lc, tl, False, "l1_pool_ctx", BF16, into=pm1)
    y2_1 = _pool2_fwd(pm1, w['pl_w_grp'], w['pl_scale'], z1, tm, "l1_group")
    o1, xa2 = _mm(y2_1, w['pl_w_out'], "l1_out", resid=(xa1, gt1, segs_a))

    sh2, sc2, gt2 = _mod_rows(mods, 2, d, 2 * nb)
    h2 = _rms_mod_fwd(xa2, norm_g[2:3], sc2, sh2, segs_a, "l2_norm")
    w_in = w['ml_w_in']
    kvc = MLA_KV_RANK + MLA_ROPE
    w_in_p = jnp.concatenate([w_in[:, :kvc], jnp.zeros((d, LANES - MLA_ROPE), w_in.dtype), w_in[:, kvc:]], axis=1)
    w_uq_p = jnp.pad(w['ml_w_uq'].reshape(MLA_Q_RANK, MLA_HEADS, MLA_NOPE + MLA_ROPE),
                     ((0, 0), (0, 0), (0, HEAD_W - MLA_NOPE - MLA_ROPE))).reshape(MLA_Q_RANK, MLA_HEADS * HEAD_W)
    rope_g = jnp.pad(w['ml_rope_norm'], ((0, 0), (0, LANES - MLA_ROPE)))
    nope_g = w['ml_nope_norm']
    ctab, stab = _rope_tables(l, lc, nb)
    kvw = MLA_KV_RANK + LANES
    w_kv, w_q, w_g = w_in_p[:, :kvw], w_in_p[:, kvw:kvw + MLA_Q_RANK], w_in_p[:, kvw + MLA_Q_RANK:]
    zkv = _mm(h2, w_kv, "l2_in_kv")
    zq, zg = _mm(h2, w_q, "l2_in_q", rows=tl), _mm(h2, w_g, "l2_in_g", rows=tl)
    ckvn, krr = _kv_pre_fwd(zkv, w['ml_kv_norm'], rope_g[1:2], ctab, stab, tm, "l2_kv_pre")
    cqn = _q_pre_fwd(zq, w['ml_q_norm'], tm, "l2_q_pre")
    q2 = _mm(cqn, w_uq_p, "l2_uq")
    kv2 = _mm(ckvn, w['ml_w_ukv'], "l2_ukv")
    kf, vf = _k_post_fwd(kv2, krr, nope_g[1:2], tm, "l2_k_post")
    o_att, lse, qf, og = carried('fwd4', _attn_fwd, q2, nope_g[0:1], rope_g[0:1], ctab, stab, kf, vf, zg, nb, l, lc,
                                 "l2_attn")
    o2, x3 = _mm(og, w['ml_w_out'], "l2_out", resid=(xa2, gt2[:nb], segs_l))

    sh3, sc3, gt3 = _mod_rows(mods, 3, d, nb)
    h3 = _rms_mod_fwd(x3, norm_g[3:4], sc3, sh3, segs_l, "l3_norm")
    z3 = _mm(h3, w['ch_w_in'], "l3_in")
    bs_full = jnp.repeat(w['ch_b_s'], e // CHUNK_GROUPS, axis=1)
    y3 = _chunk_fwd(z3, w['ch_ln_g'], w['ch_ln_b'], w['ch_w_s'], bs_full, "l3_chunk")
    o3, x4 = _mm(y3, w['ch_w_out'], "l3_out", resid=(x3, gt3, segs_l))

    loss_vec, dx4 = _loss_head(x4, tgt.reshape(tl, d), tm, "loss")

    do3, dgt3 = _resid_bwd(dx4, o3, gt3, segs_l, "l3_resid_b")
    dy3 = _mm_nt(do3, w['ch_w_out'], "l3_out_bx")
    g['ch_w_out'] = _mm_tn(y3, do3, "l3_out_bw")
    dz3, g['ch_w_s'], g['ch_b_s'], g['ch_ln_g'], g['ch_ln_b'] = _chunk_bwd(
        dy3, z3, w['ch_ln_g'], w['ch_ln_b'], w['ch_w_s'], bs_full, "l3_chunk_b")
    dh3 = _mm_nt(dz3, w['ch_w_in'], "l3_in_bx")
    g['ch_w_in'] = _mm_tn(h3, dz3, "l3_in_bw", shards=N_DEV)
    dx3, dng3, dsc3, dsh3 = _rms_mod_bwd(x3, norm_g[3:4], sc3, sh3, dh3, dx4, segs_l, "l3_norm_b")
    if comm is not None:
        comm.grads_ready(3, g)

    do2, dgt2 = _resid_bwd(dx3, o2, gt2[:nb], segs_l, "l2_resid_b")
    dog = carried('swap3', _mm_nt, do2, w['ml_w_out'], "l2_out_bx")
    g['ml_w_out'] = _mm_tn(og, do2, "l2_out_bw")
    d_att, dzg = _gate_bwd(dog, o_att, zg, tm, "l2_gate_b")
    dq2, dkl, dkc, dvl, dvc, dnope_q, drope_q = carried(
        'quad3', _attn_bwd, d_att, o_att, lse, qf, q2, nope_g[0:1], rope_g[0:1], ctab, stab, kf, vf, nb, l, lc,
        "l2_attn_b")
    dkv2, dkrr, dnope_k = _k_post_bwd(dkl, dkc, dvl, dvc, kv2, nope_g[1:2], tm, "l2_k_post_b")
    dcqn = _mm_nt(dq2, w_uq_p, "l2_uq_bx")
    g_uq_p = _mm_tn(cqn, dq2, "l2_uq_bw")
    dckvn = _mm_nt(dkv2, w['ml_w_ukv'], "l2_ukv_bx")
    g['ml_w_ukv'] = _mm_tn(ckvn, dkv2, "l2_ukv_bw", shards=N_DEV)
    dzq, g['ml_q_norm'] = _q_pre_bwd(dcqn, zq, w['ml_q_norm'], tm, "l2_q_pre_b")
    dzkv, g['ml_kv_norm'], drope_k = _kv_pre_bwd(dckvn, dkrr, zkv, w['ml_kv_norm'], rope_g[1:2], ctab, stab, tm,
                                                  "l2_kv_pre_b")
    dh2 = _mm_nt_sum([(dzkv, w_kv), (dzq, w_q), (dzg, w_g)], "l2_in_bx")
    g['ml_w_in'] = jnp.concatenate([_mm_tn(h2, dzkv, "l2_in_kv_bw")[:, :kvc], _mm_tn(h2, dzq, "l2_in_q_bw", rows=tl),
                                    _mm_tn(h2, dzg, "l2_in_g_bw", rows=tl)], axis=1)
    g['ml_w_uq'] = g_uq_p.reshape(MLA_Q_RANK, MLA_HEADS, HEAD_W)[:, :, :MLA_NOPE + MLA_ROPE].reshape(
        MLA_Q_RANK, MLA_HEADS * (MLA_NOPE + MLA_ROPE))
    g['ml_nope_norm'] = jnp.concatenate([dnope_q, dnope_k], axis=0)
    g['ml_rope_norm'] = jnp.concatenate([drope_q, drope_k], axis=0)[:, :MLA_ROPE]
    dxa2, dng2, dsc2, dsh2 = _rms_mod_bwd(xa2, norm_g[2:3], sc2, sh2, dh2, dx3, segs_a, "l2_norm_b")
    if comm is not None:
        comm.grads_ready(2, g)

    do1, dgt1 = _resid_bwd(dxa2, o1, gt1, segs_a, "l1_resid_b")
    dy2_1 = carried('swap2', _mm_nt, do1, w['pl_w_out'], "l1_out_bx")
    g['pl_w_out'] = _mm_tn(y2_1, do1, "l1_out_bw")
    dpm, dgz1, g['pl_scale'], g['pl_w_grp'] = _pool2_bwd(dy2_1, pm1, w['pl_w_grp'], w['pl_scale'], z1, tm,
                                                          "l1_group_b")
    dv1 = _pool1(dpm, taps_l, ic_l, nb, l, 0, True, "l1_pool_lat_b", BF16)
    dv1 = _pool1(dpm, taps_c, ic_c, nb, lc, tl, True, "l1_pool_ctx_b", BF16, into=dv1)
    dz1 = jnp.concatenate([dv1, dgz1], axis=1)
    dh1 = _mm_nt(dz1, w['pl_w_in'], "l1_in_bx")
    g['pl_w_in'] = carried('quad2', _mm_tn, h1, dz1, "l1_in_bw", shards=N_DEV)
    dxa1, dng1, dsc1, dsh1 = _rms_mod_bwd(xa1, norm_g[1:2], sc1, sh1, dh1, dxa2, segs_a, "l1_norm_b")

    do0, dgt0 = _resid_bwd(dxa1, o0, gt0, segs_a, "l0_resid_b")
    g['cv_w_out'] = _mm_tn(y4_0, do0, "l0_out_bw")
    if comm is not None:
        comm.grads_ready(1, g)
    dy4 = carried('swap1', _mm_nt, do0, w['cv_w_out'], "l0_out_bx")
    dy2, dgz0, g['cv_ln_g'], g['cv_ln_b'] = _conv2_bwd(dy4, y2_0, z0, w['cv_ln_g'], w['cv_ln_b'], tm, "l0_gate_b")
    da_l, db_l, ddw, ddb = carried('quad1', _conv1_bwd, dy2, z0, w['cv_dw'], jnp.zeros((CONV_WIDTH, e), F32),
                                   jnp.zeros((1, e), F32), nb, l, 0, "l0_conv_lat_b")
    da, db_, g['cv_dw'], g['cv_db'] = _conv1_bwd(dy2, z0, w['cv_dw'], ddw, ddb, nb, lc, tl, "l0_conv_ctx_b",
                                                 into=(da_l, db_l))
    dz0 = jnp.concatenate([da, db_, dgz0], axis=1)
    g['cv_w_in'] = _mm_tn(h0, dz0, "l0_in_bw", shards=N_DEV)
    if comm is not None:
        comm.grads_ready(0, g)
    dh0 = carried('quad0', _mm_nt, dz0, w['cv_w_in'], "l0_in_bx")
    dx0, dng0, dsc0, dsh0 = _rms_mod_bwd(xa0, norm_g[0:1], sc0, sh0, dh0, dxa1, segs_a, "l0_norm_b", dx_rows=tl)

    def rows4(t):
        return jnp.pad(t[:, 0], ((0, 2 * nb - t.shape[0]), (0, 0)))

    dmods = jnp.stack([
        jnp.concatenate([rows4(dsh0), rows4(dsc0), rows4(dgt0)], axis=1),
        jnp.concatenate([rows4(dsh1), rows4(dsc1), rows4(dgt1)], axis=1),
        jnp.concatenate([rows4(dsh2), rows4(dsc2), rows4(dgt2)], axis=1),
        jnp.concatenate([rows4(dsh3), rows4(dsc3), rows4(dgt3)], axis=1)])
    dnorm_g = jnp.concatenate([dng0, dng1, dng2, dng3], axis=0)
    return loss_vec, dx0.reshape(nb, l, d), g, dmods, dnorm_g


def _mesh_pos():
    return lax.axis_index("x"), lax.axis_index("y"), lax.axis_index("c")


def _remote(src, dst, send_sems, recv_sems, k, dev):
    return pltpu.make_async_remote_copy(src_ref=src, dst_ref=dst, send_sem=send_sems.at[k], recv_sem=recv_sems.at[k],
                                        device_id=dev, device_id_type=pl.DeviceIdType.MESH)


def _comm_call(body, xs, out_shapes, n_remote, n_local, name):
    hbm = pl.BlockSpec(memory_space=pltpu.HBM)
    return pl.pallas_call(
        body, in_specs=[hbm] * len(xs), out_specs=[hbm] * len(out_shapes), out_shape=out_shapes,
        scratch_shapes=_sem_shapes(n_remote, n_local),
        compiler_params=pltpu.CompilerParams(has_side_effects=True), name=name)(*xs)


def _run_side(side, name):
    n = len(side.xs)

    def body(*refs):
        side.start(refs[:n], refs[n:n + len(side.out_shapes)], *refs[n + len(side.out_shapes):])
        side.finish(refs[:n], refs[n:n + len(side.out_shapes)], *refs[n + len(side.out_shapes):])

    return _comm_call(body, side.xs, side.out_shapes, side.n_remote, side.n_local, name)


def _gather_side(xs):
    n = len(xs)

    def plan(x_refs, o_refs, send_sems, recv_sems, local_sems):
        x, y, c = _mesh_pos()
        me, sib = (x, y, c), (x, y, 1 - c)
        chips = [(1 - x, y), (x, 1 - y), (1 - x, 1 - y)]

        def slot(a, p):
            return o_refs[a].at[4 * p[0] + 2 * p[1] + p[2]]

        def copy(a, k, block, to, src=None):
            return _remote(slot(a, block) if src is None else src, slot(a, block), send_sems, recv_sems, 7 * a + k, to)

        mine = [pltpu.make_async_copy(x_refs[a], slot(a, me), local_sems.at[a]) for a in range(n)]
        first = []
        for a in range(n):
            first += [copy(a, 1 + j, me, chip + (c,), src=x_refs[a]) for j, chip in enumerate(chips)]
            first.append(copy(a, 0, me, sib, src=x_refs[a]))
        return me, sib, c, chips, copy, mine, first

    def start(x_refs, o_refs, send_sems, recv_sems, local_sems):
        _, _, _, _, _, mine, first = plan(x_refs, o_refs, send_sems, recv_sems, local_sems)
        for cp in mine + first:
            cp.start()

    def finish(x_refs, o_refs, send_sems, recv_sems, local_sems):
        me, sib, c, chips, copy, mine, first = plan(x_refs, o_refs, send_sems, recv_sems, local_sems)
        passed = []
        for j, chip in enumerate(chips):
            for a in range(n):
                copy(a, 1 + j, chip + (c,), me).wait_recv()
                passed.append(copy(a, 4 + j, chip + (c,), sib))
                passed[-1].start()
        for a in range(n):
            copy(a, 0, sib, me).wait_recv()
        for j, chip in enumerate(chips):
            for a in range(n):
                copy(a, 4 + j, chip + (1 - c,), me).wait_recv()
        for cp in first + passed:
            cp.wait_send()
        for cp in mine:
            cp.wait()

    return _Side(xs, [_sds((N_DEV,) + x.shape, x.dtype) for x in xs], 7 * n, n, start, finish)


def _gather_all(xs, name):
    return _run_side(_gather_side(xs), name)


def _swap_side(xs):
    n = len(xs)

    def plan(x_refs, o_refs, send_sems, recv_sems, _):
        x, y, c = _mesh_pos()
        return [_remote(x_refs[a].at[q, 1 - c], o_refs[a].at[q], send_sems, recv_sems, 4 * a + q, (x, y, 1 - c))
                for a in range(n) for q in range(4)]

    def start(*refs):
        for cp in plan(*refs):
            cp.start()

    def finish(*refs):
        copies = plan(*refs)
        for cp in copies:
            cp.wait_recv()
        for cp in copies:
            cp.wait_send()

    return _Side(xs, [_sds((4,) + x.shape[2:], x.dtype) for x in xs], 4 * n, 0, start, finish)


def _quad_side(xs):
    n = len(xs)

    def plan(x_refs, o_refs, send_sems, recv_sems, local_sems):
        x, y, c = _mesh_pos()
        q = 2 * x + y
        chips = [(1 - x, y), (x, 1 - y), (1 - x, 1 - y)]
        mine = [pltpu.make_async_copy(x_refs[a].at[q], o_refs[a].at[q], local_sems.at[a]) for a in range(n)]
        sends, arrivals = [], []
        for a in range(n):
            for j, chip in enumerate(chips):
                qj = 2 * chip[0] + chip[1]
                sends.append(_remote(x_refs[a].at[qj], o_refs[a].at[q], send_sems, recv_sems, 3 * a + j, chip + (c,)))
                arrivals.append(_remote(x_refs[a].at[qj], o_refs[a].at[qj], send_sems, recv_sems, 3 * a + j,
                                        chip + (c,)))
        return mine, sends, arrivals

    def start(*refs):
        mine, sends, _ = plan(*refs)
        for cp in mine + sends:
            cp.start()

    def finish(*refs):
        mine, sends, arrivals = plan(*refs)
        for cp in arrivals:
            cp.wait_recv()
        for cp in sends:
            cp.wait_send()
        for cp in mine:
            cp.wait()

    return _Side(xs, [_sds(x.shape, x.dtype) for x in xs], 3 * n, n, start, finish)


def _pair_add(xs, rs, name):
    n = len(xs)

    def body(*refs):
        c = lax.axis_index("c")
        for x_ref, r_ref, o_ref in zip(refs[:n], refs[n:2 * n], refs[2 * n:]):
            o_ref[...] = (x_ref[c].astype(F32) + r_ref[...].astype(F32)).astype(o_ref.dtype)

    slot = lambda x: pl.BlockSpec((None,) + x.shape[2:], lambda q: (q, 0, 0))
    return pl.pallas_call(
        body, grid=(4,),
        in_specs=[pl.BlockSpec((None, 2) + x.shape[2:], lambda q: (q, 0, 0, 0)) for x in xs] + [slot(x) for x in xs],
        out_specs=[slot(x) for x in xs], out_shape=[_sds((4,) + x.shape[2:], x.dtype) for x in xs],
        compiler_params=_params("parallel"), name=name)(*xs, *rs)


def _pack_rows(n):
    r = -(-n // PACK_COLS)
    return -(-r // 256) * 256 if r > 256 else -(-r // 16) * 16


def _pack(arrs, dtype):
    flat = jnp.concatenate([a.reshape(-1).astype(dtype) for a in arrs])
    rows = _pack_rows(flat.shape[0])
    return jnp.pad(flat, (0, rows * PACK_COLS - flat.shape[0])).reshape(rows, PACK_COLS)


def _pack_shards(arrs):
    flat = jnp.concatenate([a.astype(F32) for a in arrs], axis=1)
    rows = _pack_rows(flat.shape[1])
    return jnp.pad(flat, ((0, 0), (0, rows * PACK_COLS - flat.shape[1]))).reshape(N_DEV, rows, PACK_COLS)


def _unpack(packed, shapes, lead=()):
    flat = packed.reshape(tuple(lead) + (-1,))
    out, off = [], 0
    for s in shapes:
        n = 1
        for v in s:
            n *= v
        out.append(flat[..., off:off + n].reshape(tuple(lead) + tuple(s)))
        off += n
    return out


def _to_shards(full, ax):
    s = full.shape
    t = full.reshape(s[:ax] + (N_DEV, s[ax] // N_DEV) + s[ax + 1:])
    return jnp.moveaxis(t, ax, 0).reshape(N_DEV, -1)


def _from_shards(shards, local_shape, ax):
    t = jnp.moveaxis(shards.reshape((N_DEV,) + tuple(local_shape)), 0, ax)
    s = t.shape
    return t.reshape(s[:ax] + (s[ax] * s[ax + 1],) + s[ax + 2:])


def _mod_fwd(c_rows, w_mod, b_mod, name):
    nl, d, n = w_mod.shape
    r = c_rows.shape[0]

    def body(c_ref, w_ref, b_ref, o_ref):
        s = _silu(c_ref[...])
        for l in range(nl):
            o_ref[l] = _dot(s, w_ref[l]) + b_ref[l]

    return pl.pallas_call(body, out_shape=_sds((nl, r, n)),
                          compiler_params=pltpu.CompilerParams(vmem_limit_bytes=VMEM_LIMIT), name=name)(
        c_rows, w_mod, b_mod)


def _mod_bwd(c_rows, dcols, dall, w_mod, c_ctx, name):
    nl, d, n = w_mod.shape
    r = c_rows.shape[0]

    def body(c_ref, dc_ref, da_ref, w_ref, cc_ref, gw_ref, gb_ref, gc_ref):
        s = _silu(c_ref[...])
        ds = jnp.zeros((r, d), F32)
        for l in range(nl):
            gw_ref[l] = _dot_tn(s, dc_ref[l])
            gb_ref[l] = _rowsum(da_ref[l])
            ds = ds + _dot_nt(dc_ref[l], w_ref[l])
        row = lax.broadcasted_iota(jnp.int32, (r, d), 0)
        gc_ref[...] = _rowsum(jnp.where(row % 4 >= 2, ds, 0.0)) * _dsilu(cc_ref[...])

    return pl.pallas_call(body, out_shape=[_sds((nl, d, n)), _sds((nl, 1, 3 * d)), _sds((1, d))],
                          compiler_params=pltpu.CompilerParams(vmem_limit_bytes=VMEM_LIMIT), name=name)(
        c_rows, dcols, dall, w_mod, c_ctx)


def _adam_math(w, gsum, m, v):
    c1, c2 = 1.0 - ADAM_B1 ** ADAM_STEP, 1.0 - ADAM_B2 ** ADAM_STEP
    mn = ADAM_B1 * m + (1.0 - ADAM_B1) * gsum
    vn = ADAM_B2 * v + (1.0 - ADAM_B2) * (gsum * gsum)
    return -ADAM_LR * ((mn / c1) / (jnp.sqrt(vn / c2) + ADAM_EPS) + ADAM_WD * w), mn, vn


def _adam(w, gparts, row0, m, v, name):
    rows, cols = w.shape
    npart = gparts.shape[0]
    if rows % 8:
        tr = rows
        assert row0 == 0 and gparts.shape[1] == rows
    else:
        tr = max(t for t in (512, 256, 128, 64, 32, 16, 8) if rows % t == 0 and row0 % t == 0
                 and (t * cols <= 256 * 1024 or t == 8))

    def body(w_ref, g_ref, m_ref, v_ref, go_ref, d_ref, mo_ref, vo_ref):
        gsum = g_ref[0].astype(F32)
        for p in range(1, npart):
            gsum = gsum + g_ref[p].astype(F32)
        go_ref[...] = gsum
        d_ref[...], mo_ref[...], vo_ref[...] = _adam_math(w_ref[...], gsum, m_ref[...], v_ref[...])

    spec = _row(tr, cols)
    return pl.pallas_call(
        body, grid=(rows // tr,),
        in_specs=[spec, pl.BlockSpec((npart, tr, cols), lambda i: (0, row0 // tr + i, 0)), spec, spec],
        out_specs=[spec] * 4, out_shape=[_sds((rows, cols))] * 4, compiler_params=_params("parallel"),
        name=name)(w, gparts, m, v)


INPUTS = ['x', 'c', 'ctx'] + WEIGHTS + ['loss_target'] + ['m_' + n for n in WEIGHTS] + ['v_' + n for n in WEIGHTS]
AXES = ("x", "y", "c")
LAYER_MATS = (('cv_w_in', 'cv_w_out'), ('pl_w_in', 'pl_w_grp', 'pl_w_out'),
              ('ml_w_in', 'ml_w_uq', 'ml_w_ukv', 'ml_w_out'), ('ch_w_in', 'ch_w_out'))
GATHERS = {'fwd1': LAYER_MATS[1], 'fwd2': ('ml_w_in', 'ml_w_uq'), 'fwd3': ('ml_w_ukv', 'ml_w_out'), 'fwd4': LAYER_MATS[3]}
GRAD_GROUPS = (('cv_w_in',), ('pl_w_in', 'pl_w_grp', 'pl_w_out', 'cv_w_out'), LAYER_MATS[2], LAYER_MATS[3])
KINDS = ('grad_', 'delta_', 'new_m_', 'new_v_')


def _squeeze_layer(name, a):
    return a if name == 'norm_g' or a.ndim < 3 else a[0]


def _as2d(a):
    return a.reshape(-1, a.shape[-1])


class _Exchanges:
    def __init__(self, a, w):
        self.a, self.w, self.bufs, self.sums, self.quad = a, w, {}, {}, {}

    def mats(self, names):
        return [_as2d(self.a[n]).astype(BF16) for n in names]

    def take_weights(self, names, bufs):
        for n, buf in zip(names, bufs):
            self.w[n] = _squeeze_layer(n, _from_shards(buf, self.a[n].shape, SHARD_AXIS[n]))

    def side(self, tag):
        if tag in GATHERS:
            return _gather_side(self.mats(GATHERS[tag]))
        group = int(tag[-1])
        return _swap_side(self.bufs[group]) if tag.startswith('swap') else _quad_side(self.sums[group])

    def done(self, tag, brought):
        if tag in GATHERS:
            self.take_weights(GATHERS[tag], brought)
        elif tag.startswith('swap'):
            group = int(tag[-1])
            self.sums[group] = _pair_add(self.bufs[group], brought, "grads_add_cores_%d" % group)
        else:
            self.quad[int(tag[-1])] = brought

    def shard_major(self, n, gn):
        if gn.ndim == 3 and gn.shape[0] == N_DEV and gn.dtype == BF16:
            return gn
        whole = tuple(N_DEV * s if i == SHARD_AXIS[n] else s for i, s in enumerate(self.a[n].shape))
        return _to_shards(gn.reshape(whole), SHARD_AXIS[n]).reshape((N_DEV,) + _as2d(self.a[n]).shape)

    def grads_ready(self, group, g):
        bufs = [self.shard_major(n, g[n]).astype(BF16) for n in GRAD_GROUPS[group]]
        if group == 0:
            bufs.append(_pack_shards([self.shard_major(n, g[n]).reshape(N_DEV, -1) for n in VECTOR_WEIGHTS]))
        self.bufs[group] = [b.reshape((4, 2) + b.shape[1:]) for b in bufs]
        if group == 0:
            self.done('swap0', _run_side(self.side('swap0'), "grads_swap_cores_0"))


def _train_step(a):
    x, c, ctx, tgt = a['x'], a['c'], a['ctx'], a['loss_target']
    d = x.shape[-1]
    nb = x.shape[0]
    dev = 4 * lax.axis_index("x") + 2 * lax.axis_index("y") + lax.axis_index("c")
    local_shape = {n: a[n].shape for n in WEIGHTS}

    w = {n: _squeeze_layer(n, a[n]) for n in WEIGHTS if SHARD_AXIS[n] is None}
    comm = _Exchanges(a, w)
    vec_names = ['c'] + VECTOR_WEIGHTS
    vec_all, = _gather_all([_pack([a[n] for n in vec_names], F32)], "gather_vectors")
    parts = dict(zip(vec_names, _unpack(vec_all, [a[n].shape for n in vec_names], lead=(N_DEV,))))
    for n in VECTOR_WEIGHTS:
        w[n] = _squeeze_layer(n, _from_shards(parts[n], local_shape[n], SHARD_AXIS[n]))
    c_all = parts['c'].reshape(N_DEV * nb, d)
    c_ctx = a['c_ctx'].reshape(1, d)

    w_mod = a['w_mod']
    nl, ncol = w_mod.shape[0], w_mod.shape[2]
    mod_rows = -(-(N_DEV * nb + 1) // 8) * 8
    c_rows = jnp.concatenate([c_all, c_ctx, jnp.zeros((mod_rows - N_DEV * nb - 1, d), F32)], axis=0)
    b_loc = lax.dynamic_slice(a['b_mod'], (0, dev * ncol), (nl, ncol))[:, None, :]
    mod_loc = _mod_fwd(c_rows, w_mod, b_loc, "mod_fwd")
    gathered = _gather_all([mod_loc.reshape(nl * mod_rows, ncol)] + comm.mats(LAYER_MATS[0]), "gather_first")
    comm.take_weights(LAYER_MATS[0], gathered[1:])
    mod_all = gathered[0].reshape(N_DEV, nl, mod_rows, ncol).transpose(1, 2, 0, 3).reshape(nl, mod_rows, N_DEV * ncol)
    ctx_row = mod_all[:, N_DEV * nb:N_DEV * nb + 1]
    mods = jnp.concatenate([lax.dynamic_slice(mod_all, (0, dev * nb, 0), (nl, nb, 3 * d))] + [ctx_row] * nb, axis=1)

    loss_vec, grad_x, g, dmods, dnorm_g = _local_step(x, ctx, tgt, w, mods, comm)
    loss = lax.psum(jnp.sum(loss_vec), AXES)

    nseg = dmods.shape[1]
    dm_all, = _gather_all([dmods.reshape(nl * nseg, 3 * d)], "gather_dmods")
    dm_all = dm_all.reshape(N_DEV, nl, nseg, 3 * d).transpose(1, 0, 2, 3).reshape(nl, N_DEV * nseg, 3 * d)
    dcols = lax.dynamic_slice(dm_all, (0, 0, dev * ncol), (nl, N_DEV * nseg, ncol))
    c_rows_b = jnp.concatenate([c_all.reshape(N_DEV, nb, d), jnp.broadcast_to(c_ctx, (N_DEV, nb, d))], axis=1)
    g_w_mod, g_b_mod, g_c_ctx = _mod_bwd(c_rows_b.reshape(N_DEV * nseg, d), dcols, dm_all, w_mod, c_ctx, "mod_bwd")
    g['c_ctx'], g['norm_g'] = g_c_ctx, dnorm_g
    rep_all, = _gather_all([_pack([g[n] for n in REPLICATED], F32)], "gather_replicated_grads")

    out = {}

    def keep(names, res, shapes=None):
        for kind, val in zip(KINDS, res):
            if shapes is None:
                out[kind + names[0]] = val.reshape(local_shape[names[0]])
            else:
                for n, leaf in zip(names, _unpack(val, shapes)):
                    out[kind + n] = leaf

    def update_packed(names, gparts, tag):
        res = _adam(_pack([a[n] for n in names], F32), gparts, 0, _pack([a['m_' + n] for n in names], F32),
                    _pack([a['v_' + n] for n in names], F32), "adam_" + tag)
        keep(names, res, [local_shape[n] for n in names])

    for group, names in enumerate(GRAD_GROUPS):
        for n, gparts in zip(names, comm.quad[group]):
            keep([n], _adam(_as2d(a[n]), gparts, 0, _as2d(a['m_' + n]), _as2d(a['v_' + n]), "adam_" + n))
    update_packed(VECTOR_WEIGHTS, comm.quad[0][-1], "vectors")
    update_packed(REPLICATED, rep_all, "replicated")
    keep(['w_mod'], _adam(_as2d(w_mod), _as2d(g_w_mod)[None], 0, _as2d(a['m_w_mod']), _as2d(a['v_w_mod']),
                          "adam_w_mod"))
    keep(['b_mod'], _adam(a['b_mod'], g_b_mod.reshape((1,) + a['b_mod'].shape), 0, a['m_b_mod'], a['v_b_mod'],
                          "adam_b_mod"))
    return (loss, grad_x) + tuple(out[kind + n] for kind in KINDS for n in WEIGHTS)


def kernel(x, c, ctx, c_ctx, norm_g, w_mod, b_mod, cv_w_in, cv_dw, cv_db, cv_ln_g, cv_ln_b, cv_w_out, pl_w_in, pl_w_grp, pl_scale, pl_w_out, ml_w_in, ml_q_norm, ml_kv_norm, ml_w_uq, ml_w_ukv, ml_nope_norm, ml_rope_norm, ml_w_out, ch_w_in, ch_ln_g, ch_ln_b, ch_w_s, ch_b_s, ch_w_out, loss_target, m_c_ctx, m_norm_g, m_w_mod, m_b_mod, m_cv_w_in, m_cv_dw, m_cv_db, m_cv_ln_g, m_cv_ln_b, m_cv_w_out, m_pl_w_in, m_pl_w_grp, m_pl_scale, m_pl_w_out, m_ml_w_in, m_ml_q_norm, m_ml_kv_norm, m_ml_w_uq, m_ml_w_ukv, m_ml_nope_norm, m_ml_rope_norm, m_ml_w_out, m_ch_w_in, m_ch_ln_g, m_ch_ln_b, m_ch_w_s, m_ch_b_s, m_ch_w_out, v_c_ctx, v_norm_g, v_w_mod, v_b_mod, v_cv_w_in, v_cv_dw, v_cv_db, v_cv_ln_g, v_cv_ln_b, v_cv_w_out, v_pl_w_in, v_pl_w_grp, v_pl_scale, v_pl_w_out, v_ml_w_in, v_ml_q_norm, v_ml_kv_norm, v_ml_w_uq, v_ml_w_ukv, v_ml_nope_norm, v_ml_rope_norm, v_ml_w_out, v_ch_w_in, v_ch_ln_g, v_ch_ln_b, v_ch_w_s, v_ch_b_s, v_ch_w_out):
    return _train_step(dict(zip(INPUTS, (x, c, ctx, c_ctx, norm_g, w_mod, b_mod, cv_w_in, cv_dw, cv_db, cv_ln_g, cv_ln_b, cv_w_out, pl_w_in, pl_w_grp, pl_scale, pl_w_out, ml_w_in, ml_q_norm, ml_kv_norm, ml_w_uq, ml_w_ukv, ml_nope_norm, ml_rope_norm, ml_w_out, ch_w_in, ch_ln_g, ch_ln_b, ch_w_s, ch_b_s, ch_w_out, loss_target, m_c_ctx, m_norm_g, m_w_mod, m_b_mod, m_cv_w_in, m_cv_dw, m_cv_db, m_cv_ln_g, m_cv_ln_b, m_cv_w_out, m_pl_w_in, m_pl_w_grp, m_pl_scale, m_pl_w_out, m_ml_w_in, m_ml_q_norm, m_ml_kv_norm, m_ml_w_uq, m_ml_w_ukv, m_ml_nope_norm, m_ml_rope_norm, m_ml_w_out, m_ch_w_in, m_ch_ln_g, m_ch_ln_b, m_ch_w_s, m_ch_b_s, m_ch_w_out, v_c_ctx, v_norm_g, v_w_mod, v_b_mod, v_cv_w_in, v_cv_dw, v_cv_db, v_cv_ln_g, v_cv_ln_b, v_cv_w_out, v_pl_w_in, v_pl_w_grp, v_pl_scale, v_pl_w_out, v_ml_w_in, v_ml_q_norm, v_ml_kv_norm, v_ml_w_uq, v_ml_w_ukv, v_ml_nope_norm, v_ml_rope_norm, v_ml_w_out, v_ch_w_in, v_ch_ln_g, v_ch_ln_b, v_ch_w_s, v_ch_b_s, v_ch_w_out))))
```
